```python
import jax, jax.numpy as jnp
from jax import lax
import numpy as np

D_MODEL = 1024
BATCH = 8
SEQ = 2048
DEPTH = 2

MEM_LEN = 256
POOL_GROUPS = 4
POOL_GROUP_DIM = D_MODEL // 16
POOL_WIDTH = POOL_GROUPS * POOL_GROUP_DIM
POOL_WINDOWS = (2, 4, 8, 16)
FOX_HEADS = 8
FOX_HEAD_DIM = 64
FOX_WIDTH = FOX_HEADS * FOX_HEAD_DIM
Q_BLOCK = 128
SGU_GROUPS = 4
SGU_GROUP_DIM = D_MODEL // 16
SGU_WIDTH = SGU_GROUPS * SGU_GROUP_DIM
SGU_CHUNK = 128
N_BRANCH = 3
OFF_A = 0
OFF_Q = OFF_A + POOL_WIDTH
OFF_K = OFF_Q + FOX_WIDTH
OFF_V = OFF_K + FOX_WIDTH
OFF_F = OFF_V + FOX_WIDTH
OFF_C = OFF_F + FOX_HEADS
OFF_G = OFF_C + 2 * SGU_WIDTH
N_IN = OFF_G + N_BRANCH * D_MODEL
XATTN_HEADS = 4
XATTN_HEAD_DIM = D_MODEL // XATTN_HEADS
D_FF = 4 * D_MODEL
EPS = 1e-6
NEG = -1e30

kernel_name = "hybrid_pool_fox_sgu_gated_block"


def rmsnorm(x, g):
    xf = x.astype(jnp.float32)
    y = xf * lax.rsqrt(jnp.mean(xf * xf, axis=-1, keepdims=True) + EPS)
    return (y * g.astype(jnp.float32)).astype(x.dtype)


def pool_mixer(a, w, scale):
    B, S, _ = a.shape
    af = a.astype(jnp.float32)
    c = jnp.pad(jnp.cumsum(af, axis=1), ((0, 0), (1, 0), (0, 0)))
    t = jnp.arange(S)
    outs = []
    for gi, win in enumerate(POOL_WINDOWS):
        sl = slice(gi * POOL_GROUP_DIM, (gi + 1) * POOL_GROUP_DIM)
        cg = c[..., sl]
        lo = jnp.take(cg, jnp.maximum(t + 1 - win, 0), axis=1)
        cnt = jnp.minimum(t + 1, win).astype(jnp.float32)[None, :, None]
        outs.append((cg[:, 1:] - lo) / cnt - af[..., sl])
    d = jnp.stack(outs, axis=2).astype(a.dtype)
    y = jnp.einsum('bsgc,gcd->bsgd', d, w).reshape(B, S, POOL_WIDTH)
    return y * scale


def forgetting_attention(q, k, v, logf):
    S = q.shape[1]
    F = jnp.cumsum(logf, axis=1).transpose(0, 2, 1)
    scale = FOX_HEAD_DIM ** -0.5
    outs = []
    for i in range(S // Q_BLOCK):
        q0 = i * Q_BLOCK
        kend = q0 + Q_BLOCK
        s = jnp.einsum('bqhd,bkhd->bhqk', q[:, q0:kend], k[:, :kend]).astype(jnp.float32) * scale
        s = s + F[:, :, q0:kend, None] - F[:, :, None, :kend]
        mask = (q0 + jnp.arange(Q_BLOCK))[:, None] >= jnp.arange(kend)[None, :]
        s = jnp.where(mask, s, NEG)
        p = jax.nn.softmax(s, axis=-1).astype(v.dtype)
        outs.append(jnp.einsum('bhqk,bkhd->bqhd', p, v[:, :kend]))
    return jnp.concatenate(outs, axis=1)


def spatial_gating(z, norm_g, ws, b):
    B, S, _ = z.shape
    u, v = z[..., :SGU_WIDTH], z[..., SGU_WIDTH:]
    v = rmsnorm(v, norm_g)
    vc = v.reshape(B, S // SGU_CHUNK, SGU_CHUNK, SGU_GROUPS, SGU_GROUP_DIM)
    causal = jnp.tril(jnp.ones((SGU_CHUNK, SGU_CHUNK), dtype=ws.dtype))
    w = ws * causal[None]
    mixed = jnp.einsum('gts,bcsgd->bctgd', w, vc) + b.T[None, None, :, :, None]
    return u * mixed.reshape(B, S, SGU_WIDTH)


def _fwd_setup_inputs(seed: int = 0) -> dict:
    key = jax.random.key(seed)
    ks = jax.random.split(key, 24)
    L, D = DEPTH, D_MODEL
    nrm = lambda k, shape, fan_in: jax.random.normal(k, shape, jnp.float32) * (fan_in ** -0.5)
    gain = lambda k, shape: 1.0 + 0.05 * jax.random.normal(k, shape, jnp.float32)
    b_forget = jnp.linspace(1.0, 6.0, FOX_HEADS, dtype=jnp.float32)[None, :] + 0.1 * jax.random.normal(ks[3], (L, FOX_HEADS), jnp.float32)
    return {
        "x": jax.random.normal(ks[0], (BATCH, SEQ, D), jnp.float32),
        "mem": jax.random.normal(ks[1], (BATCH, MEM_LEN, D), jnp.float32),
        "norm_mix_g": gain(ks[2], (L, D)),
        "w_in": nrm(ks[4], (L, D, N_IN), D),
        "b_forget": b_forget,
        "pool_w": nrm(ks[5], (L, POOL_GROUPS, POOL_GROUP_DIM, POOL_GROUP_DIM), POOL_GROUP_DIM),
        "pool_scale": gain(ks[6], (L, POOL_WIDTH)),
        "sgu_norm_g": gain(ks[7], (L, SGU_WIDTH)),
        "sgu_w": nrm(ks[8], (L, SGU_GROUPS, SGU_CHUNK, SGU_CHUNK), SGU_CHUNK),
        "sgu_b": gain(ks[9], (L, SGU_GROUPS, SGU_CHUNK)),
        "w_branch_a": nrm(ks[10], (L, POOL_WIDTH, D), POOL_WIDTH),
        "w_branch_b": nrm(ks[11], (L, FOX_WIDTH, D), FOX_WIDTH),
        "w_branch_c": nrm(ks[12], (L, SGU_WIDTH, D), SGU_WIDTH),
        "b_gate": 0.01 * jax.random.normal(ks[13], (L, N_BRANCH * D), jnp.float32),
        "w_out": nrm(ks[14], (L, D, D), D),
        "norm_xattn_g": gain(ks[15], (L, D)),
        "norm_mem_g": gain(ks[16], (L, D)),
        "w_xq": nrm(ks[17], (L, D, D), D),
        "w_xkv": nrm(ks[18], (L, D, 2 * D), D),
        "w_xo": nrm(ks[19], (L, D, D), D),
        "norm_ffn_g": gain(ks[20], (L, D)),
        "w_ff1": nrm(ks[21], (L, D, D_FF), D),
        "w_ff2": nrm(ks[22], (L, D_FF, D), D_FF),
        "final_norm_g": gain(ks[23], (D,)),
    }


def _fwd_reference(x, mem, norm_mix_g, w_in, b_forget, pool_w, pool_scale, sgu_norm_g, sgu_w, sgu_b,
              w_branch_a, w_branch_b, w_branch_c, b_gate, w_out, norm_xattn_g, norm_mem_g,
              w_xq, w_xkv, w_xo, norm_ffn_g, w_ff1, w_ff2, final_norm_g):
    B, S, D = x.shape
    M = mem.shape[1]
    for l in range(DEPTH):
        h = rmsnorm(x, norm_mix_g[l])
        proj = h @ w_in[l]
        a = proj[..., OFF_A:OFF_Q]
        q = proj[..., OFF_Q:OFF_K].reshape(B, S, FOX_HEADS, FOX_HEAD_DIM)
        k = proj[..., OFF_K:OFF_V].reshape(B, S, FOX_HEADS, FOX_HEAD_DIM)
        v = proj[..., OFF_V:OFF_F].reshape(B, S, FOX_HEADS, FOX_HEAD_DIM)
        logf = jax.nn.log_sigmoid(proj[..., OFF_F:OFF_C].astype(jnp.float32) + b_forget[l].astype(jnp.float32))
        zc = jax.nn.gelu(proj[..., OFF_C:OFF_G])
        gates = jax.nn.sigmoid(proj[..., OFF_G:] + b_gate[l])

        y_a = pool_mixer(a, pool_w[l], pool_scale[l]) @ w_branch_a[l]
        y_b = forgetting_attention(q, k, v, logf).reshape(B, S, FOX_WIDTH) @ w_branch_b[l]
        y_c = spatial_gating(zc, sgu_norm_g[l], sgu_w[l], sgu_b[l]) @ w_branch_c[l]
        merged = gates[..., :D] * y_a + gates[..., D:2 * D] * y_b + gates[..., 2 * D:] * y_c
        x = x + merged @ w_out[l]

        hx = rmsnorm(x, norm_xattn_g[l])
        hm = rmsnorm(mem, norm_mem_g[l])
        xq = (hx @ w_xq[l]).reshape(B, S, XATTN_HEADS, XATTN_HEAD_DIM)
        kv = hm @ w_xkv[l]
        xk = kv[..., :D].reshape(B, M, XATTN_HEADS, XATTN_HEAD_DIM)
        xv = kv[..., D:].reshape(B, M, XATTN_HEADS, XATTN_HEAD_DIM)
        s = jnp.einsum('bqhd,bkhd->bhqk', xq, xk).astype(jnp.float32) * (XATTN_HEAD_DIM ** -0.5)
        p = jax.nn.softmax(s, axis=-1).astype(xv.dtype)
        o = jnp.einsum('bhqk,bkhd->bqhd', p, xv).reshape(B, S, D)
        x = x + o @ w_xo[l]

        hf = rmsnorm(x, norm_ffn_g[l])
        x = x + jnp.square(jax.nn.relu(hf @ w_ff1[l])) @ w_ff2[l]
    return rmsnorm(x, final_norm_g)


import jax as _jax
import jax.numpy as _jnp

TWIN_FORMAT = 'train_step'
FWD_PARAMS = ['x', 'mem', 'norm_mix_g', 'w_in', 'b_forget', 'pool_w', 'pool_scale', 'sgu_norm_g', 'sgu_w', 'sgu_b', 'w_branch_a', 'w_branch_b', 'w_branch_c', 'b_gate', 'w_out', 'norm_xattn_g', 'norm_mem_g', 'w_xq', 'w_xkv', 'w_xo', 'norm_ffn_g', 'w_ff1', 'w_ff2', 'final_norm_g']
TWIN_WEIGHTS = ['norm_mix_g', 'w_in', 'b_forget', 'pool_w', 'pool_scale', 'sgu_norm_g', 'sgu_w', 'sgu_b', 'w_branch_a', 'w_branch_b', 'w_branch_c', 'b_gate', 'w_out', 'norm_xattn_g', 'norm_mem_g', 'w_xq', 'w_xkv', 'w_xo', 'norm_ffn_g', 'w_ff1', 'w_ff2', 'final_norm_g']
TWIN_DIFF_INPUT = 'x'
TWIN_INPUTS = ['x', 'mem', 'norm_mix_g', 'w_in', 'b_forget', 'pool_w', 'pool_scale', 'sgu_norm_g', 'sgu_w', 'sgu_b', 'w_branch_a', 'w_branch_b', 'w_branch_c', 'b_gate', 'w_out', 'norm_xattn_g', 'norm_mem_g', 'w_xq', 'w_xkv', 'w_xo', 'norm_ffn_g', 'w_ff1', 'w_ff2', 'final_norm_g', 'loss_target', 'm_norm_mix_g', 'm_w_in', 'm_b_forget', 'm_pool_w', 'm_pool_scale', 'm_sgu_norm_g', 'm_sgu_w', 'm_sgu_b', 'm_w_branch_a', 'm_w_branch_b', 'm_w_branch_c', 'm_b_gate', 'm_w_out', 'm_norm_xattn_g', 'm_norm_mem_g', 'm_w_xq', 'm_w_xkv', 'm_w_xo', 'm_norm_ffn_g', 'm_w_ff1', 'm_w_ff2', 'm_final_norm_g', 'v_norm_mix_g', 'v_w_in', 'v_b_forget', 'v_pool_w', 'v_pool_scale', 'v_sgu_norm_g', 'v_sgu_w', 'v_sgu_b', 'v_w_branch_a', 'v_w_branch_b', 'v_w_branch_c', 'v_b_gate', 'v_w_out', 'v_norm_xattn_g', 'v_norm_mem_g', 'v_w_xq', 'v_w_xkv', 'v_w_xo', 'v_norm_ffn_g', 'v_w_ff1', 'v_w_ff2', 'v_final_norm_g']
TWIN_OUTPUTS = ['loss', 'grad_x', 'grad_norm_mix_g', 'grad_w_in', 'grad_b_forget', 'grad_pool_w', 'grad_pool_scale', 'grad_sgu_norm_g', 'grad_sgu_w', 'grad_sgu_b', 'grad_w_branch_a', 'grad_w_branch_b', 'grad_w_branch_c', 'grad_b_gate', 'grad_w_out', 'grad_norm_xattn_g', 'grad_norm_mem_g', 'grad_w_xq', 'grad_w_xkv', 'grad_w_xo', 'grad_norm_ffn_g', 'grad_w_ff1', 'grad_w_ff2', 'grad_final_norm_g', 'delta_norm_mix_g', 'delta_w_in', 'delta_b_forget', 'delta_pool_w', 'delta_pool_scale', 'delta_sgu_norm_g', 'delta_sgu_w', 'delta_sgu_b', 'delta_w_branch_a', 'delta_w_branch_b', 'delta_w_branch_c', 'delta_b_gate', 'delta_w_out', 'delta_norm_xattn_g', 'delta_norm_mem_g', 'delta_w_xq', 'delta_w_xkv', 'delta_w_xo', 'delta_norm_ffn_g', 'delta_w_ff1', 'delta_w_ff2', 'delta_final_norm_g', 'new_m_norm_mix_g', 'new_m_w_in', 'new_m_b_forget', 'new_m_pool_w', 'new_m_pool_scale', 'new_m_sgu_norm_g', 'new_m_sgu_w', 'new_m_sgu_b', 'new_m_w_branch_a', 'new_m_w_branch_b', 'new_m_w_branch_c', 'new_m_b_gate', 'new_m_w_out', 'new_m_norm_xattn_g', 'new_m_norm_mem_g', 'new_m_w_xq', 'new_m_w_xkv', 'new_m_w_xo', 'new_m_norm_ffn_g', 'new_m_w_ff1', 'new_m_w_ff2', 'new_m_final_norm_g', 'new_v_norm_mix_g', 'new_v_w_in', 'new_v_b_forget', 'new_v_pool_w', 'new_v_pool_scale', 'new_v_sgu_norm_g', 'new_v_sgu_w', 'new_v_sgu_b', 'new_v_w_branch_a', 'new_v_w_branch_b', 'new_v_w_branch_c', 'new_v_b_gate', 'new_v_w_out', 'new_v_norm_xattn_g', 'new_v_norm_mem_g', 'new_v_w_xq', 'new_v_w_xkv', 'new_v_w_xo', 'new_v_norm_ffn_g', 'new_v_w_ff1', 'new_v_w_ff2', 'new_v_final_norm_g']
TWIN_LEAF_KINDS = {'loss': 'loss', 'grad_x': 'grad_x', 'grad_norm_mix_g': 'grad_w', 'grad_w_in': 'grad_w', 'grad_b_forget': 'grad_w', 'grad_pool_w': 'grad_w', 'grad_pool_scale': 'grad_w', 'grad_sgu_norm_g': 'grad_w', 'grad_sgu_w': 'grad_w', 'grad_sgu_b': 'grad_w', 'grad_w_branch_a': 'grad_w', 'grad_w_branch_b': 'grad_w', 'grad_w_branch_c': 'grad_w', 'grad_b_gate': 'grad_w', 'grad_w_out': 'grad_w', 'grad_norm_xattn_g': 'grad_w', 'grad_norm_mem_g': 'grad_w', 'grad_w_xq': 'grad_w', 'grad_w_xkv': 'grad_w', 'grad_w_xo': 'grad_w', 'grad_norm_ffn_g': 'grad_w', 'grad_w_ff1': 'grad_w', 'grad_w_ff2': 'grad_w', 'grad_final_norm_g': 'grad_w', 'delta_norm_mix_g': 'delta_w', 'delta_w_in': 'delta_w', 'delta_b_forget': 'delta_w', 'delta_pool_w': 'delta_w', 'delta_pool_scale': 'delta_w', 'delta_sgu_norm_g': 'delta_w', 'delta_sgu_w': 'delta_w', 'delta_sgu_b': 'delta_w', 'delta_w_branch_a': 'delta_w', 'delta_w_branch_b': 'delta_w', 'delta_w_branch_c': 'delta_w', 'delta_b_gate': 'delta_w', 'delta_w_out': 'delta_w', 'delta_norm_xattn_g': 'delta_w', 'delta_norm_mem_g': 'delta_w', 'delta_w_xq': 'delta_w', 'delta_w_xkv': 'delta_w', 'delta_w_xo': 'delta_w', 'delta_norm_ffn_g': 'delta_w', 'delta_w_ff1': 'delta_w', 'delta_w_ff2': 'delta_w', 'delta_final_norm_g': 'delta_w', 'new_m_norm_mix_g': 'new_m', 'new_m_w_in': 'new_m', 'new_m_b_forget': 'new_m', 'new_m_pool_w': 'new_m', 'new_m_pool_scale': 'new_m', 'new_m_sgu_norm_g': 'new_m', 'new_m_sgu_w': 'new_m', 'new_m_sgu_b': 'new_m', 'new_m_w_branch_a': 'new_m', 'new_m_w_branch_b': 'new_m', 'new_m_w_branch_c': 'new_m', 'new_m_b_gate': 'new_m', 'new_m_w_out': 'new_m', 'new_m_norm_xattn_g': 'new_m', 'new_m_norm_mem_g': 'new_m', 'new_m_w_xq': 'new_m', 'new_m_w_xkv': 'new_m', 'new_m_w_xo': 'new_m', 'new_m_norm_ffn_g': 'new_m', 'new_m_w_ff1': 'new_m', 'new_m_w_ff2': 'new_m', 'new_m_final_norm_g': 'new_m', 'new_v_norm_mix_g': 'new_v', 'new_v_w_in': 'new_v', 'new_v_b_forget': 'new_v', 'new_v_pool_w': 'new_v', 'new_v_pool_scale': 'new_v', 'new_v_sgu_norm_g': 'new_v', 'new_v_sgu_w': 'new_v', 'new_v_sgu_b': 'new_v', 'new_v_w_branch_a': 'new_v', 'new_v_w_branch_b': 'new_v', 'new_v_w_branch_c': 'new_v', 'new_v_b_gate': 'new_v', 'new_v_w_out': 'new_v', 'new_v_norm_xattn_g': 'new_v', 'new_v_norm_mem_g': 'new_v', 'new_v_w_xq': 'new_v', 'new_v_w_xkv': 'new_v', 'new_v_w_xo': 'new_v', 'new_v_norm_ffn_g': 'new_v', 'new_v_w_ff1': 'new_v', 'new_v_w_ff2': 'new_v', 'new_v_final_norm_g': 'new_v'}


def _forward(args):
    return _fwd_reference(*[args[k] for k in FWD_PARAMS])


def _output_shape():
    out = _jax.eval_shape(lambda: _forward(_fwd_setup_inputs(0)))
    return out.shape, out.dtype

N_MICROBATCH = 1
ADAM_LR = 0.001
ADAM_B1 = 0.9
ADAM_B2 = 0.999
ADAM_EPS = 1e-08
ADAM_WD = 0.01
ADAM_STEP = 10
PER_EXAMPLE_BATCH_AXIS = {'x': 0, 'mem': 0, 'loss_target': 0}
SHARED_INPUTS = []
_WEIGHT_DTYPES = {'norm_mix_g': _jnp.float32, 'w_in': _jnp.float32, 'b_forget': _jnp.float32, 'pool_w': _jnp.float32, 'pool_scale': _jnp.float32, 'sgu_norm_g': _jnp.float32, 'sgu_w': _jnp.float32, 'sgu_b': _jnp.float32, 'w_branch_a': _jnp.float32, 'w_branch_b': _jnp.float32, 'w_branch_c': _jnp.float32, 'b_gate': _jnp.float32, 'w_out': _jnp.float32, 'norm_xattn_g': _jnp.float32, 'norm_mem_g': _jnp.float32, 'w_xq': _jnp.float32, 'w_xkv': _jnp.float32, 'w_xo': _jnp.float32, 'norm_ffn_g': _jnp.float32, 'w_ff1': _jnp.float32, 'w_ff2': _jnp.float32, 'final_norm_g': _jnp.float32}
MOMENT_SCALE = {'norm_mix_g': 8.374147e-02, 'w_in': 3.569164e-02, 'b_forget': 8.865622e-02, 'pool_w': 9.429928e-02, 'pool_scale': 9.970543e-02, 'sgu_norm_g': 5.133015e-02, 'sgu_w': 3.498872e-02, 'sgu_b': 5.059290e-02, 'w_branch_a': 4.805185e-02, 'w_branch_b': 2.718456e-02, 'w_branch_c': 5.403841e-02, 'b_gate': 1.733680e-02, 'w_out': 7.797234e-02, 'norm_xattn_g': 1.032084e-02, 'norm_mem_g': 1.809263e-02, 'w_xq': 1.067545e-02, 'w_xkv': 1.175844e-02, 'w_xo': 1.281520e-02, 'norm_ffn_g': 1.013795e-01, 'w_ff1': 5.258082e-02, 'w_ff2': 1.380494e-01, 'final_norm_g': 1.626690e+01}


def _to_microbatches(a, axis):
    t = _jnp.moveaxis(a, axis, 0)
    t = t.reshape((N_MICROBATCH, t.shape[0] // N_MICROBATCH) + t.shape[1:])
    return _jnp.moveaxis(t, 1, axis + 1)


def setup_inputs(seed: int = 0) -> dict:
    inp = _fwd_setup_inputs(seed)
    key = _jax.random.fold_in(_jax.random.key(seed), 7919)
    shape, _ = _output_shape()
    out = dict(inp)
    out["loss_target"] = _jax.random.normal(_jax.random.fold_in(key, 0), shape, _jnp.float32)
    for i, name in enumerate(TWIN_WEIGHTS):
        w = inp[name].astype(_jnp.float32)
        if MOMENT_SCALE is None:
            s = _jnp.sqrt(_jnp.mean(_jnp.square(w)) + 1e-30)
        else:
            s = MOMENT_SCALE[name]
        km, kv = _jax.random.split(_jax.random.fold_in(key, i + 1))
        out[name] = w
        out["m_" + name] = s * _jax.random.normal(km, w.shape, _jnp.float32)
        out["v_" + name] = (s * s) * _jax.random.uniform(kv, w.shape, _jnp.float32, 0.5, 1.5)
    if N_MICROBATCH > 1:
        for name, axis in PER_EXAMPLE_BATCH_AXIS.items():
            out[name] = _to_microbatches(out[name], axis)
    return {'x': out['x'], 'mem': out['mem'], 'norm_mix_g': out['norm_mix_g'], 'w_in': out['w_in'], 'b_forget': out['b_forget'], 'pool_w': out['pool_w'], 'pool_scale': out['pool_scale'], 'sgu_norm_g': out['sgu_norm_g'], 'sgu_w': out['sgu_w'], 'sgu_b': out['sgu_b'], 'w_branch_a': out['w_branch_a'], 'w_branch_b': out['w_branch_b'], 'w_branch_c': out['w_branch_c'], 'b_gate': out['b_gate'], 'w_out': out['w_out'], 'norm_xattn_g': out['norm_xattn_g'], 'norm_mem_g': out['norm_mem_g'], 'w_xq': out['w_xq'], 'w_xkv': out['w_xkv'], 'w_xo': out['w_xo'], 'norm_ffn_g': out['norm_ffn_g'], 'w_ff1': out['w_ff1'], 'w_ff2': out['w_ff2'], 'final_norm_g': out['final_norm_g'], 'loss_target': out['loss_target'], 'm_norm_mix_g': out['m_norm_mix_g'], 'm_w_in': out['m_w_in'], 'm_b_forget': out['m_b_forget'], 'm_pool_w': out['m_pool_w'], 'm_pool_scale': out['m_pool_scale'], 'm_sgu_norm_g': out['m_sgu_norm_g'], 'm_sgu_w': out['m_sgu_w'], 'm_sgu_b': out['m_sgu_b'], 'm_w_branch_a': out['m_w_branch_a'], 'm_w_branch_b': out['m_w_branch_b'], 'm_w_branch_c': out['m_w_branch_c'], 'm_b_gate': out['m_b_gate'], 'm_w_out': out['m_w_out'], 'm_norm_xattn_g': out['m_norm_xattn_g'], 'm_norm_mem_g': out['m_norm_mem_g'], 'm_w_xq': out['m_w_xq'], 'm_w_xkv': out['m_w_xkv'], 'm_w_xo': out['m_w_xo'], 'm_norm_ffn_g': out['m_norm_ffn_g'], 'm_w_ff1': out['m_w_ff1'], 'm_w_ff2': out['m_w_ff2'], 'm_final_norm_g': out['m_final_norm_g'], 'v_norm_mix_g': out['v_norm_mix_g'], 'v_w_in': out['v_w_in'], 'v_b_forget': out['v_b_forget'], 'v_pool_w': out['v_pool_w'], 'v_pool_scale': out['v_pool_scale'], 'v_sgu_norm_g': out['v_sgu_norm_g'], 'v_sgu_w': out['v_sgu_w'], 'v_sgu_b': out['v_sgu_b'], 'v_w_branch_a': out['v_w_branch_a'], 'v_w_branch_b': out['v_w_branch_b'], 'v_w_branch_c': out['v_w_branch_c'], 'v_b_gate': out['v_b_gate'], 'v_w_out': out['v_w_out'], 'v_norm_xattn_g': out['v_norm_xattn_g'], 'v_norm_mem_g': out['v_norm_mem_g'], 'v_w_xq': out['v_w_xq'], 'v_w_xkv': out['v_w_xkv'], 'v_w_xo': out['v_w_xo'], 'v_norm_ffn_g': out['v_norm_ffn_g'], 'v_w_ff1': out['v_w_ff1'], 'v_w_ff2': out['v_w_ff2'], 'v_final_norm_g': out['v_final_norm_g']}


def _loss(weights, diff, rest, loss_target):
    with _jax.named_scope("forward"):
        args = {**rest, TWIN_DIFF_INPUT: diff, **{k: w.astype(_WEIGHT_DTYPES[k]) for k, w in weights.items()}}
        y = _forward(args)
    with _jax.named_scope("loss_head"):
        err = _jnp.square(y.astype(_jnp.float32) - loss_target)
        return 0.5 * _jnp.sum(_jnp.mean(err, axis=-1)) if err.ndim else 0.5 * err


def _adamw(w, g, m, v):
    m = ADAM_B1 * m + (1.0 - ADAM_B1) * g
    v = ADAM_B2 * v + (1.0 - ADAM_B2) * _jnp.square(g)
    m_hat = m / (1.0 - ADAM_B1 ** ADAM_STEP)
    v_hat = v / (1.0 - ADAM_B2 ** ADAM_STEP)
    delta = -ADAM_LR * (m_hat / (_jnp.sqrt(v_hat) + ADAM_EPS) + ADAM_WD * w)
    return delta, m, v


def reference(x, mem, norm_mix_g, w_in, b_forget, pool_w, pool_scale, sgu_norm_g, sgu_w, sgu_b, w_branch_a, w_branch_b, w_branch_c, b_gate, w_out, norm_xattn_g, norm_mem_g, w_xq, w_xkv, w_xo, norm_ffn_g, w_ff1, w_ff2, final_norm_g, loss_target, m_norm_mix_g, m_w_in, m_b_forget, m_pool_w, m_pool_scale, m_sgu_norm_g, m_sgu_w, m_sgu_b, m_w_branch_a, m_w_branch_b, m_w_branch_c, m_b_gate, m_w_out, m_norm_xattn_g, m_norm_mem_g, m_w_xq, m_w_xkv, m_w_xo, m_norm_ffn_g, m_w_ff1, m_w_ff2, m_final_norm_g, v_norm_mix_g, v_w_in, v_b_forget, v_pool_w, v_pool_scale, v_sgu_norm_g, v_sgu_w, v_sgu_b, v_w_branch_a, v_w_branch_b, v_w_branch_c, v_b_gate, v_w_out, v_norm_xattn_g, v_norm_mem_g, v_w_xq, v_w_xkv, v_w_xo, v_norm_ffn_g, v_w_ff1, v_w_ff2, v_final_norm_g):
    given = dict(x=x, mem=mem, norm_mix_g=norm_mix_g, w_in=w_in, b_forget=b_forget, pool_w=pool_w, pool_scale=pool_scale, sgu_norm_g=sgu_norm_g, sgu_w=sgu_w, sgu_b=sgu_b, w_branch_a=w_branch_a, w_branch_b=w_branch_b, w_branch_c=w_branch_c, b_gate=b_gate, w_out=w_out, norm_xattn_g=norm_xattn_g, norm_mem_g=norm_mem_g, w_xq=w_xq, w_xkv=w_xkv, w_xo=w_xo, norm_ffn_g=norm_ffn_g, w_ff1=w_ff1, w_ff2=w_ff2, final_norm_g=final_norm_g, loss_target=loss_target, m_norm_mix_g=m_norm_mix_g, m_w_in=m_w_in, m_b_forget=m_b_forget, m_pool_w=m_pool_w, m_pool_scale=m_pool_scale, m_sgu_norm_g=m_sgu_norm_g, m_sgu_w=m_sgu_w, m_sgu_b=m_sgu_b, m_w_branch_a=m_w_branch_a, m_w_branch_b=m_w_branch_b, m_w_branch_c=m_w_branch_c, m_b_gate=m_b_gate, m_w_out=m_w_out, m_norm_xattn_g=m_norm_xattn_g, m_norm_mem_g=m_norm_mem_g, m_w_xq=m_w_xq, m_w_xkv=m_w_xkv, m_w_xo=m_w_xo, m_norm_ffn_g=m_norm_ffn_g, m_w_ff1=m_w_ff1, m_w_ff2=m_w_ff2, m_final_norm_g=m_final_norm_g, v_norm_mix_g=v_norm_mix_g, v_w_in=v_w_in, v_b_forget=v_b_forget, v_pool_w=v_pool_w, v_pool_scale=v_pool_scale, v_sgu_norm_g=v_sgu_norm_g, v_sgu_w=v_sgu_w, v_sgu_b=v_sgu_b, v_w_branch_a=v_w_branch_a, v_w_branch_b=v_w_branch_b, v_w_branch_c=v_w_branch_c, v_b_gate=v_b_gate, v_w_out=v_w_out, v_norm_xattn_g=v_norm_xattn_g, v_norm_mem_g=v_norm_mem_g, v_w_xq=v_w_xq, v_w_xkv=v_w_xkv, v_w_xo=v_w_xo, v_norm_ffn_g=v_norm_ffn_g, v_w_ff1=v_w_ff1, v_w_ff2=v_w_ff2, v_final_norm_g=v_final_norm_g)
    weights = {n: given[n] for n in TWIN_WEIGHTS}
    shared = {n: given[n] for n in SHARED_INPUTS}
    per_example = {n: given[n] for n in ['x', 'mem']}
    grad_fn = _jax.value_and_grad(_loss, argnums=(0, 1))

    def one_microbatch(ex, loss_target):
        ex = dict(ex)
        diff = ex.pop(TWIN_DIFF_INPUT)
        return grad_fn(weights, diff, {**shared, **ex}, loss_target)

    if N_MICROBATCH == 1:
        loss, (grad_w, grad_x) = one_microbatch(per_example, given["loss_target"])
    else:
        def body(carry, xs):
            loss_sum, grad_sum = carry
            l_k, (gw_k, gx_k) = one_microbatch(xs[0], xs[1])
            with _jax.named_scope("update"):
                return (loss_sum + l_k, _jax.tree.map(_jnp.add, grad_sum, gw_k)), gx_k

        init = (_jnp.zeros((), _jnp.float32), _jax.tree.map(_jnp.zeros_like, weights))
        (loss, grad_w), grad_x = _jax.lax.scan(body, init, (per_example, given["loss_target"]))
    with _jax.named_scope("update"):
        delta_w, new_m, new_v = {}, {}, {}
        for n in TWIN_WEIGHTS:
            delta_w[n], new_m[n], new_v[n] = _adamw(weights[n], grad_w[n], given["m_" + n], given["v_" + n])
    return (loss, grad_x, *[grad_w[n] for n in TWIN_WEIGHTS], *[delta_w[n] for n in TWIN_WEIGHTS],
            *[new_m[n] for n in TWIN_WEIGHTS], *[new_v[n] for n in TWIN_WEIGHTS])
```

```python
import functools
import math

import jax
import jax.numpy as jnp
from jax import lax
from jax.experimental import pallas as pl
from jax.experimental.pallas import tpu as pltpu

F32 = jnp.float32
BF16 = jnp.bfloat16

D = 1024
DEPTH = 2
POOL_W = 256
FOX_W = 512
SGU_W = 256
SGU_CHUNK = 128
N_IN = 5384
P_G, P_Q, P_K, P_V, P_C, P_A, P_F = 0, 3072, 3584, 4096, 4608, 5120, 5376
NP = 5632
XH, XHD = 4, 256
D_FF = 4096
EPS = 1e-6
NEG = -1e30
FOX_SCALE = 64 ** -0.5
X_SCALE = 256 ** -0.5
GELU_K = math.sqrt(2.0 / math.pi)
GELU_C = 0.044715

ADAM_LR, ADAM_B1, ADAM_B2, ADAM_EPS, ADAM_WD, ADAM_STEP = 0.001, 0.9, 0.999, 1e-08, 0.01, 10

VMEM_LIMIT = 48 * 1024 * 1024
MESH = pl.DeviceIdType.MESH

IN_NAMES = ['x', 'mem', 'norm_mix_g', 'w_in', 'b_forget', 'pool_w', 'pool_scale', 'sgu_norm_g', 'sgu_w', 'sgu_b',
            'w_branch_a', 'w_branch_b', 'w_branch_c', 'b_gate', 'w_out', 'norm_xattn_g', 'norm_mem_g', 'w_xq',
            'w_xkv', 'w_xo', 'norm_ffn_g', 'w_ff1', 'w_ff2', 'final_norm_g']
W_NAMES = IN_NAMES[2:]
BIG = [('w_in', (D, N_IN), 1), ('w_branch_a', (POOL_W, D), 1), ('w_branch_b', (FOX_W, D), 1),
       ('w_branch_c', (SGU_W, D), 1), ('w_out', (D, D), 0), ('w_xq', (D, D), 0), ('w_xkv', (D, 2 * D), 1),
       ('w_xo', (D, D), 0), ('w_ff1', (D, D_FF), 1), ('w_ff2', (D_FF, D), 0)]
BIG_NAMES = [b[0] for b in BIG]
SMALL_NAMES = [n for n in W_NAMES if n not in BIG_NAMES]
PACK_COLS = 1024
PACK_ROWS = 4960


def _cp(sem=None):
    return pltpu.CompilerParams(dimension_semantics=sem, vmem_limit_bytes=VMEM_LIMIT)


def _shard_shape(shape, axis):
    s = list(shape)
    s[axis] //= 4
    return tuple(s)


def _mm(a, b, *, name, out_dtype, ta=False, tb=False, tm=1024, tn=512, tk=1024, a_fn=None, extra=None, epi=None):
    M = a.shape[1] if ta else a.shape[0]
    K = a.shape[0] if ta else a.shape[1]
    N = b.shape[0] if tb else b.shape[1]
    tm, tn, tk = min(tm, M), min(tn, N), min(tk, K)
    assert M % tm == 0 and N % tn == 0 and K % tk == 0, (name, M, N, K)
    nk = K // tk
    a_spec = pl.BlockSpec((tk, tm), lambda i, j, k: (k, i)) if ta else pl.BlockSpec((tm, tk), lambda i, j, k: (i, k))
    b_spec = pl.BlockSpec((tn, tk), lambda i, j, k: (j, k)) if tb else pl.BlockSpec((tk, tn), lambda i, j, k: (k, j))
    dn = (((0 if ta else 1,), (1 if tb else 0,)), ((), ()))
    o_spec = pl.BlockSpec((tm, tn), lambda i, j, k: (i, j))
    in_specs = [a_spec, b_spec] + ([o_spec] if extra is not None else [])

    def body(*refs):
        if extra is not None:
            a_ref, b_ref, e_ref, o_ref, acc_ref = refs
        else:
            a_ref, b_ref, o_ref, acc_ref = refs
            e_ref = None
        k = pl.program_id(2)

        @pl.when(k == 0)
        def _():
            acc_ref[...] = jnp.zeros_like(acc_ref)

        av = a_ref[...]
        if a_fn is not None:
            av = a_fn(av)
        acc_ref[...] += lax.dot_general(av.astype(BF16), b_ref[...].astype(BF16), dn, preferred_element_type=F32)

        @pl.when(k == nk - 1)
        def _():
            r = acc_ref[...]
            if epi is not None:
                r = epi(r, e_ref[...]) if e_ref is not None else epi(r)
            o_ref[...] = r.astype(out_dtype)

    args = (a, b) + ((extra,) if extra is not None else ())
    return pl.pallas_call(
        body, out_shape=jax.ShapeDtypeStruct((M, N), out_dtype), grid=(M // tm, N // tn, nk),
        in_specs=in_specs, out_specs=o_spec, scratch_shapes=[pltpu.VMEM((tm, tn), F32)],
        compiler_params=_cp(("parallel", "parallel", "arbitrary")), name=name)(*args)


def _relu2(z):
    r = jnp.maximum(z, 0.0)
    return r * r


def _rms_fwd(x, g, name, tr=256):
    R, n = x.shape
    tr = min(tr, R)

    def body(x_ref, g_ref, h_ref):
        xv = x_ref[...]
        rstd = lax.rsqrt(jnp.mean(xv * xv, axis=-1, keepdims=True) + EPS)
        h_ref[...] = (xv * rstd * g_ref[...]).astype(BF16)

    return pl.pallas_call(
        body, out_shape=jax.ShapeDtypeStruct((R, n), BF16), grid=(R // tr,),
        in_specs=[pl.BlockSpec((tr, n), lambda i: (i, 0)), pl.BlockSpec((1, n), lambda i: (0, 0))],
        out_specs=pl.BlockSpec((tr, n), lambda i: (i, 0)), compiler_params=_cp(("parallel",)), name=name)(x, g)


def _rms_bwd(dh, x, g, dres, name, tr=256):
    R, n = x.shape
    tr = min(tr, R)
    need_dx = dres is not None

    def body(*refs):
        if need_dx:
            dh_ref, x_ref, g_ref, r_ref, dx_ref, dg_ref = refs
        else:
            dh_ref, x_ref, g_ref, dg_ref = refs
        i = pl.program_id(0)
        xv = x_ref[...]
        dhv = dh_ref[...].astype(F32)
        rstd = lax.rsqrt(jnp.mean(xv * xv, axis=-1, keepdims=True) + EPS)
        xhat = xv * rstd

        @pl.when(i == 0)
        def _():
            dg_ref[...] = jnp.zeros_like(dg_ref)

        dg_ref[...] += jnp.sum(dhv * xhat, axis=0, keepdims=True)
        if need_dx:
            t = dhv * g_ref[...]
            dx_ref[...] = r_ref[...] + rstd * (t - xhat * jnp.mean(t * xhat, axis=-1, keepdims=True))

    row = pl.BlockSpec((tr, n), lambda i: (i, 0))
    vec = pl.BlockSpec((1, n), lambda i: (0, 0))
    if need_dx:
        return pl.pallas_call(
            body, out_shape=(jax.ShapeDtypeStruct((R, n), F32), jax.ShapeDtypeStruct((1, n), F32)), grid=(R // tr,),
            in_specs=[row, row, vec, row], out_specs=(row, vec), compiler_params=_cp(("arbitrary",)), name=name)(dh, x, g, dres)
    return pl.pallas_call(
        body, out_shape=jax.ShapeDtypeStruct((1, n), F32), grid=(R // tr,),
        in_specs=[row, row, vec], out_specs=vec, compiler_params=_cp(("arbitrary",)), name=name)(dh, x, g)


def _loss_head(x, g, tgt, name, tr=256):
    R, n = x.shape

    def body(x_ref, g_ref, t_ref, loss_ref, dx_ref, dg_ref):
        i = pl.program_id(0)
        xv = x_ref[...]
        gv = g_ref[...]
        rstd = lax.rsqrt(jnp.mean(xv * xv, axis=-1, keepdims=True) + EPS)
        xhat = xv * rstd
        e = xhat * gv - t_ref[...]

        @pl.when(i == 0)
        def _():
            loss_ref[...] = jnp.zeros_like(loss_ref)
            dg_ref[...] = jnp.zeros_like(dg_ref)

        loss_ref[...] += 0.5 * jnp.sum(jnp.sum(e * e, axis=-1, keepdims=True) / n, axis=0, keepdims=True)
        dy = e / n
        dg_ref[...] += jnp.sum(dy * xhat, axis=0, keepdims=True)
        t = dy * gv
        dx_ref[...] = rstd * (t - xhat * jnp.mean(t * xhat, axis=-1, keepdims=True))

    row = pl.BlockSpec((tr, n), lambda i: (i, 0))
    vec = pl.BlockSpec((1, n), lambda i: (0, 0))
    one = pl.BlockSpec((1, 1), lambda i: (0, 0))
    return pl.pallas_call(
        body, out_shape=(jax.ShapeDtypeStruct((1, 1), F32), jax.ShapeDtypeStruct((R, n), F32), jax.ShapeDtypeStruct((1, n), F32)),
        grid=(R // tr,), in_specs=[row, vec, row], out_specs=(one, row, vec),
        compiler_params=_cp(("arbitrary",)), name=name)(x, g, tgt)


def _pool_masks(S):
    row = lax.broadcasted_iota(jnp.int32, (S, POOL_W), 0)
    grp = lax.broadcasted_iota(jnp.int32, (S, POOL_W), 1) // 64
    win = jnp.where(grp == 0, 2, jnp.where(grp == 1, 4, jnp.where(grp == 2, 8, 16)))
    cnt = jnp.minimum(row + 1, win).astype(F32)
    return row, grp, cnt


def _by_group(grp, v0, v1, v2, v3):
    return jnp.where(grp == 0, v0, jnp.where(grp == 1, v1, jnp.where(grp == 2, v2, v3)))


def _pool_fwd(proj, bd, scale, name):
    S = proj.shape[0]

    def body(a_ref, bd_ref, sc_ref, d_ref, y_ref):
        a = a_ref[...]
        row, grp, cnt = _pool_masks(S)

        def back(v, k):
            return jnp.where(row >= k, pltpu.roll(v, k, 0), 0.0)

        s1 = a + back(a, 1)
        s2 = s1 + back(s1, 2)
        s3 = s2 + back(s2, 4)
        s4 = s3 + back(s3, 8)
        d = (_by_group(grp, s1, s2, s3, s4) / cnt - a).astype(BF16)
        d_ref[...] = d
        y_ref[...] = (jnp.dot(d, bd_ref[...], preferred_element_type=F32) * sc_ref[...]).astype(BF16)

    full = lambda r, c: pl.BlockSpec((r, c), lambda i: (0, 0))
    return pl.pallas_call(
        body, out_shape=(jax.ShapeDtypeStruct((S, POOL_W), BF16), jax.ShapeDtypeStruct((S, POOL_W), BF16)), grid=(1,),
        in_specs=[pl.BlockSpec((S, POOL_W), lambda i: (0, P_A // POOL_W)), full(POOL_W, POOL_W), full(1, POOL_W)],
        out_specs=(full(S, POOL_W), full(S, POOL_W)), compiler_params=_cp(("arbitrary",)), name=name)(proj, bd, scale)


def _pool_bwd(dya, d, bd, scale, name):
    S = dya.shape[0]

    def body(dy_ref, d_ref, bd_ref, sc_ref, da_ref, dbd_ref, dsc_ref):
        dy = dy_ref[...]
        dv = d_ref[...]
        bdv = bd_ref[...]
        row, grp, cnt = _pool_masks(S)
        yraw = jnp.dot(dv, bdv, preferred_element_type=F32)
        dsc_ref[...] = jnp.sum(dy * yraw, axis=0, keepdims=True)
        tb = (dy * sc_ref[...]).astype(BF16)
        dbd_ref[...] = lax.dot_general(dv, tb, (((0,), (0,)), ((), ())), preferred_element_type=F32)
        dd = lax.dot_general(tb, bdv, (((1,), (1,)), ((), ())), preferred_element_type=F32)
        e = dd / cnt

        def fwd(v, k):
            return jnp.where(row < S - k, pltpu.roll(v, S - k, 0), 0.0)

        r1 = e + fwd(e, 1)
        r2 = r1 + fwd(r1, 2)
        r3 = r2 + fwd(r2, 4)
        r4 = r3 + fwd(r3, 8)
        da_ref[...] = (_by_group(grp, r1, r2, r3, r4) - dd).astype(BF16)

    full = lambda r, c: pl.BlockSpec((r, c), lambda i: (0, 0))
    return pl.pallas_call(
        body, out_shape=(jax.ShapeDtypeStruct((S, POOL_W), BF16), jax.ShapeDtypeStruct((POOL_W, POOL_W), F32),
                         jax.ShapeDtypeStruct((1, POOL_W), F32)), grid=(1,),
        in_specs=[full(S, POOL_W), full(S, POOL_W), full(POOL_W, POOL_W), full(1, POOL_W)],
        out_specs=(full(S, POOL_W), full(POOL_W, POOL_W), full(1, POOL_W)),
        compiler_params=_cp(("arbitrary",)), name=name)(dya, d, bd, scale)


FCOLS = 128


def _log_sigmoid(z):
    return -(jnp.maximum(-z, 0.0) + jnp.log1p(jnp.exp(-jnp.abs(z))))


def _fgate_fwd(proj, bf, name):
    S = proj.shape[0]

    def body(f_ref, b_ref, o_ref):
        v = _log_sigmoid(f_ref[...] + b_ref[...])
        row = lax.broadcasted_iota(jnp.int32, (S, FCOLS), 0)
        k = 1
        while k < S:
            v = v + jnp.where(row >= k, pltpu.roll(v, k, 0), 0.0)
            k *= 2
        o_ref[...] = v

    return pl.pallas_call(
        body, out_shape=jax.ShapeDtypeStruct((S, FCOLS), F32), grid=(1,),
        in_specs=[pl.BlockSpec((S, FCOLS), lambda i: (0, P_F // FCOLS)), pl.BlockSpec((1, FCOLS), lambda i: (0, 0))],
        out_specs=pl.BlockSpec((S, FCOLS), lambda i: (0, 0)), compiler_params=_cp(("arbitrary",)), name=name)(proj, bf)


def _fgate_bwd(dF, proj, bf, name):
    S = proj.shape[0]

    def body(dF_ref, f_ref, b_ref, df_ref, db_ref):
        v = dF_ref[...]
        row = lax.broadcasted_iota(jnp.int32, (S, FCOLS), 0)
        k = 1
        while k < S:
            v = v + jnp.where(row < S - k, pltpu.roll(v, S - k, 0), 0.0)
            k *= 2
        z = f_ref[...] + b_ref[...]
        df = v * (1.0 / (1.0 + jnp.exp(z)))
        db_ref[...] = jnp.sum(df, axis=0, keepdims=True)
        df_ref[...] = jnp.concatenate([df, jnp.zeros_like(df)], axis=1).astype(BF16)

    return pl.pallas_call(
        body, out_shape=(jax.ShapeDtypeStruct((S, 2 * FCOLS), BF16), jax.ShapeDtypeStruct((1, FCOLS), F32)), grid=(1,),
        in_specs=[pl.BlockSpec((S, FCOLS), lambda i: (0, 0)), pl.BlockSpec((S, FCOLS), lambda i: (0, P_F // FCOLS)),
                  pl.BlockSpec((1, FCOLS), lambda i: (0, 0))],
        out_specs=(pl.BlockSpec((S, 2 * FCOLS), lambda i: (0, 0)), pl.BlockSpec((1, FCOLS), lambda i: (0, 0))),
        compiler_params=_cp(("arbitrary",)), name=name)(dF, proj, bf)


def _fox_scores(qe, kj, fq, fk, r0, c0, tq, tk):
    s = lax.dot_general(qe, kj, (((1,), (1,)), ((), ())), preferred_element_type=F32) * FOX_SCALE
    s = s + (fq - fk)
    rows = r0 + lax.broadcasted_iota(jnp.int32, (tq, tk), 0)
    cols = c0 + lax.broadcasted_iota(jnp.int32, (tq, tk), 1)
    return jnp.where(rows >= cols, s, NEG)


def _fox_fwd(qkv, fcol, frow, name, tq=256):
    S = qkv.shape[0]
    tk = tq

    def body(q_ref, k_ref, v_ref, fc_ref, fr_ref, o_ref, lse_ref):
        i = pl.program_id(1)
        r0 = i * tq
        q = q_ref[...]
        half = lax.broadcasted_iota(jnp.int32, (tq, 128), 1) // 64
        outs = []
        for e in (0, 1):
            qe = jnp.where(half == e, q, jnp.zeros_like(q))
            fq = fc_ref[0, :, e:e + 1]

            def step(j, carry, qe=qe, fq=fq, e=e):
                m, l, acc = carry
                c0 = pl.multiple_of(j * tk, tk)
                kj = k_ref[pl.ds(c0, tk), :]
                vj = v_ref[pl.ds(c0, tk), :]
                fk = fr_ref[0, e:e + 1, pl.ds(c0, tk)]
                s = _fox_scores(qe, kj, fq, fk, r0, c0, tq, tk)
                m_new = jnp.maximum(m, jnp.max(s, axis=-1, keepdims=True))
                alpha = jnp.exp(m - m_new)
                p = jnp.exp(s - m_new)
                l = alpha * l + jnp.sum(p, axis=-1, keepdims=True)
                acc = alpha * acc + jnp.dot(p.astype(BF16), vj, preferred_element_type=F32)
                return m_new, l, acc

            m, l, acc = lax.fori_loop(0, i + 1, step, (jnp.full((tq, 1), NEG, F32), jnp.zeros((tq, 1), F32),
                                                      jnp.zeros((tq, 128), F32)))
            outs.append(acc / l)
            lse_ref[0, :, e:e + 1] = m + jnp.log(l)
        o_ref[...] = jnp.where(half == 0, outs[0], outs[1]).astype(BF16)

    return pl.pallas_call(
        body, out_shape=(jax.ShapeDtypeStruct((S, FOX_W), BF16), jax.ShapeDtypeStruct((4, S, 2), F32)), grid=(4, S // tq),
        in_specs=[pl.BlockSpec((tq, 128), lambda h, i: (i, h)), pl.BlockSpec((S, 128), lambda h, i: (0, 4 + h)),
                  pl.BlockSpec((S, 128), lambda h, i: (0, 8 + h)), pl.BlockSpec((1, tq, 2), lambda h, i: (h, i, 0)),
                  pl.BlockSpec((1, 2, S), lambda h, i: (h, 0, 0))],
        out_specs=(pl.BlockSpec((tq, 128), lambda h, i: (i, h)), pl.BlockSpec((1, tq, 2), lambda h, i: (h, i, 0))),
        compiler_params=_cp(("parallel", "parallel")), name=name)(qkv, qkv, qkv, fcol, frow)


def _fox_bwd(qkv, do, lse, fcol, frow, name, tq=256):
    S = qkv.shape[0]
    tk = tq
    nq = S // tq

    def body(q_ref, k_ref, v_ref, do_ref, lse_ref, fc_ref, fr_ref, dq_ref, dk_ref, dv_ref, dfr_ref, dk_acc, dv_acc):
        dk_acc[...] = jnp.zeros_like(dk_acc)
        dv_acc[...] = jnp.zeros_like(dv_acc)
        dfr_ref[...] = jnp.zeros_like(dfr_ref)
        half = lax.broadcasted_iota(jnp.int32, (tq, 128), 1) // 64

        def q_block(i, _):
            r0 = pl.multiple_of(i * tq, tq)
            qi = q_ref[pl.ds(r0, tq), :]
            dob = do_ref[pl.ds(r0, tq), :].astype(BF16)
            dq_tot = jnp.zeros((tq, 128), F32)
            for e in (0, 1):
                qe = jnp.where(half == e, qi, jnp.zeros_like(qi))
                doe = jnp.where(half == e, dob, jnp.zeros_like(dob))
                lse_e = lse_ref[0, pl.ds(r0, tq), e:e + 1]
                fq = fc_ref[0, pl.ds(r0, tq), e:e + 1]

                def probs(j, qe=qe, doe=doe, lse_e=lse_e, fq=fq, e=e):
                    c0 = pl.multiple_of(j * tk, tk)
                    kj = k_ref[pl.ds(c0, tk), :]
                    vj = v_ref[pl.ds(c0, tk), :]
                    fk = fr_ref[0, e:e + 1, pl.ds(c0, tk)]
                    p = jnp.exp(_fox_scores(qe, kj, fq, fk, r0, c0, tq, tk) - lse_e)
                    dp = lax.dot_general(doe, vj, (((1,), (1,)), ((), ())), preferred_element_type=F32)
                    return c0, kj, p, dp

                def row_term(j, acc, probs=probs):
                    _, _, p, dp = probs(j)
                    return acc + jnp.sum(p * dp, axis=-1, keepdims=True)

                delta = lax.fori_loop(0, i + 1, row_term, jnp.zeros((tq, 1), F32))

                def step(j, dq, probs=probs, delta=delta, e=e):
                    c0, kj, p, dp = probs(j)
                    ds = p * (dp - delta)
                    dfr_ref[0, e:e + 1, pl.ds(c0, tk)] -= jnp.sum(ds, axis=0, keepdims=True)
                    dsb = (ds * FOX_SCALE).astype(BF16)
                    dkc = lax.dot_general(dsb, qi, (((0,), (0,)), ((), ())), preferred_element_type=F32)
                    dvc = lax.dot_general(p.astype(BF16), dob, (((0,), (0,)), ((), ())), preferred_element_type=F32)
                    dk_acc[pl.ds(c0, tk), :] += jnp.where(half == e, dkc, 0.0)
                    dv_acc[pl.ds(c0, tk), :] += jnp.where(half == e, dvc, 0.0)
                    return dq + jnp.dot(dsb, kj, preferred_element_type=F32)

                dq_e = lax.fori_loop(0, i + 1, step, jnp.zeros((tq, 128), F32))
                dq_tot = dq_tot + jnp.where(half == e, dq_e, 0.0)
            dq_ref[pl.ds(r0, tq), :] = dq_tot.astype(BF16)
            return 0

        lax.fori_loop(0, nq, q_block, 0)
        dk_ref[...] = dk_acc[...].astype(BF16)
        dv_ref[...] = dv_acc[...].astype(BF16)

    col = lambda off: pl.BlockSpec((S, 128), lambda h: (0, off + h))
    hs2 = pl.BlockSpec((1, S, 2), lambda h: (h, 0, 0))
    h2s = pl.BlockSpec((1, 2, S), lambda h: (h, 0, 0))
    return pl.pallas_call(
        body, out_shape=(jax.ShapeDtypeStruct((S, FOX_W), BF16),) * 3 + (jax.ShapeDtypeStruct((4, 2, S), F32),), grid=(4,),
        in_specs=[col(0), col(4), col(8), col(0), hs2, hs2, h2s],
        out_specs=(col(0), col(0), col(0), h2s),
        scratch_shapes=[pltpu.VMEM((S, 128), F32), pltpu.VMEM((S, 128), F32)],
        compiler_params=_cp(("parallel",)), name=name)(qkv, qkv, qkv, do, lse, fcol, frow)


def _gelu(x):
    return 0.5 * x * (1.0 + jnp.tanh(GELU_K * (x + GELU_C * x * x * x)))


def _gelu_grad(x):
    th = jnp.tanh(GELU_K * (x + GELU_C * x * x * x))
    return 0.5 * (1.0 + th) + 0.5 * x * (1.0 - th * th) * GELU_K * (1.0 + 3.0 * GELU_C * x * x)


def _sgu_parts(c, gn, w_ref, bias):
    zc = _gelu(c)
    u, vv = zc[:, :SGU_W], zc[:, SGU_W:]
    rstd = lax.rsqrt(jnp.mean(vv * vv, axis=-1, keepdims=True) + EPS)
    vhat = vv * rstd
    vnb = (vhat * gn).astype(BF16)
    grp = lax.broadcasted_iota(jnp.int32, (SGU_CHUNK, SGU_W), 1) // 64
    mixed = bias
    for gi in range(4):
        mixed = mixed + jnp.where(grp == gi, jnp.dot(w_ref[gi], vnb, preferred_element_type=F32), 0.0)
    return u, rstd, vhat, vnb, grp, mixed


def _sgu_fwd(proj, gn, wm, bias, name):
    S = proj.shape[0]

    def body(c_ref, g_ref, w_ref, b_ref, o_ref):
        u, _, _, _, _, mixed = _sgu_parts(c_ref[...], g_ref[...], w_ref, b_ref[...])
        o_ref[...] = (u * mixed).astype(BF16)

    return pl.pallas_call(
        body, out_shape=jax.ShapeDtypeStruct((S, SGU_W), BF16), grid=(S // SGU_CHUNK,),
        in_specs=[pl.BlockSpec((SGU_CHUNK, 2 * SGU_W), lambda i: (i, P_C // (2 * SGU_W))),
                  pl.BlockSpec((1, SGU_W), lambda i: (0, 0)), pl.BlockSpec((4, SGU_CHUNK, SGU_CHUNK), lambda i: (0, 0, 0)),
                  pl.BlockSpec((SGU_CHUNK, SGU_W), lambda i: (0, 0))],
        out_specs=pl.BlockSpec((SGU_CHUNK, SGU_W), lambda i: (i, 0)),
        compiler_params=_cp(("parallel",)), name=name)(proj, gn, wm, bias)


def _sgu_bwd(dsg, proj, gn, wm, wmt, bias, name):
    S = proj.shape[0]

    def body(dsg_ref, c_ref, g_ref, w_ref, wt_ref, b_ref, dc_ref, dw_ref, db_ref, dg_ref):
        i = pl.program_id(0)

        @pl.when(i == 0)
        def _():
            dw_ref[...] = jnp.zeros_like(dw_ref)
            db_ref[...] = jnp.zeros_like(db_ref)
            dg_ref[...] = jnp.zeros_like(dg_ref)

        c = c_ref[...]
        gn_v = g_ref[...]
        u, rstd, vhat, vnb, grp, mixed = _sgu_parts(c, gn_v, w_ref, b_ref[...])
        dsg_v = dsg_ref[...]
        du = dsg_v * mixed
        dmix = dsg_v * u
        db_ref[...] += dmix
        dmb = dmix.astype(BF16)
        dvn = jnp.zeros((SGU_CHUNK, SGU_W), F32)
        for gi in range(4):
            dmg = jnp.where(grp == gi, dmb, jnp.zeros_like(dmb))
            dw_ref[gi] += lax.dot_general(dmg, vnb, (((1,), (1,)), ((), ())), preferred_element_type=F32)
            dvn = dvn + jnp.where(grp == gi, jnp.dot(wt_ref[gi], dmb, preferred_element_type=F32), 0.0)
        dg_ref[...] += jnp.sum(dvn * vhat, axis=0, keepdims=True)
        t = dvn * gn_v
        dvv = rstd * (t - vhat * jnp.mean(t * vhat, axis=-1, keepdims=True))
        dc_ref[...] = (jnp.concatenate([du, dvv], axis=1) * _gelu_grad(c)).astype(BF16)

    w_spec = pl.BlockSpec((4, SGU_CHUNK, SGU_CHUNK), lambda i: (0, 0, 0))
    tile = pl.BlockSpec((SGU_CHUNK, SGU_W), lambda i: (0, 0))
    vec = pl.BlockSpec((1, SGU_W), lambda i: (0, 0))
    return pl.pallas_call(
        body, out_shape=(jax.ShapeDtypeStruct((S, 2 * SGU_W), BF16), jax.ShapeDtypeStruct((4, SGU_CHUNK, SGU_CHUNK), F32),
                         jax.ShapeDtypeStruct((SGU_CHUNK, SGU_W), F32), jax.ShapeDtypeStruct((1, SGU_W), F32)),
        grid=(S // SGU_CHUNK,),
        in_specs=[pl.BlockSpec((SGU_CHUNK, SGU_W), lambda i: (i, 0)),
                  pl.BlockSpec((SGU_CHUNK, 2 * SGU_W), lambda i: (i, P_C // (2 * SGU_W))), vec, w_spec, w_spec, tile],
        out_specs=(pl.BlockSpec((SGU_CHUNK, 2 * SGU_W), lambda i: (i, 0)), w_spec, tile, vec),
        compiler_params=_cp(("arbitrary",)), name=name)(dsg, proj, gn, wm, wmt, bias)


def _sigmoid(z):
    return 1.0 / (1.0 + jnp.exp(-z))


def _merge_specs(tm):
    row = lambda n: pl.BlockSpec((tm, n), lambda i: (i, 0))
    gate = lambda b: pl.BlockSpec((tm, D), lambda i: (i, b))
    full = lambda r, c: pl.BlockSpec((r, c), lambda i: (0, 0))
    return row, gate, full


def _merge_fwd(proj, ya, o, sg, wa, wb, wc, bg, name, tm=256):
    S = proj.shape[0]
    row, gate, full = _merge_specs(tm)

    def body(g0, g1, g2, ya_ref, o_ref, sg_ref, wa_ref, wb_ref, wc_ref, bg_ref, out_ref):
        acc = jnp.zeros((tm, D), F32)
        for b, (g_ref, br_ref, w_ref) in enumerate(((g0, ya_ref, wa_ref), (g1, o_ref, wb_ref), (g2, sg_ref, wc_ref))):
            y = jnp.dot(br_ref[...], w_ref[...], preferred_element_type=F32)
            acc = acc + _sigmoid(g_ref[...] + bg_ref[:, b * D:(b + 1) * D]) * y
        out_ref[...] = acc.astype(BF16)

    return pl.pallas_call(
        body, out_shape=jax.ShapeDtypeStruct((S, D), BF16), grid=(S // tm,),
        in_specs=[gate(0), gate(1), gate(2), row(POOL_W), row(FOX_W), row(SGU_W), full(POOL_W, D), full(FOX_W, D),
                  full(SGU_W, D), full(1, 3 * D)],
        out_specs=row(D), compiler_params=_cp(("parallel",)), name=name)(proj, proj, proj, ya, o, sg, wa, wb, wc, bg)


def _merge_bwd(dm, proj, ya, o, sg, wa, wb, wc, bg, name, tm=256):
    S = proj.shape[0]
    row, gate, full = _merge_specs(tm)

    def body(dm_ref, g0, g1, g2, ya_ref, o_ref, sg_ref, wa_ref, wb_ref, wc_ref, bg_ref,
             dg_ref, dya_ref, do_ref, dsg_ref, dwa_ref, dwb_ref, dwc_ref, dbg_ref, awa, awb, awc):
        i = pl.program_id(0)

        @pl.when(i == 0)
        def _():
            awa[...] = jnp.zeros_like(awa)
            awb[...] = jnp.zeros_like(awb)
            awc[...] = jnp.zeros_like(awc)
            dbg_ref[...] = jnp.zeros_like(dbg_ref)

        dmv = dm_ref[...]
        for b, (g_ref, br_ref, w_ref, dbr_ref, acc_ref) in enumerate(
                ((g0, ya_ref, wa_ref, dya_ref, awa), (g1, o_ref, wb_ref, do_ref, awb), (g2, sg_ref, wc_ref, dsg_ref, awc))):
            br = br_ref[...]
            wv = w_ref[...]
            y = jnp.dot(br, wv, preferred_element_type=F32)
            gt = _sigmoid(g_ref[...] + bg_ref[:, b * D:(b + 1) * D])
            dgp = dmv * y * gt * (1.0 - gt)
            dg_ref[:, b * D:(b + 1) * D] = dgp.astype(BF16)
            dbg_ref[:, b * D:(b + 1) * D] += jnp.sum(dgp, axis=0, keepdims=True)
            dy = (dmv * gt).astype(BF16)
            dbr_ref[...] = lax.dot_general(dy, wv, (((1,), (1,)), ((), ())), preferred_element_type=F32)
            acc_ref[...] += lax.dot_general(br, dy, (((0,), (0,)), ((), ())), preferred_element_type=F32)

        @pl.when(i == pl.num_programs(0) - 1)
        def _():
            dwa_ref[...] = awa[...].astype(BF16)
            dwb_ref[...] = awb[...].astype(BF16)
            dwc_ref[...] = awc[...].astype(BF16)

    return pl.pallas_call(
        body, out_shape=(jax.ShapeDtypeStruct((S, 3 * D), BF16), jax.ShapeDtypeStruct((S, POOL_W), F32),
                         jax.ShapeDtypeStruct((S, FOX_W), F32), jax.ShapeDtypeStruct((S, SGU_W), F32),
                         jax.ShapeDtypeStruct((POOL_W, D), BF16), jax.ShapeDtypeStruct((FOX_W, D), BF16),
                         jax.ShapeDtypeStruct((SGU_W, D), BF16), jax.ShapeDtypeStruct((1, 3 * D), F32)),
        grid=(S // tm,),
        in_specs=[row(D), gate(0), gate(1), gate(2), row(POOL_W), row(FOX_W), row(SGU_W), full(POOL_W, D), full(FOX_W, D),
                  full(SGU_W, D), full(1, 3 * D)],
        out_specs=(row(3 * D), row(POOL_W), row(FOX_W), row(SGU_W), full(POOL_W, D), full(FOX_W, D), full(SGU_W, D),
                   full(1, 3 * D)),
        scratch_shapes=[pltpu.VMEM((POOL_W, D), F32), pltpu.VMEM((FOX_W, D), F32), pltpu.VMEM((SGU_W, D), F32)],
        compiler_params=_cp(("arbitrary",)), name=name)(dm, proj, proj, proj, ya, o, sg, wa, wb, wc, bg)


def _xattn_probs(qh, kh):
    s = lax.dot_general(qh, kh, (((1,), (1,)), ((), ())), preferred_element_type=F32) * X_SCALE
    p = jnp.exp(s - jnp.max(s, axis=-1, keepdims=True))
    return p / jnp.sum(p, axis=-1, keepdims=True)


def _xattn_fwd(xq, kv, name, tq=256):
    S = xq.shape[0]
    M = kv.shape[0]

    def body(q_ref, k_ref, v_ref, o_ref):
        for h in range(XH):
            sl = slice(h * XHD, (h + 1) * XHD)
            p = _xattn_probs(q_ref[:, sl], k_ref[:, sl])
            o_ref[:, sl] = jnp.dot(p.astype(BF16), v_ref[:, sl], preferred_element_type=F32).astype(BF16)

    return pl.pallas_call(
        body, out_shape=jax.ShapeDtypeStruct((S, D), BF16), grid=(S // tq,),
        in_specs=[pl.BlockSpec((tq, D), lambda i: (i, 0)), pl.BlockSpec((M, D), lambda i: (0, 0)),
                  pl.BlockSpec((M, D), lambda i: (0, 1))],
        out_specs=pl.BlockSpec((tq, D), lambda i: (i, 0)), compiler_params=_cp(("parallel",)), name=name)(xq, kv, kv)


def _xattn_bwd(xq, kv, do, name, tq=256):
    S = xq.shape[0]
    M = kv.shape[0]

    def body(q_ref, k_ref, v_ref, do_ref, dq_ref, dkv_ref, dk_acc, dv_acc):
        i = pl.program_id(0)

        @pl.when(i == 0)
        def _():
            dk_acc[...] = jnp.zeros_like(dk_acc)
            dv_acc[...] = jnp.zeros_like(dv_acc)

        for h in range(XH):
            sl = slice(h * XHD, (h + 1) * XHD)
            qh, kh, vh, doh = q_ref[:, sl], k_ref[:, sl], v_ref[:, sl], do_ref[:, sl]
            p = _xattn_probs(qh, kh)
            dp = lax.dot_general(doh, vh, (((1,), (1,)), ((), ())), preferred_element_type=F32)
            ds = p * (dp - jnp.sum(p * dp, axis=-1, keepdims=True))
            dsb = (ds * X_SCALE).astype(BF16)
            dq_ref[:, sl] = jnp.dot(dsb, kh, preferred_element_type=F32).astype(BF16)
            dk_acc[:, sl] += lax.dot_general(dsb, qh, (((0,), (0,)), ((), ())), preferred_element_type=F32)
            dv_acc[:, sl] += lax.dot_general(p.astype(BF16), doh, (((0,), (0,)), ((), ())), preferred_element_type=F32)

        @pl.when(i == pl.num_programs(0) - 1)
        def _():
            dkv_ref[:, :D] = dk_acc[...].astype(BF16)
            dkv_ref[:, D:] = dv_acc[...].astype(BF16)

    return pl.pallas_call(
        body, out_shape=(jax.ShapeDtypeStruct((S, D), BF16), jax.ShapeDtypeStruct((M, 2 * D), BF16)), grid=(S // tq,),
        in_specs=[pl.BlockSpec((tq, D), lambda i: (i, 0)), pl.BlockSpec((M, D), lambda i: (0, 0)),
                  pl.BlockSpec((M, D), lambda i: (0, 1)), pl.BlockSpec((tq, D), lambda i: (i, 0))],
        out_specs=(pl.BlockSpec((tq, D), lambda i: (i, 0)), pl.BlockSpec((M, 2 * D), lambda i: (0, 0))),
        scratch_shapes=[pltpu.VMEM((M, D), F32), pltpu.VMEM((M, D), F32)],
        compiler_params=_cp(("arbitrary",)), name=name)(xq, kv, kv, do)


def _adamw(g, w, m, v, name, tr=256):
    L, r, c = g.shape
    tr = tr if r % tr == 0 else r
    c1 = 1.0 - ADAM_B1 ** ADAM_STEP
    c2 = 1.0 - ADAM_B2 ** ADAM_STEP

    def body(g_ref, w_ref, m_ref, v_ref, d_ref, nm_ref, nv_ref):
        gv = g_ref[...]
        nm = ADAM_B1 * m_ref[...] + (1.0 - ADAM_B1) * gv
        nv = ADAM_B2 * v_ref[...] + (1.0 - ADAM_B2) * (gv * gv)
        nm_ref[...] = nm
        nv_ref[...] = nv
        d_ref[...] = -ADAM_LR * ((nm / c1) / (jnp.sqrt(nv / c2) + ADAM_EPS) + ADAM_WD * w_ref[...])

    blk = pl.BlockSpec((1, tr, c), lambda l, i: (l, i, 0))
    return pl.pallas_call(
        body, out_shape=(jax.ShapeDtypeStruct(g.shape, F32),) * 3, grid=(L, r // tr),
        in_specs=[blk] * 4, out_specs=(blk,) * 3, compiler_params=_cp(("parallel", "parallel")), name=name)(g, w, m, v)


def _sum_slots(a, out_dtype, name, tr=496):
    n, R, C = a.shape
    tr = tr if R % tr == 0 else R

    def body(a_ref, o_ref):
        acc = a_ref[0].astype(F32)
        for k in range(1, n):
            acc = acc + a_ref[k].astype(F32)
        o_ref[...] = acc.astype(out_dtype)

    return pl.pallas_call(
        body, out_shape=jax.ShapeDtypeStruct((R, C), out_dtype), grid=(R // tr,),
        in_specs=[pl.BlockSpec((n, tr, C), lambda i: (0, i, 0))], out_specs=pl.BlockSpec((tr, C), lambda i: (i, 0)),
        compiler_params=_cp(("parallel",)), name=name)(a)


def _add_pair(a, b, name, tr=496):
    n, R, C = a.shape
    tr = tr if R % tr == 0 else R

    def body(a_ref, b_ref, o_ref):
        o_ref[...] = (a_ref[...].astype(F32) + b_ref[...].astype(F32)).astype(BF16)

    blk = pl.BlockSpec((1, tr, C), lambda k, i: (k, i, 0))
    return pl.pallas_call(
        body, out_shape=jax.ShapeDtypeStruct(a.shape, BF16), grid=(n, R // tr), in_specs=[blk, blk], out_specs=blk,
        compiler_params=_cp(("parallel", "parallel")), name=name)(a, b)


ANY = pl.BlockSpec(memory_space=pl.ANY)


def _place():
    return lax.axis_index("x"), lax.axis_index("y"), lax.axis_index("c")


def _other_chips(x, y):
    return [(1 - x, y), (x, 1 - y), (1 - x, 1 - y)]


def _gather_weights(shard, name):
    R, C = shard.shape
    half = R // 2

    def body(s_ref, o_ref, send_sems, recv_sems, local_sem):
        x, y, c = _place()
        j = 2 * x + y
        mine0 = pl.multiple_of(c * half, 16)
        theirs0 = pl.multiple_of((1 - c) * half, 16)

        def rows(jj, r0):
            return o_ref.at[jj, pl.ds(r0, half), :]

        def copy(k, src, dst, to):
            return pltpu.make_async_remote_copy(src_ref=src, dst_ref=dst, send_sem=send_sems.at[k], recv_sem=recv_sems.at[k],
                                                device_id=to, device_id_type=MESH)

        own = pltpu.make_async_copy(s_ref, o_ref.at[j], local_sem)
        own.start()
        chips = _other_chips(x, y)
        first = [copy(k, s_ref.at[pl.ds(mine0, half), :], rows(j, mine0), (px, py, c)) for k, (px, py) in enumerate(chips)]
        for cp in first:
            cp.start()
        passed = []
        for k, (px, py) in enumerate(chips):
            jj = 2 * px + py
            copy(k, rows(jj, mine0), rows(jj, mine0), (px, py, c)).wait_recv()
            fw = copy(3 + k, rows(jj, mine0), rows(jj, mine0), (x, y, 1 - c))
            fw.start()
            passed.append(fw)
        for k, (px, py) in enumerate(chips):
            jj = 2 * px + py
            copy(3 + k, rows(jj, theirs0), rows(jj, theirs0), (x, y, 1 - c)).wait_recv()
        for cp in first + passed:
            cp.wait_send()
        own.wait()

    return pl.pallas_call(
        body, out_shape=jax.ShapeDtypeStruct((4, R, C), shard.dtype), in_specs=[ANY], out_specs=ANY,
        scratch_shapes=[pltpu.SemaphoreType.DMA((6,)), pltpu.SemaphoreType.DMA((6,)), pltpu.SemaphoreType.DMA],
        name=name)(shard)


def _pair_split(g, name):
    n, R, C = g.shape
    half = R // 2

    def body(g_ref, own_ref, got_ref, send_sem, recv_sem, local_sem):
        x, y, c = _place()
        mine0 = pl.multiple_of(c * half, 16)
        theirs0 = pl.multiple_of((1 - c) * half, 16)
        keep = pltpu.make_async_copy(g_ref.at[:, pl.ds(mine0, half), :], own_ref, local_sem)
        keep.start()
        cp = pltpu.make_async_remote_copy(src_ref=g_ref.at[:, pl.ds(theirs0, half), :], dst_ref=got_ref, send_sem=send_sem,
                                          recv_sem=recv_sem, device_id=(x, y, 1 - c), device_id_type=MESH)
        cp.start()
        cp.wait()
        keep.wait()

    sh = jax.ShapeDtypeStruct((n, half, C), g.dtype)
    return pl.pallas_call(
        body, out_shape=(sh, sh), in_specs=[ANY], out_specs=(ANY, ANY),
        scratch_shapes=[pltpu.SemaphoreType.DMA, pltpu.SemaphoreType.DMA, pltpu.SemaphoreType.DMA], name=name)(g)


def _chip_all_to_all(p, name):
    def body(p_ref, o_ref, send_sems, recv_sems, local_sem):
        x, y, c = _place()
        j = 2 * x + y
        own = pltpu.make_async_copy(p_ref.at[j], o_ref.at[j], local_sem)
        own.start()
        cps = []
        for k, (px, py) in enumerate(_other_chips(x, y)):
            cps.append(pltpu.make_async_remote_copy(src_ref=p_ref.at[2 * px + py], dst_ref=o_ref.at[j], send_sem=send_sems.at[k],
                                                    recv_sem=recv_sems.at[k], device_id=(px, py, c), device_id_type=MESH))
        for cp in cps:
            cp.start()
        for cp in cps:
            cp.wait()
        own.wait()

    return pl.pallas_call(
        body, out_shape=jax.ShapeDtypeStruct(p.shape, p.dtype), in_specs=[ANY], out_specs=ANY,
        scratch_shapes=[pltpu.SemaphoreType.DMA((3,)), pltpu.SemaphoreType.DMA((3,)), pltpu.SemaphoreType.DMA], name=name)(p)


def _pair_gather(t, name):
    def body(t_ref, o_ref, send_sem, recv_sem, local_sem):
        x, y, c = _place()
        own = pltpu.make_async_copy(t_ref, o_ref.at[c], local_sem)
        own.start()
        cp = pltpu.make_async_remote_copy(src_ref=t_ref, dst_ref=o_ref.at[c], send_sem=send_sem, recv_sem=recv_sem,
                                          device_id=(x, y, 1 - c), device_id_type=MESH)
        cp.start()
        cp.wait()
        own.wait()

    return pl.pallas_call(
        body, out_shape=jax.ShapeDtypeStruct((2,) + t.shape, t.dtype), in_specs=[ANY], out_specs=ANY,
        scratch_shapes=[pltpu.SemaphoreType.DMA, pltpu.SemaphoreType.DMA, pltpu.SemaphoreType.DMA], name=name)(t)


def _reduce_scatter(g, tag):
    own, got = _pair_split(g, f"rs_pair_{tag}")
    p = _add_pair(own, got, f"rs_add_{tag}")
    q = _chip_all_to_all(p, f"rs_a2a_{tag}")
    t = _sum_slots(q, F32, f"rs_sum_{tag}")
    both = _pair_gather(t, f"rs_join_{tag}")
    return both.reshape(g.shape[1], g.shape[2])


def _all_reduce_small(v, tag):
    pair = _pair_gather(v, f"ar_pair_{tag}")
    p = _sum_slots(pair, F32, f"ar_add_{tag}")
    q = _chip_all_to_all(jnp.broadcast_to(p[None], (4,) + p.shape), f"ar_a2a_{tag}")
    return _sum_slots(q, F32, f"ar_sum_{tag}")


def _pack_rows(n):
    return -(-n // 16) * 16


def _pack_shards(parts):
    flat = [p.reshape(-1, PACK_COLS) for p in parts]
    flat = [jnp.pad(f, ((0, _pack_rows(f.shape[0]) - f.shape[0]), (0, 0))) for f in flat]
    used = sum(f.shape[0] for f in flat)
    flat.append(jnp.zeros((PACK_ROWS - used, PACK_COLS), flat[0].dtype))
    return jnp.concatenate(flat, axis=0)


def _unpack_shard(buf):
    out, r = [], 0
    for _, shape, axis in BIG:
        ss = _shard_shape(shape, axis)
        n = ss[0] * ss[1] // PACK_COLS
        out.append(buf[r:r + n].reshape(ss))
        r += _pack_rows(n)
    return out


def _unpack_full(gathered):
    per_chip = [_unpack_shard(gathered[j]) for j in range(4)]
    return {name: jnp.concatenate([per_chip[j][i] for j in range(4)], axis=axis) for i, (name, _, axis) in enumerate(BIG)}


def _pack_full(grads):
    slots = []
    for j in range(4):
        parts = []
        for name, shape, axis in BIG:
            n = shape[axis] // 4
            parts.append(lax.slice_in_dim(grads[name], j * n, (j + 1) * n, axis=axis))
        slots.append(_pack_shards(parts))
    return jnp.stack(slots)


def _pad_w_in(w):
    return jnp.concatenate([w[:, 2312:5384], w[:, 256:1792], w[:, 1800:2312], w[:, 0:256], w[:, 1792:1800],
                            jnp.zeros((w.shape[0], NP - N_IN), w.dtype)], axis=1)


def _unpad_w_in(w):
    return jnp.concatenate([w[:, P_A:P_A + 256], w[:, P_Q:P_Q + 1536], w[:, P_F:P_F + 8], w[:, P_C:P_C + 512], w[:, P_G:P_G + 3072]],
                           axis=1)


def _small_prep(sw, l):
    eye = jnp.eye(4, dtype=F32)
    bd = jnp.einsum('gh,gcd->gchd', eye, sw['pool_w'][l]).reshape(POOL_W, POOL_W).astype(BF16)
    tril = jnp.tril(jnp.ones((SGU_CHUNK, SGU_CHUNK), F32))
    wm = (sw['sgu_w'][l] * tril[None]).astype(BF16)
    return dict(
        g_mix=sw['norm_mix_g'][l][None], g_x=sw['norm_xattn_g'][l][None], g_mem=sw['norm_mem_g'][l][None],
        g_ffn=sw['norm_ffn_g'][l][None], bd=bd, pool_scale=sw['pool_scale'][l][None],
        bf=jnp.pad(sw['b_forget'][l], (0, FCOLS - 8))[None], sgu_g=sw['sgu_norm_g'][l][None], wm=wm,
        wmt=jnp.transpose(wm, (0, 2, 1)), sgu_bias=jnp.repeat(sw['sgu_b'][l].T, 64, axis=1), bg=sw['b_gate'][l][None])


def _layer_fwd(x, mem, W, sp, l):
    t = f"l{l}"
    S = x.shape[0]
    h = _rms_fwd(x, sp['g_mix'], f"rms_mix_{t}")
    proj = _mm(h, W['w_in_p'], name=f"proj_{t}", out_dtype=F32)
    d, ya = _pool_fwd(proj, sp['bd'], sp['pool_scale'], f"pool_fwd_{t}")
    fcum = _fgate_fwd(proj, sp['bf'], f"fgate_fwd_{t}")
    f8 = fcum[:, :8]
    fcol = f8.reshape(S, 4, 2).transpose(1, 0, 2)
    frow = f8.T.reshape(4, 2, S)
    qkv = proj[:, P_Q:P_Q + 3 * FOX_W].astype(BF16)
    o, lse = _fox_fwd(qkv, fcol, frow, f"fox_fwd_{t}")
    sg = _sgu_fwd(proj, sp['sgu_g'], sp['wm'], sp['sgu_bias'], f"sgu_fwd_{t}")
    merged = _merge_fwd(proj, ya, o, sg, W['w_branch_a'], W['w_branch_b'], W['w_branch_c'], sp['bg'], f"merge_fwd_{t}")
    x1 = _mm(merged, W['w_out'], name=f"out_{t}", out_dtype=F32, extra=x, epi=lambda r, e: e + r)
    hx = _rms_fwd(x1, sp['g_x'], f"rms_x_{t}")
    hm = _rms_fwd(mem, sp['g_mem'], f"rms_mem_{t}")
    xq = _mm(hx, W['w_xq'], name=f"xq_{t}", out_dtype=BF16)
    kv = _mm(hm, W['w_xkv'], name=f"xkv_{t}", out_dtype=BF16)
    o2 = _xattn_fwd(xq, kv, f"xattn_fwd_{t}")
    x2 = _mm(o2, W['w_xo'], name=f"xo_{t}", out_dtype=F32, extra=x1, epi=lambda r, e: e + r)
    hf = _rms_fwd(x2, sp['g_ffn'], f"rms_ffn_{t}")
    z = _mm(hf, W['w_ff1'], name=f"ff1_{t}", out_dtype=F32)
    x3 = _mm(z, W['w_ff2'], name=f"ff2_{t}", out_dtype=F32, a_fn=_relu2, extra=x2, epi=lambda r, e: e + r)
    saved = dict(x=x, h=h, proj=proj, d=d, ya=ya, fcol=fcol, frow=frow, qkv=qkv, o=o, lse=lse, sg=sg, merged=merged, x1=x1,
                 hx=hx, hm=hm, xq=xq, kv=kv, o2=o2, x2=x2, hf=hf, z=z)
    return x3, saved


def _layer_bwd(dx3, mem, W, sp, sv, l):
    t = f"l{l}"
    S = dx3.shape[0]
    gb, gs = {}, {}
    dz = _mm(dx3, W['w_ff2'], name=f"d_a2_{t}", out_dtype=BF16, tb=True, extra=sv['z'],
             epi=lambda r, e: r * (2.0 * jnp.maximum(e, 0.0)))
    gb['w_ff2'] = _mm(sv['z'], dx3, name=f"dw_ff2_{t}", out_dtype=BF16, ta=True, a_fn=_relu2)
    gb['w_ff1'] = _mm(sv['hf'], dz, name=f"dw_ff1_{t}", out_dtype=BF16, ta=True)
    dhf = _mm(dz, W['w_ff1'], name=f"d_hf_{t}", out_dtype=F32, tb=True)
    dx2, gs['norm_ffn_g'] = _rms_bwd(dhf, sv['x2'], sp['g_ffn'], dx3, f"rms_ffn_bwd_{t}")
    do2 = _mm(dx2, W['w_xo'], name=f"d_o2_{t}", out_dtype=BF16, tb=True)
    gb['w_xo'] = _mm(sv['o2'], dx2, name=f"dw_xo_{t}", out_dtype=BF16, ta=True)
    dxq, dkv = _xattn_bwd(sv['xq'], sv['kv'], do2, f"xattn_bwd_{t}")
    gb['w_xkv'] = _mm(sv['hm'], dkv, name=f"dw_xkv_{t}", out_dtype=BF16, ta=True)
    dhm = _mm(dkv, W['w_xkv'], name=f"d_hm_{t}", out_dtype=F32, tb=True)
    gs['norm_mem_g'] = _rms_bwd(dhm, mem, sp['g_mem'], None, f"rms_mem_bwd_{t}")
    gb['w_xq'] = _mm(sv['hx'], dxq, name=f"dw_xq_{t}", out_dtype=BF16, ta=True)
    dhx = _mm(dxq, W['w_xq'], name=f"d_hx_{t}", out_dtype=F32, tb=True)
    dx1, gs['norm_xattn_g'] = _rms_bwd(dhx, sv['x1'], sp['g_x'], dx2, f"rms_x_bwd_{t}")
    gb['w_out'] = _mm(sv['merged'], dx1, name=f"dw_out_{t}", out_dtype=BF16, ta=True)
    dm = _mm(dx1, W['w_out'], name=f"d_merged_{t}", out_dtype=F32, tb=True)
    dg, dya, do, dsg, gb['w_branch_a'], gb['w_branch_b'], gb['w_branch_c'], gs['b_gate'] = _merge_bwd(
        dm, sv['proj'], sv['ya'], sv['o'], sv['sg'], W['w_branch_a'], W['w_branch_b'], W['w_branch_c'], sp['bg'], f"merge_bwd_{t}")
    dc, dws, dbias, gs['sgu_norm_g'] = _sgu_bwd(dsg, sv['proj'], sp['sgu_g'], sp['wm'], sp['wmt'], sp['sgu_bias'], f"sgu_bwd_{t}")
    tril = jnp.tril(jnp.ones((SGU_CHUNK, SGU_CHUNK), F32))
    gs['sgu_w'] = dws * tril[None]
    gs['sgu_b'] = dbias.reshape(SGU_CHUNK, 4, 64).sum(-1).T
    dq, dk, dv, dfrow = _fox_bwd(sv['qkv'], do, sv['lse'], sv['fcol'], sv['frow'], f"fox_bwd_{t}")
    dF = jnp.pad(dfrow.reshape(8, S).T, ((0, 0), (0, FCOLS - 8)))
    df, dbf = _fgate_bwd(dF, sv['proj'], sp['bf'], f"fgate_bwd_{t}")
    gs['b_forget'] = dbf[:, :8]
    da, dbd, gs['pool_scale'] = _pool_bwd(dya, sv['d'], sp['bd'], sp['pool_scale'], f"pool_bwd_{t}")
    gs['pool_w'] = jnp.stack([dbd[g * 64:(g + 1) * 64, g * 64:(g + 1) * 64] for g in range(4)])
    dproj = jnp.concatenate([dg, dq, dk, dv, dc, da, df], axis=1)
    gb['w_in'] = _unpad_w_in(_mm(sv['h'], dproj, name=f"dw_in_{t}", out_dtype=BF16, ta=True))
    dh = _mm(dproj, W['w_in_p'], name=f"d_h_{t}", out_dtype=F32, tb=True, tk=512)
    dx, gs['norm_mix_g'] = _rms_bwd(dh, sv['x'], sp['g_mix'], dx1, f"rms_mix_bwd_{t}")
    return dx, gb, gs


SMALL_ROWS = 1424


def _pack_small(parts):
    flat = jnp.concatenate([p.reshape(-1) for p in parts])
    return jnp.pad(flat, (0, SMALL_ROWS * 128 - flat.shape[0])).reshape(SMALL_ROWS, 128)


def _unpack_small(buf, shapes):
    flat, out, r = buf.reshape(-1), [], 0
    for s in shapes:
        n = math.prod(s)
        out.append(flat[r:r + n].reshape(s))
        r += n
    return out


def kernel(x, mem, norm_mix_g, w_in, b_forget, pool_w, pool_scale, sgu_norm_g, sgu_w, sgu_b, w_branch_a, w_branch_b, w_branch_c, b_gate, w_out, norm_xattn_g, norm_mem_g, w_xq, w_xkv, w_xo, norm_ffn_g, w_ff1, w_ff2, final_norm_g, loss_target, m_norm_mix_g, m_w_in, m_b_forget, m_pool_w, m_pool_scale, m_sgu_norm_g, m_sgu_w, m_sgu_b, m_w_branch_a, m_w_branch_b, m_w_branch_c, m_b_gate, m_w_out, m_norm_xattn_g, m_norm_mem_g, m_w_xq, m_w_xkv, m_w_xo, m_norm_ffn_g, m_w_ff1, m_w_ff2, m_final_norm_g, v_norm_mix_g, v_w_in, v_b_forget, v_pool_w, v_pool_scale, v_sgu_norm_g, v_sgu_w, v_sgu_b, v_w_branch_a, v_w_branch_b, v_w_branch_c, v_b_gate, v_w_out, v_norm_xattn_g, v_norm_mem_g, v_w_xq, v_w_xkv, v_w_xo, v_norm_ffn_g, v_w_ff1, v_w_ff2, v_final_norm_g):
    args = (norm_mix_g, w_in, b_forget, pool_w, pool_scale, sgu_norm_g, sgu_w, sgu_b, w_branch_a, w_branch_b, w_branch_c, b_gate,
            w_out, norm_xattn_g, norm_mem_g, w_xq, w_xkv, w_xo, norm_ffn_g, w_ff1, w_ff2, final_norm_g)
    margs = (m_norm_mix_g, m_w_in, m_b_forget, m_pool_w, m_pool_scale, m_sgu_norm_g, m_sgu_w, m_sgu_b, m_w_branch_a, m_w_branch_b,
             m_w_branch_c, m_b_gate, m_w_out, m_norm_xattn_g, m_norm_mem_g, m_w_xq, m_w_xkv, m_w_xo, m_norm_ffn_g, m_w_ff1, m_w_ff2,
             m_final_norm_g)
    vargs = (v_norm_mix_g, v_w_in, v_b_forget, v_pool_w, v_pool_scale, v_sgu_norm_g, v_sgu_w, v_sgu_b, v_w_branch_a, v_w_branch_b,
             v_w_branch_c, v_b_gate, v_w_out, v_norm_xattn_g, v_norm_mem_g, v_w_xq, v_w_xkv, v_w_xo, v_norm_ffn_g, v_w_ff1, v_w_ff2,
             v_final_norm_g)
    w = dict(zip(W_NAMES, args))
    mo = dict(zip(W_NAMES, margs))
    vo = dict(zip(W_NAMES, vargs))
    xs, mems, tgt = x[0], mem[0], loss_target[0]

    full = []
    for l in range(DEPTH):
        shard = _pack_shards([w[n][l].astype(BF16) for n in BIG_NAMES])
        Wl = _unpack_full(_gather_weights(shard, f"gather_w_l{l}"))
        Wl['w_in_p'] = _pad_w_in(Wl.pop('w_in'))
        full.append(Wl)
    preps = [_small_prep(w, l) for l in range(DEPTH)]

    act, saved = xs, []
    for l in range(DEPTH):
        act, sv = _layer_fwd(act, mems, full[l], preps[l], l)
        saved.append(sv)
    loss_part, dact, d_final_g = _loss_head(act, w['final_norm_g'][None], tgt, "loss_head")

    big_red, small_g = [None] * DEPTH, [None] * DEPTH
    for l in reversed(range(DEPTH)):
        dact, gb, gs = _layer_bwd(dact, mems, full[l], preps[l], saved[l], l)
        big_red[l] = _unpack_shard(_reduce_scatter(_pack_full(gb), f"l{l}"))
        small_g[l] = gs
    grad_x = dact[None]

    per_layer = [n for n in SMALL_NAMES if n != 'final_norm_g']
    small_shapes = [w[n].shape for n in per_layer] + [(D,), (1,)]
    parts = [jnp.stack([small_g[l][n].reshape(w[n].shape[1:]) for l in range(DEPTH)]) for n in per_layer]
    red = _unpack_small(_all_reduce_small(_pack_small(parts + [d_final_g.reshape(D), loss_part.reshape(1)]), "small"), small_shapes)
    grads = dict(zip(per_layer + ['final_norm_g'], red[:-1]))
    loss = red[-1].reshape(())
    for i, n in enumerate(BIG_NAMES):
        grads[n] = jnp.stack([big_red[l][i] for l in range(DEPTH)])

    delta, new_m, new_v = {}, {}, {}
    for n in BIG_NAMES:
        delta[n], new_m[n], new_v[n] = _adamw(grads[n], w[n], mo[n], vo[n], f"adamw_{n}")
    small_all = per_layer + ['final_norm_g']
    shapes_all = [w[n].shape for n in small_all]
    packed = [_pack_small([d[n] for n in small_all])[None] for d in (grads, w, mo, vo)]
    ds, ms, vs = _adamw(*packed, "adamw_small")
    for n, a, b, c in zip(small_all, _unpack_small(ds[0], shapes_all), _unpack_small(ms[0], shapes_all), _unpack_small(vs[0], shapes_all)):
        delta[n], new_m[n], new_v[n] = a, b, c

    return (loss, grad_x, *[grads[n] for n in W_NAMES], *[delta[n] for n in W_NAMES], *[new_m[n] for n in W_NAMES],
            *[new_v[n] for n in W_NAMES])
```

```python
import functools
import math

import jax
import jax.numpy as jnp
from jax import lax
from jax.experimental import pallas as pl
from jax.experimental.pallas import tpu as pltpu

F32 = jnp.float32
BF16 = jnp.bfloat16

D = 1024
DEPTH = 2
POOL_W = 256
FOX_W = 512
SGU_W = 256
SGU_CHUNK = 128
N_IN = 5384
P_G, P_Q, P_K, P_V, P_C, P_A, P_F = 0, 3072, 3584, 4096, 4608, 5120, 5376
NP = 5632
XH, XHD = 4, 256
D_FF = 4096
EPS = 1e-6
NEG = -1e30
FOX_SCALE = 64 ** -0.5
X_SCALE = 256 ** -0.5
GELU_K = math.sqrt(2.0 / math.pi)
GELU_C = 0.044715

ADAM_LR, ADAM_B1, ADAM_B2, ADAM_EPS, ADAM_WD, ADAM_STEP = 0.001, 0.9, 0.999, 1e-08, 0.01, 10

VMEM_LIMIT = 48 * 1024 * 1024
MESH = pl.DeviceIdType.MESH

IN_NAMES = ['x', 'mem', 'norm_mix_g', 'w_in', 'b_forget', 'pool_w', 'pool_scale', 'sgu_norm_g', 'sgu_w', 'sgu_b',
            'w_branch_a', 'w_branch_b', 'w_branch_c', 'b_gate', 'w_out', 'norm_xattn_g', 'norm_mem_g', 'w_xq',
            'w_xkv', 'w_xo', 'norm_ffn_g', 'w_ff1', 'w_ff2', 'final_norm_g']
W_NAMES = IN_NAMES[2:]
BIG = [('w_in', (D, N_IN), 1), ('w_branch_a', (POOL_W, D), 1), ('w_branch_b', (FOX_W, D), 1),
       ('w_branch_c', (SGU_W, D), 1), ('w_out', (D, D), 0), ('w_xq', (D, D), 0), ('w_xkv', (D, 2 * D), 1),
       ('w_xo', (D, D), 0), ('w_ff1', (D, D_FF), 1), ('w_ff2', (D_FF, D), 0)]
BIG_NAMES = [b[0] for b in BIG]
SMALL_NAMES = [n for n in W_NAMES if n not in BIG_NAMES]
PACK_COLS = 1024
PACK_ROWS = 4960


def _cp(sem=None):
    return pltpu.CompilerParams(dimension_semantics=sem, vmem_limit_bytes=VMEM_LIMIT)


def _shard_shape(shape, axis):
    s = list(shape)
    s[axis] //= 4
    return tuple(s)


def _mm(a, b, *, name, out_dtype, ta=False, tb=False, tm=1024, tn=512, tk=1024, a_fn=None, extra=None, epi=None):
    M = a.shape[1] if ta else a.shape[0]
    K = a.shape[0] if ta else a.shape[1]
    N = b.shape[0] if tb else b.shape[1]
    tm, tn, tk = min(tm, M), min(tn, N), min(tk, K)
    assert M % tm == 0 and N % tn == 0 and K % tk == 0, (name, M, N, K)
    nk = K // tk
    a_spec = pl.BlockSpec((tk, tm), lambda i, j, k: (k, i)) if ta else pl.BlockSpec((tm, tk), lambda i, j, k: (i, k))
    b_spec = pl.BlockSpec((tn, tk), lambda i, j, k: (j, k)) if tb else pl.BlockSpec((tk, tn), lambda i, j, k: (k, j))
    dn = (((0 if ta else 1,), (1 if tb else 0,)), ((), ()))
    o_spec = pl.BlockSpec((tm, tn), lambda i, j, k: (i, j))
    in_specs = [a_spec, b_spec] + ([o_spec] if extra is not None else [])

    def body(*refs):
        if extra is not None:
            a_ref, b_ref, e_ref, o_ref, acc_ref = refs
        else:
            a_ref, b_ref, o_ref, acc_ref = refs
            e_ref = None
        k = pl.program_id(2)

        @pl.when(k == 0)
        def _():
            acc_ref[...] = jnp.zeros_like(acc_ref)

        av = a_ref[...]
        if a_fn is not None:
            av = a_fn(av)
        acc_ref[...] += lax.dot_general(av.astype(BF16), b_ref[...].astype(BF16), dn, preferred_element_type=F32)

        @pl.when(k == nk - 1)
        def _():
            r = acc_ref[...]
            if epi is not None:
                r = epi(r, e_ref[...]) if e_ref is not None else epi(r)
            o_ref[...] = r.astype(out_dtype)

    args = (a, b) + ((extra,) if extra is not None else ())
    return pl.pallas_call(
        body, out_shape=jax.ShapeDtypeStruct((M, N), out_dtype), grid=(M // tm, N // tn, nk),
        in_specs=in_specs, out_specs=o_spec, scratch_shapes=[pltpu.VMEM((tm, tn), F32)],
        compiler_params=_cp(("parallel", "parallel", "arbitrary")), name=name)(*args)


def _relu2(z):
    r = jnp.maximum(z, 0.0)
    return r * r


def _rms_fwd(x, g, name, tr=256):
    R, n = x.shape
    tr = min(tr, R)

    def body(x_ref, g_ref, h_ref):
        xv = x_ref[...]
        rstd = lax.rsqrt(jnp.mean(xv * xv, axis=-1, keepdims=True) + EPS)
        h_ref[...] = (xv * rstd * g_ref[...]).astype(BF16)

    return pl.pallas_call(
        body, out_shape=jax.ShapeDtypeStruct((R, n), BF16), grid=(R // tr,),
        in_specs=[pl.BlockSpec((tr, n), lambda i: (i, 0)), pl.BlockSpec((1, n), lambda i: (0, 0))],
        out_specs=pl.BlockSpec((tr, n), lambda i: (i, 0)), compiler_params=_cp(("parallel",)), name=name)(x, g)


def _rms_bwd(dh, x, g, dres, name, tr=256):
    R, n = x.shape
    tr = min(tr, R)
    need_dx = dres is not None

    def body(*refs):
        if need_dx:
            dh_ref, x_ref, g_ref, r_ref, dx_ref, dg_ref = refs
        else:
            dh_ref, x_ref, g_ref, dg_ref = refs
        i = pl.program_id(0)
        xv = x_ref[...]
        dhv = dh_ref[...].astype(F32)
        rstd = lax.rsqrt(jnp.mean(xv * xv, axis=-1, keepdims=True) + EPS)
        xhat = xv * rstd

        @pl.when(i == 0)
        def _():
            dg_ref[...] = jnp.zeros_like(dg_ref)

        dg_ref[...] += jnp.sum(dhv * xhat, axis=0, keepdims=True)
        if need_dx:
            t = dhv * g_ref[...]
            dx_ref[...] = r_ref[...] + rstd * (t - xhat * jnp.mean(t * xhat, axis=-1, keepdims=True))

    row = pl.BlockSpec((tr, n), lambda i: (i, 0))
    vec = pl.BlockSpec((1, n), lambda i: (0, 0))
    if need_dx:
        return pl.pallas_call(
            body, out_shape=(jax.ShapeDtypeStruct((R, n), F32), jax.ShapeDtypeStruct((1, n), F32)), grid=(R // tr,),
            in_specs=[row, row, vec, row], out_specs=(row, vec), compiler_params=_cp(("arbitrary",)), name=name)(dh, x, g, dres)
    return pl.pallas_call(
        body, out_shape=jax.ShapeDtypeStruct((1, n), F32), grid=(R // tr,),
        in_specs=[row, row, vec], out_specs=vec, compiler_params=_cp(("arbitrary",)), name=name)(dh, x, g)


def _loss_head(x, g, tgt, name, tr=256):
    R, n = x.shape

    def body(x_ref, g_ref, t_ref, loss_ref, dx_ref, dg_ref):
        i = pl.program_id(0)
        xv = x_ref[...]
        gv = g_ref[...]
        rstd = lax.rsqrt(jnp.mean(xv * xv, axis=-1, keepdims=True) + EPS)
        xhat = xv * rstd
        e = xhat * gv - t_ref[...]

        @pl.when(i == 0)
        def _():
            loss_ref[...] = jnp.zeros_like(loss_ref)
            dg_ref[...] = jnp.zeros_like(dg_ref)

        loss_ref[...] += 0.5 * jnp.sum(jnp.sum(e * e, axis=-1, keepdims=True) / n, axis=0, keepdims=True)
        dy = e / n
        dg_ref[...] += jnp.sum(dy * xhat, axis=0, keepdims=True)
        t = dy * gv
        dx_ref[...] = rstd * (t - xhat * jnp.mean(t * xhat, axis=-1, keepdims=True))

    row = pl.BlockSpec((tr, n), lambda i: (i, 0))
    vec = pl.BlockSpec((1, n), lambda i: (0, 0))
    one = pl.BlockSpec((1, 1), lambda i: (0, 0))
    return pl.pallas_call(
        body, out_shape=(jax.ShapeDtypeStruct((1, 1), F32), jax.ShapeDtypeStruct((R, n), F32), jax.ShapeDtypeStruct((1, n), F32)),
        grid=(R // tr,), in_specs=[row, vec, row], out_specs=(one, row, vec),
        compiler_params=_cp(("arbitrary",)), name=name)(x, g, tgt)


def _pool_masks(S):
    row = lax.broadcasted_iota(jnp.int32, (S, POOL_W), 0)
    grp = lax.broadcasted_iota(jnp.int32, (S, POOL_W), 1) // 64
    win = jnp.where(grp == 0, 2, jnp.where(grp == 1, 4, jnp.where(grp == 2, 8, 16)))
    cnt = jnp.minimum(row + 1, win).astype(F32)
    return row, grp, cnt


def _by_group(grp, v0, v1, v2, v3):
    return jnp.where(grp == 0, v0, jnp.where(grp == 1, v1, jnp.where(grp == 2, v2, v3)))


def _pool_fwd(proj, bd, scale, name):
    S = proj.shape[0]

    def body(a_ref, bd_ref, sc_ref, d_ref, y_ref):
        a = a_ref[...]
        row, grp, cnt = _pool_masks(S)

        def back(v, k):
            return jnp.where(row >= k, pltpu.roll(v, k, 0), 0.0)

        s1 = a + back(a, 1)
        s2 = s1 + back(s1, 2)
        s3 = s2 + back(s2, 4)
        s4 = s3 + back(s3, 8)
        d = (_by_group(grp, s1, s2, s3, s4) / cnt - a).astype(BF16)
        d_ref[...] = d
        y_ref[...] = (jnp.dot(d, bd_ref[...], preferred_element_type=F32) * sc_ref[...]).astype(BF16)

    full = lambda r, c: pl.BlockSpec((r, c), lambda i: (0, 0))
    return pl.pallas_call(
        body, out_shape=(jax.ShapeDtypeStruct((S, POOL_W), BF16), jax.ShapeDtypeStruct((S, POOL_W), BF16)), grid=(1,),
        in_specs=[pl.BlockSpec((S, POOL_W), lambda i: (0, P_A // POOL_W)), full(POOL_W, POOL_W), full(1, POOL_W)],
        out_specs=(full(S, POOL_W), full(S, POOL_W)), compiler_params=_cp(("arbitrary",)), name=name)(proj, bd, scale)


def _pool_bwd(dya, d, bd, scale, name):
    S = dya.shape[0]

    def body(dy_ref, d_ref, bd_ref, sc_ref, da_ref, dbd_ref, dsc_ref):
        dy = dy_ref[...]
        dv = d_ref[...]
        bdv = bd_ref[...]
        row, grp, cnt = _pool_masks(S)
        yraw = jnp.dot(dv, bdv, preferred_element_type=F32)
        dsc_ref[...] = jnp.sum(dy * yraw, axis=0, keepdims=True)
        tb = (dy * sc_ref[...]).astype(BF16)
        dbd_ref[...] = lax.dot_general(dv, tb, (((0,), (0,)), ((), ())), preferred_element_type=F32)
        dd = lax.dot_general(tb, bdv, (((1,), (1,)), ((), ())), preferred_element_type=F32)
        e = dd / cnt

        def fwd(v, k):
            return jnp.where(row < S - k, pltpu.roll(v, S - k, 0), 0.0)

        r1 = e + fwd(e, 1)
        r2 = r1 + fwd(r1, 2)
        r3 = r2 + fwd(r2, 4)
        r4 = r3 + fwd(r3, 8)
        da_ref[...] = (_by_group(grp, r1, r2, r3, r4) - dd).astype(BF16)

    full = lambda r, c: pl.BlockSpec((r, c), lambda i: (0, 0))
    return pl.pallas_call(
        body, out_shape=(jax.ShapeDtypeStruct((S, POOL_W), BF16), jax.ShapeDtypeStruct((POOL_W, POOL_W), F32),
                         jax.ShapeDtypeStruct((1, POOL_W), F32)), grid=(1,),
        in_specs=[full(S, POOL_W), full(S, POOL_W), full(POOL_W, POOL_W), full(1, POOL_W)],
        out_specs=(full(S, POOL_W), full(POOL_W, POOL_W), full(1, POOL_W)),
        compiler_params=_cp(("arbitrary",)), name=name)(dya, d, bd, scale)


FCOLS = 128


def _log_sigmoid(z):
    return -(jnp.maximum(-z, 0.0) + jnp.log1p(jnp.exp(-jnp.abs(z))))


def _fgate_fwd(proj, bf, name):
    S = proj.shape[0]

    def body(f_ref, b_ref, o_ref):
        v = _log_sigmoid(f_ref[...] + b_ref[...])
        row = lax.broadcasted_iota(jnp.int32, (S, FCOLS), 0)
        k = 1
        while k < S:
            v = v + jnp.where(row >= k, pltpu.roll(v, k, 0), 0.0)
            k *= 2
        o_ref[...] = v

    return pl.pallas_call(
        body, out_shape=jax.ShapeDtypeStruct((S, FCOLS), F32), grid=(1,),
        in_specs=[pl.BlockSpec((S, FCOLS), lambda i: (0, P_F // FCOLS)), pl.BlockSpec((1, FCOLS), lambda i: (0, 0))],
        out_specs=pl.BlockSpec((S, FCOLS), lambda i: (0, 0)), compiler_params=_cp(("arbitrary",)), name=name)(proj, bf)


def _fgate_bwd(dF, proj, bf, name):
    S = proj.shape[0]

    def body(dF_ref, f_ref, b_ref, df_ref, db_ref):
        v = dF_ref[...]
        row = lax.broadcasted_iota(jnp.int32, (S, FCOLS), 0)
        k = 1
        while k < S:
            v = v + jnp.where(row < S - k, pltpu.roll(v, S - k, 0), 0.0)
            k *= 2
        z = f_ref[...] + b_ref[...]
        df = v * (1.0 / (1.0 + jnp.exp(z)))
        db_ref[...] = jnp.sum(df, axis=0, keepdims=True)
        df_ref[...] = jnp.concatenate([df, jnp.zeros_like(df)], axis=1).astype(BF16)

    return pl.pallas_call(
        body, out_shape=(jax.ShapeDtypeStruct((S, 2 * FCOLS), BF16), jax.ShapeDtypeStruct((1, FCOLS), F32)), grid=(1,),
        in_specs=[pl.BlockSpec((S, FCOLS), lambda i: (0, 0)), pl.BlockSpec((S, FCOLS), lambda i: (0, P_F // FCOLS)),
                  pl.BlockSpec((1, FCOLS), lambda i: (0, 0))],
        out_specs=(pl.BlockSpec((S, 2 * FCOLS), lambda i: (0, 0)), pl.BlockSpec((1, FCOLS), lambda i: (0, 0))),
        compiler_params=_cp(("arbitrary",)), name=name)(dF, proj, bf)


def _fox_scores(qe, kj, fq, fk, r0, c0, tq, tk):
    s = lax.dot_general(qe, kj, (((1,), (1,)), ((), ())), preferred_element_type=F32) * FOX_SCALE
    s = s + (fq - fk)
    rows = r0 + lax.broadcasted_iota(jnp.int32, (tq, tk), 0)
    cols = c0 + lax.broadcasted_iota(jnp.int32, (tq, tk), 1)
    return jnp.where(rows >= cols, s, NEG)


def _fox_fwd(qkv, fcol, frow, name, tq=256):
    S = qkv.shape[0]
    tk = tq

    def body(q_ref, k_ref, v_ref, fc_ref, fr_ref, o_ref, lse_ref):
        i = pl.program_id(1)
        r0 = i * tq
        q = q_ref[...]
        half = lax.broadcasted_iota(jnp.int32, (tq, 128), 1) // 64
        outs = []
        for e in (0, 1):
            qe = jnp.where(half == e, q, jnp.zeros_like(q))
            fq = fc_ref[0, :, e:e + 1]

            def step(j, carry, qe=qe, fq=fq, e=e):
                m, l, acc = carry
                c0 = pl.multiple_of(j * tk, tk)
                kj = k_ref[pl.ds(c0, tk), :]
                vj = v_ref[pl.ds(c0, tk), :]
                fk = fr_ref[0, e:e + 1, pl.ds(c0, tk)]
                s = _fox_scores(qe, kj, fq, fk, r0, c0, tq, tk)
                m_new = jnp.maximum(m, jnp.max(s, axis=-1, keepdims=True))
                alpha = jnp.exp(m - m_new)
                p = jnp.exp(s - m_new)
                l = alpha * l + jnp.sum(p, axis=-1, keepdims=True)
                acc = alpha * acc + jnp.dot(p.astype(BF16), vj, preferred_element_type=F32)
                return m_new, l, acc

            m, l, acc = lax.fori_loop(0, i + 1, step, (jnp.full((tq, 1), NEG, F32), jnp.zeros((tq, 1), F32),
                                                      jnp.zeros((tq, 128), F32)))
            outs.append(acc / l)
            lse_ref[0, :, e:e + 1] = m + jnp.log(l)
        o_ref[...] = jnp.where(half == 0, outs[0], outs[1]).astype(BF16)

    return pl.pallas_call(
        body, out_shape=(jax.ShapeDtypeStruct((S, FOX_W), BF16), jax.ShapeDtypeStruct((4, S, 2), F32)), grid=(4, S // tq),
        in_specs=[pl.BlockSpec((tq, 128), lambda h, i: (i, h)), pl.BlockSpec((S, 128), lambda h, i: (0, 4 + h)),
                  pl.BlockSpec((S, 128), lambda h, i: (0, 8 + h)), pl.BlockSpec((1, tq, 2), lambda h, i: (h, i, 0)),
                  pl.BlockSpec((1, 2, S), lambda h, i: (h, 0, 0))],
        out_specs=(pl.BlockSpec((tq, 128), lambda h, i: (i, h)), pl.BlockSpec((1, tq, 2), lambda h, i: (h, i, 0))),
        compiler_params=_cp(("parallel", "parallel")), name=name)(qkv, qkv, qkv, fcol, frow)


def _fox_bwd(qkv, do, lse, fcol, frow, name, tq=256):
    S = qkv.shape[0]
    tk = tq
    nq = S // tq

    def body(q_ref, k_ref, v_ref, do_ref, lse_ref, fc_ref, fr_ref, dq_ref, dk_ref, dv_ref, dfr_ref, dk_acc, dv_acc):
        dk_acc[...] = jnp.zeros_like(dk_acc)
        dv_acc[...] = jnp.zeros_like(dv_acc)
        dfr_ref[...] = jnp.zeros_like(dfr_ref)
        half = lax.broadcasted_iota(jnp.int32, (tq, 128), 1) // 64

        def q_block(i, _):
            r0 = pl.multiple_of(i * tq, tq)
            qi = q_ref[pl.ds(r0, tq), :]
            dob = do_ref[pl.ds(r0, tq), :].astype(BF16)
            dq_tot = jnp.zeros((tq, 128), F32)
            for e in (0, 1):
                qe = jnp.where(half == e, qi, jnp.zeros_like(qi))
                doe = jnp.where(half == e, dob, jnp.zeros_like(dob))
                lse_e = lse_ref[0, pl.ds(r0, tq), e:e + 1]
                fq = fc_ref[0, pl.ds(r0, tq), e:e + 1]

                def probs(j, qe=qe, doe=doe, lse_e=lse_e, fq=fq, e=e):
                    c0 = pl.multiple_of(j * tk, tk)
                    kj = k_ref[pl.ds(c0, tk), :]
                    vj = v_ref[pl.ds(c0, tk), :]
                    fk = fr_ref[0, e:e + 1, pl.ds(c0, tk)]
                    p = jnp.exp(_fox_scores(qe, kj, fq, fk, r0, c0, tq, tk) - lse_e)
                    dp = lax.dot_general(doe, vj, (((1,), (1,)), ((), ())), preferred_element_type=F32)
                    return c0, kj, p, dp

                def row_term(j, acc, probs=probs):
                    _, _, p, dp = probs(j)
                    return acc + jnp.sum(p * dp, axis=-1, keepdims=True)

                delta = lax.fori_loop(0, i + 1, row_term, jnp.zeros((tq, 1), F32))

                def step(j, dq, probs=probs, delta=delta, e=e):
                    c0, kj, p, dp = probs(j)
                    ds = p * (dp - delta)
                    dfr_ref[0, e:e + 1, pl.ds(c0, tk)] -= jnp.sum(ds, axis=0, keepdims=True)
                    dsb = (ds * FOX_SCALE).astype(BF16)
                    dkc = lax.dot_general(dsb, qi, (((0,), (0,)), ((), ())), preferred_element_type=F32)
                    dvc = lax.dot_general(p.astype(BF16), dob, (((0,), (0,)), ((), ())), preferred_element_type=F32)
                    dk_acc[pl.ds(c0, tk), :] += jnp.where(half == e, dkc, 0.0)
                    dv_acc[pl.ds(c0, tk), :] += jnp.where(half == e, dvc, 0.0)
                    return dq + jnp.dot(dsb, kj, preferred_element_type=F32)

                dq_e = lax.fori_loop(0, i + 1, step, jnp.zeros((tq, 128), F32))
                dq_tot = dq_tot + jnp.where(half == e, dq_e, 0.0)
            dq_ref[pl.ds(r0, tq), :] = dq_tot.astype(BF16)
            return 0

        lax.fori_loop(0, nq, q_block, 0)
        dk_ref[...] = dk_acc[...].astype(BF16)
        dv_ref[...] = dv_acc[...].astype(BF16)

    col = lambda off: pl.BlockSpec((S, 128), lambda h: (0, off + h))
    hs2 = pl.BlockSpec((1, S, 2), lambda h: (h, 0, 0))
    h2s = pl.BlockSpec((1, 2, S), lambda h: (h, 0, 0))
    return pl.pallas_call(
        body, out_shape=(jax.ShapeDtypeStruct((S, FOX_W), BF16),) * 3 + (jax.ShapeDtypeStruct((4, 2, S), F32),), grid=(4,),
        in_specs=[col(0), col(4), col(8), col(0), hs2, hs2, h2s],
        out_specs=(col(0), col(0), col(0), h2s),
        scratch_shapes=[pltpu.VMEM((S, 128), F32), pltpu.VMEM((S, 128), F32)],
        compiler_params=_cp(("parallel",)), name=name)(qkv, qkv, qkv, do, lse, fcol, frow)


def _gelu(x):
    return 0.5 * x * (1.0 + jnp.tanh(GELU_K * (x + GELU_C * x * x * x)))


def _gelu_grad(x):
    th = jnp.tanh(GELU_K * (x + GELU_C * x * x * x))
    return 0.5 * (1.0 + th) + 0.5 * x * (1.0 - th * th) * GELU_K * (1.0 + 3.0 * GELU_C * x * x)


def _sgu_parts(c, gn, w_ref, bias):
    zc = _gelu(c)
    u, vv = zc[:, :SGU_W], zc[:, SGU_W:]
    rstd = lax.rsqrt(jnp.mean(vv * vv, axis=-1, keepdims=True) + EPS)
    vhat = vv * rstd
    vnb = (vhat * gn).astype(BF16)
    grp = lax.broadcasted_iota(jnp.int32, (SGU_CHUNK, SGU_W), 1) // 64
    mixed = bias
    for gi in range(4):
        mixed = mixed + jnp.where(grp == gi, jnp.dot(w_ref[gi], vnb, preferred_element_type=F32), 0.0)
    return u, rstd, vhat, vnb, grp, mixed


def _sgu_fwd(proj, gn, wm, bias, name):
    S = proj.shape[0]

    def body(c_ref, g_ref, w_ref, b_ref, o_ref):
        u, _, _, _, _, mixed = _sgu_parts(c_ref[...], g_ref[...], w_ref, b_ref[...])
        o_ref[...] = (u * mixed).astype(BF16)

    return pl.pallas_call(
        body, out_shape=jax.ShapeDtypeStruct((S, SGU_W), BF16), grid=(S // SGU_CHUNK,),
        in_specs=[pl.BlockSpec((SGU_CHUNK, 2 * SGU_W), lambda i: (i, P_C // (2 * SGU_W))),
                  pl.BlockSpec((1, SGU_W), lambda i: (0, 0)), pl.BlockSpec((4, SGU_CHUNK, SGU_CHUNK), lambda i: (0, 0, 0)),
                  pl.BlockSpec((SGU_CHUNK, SGU_W), lambda i: (0, 0))],
        out_specs=pl.BlockSpec((SGU_CHUNK, SGU_W), lambda i: (i, 0)),
        compiler_params=_cp(("parallel",)), name=name)(proj, gn, wm, bias)


def _sgu_bwd(dsg, proj, gn, wm, wmt, bias, name):
    S = proj.shape[0]

    def body(dsg_ref, c_ref, g_ref, w_ref, wt_ref, b_ref, dc_ref, dw_ref, db_ref, dg_ref):
        i = pl.program_id(0)

        @pl.when(i == 0)
        def _():
            dw_ref[...] = jnp.zeros_like(dw_ref)
            db_ref[...] = jnp.zeros_like(db_ref)
            dg_ref[...] = jnp.zeros_like(dg_ref)

        c = c_ref[...]
        gn_v = g_ref[...]
        u, rstd, vhat, vnb, grp, mixed = _sgu_parts(c, gn_v, w_ref, b_ref[...])
        dsg_v = dsg_ref[...]
        du = dsg_v * mixed
        dmix = dsg_v * u
        db_ref[...] += dmix
        dmb = dmix.astype(BF16)
        dvn = jnp.zeros((SGU_CHUNK, SGU_W), F32)
        for gi in range(4):
            dmg = jnp.where(grp == gi, dmb, jnp.zeros_like(dmb))
            dw_ref[gi] += lax.dot_general(dmg, vnb, (((1,), (1,)), ((), ())), preferred_element_type=F32)
            dvn = dvn + jnp.where(grp == gi, jnp.dot(wt_ref[gi], dmb, preferred_element_type=F32), 0.0)
        dg_ref[...] += jnp.sum(dvn * vhat, axis=0, keepdims=True)
        t = dvn * gn_v
        dvv = rstd * (t - vhat * jnp.mean(t * vhat, axis=-1, keepdims=True))
        dc_ref[...] = (jnp.concatenate([du, dvv], axis=1) * _gelu_grad(c)).astype(BF16)

    w_spec = pl.BlockSpec((4, SGU_CHUNK, SGU_CHUNK), lambda i: (0, 0, 0))
    tile = pl.BlockSpec((SGU_CHUNK, SGU_W), lambda i: (0, 0))
    vec = pl.BlockSpec((1, SGU_W), lambda i: (0, 0))
    return pl.pallas_call(
        body, out_shape=(jax.ShapeDtypeStruct((S, 2 * SGU_W), BF16), jax.ShapeDtypeStruct((4, SGU_CHUNK, SGU_CHUNK), F32),
                         jax.ShapeDtypeStruct((SGU_CHUNK, SGU_W), F32), jax.ShapeDtypeStruct((1, SGU_W), F32)),
        grid=(S // SGU_CHUNK,),
        in_specs=[pl.BlockSpec((SGU_CHUNK, SGU_W), lambda i: (i, 0)),
                  pl.BlockSpec((SGU_CHUNK, 2 * SGU_W), lambda i: (i, P_C // (2 * SGU_W))), vec, w_spec, w_spec, tile],
        out_specs=(pl.BlockSpec((SGU_CHUNK, 2 * SGU_W), lambda i: (i, 0)), w_spec, tile, vec),
        compiler_params=_cp(("arbitrary",)), name=name)(dsg, proj, gn, wm, wmt, bias)


def _sigmoid(z):
    return 1.0 / (1.0 + jnp.exp(-z))


def _merge_specs(tm):
    row = lambda n: pl.BlockSpec((tm, n), lambda i: (i, 0))
    gate = lambda b: pl.BlockSpec((tm, D), lambda i: (i, b))
    full = lambda r, c: pl.BlockSpec((r, c), lambda i: (0, 0))
    return row, gate, full


def _merge_fwd(proj, ya, o, sg, wa, wb, wc, bg, name, tm=256):
    S = proj.shape[0]
    row, gate, full = _merge_specs(tm)

    def body(g0, g1, g2, ya_ref, o_ref, sg_ref, wa_ref, wb_ref, wc_ref, bg_ref, out_ref):
        acc = jnp.zeros((tm, D), F32)
        for b, (g_ref, br_ref, w_ref) in enumerate(((g0, ya_ref, wa_ref), (g1, o_ref, wb_ref), (g2, sg_ref, wc_ref))):
            y = jnp.dot(br_ref[...], w_ref[...], preferred_element_type=F32)
            acc = acc + _sigmoid(g_ref[...] + bg_ref[:, b * D:(b + 1) * D]) * y
        out_ref[...] = acc.astype(BF16)

    return pl.pallas_call(
        body, out_shape=jax.ShapeDtypeStruct((S, D), BF16), grid=(S // tm,),
        in_specs=[gate(0), gate(1), gate(2), row(POOL_W), row(FOX_W), row(SGU_W), full(POOL_W, D), full(FOX_W, D),
                  full(SGU_W, D), full(1, 3 * D)],
        out_specs=row(D), compiler_params=_cp(("parallel",)), name=name)(proj, proj, proj, ya, o, sg, wa, wb, wc, bg)


def _merge_bwd(dm, proj, ya, o, sg, wa, wb, wc, bg, name, tm=256):
    S = proj.shape[0]
    row, gate, full = _merge_specs(tm)

    def body(dm_ref, g0, g1, g2, ya_ref, o_ref, sg_ref, wa_ref, wb_ref, wc_ref, bg_ref,
             dg_ref, dya_ref, do_ref, dsg_ref, dwa_ref, dwb_ref, dwc_ref, dbg_ref, awa, awb, awc):
        i = pl.program_id(0)

        @pl.when(i == 0)
        def _():
            awa[...] = jnp.zeros_like(awa)
            awb[...] = jnp.zeros_like(awb)
            awc[...] = jnp.zeros_like(awc)
            dbg_ref[...] = jnp.zeros_like(dbg_ref)

        dmv = dm_ref[...]
        for b, (g_ref, br_ref, w_ref, dbr_ref, acc_ref) in enumerate(
                ((g0, ya_ref, wa_ref, dya_ref, awa), (g1, o_ref, wb_ref, do_ref, awb), (g2, sg_ref, wc_ref, dsg_ref, awc))):
            br = br_ref[...]
            wv = w_ref[...]
            y = jnp.dot(br, wv, preferred_element_type=F32)
            gt = _sigmoid(g_ref[...] + bg_ref[:, b * D:(b + 1) * D])
            dgp = dmv * y * gt * (1.0 - gt)
            dg_ref[:, b * D:(b + 1) * D] = dgp.astype(BF16)
            dbg_ref[:, b * D:(b + 1) * D] += jnp.sum(dgp, axis=0, keepdims=True)
            dy = (dmv * gt).astype(BF16)
            dbr_ref[...] = lax.dot_general(dy, wv, (((1,), (1,)), ((), ())), preferred_element_type=F32)
            acc_ref[...] += lax.dot_general(br, dy, (((0,), (0,)), ((), ())), preferred_element_type=F32)

        @pl.when(i == pl.num_programs(0) - 1)
        def _():
            dwa_ref[...] = awa[...].astype(BF16)
            dwb_ref[...] = awb[...].astype(BF16)
            dwc_ref[...] = awc[...].astype(BF16)

    return pl.pallas_call(
        body, out_shape=(jax.ShapeDtypeStruct((S, 3 * D), BF16), jax.ShapeDtypeStruct((S, POOL_W), F32),
                         jax.ShapeDtypeStruct((S, FOX_W), F32), jax.ShapeDtypeStruct((S, SGU_W), F32),
                         jax.ShapeDtypeStruct((POOL_W, D), BF16), jax.ShapeDtypeStruct((FOX_W, D), BF16),
                         jax.ShapeDtypeStruct((SGU_W, D), BF16), jax.ShapeDtypeStruct((1, 3 * D), F32)),
        grid=(S // tm,),
        in_specs=[row(D), gate(0), gate(1), gate(2), row(POOL_W), row(FOX_W), row(SGU_W), full(POOL_W, D), full(FOX_W, D),
                  full(SGU_W, D), full(1, 3 * D)],
        out_specs=(row(3 * D), row(POOL_W), row(FOX_W), row(SGU_W), full(POOL_W, D), full(FOX_W, D), full(SGU_W, D),
                   full(1, 3 * D)),
        scratch_shapes=[pltpu.VMEM((POOL_W, D), F32), pltpu.VMEM((FOX_W, D), F32), pltpu.VMEM((SGU_W, D), F32)],
        compiler_params=_cp(("arbitrary",)), name=name)(dm, proj, proj, proj, ya, o, sg, wa, wb, wc, bg)


def _xattn_probs(qh, kh):
    s = lax.dot_general(qh, kh, (((1,), (1,)), ((), ())), preferred_element_type=F32) * X_SCALE
    p = jnp.exp(s - jnp.max(s, axis=-1, keepdims=True))
    return p / jnp.sum(p, axis=-1, keepdims=True)


def _xattn_fwd(xq, kv, name, tq=256):
    S = xq.shape[0]
    M = kv.shape[0]

    def body(q_ref, k_ref, v_ref, o_ref):
        for h in range(XH):
            sl = slice(h * XHD, (h + 1) * XHD)
            p = _xattn_probs(q_ref[:, sl], k_ref[:, sl])
            o_ref[:, sl] = jnp.dot(p.astype(BF16), v_ref[:, sl], preferred_element_type=F32).astype(BF16)

    return pl.pallas_call(
        body, out_shape=jax.ShapeDtypeStruct((S, D), BF16), grid=(S // tq,),
        in_specs=[pl.BlockSpec((tq, D), lambda i: (i, 0)), pl.BlockSpec((M, D), lambda i: (0, 0)),
                  pl.BlockSpec((M, D), lambda i: (0, 1))],
        out_specs=pl.BlockSpec((tq, D), lambda i: (i, 0)), compiler_params=_cp(("parallel",)), name=name)(xq, kv, kv)


def _xattn_bwd(xq, kv, do, name, tq=256):
    S = xq.shape[0]
    M = kv.shape[0]

    def body(q_ref, k_ref, v_ref, do_ref, dq_ref, dkv_ref, dk_acc, dv_acc):
        i = pl.program_id(0)

        @pl.when(i == 0)
        def _():
            dk_acc[...] = jnp.zeros_like(dk_acc)
            dv_acc[...] = jnp.zeros_like(dv_acc)

        for h in range(XH):
            sl = slice(h * XHD, (h + 1) * XHD)
            qh, kh, vh, doh = q_ref[:, sl], k_ref[:, sl], v_ref[:, sl], do_ref[:, sl]
            p = _xattn_probs(qh, kh)
            dp = lax.dot_general(doh, vh, (((1,), (1,)), ((), ())), preferred_element_type=F32)
            ds = p * (dp - jnp.sum(p * dp, axis=-1, keepdims=True))
            dsb = (ds * X_SCALE).astype(BF16)
            dq_ref[:, sl] = jnp.dot(dsb, kh, preferred_element_type=F32).astype(BF16)
            dk_acc[:, sl] += lax.dot_general(dsb, qh, (((0,), (0,)), ((), ())), preferred_element_type=F32)
            dv_acc[:, sl] += lax.dot_general(p.astype(BF16), doh, (((0,), (0,)), ((), ())), preferred_element_type=F32)

        @pl.when(i == pl.num_programs(0) - 1)
        def _():
            dkv_ref[:, :D] = dk_acc[...].astype(BF16)
            dkv_ref[:, D:] = dv_acc[...].astype(BF16)

    return pl.pallas_call(
        body, out_shape=(jax.ShapeDtypeStruct((S, D), BF16), jax.ShapeDtypeStruct((M, 2 * D), BF16)), grid=(S // tq,),
        in_specs=[pl.BlockSpec((tq, D), lambda i: (i, 0)), pl.BlockSpec((M, D), lambda i: (0, 0)),
                  pl.BlockSpec((M, D), lambda i: (0, 1)), pl.BlockSpec((tq, D), lambda i: (i, 0))],
        out_specs=(pl.BlockSpec((tq, D), lambda i: (i, 0)), pl.BlockSpec((M, 2 * D), lambda i: (0, 0))),
        scratch_shapes=[pltpu.VMEM((M, D), F32), pltpu.VMEM((M, D), F32)],
        compiler_params=_cp(("arbitrary",)), name=name)(xq, kv, kv, do)


def _adamw(g, w, m, v, name, tr=256):
    L, r, c = g.shape
    tr = tr if r % tr == 0 else r
    c1 = 1.0 - ADAM_B1 ** ADAM_STEP
    c2 = 1.0 - ADAM_B2 ** ADAM_STEP

    def body(g_ref, w_ref, m_ref, v_ref, d_ref, nm_ref, nv_ref):
        gv = g_ref[...]
        nm = ADAM_B1 * m_ref[...] + (1.0 - ADAM_B1) * gv
        nv = ADAM_B2 * v_ref[...] + (1.0 - ADAM_B2) * (gv * gv)
        nm_ref[...] = nm
        nv_ref[...] = nv
        d_ref[...] = -ADAM_LR * ((nm / c1) / (jnp.sqrt(nv / c2) + ADAM_EPS) + ADAM_WD * w_ref[...])

    blk = pl.BlockSpec((1, tr, c), lambda l, i: (l, i, 0))
    return pl.pallas_call(
        body, out_shape=(jax.ShapeDtypeStruct(g.shape, F32),) * 3, grid=(L, r // tr),
        in_specs=[blk] * 4, out_specs=(blk,) * 3, compiler_params=_cp(("parallel", "parallel")), name=name)(g, w, m, v)


def _sum_slots(a, out_dtype, name, tr=496):
    n, R, C = a.shape
    tr = tr if R % tr == 0 else R

    def body(a_ref, o_ref):
        acc = a_ref[0].astype(F32)
        for k in range(1, n):
            acc = acc + a_ref[k].astype(F32)
        o_ref[...] = acc.astype(out_dtype)

    return pl.pallas_call(
        body, out_shape=jax.ShapeDtypeStruct((R, C), out_dtype), grid=(R // tr,),
        in_specs=[pl.BlockSpec((n, tr, C), lambda i: (0, i, 0))], out_specs=pl.BlockSpec((tr, C), lambda i: (i, 0)),
        compiler_params=_cp(("parallel",)), name=name)(a)


def _add_pair(a, b, name, tr=496):
    n, R, C = a.shape
    tr = tr if R % tr == 0 else R

    def body(a_ref, b_ref, o_ref):
        o_ref[...] = (a_ref[...].astype(F32) + b_ref[...].astype(F32)).astype(BF16)

    blk = pl.BlockSpec((1, tr, C), lambda k, i: (k, i, 0))
    return pl.pallas_call(
        body, out_shape=jax.ShapeDtypeStruct(a.shape, BF16), grid=(n, R // tr), in_specs=[blk, blk], out_specs=blk,
        compiler_params=_cp(("parallel", "parallel")), name=name)(a, b)


ANY = pl.BlockSpec(memory_space=pl.ANY)


def _place():
    return lax.axis_index("x"), lax.axis_index("y"), lax.axis_index("c")


def _other_chips(x, y):
    return [(1 - x, y), (x, 1 - y), (1 - x, 1 - y)]


def _row_chunks(rows, want, align=16):
    n = want
    while n > 1 and rows % (n * align):
        n -= 1
    return n


def _gather_weights(shard, name, nch=5):
    R, C = shard.shape
    half = R // 2
    nch = _row_chunks(half, nch)
    cr = half // nch

    def body(s_ref, o_ref, send_sems, recv_sems, local_sem):
        x, y, c = _place()
        j = 2 * x + y
        mine0 = c * half
        theirs0 = (1 - c) * half

        def rows(jj, r0, q):
            return o_ref.at[jj, pl.ds(pl.multiple_of(r0 + q * cr, 16), cr), :]

        def copy(k, src, dst, to):
            return pltpu.make_async_remote_copy(src_ref=src, dst_ref=dst, send_sem=send_sems.at[k], recv_sem=recv_sems.at[k],
                                                device_id=to, device_id_type=MESH)

        own = pltpu.make_async_copy(s_ref, o_ref.at[j], local_sem)
        own.start()
        chips = _other_chips(x, y)
        first = []
        for q in range(nch):
            for k, (px, py) in enumerate(chips):
                src = s_ref.at[pl.ds(pl.multiple_of(mine0 + q * cr, 16), cr), :]
                first.append(copy(k * nch + q, src, rows(j, mine0, q), (px, py, c)))
        for cp in first:
            cp.start()
        passed = []
        for q in range(nch):
            for k, (px, py) in enumerate(chips):
                jj = 2 * px + py
                copy(k * nch + q, rows(jj, mine0, q), rows(jj, mine0, q), (px, py, c)).wait_recv()
                fw = copy((3 + k) * nch + q, rows(jj, mine0, q), rows(jj, mine0, q), (x, y, 1 - c))
                fw.start()
                passed.append(fw)
        for q in range(nch):
            for k, (px, py) in enumerate(chips):
                jj = 2 * px + py
                copy((3 + k) * nch + q, rows(jj, theirs0, q), rows(jj, theirs0, q), (x, y, 1 - c)).wait_recv()
        for cp in first + passed:
            cp.wait_send()
        own.wait()

    return pl.pallas_call(
        body, out_shape=jax.ShapeDtypeStruct((4, R, C), shard.dtype), in_specs=[ANY], out_specs=ANY,
        scratch_shapes=[pltpu.SemaphoreType.DMA((6 * nch,)), pltpu.SemaphoreType.DMA((6 * nch,)), pltpu.SemaphoreType.DMA],
        name=name)(shard)


def _pair_split(g, name, nch=5):
    n, R, C = g.shape
    half = R // 2
    nch = _row_chunks(half, nch)
    cr = half // nch

    def body(g_ref, own_ref, got_ref, send_sems, recv_sems, local_sem):
        x, y, c = _place()
        mine0 = pl.multiple_of(c * half, 16)
        theirs0 = (1 - c) * half
        keep = pltpu.make_async_copy(g_ref.at[:, pl.ds(mine0, half), :], own_ref, local_sem)
        keep.start()
        cps = []
        for s in range(n):
            for q in range(nch):
                src = g_ref.at[s, pl.ds(pl.multiple_of(theirs0 + q * cr, 16), cr), :]
                cps.append(pltpu.make_async_remote_copy(
                    src_ref=src, dst_ref=got_ref.at[s, pl.ds(q * cr, cr), :], send_sem=send_sems.at[s * nch + q],
                    recv_sem=recv_sems.at[s * nch + q], device_id=(x, y, 1 - c), device_id_type=MESH))
        for cp in cps:
            cp.start()
        for cp in cps:
            cp.wait()
        keep.wait()

    sh = jax.ShapeDtypeStruct((n, half, C), g.dtype)
    return pl.pallas_call(
        body, out_shape=(sh, sh), in_specs=[ANY], out_specs=(ANY, ANY),
        scratch_shapes=[pltpu.SemaphoreType.DMA((n * nch,)), pltpu.SemaphoreType.DMA((n * nch,)), pltpu.SemaphoreType.DMA],
        name=name)(g)


def _chip_all_to_all(p, name, nch=5):
    R = p.shape[1]
    nch = _row_chunks(R, nch)
    cr = R // nch

    def body(p_ref, o_ref, send_sems, recv_sems, local_sem):
        x, y, c = _place()
        j = 2 * x + y
        own = pltpu.make_async_copy(p_ref.at[j], o_ref.at[j], local_sem)
        own.start()
        cps = []
        for q in range(nch):
            for k, (px, py) in enumerate(_other_chips(x, y)):
                cps.append(pltpu.make_async_remote_copy(
                    src_ref=p_ref.at[2 * px + py, pl.ds(q * cr, cr), :], dst_ref=o_ref.at[j, pl.ds(q * cr, cr), :],
                    send_sem=send_sems.at[k * nch + q], recv_sem=recv_sems.at[k * nch + q], device_id=(px, py, c),
                    device_id_type=MESH))
        for cp in cps:
            cp.start()
        for cp in cps:
            cp.wait()
        own.wait()

    return pl.pallas_call(
        body, out_shape=jax.ShapeDtypeStruct(p.shape, p.dtype), in_specs=[ANY], out_specs=ANY,
        scratch_shapes=[pltpu.SemaphoreType.DMA((3 * nch,)), pltpu.SemaphoreType.DMA((3 * nch,)), pltpu.SemaphoreType.DMA],
        name=name)(p)


def _pair_gather(t, name, nch=10):
    R = t.shape[0]
    nch = _row_chunks(R, nch, 8)
    cr = R // nch

    def body(t_ref, o_ref, send_sems, recv_sems, local_sem):
        x, y, c = _place()
        own = pltpu.make_async_copy(t_ref, o_ref.at[c], local_sem)
        own.start()
        cps = [pltpu.make_async_remote_copy(src_ref=t_ref.at[pl.ds(q * cr, cr), :], dst_ref=o_ref.at[c, pl.ds(q * cr, cr), :],
                                            send_sem=send_sems.at[q], recv_sem=recv_sems.at[q], device_id=(x, y, 1 - c),
                                            device_id_type=MESH) for q in range(nch)]
        for cp in cps:
            cp.start()
        for cp in cps:
            cp.wait()
        own.wait()

    return pl.pallas_call(
        body, out_shape=jax.ShapeDtypeStruct((2,) + t.shape, t.dtype), in_specs=[ANY], out_specs=ANY,
        scratch_shapes=[pltpu.SemaphoreType.DMA((nch,)), pltpu.SemaphoreType.DMA((nch,)), pltpu.SemaphoreType.DMA], name=name)(t)


def _reduce_scatter(g, tag):
    own, got = _pair_split(g, f"rs_pair_{tag}")
    p = _add_pair(own, got, f"rs_add_{tag}")
    q = _chip_all_to_all(p, f"rs_a2a_{tag}")
    t = _sum_slots(q, F32, f"rs_sum_{tag}")
    both = _pair_gather(t, f"rs_join_{tag}")
    return both.reshape(g.shape[1], g.shape[2])


def _all_reduce_small(v, tag):
    pair = _pair_gather(v, f"ar_pair_{tag}")
    p = _sum_slots(pair, F32, f"ar_add_{tag}")
    q = _chip_all_to_all(jnp.broadcast_to(p[None], (4,) + p.shape), f"ar_a2a_{tag}")
    return _sum_slots(q, F32, f"ar_sum_{tag}")


def _pack_rows(n):
    return -(-n // 16) * 16


def _pack_shards(parts):
    flat = [p.reshape(-1, PACK_COLS) for p in parts]
    flat = [jnp.pad(f, ((0, _pack_rows(f.shape[0]) - f.shape[0]), (0, 0))) for f in flat]
    used = sum(f.shape[0] for f in flat)
    flat.append(jnp.zeros((PACK_ROWS - used, PACK_COLS), flat[0].dtype))
    return jnp.concatenate(flat, axis=0)


def _unpack_shard(buf):
    out, r = [], 0
    for _, shape, axis in BIG:
        ss = _shard_shape(shape, axis)
        n = ss[0] * ss[1] // PACK_COLS
        out.append(buf[r:r + n].reshape(ss))
        r += _pack_rows(n)
    return out


def _unpack_full(gathered):
    per_chip = [_unpack_shard(gathered[j]) for j in range(4)]
    return {name: jnp.concatenate([per_chip[j][i] for j in range(4)], axis=axis) for i, (name, _, axis) in enumerate(BIG)}


def _pack_full(grads):
    slots = []
    for j in range(4):
        parts = []
        for name, shape, axis in BIG:
            n = shape[axis] // 4
            parts.append(lax.slice_in_dim(grads[name], j * n, (j + 1) * n, axis=axis))
        slots.append(_pack_shards(parts))
    return jnp.stack(slots)


def _pad_w_in(w):
    return jnp.concatenate([w[:, 2312:5384], w[:, 256:1792], w[:, 1800:2312], w[:, 0:256], w[:, 1792:1800],
                            jnp.zeros((w.shape[0], NP - N_IN), w.dtype)], axis=1)


def _unpad_w_in(w):
    return jnp.concatenate([w[:, P_A:P_A + 256], w[:, P_Q:P_Q + 1536], w[:, P_F:P_F + 8], w[:, P_C:P_C + 512], w[:, P_G:P_G + 3072]],
                           axis=1)


def _small_prep(sw, l):
    eye = jnp.eye(4, dtype=F32)
    bd = jnp.einsum('gh,gcd->gchd', eye, sw['pool_w'][l]).reshape(POOL_W, POOL_W).astype(BF16)
    tril = jnp.tril(jnp.ones((SGU_CHUNK, SGU_CHUNK), F32))
    wm = (sw['sgu_w'][l] * tril[None]).astype(BF16)
    return dict(
        g_mix=sw['norm_mix_g'][l][None], g_x=sw['norm_xattn_g'][l][None], g_mem=sw['norm_mem_g'][l][None],
        g_ffn=sw['norm_ffn_g'][l][None], bd=bd, pool_scale=sw['pool_scale'][l][None],
        bf=jnp.pad(sw['b_forget'][l], (0, FCOLS - 8))[None], sgu_g=sw['sgu_norm_g'][l][None], wm=wm,
        wmt=jnp.transpose(wm, (0, 2, 1)), sgu_bias=jnp.repeat(sw['sgu_b'][l].T, 64, axis=1), bg=sw['b_gate'][l][None])


def _layer_fwd(x, mem, W, sp, l):
    t = f"l{l}"
    S = x.shape[0]
    h = _rms_fwd(x, sp['g_mix'], f"rms_mix_{t}")
    proj = _mm(h, W['w_in_p'], name=f"proj_{t}", out_dtype=F32)
    d, ya = _pool_fwd(proj, sp['bd'], sp['pool_scale'], f"pool_fwd_{t}")
    fcum = _fgate_fwd(proj, sp['bf'], f"fgate_fwd_{t}")
    f8 = fcum[:, :8]
    fcol = f8.reshape(S, 4, 2).transpose(1, 0, 2)
    frow = f8.T.reshape(4, 2, S)
    qkv = proj[:, P_Q:P_Q + 3 * FOX_W].astype(BF16)
    o, lse = _fox_fwd(qkv, fcol, frow, f"fox_fwd_{t}")
    sg = _sgu_fwd(proj, sp['sgu_g'], sp['wm'], sp['sgu_bias'], f"sgu_fwd_{t}")
    merged = _merge_fwd(proj, ya, o, sg, W['w_branch_a'], W['w_branch_b'], W['w_branch_c'], sp['bg'], f"merge_fwd_{t}")
    x1 = _mm(merged, W['w_out'], name=f"out_{t}", out_dtype=F32, extra=x, epi=lambda r, e: e + r)
    hx = _rms_fwd(x1, sp['g_x'], f"rms_x_{t}")
    hm = _rms_fwd(mem, sp['g_mem'], f"rms_mem_{t}")
    xq = _mm(hx, W['w_xq'], name=f"xq_{t}", out_dtype=BF16)
    kv = _mm(hm, W['w_xkv'], name=f"xkv_{t}", out_dtype=BF16)
    o2 = _xattn_fwd(xq, kv, f"xattn_fwd_{t}")
    x2 = _mm(o2, W['w_xo'], name=f"xo_{t}", out_dtype=F32, extra=x1, epi=lambda r, e: e + r)
    hf = _rms_fwd(x2, sp['g_ffn'], f"rms_ffn_{t}")
    z = _mm(hf, W['w_ff1'], name=f"ff1_{t}", out_dtype=F32)
    x3 = _mm(z, W['w_ff2'], name=f"ff2_{t}", out_dtype=F32, a_fn=_relu2, extra=x2, epi=lambda r, e: e + r)
    saved = dict(x=x, h=h, proj=proj, d=d, ya=ya, fcol=fcol, frow=frow, qkv=qkv, o=o, lse=lse, sg=sg, merged=merged, x1=x1,
                 hx=hx, hm=hm, xq=xq, kv=kv, o2=o2, x2=x2, hf=hf, z=z)
    return x3, saved


def _layer_bwd(dx3, mem, W, sp, sv, l):
    t = f"l{l}"
    S = dx3.shape[0]
    gb, gs = {}, {}
    dz = _mm(dx3, W['w_ff2'], name=f"d_a2_{t}", out_dtype=BF16, tb=True, extra=sv['z'],
             epi=lambda r, e: r * (2.0 * jnp.maximum(e, 0.0)))
    gb['w_ff2'] = _mm(sv['z'], dx3, name=f"dw_ff2_{t}", out_dtype=BF16, ta=True, a_fn=_relu2)
    gb['w_ff1'] = _mm(sv['hf'], dz, name=f"dw_ff1_{t}", out_dtype=BF16, ta=True)
    dhf = _mm(dz, W['w_ff1'], name=f"d_hf_{t}", out_dtype=F32, tb=True)
    dx2, gs['norm_ffn_g'] = _rms_bwd(dhf, sv['x2'], sp['g_ffn'], dx3, f"rms_ffn_bwd_{t}")
    do2 = _mm(dx2, W['w_xo'], name=f"d_o2_{t}", out_dtype=BF16, tb=True)
    gb['w_xo'] = _mm(sv['o2'], dx2, name=f"dw_xo_{t}", out_dtype=BF16, ta=True)
    dxq, dkv = _xattn_bwd(sv['xq'], sv['kv'], do2, f"xattn_bwd_{t}")
    gb['w_xkv'] = _mm(sv['hm'], dkv, name=f"dw_xkv_{t}", out_dtype=BF16, ta=True)
    dhm = _mm(dkv, W['w_xkv'], name=f"d_hm_{t}", out_dtype=F32, tb=True)
    gs['norm_mem_g'] = _rms_bwd(dhm, mem, sp['g_mem'], None, f"rms_mem_bwd_{t}")
    gb['w_xq'] = _mm(sv['hx'], dxq, name=f"dw_xq_{t}", out_dtype=BF16, ta=True)
    dhx = _mm(dxq, W['w_xq'], name=f"d_hx_{t}", out_dtype=F32, tb=True)
    dx1, gs['norm_xattn_g'] = _rms_bwd(dhx, sv['x1'], sp['g_x'], dx2, f"rms_x_bwd_{t}")
    gb['w_out'] = _mm(sv['merged'], dx1, name=f"dw_out_{t}", out_dtype=BF16, ta=True)
    dm = _mm(dx1, W['w_out'], name=f"d_merged_{t}", out_dtype=F32, tb=True)
    dg, dya, do, dsg, gb['w_branch_a'], gb['w_branch_b'], gb['w_branch_c'], gs['b_gate'] = _merge_bwd(
        dm, sv['proj'], sv['ya'], sv['o'], sv['sg'], W['w_branch_a'], W['w_branch_b'], W['w_branch_c'], sp['bg'], f"merge_bwd_{t}")
    dc, dws, dbias, gs['sgu_norm_g'] = _sgu_bwd(dsg, sv['proj'], sp['sgu_g'], sp['wm'], sp['wmt'], sp['sgu_bias'], f"sgu_bwd_{t}")
    tril = jnp.tril(jnp.ones((SGU_CHUNK, SGU_CHUNK), F32))
    gs['sgu_w'] = dws * tril[None]
    gs['sgu_b'] = dbias.reshape(SGU_CHUNK, 4, 64).sum(-1).T
    dq, dk, dv, dfrow = _fox_bwd(sv['qkv'], do, sv['lse'], sv['fcol'], sv['frow'], f"fox_bwd_{t}")
    dF = jnp.pad(dfrow.reshape(8, S).T, ((0, 0), (0, FCOLS - 8)))
    df, dbf = _fgate_bwd(dF, sv['proj'], sp['bf'], f"fgate_bwd_{t}")
    gs['b_forget'] = dbf[:, :8]
    da, dbd, gs['pool_scale'] = _pool_bwd(dya, sv['d'], sp['bd'], sp['pool_scale'], f"pool_bwd_{t}")
    gs['pool_w'] = jnp.stack([dbd[g * 64:(g + 1) * 64, g * 64:(g + 1) * 64] for g in range(4)])
    dproj = jnp.concatenate([dg, dq, dk, dv, dc, da, df], axis=1)
    gb['w_in'] = _unpad_w_in(_mm(sv['h'], dproj, name=f"dw_in_{t}", out_dtype=BF16, ta=True))
    dh = _mm(dproj, W['w_in_p'], name=f"d_h_{t}", out_dtype=F32, tb=True, tk=512)
    dx, gs['norm_mix_g'] = _rms_bwd(dh, sv['x'], sp['g_mix'], dx1, f"rms_mix_bwd_{t}")
    return dx, gb, gs


SMALL_ROWS = 1424


def _pack_small(parts):
    flat = jnp.concatenate([p.reshape(-1) for p in parts])
    return jnp.pad(flat, (0, SMALL_ROWS * 128 - flat.shape[0])).reshape(SMALL_ROWS, 128)


def _unpack_small(buf, shapes):
    flat, out, r = buf.reshape(-1), [], 0
    for s in shapes:
        n = math.prod(s)
        out.append(flat[r:r + n].reshape(s))
        r += n
    return out


def kernel(x, mem, norm_mix_g, w_in, b_forget, pool_w, pool_scale, sgu_norm_g, sgu_w, sgu_b, w_branch_a, w_branch_b, w_branch_c, b_gate, w_out, norm_xattn_g, norm_mem_g, w_xq, w_xkv, w_xo, norm_ffn_g, w_ff1, w_ff2, final_norm_g, loss_target, m_norm_mix_g, m_w_in, m_b_forget, m_pool_w, m_pool_scale, m_sgu_norm_g, m_sgu_w, m_sgu_b, m_w_branch_a, m_w_branch_b, m_w_branch_c, m_b_gate, m_w_out, m_norm_xattn_g, m_norm_mem_g, m_w_xq, m_w_xkv, m_w_xo, m_norm_ffn_g, m_w_ff1, m_w_ff2, m_final_norm_g, v_norm_mix_g, v_w_in, v_b_forget, v_pool_w, v_pool_scale, v_sgu_norm_g, v_sgu_w, v_sgu_b, v_w_branch_a, v_w_branch_b, v_w_branch_c, v_b_gate, v_w_out, v_norm_xattn_g, v_norm_mem_g, v_w_xq, v_w_xkv, v_w_xo, v_norm_ffn_g, v_w_ff1, v_w_ff2, v_final_norm_g):
    args = (norm_mix_g, w_in, b_forget, pool_w, pool_scale, sgu_norm_g, sgu_w, sgu_b, w_branch_a, w_branch_b, w_branch_c, b_gate,
            w_out, norm_xattn_g, norm_mem_g, w_xq, w_xkv, w_xo, norm_ffn_g, w_ff1, w_ff2, final_norm_g)
    margs = (m_norm_mix_g, m_w_in, m_b_forget, m_pool_w, m_pool_scale, m_sgu_norm_g, m_sgu_w, m_sgu_b, m_w_branch_a, m_w_branch_b,
             m_w_branch_c, m_b_gate, m_w_out, m_norm_xattn_g, m_norm_mem_g, m_w_xq, m_w_xkv, m_w_xo, m_norm_ffn_g, m_w_ff1, m_w_ff2,
             m_final_norm_g)
    vargs = (v_norm_mix_g, v_w_in, v_b_forget, v_pool_w, v_pool_scale, v_sgu_norm_g, v_sgu_w, v_sgu_b, v_w_branch_a, v_w_branch_b,
             v_w_branch_c, v_b_gate, v_w_out, v_norm_xattn_g, v_norm_mem_g, v_w_xq, v_w_xkv, v_w_xo, v_norm_ffn_g, v_w_ff1, v_w_ff2,
             v_final_norm_g)
    w = dict(zip(W_NAMES, args))
    mo = dict(zip(W_NAMES, margs))
    vo = dict(zip(W_NAMES, vargs))
    xs, mems, tgt = x[0], mem[0], loss_target[0]

    full = []
    for l in range(DEPTH):
        shard = _pack_shards([w[n][l].astype(BF16) for n in BIG_NAMES])
        Wl = _unpack_full(_gather_weights(shard, f"gather_w_l{l}"))
        Wl['w_in_p'] = _pad_w_in(Wl.pop('w_in'))
        full.append(Wl)
    preps = [_small_prep(w, l) for l in range(DEPTH)]

    act, saved = xs, []
    for l in range(DEPTH):
        act, sv = _layer_fwd(act, mems, full[l], preps[l], l)
        saved.append(sv)
    loss_part, dact, d_final_g = _loss_head(act, w['final_norm_g'][None], tgt, "loss_head")

    big_red, small_g = [None] * DEPTH, [None] * DEPTH
    for l in reversed(range(DEPTH)):
        dact, gb, gs = _layer_bwd(dact, mems, full[l], preps[l], saved[l], l)
        big_red[l] = _unpack_shard(_reduce_scatter(_pack_full(gb), f"l{l}"))
        small_g[l] = gs
    grad_x = dact[None]

    per_layer = [n for n in SMALL_NAMES if n != 'final_norm_g']
    small_shapes = [w[n].shape for n in per_layer] + [(D,), (1,)]
    parts = [jnp.stack([small_g[l][n].reshape(w[n].shape[1:]) for l in range(DEPTH)]) for n in per_layer]
    red = _unpack_small(_all_reduce_small(_pack_small(parts + [d_final_g.reshape(D), loss_part.reshape(1)]), "small"), small_shapes)
    grads = dict(zip(per_layer + ['final_norm_g'], red[:-1]))
    loss = red[-1].reshape(())
    for i, n in enumerate(BIG_NAMES):
        grads[n] = jnp.stack([big_red[l][i] for l in range(DEPTH)])

    delta, new_m, new_v = {}, {}, {}
    for n in BIG_NAMES:
        delta[n], new_m[n], new_v[n] = _adamw(grads[n], w[n], mo[n], vo[n], f"adamw_{n}")
    small_all = per_layer + ['final_norm_g']
    shapes_all = [w[n].shape for n in small_all]
    packed = [_pack_small([d[n] for n in small_all])[None] for d in (grads, w, mo, vo)]
    ds, ms, vs = _adamw(*packed, "adamw_small")
    for n, a, b, c in zip(small_all, _unpack_small(ds[0], shapes_all), _unpack_small(ms[0], shapes_all), _unpack_small(vs[0], shapes_all)):
        delta[n], new_m[n], new_v[n] = a, b, c

    return (loss, grad_x, *[grads[n] for n in W_NAMES], *[delta[n] for n in W_NAMES], *[new_m[n] for n in W_NAMES],
            *[new_v[n] for n in W_NAMES])
```

```python
import functools
import math

import jax
import jax.numpy as jnp
from jax import lax
from jax.experimental import pallas as pl
from jax.experimental.pallas import tpu as pltpu

F32 = jnp.float32
BF16 = jnp.bfloat16

D = 1024
DEPTH = 2
POOL_W = 256
FOX_W = 512
SGU_W = 256
SGU_CHUNK = 128
N_IN = 5384
P_G, P_Q, P_K, P_V, P_C, P_A, P_F = 0, 3072, 3584, 4096, 4608, 5120, 5376
NP = 5632
XH, XHD = 4, 256
D_FF = 4096
EPS = 1e-6
NEG = -1e30
FOX_SCALE = 64 ** -0.5
X_SCALE = 256 ** -0.5
GELU_K = math.sqrt(2.0 / math.pi)
GELU_C = 0.044715

ADAM_LR, ADAM_B1, ADAM_B2, ADAM_EPS, ADAM_WD, ADAM_STEP = 0.001, 0.9, 0.999, 1e-08, 0.01, 10

VMEM_LIMIT = 48 * 1024 * 1024
MESH = pl.DeviceIdType.MESH

IN_NAMES = ['x', 'mem', 'norm_mix_g', 'w_in', 'b_forget', 'pool_w', 'pool_scale', 'sgu_norm_g', 'sgu_w', 'sgu_b',
            'w_branch_a', 'w_branch_b', 'w_branch_c', 'b_gate', 'w_out', 'norm_xattn_g', 'norm_mem_g', 'w_xq',
            'w_xkv', 'w_xo', 'norm_ffn_g', 'w_ff1', 'w_ff2', 'final_norm_g']
W_NAMES = IN_NAMES[2:]
BIG = [('w_in', (D, N_IN), 1), ('w_branch_a', (POOL_W, D), 1), ('w_branch_b', (FOX_W, D), 1),
       ('w_branch_c', (SGU_W, D), 1), ('w_out', (D, D), 0), ('w_xq', (D, D), 0), ('w_xkv', (D, 2 * D), 1),
       ('w_xo', (D, D), 0), ('w_ff1', (D, D_FF), 1), ('w_ff2', (D_FF, D), 0)]
BIG_NAMES = [b[0] for b in BIG]
SMALL_NAMES = [n for n in W_NAMES if n not in BIG_NAMES]
PACK_COLS = 1024
PACK_ROWS = 4960


def _cp(sem=None):
    return pltpu.CompilerParams(dimension_semantics=sem, vmem_limit_bytes=VMEM_LIMIT)


def _shard_shape(shape, axis):
    s = list(shape)
    s[axis] //= 4
    return tuple(s)


def _mm(a, b, *, name, out_dtype, ta=False, tb=False, tm=1024, tn=512, tk=1024, a_fn=None, extra=None, epi=None):
    M = a.shape[1] if ta else a.shape[0]
    K = a.shape[0] if ta else a.shape[1]
    N = b.shape[0] if tb else b.shape[1]
    tm, tn, tk = min(tm, M), min(tn, N), min(tk, K)
    assert M % tm == 0 and N % tn == 0 and K % tk == 0, (name, M, N, K)
    nk = K // tk
    a_spec = pl.BlockSpec((tk, tm), lambda i, j, k: (k, i)) if ta else pl.BlockSpec((tm, tk), lambda i, j, k: (i, k))
    b_spec = pl.BlockSpec((tn, tk), lambda i, j, k: (j, k)) if tb else pl.BlockSpec((tk, tn), lambda i, j, k: (k, j))
    dn = (((0 if ta else 1,), (1 if tb else 0,)), ((), ()))
    o_spec = pl.BlockSpec((tm, tn), lambda i, j, k: (i, j))
    in_specs = [a_spec, b_spec] + ([o_spec] if extra is not None else [])

    def body(*refs):
        if extra is not None:
            a_ref, b_ref, e_ref, o_ref, acc_ref = refs
        else:
            a_ref, b_ref, o_ref, acc_ref = refs
            e_ref = None
        k = pl.program_id(2)

        @pl.when(k == 0)
        def _():
            acc_ref[...] = jnp.zeros_like(acc_ref)

        av = a_ref[...]
        if a_fn is not None:
            av = a_fn(av)
        acc_ref[...] += lax.dot_general(av.astype(BF16), b_ref[...].astype(BF16), dn, preferred_element_type=F32)

        @pl.when(k == nk - 1)
        def _():
            r = acc_ref[...]
            if epi is not None:
                r = epi(r, e_ref[...]) if e_ref is not None else epi(r)
            o_ref[...] = r.astype(out_dtype)

    args = (a, b) + ((extra,) if extra is not None else ())
    return pl.pallas_call(
        body, out_shape=jax.ShapeDtypeStruct((M, N), out_dtype), grid=(M // tm, N // tn, nk),
        in_specs=in_specs, out_specs=o_spec, scratch_shapes=[pltpu.VMEM((tm, tn), F32)],
        compiler_params=_cp(("parallel", "parallel", "arbitrary")), name=name)(*args)


def _relu2(z):
    r = jnp.maximum(z, 0.0)
    return r * r


def _rms_fwd(x, g, name, tr=256):
    R, n = x.shape
    tr = min(tr, R)

    def body(x_ref, g_ref, h_ref):
        xv = x_ref[...]
        rstd = lax.rsqrt(jnp.mean(xv * xv, axis=-1, keepdims=True) + EPS)
        h_ref[...] = (xv * rstd * g_ref[...]).astype(BF16)

    return pl.pallas_call(
        body, out_shape=jax.ShapeDtypeStruct((R, n), BF16), grid=(R // tr,),
        in_specs=[pl.BlockSpec((tr, n), lambda i: (i, 0)), pl.BlockSpec((1, n), lambda i: (0, 0))],
        out_specs=pl.BlockSpec((tr, n), lambda i: (i, 0)), compiler_params=_cp(("parallel",)), name=name)(x, g)


def _rms_bwd(dh, x, g, dres, name, tr=256):
    R, n = x.shape
    tr = min(tr, R)
    need_dx = dres is not None

    def body(*refs):
        if need_dx:
            dh_ref, x_ref, g_ref, r_ref, dx_ref, dg_ref = refs
        else:
            dh_ref, x_ref, g_ref, dg_ref = refs
        i = pl.program_id(0)
        xv = x_ref[...]
        dhv = dh_ref[...].astype(F32)
        rstd = lax.rsqrt(jnp.mean(xv * xv, axis=-1, keepdims=True) + EPS)
        xhat = xv * rstd

        @pl.when(i == 0)
        def _():
            dg_ref[...] = jnp.zeros_like(dg_ref)

        dg_ref[...] += jnp.sum(dhv * xhat, axis=0, keepdims=True)
        if need_dx:
            t = dhv * g_ref[...]
            dx_ref[...] = r_ref[...] + rstd * (t - xhat * jnp.mean(t * xhat, axis=-1, keepdims=True))

    row = pl.BlockSpec((tr, n), lambda i: (i, 0))
    vec = pl.BlockSpec((1, n), lambda i: (0, 0))
    if need_dx:
        return pl.pallas_call(
            body, out_shape=(jax.ShapeDtypeStruct((R, n), F32), jax.ShapeDtypeStruct((1, n), F32)), grid=(R // tr,),
            in_specs=[row, row, vec, row], out_specs=(row, vec), compiler_params=_cp(("arbitrary",)), name=name)(dh, x, g, dres)
    return pl.pallas_call(
        body, out_shape=jax.ShapeDtypeStruct((1, n), F32), grid=(R // tr,),
        in_specs=[row, row, vec], out_specs=vec, compiler_params=_cp(("arbitrary",)), name=name)(dh, x, g)


def _loss_head(x, g, tgt, name, tr=256):
    R, n = x.shape

    def body(x_ref, g_ref, t_ref, loss_ref, dx_ref, dg_ref):
        i = pl.program_id(0)
        xv = x_ref[...]
        gv = g_ref[...]
        rstd = lax.rsqrt(jnp.mean(xv * xv, axis=-1, keepdims=True) + EPS)
        xhat = xv * rstd
        e = xhat * gv - t_ref[...]

        @pl.when(i == 0)
        def _():
            loss_ref[...] = jnp.zeros_like(loss_ref)
            dg_ref[...] = jnp.zeros_like(dg_ref)

        loss_ref[...] += 0.5 * jnp.sum(jnp.sum(e * e, axis=-1, keepdims=True) / n, axis=0, keepdims=True)
        dy = e / n
        dg_ref[...] += jnp.sum(dy * xhat, axis=0, keepdims=True)
        t = dy * gv
        dx_ref[...] = rstd * (t - xhat * jnp.mean(t * xhat, axis=-1, keepdims=True))

    row = pl.BlockSpec((tr, n), lambda i: (i, 0))
    vec = pl.BlockSpec((1, n), lambda i: (0, 0))
    one = pl.BlockSpec((1, 1), lambda i: (0, 0))
    return pl.pallas_call(
        body, out_shape=(jax.ShapeDtypeStruct((1, 1), F32), jax.ShapeDtypeStruct((R, n), F32), jax.ShapeDtypeStruct((1, n), F32)),
        grid=(R // tr,), in_specs=[row, vec, row], out_specs=(one, row, vec),
        compiler_params=_cp(("arbitrary",)), name=name)(x, g, tgt)


def _pool_masks(S):
    row = lax.broadcasted_iota(jnp.int32, (S, POOL_W), 0)
    grp = lax.broadcasted_iota(jnp.int32, (S, POOL_W), 1) // 64
    win = jnp.where(grp == 0, 2, jnp.where(grp == 1, 4, jnp.where(grp == 2, 8, 16)))
    cnt = jnp.minimum(row + 1, win).astype(F32)
    return row, grp, cnt


def _by_group(grp, v0, v1, v2, v3):
    return jnp.where(grp == 0, v0, jnp.where(grp == 1, v1, jnp.where(grp == 2, v2, v3)))


def _pool_fwd(proj, bd, scale, name):
    S = proj.shape[0]

    def body(a_ref, bd_ref, sc_ref, d_ref, y_ref):
        a = a_ref[...]
        row, grp, cnt = _pool_masks(S)

        def back(v, k):
            return jnp.where(row >= k, pltpu.roll(v, k, 0), 0.0)

        s1 = a + back(a, 1)
        s2 = s1 + back(s1, 2)
        s3 = s2 + back(s2, 4)
        s4 = s3 + back(s3, 8)
        d = (_by_group(grp, s1, s2, s3, s4) / cnt - a).astype(BF16)
        d_ref[...] = d
        y_ref[...] = (jnp.dot(d, bd_ref[...], preferred_element_type=F32) * sc_ref[...]).astype(BF16)

    full = lambda r, c: pl.BlockSpec((r, c), lambda i: (0, 0))
    return pl.pallas_call(
        body, out_shape=(jax.ShapeDtypeStruct((S, POOL_W), BF16), jax.ShapeDtypeStruct((S, POOL_W), BF16)), grid=(1,),
        in_specs=[pl.BlockSpec((S, POOL_W), lambda i: (0, P_A // POOL_W)), full(POOL_W, POOL_W), full(1, POOL_W)],
        out_specs=(full(S, POOL_W), full(S, POOL_W)), compiler_params=_cp(("arbitrary",)), name=name)(proj, bd, scale)


def _pool_bwd(dya, d, bd, scale, name):
    S = dya.shape[0]

    def body(dy_ref, d_ref, bd_ref, sc_ref, da_ref, dbd_ref, dsc_ref):
        dy = dy_ref[...]
        dv = d_ref[...]
        bdv = bd_ref[...]
        row, grp, cnt = _pool_masks(S)
        yraw = jnp.dot(dv, bdv, preferred_element_type=F32)
        dsc_ref[...] = jnp.sum(dy * yraw, axis=0, keepdims=True)
        tb = (dy * sc_ref[...]).astype(BF16)
        dbd_ref[...] = lax.dot_general(dv, tb, (((0,), (0,)), ((), ())), preferred_element_type=F32)
        dd = lax.dot_general(tb, bdv, (((1,), (1,)), ((), ())), preferred_element_type=F32)
        e = dd / cnt

        def fwd(v, k):
            return jnp.where(row < S - k, pltpu.roll(v, S - k, 0), 0.0)

        r1 = e + fwd(e, 1)
        r2 = r1 + fwd(r1, 2)
        r3 = r2 + fwd(r2, 4)
        r4 = r3 + fwd(r3, 8)
        da_ref[...] = (_by_group(grp, r1, r2, r3, r4) - dd).astype(BF16)

    full = lambda r, c: pl.BlockSpec((r, c), lambda i: (0, 0))
    return pl.pallas_call(
        body, out_shape=(jax.ShapeDtypeStruct((S, POOL_W), BF16), jax.ShapeDtypeStruct((POOL_W, POOL_W), F32),
                         jax.ShapeDtypeStruct((1, POOL_W), F32)), grid=(1,),
        in_specs=[full(S, POOL_W), full(S, POOL_W), full(POOL_W, POOL_W), full(1, POOL_W)],
        out_specs=(full(S, POOL_W), full(POOL_W, POOL_W), full(1, POOL_W)),
        compiler_params=_cp(("arbitrary",)), name=name)(dya, d, bd, scale)


FCOLS = 128


def _log_sigmoid(z):
    return -(jnp.maximum(-z, 0.0) + jnp.log1p(jnp.exp(-jnp.abs(z))))


def _fgate_fwd(proj, bf, name):
    S = proj.shape[0]

    def body(f_ref, b_ref, o_ref):
        v = _log_sigmoid(f_ref[...] + b_ref[...])
        row = lax.broadcasted_iota(jnp.int32, (S, FCOLS), 0)
        k = 1
        while k < S:
            v = v + jnp.where(row >= k, pltpu.roll(v, k, 0), 0.0)
            k *= 2
        o_ref[...] = v

    return pl.pallas_call(
        body, out_shape=jax.ShapeDtypeStruct((S, FCOLS), F32), grid=(1,),
        in_specs=[pl.BlockSpec((S, FCOLS), lambda i: (0, P_F // FCOLS)), pl.BlockSpec((1, FCOLS), lambda i: (0, 0))],
        out_specs=pl.BlockSpec((S, FCOLS), lambda i: (0, 0)), compiler_params=_cp(("arbitrary",)), name=name)(proj, bf)


def _fgate_bwd(dF, proj, bf, name):
    S = proj.shape[0]

    def body(dF_ref, f_ref, b_ref, df_ref, db_ref):
        v = dF_ref[...]
        row = lax.broadcasted_iota(jnp.int32, (S, FCOLS), 0)
        k = 1
        while k < S:
            v = v + jnp.where(row < S - k, pltpu.roll(v, S - k, 0), 0.0)
            k *= 2
        z = f_ref[...] + b_ref[...]
        df = v * (1.0 / (1.0 + jnp.exp(z)))
        db_ref[...] = jnp.sum(df, axis=0, keepdims=True)
        df_ref[...] = jnp.concatenate([df, jnp.zeros_like(df)], axis=1).astype(BF16)

    return pl.pallas_call(
        body, out_shape=(jax.ShapeDtypeStruct((S, 2 * FCOLS), BF16), jax.ShapeDtypeStruct((1, FCOLS), F32)), grid=(1,),
        in_specs=[pl.BlockSpec((S, FCOLS), lambda i: (0, 0)), pl.BlockSpec((S, FCOLS), lambda i: (0, P_F // FCOLS)),
                  pl.BlockSpec((1, FCOLS), lambda i: (0, 0))],
        out_specs=(pl.BlockSpec((S, 2 * FCOLS), lambda i: (0, 0)), pl.BlockSpec((1, FCOLS), lambda i: (0, 0))),
        compiler_params=_cp(("arbitrary",)), name=name)(dF, proj, bf)


def _fox_scores(qe, kj, fq, fk, r0, c0, tq, tk):
    s = lax.dot_general(qe, kj, (((1,), (1,)), ((), ())), preferred_element_type=F32) * FOX_SCALE
    s = s + (fq - fk)
    rows = r0 + lax.broadcasted_iota(jnp.int32, (tq, tk), 0)
    cols = c0 + lax.broadcasted_iota(jnp.int32, (tq, tk), 1)
    return jnp.where(rows >= cols, s, NEG)


def _fox_fwd(qkv, fcol, frow, name, tq=256):
    S = qkv.shape[0]
    tk = tq

    def body(q_ref, k_ref, v_ref, fc_ref, fr_ref, o_ref, lse_ref):
        i = pl.program_id(1)
        r0 = i * tq
        q = q_ref[...]
        half = lax.broadcasted_iota(jnp.int32, (tq, 128), 1) // 64
        outs = []
        for e in (0, 1):
            qe = jnp.where(half == e, q, jnp.zeros_like(q))
            fq = fc_ref[0, :, e:e + 1]

            def step(j, carry, qe=qe, fq=fq, e=e):
                m, l, acc = carry
                c0 = pl.multiple_of(j * tk, tk)
                kj = k_ref[pl.ds(c0, tk), :]
                vj = v_ref[pl.ds(c0, tk), :]
                fk = fr_ref[0, e:e + 1, pl.ds(c0, tk)]
                s = _fox_scores(qe, kj, fq, fk, r0, c0, tq, tk)
                m_new = jnp.maximum(m, jnp.max(s, axis=-1, keepdims=True))
                alpha = jnp.exp(m - m_new)
                p = jnp.exp(s - m_new)
                l = alpha * l + jnp.sum(p, axis=-1, keepdims=True)
                acc = alpha * acc + jnp.dot(p.astype(BF16), vj, preferred_element_type=F32)
                return m_new, l, acc

            m, l, acc = lax.fori_loop(0, i + 1, step, (jnp.full((tq, 1), NEG, F32), jnp.zeros((tq, 1), F32),
                                                      jnp.zeros((tq, 128), F32)))
            outs.append(acc / l)
            lse_ref[0, :, e:e + 1] = m + jnp.log(l)
        o_ref[...] = jnp.where(half == 0, outs[0], outs[1]).astype(BF16)

    return pl.pallas_call(
        body, out_shape=(jax.ShapeDtypeStruct((S, FOX_W), BF16), jax.ShapeDtypeStruct((4, S, 2), F32)), grid=(4, S // tq),
        in_specs=[pl.BlockSpec((tq, 128), lambda h, i: (i, h)), pl.BlockSpec((S, 128), lambda h, i: (0, 4 + h)),
                  pl.BlockSpec((S, 128), lambda h, i: (0, 8 + h)), pl.BlockSpec((1, tq, 2), lambda h, i: (h, i, 0)),
                  pl.BlockSpec((1, 2, S), lambda h, i: (h, 0, 0))],
        out_specs=(pl.BlockSpec((tq, 128), lambda h, i: (i, h)), pl.BlockSpec((1, tq, 2), lambda h, i: (h, i, 0))),
        compiler_params=_cp(("parallel", "parallel")), name=name)(qkv, qkv, qkv, fcol, frow)


def _fox_bwd(qkv, do, lse, fcol, frow, name, tq=256):
    S = qkv.shape[0]
    tk = tq
    nq = S // tq

    def body(q_ref, k_ref, v_ref, do_ref, lse_ref, fc_ref, fr_ref, dq_ref, dk_ref, dv_ref, dfr_ref, dk_acc, dv_acc):
        dk_acc[...] = jnp.zeros_like(dk_acc)
        dv_acc[...] = jnp.zeros_like(dv_acc)
        dfr_ref[...] = jnp.zeros_like(dfr_ref)
        half = lax.broadcasted_iota(jnp.int32, (tq, 128), 1) // 64

        def q_block(i, _):
            r0 = pl.multiple_of(i * tq, tq)
            qi = q_ref[pl.ds(r0, tq), :]
            dob = do_ref[pl.ds(r0, tq), :].astype(BF16)
            dq_tot = jnp.zeros((tq, 128), F32)
            for e in (0, 1):
                qe = jnp.where(half == e, qi, jnp.zeros_like(qi))
                doe = jnp.where(half == e, dob, jnp.zeros_like(dob))
                lse_e = lse_ref[0, pl.ds(r0, tq), e:e + 1]
                fq = fc_ref[0, pl.ds(r0, tq), e:e + 1]

                def probs(j, qe=qe, doe=doe, lse_e=lse_e, fq=fq, e=e):
                    c0 = pl.multiple_of(j * tk, tk)
                    kj = k_ref[pl.ds(c0, tk), :]
                    vj = v_ref[pl.ds(c0, tk), :]
                    fk = fr_ref[0, e:e + 1, pl.ds(c0, tk)]
                    p = jnp.exp(_fox_scores(qe, kj, fq, fk, r0, c0, tq, tk) - lse_e)
                    dp = lax.dot_general(doe, vj, (((1,), (1,)), ((), ())), preferred_element_type=F32)
                    return c0, kj, p, dp

                def row_term(j, acc, probs=probs):
                    _, _, p, dp = probs(j)
                    return acc + jnp.sum(p * dp, axis=-1, keepdims=True)

                delta = lax.fori_loop(0, i + 1, row_term, jnp.zeros((tq, 1), F32))

                def step(j, dq, probs=probs, delta=delta, e=e):
                    c0, kj, p, dp = probs(j)
                    ds = p * (dp - delta)
                    dfr_ref[0, e:e + 1, pl.ds(c0, tk)] -= jnp.sum(ds, axis=0, keepdims=True)
                    dsb = (ds * FOX_SCALE).astype(BF16)
                    dkc = lax.dot_general(dsb, qi, (((0,), (0,)), ((), ())), preferred_element_type=F32)
                    dvc = lax.dot_general(p.astype(BF16), dob, (((0,), (0,)), ((), ())), preferred_element_type=F32)
                    dk_acc[pl.ds(c0, tk), :] += jnp.where(half == e, dkc, 0.0)
                    dv_acc[pl.ds(c0, tk), :] += jnp.where(half == e, dvc, 0.0)
                    return dq + jnp.dot(dsb, kj, preferred_element_type=F32)

                dq_e = lax.fori_loop(0, i + 1, step, jnp.zeros((tq, 128), F32))
                dq_tot = dq_tot + jnp.where(half == e, dq_e, 0.0)
            dq_ref[pl.ds(r0, tq), :] = dq_tot.astype(BF16)
            return 0

        lax.fori_loop(0, nq, q_block, 0)
        dk_ref[...] = dk_acc[...].astype(BF16)
        dv_ref[...] = dv_acc[...].astype(BF16)

    col = lambda off: pl.BlockSpec((S, 128), lambda h: (0, off + h))
    hs2 = pl.BlockSpec((1, S, 2), lambda h: (h, 0, 0))
    h2s = pl.BlockSpec((1, 2, S), lambda h: (h, 0, 0))
    return pl.pallas_call(
        body, out_shape=(jax.ShapeDtypeStruct((S, FOX_W), BF16),) * 3 + (jax.ShapeDtypeStruct((4, 2, S), F32),), grid=(4,),
        in_specs=[col(0), col(4), col(8), col(0), hs2, hs2, h2s],
        out_specs=(col(0), col(0), col(0), h2s),
        scratch_shapes=[pltpu.VMEM((S, 128), F32), pltpu.VMEM((S, 128), F32)],
        compiler_params=_cp(("parallel",)), name=name)(qkv, qkv, qkv, do, lse, fcol, frow)


def _gelu(x):
    return 0.5 * x * (1.0 + jnp.tanh(GELU_K * (x + GELU_C * x * x * x)))


def _gelu_grad(x):
    th = jnp.tanh(GELU_K * (x + GELU_C * x * x * x))
    return 0.5 * (1.0 + th) + 0.5 * x * (1.0 - th * th) * GELU_K * (1.0 + 3.0 * GELU_C * x * x)


def _sgu_parts(c, gn, w_ref, bias):
    zc = _gelu(c)
    u, vv = zc[:, :SGU_W], zc[:, SGU_W:]
    rstd = lax.rsqrt(jnp.mean(vv * vv, axis=-1, keepdims=True) + EPS)
    vhat = vv * rstd
    vnb = (vhat * gn).astype(BF16)
    grp = lax.broadcasted_iota(jnp.int32, (SGU_CHUNK, SGU_W), 1) // 64
    mixed = bias
    for gi in range(4):
        mixed = mixed + jnp.where(grp == gi, jnp.dot(w_ref[gi], vnb, preferred_element_type=F32), 0.0)
    return u, rstd, vhat, vnb, grp, mixed


def _sgu_fwd(proj, gn, wm, bias, name):
    S = proj.shape[0]

    def body(c_ref, g_ref, w_ref, b_ref, o_ref):
        u, _, _, _, _, mixed = _sgu_parts(c_ref[...], g_ref[...], w_ref, b_ref[...])
        o_ref[...] = (u * mixed).astype(BF16)

    return pl.pallas_call(
        body, out_shape=jax.ShapeDtypeStruct((S, SGU_W), BF16), grid=(S // SGU_CHUNK,),
        in_specs=[pl.BlockSpec((SGU_CHUNK, 2 * SGU_W), lambda i: (i, P_C // (2 * SGU_W))),
                  pl.BlockSpec((1, SGU_W), lambda i: (0, 0)), pl.BlockSpec((4, SGU_CHUNK, SGU_CHUNK), lambda i: (0, 0, 0)),
                  pl.BlockSpec((SGU_CHUNK, SGU_W), lambda i: (0, 0))],
        out_specs=pl.BlockSpec((SGU_CHUNK, SGU_W), lambda i: (i, 0)),
        compiler_params=_cp(("parallel",)), name=name)(proj, gn, wm, bias)


def _sgu_bwd(dsg, proj, gn, wm, wmt, bias, name):
    S = proj.shape[0]

    def body(dsg_ref, c_ref, g_ref, w_ref, wt_ref, b_ref, dc_ref, dw_ref, db_ref, dg_ref):
        i = pl.program_id(0)

        @pl.when(i == 0)
        def _():
            dw_ref[...] = jnp.zeros_like(dw_ref)
            db_ref[...] = jnp.zeros_like(db_ref)
            dg_ref[...] = jnp.zeros_like(dg_ref)

        c = c_ref[...]
        gn_v = g_ref[...]
        u, rstd, vhat, vnb, grp, mixed = _sgu_parts(c, gn_v, w_ref, b_ref[...])
        dsg_v = dsg_ref[...]
        du = dsg_v * mixed
        dmix = dsg_v * u
        db_ref[...] += dmix
        dmb = dmix.astype(BF16)
        dvn = jnp.zeros((SGU_CHUNK, SGU_W), F32)
        for gi in range(4):
            dmg = jnp.where(grp == gi, dmb, jnp.zeros_like(dmb))
            dw_ref[gi] += lax.dot_general(dmg, vnb, (((1,), (1,)), ((), ())), preferred_element_type=F32)
            dvn = dvn + jnp.where(grp == gi, jnp.dot(wt_ref[gi], dmb, preferred_element_type=F32), 0.0)
        dg_ref[...] += jnp.sum(dvn * vhat, axis=0, keepdims=True)
        t = dvn * gn_v
        dvv = rstd * (t - vhat * jnp.mean(t * vhat, axis=-1, keepdims=True))
        dc_ref[...] = (jnp.concatenate([du, dvv], axis=1) * _gelu_grad(c)).astype(BF16)

    w_spec = pl.BlockSpec((4, SGU_CHUNK, SGU_CHUNK), lambda i: (0, 0, 0))
    tile = pl.BlockSpec((SGU_CHUNK, SGU_W), lambda i: (0, 0))
    vec = pl.BlockSpec((1, SGU_W), lambda i: (0, 0))
    return pl.pallas_call(
        body, out_shape=(jax.ShapeDtypeStruct((S, 2 * SGU_W), BF16), jax.ShapeDtypeStruct((4, SGU_CHUNK, SGU_CHUNK), F32),
                         jax.ShapeDtypeStruct((SGU_CHUNK, SGU_W), F32), jax.ShapeDtypeStruct((1, SGU_W), F32)),
        grid=(S // SGU_CHUNK,),
        in_specs=[pl.BlockSpec((SGU_CHUNK, SGU_W), lambda i: (i, 0)),
                  pl.BlockSpec((SGU_CHUNK, 2 * SGU_W), lambda i: (i, P_C // (2 * SGU_W))), vec, w_spec, w_spec, tile],
        out_specs=(pl.BlockSpec((SGU_CHUNK, 2 * SGU_W), lambda i: (i, 0)), w_spec, tile, vec),
        compiler_params=_cp(("arbitrary",)), name=name)(dsg, proj, gn, wm, wmt, bias)


def _sigmoid(z):
    return 1.0 / (1.0 + jnp.exp(-z))


def _merge_specs(tm):
    row = lambda n: pl.BlockSpec((tm, n), lambda i: (i, 0))
    gate = lambda b: pl.BlockSpec((tm, D), lambda i: (i, b))
    full = lambda r, c: pl.BlockSpec((r, c), lambda i: (0, 0))
    return row, gate, full


def _merge_fwd(proj, ya, o, sg, wa, wb, wc, bg, name, tm=256):
    S = proj.shape[0]
    row, gate, full = _merge_specs(tm)

    def body(g0, g1, g2, ya_ref, o_ref, sg_ref, wa_ref, wb_ref, wc_ref, bg_ref, out_ref):
        acc = jnp.zeros((tm, D), F32)
        for b, (g_ref, br_ref, w_ref) in enumerate(((g0, ya_ref, wa_ref), (g1, o_ref, wb_ref), (g2, sg_ref, wc_ref))):
            y = jnp.dot(br_ref[...], w_ref[...], preferred_element_type=F32)
            acc = acc + _sigmoid(g_ref[...] + bg_ref[:, b * D:(b + 1) * D]) * y
        out_ref[...] = acc.astype(BF16)

    return pl.pallas_call(
        body, out_shape=jax.ShapeDtypeStruct((S, D), BF16), grid=(S // tm,),
        in_specs=[gate(0), gate(1), gate(2), row(POOL_W), row(FOX_W), row(SGU_W), full(POOL_W, D), full(FOX_W, D),
                  full(SGU_W, D), full(1, 3 * D)],
        out_specs=row(D), compiler_params=_cp(("parallel",)), name=name)(proj, proj, proj, ya, o, sg, wa, wb, wc, bg)


def _merge_bwd(dm, proj, ya, o, sg, wa, wb, wc, bg, name, tm=256):
    S = proj.shape[0]
    row, gate, full = _merge_specs(tm)

    def body(dm_ref, g0, g1, g2, ya_ref, o_ref, sg_ref, wa_ref, wb_ref, wc_ref, bg_ref,
             dg_ref, dya_ref, do_ref, dsg_ref, dwa_ref, dwb_ref, dwc_ref, dbg_ref, awa, awb, awc):
        i = pl.program_id(0)

        @pl.when(i == 0)
        def _():
            awa[...] = jnp.zeros_like(awa)
            awb[...] = jnp.zeros_like(awb)
            awc[...] = jnp.zeros_like(awc)
            dbg_ref[...] = jnp.zeros_like(dbg_ref)

        dmv = dm_ref[...]
        for b, (g_ref, br_ref, w_ref, dbr_ref, acc_ref) in enumerate(
                ((g0, ya_ref, wa_ref, dya_ref, awa), (g1, o_ref, wb_ref, do_ref, awb), (g2, sg_ref, wc_ref, dsg_ref, awc))):
            br = br_ref[...]
            wv = w_ref[...]
            y = jnp.dot(br, wv, preferred_element_type=F32)
            gt = _sigmoid(g_ref[...] + bg_ref[:, b * D:(b + 1) * D])
            dgp = dmv * y * gt * (1.0 - gt)
            dg_ref[:, b * D:(b + 1) * D] = dgp.astype(BF16)
            dbg_ref[:, b * D:(b + 1) * D] += jnp.sum(dgp, axis=0, keepdims=True)
            dy = (dmv * gt).astype(BF16)
            dbr_ref[...] = lax.dot_general(dy, wv, (((1,), (1,)), ((), ())), preferred_element_type=F32)
            acc_ref[...] += lax.dot_general(br, dy, (((0,), (0,)), ((), ())), preferred_element_type=F32)

        @pl.when(i == pl.num_programs(0) - 1)
        def _():
            dwa_ref[...] = awa[...].astype(BF16)
            dwb_ref[...] = awb[...].astype(BF16)
            dwc_ref[...] = awc[...].astype(BF16)

    return pl.pallas_call(
        body, out_shape=(jax.ShapeDtypeStruct((S, 3 * D), BF16), jax.ShapeDtypeStruct((S, POOL_W), F32),
                         jax.ShapeDtypeStruct((S, FOX_W), F32), jax.ShapeDtypeStruct((S, SGU_W), F32),
                         jax.ShapeDtypeStruct((POOL_W, D), BF16), jax.ShapeDtypeStruct((FOX_W, D), BF16),
                         jax.ShapeDtypeStruct((SGU_W, D), BF16), jax.ShapeDtypeStruct((1, 3 * D), F32)),
        grid=(S // tm,),
        in_specs=[row(D), gate(0), gate(1), gate(2), row(POOL_W), row(FOX_W), row(SGU_W), full(POOL_W, D), full(FOX_W, D),
                  full(SGU_W, D), full(1, 3 * D)],
        out_specs=(row(3 * D), row(POOL_W), row(FOX_W), row(SGU_W), full(POOL_W, D), full(FOX_W, D), full(SGU_W, D),
                   full(1, 3 * D)),
        scratch_shapes=[pltpu.VMEM((POOL_W, D), F32), pltpu.VMEM((FOX_W, D), F32), pltpu.VMEM((SGU_W, D), F32)],
        compiler_params=_cp(("arbitrary",)), name=name)(dm, proj, proj, proj, ya, o, sg, wa, wb, wc, bg)


def _xattn_probs(qh, kh):
    s = lax.dot_general(qh, kh, (((1,), (1,)), ((), ())), preferred_element_type=F32) * X_SCALE
    p = jnp.exp(s - jnp.max(s, axis=-1, keepdims=True))
    return p / jnp.sum(p, axis=-1, keepdims=True)


def _xattn_fwd(xq, kv, name, tq=256):
    S = xq.shape[0]
    M = kv.shape[0]

    def body(q_ref, k_ref, v_ref, o_ref):
        for h in range(XH):
            sl = slice(h * XHD, (h + 1) * XHD)
            p = _xattn_probs(q_ref[:, sl], k_ref[:, sl])
            o_ref[:, sl] = jnp.dot(p.astype(BF16), v_ref[:, sl], preferred_element_type=F32).astype(BF16)

    return pl.pallas_call(
        body, out_shape=jax.ShapeDtypeStruct((S, D), BF16), grid=(S // tq,),
        in_specs=[pl.BlockSpec((tq, D), lambda i: (i, 0)), pl.BlockSpec((M, D), lambda i: (0, 0)),
                  pl.BlockSpec((M, D), lambda i: (0, 1))],
        out_specs=pl.BlockSpec((tq, D), lambda i: (i, 0)), compiler_params=_cp(("parallel",)), name=name)(xq, kv, kv)


def _xattn_bwd(xq, kv, do, name, tq=256):
    S = xq.shape[0]
    M = kv.shape[0]

    def body(q_ref, k_ref, v_ref, do_ref, dq_ref, dkv_ref, dk_acc, dv_acc):
        i = pl.program_id(0)

        @pl.when(i == 0)
        def _():
            dk_acc[...] = jnp.zeros_like(dk_acc)
            dv_acc[...] = jnp.zeros_like(dv_acc)

        for h in range(XH):
            sl = slice(h * XHD, (h + 1) * XHD)
            qh, kh, vh, doh = q_ref[:, sl], k_ref[:, sl], v_ref[:, sl], do_ref[:, sl]
            p = _xattn_probs(qh, kh)
            dp = lax.dot_general(doh, vh, (((1,), (1,)), ((), ())), preferred_element_type=F32)
            ds = p * (dp - jnp.sum(p * dp, axis=-1, keepdims=True))
            dsb = (ds * X_SCALE).astype(BF16)
            dq_ref[:, sl] = jnp.dot(dsb, kh, preferred_element_type=F32).astype(BF16)
            dk_acc[:, sl] += lax.dot_general(dsb, qh, (((0,), (0,)), ((), ())), preferred_element_type=F32)
            dv_acc[:, sl] += lax.dot_general(p.astype(BF16), doh, (((0,), (0,)), ((), ())), preferred_element_type=F32)

        @pl.when(i == pl.num_programs(0) - 1)
        def _():
            dkv_ref[:, :D] = dk_acc[...].astype(BF16)
            dkv_ref[:, D:] = dv_acc[...].astype(BF16)

    return pl.pallas_call(
        body, out_shape=(jax.ShapeDtypeStruct((S, D), BF16), jax.ShapeDtypeStruct((M, 2 * D), BF16)), grid=(S // tq,),
        in_specs=[pl.BlockSpec((tq, D), lambda i: (i, 0)), pl.BlockSpec((M, D), lambda i: (0, 0)),
                  pl.BlockSpec((M, D), lambda i: (0, 1)), pl.BlockSpec((tq, D), lambda i: (i, 0))],
        out_specs=(pl.BlockSpec((tq, D), lambda i: (i, 0)), pl.BlockSpec((M, 2 * D), lambda i: (0, 0))),
        scratch_shapes=[pltpu.VMEM((M, D), F32), pltpu.VMEM((M, D), F32)],
        compiler_params=_cp(("arbitrary",)), name=name)(xq, kv, kv, do)


def _adamw(g, w, m, v, name, tr=256):
    L, r, c = g.shape
    tr = tr if r % tr == 0 else r
    c1 = 1.0 - ADAM_B1 ** ADAM_STEP
    c2 = 1.0 - ADAM_B2 ** ADAM_STEP

    def body(g_ref, w_ref, m_ref, v_ref, d_ref, nm_ref, nv_ref):
        gv = g_ref[...]
        nm = ADAM_B1 * m_ref[...] + (1.0 - ADAM_B1) * gv
        nv = ADAM_B2 * v_ref[...] + (1.0 - ADAM_B2) * (gv * gv)
        nm_ref[...] = nm
        nv_ref[...] = nv
        d_ref[...] = -ADAM_LR * ((nm / c1) / (jnp.sqrt(nv / c2) + ADAM_EPS) + ADAM_WD * w_ref[...])

    blk = pl.BlockSpec((1, tr, c), lambda l, i: (l, i, 0))
    return pl.pallas_call(
        body, out_shape=(jax.ShapeDtypeStruct(g.shape, F32),) * 3, grid=(L, r // tr),
        in_specs=[blk] * 4, out_specs=(blk,) * 3, compiler_params=_cp(("parallel", "parallel")), name=name)(g, w, m, v)


def _sum_slots(a, out_dtype, name, tr=496):
    n, R, C = a.shape
    tr = tr if R % tr == 0 else R

    def body(a_ref, o_ref):
        acc = a_ref[0].astype(F32)
        for k in range(1, n):
            acc = acc + a_ref[k].astype(F32)
        o_ref[...] = acc.astype(out_dtype)

    return pl.pallas_call(
        body, out_shape=jax.ShapeDtypeStruct((R, C), out_dtype), grid=(R // tr,),
        in_specs=[pl.BlockSpec((n, tr, C), lambda i: (0, i, 0))], out_specs=pl.BlockSpec((tr, C), lambda i: (i, 0)),
        compiler_params=_cp(("parallel",)), name=name)(a)


def _add_pair(a, b, name, tr=496):
    n, R, C = a.shape
    tr = tr if R % tr == 0 else R

    def body(a_ref, b_ref, o_ref):
        o_ref[...] = (a_ref[...].astype(F32) + b_ref[...].astype(F32)).astype(BF16)

    blk = pl.BlockSpec((1, tr, C), lambda k, i: (k, i, 0))
    return pl.pallas_call(
        body, out_shape=jax.ShapeDtypeStruct(a.shape, BF16), grid=(n, R // tr), in_specs=[blk, blk], out_specs=blk,
        compiler_params=_cp(("parallel", "parallel")), name=name)(a, b)


ANY = pl.BlockSpec(memory_space=pl.ANY)
LANDING = pl.BlockSpec(memory_space=pltpu.VMEM)


def _landing_params(shape, dtype):
    return pltpu.CompilerParams(vmem_limit_bytes=math.prod(shape) * jnp.dtype(dtype).itemsize + 4 * 1024 * 1024)


def _place():
    return lax.axis_index("x"), lax.axis_index("y"), lax.axis_index("c")


def _other_chips(x, y):
    return [(1 - x, y), (x, 1 - y), (1 - x, 1 - y)]


def _row_chunks(rows, want, align=16):
    n = want
    while n > 1 and rows % (n * align):
        n -= 1
    return n


def _gather_weights(shard, name, nch=5):
    R, C = shard.shape
    half = R // 2
    nch = _row_chunks(half, nch)
    cr = half // nch

    def body(s_ref, o_ref, send_sems, recv_sems, local_sem):
        x, y, c = _place()
        j = 2 * x + y
        mine0 = c * half
        theirs0 = (1 - c) * half

        def rows(jj, r0, q):
            return o_ref.at[jj, pl.ds(pl.multiple_of(r0 + q * cr, 16), cr), :]

        def copy(k, src, dst, to):
            return pltpu.make_async_remote_copy(src_ref=src, dst_ref=dst, send_sem=send_sems.at[k], recv_sem=recv_sems.at[k],
                                                device_id=to, device_id_type=MESH)

        own = pltpu.make_async_copy(s_ref, o_ref.at[j], local_sem)
        own.start()
        chips = _other_chips(x, y)
        first = []
        for q in range(nch):
            for k, (px, py) in enumerate(chips):
                src = s_ref.at[pl.ds(pl.multiple_of(mine0 + q * cr, 16), cr), :]
                first.append(copy(k * nch + q, src, rows(j, mine0, q), (px, py, c)))
        for cp in first:
            cp.start()
        passed = []
        for q in range(nch):
            for k, (px, py) in enumerate(chips):
                jj = 2 * px + py
                copy(k * nch + q, rows(jj, mine0, q), rows(jj, mine0, q), (px, py, c)).wait_recv()
                fw = copy((3 + k) * nch + q, rows(jj, mine0, q), rows(jj, mine0, q), (x, y, 1 - c))
                fw.start()
                passed.append(fw)
        for q in range(nch):
            for k, (px, py) in enumerate(chips):
                jj = 2 * px + py
                copy((3 + k) * nch + q, rows(jj, theirs0, q), rows(jj, theirs0, q), (x, y, 1 - c)).wait_recv()
        for cp in first + passed:
            cp.wait_send()
        own.wait()

    return pl.pallas_call(
        body, out_shape=jax.ShapeDtypeStruct((4, R, C), shard.dtype), in_specs=[ANY], out_specs=LANDING,
        scratch_shapes=[pltpu.SemaphoreType.DMA((6 * nch,)), pltpu.SemaphoreType.DMA((6 * nch,)), pltpu.SemaphoreType.DMA],
        compiler_params=_landing_params((4, R, C), shard.dtype), name=name)(shard)


def _pair_split(g, name, nch=5):
    n, R, C = g.shape
    half = R // 2
    nch = _row_chunks(half, nch)
    cr = half // nch

    def body(g_ref, own_ref, got_ref, send_sems, recv_sems, local_sem):
        x, y, c = _place()
        mine0 = pl.multiple_of(c * half, 16)
        theirs0 = (1 - c) * half
        keep = pltpu.make_async_copy(g_ref.at[:, pl.ds(mine0, half), :], own_ref, local_sem)
        keep.start()
        cps = []
        for s in range(n):
            for q in range(nch):
                src = g_ref.at[s, pl.ds(pl.multiple_of(theirs0 + q * cr, 16), cr), :]
                cps.append(pltpu.make_async_remote_copy(
                    src_ref=src, dst_ref=got_ref.at[s, pl.ds(q * cr, cr), :], send_sem=send_sems.at[s * nch + q],
                    recv_sem=recv_sems.at[s * nch + q], device_id=(x, y, 1 - c), device_id_type=MESH))
        for cp in cps:
            cp.start()
        for cp in cps:
            cp.wait()
        keep.wait()

    sh = jax.ShapeDtypeStruct((n, half, C), g.dtype)
    return pl.pallas_call(
        body, out_shape=(sh, sh), in_specs=[ANY], out_specs=(ANY, LANDING),
        scratch_shapes=[pltpu.SemaphoreType.DMA((n * nch,)), pltpu.SemaphoreType.DMA((n * nch,)), pltpu.SemaphoreType.DMA],
        compiler_params=_landing_params(sh.shape, g.dtype), name=name)(g)


def _chip_all_to_all(p, name, nch=5):
    R = p.shape[1]
    nch = _row_chunks(R, nch)
    cr = R // nch

    def body(p_ref, o_ref, send_sems, recv_sems, local_sem):
        x, y, c = _place()
        j = 2 * x + y
        own = pltpu.make_async_copy(p_ref.at[j], o_ref.at[j], local_sem)
        own.start()
        cps = []
        for q in range(nch):
            for k, (px, py) in enumerate(_other_chips(x, y)):
                cps.append(pltpu.make_async_remote_copy(
                    src_ref=p_ref.at[2 * px + py, pl.ds(q * cr, cr), :], dst_ref=o_ref.at[j, pl.ds(q * cr, cr), :],
                    send_sem=send_sems.at[k * nch + q], recv_sem=recv_sems.at[k * nch + q], device_id=(px, py, c),
                    device_id_type=MESH))
        for cp in cps:
            cp.start()
        for cp in cps:
            cp.wait()
        own.wait()

    return pl.pallas_call(
        body, out_shape=jax.ShapeDtypeStruct(p.shape, p.dtype), in_specs=[ANY], out_specs=LANDING,
        scratch_shapes=[pltpu.SemaphoreType.DMA((3 * nch,)), pltpu.SemaphoreType.DMA((3 * nch,)), pltpu.SemaphoreType.DMA],
        compiler_params=_landing_params(p.shape, p.dtype), name=name)(p)


def _pair_gather(t, name, nch=10):
    R = t.shape[0]
    nch = _row_chunks(R, nch, 8)
    cr = R // nch

    def body(t_ref, o_ref, send_sems, recv_sems, local_sem):
        x, y, c = _place()
        own = pltpu.make_async_copy(t_ref, o_ref.at[c], local_sem)
        own.start()
        cps = [pltpu.make_async_remote_copy(src_ref=t_ref.at[pl.ds(q * cr, cr), :], dst_ref=o_ref.at[c, pl.ds(q * cr, cr), :],
                                            send_sem=send_sems.at[q], recv_sem=recv_sems.at[q], device_id=(x, y, 1 - c),
                                            device_id_type=MESH) for q in range(nch)]
        for cp in cps:
            cp.start()
        for cp in cps:
            cp.wait()
        own.wait()

    return pl.pallas_call(
        body, out_shape=jax.ShapeDtypeStruct((2,) + t.shape, t.dtype), in_specs=[ANY], out_specs=LANDING,
        scratch_shapes=[pltpu.SemaphoreType.DMA((nch,)), pltpu.SemaphoreType.DMA((nch,)), pltpu.SemaphoreType.DMA],
        compiler_params=_landing_params((2,) + t.shape, t.dtype), name=name)(t)


def _reduce_scatter(g, tag):
    own, got = _pair_split(g, f"rs_pair_{tag}")
    p = _add_pair(own, got, f"rs_add_{tag}")
    q = _chip_all_to_all(p, f"rs_a2a_{tag}")
    t = _sum_slots(q, F32, f"rs_sum_{tag}")
    both = _pair_gather(t, f"rs_join_{tag}")
    return both.reshape(g.shape[1], g.shape[2])


def _all_reduce_small(v, tag):
    pair = _pair_gather(v, f"ar_pair_{tag}")
    p = _sum_slots(pair, F32, f"ar_add_{tag}")
    q = _chip_all_to_all(jnp.broadcast_to(p[None], (4,) + p.shape), f"ar_a2a_{tag}")
    return _sum_slots(q, F32, f"ar_sum_{tag}")


def _pack_rows(n):
    return -(-n // 16) * 16


def _pack_shards(parts):
    flat = [p.reshape(-1, PACK_COLS) for p in parts]
    flat = [jnp.pad(f, ((0, _pack_rows(f.shape[0]) - f.shape[0]), (0, 0))) for f in flat]
    used = sum(f.shape[0] for f in flat)
    flat.append(jnp.zeros((PACK_ROWS - used, PACK_COLS), flat[0].dtype))
    return jnp.concatenate(flat, axis=0)


def _unpack_shard(buf):
    out, r = [], 0
    for _, shape, axis in BIG:
        ss = _shard_shape(shape, axis)
        n = ss[0] * ss[1] // PACK_COLS
        out.append(buf[r:r + n].reshape(ss))
        r += _pack_rows(n)
    return out


def _unpack_full(gathered):
    per_chip = [_unpack_shard(gathered[j]) for j in range(4)]
    return {name: jnp.concatenate([per_chip[j][i] for j in range(4)], axis=axis) for i, (name, _, axis) in enumerate(BIG)}


def _pack_full(grads):
    slots = []
    for j in range(4):
        parts = []
        for name, shape, axis in BIG:
            n = shape[axis] // 4
            parts.append(lax.slice_in_dim(grads[name], j * n, (j + 1) * n, axis=axis))
        slots.append(_pack_shards(parts))
    return jnp.stack(slots)


def _pad_w_in(w):
    return jnp.concatenate([w[:, 2312:5384], w[:, 256:1792], w[:, 1800:2312], w[:, 0:256], w[:, 1792:1800],
                            jnp.zeros((w.shape[0], NP - N_IN), w.dtype)], axis=1)


def _unpad_w_in(w):
    return jnp.concatenate([w[:, P_A:P_A + 256], w[:, P_Q:P_Q + 1536], w[:, P_F:P_F + 8], w[:, P_C:P_C + 512], w[:, P_G:P_G + 3072]],
                           axis=1)


def _small_prep(sw, l):
    eye = jnp.eye(4, dtype=F32)
    bd = jnp.einsum('gh,gcd->gchd', eye, sw['pool_w'][l]).reshape(POOL_W, POOL_W).astype(BF16)
    tril = jnp.tril(jnp.ones((SGU_CHUNK, SGU_CHUNK), F32))
    wm = (sw['sgu_w'][l] * tril[None]).astype(BF16)
    return dict(
        g_mix=sw['norm_mix_g'][l][None], g_x=sw['norm_xattn_g'][l][None], g_mem=sw['norm_mem_g'][l][None],
        g_ffn=sw['norm_ffn_g'][l][None], bd=bd, pool_scale=sw['pool_scale'][l][None],
        bf=jnp.pad(sw['b_forget'][l], (0, FCOLS - 8))[None], sgu_g=sw['sgu_norm_g'][l][None], wm=wm,
        wmt=jnp.transpose(wm, (0, 2, 1)), sgu_bias=jnp.repeat(sw['sgu_b'][l].T, 64, axis=1), bg=sw['b_gate'][l][None])


def _layer_fwd(x, mem, W, sp, l):
    t = f"l{l}"
    S = x.shape[0]
    h = _rms_fwd(x, sp['g_mix'], f"rms_mix_{t}")
    proj = _mm(h, W['w_in_p'], name=f"proj_{t}", out_dtype=F32)
    d, ya = _pool_fwd(proj, sp['bd'], sp['pool_scale'], f"pool_fwd_{t}")
    fcum = _fgate_fwd(proj, sp['bf'], f"fgate_fwd_{t}")
    f8 = fcum[:, :8]
    fcol = f8.reshape(S, 4, 2).transpose(1, 0, 2)
    frow = f8.T.reshape(4, 2, S)
    qkv = proj[:, P_Q:P_Q + 3 * FOX_W].astype(BF16)
    o, lse = _fox_fwd(qkv, fcol, frow, f"fox_fwd_{t}")
    sg = _sgu_fwd(proj, sp['sgu_g'], sp['wm'], sp['sgu_bias'], f"sgu_fwd_{t}")
    merged = _merge_fwd(proj, ya, o, sg, W['w_branch_a'], W['w_branch_b'], W['w_branch_c'], sp['bg'], f"merge_fwd_{t}")
    x1 = _mm(merged, W['w_out'], name=f"out_{t}", out_dtype=F32, extra=x, epi=lambda r, e: e + r)
    hx = _rms_fwd(x1, sp['g_x'], f"rms_x_{t}")
    hm = _rms_fwd(mem, sp['g_mem'], f"rms_mem_{t}")
    xq = _mm(hx, W['w_xq'], name=f"xq_{t}", out_dtype=BF16)
    kv = _mm(hm, W['w_xkv'], name=f"xkv_{t}", out_dtype=BF16)
    o2 = _xattn_fwd(xq, kv, f"xattn_fwd_{t}")
    x2 = _mm(o2, W['w_xo'], name=f"xo_{t}", out_dtype=F32, extra=x1, epi=lambda r, e: e + r)
    hf = _rms_fwd(x2, sp['g_ffn'], f"rms_ffn_{t}")
    z = _mm(hf, W['w_ff1'], name=f"ff1_{t}", out_dtype=F32)
    x3 = _mm(z, W['w_ff2'], name=f"ff2_{t}", out_dtype=F32, a_fn=_relu2, extra=x2, epi=lambda r, e: e + r)
    saved = dict(x=x, h=h, proj=proj, d=d, ya=ya, fcol=fcol, frow=frow, qkv=qkv, o=o, lse=lse, sg=sg, merged=merged, x1=x1,
                 hx=hx, hm=hm, xq=xq, kv=kv, o2=o2, x2=x2, hf=hf, z=z)
    return x3, saved


def _layer_bwd(dx3, mem, W, sp, sv, l):
    t = f"l{l}"
    S = dx3.shape[0]
    gb, gs = {}, {}
    dz = _mm(dx3, W['w_ff2'], name=f"d_a2_{t}", out_dtype=BF16, tb=True, extra=sv['z'],
             epi=lambda r, e: r * (2.0 * jnp.maximum(e, 0.0)))
    gb['w_ff2'] = _mm(sv['z'], dx3, name=f"dw_ff2_{t}", out_dtype=BF16, ta=True, a_fn=_relu2)
    gb['w_ff1'] = _mm(sv['hf'], dz, name=f"dw_ff1_{t}", out_dtype=BF16, ta=True)
    dhf = _mm(dz, W['w_ff1'], name=f"d_hf_{t}", out_dtype=F32, tb=True)
    dx2, gs['norm_ffn_g'] = _rms_bwd(dhf, sv['x2'], sp['g_ffn'], dx3, f"rms_ffn_bwd_{t}")
    do2 = _mm(dx2, W['w_xo'], name=f"d_o2_{t}", out_dtype=BF16, tb=True)
    gb['w_xo'] = _mm(sv['o2'], dx2, name=f"dw_xo_{t}", out_dtype=BF16, ta=True)
    dxq, dkv = _xattn_bwd(sv['xq'], sv['kv'], do2, f"xattn_bwd_{t}")
    gb['w_xkv'] = _mm(sv['hm'], dkv, name=f"dw_xkv_{t}", out_dtype=BF16, ta=True)
    dhm = _mm(dkv, W['w_xkv'], name=f"d_hm_{t}", out_dtype=F32, tb=True)
    gs['norm_mem_g'] = _rms_bwd(dhm, mem, sp['g_mem'], None, f"rms_mem_bwd_{t}")
    gb['w_xq'] = _mm(sv['hx'], dxq, name=f"dw_xq_{t}", out_dtype=BF16, ta=True)
    dhx = _mm(dxq, W['w_xq'], name=f"d_hx_{t}", out_dtype=F32, tb=True)
    dx1, gs['norm_xattn_g'] = _rms_bwd(dhx, sv['x1'], sp['g_x'], dx2, f"rms_x_bwd_{t}")
    gb['w_out'] = _mm(sv['merged'], dx1, name=f"dw_out_{t}", out_dtype=BF16, ta=True)
    dm = _mm(dx1, W['w_out'], name=f"d_merged_{t}", out_dtype=F32, tb=True)
    dg, dya, do, dsg, gb['w_branch_a'], gb['w_branch_b'], gb['w_branch_c'], gs['b_gate'] = _merge_bwd(
        dm, sv['proj'], sv['ya'], sv['o'], sv['sg'], W['w_branch_a'], W['w_branch_b'], W['w_branch_c'], sp['bg'], f"merge_bwd_{t}")
    dc, dws, dbias, gs['sgu_norm_g'] = _sgu_bwd(dsg, sv['proj'], sp['sgu_g'], sp['wm'], sp['wmt'], sp['sgu_bias'], f"sgu_bwd_{t}")
    tril = jnp.tril(jnp.ones((SGU_CHUNK, SGU_CHUNK), F32))
    gs['sgu_w'] = dws * tril[None]
    gs['sgu_b'] = dbias.reshape(SGU_CHUNK, 4, 64).sum(-1).T
    dq, dk, dv, dfrow = _fox_bwd(sv['qkv'], do, sv['lse'], sv['fcol'], sv['frow'], f"fox_bwd_{t}")
    dF = jnp.pad(dfrow.reshape(8, S).T, ((0, 0), (0, FCOLS - 8)))
    df, dbf = _fgate_bwd(dF, sv['proj'], sp['bf'], f"fgate_bwd_{t}")
    gs['b_forget'] = dbf[:, :8]
    da, dbd, gs['pool_scale'] = _pool_bwd(dya, sv['d'], sp['bd'], sp['pool_scale'], f"pool_bwd_{t}")
    gs['pool_w'] = jnp.stack([dbd[g * 64:(g + 1) * 64, g * 64:(g + 1) * 64] for g in range(4)])
    dproj = jnp.concatenate([dg, dq, dk, dv, dc, da, df], axis=1)
    gb['w_in'] = _unpad_w_in(_mm(sv['h'], dproj, name=f"dw_in_{t}", out_dtype=BF16, ta=True))
    dh = _mm(dproj, W['w_in_p'], name=f"d_h_{t}", out_dtype=F32, tb=True, tk=512)
    dx, gs['norm_mix_g'] = _rms_bwd(dh, sv['x'], sp['g_mix'], dx1, f"rms_mix_bwd_{t}")
    return dx, gb, gs


SMALL_ROWS = 1424


def _pack_small(parts):
    flat = jnp.concatenate([p.reshape(-1) for p in parts])
    return jnp.pad(flat, (0, SMALL_ROWS * 128 - flat.shape[0])).reshape(SMALL_ROWS, 128)


def _unpack_small(buf, shapes):
    flat, out, r = buf.reshape(-1), [], 0
    for s in shapes:
        n = math.prod(s)
        out.append(flat[r:r + n].reshape(s))
        r += n
    return out


def kernel(x, mem, norm_mix_g, w_in, b_forget, pool_w, pool_scale, sgu_norm_g, sgu_w, sgu_b, w_branch_a, w_branch_b, w_branch_c, b_gate, w_out, norm_xattn_g, norm_mem_g, w_xq, w_xkv, w_xo, norm_ffn_g, w_ff1, w_ff2, final_norm_g, loss_target, m_norm_mix_g, m_w_in, m_b_forget, m_pool_w, m_pool_scale, m_sgu_norm_g, m_sgu_w, m_sgu_b, m_w_branch_a, m_w_branch_b, m_w_branch_c, m_b_gate, m_w_out, m_norm_xattn_g, m_norm_mem_g, m_w_xq, m_w_xkv, m_w_xo, m_norm_ffn_g, m_w_ff1, m_w_ff2, m_final_norm_g, v_norm_mix_g, v_w_in, v_b_forget, v_pool_w, v_pool_scale, v_sgu_norm_g, v_sgu_w, v_sgu_b, v_w_branch_a, v_w_branch_b, v_w_branch_c, v_b_gate, v_w_out, v_norm_xattn_g, v_norm_mem_g, v_w_xq, v_w_xkv, v_w_xo, v_norm_ffn_g, v_w_ff1, v_w_ff2, v_final_norm_g):
    args = (norm_mix_g, w_in, b_forget, pool_w, pool_scale, sgu_norm_g, sgu_w, sgu_b, w_branch_a, w_branch_b, w_branch_c, b_gate,
            w_out, norm_xattn_g, norm_mem_g, w_xq, w_xkv, w_xo, norm_ffn_g, w_ff1, w_ff2, final_norm_g)
    margs = (m_norm_mix_g, m_w_in, m_b_forget, m_pool_w, m_pool_scale, m_sgu_norm_g, m_sgu_w, m_sgu_b, m_w_branch_a, m_w_branch_b,
             m_w_branch_c, m_b_gate, m_w_out, m_norm_xattn_g, m_norm_mem_g, m_w_xq, m_w_xkv, m_w_xo, m_norm_ffn_g, m_w_ff1, m_w_ff2,
             m_final_norm_g)
    vargs = (v_norm_mix_g, v_w_in, v_b_forget, v_pool_w, v_pool_scale, v_sgu_norm_g, v_sgu_w, v_sgu_b, v_w_branch_a, v_w_branch_b,
             v_w_branch_c, v_b_gate, v_w_out, v_norm_xattn_g, v_norm_mem_g, v_w_xq, v_w_xkv, v_w_xo, v_norm_ffn_g, v_w_ff1, v_w_ff2,
             v_final_norm_g)
    w = dict(zip(W_NAMES, args))
    mo = dict(zip(W_NAMES, margs))
    vo = dict(zip(W_NAMES, vargs))
    xs, mems, tgt = x[0], mem[0], loss_target[0]

    full = []
    for l in range(DEPTH):
        shard = _pack_shards([w[n][l].astype(BF16) for n in BIG_NAMES])
        Wl = _unpack_full(_gather_weights(shard, f"gather_w_l{l}"))
        Wl['w_in_p'] = _pad_w_in(Wl.pop('w_in'))
        full.append(Wl)
    preps = [_small_prep(w, l) for l in range(DEPTH)]

    act, saved = xs, []
    for l in range(DEPTH):
        act, sv = _layer_fwd(act, mems, full[l], preps[l], l)
        saved.append(sv)
    loss_part, dact, d_final_g = _loss_head(act, w['final_norm_g'][None], tgt, "loss_head")

    big_red, small_g = [None] * DEPTH, [None] * DEPTH
    for l in reversed(range(DEPTH)):
        dact, gb, gs = _layer_bwd(dact, mems, full[l], preps[l], saved[l], l)
        big_red[l] = _unpack_shard(_reduce_scatter(_pack_full(gb), f"l{l}"))
        small_g[l] = gs
    grad_x = dact[None]

    per_layer = [n for n in SMALL_NAMES if n != 'final_norm_g']
    small_shapes = [w[n].shape for n in per_layer] + [(D,), (1,)]
    parts = [jnp.stack([small_g[l][n].reshape(w[n].shape[1:]) for l in range(DEPTH)]) for n in per_layer]
    red = _unpack_small(_all_reduce_small(_pack_small(parts + [d_final_g.reshape(D), loss_part.reshape(1)]), "small"), small_shapes)
    grads = dict(zip(per_layer + ['final_norm_g'], red[:-1]))
    loss = red[-1].reshape(())
    for i, n in enumerate(BIG_NAMES):
        grads[n] = jnp.stack([big_red[l][i] for l in range(DEPTH)])

    delta, new_m, new_v = {}, {}, {}
    for n in BIG_NAMES:
        delta[n], new_m[n], new_v[n] = _adamw(grads[n], w[n], mo[n], vo[n], f"adamw_{n}")
    small_all = per_layer + ['final_norm_g']
    shapes_all = [w[n].shape for n in small_all]
    packed = [_pack_small([d[n] for n in small_all])[None] for d in (grads, w, mo, vo)]
    ds, ms, vs = _adamw(*packed, "adamw_small")
    for n, a, b, c in zip(small_all, _unpack_small(ds[0], shapes_all), _unpack_small(ms[0], shapes_all), _unpack_small(vs[0], shapes_all)):
        delta[n], new_m[n], new_v[n] = a, b, c

    return (loss, grad_x, *[grads[n] for n in W_NAMES], *[delta[n] for n in W_NAMES], *[new_m[n] for n in W_NAMES],
            *[new_v[n] for n in W_NAMES])
```

```python
import math

import jax
import jax.numpy as jnp
from jax import lax
from jax.experimental import pallas as pl
from jax.experimental.pallas import tpu as pltpu

F32 = jnp.float32
BF16 = jnp.bfloat16

D = 1024
DEPTH = 2
POOL_W = 256
FOX_W = 512
SGU_W = 256
SGU_CHUNK = 128
N_IN = 5384
P_G, P_Q, P_K, P_V, P_C, P_A, P_F = 0, 3072, 3584, 4096, 4608, 5120, 5376
NP = 5632
XH, XHD = 4, 256
D_FF = 4096
EPS = 1e-6
NEG = -1e30
FOX_SCALE = 64 ** -0.5
X_SCALE = 256 ** -0.5
GELU_K = math.sqrt(2.0 / math.pi)
GELU_C = 0.044715

ADAM_LR, ADAM_B1, ADAM_B2, ADAM_EPS, ADAM_WD, ADAM_STEP = 0.001, 0.9, 0.999, 1e-08, 0.01, 10

VMEM_LIMIT = 48 * 1024 * 1024
MESH = pl.DeviceIdType.MESH

IN_NAMES = ['x', 'mem', 'norm_mix_g', 'w_in', 'b_forget', 'pool_w', 'pool_scale', 'sgu_norm_g', 'sgu_w', 'sgu_b',
            'w_branch_a', 'w_branch_b', 'w_branch_c', 'b_gate', 'w_out', 'norm_xattn_g', 'norm_mem_g', 'w_xq',
            'w_xkv', 'w_xo', 'norm_ffn_g', 'w_ff1', 'w_ff2', 'final_norm_g']
W_NAMES = IN_NAMES[2:]
BIG_NAMES = ['w_in', 'w_branch_a', 'w_branch_b', 'w_branch_c', 'w_out', 'w_xq', 'w_xkv', 'w_xo', 'w_ff1', 'w_ff2']
SMALL_NAMES = [n for n in W_NAMES if n not in BIG_NAMES]
PACK_COLS = 1024
PACK_ROWS = 4960


ANY = pl.BlockSpec(memory_space=pl.ANY)


def _cp(sem=None):
    return pltpu.CompilerParams(dimension_semantics=sem, vmem_limit_bytes=VMEM_LIMIT)


def _mm(a, b, *, name, out_dtype, ta=False, tb=False, tm=1024, tn=512, tk=1024, a_fn=None, extra=None, epi=None,
        n=None, k=None, b_block=None, b_index=None, into=None, o_block=None, o_index=None):
    M = a.shape[1] if ta else a.shape[0]
    K = k if k is not None else (a.shape[0] if ta else a.shape[1])
    N = n if n is not None else (b.shape[0] if tb else b.shape[1])
    tm, tn, tk = min(tm, M), min(tn, N), min(tk, K)
    assert M % tm == 0 and N % tn == 0 and K % tk == 0, (name, M, N, K)
    nk = K // tk
    a_spec = pl.BlockSpec((tk, tm), lambda i, j, k: (k, i)) if ta else pl.BlockSpec((tm, tk), lambda i, j, k: (i, k))
    if b_block is not None:
        b_spec = pl.BlockSpec(b_block, b_index)
    else:
        b_spec = pl.BlockSpec((tn, tk), lambda i, j, k: (j, k)) if tb else pl.BlockSpec((tk, tn), lambda i, j, k: (k, j))
    dn = (((0 if ta else 1,), (1 if tb else 0,)), ((), ()))
    tile = pl.BlockSpec((tm, tn), lambda i, j, k: (i, j))
    o_spec = pl.BlockSpec(o_block, o_index) if into is not None else tile
    in_specs = [a_spec, b_spec] + ([tile] if extra is not None else []) + ([ANY] if into is not None else [])
    n_in = len(in_specs)

    def body(*refs):
        a_ref, b_ref = refs[0], refs[1]
        e_ref = refs[2] if extra is not None else None
        o_ref, acc_ref = refs[n_in], refs[n_in + 1]
        kk = pl.program_id(2)

        @pl.when(kk == 0)
        def _():
            acc_ref[...] = jnp.zeros_like(acc_ref)

        av = a_ref[...]
        if a_fn is not None:
            av = a_fn(av)
        bv = b_ref[...]
        if bv.ndim == 3:
            bv = bv.reshape(-1, bv.shape[-1])
        acc_ref[...] += lax.dot_general(av.astype(BF16), bv.astype(BF16), dn, preferred_element_type=F32)

        @pl.when(kk == nk - 1)
        def _():
            r = acc_ref[...]
            if epi is not None:
                r = epi(r, e_ref[...])
            o_ref[...] = r.astype(o_ref.dtype)

    args = (a, b) + ((extra,) if extra is not None else ()) + ((into,) if into is not None else ())
    out_shape = jax.ShapeDtypeStruct(into.shape, into.dtype) if into is not None else jax.ShapeDtypeStruct((M, N), out_dtype)
    return pl.pallas_call(
        body, out_shape=out_shape, grid=(M // tm, N // tn, nk), in_specs=in_specs, out_specs=o_spec,
        scratch_shapes=[pltpu.VMEM((tm, tn), F32)], input_output_aliases={n_in - 1: 0} if into is not None else {},
        compiler_params=_cp(("parallel", "parallel", "arbitrary")), name=name)(*args)


def _relu2(z):
    r = jnp.maximum(z, 0.0)
    return r * r


def _rms_fwd(x, g, name, tr=256):
    R, n = x.shape
    tr = min(tr, R)

    def body(x_ref, g_ref, h_ref):
        xv = x_ref[...]
        rstd = lax.rsqrt(jnp.mean(xv * xv, axis=-1, keepdims=True) + EPS)
        h_ref[...] = (xv * rstd * g_ref[...]).astype(BF16)

    return pl.pallas_call(
        body, out_shape=jax.ShapeDtypeStruct((R, n), BF16), grid=(R // tr,),
        in_specs=[pl.BlockSpec((tr, n), lambda i: (i, 0)), pl.BlockSpec((1, n), lambda i: (0, 0))],
        out_specs=pl.BlockSpec((tr, n), lambda i: (i, 0)), compiler_params=_cp(("parallel",)), name=name)(x, g)


def _rms_bwd(dh, x, g, dres, name, tr=256):
    R, n = x.shape
    tr = min(tr, R)
    need_dx = dres is not None

    def body(*refs):
        if need_dx:
            dh_ref, x_ref, g_ref, r_ref, dx_ref, dg_ref = refs
        else:
            dh_ref, x_ref, g_ref, dg_ref = refs
        i = pl.program_id(0)
        xv = x_ref[...]
        dhv = dh_ref[...].astype(F32)
        rstd = lax.rsqrt(jnp.mean(xv * xv, axis=-1, keepdims=True) + EPS)
        xhat = xv * rstd

        @pl.when(i == 0)
        def _():
            dg_ref[...] = jnp.zeros_like(dg_ref)

        dg_ref[...] += jnp.sum(dhv * xhat, axis=0, keepdims=True)
        if need_dx:
            t = dhv * g_ref[...]
            dx_ref[...] = r_ref[...] + rstd * (t - xhat * jnp.mean(t * xhat, axis=-1, keepdims=True))

    row = pl.BlockSpec((tr, n), lambda i: (i, 0))
    vec = pl.BlockSpec((1, n), lambda i: (0, 0))
    if need_dx:
        return pl.pallas_call(
            body, out_shape=(jax.ShapeDtypeStruct((R, n), F32), jax.ShapeDtypeStruct((1, n), F32)), grid=(R // tr,),
            in_specs=[row, row, vec, row], out_specs=(row, vec), compiler_params=_cp(("arbitrary",)), name=name)(dh, x, g, dres)
    return pl.pallas_call(
        body, out_shape=jax.ShapeDtypeStruct((1, n), F32), grid=(R // tr,),
        in_specs=[row, row, vec], out_specs=vec, compiler_params=_cp(("arbitrary",)), name=name)(dh, x, g)


def _loss_head(x, g, tgt, name, tr=256):
    R, n = x.shape

    def body(x_ref, g_ref, t_ref, loss_ref, dx_ref, dg_ref):
        i = pl.program_id(0)
        xv = x_ref[...]
        gv = g_ref[...]
        rstd = lax.rsqrt(jnp.mean(xv * xv, axis=-1, keepdims=True) + EPS)
        xhat = xv * rstd
        e = xhat * gv - t_ref[...]

        @pl.when(i == 0)
        def _():
            loss_ref[...] = jnp.zeros_like(loss_ref)
            dg_ref[...] = jnp.zeros_like(dg_ref)

        loss_ref[...] += 0.5 * jnp.sum(jnp.sum(e * e, axis=-1, keepdims=True) / n, axis=0, keepdims=True)
        dy = e / n
        dg_ref[...] += jnp.sum(dy * xhat, axis=0, keepdims=True)
        t = dy * gv
        dx_ref[...] = rstd * (t - xhat * jnp.mean(t * xhat, axis=-1, keepdims=True))

    row = pl.BlockSpec((tr, n), lambda i: (i, 0))
    vec = pl.BlockSpec((1, n), lambda i: (0, 0))
    one = pl.BlockSpec((1, 1), lambda i: (0, 0))
    return pl.pallas_call(
        body, out_shape=(jax.ShapeDtypeStruct((1, 1), F32), jax.ShapeDtypeStruct((R, n), F32), jax.ShapeDtypeStruct((1, n), F32)),
        grid=(R // tr,), in_specs=[row, vec, row], out_specs=(one, row, vec),
        compiler_params=_cp(("arbitrary",)), name=name)(x, g, tgt)


def _pool_masks(S):
    row = lax.broadcasted_iota(jnp.int32, (S, POOL_W), 0)
    grp = lax.broadcasted_iota(jnp.int32, (S, POOL_W), 1) // 64
    win = jnp.where(grp == 0, 2, jnp.where(grp == 1, 4, jnp.where(grp == 2, 8, 16)))
    cnt = jnp.minimum(row + 1, win).astype(F32)
    return row, grp, cnt


def _by_group(grp, v0, v1, v2, v3):
    return jnp.where(grp == 0, v0, jnp.where(grp == 1, v1, jnp.where(grp == 2, v2, v3)))


def _pool_fwd(proj, bd, scale, name):
    S = proj.shape[0]

    def body(a_ref, bd_ref, sc_ref, d_ref, y_ref):
        a = a_ref[...]
        row, grp, cnt = _pool_masks(S)

        def back(v, k):
            return jnp.where(row >= k, pltpu.roll(v, k, 0), 0.0)

        s1 = a + back(a, 1)
        s2 = s1 + back(s1, 2)
        s3 = s2 + back(s2, 4)
        s4 = s3 + back(s3, 8)
        d = (_by_group(grp, s1, s2, s3, s4) / cnt - a).astype(BF16)
        d_ref[...] = d
        y_ref[...] = (jnp.dot(d, bd_ref[...], preferred_element_type=F32) * sc_ref[...]).astype(BF16)

    full = lambda r, c: pl.BlockSpec((r, c), lambda i: (0, 0))
    return pl.pallas_call(
        body, out_shape=(jax.ShapeDtypeStruct((S, POOL_W), BF16), jax.ShapeDtypeStruct((S, POOL_W), BF16)), grid=(1,),
        in_specs=[pl.BlockSpec((S, POOL_W), lambda i: (0, P_A // POOL_W)), full(POOL_W, POOL_W), full(1, POOL_W)],
        out_specs=(full(S, POOL_W), full(S, POOL_W)), compiler_params=_cp(("arbitrary",)), name=name)(proj, bd, scale)


def _pool_bwd(dya, d, bd, scale, name):
    S = dya.shape[0]

    def body(dy_ref, d_ref, bd_ref, sc_ref, da_ref, dbd_ref, dsc_ref):
        dy = dy_ref[...]
        dv = d_ref[...]
        bdv = bd_ref[...]
        row, grp, cnt = _pool_masks(S)
        yraw = jnp.dot(dv, bdv, preferred_element_type=F32)
        dsc_ref[...] = jnp.sum(dy * yraw, axis=0, keepdims=True)
        tb = (dy * sc_ref[...]).astype(BF16)
        dbd_ref[...] = lax.dot_general(dv, tb, (((0,), (0,)), ((), ())), preferred_element_type=F32)
        dd = lax.dot_general(tb, bdv, (((1,), (1,)), ((), ())), preferred_element_type=F32)
        e = dd / cnt

        def fwd(v, k):
            return jnp.where(row < S - k, pltpu.roll(v, S - k, 0), 0.0)

        r1 = e + fwd(e, 1)
        r2 = r1 + fwd(r1, 2)
        r3 = r2 + fwd(r2, 4)
        r4 = r3 + fwd(r3, 8)
        da_ref[...] = (_by_group(grp, r1, r2, r3, r4) - dd).astype(BF16)

    full = lambda r, c: pl.BlockSpec((r, c), lambda i: (0, 0))
    return pl.pallas_call(
        body, out_shape=(jax.ShapeDtypeStruct((S, POOL_W), BF16), jax.ShapeDtypeStruct((POOL_W, POOL_W), F32),
                         jax.ShapeDtypeStruct((1, POOL_W), F32)), grid=(1,),
        in_specs=[full(S, POOL_W), full(S, POOL_W), full(POOL_W, POOL_W), full(1, POOL_W)],
        out_specs=(full(S, POOL_W), full(POOL_W, POOL_W), full(1, POOL_W)),
        compiler_params=_cp(("arbitrary",)), name=name)(dya, d, bd, scale)


FCOLS = 128


def _log_sigmoid(z):
    return -(jnp.maximum(-z, 0.0) + jnp.log1p(jnp.exp(-jnp.abs(z))))


def _fgate_fwd(proj, bf, name):
    S = proj.shape[0]

    def body(f_ref, b_ref, o_ref):
        v = _log_sigmoid(f_ref[...] + b_ref[...])
        row = lax.broadcasted_iota(jnp.int32, (S, FCOLS), 0)
        k = 1
        while k < S:
            v = v + jnp.where(row >= k, pltpu.roll(v, k, 0), 0.0)
            k *= 2
        o_ref[...] = v

    return pl.pallas_call(
        body, out_shape=jax.ShapeDtypeStruct((S, FCOLS), F32), grid=(1,),
        in_specs=[pl.BlockSpec((S, FCOLS), lambda i: (0, P_F // FCOLS)), pl.BlockSpec((1, FCOLS), lambda i: (0, 0))],
        out_specs=pl.BlockSpec((S, FCOLS), lambda i: (0, 0)), compiler_params=_cp(("arbitrary",)), name=name)(proj, bf)


def _fgate_bwd(dF, proj, bf, name):
    S = proj.shape[0]

    def body(dF_ref, f_ref, b_ref, df_ref, db_ref):
        v = dF_ref[...]
        row = lax.broadcasted_iota(jnp.int32, (S, FCOLS), 0)
        k = 1
        while k < S:
            v = v + jnp.where(row < S - k, pltpu.roll(v, S - k, 0), 0.0)
            k *= 2
        z = f_ref[...] + b_ref[...]
        df = v * (1.0 / (1.0 + jnp.exp(z)))
        db_ref[...] = jnp.sum(df, axis=0, keepdims=True)
        df_ref[...] = jnp.concatenate([df, jnp.zeros_like(df)], axis=1).astype(BF16)

    return pl.pallas_call(
        body, out_shape=(jax.ShapeDtypeStruct((S, 2 * FCOLS), BF16), jax.ShapeDtypeStruct((1, FCOLS), F32)), grid=(1,),
        in_specs=[pl.BlockSpec((S, FCOLS), lambda i: (0, 0)), pl.BlockSpec((S, FCOLS), lambda i: (0, P_F // FCOLS)),
                  pl.BlockSpec((1, FCOLS), lambda i: (0, 0))],
        out_specs=(pl.BlockSpec((S, 2 * FCOLS), lambda i: (0, 0)), pl.BlockSpec((1, FCOLS), lambda i: (0, 0))),
        compiler_params=_cp(("arbitrary",)), name=name)(dF, proj, bf)


def _fox_scores(qe, kj, fq, fk, r0, c0, tq, tk):
    s = lax.dot_general(qe, kj, (((1,), (1,)), ((), ())), preferred_element_type=F32) * FOX_SCALE
    s = s + (fq - fk)
    rows = r0 + lax.broadcasted_iota(jnp.int32, (tq, tk), 0)
    cols = c0 + lax.broadcasted_iota(jnp.int32, (tq, tk), 1)
    return jnp.where(rows >= cols, s, NEG)


def _fox_fwd(qkv, fcol, frow, name, tq=256):
    S = qkv.shape[0]
    tk = tq

    def body(q_ref, k_ref, v_ref, fc_ref, fr_ref, o_ref, lse_ref):
        i = pl.program_id(1)
        r0 = i * tq
        q = q_ref[...]
        half = lax.broadcasted_iota(jnp.int32, (tq, 128), 1) // 64
        outs = []
        for e in (0, 1):
            qe = jnp.where(half == e, q, jnp.zeros_like(q))
            fq = fc_ref[0, :, e:e + 1]

            def step(j, carry, qe=qe, fq=fq, e=e):
                m, l, acc = carry
                c0 = pl.multiple_of(j * tk, tk)
                kj = k_ref[pl.ds(c0, tk), :]
                vj = v_ref[pl.ds(c0, tk), :]
                fk = fr_ref[0, e:e + 1, pl.ds(c0, tk)]
                s = _fox_scores(qe, kj, fq, fk, r0, c0, tq, tk)
                m_new = jnp.maximum(m, jnp.max(s, axis=-1, keepdims=True))
                alpha = jnp.exp(m - m_new)
                p = jnp.exp(s - m_new)
                l = alpha * l + jnp.sum(p, axis=-1, keepdims=True)
                acc = alpha * acc + jnp.dot(p.astype(BF16), vj, preferred_element_type=F32)
                return m_new, l, acc

            m, l, acc = lax.fori_loop(0, i + 1, step, (jnp.full((tq, 1), NEG, F32), jnp.zeros((tq, 1), F32),
                                                      jnp.zeros((tq, 128), F32)))
            outs.append(acc / l)
            lse_ref[0, :, e:e + 1] = m + jnp.log(l)
        o_ref[...] = jnp.where(half == 0, outs[0], outs[1]).astype(BF16)

    return pl.pallas_call(
        body, out_shape=(jax.ShapeDtypeStruct((S, FOX_W), BF16), jax.ShapeDtypeStruct((4, S, 2), F32)), grid=(4, S // tq),
        in_specs=[pl.BlockSpec((tq, 128), lambda h, i: (i, h)), pl.BlockSpec((S, 128), lambda h, i: (0, 4 + h)),
                  pl.BlockSpec((S, 128), lambda h, i: (0, 8 + h)), pl.BlockSpec((1, tq, 2), lambda h, i: (h, i, 0)),
                  pl.BlockSpec((1, 2, S), lambda h, i: (h, 0, 0))],
        out_specs=(pl.BlockSpec((tq, 128), lambda h, i: (i, h)), pl.BlockSpec((1, tq, 2), lambda h, i: (h, i, 0))),
        compiler_params=_cp(("parallel", "parallel")), name=name)(qkv, qkv, qkv, fcol, frow)


def _fox_bwd(qkv, do, lse, fcol, frow, name, tq=256):
    S = qkv.shape[0]
    tk = tq
    nq = S // tq

    def body(q_ref, k_ref, v_ref, do_ref, lse_ref, fc_ref, fr_ref, dq_ref, dk_ref, dv_ref, dfr_ref, dk_acc, dv_acc):
        dk_acc[...] = jnp.zeros_like(dk_acc)
        dv_acc[...] = jnp.zeros_like(dv_acc)
        dfr_ref[...] = jnp.zeros_like(dfr_ref)
        half = lax.broadcasted_iota(jnp.int32, (tq, 128), 1) // 64

        def q_block(i, _):
            r0 = pl.multiple_of(i * tq, tq)
            qi = q_ref[pl.ds(r0, tq), :]
            dob = do_ref[pl.ds(r0, tq), :].astype(BF16)
            dq_tot = jnp.zeros((tq, 128), F32)
            for e in (0, 1):
                qe = jnp.where(half == e, qi, jnp.zeros_like(qi))
                doe = jnp.where(half == e, dob, jnp.zeros_like(dob))
                lse_e = lse_ref[0, pl.ds(r0, tq), e:e + 1]
                fq = fc_ref[0, pl.ds(r0, tq), e:e + 1]

                def probs(j, qe=qe, doe=doe, lse_e=lse_e, fq=fq, e=e):
                    c0 = pl.multiple_of(j * tk, tk)
                    kj = k_ref[pl.ds(c0, tk), :]
                    vj = v_ref[pl.ds(c0, tk), :]
                    fk = fr_ref[0, e:e + 1, pl.ds(c0, tk)]
                    p = jnp.exp(_fox_scores(qe, kj, fq, fk, r0, c0, tq, tk) - lse_e)
                    dp = lax.dot_general(doe, vj, (((1,), (1,)), ((), ())), preferred_element_type=F32)
                    return c0, kj, p, dp

                def row_term(j, acc, probs=probs):
                    _, _, p, dp = probs(j)
                    return acc + jnp.sum(p * dp, axis=-1, keepdims=True)

                delta = lax.fori_loop(0, i + 1, row_term, jnp.zeros((tq, 1), F32))

                def step(j, dq, probs=probs, delta=delta, e=e):
                    c0, kj, p, dp = probs(j)
                    ds = p * (dp - delta)
                    dfr_ref[0, e:e + 1, pl.ds(c0, tk)] -= jnp.sum(ds, axis=0, keepdims=True)
                    dsb = (ds * FOX_SCALE).astype(BF16)
                    dkc = lax.dot_general(dsb, qi, (((0,), (0,)), ((), ())), preferred_element_type=F32)
                    dvc = lax.dot_general(p.astype(BF16), dob, (((0,), (0,)), ((), ())), preferred_element_type=F32)
                    dk_acc[pl.ds(c0, tk), :] += jnp.where(half == e, dkc, 0.0)
                    dv_acc[pl.ds(c0, tk), :] += jnp.where(half == e, dvc, 0.0)
                    return dq + jnp.dot(dsb, kj, preferred_element_type=F32)

                dq_e = lax.fori_loop(0, i + 1, step, jnp.zeros((tq, 128), F32))
                dq_tot = dq_tot + jnp.where(half == e, dq_e, 0.0)
            dq_ref[pl.ds(r0, tq), :] = dq_tot.astype(BF16)
            return 0

        lax.fori_loop(0, nq, q_block, 0)
        dk_ref[...] = dk_acc[...].astype(BF16)
        dv_ref[...] = dv_acc[...].astype(BF16)

    col = lambda off: pl.BlockSpec((S, 128), lambda h: (0, off + h))
    hs2 = pl.BlockSpec((1, S, 2), lambda h: (h, 0, 0))
    h2s = pl.BlockSpec((1, 2, S), lambda h: (h, 0, 0))
    return pl.pallas_call(
        body, out_shape=(jax.ShapeDtypeStruct((S, FOX_W), BF16),) * 3 + (jax.ShapeDtypeStruct((4, 2, S), F32),), grid=(4,),
        in_specs=[col(0), col(4), col(8), col(0), hs2, hs2, h2s],
        out_specs=(col(0), col(0), col(0), h2s),
        scratch_shapes=[pltpu.VMEM((S, 128), F32), pltpu.VMEM((S, 128), F32)],
        compiler_params=_cp(("parallel",)), name=name)(qkv, qkv, qkv, do, lse, fcol, frow)


def _gelu(x):
    return 0.5 * x * (1.0 + jnp.tanh(GELU_K * (x + GELU_C * x * x * x)))


def _gelu_grad(x):
    th = jnp.tanh(GELU_K * (x + GELU_C * x * x * x))
    return 0.5 * (1.0 + th) + 0.5 * x * (1.0 - th * th) * GELU_K * (1.0 + 3.0 * GELU_C * x * x)


def _sgu_parts(c, gn, w_ref, bias):
    zc = _gelu(c)
    u, vv = zc[:, :SGU_W], zc[:, SGU_W:]
    rstd = lax.rsqrt(jnp.mean(vv * vv, axis=-1, keepdims=True) + EPS)
    vhat = vv * rstd
    vnb = (vhat * gn).astype(BF16)
    grp = lax.broadcasted_iota(jnp.int32, (SGU_CHUNK, SGU_W), 1) // 64
    mixed = bias
    for gi in range(4):
        mixed = mixed + jnp.where(grp == gi, jnp.dot(w_ref[gi], vnb, preferred_element_type=F32), 0.0)
    return u, rstd, vhat, vnb, grp, mixed


def _sgu_fwd(proj, gn, wm, bias, name):
    S = proj.shape[0]

    def body(c_ref, g_ref, w_ref, b_ref, o_ref):
        u, _, _, _, _, mixed = _sgu_parts(c_ref[...], g_ref[...], w_ref, b_ref[...])
        o_ref[...] = (u * mixed).astype(BF16)

    return pl.pallas_call(
        body, out_shape=jax.ShapeDtypeStruct((S, SGU_W), BF16), grid=(S // SGU_CHUNK,),
        in_specs=[pl.BlockSpec((SGU_CHUNK, 2 * SGU_W), lambda i: (i, P_C // (2 * SGU_W))),
                  pl.BlockSpec((1, SGU_W), lambda i: (0, 0)), pl.BlockSpec((4, SGU_CHUNK, SGU_CHUNK), lambda i: (0, 0, 0)),
                  pl.BlockSpec((SGU_CHUNK, SGU_W), lambda i: (0, 0))],
        out_specs=pl.BlockSpec((SGU_CHUNK, SGU_W), lambda i: (i, 0)),
        compiler_params=_cp(("parallel",)), name=name)(proj, gn, wm, bias)


def _sgu_bwd(dsg, proj, gn, wm, wmt, bias, name):
    S = proj.shape[0]

    def body(dsg_ref, c_ref, g_ref, w_ref, wt_ref, b_ref, dc_ref, dw_ref, db_ref, dg_ref):
        i = pl.program_id(0)

        @pl.when(i == 0)
        def _():
            dw_ref[...] = jnp.zeros_like(dw_ref)
            db_ref[...] = jnp.zeros_like(db_ref)
            dg_ref[...] = jnp.zeros_like(dg_ref)

        c = c_ref[...]
        gn_v = g_ref[...]
        u, rstd, vhat, vnb, grp, mixed = _sgu_parts(c, gn_v, w_ref, b_ref[...])
        dsg_v = dsg_ref[...]
        du = dsg_v * mixed
        dmix = dsg_v * u
        db_ref[...] += dmix
        dmb = dmix.astype(BF16)
        dvn = jnp.zeros((SGU_CHUNK, SGU_W), F32)
        for gi in range(4):
            dmg = jnp.where(grp == gi, dmb, jnp.zeros_like(dmb))
            dw_ref[gi] += lax.dot_general(dmg, vnb, (((1,), (1,)), ((), ())), preferred_element_type=F32)
            dvn = dvn + jnp.where(grp == gi, jnp.dot(wt_ref[gi], dmb, preferred_element_type=F32), 0.0)
        dg_ref[...] += jnp.sum(dvn * vhat, axis=0, keepdims=True)
        t = dvn * gn_v
        dvv = rstd * (t - vhat * jnp.mean(t * vhat, axis=-1, keepdims=True))
        dc_ref[...] = (jnp.concatenate([du, dvv], axis=1) * _gelu_grad(c)).astype(BF16)

    w_spec = pl.BlockSpec((4, SGU_CHUNK, SGU_CHUNK), lambda i: (0, 0, 0))
    tile = pl.BlockSpec((SGU_CHUNK, SGU_W), lambda i: (0, 0))
    vec = pl.BlockSpec((1, SGU_W), lambda i: (0, 0))
    return pl.pallas_call(
        body, out_shape=(jax.ShapeDtypeStruct((S, 2 * SGU_W), BF16), jax.ShapeDtypeStruct((4, SGU_CHUNK, SGU_CHUNK), F32),
                         jax.ShapeDtypeStruct((SGU_CHUNK, SGU_W), F32), jax.ShapeDtypeStruct((1, SGU_W), F32)),
        grid=(S // SGU_CHUNK,),
        in_specs=[pl.BlockSpec((SGU_CHUNK, SGU_W), lambda i: (i, 0)),
                  pl.BlockSpec((SGU_CHUNK, 2 * SGU_W), lambda i: (i, P_C // (2 * SGU_W))), vec, w_spec, w_spec, tile],
        out_specs=(pl.BlockSpec((SGU_CHUNK, 2 * SGU_W), lambda i: (i, 0)), w_spec, tile, vec),
        compiler_params=_cp(("arbitrary",)), name=name)(dsg, proj, gn, wm, wmt, bias)


def _sigmoid(z):
    return 1.0 / (1.0 + jnp.exp(-z))


def _merge_specs(tm):
    row = lambda n: pl.BlockSpec((tm, n), lambda i: (i, 0))
    gate = lambda b: pl.BlockSpec((tm, D), lambda i: (i, b))
    full = lambda r, c: pl.BlockSpec((r, c), lambda i: (0, 0))
    packed = pl.BlockSpec((4, 256, PACK_COLS), lambda i: (0, R_BRANCH // 256, 0))
    return row, gate, full, packed


def _branch_shards(c_ref, j):
    return c_ref[j, :, 0:256], c_ref[j, :, 256:512], c_ref[j, :, 512:768], c_ref[j, :, 768:1024]


def _merge_fwd(proj, ya, o, sg, packed_w, bg, name, tm=256):
    S = proj.shape[0]
    row, gate, full, packed = _merge_specs(tm)

    def body(g0, g1, g2, ya_ref, o_ref, sg_ref, c_ref, bg_ref, out_ref):
        yav, ov, sgv = ya_ref[...], o_ref[...], sg_ref[...]
        for j in range(4):
            cols = slice(256 * j, 256 * (j + 1))
            wa, wb0, wb1, wc = _branch_shards(c_ref, j)
            y = (jnp.dot(yav, wa, preferred_element_type=F32),
                 jnp.dot(ov[:, :256], wb0, preferred_element_type=F32) + jnp.dot(ov[:, 256:], wb1, preferred_element_type=F32),
                 jnp.dot(sgv, wc, preferred_element_type=F32))
            acc = jnp.zeros((tm, 256), F32)
            for b, g_ref in enumerate((g0, g1, g2)):
                acc = acc + _sigmoid(g_ref[:, cols] + bg_ref[:, b * D + 256 * j:b * D + 256 * (j + 1)]) * y[b]
            out_ref[:, cols] = acc.astype(BF16)

    return pl.pallas_call(
        body, out_shape=jax.ShapeDtypeStruct((S, D), BF16), grid=(S // tm,),
        in_specs=[gate(0), gate(1), gate(2), row(POOL_W), row(FOX_W), row(SGU_W), packed, full(1, 3 * D)],
        out_specs=row(D), compiler_params=_cp(("parallel",)), name=name)(proj, proj, proj, ya, o, sg, packed_w, bg)


def _merge_bwd(dm, proj, ya, o, sg, packed_w, bg, grads, name, tm=256):
    S = proj.shape[0]
    row, gate, full, packed = _merge_specs(tm)
    tn_dims = (((0,), (0,)), ((), ()))
    nt_dims = (((1,), (1,)), ((), ()))

    def body(dm_ref, g0, g1, g2, ya_ref, o_ref, sg_ref, c_ref, bg_ref, _, dg_ref, dya_ref, do_ref, dsg_ref, dc_ref, dbg_ref, acc):
        i = pl.program_id(0)

        @pl.when(i == 0)
        def _():
            acc[...] = jnp.zeros_like(acc)
            dbg_ref[...] = jnp.zeros_like(dbg_ref)

        yav, ov, sgv = ya_ref[...], o_ref[...], sg_ref[...]
        o0, o1 = ov[:, :256], ov[:, 256:]
        dya = jnp.zeros((tm, POOL_W), F32)
        do0 = jnp.zeros((tm, 256), F32)
        do1 = jnp.zeros((tm, 256), F32)
        dsg = jnp.zeros((tm, SGU_W), F32)
        for j in range(4):
            cols = slice(256 * j, 256 * (j + 1))
            wa, wb0, wb1, wc = _branch_shards(c_ref, j)
            y = (jnp.dot(yav, wa, preferred_element_type=F32),
                 jnp.dot(o0, wb0, preferred_element_type=F32) + jnp.dot(o1, wb1, preferred_element_type=F32),
                 jnp.dot(sgv, wc, preferred_element_type=F32))
            dmv = dm_ref[:, cols]
            dy = []
            for b, g_ref in enumerate((g0, g1, g2)):
                bcols = slice(b * D + 256 * j, b * D + 256 * (j + 1))
                gt = _sigmoid(g_ref[:, cols] + bg_ref[:, bcols])
                dgp = dmv * y[b] * gt * (1.0 - gt)
                dg_ref[:, bcols] = dgp.astype(BF16)
                dbg_ref[:, bcols] += jnp.sum(dgp, axis=0, keepdims=True)
                dy.append((dmv * gt).astype(BF16))
            dya = dya + lax.dot_general(dy[0], wa, nt_dims, preferred_element_type=F32)
            do0 = do0 + lax.dot_general(dy[1], wb0, nt_dims, preferred_element_type=F32)
            do1 = do1 + lax.dot_general(dy[1], wb1, nt_dims, preferred_element_type=F32)
            dsg = dsg + lax.dot_general(dy[2], wc, nt_dims, preferred_element_type=F32)
            acc[j, :, 0:256] += lax.dot_general(yav, dy[0], tn_dims, preferred_element_type=F32)
            acc[j, :, 256:512] += lax.dot_general(o0, dy[1], tn_dims, preferred_element_type=F32)
            acc[j, :, 512:768] += lax.dot_general(o1, dy[1], tn_dims, preferred_element_type=F32)
            acc[j, :, 768:1024] += lax.dot_general(sgv, dy[2], tn_dims, preferred_element_type=F32)
        dya_ref[...] = dya
        do_ref[:, :256] = do0
        do_ref[:, 256:] = do1
        dsg_ref[...] = dsg

        @pl.when(i == pl.num_programs(0) - 1)
        def _():
            dc_ref[...] = acc[...].astype(dc_ref.dtype)

    return pl.pallas_call(
        body, out_shape=(jax.ShapeDtypeStruct((S, 3 * D), BF16), jax.ShapeDtypeStruct((S, POOL_W), F32),
                         jax.ShapeDtypeStruct((S, FOX_W), F32), jax.ShapeDtypeStruct((S, SGU_W), F32),
                         jax.ShapeDtypeStruct(grads.shape, grads.dtype), jax.ShapeDtypeStruct((1, 3 * D), F32)),
        grid=(S // tm,),
        in_specs=[row(D), gate(0), gate(1), gate(2), row(POOL_W), row(FOX_W), row(SGU_W), packed, full(1, 3 * D), ANY],
        out_specs=(row(3 * D), row(POOL_W), row(FOX_W), row(SGU_W), packed, full(1, 3 * D)),
        scratch_shapes=[pltpu.VMEM((4, 256, PACK_COLS), F32)], input_output_aliases={9: 4},
        compiler_params=_cp(("arbitrary",)), name=name)(dm, proj, proj, proj, ya, o, sg, packed_w, bg, grads)


def _xattn_probs(qh, kh):
    s = lax.dot_general(qh, kh, (((1,), (1,)), ((), ())), preferred_element_type=F32) * X_SCALE
    p = jnp.exp(s - jnp.max(s, axis=-1, keepdims=True))
    return p / jnp.sum(p, axis=-1, keepdims=True)


def _xattn_fwd(xq, kv, name, tq=256):
    S = xq.shape[0]
    M = kv.shape[0]

    def body(q_ref, k_ref, v_ref, o_ref):
        for h in range(XH):
            sl = slice(h * XHD, (h + 1) * XHD)
            p = _xattn_probs(q_ref[:, sl], k_ref[:, sl])
            o_ref[:, sl] = jnp.dot(p.astype(BF16), v_ref[:, sl], preferred_element_type=F32).astype(BF16)

    return pl.pallas_call(
        body, out_shape=jax.ShapeDtypeStruct((S, D), BF16), grid=(S // tq,),
        in_specs=[pl.BlockSpec((tq, D), lambda i: (i, 0)), pl.BlockSpec((M, D), lambda i: (0, 0)),
                  pl.BlockSpec((M, D), lambda i: (0, 1))],
        out_specs=pl.BlockSpec((tq, D), lambda i: (i, 0)), compiler_params=_cp(("parallel",)), name=name)(xq, kv, kv)


def _xattn_bwd(xq, kv, do, name, tq=256):
    S = xq.shape[0]
    M = kv.shape[0]

    def body(q_ref, k_ref, v_ref, do_ref, dq_ref, dkv_ref, dk_acc, dv_acc):
        i = pl.program_id(0)

        @pl.when(i == 0)
        def _():
            dk_acc[...] = jnp.zeros_like(dk_acc)
            dv_acc[...] = jnp.zeros_like(dv_acc)

        for h in range(XH):
            sl = slice(h * XHD, (h + 1) * XHD)
            qh, kh, vh, doh = q_ref[:, sl], k_ref[:, sl], v_ref[:, sl], do_ref[:, sl]
            p = _xattn_probs(qh, kh)
            dp = lax.dot_general(doh, vh, (((1,), (1,)), ((), ())), preferred_element_type=F32)
            ds = p * (dp - jnp.sum(p * dp, axis=-1, keepdims=True))
            dsb = (ds * X_SCALE).astype(BF16)
            dq_ref[:, sl] = jnp.dot(dsb, kh, preferred_element_type=F32).astype(BF16)
            dk_acc[:, sl] += lax.dot_general(dsb, qh, (((0,), (0,)), ((), ())), preferred_element_type=F32)
            dv_acc[:, sl] += lax.dot_general(p.astype(BF16), doh, (((0,), (0,)), ((), ())), preferred_element_type=F32)

        @pl.when(i == pl.num_programs(0) - 1)
        def _():
            dkv_ref[:, :D] = dk_acc[...].astype(BF16)
            dkv_ref[:, D:] = dv_acc[...].astype(BF16)

    return pl.pallas_call(
        body, out_shape=(jax.ShapeDtypeStruct((S, D), BF16), jax.ShapeDtypeStruct((M, 2 * D), BF16)), grid=(S // tq,),
        in_specs=[pl.BlockSpec((tq, D), lambda i: (i, 0)), pl.BlockSpec((M, D), lambda i: (0, 0)),
                  pl.BlockSpec((M, D), lambda i: (0, 1)), pl.BlockSpec((tq, D), lambda i: (i, 0))],
        out_specs=(pl.BlockSpec((tq, D), lambda i: (i, 0)), pl.BlockSpec((M, 2 * D), lambda i: (0, 0))),
        scratch_shapes=[pltpu.VMEM((M, D), F32), pltpu.VMEM((M, D), F32)],
        compiler_params=_cp(("arbitrary",)), name=name)(xq, kv, kv, do)


def _adam_math(gv, wv, mv, vv):
    c1 = 1.0 - ADAM_B1 ** ADAM_STEP
    c2 = 1.0 - ADAM_B2 ** ADAM_STEP
    nm = ADAM_B1 * mv + (1.0 - ADAM_B1) * gv
    nv = ADAM_B2 * vv + (1.0 - ADAM_B2) * (gv * gv)
    return -ADAM_LR * ((nm / c1) / (jnp.sqrt(nv / c2) + ADAM_EPS) + ADAM_WD * wv), nm, nv


def _adamw(g, w, m, v, name, block=None):
    if block is None:
        block = (1, 256 if g.shape[1] % 256 == 0 else g.shape[1], g.shape[2])
    grid = tuple(s // b for s, b in zip(g.shape, block))

    def body(g_ref, w_ref, m_ref, v_ref, d_ref, nm_ref, nv_ref):
        d_ref[...], nm_ref[...], nv_ref[...] = _adam_math(g_ref[...], w_ref[...], m_ref[...], v_ref[...])

    blk = pl.BlockSpec(block, lambda a, b, c: (a, b, c))
    return pl.pallas_call(
        body, out_shape=(jax.ShapeDtypeStruct(g.shape, F32),) * 3, grid=grid,
        in_specs=[blk] * 4, out_specs=(blk,) * 3, compiler_params=_cp(("parallel",) * 3), name=name)(g, w, m, v)


def _adamw_packed(red, w, m, v, g_index, name, tr=256):
    L, r, c = w.shape
    tr = min(tr, r)

    def body(g0_ref, g1_ref, w_ref, m_ref, v_ref, g_ref, d_ref, nm_ref, nv_ref):
        gv = jnp.where(pl.program_id(0) == 0, g0_ref[...], g1_ref[...])
        g_ref[0] = gv
        d_ref[0], nm_ref[0], nv_ref[0] = _adam_math(gv, w_ref[0], m_ref[0], v_ref[0])

    gblk = pl.BlockSpec((tr, c), lambda l, i: g_index(i))
    blk = pl.BlockSpec((1, tr, c), lambda l, i: (l, i, 0))
    return pl.pallas_call(
        body, out_shape=(jax.ShapeDtypeStruct(w.shape, F32),) * 4, grid=(L, r // tr),
        in_specs=[gblk, gblk, blk, blk, blk], out_specs=(blk,) * 4,
        compiler_params=_cp(("parallel", "parallel")), name=name)(red[0], red[1], w, m, v)


def _sum_slots(a, out_dtype, name, tr=496):
    n, R, C = a.shape
    tr = tr if R % tr == 0 else R

    def body(a_ref, o_ref):
        acc = a_ref[0].astype(F32)
        for k in range(1, n):
            acc = acc + a_ref[k].astype(F32)
        o_ref[...] = acc.astype(out_dtype)

    return pl.pallas_call(
        body, out_shape=jax.ShapeDtypeStruct((R, C), out_dtype), grid=(R // tr,),
        in_specs=[pl.BlockSpec((n, tr, C), lambda i: (0, i, 0))], out_specs=pl.BlockSpec((tr, C), lambda i: (i, 0)),
        compiler_params=_cp(("parallel",)), name=name)(a)


def _add_pair(a, b, name, tr=496):
    n, R, C = a.shape
    tr = tr if R % tr == 0 else R

    def body(a_ref, b_ref, o_ref):
        o_ref[...] = (a_ref[...].astype(F32) + b_ref[...].astype(F32)).astype(BF16)

    blk = pl.BlockSpec((1, tr, C), lambda k, i: (k, i, 0))
    return pl.pallas_call(
        body, out_shape=jax.ShapeDtypeStruct(a.shape, BF16), grid=(n, R // tr), in_specs=[blk, blk], out_specs=blk,
        compiler_params=_cp(("parallel", "parallel")), name=name)(a, b)


LANDING = pl.BlockSpec(memory_space=pltpu.VMEM)


def _landing_params(shape, dtype):
    return pltpu.CompilerParams(vmem_limit_bytes=math.prod(shape) * jnp.dtype(dtype).itemsize + 4 * 1024 * 1024)


def _place():
    return lax.axis_index("x"), lax.axis_index("y"), lax.axis_index("c")


def _other_chips(x, y):
    return [(1 - x, y), (x, 1 - y), (1 - x, 1 - y)]


def _row_chunks(rows, want, align=16):
    n = want
    while n > 1 and rows % (n * align):
        n -= 1
    return n


def _gather_weights(shard, name, nch=5):
    R, C = shard.shape
    half = R // 2
    nch = _row_chunks(half, nch)
    cr = half // nch

    def body(s_ref, o_ref, send_sems, recv_sems, local_sem):
        x, y, c = _place()
        j = 2 * x + y
        mine0 = c * half
        theirs0 = (1 - c) * half

        def rows(jj, r0, q):
            return o_ref.at[jj, pl.ds(pl.multiple_of(r0 + q * cr, 16), cr), :]

        def copy(k, src, dst, to):
            return pltpu.make_async_remote_copy(src_ref=src, dst_ref=dst, send_sem=send_sems.at[k], recv_sem=recv_sems.at[k],
                                                device_id=to, device_id_type=MESH)

        own = pltpu.make_async_copy(s_ref, o_ref.at[j], local_sem)
        own.start()
        chips = _other_chips(x, y)
        first = []
        for q in range(nch):
            for k, (px, py) in enumerate(chips):
                src = s_ref.at[pl.ds(pl.multiple_of(mine0 + q * cr, 16), cr), :]
                first.append(copy(k * nch + q, src, rows(j, mine0, q), (px, py, c)))
        for cp in first:
            cp.start()
        passed = []
        for q in range(nch):
            for k, (px, py) in enumerate(chips):
                jj = 2 * px + py
                copy(k * nch + q, rows(jj, mine0, q), rows(jj, mine0, q), (px, py, c)).wait_recv()
                fw = copy((3 + k) * nch + q, rows(jj, mine0, q), rows(jj, mine0, q), (x, y, 1 - c))
                fw.start()
                passed.append(fw)
        for q in range(nch):
            for k, (px, py) in enumerate(chips):
                jj = 2 * px + py
                copy((3 + k) * nch + q, rows(jj, theirs0, q), rows(jj, theirs0, q), (x, y, 1 - c)).wait_recv()
        for cp in first + passed:
            cp.wait_send()
        own.wait()

    return pl.pallas_call(
        body, out_shape=jax.ShapeDtypeStruct((4, R, C), shard.dtype), in_specs=[ANY], out_specs=LANDING,
        scratch_shapes=[pltpu.SemaphoreType.DMA((6 * nch,)), pltpu.SemaphoreType.DMA((6 * nch,)), pltpu.SemaphoreType.DMA],
        compiler_params=_landing_params((4, R, C), shard.dtype), name=name)(shard)


def _pair_split(g, name, nch=5):
    n, R, C = g.shape
    half = R // 2
    nch = _row_chunks(half, nch)
    cr = half // nch

    def body(g_ref, own_ref, got_ref, send_sems, recv_sems, local_sem):
        x, y, c = _place()
        mine0 = pl.multiple_of(c * half, 16)
        theirs0 = (1 - c) * half
        keep = pltpu.make_async_copy(g_ref.at[:, pl.ds(mine0, half), :], own_ref, local_sem)
        keep.start()
        cps = []
        for s in range(n):
            for q in range(nch):
                src = g_ref.at[s, pl.ds(pl.multiple_of(theirs0 + q * cr, 16), cr), :]
                cps.append(pltpu.make_async_remote_copy(
                    src_ref=src, dst_ref=got_ref.at[s, pl.ds(q * cr, cr), :], send_sem=send_sems.at[s * nch + q],
                    recv_sem=recv_sems.at[s * nch + q], device_id=(x, y, 1 - c), device_id_type=MESH))
        for cp in cps:
            cp.start()
        for cp in cps:
            cp.wait()
        keep.wait()

    sh = jax.ShapeDtypeStruct((n, half, C), g.dtype)
    return pl.pallas_call(
        body, out_shape=(sh, sh), in_specs=[ANY], out_specs=(ANY, LANDING),
        scratch_shapes=[pltpu.SemaphoreType.DMA((n * nch,)), pltpu.SemaphoreType.DMA((n * nch,)), pltpu.SemaphoreType.DMA],
        compiler_params=_landing_params(sh.shape, g.dtype), name=name)(g)


def _chip_all_to_all(p, name, nch=5):
    R = p.shape[1]
    nch = _row_chunks(R, nch)
    cr = R // nch

    def body(p_ref, o_ref, send_sems, recv_sems, local_sem):
        x, y, c = _place()
        j = 2 * x + y
        own = pltpu.make_async_copy(p_ref.at[j], o_ref.at[j], local_sem)
        own.start()
        cps = []
        for q in range(nch):
            for k, (px, py) in enumerate(_other_chips(x, y)):
                cps.append(pltpu.make_async_remote_copy(
                    src_ref=p_ref.at[2 * px + py, pl.ds(q * cr, cr), :], dst_ref=o_ref.at[j, pl.ds(q * cr, cr), :],
                    send_sem=send_sems.at[k * nch + q], recv_sem=recv_sems.at[k * nch + q], device_id=(px, py, c),
                    device_id_type=MESH))
        for cp in cps:
            cp.start()
        for cp in cps:
            cp.wait()
        own.wait()

    return pl.pallas_call(
        body, out_shape=jax.ShapeDtypeStruct(p.shape, p.dtype), in_specs=[ANY], out_specs=LANDING,
        scratch_shapes=[pltpu.SemaphoreType.DMA((3 * nch,)), pltpu.SemaphoreType.DMA((3 * nch,)), pltpu.SemaphoreType.DMA],
        compiler_params=_landing_params(p.shape, p.dtype), name=name)(p)


def _pair_gather(t, name, nch=10):
    R = t.shape[0]
    nch = _row_chunks(R, nch, 8)
    cr = R // nch

    def body(t_ref, o_ref, send_sems, recv_sems, local_sem):
        x, y, c = _place()
        own = pltpu.make_async_copy(t_ref, o_ref.at[c], local_sem)
        own.start()
        cps = [pltpu.make_async_remote_copy(src_ref=t_ref.at[pl.ds(q * cr, cr), :], dst_ref=o_ref.at[c, pl.ds(q * cr, cr), :],
                                            send_sem=send_sems.at[q], recv_sem=recv_sems.at[q], device_id=(x, y, 1 - c),
                                            device_id_type=MESH) for q in range(nch)]
        for cp in cps:
            cp.start()
        for cp in cps:
            cp.wait()
        own.wait()

    return pl.pallas_call(
        body, out_shape=jax.ShapeDtypeStruct((2,) + t.shape, t.dtype), in_specs=[ANY], out_specs=LANDING,
        scratch_shapes=[pltpu.SemaphoreType.DMA((nch,)), pltpu.SemaphoreType.DMA((nch,)), pltpu.SemaphoreType.DMA],
        compiler_params=_landing_params((2,) + t.shape, t.dtype), name=name)(t)


def _reduce_scatter(g, tag):
    own, got = _pair_split(g, f"rs_pair_{tag}")
    p = _add_pair(own, got, f"rs_add_{tag}")
    q = _chip_all_to_all(p, f"rs_a2a_{tag}")
    t = _sum_slots(q, F32, f"rs_sum_{tag}")
    both = _pair_gather(t, f"rs_join_{tag}")
    return both.reshape(g.shape[1], g.shape[2])


def _all_reduce_small(v, tag):
    pair = _pair_gather(v, f"ar_pair_{tag}")
    p = _sum_slots(pair, F32, f"ar_add_{tag}")
    q = _chip_all_to_all(jnp.broadcast_to(p[None], (4,) + p.shape), f"ar_a2a_{tag}")
    return _sum_slots(q, F32, f"ar_sum_{tag}")


R_FF1, R_FF2, R_XKV, R_BRANCH, R_OUT, R_XQ, R_XO, R_WIN = 0, 1024, 2048, 2560, 2816, 3072, 3328, 3584
WIN_ROWS = N_IN // 4


def _w_in_t(a):
    return jnp.transpose(a, (2, 0, 1))


def _pack_shard(w, l):
    xkv, wb = w['w_xkv'][l], w['w_branch_b'][l]
    parts = [w['w_ff1'][l], w['w_ff2'][l], jnp.concatenate([xkv[:512], xkv[512:]], axis=1),
             jnp.concatenate([w['w_branch_a'][l], wb[:256], wb[256:], w['w_branch_c'][l]], axis=1),
             w['w_out'][l], w['w_xq'][l], w['w_xo'][l],
             jnp.pad(_w_in_t(w['w_in'])[:, l, :], ((0, PACK_ROWS - R_WIN - WIN_ROWS), (0, 0)))]
    return jnp.concatenate(parts, axis=0).astype(BF16)


def _w_in_rows(gathered):
    t = gathered[:, R_WIN:R_WIN + WIN_ROWS, :].reshape(N_IN, PACK_COLS)
    return jnp.concatenate([t[2312:5384], t[256:1792], t[1800:2312], t[0:256],
                            jnp.pad(t[1792:1800], ((0, NP - P_F - 8), (0, 0)))], axis=0)


def _w_in_grad_rows(grads, dwt):
    t = jnp.concatenate([dwt[P_A:P_A + 256], dwt[P_Q:P_Q + 1536], dwt[P_F:P_F + 8], dwt[P_C:P_C + 512], dwt[P_G:P_G + 3072]],
                        axis=0)
    return lax.dynamic_update_slice(grads, t.reshape(4, WIN_ROWS, PACK_COLS).astype(grads.dtype), (0, R_WIN, 0))


def _small_prep(sw, l):
    eye = jnp.eye(4, dtype=F32)
    bd = jnp.einsum('gh,gcd->gchd', eye, sw['pool_w'][l]).reshape(POOL_W, POOL_W).astype(BF16)
    tril = jnp.tril(jnp.ones((SGU_CHUNK, SGU_CHUNK), F32))
    wm = (sw['sgu_w'][l] * tril[None]).astype(BF16)
    return dict(
        g_mix=sw['norm_mix_g'][l][None], g_x=sw['norm_xattn_g'][l][None], g_mem=sw['norm_mem_g'][l][None],
        g_ffn=sw['norm_ffn_g'][l][None], bd=bd, pool_scale=sw['pool_scale'][l][None],
        bf=jnp.pad(sw['b_forget'][l], (0, FCOLS - 8))[None], sgu_g=sw['sgu_norm_g'][l][None], wm=wm,
        wmt=jnp.transpose(wm, (0, 2, 1)), sgu_bias=jnp.repeat(sw['sgu_b'][l].T, 64, axis=1), bg=sw['b_gate'][l][None])


def _rows4(r0):
    return dict(n=D, k=D, b_block=(4, 256, 512), b_index=lambda i, j, k: (0, r0 // 256, j))


def _rows_t(r0):
    return dict(tb=True, n=D, k=D, tn=256, b_block=(None, 256, PACK_COLS), b_index=lambda i, j, k: (j, r0 // 256, 0))


def _rows_grad(r0):
    return dict(ta=True, tm=256, tn=512, o_block=(None, 256, 512), o_index=lambda i, j, k: (i, r0 // 256, j))


def _add_to(r, e):
    return e + r


def _layer_fwd(x, mem, G, w_in_t, sp, l):
    t = f"l{l}"
    S = x.shape[0]
    h = _rms_fwd(x, sp['g_mix'], f"rms_mix_{t}")
    proj = _mm(h, w_in_t, name=f"proj_{t}", out_dtype=F32, tb=True)
    d, ya = _pool_fwd(proj, sp['bd'], sp['pool_scale'], f"pool_fwd_{t}")
    fcum = _fgate_fwd(proj, sp['bf'], f"fgate_fwd_{t}")
    f8 = fcum[:, :8]
    fcol = f8.reshape(S, 4, 2).transpose(1, 0, 2)
    frow = f8.T.reshape(4, 2, S)
    qkv = proj[:, P_Q:P_Q + 3 * FOX_W].astype(BF16)
    o, lse = _fox_fwd(qkv, fcol, frow, f"fox_fwd_{t}")
    sg = _sgu_fwd(proj, sp['sgu_g'], sp['wm'], sp['sgu_bias'], f"sgu_fwd_{t}")
    merged = _merge_fwd(proj, ya, o, sg, G, sp['bg'], f"merge_fwd_{t}")
    x1 = _mm(merged, G, name=f"out_{t}", out_dtype=F32, extra=x, epi=_add_to, **_rows4(R_OUT))
    hx = _rms_fwd(x1, sp['g_x'], f"rms_x_{t}")
    hm = _rms_fwd(mem, sp['g_mem'], f"rms_mem_{t}")
    xq = _mm(hx, G, name=f"xq_{t}", out_dtype=BF16, **_rows4(R_XQ))
    kv = _mm(hm, G, name=f"xkv_{t}", out_dtype=BF16, n=2 * D, k=D, tn=512, tk=512, b_block=(None, 512, 512),
             b_index=lambda i, j, k: (j, R_XKV // 512, k))
    o2 = _xattn_fwd(xq, kv, f"xattn_fwd_{t}")
    x2 = _mm(o2, G, name=f"xo_{t}", out_dtype=F32, extra=x1, epi=_add_to, **_rows4(R_XO))
    hf = _rms_fwd(x2, sp['g_ffn'], f"rms_ffn_{t}")
    z = _mm(hf, G, name=f"ff1_{t}", out_dtype=F32, n=D_FF, k=D, tn=512, b_block=(None, 1024, 512),
            b_index=lambda i, j, k: (j // 2, R_FF1 // 1024, j % 2))
    x3 = _mm(z, G, name=f"ff2_{t}", out_dtype=F32, a_fn=_relu2, extra=x2, epi=_add_to, n=D, k=D_FF, tk=1024,
             b_block=(None, 1024, 512), b_index=lambda i, j, k: (k, R_FF2 // 1024, j))
    saved = dict(x=x, h=h, proj=proj, d=d, ya=ya, fcol=fcol, frow=frow, qkv=qkv, o=o, lse=lse, sg=sg, merged=merged, x1=x1,
                 hx=hx, hm=hm, xq=xq, kv=kv, o2=o2, x2=x2, hf=hf, z=z)
    return x3, saved


def _layer_bwd(dx3, mem, G, w_in_t, sp, sv, l):
    t = f"l{l}"
    S = dx3.shape[0]
    gs = {}
    gp = jnp.zeros((4, PACK_ROWS, PACK_COLS), BF16)
    dz = _mm(dx3, G, name=f"d_a2_{t}", out_dtype=BF16, tb=True, n=D_FF, k=D, tn=512, b_block=(None, 512, PACK_COLS),
             b_index=lambda i, j, k: (j // 2, R_FF2 // 512 + j % 2, 0), extra=sv['z'],
             epi=lambda r, e: r * (2.0 * jnp.maximum(e, 0.0)))
    gp = _mm(sv['z'], dx3, name=f"dw_ff2_{t}", out_dtype=BF16, ta=True, a_fn=_relu2, into=gp, tm=1024, tn=512,
             o_block=(None, 1024, 512), o_index=lambda i, j, k: (i, R_FF2 // 1024, j))
    gp = _mm(sv['hf'], dz, name=f"dw_ff1_{t}", out_dtype=BF16, ta=True, into=gp, tm=1024, tn=512,
             o_block=(None, 1024, 512), o_index=lambda i, j, k: (j // 2, R_FF1 // 1024, j % 2))
    dhf = _mm(dz, G, name=f"d_hf_{t}", out_dtype=F32, tb=True, n=D, k=D_FF, tn=512, tk=1024, b_block=(None, 512, PACK_COLS),
              b_index=lambda i, j, k: (k, R_FF1 // 512 + j, 0))
    dx2, gs['norm_ffn_g'] = _rms_bwd(dhf, sv['x2'], sp['g_ffn'], dx3, f"rms_ffn_bwd_{t}")
    do2 = _mm(dx2, G, name=f"d_o2_{t}", out_dtype=BF16, **_rows_t(R_XO))
    gp = _mm(sv['o2'], dx2, name=f"dw_xo_{t}", out_dtype=BF16, into=gp, **_rows_grad(R_XO))
    dxq, dkv = _xattn_bwd(sv['xq'], sv['kv'], do2, f"xattn_bwd_{t}")
    gp = _mm(sv['hm'], dkv, name=f"dw_xkv_{t}", out_dtype=BF16, ta=True, into=gp, tm=512, tn=512,
             o_block=(None, 512, 512), o_index=lambda i, j, k: (j, R_XKV // 512, i))
    dhm = _mm(dkv, G, name=f"d_hm_{t}", out_dtype=F32, tb=True, n=D, k=2 * D, tn=512, tk=512, b_block=(None, 512, 512),
              b_index=lambda i, j, k: (k, R_XKV // 512, j))
    gs['norm_mem_g'] = _rms_bwd(dhm, mem, sp['g_mem'], None, f"rms_mem_bwd_{t}")
    gp = _mm(sv['hx'], dxq, name=f"dw_xq_{t}", out_dtype=BF16, into=gp, **_rows_grad(R_XQ))
    dhx = _mm(dxq, G, name=f"d_hx_{t}", out_dtype=F32, **_rows_t(R_XQ))
    dx1, gs['norm_xattn_g'] = _rms_bwd(dhx, sv['x1'], sp['g_x'], dx2, f"rms_x_bwd_{t}")
    gp = _mm(sv['merged'], dx1, name=f"dw_out_{t}", out_dtype=BF16, into=gp, **_rows_grad(R_OUT))
    dm = _mm(dx1, G, name=f"d_merged_{t}", out_dtype=F32, **_rows_t(R_OUT))
    dg, dya, do, dsg, gp, gs['b_gate'] = _merge_bwd(dm, sv['proj'], sv['ya'], sv['o'], sv['sg'], G, sp['bg'], gp, f"merge_bwd_{t}")
    dc, dws, dbias, gs['sgu_norm_g'] = _sgu_bwd(dsg, sv['proj'], sp['sgu_g'], sp['wm'], sp['wmt'], sp['sgu_bias'], f"sgu_bwd_{t}")
    tril = jnp.tril(jnp.ones((SGU_CHUNK, SGU_CHUNK), F32))
    gs['sgu_w'] = dws * tril[None]
    gs['sgu_b'] = dbias.reshape(SGU_CHUNK, 4, 64).sum(-1).T
    dq, dk, dv, dfrow = _fox_bwd(sv['qkv'], do, sv['lse'], sv['fcol'], sv['frow'], f"fox_bwd_{t}")
    dF = jnp.pad(dfrow.reshape(8, S).T, ((0, 0), (0, FCOLS - 8)))
    df, dbf = _fgate_bwd(dF, sv['proj'], sp['bf'], f"fgate_bwd_{t}")
    gs['b_forget'] = dbf[:, :8]
    da, dbd, gs['pool_scale'] = _pool_bwd(dya, sv['d'], sp['bd'], sp['pool_scale'], f"pool_bwd_{t}")
    gs['pool_w'] = jnp.stack([dbd[g * 64:(g + 1) * 64, g * 64:(g + 1) * 64] for g in range(4)])
    dproj = jnp.concatenate([dg, dq, dk, dv, dc, da, df], axis=1)
    dwt = _mm(dproj, sv['h'], name=f"dw_in_{t}", out_dtype=BF16, ta=True, tm=512, tn=1024)
    gp = _w_in_grad_rows(gp, dwt)
    dh = _mm(dproj, w_in_t, name=f"d_h_{t}", out_dtype=F32, tk=512)
    dx, gs['norm_mix_g'] = _rms_bwd(dh, sv['x'], sp['g_mix'], dx1, f"rms_mix_bwd_{t}")
    return dx, gp, gs


SMALL_ROWS = 1424
GRAD_BLOCKS = {
    'w_ff1': (1024, lambda i: (R_FF1 // 256 + i, 0)), 'w_ff2': (1024, lambda i: (R_FF2 // 256 + i, 0)),
    'w_out': (1024, lambda i: (R_OUT // 256 + i, 0)), 'w_xq': (1024, lambda i: (R_XQ // 256 + i, 0)),
    'w_xo': (1024, lambda i: (R_XO // 256 + i, 0)), 'w_xkv': (512, lambda i: (R_XKV // 256 + i % 2, i // 2)),
    'w_branch_a': (256, lambda i: (R_BRANCH // 256, 0)), 'w_branch_b': (256, lambda i: (R_BRANCH // 256, 1 + i)),
    'w_branch_c': (256, lambda i: (R_BRANCH // 256, 3)),
}


def _pack_small(parts):
    flat = jnp.concatenate([p.reshape(-1) for p in parts])
    return jnp.pad(flat, (0, SMALL_ROWS * 128 - flat.shape[0])).reshape(SMALL_ROWS, 128)


def _unpack_small(buf, shapes):
    flat, out, r = buf.reshape(-1), [], 0
    for s in shapes:
        n = math.prod(s)
        out.append(flat[r:r + n].reshape(s))
        r += n
    return out


def kernel(x, mem, norm_mix_g, w_in, b_forget, pool_w, pool_scale, sgu_norm_g, sgu_w, sgu_b, w_branch_a, w_branch_b, w_branch_c, b_gate, w_out, norm_xattn_g, norm_mem_g, w_xq, w_xkv, w_xo, norm_ffn_g, w_ff1, w_ff2, final_norm_g, loss_target, m_norm_mix_g, m_w_in, m_b_forget, m_pool_w, m_pool_scale, m_sgu_norm_g, m_sgu_w, m_sgu_b, m_w_branch_a, m_w_branch_b, m_w_branch_c, m_b_gate, m_w_out, m_norm_xattn_g, m_norm_mem_g, m_w_xq, m_w_xkv, m_w_xo, m_norm_ffn_g, m_w_ff1, m_w_ff2, m_final_norm_g, v_norm_mix_g, v_w_in, v_b_forget, v_pool_w, v_pool_scale, v_sgu_norm_g, v_sgu_w, v_sgu_b, v_w_branch_a, v_w_branch_b, v_w_branch_c, v_b_gate, v_w_out, v_norm_xattn_g, v_norm_mem_g, v_w_xq, v_w_xkv, v_w_xo, v_norm_ffn_g, v_w_ff1, v_w_ff2, v_final_norm_g):
    args = (norm_mix_g, w_in, b_forget, pool_w, pool_scale, sgu_norm_g, sgu_w, sgu_b, w_branch_a, w_branch_b, w_branch_c, b_gate,
            w_out, norm_xattn_g, norm_mem_g, w_xq, w_xkv, w_xo, norm_ffn_g, w_ff1, w_ff2, final_norm_g)
    margs = (m_norm_mix_g, m_w_in, m_b_forget, m_pool_w, m_pool_scale, m_sgu_norm_g, m_sgu_w, m_sgu_b, m_w_branch_a, m_w_branch_b,
             m_w_branch_c, m_b_gate, m_w_out, m_norm_xattn_g, m_norm_mem_g, m_w_xq, m_w_xkv, m_w_xo, m_norm_ffn_g, m_w_ff1, m_w_ff2,
             m_final_norm_g)
    vargs = (v_norm_mix_g, v_w_in, v_b_forget, v_pool_w, v_pool_scale, v_sgu_norm_g, v_sgu_w, v_sgu_b, v_w_branch_a, v_w_branch_b,
             v_w_branch_c, v_b_gate, v_w_out, v_norm_xattn_g, v_norm_mem_g, v_w_xq, v_w_xkv, v_w_xo, v_norm_ffn_g, v_w_ff1, v_w_ff2,
             v_final_norm_g)
    w = dict(zip(W_NAMES, args))
    mo = dict(zip(W_NAMES, margs))
    vo = dict(zip(W_NAMES, vargs))
    xs, mems, tgt = x[0], mem[0], loss_target[0]

    G = [_gather_weights(_pack_shard(w, l), f"gather_w_l{l}") for l in range(DEPTH)]
    w_in_t = [_w_in_rows(g) for g in G]
    preps = [_small_prep(w, l) for l in range(DEPTH)]

    act, saved = xs, []
    for l in range(DEPTH):
        act, sv = _layer_fwd(act, mems, G[l], w_in_t[l], preps[l], l)
        saved.append(sv)
    loss_part, dact, d_final_g = _loss_head(act, w['final_norm_g'][None], tgt, "loss_head")

    red, small_g = [None] * DEPTH, [None] * DEPTH
    for l in reversed(range(DEPTH)):
        dact, gp, small_g[l] = _layer_bwd(dact, mems, G[l], w_in_t[l], preps[l], saved[l], l)
        red[l] = _reduce_scatter(gp, f"l{l}")
    grad_x = dact[None]

    per_layer = [n for n in SMALL_NAMES if n != 'final_norm_g']
    small_shapes = [w[n].shape for n in per_layer] + [(D,), (1,)]
    parts = [jnp.stack([small_g[l][n].reshape(w[n].shape[1:]) for l in range(DEPTH)]) for n in per_layer]
    small_red = _unpack_small(_all_reduce_small(_pack_small(parts + [d_final_g.reshape(D), loss_part.reshape(1)]), "small"), small_shapes)
    grads = dict(zip(per_layer + ['final_norm_g'], small_red[:-1]))
    loss = small_red[-1].reshape(())

    delta, new_m, new_v = {}, {}, {}
    for n, (c, g_index) in GRAD_BLOCKS.items():
        grads[n], delta[n], new_m[n], new_v[n] = _adamw_packed(red, w[n], mo[n], vo[n], g_index, f"adamw_{n}")
    g_t = jnp.stack([r[R_WIN:R_WIN + WIN_ROWS] for r in red], axis=1)
    upd = _adamw(g_t, _w_in_t(w['w_in']), _w_in_t(mo['w_in']), _w_in_t(vo['w_in']), "adamw_w_in", block=(WIN_ROWS, DEPTH, 128))
    grads['w_in'], delta['w_in'], new_m['w_in'], new_v['w_in'] = [jnp.transpose(a, (1, 2, 0)) for a in (g_t,) + tuple(upd)]
    small_all = per_layer + ['final_norm_g']
    shapes_all = [w[n].shape for n in small_all]
    packed = [_pack_small([d[n] for n in small_all])[None] for d in (grads, w, mo, vo)]
    ds, ms, vs = _adamw(*packed, "adamw_small")
    for n, a, b, c in zip(small_all, _unpack_small(ds[0], shapes_all), _unpack_small(ms[0], shapes_all), _unpack_small(vs[0], shapes_all)):
        delta[n], new_m[n], new_v[n] = a, b, c

    return (loss, grad_x, *[grads[n] for n in W_NAMES], *[delta[n] for n in W_NAMES], *[new_m[n] for n in W_NAMES],
            *[new_v[n] for n in W_NAMES])
```

```python
import math

import jax
import jax.numpy as jnp
from jax import lax
from jax.experimental import pallas as pl
from jax.experimental.pallas import tpu as pltpu

F32 = jnp.float32
BF16 = jnp.bfloat16

D = 1024
DEPTH = 2
POOL_W = 256
FOX_W = 512
SGU_W = 256
SGU_CHUNK = 128
N_IN = 5384
P_G, P_Q, P_K, P_V, P_C, P_A, P_F = 0, 3072, 3584, 4096, 4608, 5120, 5376
NP = 5632
XH, XHD = 4, 256
D_FF = 4096
EPS = 1e-6
NEG = -1e30
FOX_SCALE = 64 ** -0.5
X_SCALE = 256 ** -0.5
GELU_K = math.sqrt(2.0 / math.pi)
GELU_C = 0.044715

ADAM_LR, ADAM_B1, ADAM_B2, ADAM_EPS, ADAM_WD, ADAM_STEP = 0.001, 0.9, 0.999, 1e-08, 0.01, 10

VMEM_LIMIT = 48 * 1024 * 1024
MESH = pl.DeviceIdType.MESH

IN_NAMES = ['x', 'mem', 'norm_mix_g', 'w_in', 'b_forget', 'pool_w', 'pool_scale', 'sgu_norm_g', 'sgu_w', 'sgu_b',
            'w_branch_a', 'w_branch_b', 'w_branch_c', 'b_gate', 'w_out', 'norm_xattn_g', 'norm_mem_g', 'w_xq',
            'w_xkv', 'w_xo', 'norm_ffn_g', 'w_ff1', 'w_ff2', 'final_norm_g']
W_NAMES = IN_NAMES[2:]
BIG_NAMES = ['w_in', 'w_branch_a', 'w_branch_b', 'w_branch_c', 'w_out', 'w_xq', 'w_xkv', 'w_xo', 'w_ff1', 'w_ff2']
SMALL_NAMES = [n for n in W_NAMES if n not in BIG_NAMES]
PACK_COLS = 1024
PACK_ROWS = 4960


ANY = pl.BlockSpec(memory_space=pl.ANY)


def _cp(sem=None):
    return pltpu.CompilerParams(dimension_semantics=sem, vmem_limit_bytes=VMEM_LIMIT)


def _mm(a, b, *, name, out_dtype, ta=False, tb=False, tm=1024, tn=512, tk=1024, a_fn=None, extra=None, epi=None,
        n=None, k=None, b_block=None, b_index=None, into=None, o_block=None, o_index=None):
    M = a.shape[1] if ta else a.shape[0]
    K = k if k is not None else (a.shape[0] if ta else a.shape[1])
    N = n if n is not None else (b.shape[0] if tb else b.shape[1])
    tm, tn, tk = min(tm, M), min(tn, N), min(tk, K)
    assert M % tm == 0 and N % tn == 0 and K % tk == 0, (name, M, N, K)
    nk = K // tk
    a_spec = pl.BlockSpec((tk, tm), lambda i, j, k: (k, i)) if ta else pl.BlockSpec((tm, tk), lambda i, j, k: (i, k))
    if b_block is not None:
        b_spec = pl.BlockSpec(b_block, b_index)
    else:
        b_spec = pl.BlockSpec((tn, tk), lambda i, j, k: (j, k)) if tb else pl.BlockSpec((tk, tn), lambda i, j, k: (k, j))
    dn = (((0 if ta else 1,), (1 if tb else 0,)), ((), ()))
    tile = pl.BlockSpec((tm, tn), lambda i, j, k: (i, j))
    o_spec = pl.BlockSpec(o_block, o_index) if into is not None else tile
    in_specs = [a_spec, b_spec] + ([tile] if extra is not None else []) + ([ANY] if into is not None else [])
    n_in = len(in_specs)

    def body(*refs):
        a_ref, b_ref = refs[0], refs[1]
        e_ref = refs[2] if extra is not None else None
        o_ref, acc_ref = refs[n_in], refs[n_in + 1]
        kk = pl.program_id(2)

        @pl.when(kk == 0)
        def _():
            acc_ref[...] = jnp.zeros_like(acc_ref)

        av = a_ref[...]
        if a_fn is not None:
            av = a_fn(av)
        bv = b_ref[...]
        if bv.ndim == 3:
            bv = bv.reshape(-1, bv.shape[-1])
        acc_ref[...] += lax.dot_general(av.astype(BF16), bv.astype(BF16), dn, preferred_element_type=F32)

        @pl.when(kk == nk - 1)
        def _():
            r = acc_ref[...]
            if epi is not None:
                r = epi(r, e_ref[...])
            o_ref[...] = r.astype(o_ref.dtype)

    args = (a, b) + ((extra,) if extra is not None else ()) + ((into,) if into is not None else ())
    out_shape = jax.ShapeDtypeStruct(into.shape, into.dtype) if into is not None else jax.ShapeDtypeStruct((M, N), out_dtype)
    return pl.pallas_call(
        body, out_shape=out_shape, grid=(M // tm, N // tn, nk), in_specs=in_specs, out_specs=o_spec,
        scratch_shapes=[pltpu.VMEM((tm, tn), F32)], input_output_aliases={n_in - 1: 0} if into is not None else {},
        compiler_params=_cp(("parallel", "parallel", "arbitrary")), name=name)(*args)


def _relu2(z):
    r = jnp.maximum(z, 0.0)
    return r * r


def _rms_fwd(x, g, name, tr=256):
    R, n = x.shape
    tr = min(tr, R)

    def body(x_ref, g_ref, h_ref):
        xv = x_ref[...]
        rstd = lax.rsqrt(jnp.mean(xv * xv, axis=-1, keepdims=True) + EPS)
        h_ref[...] = (xv * rstd * g_ref[...]).astype(BF16)

    return pl.pallas_call(
        body, out_shape=jax.ShapeDtypeStruct((R, n), BF16), grid=(R // tr,),
        in_specs=[pl.BlockSpec((tr, n), lambda i: (i, 0)), pl.BlockSpec((1, n), lambda i: (0, 0))],
        out_specs=pl.BlockSpec((tr, n), lambda i: (i, 0)), compiler_params=_cp(("parallel",)), name=name)(x, g)


def _rms_bwd(dh, x, g, dres, name, tr=256):
    R, n = x.shape
    tr = min(tr, R)
    need_dx = dres is not None

    def body(*refs):
        if need_dx:
            dh_ref, x_ref, g_ref, r_ref, dx_ref, dg_ref = refs
        else:
            dh_ref, x_ref, g_ref, dg_ref = refs
        i = pl.program_id(0)
        xv = x_ref[...]
        dhv = dh_ref[...].astype(F32)
        rstd = lax.rsqrt(jnp.mean(xv * xv, axis=-1, keepdims=True) + EPS)
        xhat = xv * rstd

        @pl.when(i == 0)
        def _():
            dg_ref[...] = jnp.zeros_like(dg_ref)

        dg_ref[...] += jnp.sum(dhv * xhat, axis=0, keepdims=True)
        if need_dx:
            t = dhv * g_ref[...]
            dx_ref[...] = r_ref[...] + rstd * (t - xhat * jnp.mean(t * xhat, axis=-1, keepdims=True))

    row = pl.BlockSpec((tr, n), lambda i: (i, 0))
    vec = pl.BlockSpec((1, n), lambda i: (0, 0))
    if need_dx:
        return pl.pallas_call(
            body, out_shape=(jax.ShapeDtypeStruct((R, n), F32), jax.ShapeDtypeStruct((1, n), F32)), grid=(R // tr,),
            in_specs=[row, row, vec, row], out_specs=(row, vec), compiler_params=_cp(("arbitrary",)), name=name)(dh, x, g, dres)
    return pl.pallas_call(
        body, out_shape=jax.ShapeDtypeStruct((1, n), F32), grid=(R // tr,),
        in_specs=[row, row, vec], out_specs=vec, compiler_params=_cp(("arbitrary",)), name=name)(dh, x, g)


def _loss_head(x, g, tgt, name, tr=256):
    R, n = x.shape

    def body(x_ref, g_ref, t_ref, loss_ref, dx_ref, dg_ref):
        i = pl.program_id(0)
        xv = x_ref[...]
        gv = g_ref[...]
        rstd = lax.rsqrt(jnp.mean(xv * xv, axis=-1, keepdims=True) + EPS)
        xhat = xv * rstd
        e = xhat * gv - t_ref[...]

        @pl.when(i == 0)
        def _():
            loss_ref[...] = jnp.zeros_like(loss_ref)
            dg_ref[...] = jnp.zeros_like(dg_ref)

        loss_ref[...] += 0.5 * jnp.sum(jnp.sum(e * e, axis=-1, keepdims=True) / n, axis=0, keepdims=True)
        dy = e / n
        dg_ref[...] += jnp.sum(dy * xhat, axis=0, keepdims=True)
        t = dy * gv
        dx_ref[...] = rstd * (t - xhat * jnp.mean(t * xhat, axis=-1, keepdims=True))

    row = pl.BlockSpec((tr, n), lambda i: (i, 0))
    vec = pl.BlockSpec((1, n), lambda i: (0, 0))
    one = pl.BlockSpec((1, 1), lambda i: (0, 0))
    return pl.pallas_call(
        body, out_shape=(jax.ShapeDtypeStruct((1, 1), F32), jax.ShapeDtypeStruct((R, n), F32), jax.ShapeDtypeStruct((1, n), F32)),
        grid=(R // tr,), in_specs=[row, vec, row], out_specs=(one, row, vec),
        compiler_params=_cp(("arbitrary",)), name=name)(x, g, tgt)


def _pool_masks(S):
    row = lax.broadcasted_iota(jnp.int32, (S, POOL_W), 0)
    grp = lax.broadcasted_iota(jnp.int32, (S, POOL_W), 1) // 64
    win = jnp.where(grp == 0, 2, jnp.where(grp == 1, 4, jnp.where(grp == 2, 8, 16)))
    cnt = jnp.minimum(row + 1, win).astype(F32)
    return row, grp, cnt


def _by_group(grp, v0, v1, v2, v3):
    return jnp.where(grp == 0, v0, jnp.where(grp == 1, v1, jnp.where(grp == 2, v2, v3)))


def _pool_fwd(proj, bd, scale, name):
    S = proj.shape[0]

    def body(a_ref, bd_ref, sc_ref, d_ref, y_ref):
        a = a_ref[...]
        row, grp, cnt = _pool_masks(S)

        def back(v, k):
            return jnp.where(row >= k, pltpu.roll(v, k, 0), 0.0)

        s1 = a + back(a, 1)
        s2 = s1 + back(s1, 2)
        s3 = s2 + back(s2, 4)
        s4 = s3 + back(s3, 8)
        d = (_by_group(grp, s1, s2, s3, s4) / cnt - a).astype(BF16)
        d_ref[...] = d
        y_ref[...] = (jnp.dot(d, bd_ref[...], preferred_element_type=F32) * sc_ref[...]).astype(BF16)

    full = lambda r, c: pl.BlockSpec((r, c), lambda i: (0, 0))
    return pl.pallas_call(
        body, out_shape=(jax.ShapeDtypeStruct((S, POOL_W), BF16), jax.ShapeDtypeStruct((S, POOL_W), BF16)), grid=(1,),
        in_specs=[pl.BlockSpec((S, POOL_W), lambda i: (0, P_A // POOL_W)), full(POOL_W, POOL_W), full(1, POOL_W)],
        out_specs=(full(S, POOL_W), full(S, POOL_W)), compiler_params=_cp(("arbitrary",)), name=name)(proj, bd, scale)


def _pool_bwd(dya, d, bd, scale, name):
    S = dya.shape[0]

    def body(dy_ref, d_ref, bd_ref, sc_ref, da_ref, dbd_ref, dsc_ref):
        dy = dy_ref[...]
        dv = d_ref[...]
        bdv = bd_ref[...]
        row, grp, cnt = _pool_masks(S)
        yraw = jnp.dot(dv, bdv, preferred_element_type=F32)
        dsc_ref[...] = jnp.sum(dy * yraw, axis=0, keepdims=True)
        tb = (dy * sc_ref[...]).astype(BF16)
        dbd_ref[...] = lax.dot_general(dv, tb, (((0,), (0,)), ((), ())), preferred_element_type=F32)
        dd = lax.dot_general(tb, bdv, (((1,), (1,)), ((), ())), preferred_element_type=F32)
        e = dd / cnt

        def fwd(v, k):
            return jnp.where(row < S - k, pltpu.roll(v, S - k, 0), 0.0)

        r1 = e + fwd(e, 1)
        r2 = r1 + fwd(r1, 2)
        r3 = r2 + fwd(r2, 4)
        r4 = r3 + fwd(r3, 8)
        da_ref[...] = (_by_group(grp, r1, r2, r3, r4) - dd).astype(BF16)

    full = lambda r, c: pl.BlockSpec((r, c), lambda i: (0, 0))
    return pl.pallas_call(
        body, out_shape=(jax.ShapeDtypeStruct((S, POOL_W), BF16), jax.ShapeDtypeStruct((POOL_W, POOL_W), F32),
                         jax.ShapeDtypeStruct((1, POOL_W), F32)), grid=(1,),
        in_specs=[full(S, POOL_W), full(S, POOL_W), full(POOL_W, POOL_W), full(1, POOL_W)],
        out_specs=(full(S, POOL_W), full(POOL_W, POOL_W), full(1, POOL_W)),
        compiler_params=_cp(("arbitrary",)), name=name)(dya, d, bd, scale)


FCOLS = 128


def _log_sigmoid(z):
    return -(jnp.maximum(-z, 0.0) + jnp.log1p(jnp.exp(-jnp.abs(z))))


def _fgate_fwd(proj, bf, name):
    S = proj.shape[0]

    def body(f_ref, b_ref, o_ref):
        v = _log_sigmoid(f_ref[...] + b_ref[...])
        row = lax.broadcasted_iota(jnp.int32, (S, FCOLS), 0)
        k = 1
        while k < S:
            v = v + jnp.where(row >= k, pltpu.roll(v, k, 0), 0.0)
            k *= 2
        o_ref[...] = v

    return pl.pallas_call(
        body, out_shape=jax.ShapeDtypeStruct((S, FCOLS), F32), grid=(1,),
        in_specs=[pl.BlockSpec((S, FCOLS), lambda i: (0, P_F // FCOLS)), pl.BlockSpec((1, FCOLS), lambda i: (0, 0))],
        out_specs=pl.BlockSpec((S, FCOLS), lambda i: (0, 0)), compiler_params=_cp(("arbitrary",)), name=name)(proj, bf)


def _fgate_bwd(dF, proj, bf, name):
    S = proj.shape[0]

    def body(dF_ref, f_ref, b_ref, df_ref, db_ref):
        v = dF_ref[...]
        row = lax.broadcasted_iota(jnp.int32, (S, FCOLS), 0)
        k = 1
        while k < S:
            v = v + jnp.where(row < S - k, pltpu.roll(v, S - k, 0), 0.0)
            k *= 2
        z = f_ref[...] + b_ref[...]
        df = v * (1.0 / (1.0 + jnp.exp(z)))
        db_ref[...] = jnp.sum(df, axis=0, keepdims=True)
        df_ref[...] = jnp.concatenate([df, jnp.zeros_like(df)], axis=1).astype(BF16)

    return pl.pallas_call(
        body, out_shape=(jax.ShapeDtypeStruct((S, 2 * FCOLS), BF16), jax.ShapeDtypeStruct((1, FCOLS), F32)), grid=(1,),
        in_specs=[pl.BlockSpec((S, FCOLS), lambda i: (0, 0)), pl.BlockSpec((S, FCOLS), lambda i: (0, P_F // FCOLS)),
                  pl.BlockSpec((1, FCOLS), lambda i: (0, 0))],
        out_specs=(pl.BlockSpec((S, 2 * FCOLS), lambda i: (0, 0)), pl.BlockSpec((1, FCOLS), lambda i: (0, 0))),
        compiler_params=_cp(("arbitrary",)), name=name)(dF, proj, bf)


def _fox_scores(qe, kj, fq, fk, r0, c0, tq, tk, diagonal):
    s = lax.dot_general(qe, kj, (((1,), (1,)), ((), ())), preferred_element_type=F32) * FOX_SCALE
    s = s + (fq - fk)
    if not diagonal:
        return s
    rows = r0 + lax.broadcasted_iota(jnp.int32, (tq, tk), 0)
    cols = c0 + lax.broadcasted_iota(jnp.int32, (tq, tk), 1)
    return jnp.where(rows >= cols, s, NEG)


def _fox_fwd(qkv, fcol, frow, name, tq=256):
    S = qkv.shape[0]
    tk = tq

    def body(q_ref, k_ref, v_ref, fc_ref, fr_ref, o_ref, o32_ref, lse_ref):
        i = pl.program_id(1)
        r0 = i * tq
        q = q_ref[...]
        half = lax.broadcasted_iota(jnp.int32, (tq, 128), 1) // 64
        qs = [jnp.where(half == e, q, jnp.zeros_like(q)) for e in (0, 1)]
        fqs = [fc_ref[0, :, e:e + 1] for e in (0, 1)]

        def step(j, carry, diagonal=False):
            c0 = pl.multiple_of(j * tk, tk)
            kj = k_ref[pl.ds(c0, tk), :]
            vj = v_ref[pl.ds(c0, tk), :]
            out = []
            for e in (0, 1):
                m, l, acc = carry[e]
                s = _fox_scores(qs[e], kj, fqs[e], fr_ref[0, e:e + 1, pl.ds(c0, tk)], r0, c0, tq, tk, diagonal)
                m_new = jnp.maximum(m, jnp.max(s, axis=-1, keepdims=True))
                alpha = jnp.exp(m - m_new)
                p = jnp.exp(s - m_new)
                out.append((m_new, alpha * l + jnp.sum(p, axis=-1, keepdims=True),
                            alpha * acc + jnp.dot(p.astype(BF16), vj, preferred_element_type=F32)))
            return tuple(out)

        init = (jnp.full((tq, 1), NEG, F32), jnp.zeros((tq, 1), F32), jnp.zeros((tq, 128), F32))
        carry = lax.fori_loop(0, i, step, (init, init))
        carry = step(i, carry, diagonal=True)
        outs = []
        for e in (0, 1):
            m, l, acc = carry[e]
            outs.append(acc / l)
            lse_ref[0, :, e:e + 1] = m + jnp.log(l)
        o = jnp.where(half == 0, outs[0], outs[1])
        o32_ref[...] = o
        o_ref[...] = o.astype(BF16)

    tile = pl.BlockSpec((tq, 128), lambda h, i: (i, h))
    return pl.pallas_call(
        body, out_shape=(jax.ShapeDtypeStruct((S, FOX_W), BF16), jax.ShapeDtypeStruct((S, FOX_W), F32),
                         jax.ShapeDtypeStruct((4, S, 2), F32)), grid=(4, S // tq),
        in_specs=[tile, pl.BlockSpec((S, 128), lambda h, i: (0, 4 + h)), pl.BlockSpec((S, 128), lambda h, i: (0, 8 + h)),
                  pl.BlockSpec((1, tq, 2), lambda h, i: (h, i, 0)), pl.BlockSpec((1, 2, S), lambda h, i: (h, 0, 0))],
        out_specs=(tile, tile, pl.BlockSpec((1, tq, 2), lambda h, i: (h, i, 0))),
        compiler_params=_cp(("parallel", "parallel")), name=name)(qkv, qkv, qkv, fcol, frow)


def _fox_bwd(qkv, o32, do, lse, fcol, frow, name, tq=256):
    S = qkv.shape[0]
    tk = tq
    nq = S // tq

    def body(q_ref, k_ref, v_ref, o_ref, do_ref, lse_ref, fc_ref, fr_ref, dq_ref, dk_ref, dv_ref, dfr_ref, dfc_ref, dk_acc, dv_acc):
        dk_acc[...] = jnp.zeros_like(dk_acc)
        dv_acc[...] = jnp.zeros_like(dv_acc)
        dfr_ref[...] = jnp.zeros_like(dfr_ref)
        half = lax.broadcasted_iota(jnp.int32, (tq, 128), 1) // 64

        def q_block(i, _):
            r0 = pl.multiple_of(i * tq, tq)
            qi = q_ref[pl.ds(r0, tq), :]
            dob = do_ref[pl.ds(r0, tq), :].astype(BF16)
            row_dot = dob.astype(F32) * o_ref[pl.ds(r0, tq), :]
            qs = [jnp.where(half == e, qi, jnp.zeros_like(qi)) for e in (0, 1)]
            dos = [jnp.where(half == e, dob, jnp.zeros_like(dob)) for e in (0, 1)]
            deltas = [jnp.sum(jnp.where(half == e, row_dot, 0.0), axis=-1, keepdims=True) for e in (0, 1)]
            lses = [lse_ref[0, pl.ds(r0, tq), e:e + 1] for e in (0, 1)]
            fqs = [fc_ref[0, pl.ds(r0, tq), e:e + 1] for e in (0, 1)]

            def step(j, carry, diagonal=False):
                dqs, row_sums = carry
                c0 = pl.multiple_of(j * tk, tk)
                kj = k_ref[pl.ds(c0, tk), :]
                vj = v_ref[pl.ds(c0, tk), :]
                new_dq, new_rows, dkc, dvc = [], [], [], []
                for e in (0, 1):
                    s = _fox_scores(qs[e], kj, fqs[e], fr_ref[0, e:e + 1, pl.ds(c0, tk)], r0, c0, tq, tk, diagonal)
                    p = jnp.exp(s - lses[e])
                    dp = lax.dot_general(dos[e], vj, (((1,), (1,)), ((), ())), preferred_element_type=F32)
                    ds = p * (dp - deltas[e])
                    dfr_ref[0, e:e + 1, pl.ds(c0, tk)] -= jnp.sum(ds, axis=0, keepdims=True)
                    new_rows.append(row_sums[e] + jnp.sum(ds, axis=-1, keepdims=True))
                    dsb = (ds * FOX_SCALE).astype(BF16)
                    dkc.append(lax.dot_general(dsb, qi, (((0,), (0,)), ((), ())), preferred_element_type=F32))
                    dvc.append(lax.dot_general(p.astype(BF16), dob, (((0,), (0,)), ((), ())), preferred_element_type=F32))
                    new_dq.append(dqs[e] + jnp.dot(dsb, kj, preferred_element_type=F32))
                dk_acc[pl.ds(c0, tk), :] += jnp.where(half == 0, dkc[0], dkc[1])
                dv_acc[pl.ds(c0, tk), :] += jnp.where(half == 0, dvc[0], dvc[1])
                return tuple(new_dq), tuple(new_rows)

            zero, zero_col = jnp.zeros((tq, 128), F32), jnp.zeros((tq, 1), F32)
            carry = lax.fori_loop(0, i, step, ((zero, zero), (zero_col, zero_col)))
            dqs, row_sums = step(i, carry, diagonal=True)
            for e in (0, 1):
                dfc_ref[0, pl.ds(r0, tq), e:e + 1] = row_sums[e]
            dq_ref[pl.ds(r0, tq), :] = jnp.where(half == 0, dqs[0], dqs[1]).astype(BF16)
            return 0

        lax.fori_loop(0, nq, q_block, 0)
        dk_ref[...] = dk_acc[...].astype(BF16)
        dv_ref[...] = dv_acc[...].astype(BF16)

    col = lambda off: pl.BlockSpec((S, 128), lambda h: (0, off + h))
    hs2 = pl.BlockSpec((1, S, 2), lambda h: (h, 0, 0))
    h2s = pl.BlockSpec((1, 2, S), lambda h: (h, 0, 0))
    return pl.pallas_call(
        body, out_shape=(jax.ShapeDtypeStruct((S, FOX_W), BF16),) * 3 + (jax.ShapeDtypeStruct((4, 2, S), F32),
                                                                         jax.ShapeDtypeStruct((4, S, 2), F32)), grid=(4,),
        in_specs=[col(0), col(4), col(8), col(0), col(0), hs2, hs2, h2s],
        out_specs=(col(0), col(0), col(0), h2s, hs2),
        scratch_shapes=[pltpu.VMEM((S, 128), F32), pltpu.VMEM((S, 128), F32)],
        compiler_params=_cp(("parallel",)), name=name)(qkv, qkv, qkv, o32, do, lse, fcol, frow)


def _gelu(x):
    return 0.5 * x * (1.0 + jnp.tanh(GELU_K * (x + GELU_C * x * x * x)))


def _gelu_grad(x):
    th = jnp.tanh(GELU_K * (x + GELU_C * x * x * x))
    return 0.5 * (1.0 + th) + 0.5 * x * (1.0 - th * th) * GELU_K * (1.0 + 3.0 * GELU_C * x * x)


def _sgu_parts(c, gn, w_ref, bias):
    zc = _gelu(c)
    u, vv = zc[:, :SGU_W], zc[:, SGU_W:]
    rstd = lax.rsqrt(jnp.mean(vv * vv, axis=-1, keepdims=True) + EPS)
    vhat = vv * rstd
    vnb = (vhat * gn).astype(BF16)
    grp = lax.broadcasted_iota(jnp.int32, (SGU_CHUNK, SGU_W), 1) // 64
    mixed = bias
    for gi in range(4):
        mixed = mixed + jnp.where(grp == gi, jnp.dot(w_ref[gi], vnb, preferred_element_type=F32), 0.0)
    return u, rstd, vhat, vnb, grp, mixed


def _sgu_fwd(proj, gn, wm, bias, name):
    S = proj.shape[0]

    def body(c_ref, g_ref, w_ref, b_ref, o_ref):
        u, _, _, _, _, mixed = _sgu_parts(c_ref[...], g_ref[...], w_ref, b_ref[...])
        o_ref[...] = (u * mixed).astype(BF16)

    return pl.pallas_call(
        body, out_shape=jax.ShapeDtypeStruct((S, SGU_W), BF16), grid=(S // SGU_CHUNK,),
        in_specs=[pl.BlockSpec((SGU_CHUNK, 2 * SGU_W), lambda i: (i, P_C // (2 * SGU_W))),
                  pl.BlockSpec((1, SGU_W), lambda i: (0, 0)), pl.BlockSpec((4, SGU_CHUNK, SGU_CHUNK), lambda i: (0, 0, 0)),
                  pl.BlockSpec((SGU_CHUNK, SGU_W), lambda i: (0, 0))],
        out_specs=pl.BlockSpec((SGU_CHUNK, SGU_W), lambda i: (i, 0)),
        compiler_params=_cp(("parallel",)), name=name)(proj, gn, wm, bias)


def _sgu_bwd(dsg, proj, gn, wm, wmt, bias, name):
    S = proj.shape[0]

    def body(dsg_ref, c_ref, g_ref, w_ref, wt_ref, b_ref, dc_ref, dw_ref, db_ref, dg_ref):
        i = pl.program_id(0)

        @pl.when(i == 0)
        def _():
            dw_ref[...] = jnp.zeros_like(dw_ref)
            db_ref[...] = jnp.zeros_like(db_ref)
            dg_ref[...] = jnp.zeros_like(dg_ref)

        c = c_ref[...]
        gn_v = g_ref[...]
        u, rstd, vhat, vnb, grp, mixed = _sgu_parts(c, gn_v, w_ref, b_ref[...])
        dsg_v = dsg_ref[...]
        du = dsg_v * mixed
        dmix = dsg_v * u
        db_ref[...] += dmix
        dmb = dmix.astype(BF16)
        dvn = jnp.zeros((SGU_CHUNK, SGU_W), F32)
        for gi in range(4):
            dmg = jnp.where(grp == gi, dmb, jnp.zeros_like(dmb))
            dw_ref[gi] += lax.dot_general(dmg, vnb, (((1,), (1,)), ((), ())), preferred_element_type=F32)
            dvn = dvn + jnp.where(grp == gi, jnp.dot(wt_ref[gi], dmb, preferred_element_type=F32), 0.0)
        dg_ref[...] += jnp.sum(dvn * vhat, axis=0, keepdims=True)
        t = dvn * gn_v
        dvv = rstd * (t - vhat * jnp.mean(t * vhat, axis=-1, keepdims=True))
        dc_ref[...] = (jnp.concatenate([du, dvv], axis=1) * _gelu_grad(c)).astype(BF16)

    w_spec = pl.BlockSpec((4, SGU_CHUNK, SGU_CHUNK), lambda i: (0, 0, 0))
    tile = pl.BlockSpec((SGU_CHUNK, SGU_W), lambda i: (0, 0))
    vec = pl.BlockSpec((1, SGU_W), lambda i: (0, 0))
    return pl.pallas_call(
        body, out_shape=(jax.ShapeDtypeStruct((S, 2 * SGU_W), BF16), jax.ShapeDtypeStruct((4, SGU_CHUNK, SGU_CHUNK), F32),
                         jax.ShapeDtypeStruct((SGU_CHUNK, SGU_W), F32), jax.ShapeDtypeStruct((1, SGU_W), F32)),
        grid=(S // SGU_CHUNK,),
        in_specs=[pl.BlockSpec((SGU_CHUNK, SGU_W), lambda i: (i, 0)),
                  pl.BlockSpec((SGU_CHUNK, 2 * SGU_W), lambda i: (i, P_C // (2 * SGU_W))), vec, w_spec, w_spec, tile],
        out_specs=(pl.BlockSpec((SGU_CHUNK, 2 * SGU_W), lambda i: (i, 0)), w_spec, tile, vec),
        compiler_params=_cp(("arbitrary",)), name=name)(dsg, proj, gn, wm, wmt, bias)


def _sigmoid(z):
    return 1.0 / (1.0 + jnp.exp(-z))


def _merge_specs(tm):
    row = lambda n: pl.BlockSpec((tm, n), lambda i: (i, 0))
    gate = lambda b: pl.BlockSpec((tm, D), lambda i: (i, b))
    full = lambda r, c: pl.BlockSpec((r, c), lambda i: (0, 0))
    packed = pl.BlockSpec((4, 256, PACK_COLS), lambda i: (0, R_BRANCH // 256, 0))
    return row, gate, full, packed


def _branch_shards(c_ref, j):
    return c_ref[j, :, 0:256], c_ref[j, :, 256:512], c_ref[j, :, 512:768], c_ref[j, :, 768:1024]


def _merge_fwd(proj, ya, o, sg, packed_w, bg, name, tm=256):
    S = proj.shape[0]
    row, gate, full, packed = _merge_specs(tm)

    def body(g0, g1, g2, ya_ref, o_ref, sg_ref, c_ref, bg_ref, out_ref):
        yav, ov, sgv = ya_ref[...], o_ref[...], sg_ref[...]
        for j in range(4):
            cols = slice(256 * j, 256 * (j + 1))
            wa, wb0, wb1, wc = _branch_shards(c_ref, j)
            y = (jnp.dot(yav, wa, preferred_element_type=F32),
                 jnp.dot(ov[:, :256], wb0, preferred_element_type=F32) + jnp.dot(ov[:, 256:], wb1, preferred_element_type=F32),
                 jnp.dot(sgv, wc, preferred_element_type=F32))
            acc = jnp.zeros((tm, 256), F32)
            for b, g_ref in enumerate((g0, g1, g2)):
                acc = acc + _sigmoid(g_ref[:, cols] + bg_ref[:, b * D + 256 * j:b * D + 256 * (j + 1)]) * y[b]
            out_ref[:, cols] = acc.astype(BF16)

    return pl.pallas_call(
        body, out_shape=jax.ShapeDtypeStruct((S, D), BF16), grid=(S // tm,),
        in_specs=[gate(0), gate(1), gate(2), row(POOL_W), row(FOX_W), row(SGU_W), packed, full(1, 3 * D)],
        out_specs=row(D), compiler_params=_cp(("parallel",)), name=name)(proj, proj, proj, ya, o, sg, packed_w, bg)


def _merge_bwd(dm, proj, ya, o, sg, packed_w, bg, grads, name, tm=256):
    S = proj.shape[0]
    row, gate, full, packed = _merge_specs(tm)
    tn_dims = (((0,), (0,)), ((), ()))
    nt_dims = (((1,), (1,)), ((), ()))

    def body(dm_ref, g0, g1, g2, ya_ref, o_ref, sg_ref, c_ref, bg_ref, _, dg_ref, dya_ref, do_ref, dsg_ref, dc_ref, dbg_ref, acc):
        i = pl.program_id(0)

        @pl.when(i == 0)
        def _():
            acc[...] = jnp.zeros_like(acc)
            dbg_ref[...] = jnp.zeros_like(dbg_ref)

        yav, ov, sgv = ya_ref[...], o_ref[...], sg_ref[...]
        o0, o1 = ov[:, :256], ov[:, 256:]
        dya = jnp.zeros((tm, POOL_W), F32)
        do0 = jnp.zeros((tm, 256), F32)
        do1 = jnp.zeros((tm, 256), F32)
        dsg = jnp.zeros((tm, SGU_W), F32)
        for j in range(4):
            cols = slice(256 * j, 256 * (j + 1))
            wa, wb0, wb1, wc = _branch_shards(c_ref, j)
            y = (jnp.dot(yav, wa, preferred_element_type=F32),
                 jnp.dot(o0, wb0, preferred_element_type=F32) + jnp.dot(o1, wb1, preferred_element_type=F32),
                 jnp.dot(sgv, wc, preferred_element_type=F32))
            dmv = dm_ref[:, cols]
            dy = []
            for b, g_ref in enumerate((g0, g1, g2)):
                bcols = slice(b * D + 256 * j, b * D + 256 * (j + 1))
                gt = _sigmoid(g_ref[:, cols] + bg_ref[:, bcols])
                dgp = dmv * y[b] * gt * (1.0 - gt)
                dg_ref[:, bcols] = dgp.astype(BF16)
                dbg_ref[:, bcols] += jnp.sum(dgp, axis=0, keepdims=True)
                dy.append((dmv * gt).astype(BF16))
            dya = dya + lax.dot_general(dy[0], wa, nt_dims, preferred_element_type=F32)
            do0 = do0 + lax.dot_general(dy[1], wb0, nt_dims, preferred_element_type=F32)
            do1 = do1 + lax.dot_general(dy[1], wb1, nt_dims, preferred_element_type=F32)
            dsg = dsg + lax.dot_general(dy[2], wc, nt_dims, preferred_element_type=F32)
            acc[j, :, 0:256] += lax.dot_general(yav, dy[0], tn_dims, preferred_element_type=F32)
            acc[j, :, 256:512] += lax.dot_general(o0, dy[1], tn_dims, preferred_element_type=F32)
            acc[j, :, 512:768] += lax.dot_general(o1, dy[1], tn_dims, preferred_element_type=F32)
            acc[j, :, 768:1024] += lax.dot_general(sgv, dy[2], tn_dims, preferred_element_type=F32)
        dya_ref[...] = dya
        do_ref[:, :256] = do0
        do_ref[:, 256:] = do1
        dsg_ref[...] = dsg

        @pl.when(i == pl.num_programs(0) - 1)
        def _():
            dc_ref[...] = acc[...].astype(dc_ref.dtype)

    return pl.pallas_call(
        body, out_shape=(jax.ShapeDtypeStruct((S, 3 * D), BF16), jax.ShapeDtypeStruct((S, POOL_W), F32),
                         jax.ShapeDtypeStruct((S, FOX_W), F32), jax.ShapeDtypeStruct((S, SGU_W), F32),
                         jax.ShapeDtypeStruct(grads.shape, grads.dtype), jax.ShapeDtypeStruct((1, 3 * D), F32)),
        grid=(S // tm,),
        in_specs=[row(D), gate(0), gate(1), gate(2), row(POOL_W), row(FOX_W), row(SGU_W), packed, full(1, 3 * D), ANY],
        out_specs=(row(3 * D), row(POOL_W), row(FOX_W), row(SGU_W), packed, full(1, 3 * D)),
        scratch_shapes=[pltpu.VMEM((4, 256, PACK_COLS), F32)], input_output_aliases={9: 4},
        compiler_params=_cp(("arbitrary",)), name=name)(dm, proj, proj, proj, ya, o, sg, packed_w, bg, grads)


def _xattn_probs(qh, kh):
    s = lax.dot_general(qh, kh, (((1,), (1,)), ((), ())), preferred_element_type=F32) * X_SCALE
    p = jnp.exp(s - jnp.max(s, axis=-1, keepdims=True))
    return p / jnp.sum(p, axis=-1, keepdims=True)


def _xattn_fwd(xq, kv, name, tq=256):
    S = xq.shape[0]
    M = kv.shape[0]

    def body(q_ref, k_ref, v_ref, o_ref):
        for h in range(XH):
            sl = slice(h * XHD, (h + 1) * XHD)
            p = _xattn_probs(q_ref[:, sl], k_ref[:, sl])
            o_ref[:, sl] = jnp.dot(p.astype(BF16), v_ref[:, sl], preferred_element_type=F32).astype(BF16)

    return pl.pallas_call(
        body, out_shape=jax.ShapeDtypeStruct((S, D), BF16), grid=(S // tq,),
        in_specs=[pl.BlockSpec((tq, D), lambda i: (i, 0)), pl.BlockSpec((M, D), lambda i: (0, 0)),
                  pl.BlockSpec((M, D), lambda i: (0, 1))],
        out_specs=pl.BlockSpec((tq, D), lambda i: (i, 0)), compiler_params=_cp(("parallel",)), name=name)(xq, kv, kv)


def _xattn_bwd(xq, kv, do, name, tq=256):
    S = xq.shape[0]
    M = kv.shape[0]

    def body(q_ref, k_ref, v_ref, do_ref, dq_ref, dkv_ref, dk_acc, dv_acc):
        i = pl.program_id(0)

        @pl.when(i == 0)
        def _():
            dk_acc[...] = jnp.zeros_like(dk_acc)
            dv_acc[...] = jnp.zeros_like(dv_acc)

        for h in range(XH):
            sl = slice(h * XHD, (h + 1) * XHD)
            qh, kh, vh, doh = q_ref[:, sl], k_ref[:, sl], v_ref[:, sl], do_ref[:, sl]
            p = _xattn_probs(qh, kh)
            dp = lax.dot_general(doh, vh, (((1,), (1,)), ((), ())), preferred_element_type=F32)
            ds = p * (dp - jnp.sum(p * dp, axis=-1, keepdims=True))
            dsb = (ds * X_SCALE).astype(BF16)
            dq_ref[:, sl] = jnp.dot(dsb, kh, preferred_element_type=F32).astype(BF16)
            dk_acc[:, sl] += lax.dot_general(dsb, qh, (((0,), (0,)), ((), ())), preferred_element_type=F32)
            dv_acc[:, sl] += lax.dot_general(p.astype(BF16), doh, (((0,), (0,)), ((), ())), preferred_element_type=F32)

        @pl.when(i == pl.num_programs(0) - 1)
        def _():
            dkv_ref[:, :D] = dk_acc[...].astype(BF16)
            dkv_ref[:, D:] = dv_acc[...].astype(BF16)

    return pl.pallas_call(
        body, out_shape=(jax.ShapeDtypeStruct((S, D), BF16), jax.ShapeDtypeStruct((M, 2 * D), BF16)), grid=(S // tq,),
        in_specs=[pl.BlockSpec((tq, D), lambda i: (i, 0)), pl.BlockSpec((M, D), lambda i: (0, 0)),
                  pl.BlockSpec((M, D), lambda i: (0, 1)), pl.BlockSpec((tq, D), lambda i: (i, 0))],
        out_specs=(pl.BlockSpec((tq, D), lambda i: (i, 0)), pl.BlockSpec((M, 2 * D), lambda i: (0, 0))),
        scratch_shapes=[pltpu.VMEM((M, D), F32), pltpu.VMEM((M, D), F32)],
        compiler_params=_cp(("arbitrary",)), name=name)(xq, kv, kv, do)


def _adam_math(gv, wv, mv, vv):
    c1 = 1.0 - ADAM_B1 ** ADAM_STEP
    c2 = 1.0 - ADAM_B2 ** ADAM_STEP
    nm = ADAM_B1 * mv + (1.0 - ADAM_B1) * gv
    nv = ADAM_B2 * vv + (1.0 - ADAM_B2) * (gv * gv)
    return -ADAM_LR * ((nm / c1) / (jnp.sqrt(nv / c2) + ADAM_EPS) + ADAM_WD * wv), nm, nv


def _adamw(g, w, m, v, name, block=None):
    if block is None:
        block = (1, 256 if g.shape[1] % 256 == 0 else g.shape[1], g.shape[2])
    grid = tuple(s // b for s, b in zip(g.shape, block))

    def body(g_ref, w_ref, m_ref, v_ref, d_ref, nm_ref, nv_ref):
        d_ref[...], nm_ref[...], nv_ref[...] = _adam_math(g_ref[...], w_ref[...], m_ref[...], v_ref[...])

    blk = pl.BlockSpec(block, lambda a, b, c: (a, b, c))
    return pl.pallas_call(
        body, out_shape=(jax.ShapeDtypeStruct(g.shape, F32),) * 3, grid=grid,
        in_specs=[blk] * 4, out_specs=(blk,) * 3, compiler_params=_cp(("parallel",) * 3), name=name)(g, w, m, v)


def _adamw_packed(red, w, m, v, g_index, name, tr=256):
    L, r, c = w.shape
    tr = min(tr, r)

    def body(g0_ref, g1_ref, w_ref, m_ref, v_ref, g_ref, d_ref, nm_ref, nv_ref):
        gv = jnp.where(pl.program_id(0) == 0, g0_ref[...], g1_ref[...])
        g_ref[0] = gv
        d_ref[0], nm_ref[0], nv_ref[0] = _adam_math(gv, w_ref[0], m_ref[0], v_ref[0])

    gblk = pl.BlockSpec((tr, c), lambda l, i: g_index(i))
    blk = pl.BlockSpec((1, tr, c), lambda l, i: (l, i, 0))
    return pl.pallas_call(
        body, out_shape=(jax.ShapeDtypeStruct(w.shape, F32),) * 4, grid=(L, r // tr),
        in_specs=[gblk, gblk, blk, blk, blk], out_specs=(blk,) * 4,
        compiler_params=_cp(("parallel", "parallel")), name=name)(red[0], red[1], w, m, v)


def _sum_slots(a, out_dtype, name, tr=496):
    n, R, C = a.shape
    tr = tr if R % tr == 0 else R

    def body(a_ref, o_ref):
        acc = a_ref[0].astype(F32)
        for k in range(1, n):
            acc = acc + a_ref[k].astype(F32)
        o_ref[...] = acc.astype(out_dtype)

    return pl.pallas_call(
        body, out_shape=jax.ShapeDtypeStruct((R, C), out_dtype), grid=(R // tr,),
        in_specs=[pl.BlockSpec((n, tr, C), lambda i: (0, i, 0))], out_specs=pl.BlockSpec((tr, C), lambda i: (i, 0)),
        compiler_params=_cp(("parallel",)), name=name)(a)


def _add_pair(a, b, name, tr=496):
    n, R, C = a.shape
    tr = tr if R % tr == 0 else R

    def body(a_ref, b_ref, o_ref):
        o_ref[...] = (a_ref[...].astype(F32) + b_ref[...].astype(F32)).astype(BF16)

    blk = pl.BlockSpec((1, tr, C), lambda k, i: (k, i, 0))
    return pl.pallas_call(
        body, out_shape=jax.ShapeDtypeStruct(a.shape, BF16), grid=(n, R // tr), in_specs=[blk, blk], out_specs=blk,
        compiler_params=_cp(("parallel", "parallel")), name=name)(a, b)


LANDING = pl.BlockSpec(memory_space=pltpu.VMEM)


def _landing_params(shape, dtype):
    return pltpu.CompilerParams(vmem_limit_bytes=math.prod(shape) * jnp.dtype(dtype).itemsize + 4 * 1024 * 1024)


def _place():
    return lax.axis_index("x"), lax.axis_index("y"), lax.axis_index("c")


def _other_chips(x, y):
    return [(1 - x, y), (x, 1 - y), (1 - x, 1 - y)]


def _row_chunks(rows, want, align=16):
    n = want
    while n > 1 and rows % (n * align):
        n -= 1
    return n


def _gather_weights(shard, name, nch=5):
    R, C = shard.shape
    half = R // 2
    nch = _row_chunks(half, nch)
    cr = half // nch

    def body(s_ref, o_ref, send_sems, recv_sems, local_sem):
        x, y, c = _place()
        j = 2 * x + y
        mine0 = c * half
        theirs0 = (1 - c) * half

        def rows(jj, r0, q):
            return o_ref.at[jj, pl.ds(pl.multiple_of(r0 + q * cr, 16), cr), :]

        def copy(k, src, dst, to):
            return pltpu.make_async_remote_copy(src_ref=src, dst_ref=dst, send_sem=send_sems.at[k], recv_sem=recv_sems.at[k],
                                                device_id=to, device_id_type=MESH)

        own = pltpu.make_async_copy(s_ref, o_ref.at[j], local_sem)
        own.start()
        chips = _other_chips(x, y)
        first = []
        for q in range(nch):
            for k, (px, py) in enumerate(chips):
                src = s_ref.at[pl.ds(pl.multiple_of(mine0 + q * cr, 16), cr), :]
                first.append(copy(k * nch + q, src, rows(j, mine0, q), (px, py, c)))
        for cp in first:
            cp.start()
        passed = []
        for q in range(nch):
            for k, (px, py) in enumerate(chips):
                jj = 2 * px + py
                copy(k * nch + q, rows(jj, mine0, q), rows(jj, mine0, q), (px, py, c)).wait_recv()
                fw = copy((3 + k) * nch + q, rows(jj, mine0, q), rows(jj, mine0, q), (x, y, 1 - c))
                fw.start()
                passed.append(fw)
        for q in range(nch):
            for k, (px, py) in enumerate(chips):
                jj = 2 * px + py
                copy((3 + k) * nch + q, rows(jj, theirs0, q), rows(jj, theirs0, q), (x, y, 1 - c)).wait_recv()
        for cp in first + passed:
            cp.wait_send()
        own.wait()

    return pl.pallas_call(
        body, out_shape=jax.ShapeDtypeStruct((4, R, C), shard.dtype), in_specs=[ANY], out_specs=LANDING,
        scratch_shapes=[pltpu.SemaphoreType.DMA((6 * nch,)), pltpu.SemaphoreType.DMA((6 * nch,)), pltpu.SemaphoreType.DMA],
        compiler_params=_landing_params((4, R, C), shard.dtype), name=name)(shard)


def _pair_split(g, name, nch=5):
    n, R, C = g.shape
    half = R // 2
    nch = _row_chunks(half, nch)
    cr = half // nch

    def body(g_ref, own_ref, got_ref, send_sems, recv_sems, local_sem):
        x, y, c = _place()
        mine0 = pl.multiple_of(c * half, 16)
        theirs0 = (1 - c) * half
        keep = pltpu.make_async_copy(g_ref.at[:, pl.ds(mine0, half), :], own_ref, local_sem)
        keep.start()
        cps = []
        for s in range(n):
            for q in range(nch):
                src = g_ref.at[s, pl.ds(pl.multiple_of(theirs0 + q * cr, 16), cr), :]
                cps.append(pltpu.make_async_remote_copy(
                    src_ref=src, dst_ref=got_ref.at[s, pl.ds(q * cr, cr), :], send_sem=send_sems.at[s * nch + q],
                    recv_sem=recv_sems.at[s * nch + q], device_id=(x, y, 1 - c), device_id_type=MESH))
        for cp in cps:
            cp.start()
        for cp in cps:
            cp.wait()
        keep.wait()

    sh = jax.ShapeDtypeStruct((n, half, C), g.dtype)
    return pl.pallas_call(
        body, out_shape=(sh, sh), in_specs=[ANY], out_specs=(ANY, LANDING),
        scratch_shapes=[pltpu.SemaphoreType.DMA((n * nch,)), pltpu.SemaphoreType.DMA((n * nch,)), pltpu.SemaphoreType.DMA],
        compiler_params=_landing_params(sh.shape, g.dtype), name=name)(g)


def _chip_all_to_all(p, name, nch=5):
    R = p.shape[1]
    nch = _row_chunks(R, nch)
    cr = R // nch

    def body(p_ref, o_ref, send_sems, recv_sems, local_sem):
        x, y, c = _place()
        j = 2 * x + y
        own = pltpu.make_async_copy(p_ref.at[j], o_ref.at[j], local_sem)
        own.start()
        cps = []
        for q in range(nch):
            for k, (px, py) in enumerate(_other_chips(x, y)):
                cps.append(pltpu.make_async_remote_copy(
                    src_ref=p_ref.at[2 * px + py, pl.ds(q * cr, cr), :], dst_ref=o_ref.at[j, pl.ds(q * cr, cr), :],
                    send_sem=send_sems.at[k * nch + q], recv_sem=recv_sems.at[k * nch + q], device_id=(px, py, c),
                    device_id_type=MESH))
        for cp in cps:
            cp.start()
        for cp in cps:
            cp.wait()
        own.wait()

    return pl.pallas_call(
        body, out_shape=jax.ShapeDtypeStruct(p.shape, p.dtype), in_specs=[ANY], out_specs=LANDING,
        scratch_shapes=[pltpu.SemaphoreType.DMA((3 * nch,)), pltpu.SemaphoreType.DMA((3 * nch,)), pltpu.SemaphoreType.DMA],
        compiler_params=_landing_params(p.shape, p.dtype), name=name)(p)


def _pair_gather(t, name, nch=10):
    R = t.shape[0]
    nch = _row_chunks(R, nch, 8)
    cr = R // nch

    def body(t_ref, o_ref, send_sems, recv_sems, local_sem):
        x, y, c = _place()
        own = pltpu.make_async_copy(t_ref, o_ref.at[c], local_sem)
        own.start()
        cps = [pltpu.make_async_remote_copy(src_ref=t_ref.at[pl.ds(q * cr, cr), :], dst_ref=o_ref.at[c, pl.ds(q * cr, cr), :],
                                            send_sem=send_sems.at[q], recv_sem=recv_sems.at[q], device_id=(x, y, 1 - c),
                                            device_id_type=MESH) for q in range(nch)]
        for cp in cps:
            cp.start()
        for cp in cps:
            cp.wait()
        own.wait()

    return pl.pallas_call(
        body, out_shape=jax.ShapeDtypeStruct((2,) + t.shape, t.dtype), in_specs=[ANY], out_specs=LANDING,
        scratch_shapes=[pltpu.SemaphoreType.DMA((nch,)), pltpu.SemaphoreType.DMA((nch,)), pltpu.SemaphoreType.DMA],
        compiler_params=_landing_params((2,) + t.shape, t.dtype), name=name)(t)


def _reduce_scatter(g, tag):
    own, got = _pair_split(g, f"rs_pair_{tag}")
    p = _add_pair(own, got, f"rs_add_{tag}")
    q = _chip_all_to_all(p, f"rs_a2a_{tag}")
    t = _sum_slots(q, F32, f"rs_sum_{tag}")
    both = _pair_gather(t, f"rs_join_{tag}")
    return both.reshape(g.shape[1], g.shape[2])


def _all_reduce_small(v, tag):
    pair = _pair_gather(v, f"ar_pair_{tag}")
    p = _sum_slots(pair, F32, f"ar_add_{tag}")
    q = _chip_all_to_all(jnp.broadcast_to(p[None], (4,) + p.shape), f"ar_a2a_{tag}")
    return _sum_slots(q, F32, f"ar_sum_{tag}")


R_FF1, R_FF2, R_XKV, R_BRANCH, R_OUT, R_XQ, R_XO, R_WIN = 0, 1024, 2048, 2560, 2816, 3072, 3328, 3584
WIN_ROWS = N_IN // 4


def _w_in_t(a):
    return jnp.transpose(a, (2, 0, 1))


def _pack_shard(w, l):
    xkv, wb = w['w_xkv'][l], w['w_branch_b'][l]
    parts = [w['w_ff1'][l], w['w_ff2'][l], jnp.concatenate([xkv[:512], xkv[512:]], axis=1),
             jnp.concatenate([w['w_branch_a'][l], wb[:256], wb[256:], w['w_branch_c'][l]], axis=1),
             w['w_out'][l], w['w_xq'][l], w['w_xo'][l],
             jnp.pad(_w_in_t(w['w_in'])[:, l, :], ((0, PACK_ROWS - R_WIN - WIN_ROWS), (0, 0)))]
    return jnp.concatenate(parts, axis=0).astype(BF16)


def _w_in_rows(gathered):
    t = gathered[:, R_WIN:R_WIN + WIN_ROWS, :].reshape(N_IN, PACK_COLS)
    return jnp.concatenate([t[2312:5384], t[256:1792], t[1800:2312], t[0:256],
                            jnp.pad(t[1792:1800], ((0, NP - P_F - 8), (0, 0)))], axis=0)


def _w_in_grad_rows(grads, dwt):
    t = jnp.concatenate([dwt[P_A:P_A + 256], dwt[P_Q:P_Q + 1536], dwt[P_F:P_F + 8], dwt[P_C:P_C + 512], dwt[P_G:P_G + 3072]],
                        axis=0)
    return lax.dynamic_update_slice(grads, t.reshape(4, WIN_ROWS, PACK_COLS).astype(grads.dtype), (0, R_WIN, 0))


def _small_prep(sw, l):
    eye = jnp.eye(4, dtype=F32)
    bd = jnp.einsum('gh,gcd->gchd', eye, sw['pool_w'][l]).reshape(POOL_W, POOL_W).astype(BF16)
    tril = jnp.tril(jnp.ones((SGU_CHUNK, SGU_CHUNK), F32))
    wm = (sw['sgu_w'][l] * tril[None]).astype(BF16)
    return dict(
        g_mix=sw['norm_mix_g'][l][None], g_x=sw['norm_xattn_g'][l][None], g_mem=sw['norm_mem_g'][l][None],
        g_ffn=sw['norm_ffn_g'][l][None], bd=bd, pool_scale=sw['pool_scale'][l][None],
        bf=jnp.pad(sw['b_forget'][l], (0, FCOLS - 8))[None], sgu_g=sw['sgu_norm_g'][l][None], wm=wm,
        wmt=jnp.transpose(wm, (0, 2, 1)), sgu_bias=jnp.repeat(sw['sgu_b'][l].T, 64, axis=1), bg=sw['b_gate'][l][None])


def _rows4(r0):
    return dict(n=D, k=D, b_block=(4, 256, 512), b_index=lambda i, j, k: (0, r0 // 256, j))


def _rows_t(r0):
    return dict(tb=True, n=D, k=D, tn=256, b_block=(None, 256, PACK_COLS), b_index=lambda i, j, k: (j, r0 // 256, 0))


def _rows_grad(r0):
    return dict(ta=True, tm=256, tn=512, o_block=(None, 256, 512), o_index=lambda i, j, k: (i, r0 // 256, j))


def _add_to(r, e):
    return e + r


def _layer_fwd(x, mem, G, w_in_t, sp, l):
    t = f"l{l}"
    S = x.shape[0]
    h = _rms_fwd(x, sp['g_mix'], f"rms_mix_{t}")
    proj = _mm(h, w_in_t, name=f"proj_{t}", out_dtype=F32, tb=True)
    d, ya = _pool_fwd(proj, sp['bd'], sp['pool_scale'], f"pool_fwd_{t}")
    fcum = _fgate_fwd(proj, sp['bf'], f"fgate_fwd_{t}")
    f8 = fcum[:, :8]
    fcol = f8.reshape(S, 4, 2).transpose(1, 0, 2)
    frow = f8.T.reshape(4, 2, S)
    qkv = proj[:, P_Q:P_Q + 3 * FOX_W].astype(BF16)
    o, o32, lse = _fox_fwd(qkv, fcol, frow, f"fox_fwd_{t}")
    sg = _sgu_fwd(proj, sp['sgu_g'], sp['wm'], sp['sgu_bias'], f"sgu_fwd_{t}")
    merged = _merge_fwd(proj, ya, o, sg, G, sp['bg'], f"merge_fwd_{t}")
    x1 = _mm(merged, G, name=f"out_{t}", out_dtype=F32, extra=x, epi=_add_to, **_rows4(R_OUT))
    hx = _rms_fwd(x1, sp['g_x'], f"rms_x_{t}")
    hm = _rms_fwd(mem, sp['g_mem'], f"rms_mem_{t}")
    xq = _mm(hx, G, name=f"xq_{t}", out_dtype=BF16, **_rows4(R_XQ))
    kv = _mm(hm, G, name=f"xkv_{t}", out_dtype=BF16, n=2 * D, k=D, tn=512, tk=512, b_block=(None, 512, 512),
             b_index=lambda i, j, k: (j, R_XKV // 512, k))
    o2 = _xattn_fwd(xq, kv, f"xattn_fwd_{t}")
    x2 = _mm(o2, G, name=f"xo_{t}", out_dtype=F32, extra=x1, epi=_add_to, **_rows4(R_XO))
    hf = _rms_fwd(x2, sp['g_ffn'], f"rms_ffn_{t}")
    z = _mm(hf, G, name=f"ff1_{t}", out_dtype=F32, n=D_FF, k=D, tn=512, b_block=(None, 1024, 512),
            b_index=lambda i, j, k: (j // 2, R_FF1 // 1024, j % 2))
    x3 = _mm(z, G, name=f"ff2_{t}", out_dtype=F32, a_fn=_relu2, extra=x2, epi=_add_to, n=D, k=D_FF, tk=1024,
             b_block=(None, 1024, 512), b_index=lambda i, j, k: (k, R_FF2 // 1024, j))
    saved = dict(x=x, h=h, proj=proj, d=d, ya=ya, fcol=fcol, frow=frow, qkv=qkv, o=o, o32=o32, lse=lse, sg=sg, merged=merged, x1=x1,
                 hx=hx, hm=hm, xq=xq, kv=kv, o2=o2, x2=x2, hf=hf, z=z)
    return x3, saved


def _layer_bwd(dx3, mem, G, w_in_t, sp, sv, l):
    t = f"l{l}"
    S = dx3.shape[0]
    gs = {}
    gp = jnp.zeros((4, PACK_ROWS, PACK_COLS), BF16)
    dz = _mm(dx3, G, name=f"d_a2_{t}", out_dtype=BF16, tb=True, n=D_FF, k=D, tn=512, b_block=(None, 512, PACK_COLS),
             b_index=lambda i, j, k: (j // 2, R_FF2 // 512 + j % 2, 0), extra=sv['z'],
             epi=lambda r, e: r * (2.0 * jnp.maximum(e, 0.0)))
    gp = _mm(sv['z'], dx3, name=f"dw_ff2_{t}", out_dtype=BF16, ta=True, a_fn=_relu2, into=gp, tm=1024, tn=512,
             o_block=(None, 1024, 512), o_index=lambda i, j, k: (i, R_FF2 // 1024, j))
    gp = _mm(sv['hf'], dz, name=f"dw_ff1_{t}", out_dtype=BF16, ta=True, into=gp, tm=1024, tn=512,
             o_block=(None, 1024, 512), o_index=lambda i, j, k: (j // 2, R_FF1 // 1024, j % 2))
    dhf = _mm(dz, G, name=f"d_hf_{t}", out_dtype=F32, tb=True, n=D, k=D_FF, tn=512, tk=1024, b_block=(None, 512, PACK_COLS),
              b_index=lambda i, j, k: (k, R_FF1 // 512 + j, 0))
    dx2, gs['norm_ffn_g'] = _rms_bwd(dhf, sv['x2'], sp['g_ffn'], dx3, f"rms_ffn_bwd_{t}")
    do2 = _mm(dx2, G, name=f"d_o2_{t}", out_dtype=BF16, **_rows_t(R_XO))
    gp = _mm(sv['o2'], dx2, name=f"dw_xo_{t}", out_dtype=BF16, into=gp, **_rows_grad(R_XO))
    dxq, dkv = _xattn_bwd(sv['xq'], sv['kv'], do2, f"xattn_bwd_{t}")
    gp = _mm(sv['hm'], dkv, name=f"dw_xkv_{t}", out_dtype=BF16, ta=True, into=gp, tm=512, tn=512,
             o_block=(None, 512, 512), o_index=lambda i, j, k: (j, R_XKV // 512, i))
    dhm = _mm(dkv, G, name=f"d_hm_{t}", out_dtype=F32, tb=True, n=D, k=2 * D, tn=512, tk=512, b_block=(None, 512, 512),
              b_index=lambda i, j, k: (k, R_XKV // 512, j))
    gs['norm_mem_g'] = _rms_bwd(dhm, mem, sp['g_mem'], None, f"rms_mem_bwd_{t}")
    gp = _mm(sv['hx'], dxq, name=f"dw_xq_{t}", out_dtype=BF16, into=gp, **_rows_grad(R_XQ))
    dhx = _mm(dxq, G, name=f"d_hx_{t}", out_dtype=F32, **_rows_t(R_XQ))
    dx1, gs['norm_xattn_g'] = _rms_bwd(dhx, sv['x1'], sp['g_x'], dx2, f"rms_x_bwd_{t}")
    gp = _mm(sv['merged'], dx1, name=f"dw_out_{t}", out_dtype=BF16, into=gp, **_rows_grad(R_OUT))
    dm = _mm(dx1, G, name=f"d_merged_{t}", out_dtype=F32, **_rows_t(R_OUT))
    dg, dya, do, dsg, gp, gs['b_gate'] = _merge_bwd(dm, sv['proj'], sv['ya'], sv['o'], sv['sg'], G, sp['bg'], gp, f"merge_bwd_{t}")
    dc, dws, dbias, gs['sgu_norm_g'] = _sgu_bwd(dsg, sv['proj'], sp['sgu_g'], sp['wm'], sp['wmt'], sp['sgu_bias'], f"sgu_bwd_{t}")
    tril = jnp.tril(jnp.ones((SGU_CHUNK, SGU_CHUNK), F32))
    gs['sgu_w'] = dws * tril[None]
    gs['sgu_b'] = dbias.reshape(SGU_CHUNK, 4, 64).sum(-1).T
    dq, dk, dv, dfrow, dfcol = _fox_bwd(sv['qkv'], sv['o32'], do, sv['lse'], sv['fcol'], sv['frow'], f"fox_bwd_{t}")
    dF = jnp.pad(dfrow.reshape(8, S).T + dfcol.transpose(1, 0, 2).reshape(S, 8), ((0, 0), (0, FCOLS - 8)))
    df, dbf = _fgate_bwd(dF, sv['proj'], sp['bf'], f"fgate_bwd_{t}")
    gs['b_forget'] = dbf[:, :8]
    da, dbd, gs['pool_scale'] = _pool_bwd(dya, sv['d'], sp['bd'], sp['pool_scale'], f"pool_bwd_{t}")
    gs['pool_w'] = jnp.stack([dbd[g * 64:(g + 1) * 64, g * 64:(g + 1) * 64] for g in range(4)])
    dproj = jnp.concatenate([dg, dq, dk, dv, dc, da, df], axis=1)
    dwt = _mm(dproj, sv['h'], name=f"dw_in_{t}", out_dtype=BF16, ta=True, tm=512, tn=1024)
    gp = _w_in_grad_rows(gp, dwt)
    dh = _mm(dproj, w_in_t, name=f"d_h_{t}", out_dtype=F32, tk=512)
    dx, gs['norm_mix_g'] = _rms_bwd(dh, sv['x'], sp['g_mix'], dx1, f"rms_mix_bwd_{t}")
    return dx, gp, gs


SMALL_ROWS = 1424
GRAD_BLOCKS = {
    'w_ff1': (1024, lambda i: (R_FF1 // 256 + i, 0)), 'w_ff2': (1024, lambda i: (R_FF2 // 256 + i, 0)),
    'w_out': (1024, lambda i: (R_OUT // 256 + i, 0)), 'w_xq': (1024, lambda i: (R_XQ // 256 + i, 0)),
    'w_xo': (1024, lambda i: (R_XO // 256 + i, 0)), 'w_xkv': (512, lambda i: (R_XKV // 256 + i % 2, i // 2)),
    'w_branch_a': (256, lambda i: (R_BRANCH // 256, 0)), 'w_branch_b': (256, lambda i: (R_BRANCH // 256, 1 + i)),
    'w_branch_c': (256, lambda i: (R_BRANCH // 256, 3)),
}


def _pack_small(parts):
    flat = jnp.concatenate([p.reshape(-1) for p in parts])
    return jnp.pad(flat, (0, SMALL_ROWS * 128 - flat.shape[0])).reshape(SMALL_ROWS, 128)


def _unpack_small(buf, shapes):
    flat, out, r = buf.reshape(-1), [], 0
    for s in shapes:
        n = math.prod(s)
        out.append(flat[r:r + n].reshape(s))
        r += n
    return out


def kernel(x, mem, norm_mix_g, w_in, b_forget, pool_w, pool_scale, sgu_norm_g, sgu_w, sgu_b, w_branch_a, w_branch_b, w_branch_c, b_gate, w_out, norm_xattn_g, norm_mem_g, w_xq, w_xkv, w_xo, norm_ffn_g, w_ff1, w_ff2, final_norm_g, loss_target, m_norm_mix_g, m_w_in, m_b_forget, m_pool_w, m_pool_scale, m_sgu_norm_g, m_sgu_w, m_sgu_b, m_w_branch_a, m_w_branch_b, m_w_branch_c, m_b_gate, m_w_out, m_norm_xattn_g, m_norm_mem_g, m_w_xq, m_w_xkv, m_w_xo, m_norm_ffn_g, m_w_ff1, m_w_ff2, m_final_norm_g, v_norm_mix_g, v_w_in, v_b_forget, v_pool_w, v_pool_scale, v_sgu_norm_g, v_sgu_w, v_sgu_b, v_w_branch_a, v_w_branch_b, v_w_branch_c, v_b_gate, v_w_out, v_norm_xattn_g, v_norm_mem_g, v_w_xq, v_w_xkv, v_w_xo, v_norm_ffn_g, v_w_ff1, v_w_ff2, v_final_norm_g):
    args = (norm_mix_g, w_in, b_forget, pool_w, pool_scale, sgu_norm_g, sgu_w, sgu_b, w_branch_a, w_branch_b, w_branch_c, b_gate,
            w_out, norm_xattn_g, norm_mem_g, w_xq, w_xkv, w_xo, norm_ffn_g, w_ff1, w_ff2, final_norm_g)
    margs = (m_norm_mix_g, m_w_in, m_b_forget, m_pool_w, m_pool_scale, m_sgu_norm_g, m_sgu_w, m_sgu_b, m_w_branch_a, m_w_branch_b,
             m_w_branch_c, m_b_gate, m_w_out, m_norm_xattn_g, m_norm_mem_g, m_w_xq, m_w_xkv, m_w_xo, m_norm_ffn_g, m_w_ff1, m_w_ff2,
             m_final_norm_g)
    vargs = (v_norm_mix_g, v_w_in, v_b_forget, v_pool_w, v_pool_scale, v_sgu_norm_g, v_sgu_w, v_sgu_b, v_w_branch_a, v_w_branch_b,
             v_w_branch_c, v_b_gate, v_w_out, v_norm_xattn_g, v_norm_mem_g, v_w_xq, v_w_xkv, v_w_xo, v_norm_ffn_g, v_w_ff1, v_w_ff2,
             v_final_norm_g)
    w = dict(zip(W_NAMES, args))
    mo = dict(zip(W_NAMES, margs))
    vo = dict(zip(W_NAMES, vargs))
    xs, mems, tgt = x[0], mem[0], loss_target[0]

    G = [_gather_weights(_pack_shard(w, l), f"gather_w_l{l}") for l in range(DEPTH)]
    w_in_t = [_w_in_rows(g) for g in G]
    preps = [_small_prep(w, l) for l in range(DEPTH)]

    act, saved = xs, []
    for l in range(DEPTH):
        act, sv = _layer_fwd(act, mems, G[l], w_in_t[l], preps[l], l)
        saved.append(sv)
    loss_part, dact, d_final_g = _loss_head(act, w['final_norm_g'][None], tgt, "loss_head")

    red, small_g = [None] * DEPTH, [None] * DEPTH
    for l in reversed(range(DEPTH)):
        dact, gp, small_g[l] = _layer_bwd(dact, mems, G[l], w_in_t[l], preps[l], saved[l], l)
        red[l] = _reduce_scatter(gp, f"l{l}")
    grad_x = dact[None]

    per_layer = [n for n in SMALL_NAMES if n != 'final_norm_g']
    small_shapes = [w[n].shape for n in per_layer] + [(D,), (1,)]
    parts = [jnp.stack([small_g[l][n].reshape(w[n].shape[1:]) for l in range(DEPTH)]) for n in per_layer]
    small_red = _unpack_small(_all_reduce_small(_pack_small(parts + [d_final_g.reshape(D), loss_part.reshape(1)]), "small"), small_shapes)
    grads = dict(zip(per_layer + ['final_norm_g'], small_red[:-1]))
    loss = small_red[-1].reshape(())

    delta, new_m, new_v = {}, {}, {}
    for n, (c, g_index) in GRAD_BLOCKS.items():
        grads[n], delta[n], new_m[n], new_v[n] = _adamw_packed(red, w[n], mo[n], vo[n], g_index, f"adamw_{n}")
    g_t = jnp.stack([r[R_WIN:R_WIN + WIN_ROWS] for r in red], axis=1)
    upd = _adamw(g_t, _w_in_t(w['w_in']), _w_in_t(mo['w_in']), _w_in_t(vo['w_in']), "adamw_w_in", block=(WIN_ROWS, DEPTH, 128))
    grads['w_in'], delta['w_in'], new_m['w_in'], new_v['w_in'] = [jnp.transpose(a, (1, 2, 0)) for a in (g_t,) + tuple(upd)]
    small_all = per_layer + ['final_norm_g']
    shapes_all = [w[n].shape for n in small_all]
    packed = [_pack_small([d[n] for n in small_all])[None] for d in (grads, w, mo, vo)]
    ds, ms, vs = _adamw(*packed, "adamw_small")
    for n, a, b, c in zip(small_all, _unpack_small(ds[0], shapes_all), _unpack_small(ms[0], shapes_all), _unpack_small(vs[0], shapes_all)):
        delta[n], new_m[n], new_v[n] = a, b, c

    return (loss, grad_x, *[grads[n] for n in W_NAMES], *[delta[n] for n in W_NAMES], *[new_m[n] for n in W_NAMES],
            *[new_v[n] for n in W_NAMES])
```

```python
import math

import jax
import jax.numpy as jnp
from jax import lax
from jax.experimental import pallas as pl
from jax.experimental.pallas import tpu as pltpu

F32 = jnp.float32
BF16 = jnp.bfloat16

D = 1024
DEPTH = 2
POOL_W = 256
FOX_W = 512
SGU_W = 256
SGU_CHUNK = 128
N_IN = 5384
P_G, P_Q, P_K, P_V, P_C, P_A, P_F = 0, 3072, 3584, 4096, 4608, 5120, 5376
NP = 5632
XH, XHD = 4, 256
D_FF = 4096
EPS = 1e-6
NEG = -1e30
FOX_SCALE = 64 ** -0.5
X_SCALE = 256 ** -0.5
GELU_K = math.sqrt(2.0 / math.pi)
GELU_C = 0.044715

ADAM_LR, ADAM_B1, ADAM_B2, ADAM_EPS, ADAM_WD, ADAM_STEP = 0.001, 0.9, 0.999, 1e-08, 0.01, 10

VMEM_LIMIT = 48 * 1024 * 1024
MESH = pl.DeviceIdType.MESH

IN_NAMES = ['x', 'mem', 'norm_mix_g', 'w_in', 'b_forget', 'pool_w', 'pool_scale', 'sgu_norm_g', 'sgu_w', 'sgu_b',
            'w_branch_a', 'w_branch_b', 'w_branch_c', 'b_gate', 'w_out', 'norm_xattn_g', 'norm_mem_g', 'w_xq',
            'w_xkv', 'w_xo', 'norm_ffn_g', 'w_ff1', 'w_ff2', 'final_norm_g']
W_NAMES = IN_NAMES[2:]
BIG_NAMES = ['w_in', 'w_branch_a', 'w_branch_b', 'w_branch_c', 'w_out', 'w_xq', 'w_xkv', 'w_xo', 'w_ff1', 'w_ff2']
SMALL_NAMES = [n for n in W_NAMES if n not in BIG_NAMES]
PACK_COLS = 1024
PACK_ROWS = 4960


ANY = pl.BlockSpec(memory_space=pl.ANY)


def _cp(sem=None):
    return pltpu.CompilerParams(dimension_semantics=sem, vmem_limit_bytes=VMEM_LIMIT)


def _mm(a, b, *, name, out_dtype, ta=False, tb=False, tm=1024, tn=512, tk=1024, a_fn=None, extra=None, epi=None,
        n=None, k=None, b_block=None, b_index=None, into=None, o_block=None, o_index=None):
    M = a.shape[1] if ta else a.shape[0]
    K = k if k is not None else (a.shape[0] if ta else a.shape[1])
    N = n if n is not None else (b.shape[0] if tb else b.shape[1])
    tm, tn, tk = min(tm, M), min(tn, N), min(tk, K)
    assert M % tm == 0 and N % tn == 0 and K % tk == 0, (name, M, N, K)
    nk = K // tk
    a_spec = pl.BlockSpec((tk, tm), lambda i, j, k: (k, i)) if ta else pl.BlockSpec((tm, tk), lambda i, j, k: (i, k))
    if b_block is not None:
        b_spec = pl.BlockSpec(b_block, b_index)
    else:
        b_spec = pl.BlockSpec((tn, tk), lambda i, j, k: (j, k)) if tb else pl.BlockSpec((tk, tn), lambda i, j, k: (k, j))
    dn = (((0 if ta else 1,), (1 if tb else 0,)), ((), ()))
    tile = pl.BlockSpec((tm, tn), lambda i, j, k: (i, j))
    o_spec = pl.BlockSpec(o_block, o_index) if into is not None else tile
    in_specs = [a_spec, b_spec] + ([tile] if extra is not None else []) + ([ANY] if into is not None else [])
    n_in = len(in_specs)

    def body(*refs):
        a_ref, b_ref = refs[0], refs[1]
        e_ref = refs[2] if extra is not None else None
        o_ref, acc_ref = refs[n_in], refs[n_in + 1]
        kk = pl.program_id(2)

        @pl.when(kk == 0)
        def _():
            acc_ref[...] = jnp.zeros_like(acc_ref)

        av = a_ref[...]
        if a_fn is not None:
            av = a_fn(av)
        bv = b_ref[...]
        if bv.ndim == 3:
            bv = bv.reshape(-1, bv.shape[-1])
        acc_ref[...] += lax.dot_general(av.astype(BF16), bv.astype(BF16), dn, preferred_element_type=F32)

        @pl.when(kk == nk - 1)
        def _():
            r = acc_ref[...]
            if epi is not None:
                r = epi(r, e_ref[...])
            o_ref[...] = r.astype(o_ref.dtype)

    args = (a, b) + ((extra,) if extra is not None else ()) + ((into,) if into is not None else ())
    out_shape = jax.ShapeDtypeStruct(into.shape, into.dtype) if into is not None else jax.ShapeDtypeStruct((M, N), out_dtype)
    return pl.pallas_call(
        body, out_shape=out_shape, grid=(M // tm, N // tn, nk), in_specs=in_specs, out_specs=o_spec,
        scratch_shapes=[pltpu.VMEM((tm, tn), F32)], input_output_aliases={n_in - 1: 0} if into is not None else {},
        compiler_params=_cp(("parallel", "parallel", "arbitrary")), name=name)(*args)


def _relu2(z):
    r = jnp.maximum(z, 0.0)
    return r * r


def _rms_fwd(x, g, name, tr=256):
    R, n = x.shape
    tr = min(tr, R)

    def body(x_ref, g_ref, h_ref):
        xv = x_ref[...]
        rstd = lax.rsqrt(jnp.mean(xv * xv, axis=-1, keepdims=True) + EPS)
        h_ref[...] = (xv * rstd * g_ref[...]).astype(BF16)

    return pl.pallas_call(
        body, out_shape=jax.ShapeDtypeStruct((R, n), BF16), grid=(R // tr,),
        in_specs=[pl.BlockSpec((tr, n), lambda i: (i, 0)), pl.BlockSpec((1, n), lambda i: (0, 0))],
        out_specs=pl.BlockSpec((tr, n), lambda i: (i, 0)), compiler_params=_cp(("parallel",)), name=name)(x, g)


def _rms_bwd(dh, x, g, dres, name, tr=256):
    R, n = x.shape
    tr = min(tr, R)
    need_dx = dres is not None

    def body(*refs):
        if need_dx:
            dh_ref, x_ref, g_ref, r_ref, dx_ref, dg_ref = refs
        else:
            dh_ref, x_ref, g_ref, dg_ref = refs
        i = pl.program_id(0)
        xv = x_ref[...]
        dhv = dh_ref[...].astype(F32)
        rstd = lax.rsqrt(jnp.mean(xv * xv, axis=-1, keepdims=True) + EPS)
        xhat = xv * rstd

        @pl.when(i == 0)
        def _():
            dg_ref[...] = jnp.zeros_like(dg_ref)

        dg_ref[...] += jnp.sum(dhv * xhat, axis=0, keepdims=True)
        if need_dx:
            t = dhv * g_ref[...]
            dx_ref[...] = r_ref[...] + rstd * (t - xhat * jnp.mean(t * xhat, axis=-1, keepdims=True))

    row = pl.BlockSpec((tr, n), lambda i: (i, 0))
    vec = pl.BlockSpec((1, n), lambda i: (0, 0))
    if need_dx:
        return pl.pallas_call(
            body, out_shape=(jax.ShapeDtypeStruct((R, n), F32), jax.ShapeDtypeStruct((1, n), F32)), grid=(R // tr,),
            in_specs=[row, row, vec, row], out_specs=(row, vec), compiler_params=_cp(("arbitrary",)), name=name)(dh, x, g, dres)
    return pl.pallas_call(
        body, out_shape=jax.ShapeDtypeStruct((1, n), F32), grid=(R // tr,),
        in_specs=[row, row, vec], out_specs=vec, compiler_params=_cp(("arbitrary",)), name=name)(dh, x, g)


def _loss_head(x, g, tgt, name, tr=256):
    R, n = x.shape

    def body(x_ref, g_ref, t_ref, loss_ref, dx_ref, dg_ref):
        i = pl.program_id(0)
        xv = x_ref[...]
        gv = g_ref[...]
        rstd = lax.rsqrt(jnp.mean(xv * xv, axis=-1, keepdims=True) + EPS)
        xhat = xv * rstd
        e = xhat * gv - t_ref[...]

        @pl.when(i == 0)
        def _():
            loss_ref[...] = jnp.zeros_like(loss_ref)
            dg_ref[...] = jnp.zeros_like(dg_ref)

        loss_ref[...] += 0.5 * jnp.sum(jnp.sum(e * e, axis=-1, keepdims=True) / n, axis=0, keepdims=True)
        dy = e / n
        dg_ref[...] += jnp.sum(dy * xhat, axis=0, keepdims=True)
        t = dy * gv
        dx_ref[...] = rstd * (t - xhat * jnp.mean(t * xhat, axis=-1, keepdims=True))

    row = pl.BlockSpec((tr, n), lambda i: (i, 0))
    vec = pl.BlockSpec((1, n), lambda i: (0, 0))
    one = pl.BlockSpec((1, 1), lambda i: (0, 0))
    return pl.pallas_call(
        body, out_shape=(jax.ShapeDtypeStruct((1, 1), F32), jax.ShapeDtypeStruct((R, n), F32), jax.ShapeDtypeStruct((1, n), F32)),
        grid=(R // tr,), in_specs=[row, vec, row], out_specs=(one, row, vec),
        compiler_params=_cp(("arbitrary",)), name=name)(x, g, tgt)


def _pool_masks(S):
    row = lax.broadcasted_iota(jnp.int32, (S, POOL_W), 0)
    grp = lax.broadcasted_iota(jnp.int32, (S, POOL_W), 1) // 64
    win = jnp.where(grp == 0, 2, jnp.where(grp == 1, 4, jnp.where(grp == 2, 8, 16)))
    cnt = jnp.minimum(row + 1, win).astype(F32)
    return row, grp, cnt


def _by_group(grp, v0, v1, v2, v3):
    return jnp.where(grp == 0, v0, jnp.where(grp == 1, v1, jnp.where(grp == 2, v2, v3)))


def _pool_fwd(proj, bd, scale, name):
    S = proj.shape[0]

    def body(a_ref, bd_ref, sc_ref, d_ref, y_ref):
        a = a_ref[...]
        row, grp, cnt = _pool_masks(S)

        def back(v, k):
            return jnp.where(row >= k, pltpu.roll(v, k, 0), 0.0)

        s1 = a + back(a, 1)
        s2 = s1 + back(s1, 2)
        s3 = s2 + back(s2, 4)
        s4 = s3 + back(s3, 8)
        d = (_by_group(grp, s1, s2, s3, s4) / cnt - a).astype(BF16)
        d_ref[...] = d
        y_ref[...] = (jnp.dot(d, bd_ref[...], preferred_element_type=F32) * sc_ref[...]).astype(BF16)

    full = lambda r, c: pl.BlockSpec((r, c), lambda i: (0, 0))
    return pl.pallas_call(
        body, out_shape=(jax.ShapeDtypeStruct((S, POOL_W), BF16), jax.ShapeDtypeStruct((S, POOL_W), BF16)), grid=(1,),
        in_specs=[pl.BlockSpec((S, POOL_W), lambda i: (0, P_A // POOL_W)), full(POOL_W, POOL_W), full(1, POOL_W)],
        out_specs=(full(S, POOL_W), full(S, POOL_W)), compiler_params=_cp(("arbitrary",)), name=name)(proj, bd, scale)


def _pool_bwd(dya, d, bd, scale, name):
    S = dya.shape[0]

    def body(dy_ref, d_ref, bd_ref, sc_ref, da_ref, dbd_ref, dsc_ref):
        dy = dy_ref[...]
        dv = d_ref[...]
        bdv = bd_ref[...]
        row, grp, cnt = _pool_masks(S)
        yraw = jnp.dot(dv, bdv, preferred_element_type=F32)
        dsc_ref[...] = jnp.sum(dy * yraw, axis=0, keepdims=True)
        tb = (dy * sc_ref[...]).astype(BF16)
        dbd_ref[...] = lax.dot_general(dv, tb, (((0,), (0,)), ((), ())), preferred_element_type=F32)
        dd = lax.dot_general(tb, bdv, (((1,), (1,)), ((), ())), preferred_element_type=F32)
        e = dd / cnt

        def fwd(v, k):
            return jnp.where(row < S - k, pltpu.roll(v, S - k, 0), 0.0)

        r1 = e + fwd(e, 1)
        r2 = r1 + fwd(r1, 2)
        r3 = r2 + fwd(r2, 4)
        r4 = r3 + fwd(r3, 8)
        da_ref[...] = (_by_group(grp, r1, r2, r3, r4) - dd).astype(BF16)

    full = lambda r, c: pl.BlockSpec((r, c), lambda i: (0, 0))
    return pl.pallas_call(
        body, out_shape=(jax.ShapeDtypeStruct((S, POOL_W), BF16), jax.ShapeDtypeStruct((POOL_W, POOL_W), F32),
                         jax.ShapeDtypeStruct((1, POOL_W), F32)), grid=(1,),
        in_specs=[full(S, POOL_W), full(S, POOL_W), full(POOL_W, POOL_W), full(1, POOL_W)],
        out_specs=(full(S, POOL_W), full(POOL_W, POOL_W), full(1, POOL_W)),
        compiler_params=_cp(("arbitrary",)), name=name)(dya, d, bd, scale)


FCOLS = 128


def _log_sigmoid(z):
    return -(jnp.maximum(-z, 0.0) + jnp.log1p(jnp.exp(-jnp.abs(z))))


def _fgate_fwd(proj, bf, name):
    S = proj.shape[0]

    def body(f_ref, b_ref, o_ref):
        v = _log_sigmoid(f_ref[...] + b_ref[...])
        row = lax.broadcasted_iota(jnp.int32, (S, FCOLS), 0)
        k = 1
        while k < S:
            v = v + jnp.where(row >= k, pltpu.roll(v, k, 0), 0.0)
            k *= 2
        o_ref[...] = v

    return pl.pallas_call(
        body, out_shape=jax.ShapeDtypeStruct((S, FCOLS), F32), grid=(1,),
        in_specs=[pl.BlockSpec((S, FCOLS), lambda i: (0, P_F // FCOLS)), pl.BlockSpec((1, FCOLS), lambda i: (0, 0))],
        out_specs=pl.BlockSpec((S, FCOLS), lambda i: (0, 0)), compiler_params=_cp(("arbitrary",)), name=name)(proj, bf)


def _fgate_bwd(dF, proj, bf, name):
    S = proj.shape[0]

    def body(dF_ref, f_ref, b_ref, df_ref, db_ref):
        v = dF_ref[...]
        row = lax.broadcasted_iota(jnp.int32, (S, FCOLS), 0)
        k = 1
        while k < S:
            v = v + jnp.where(row < S - k, pltpu.roll(v, S - k, 0), 0.0)
            k *= 2
        z = f_ref[...] + b_ref[...]
        df = v * (1.0 / (1.0 + jnp.exp(z)))
        db_ref[...] = jnp.sum(df, axis=0, keepdims=True)
        df_ref[...] = jnp.concatenate([df, jnp.zeros_like(df)], axis=1).astype(BF16)

    return pl.pallas_call(
        body, out_shape=(jax.ShapeDtypeStruct((S, 2 * FCOLS), BF16), jax.ShapeDtypeStruct((1, FCOLS), F32)), grid=(1,),
        in_specs=[pl.BlockSpec((S, FCOLS), lambda i: (0, 0)), pl.BlockSpec((S, FCOLS), lambda i: (0, P_F // FCOLS)),
                  pl.BlockSpec((1, FCOLS), lambda i: (0, 0))],
        out_specs=(pl.BlockSpec((S, 2 * FCOLS), lambda i: (0, 0)), pl.BlockSpec((1, FCOLS), lambda i: (0, 0))),
        compiler_params=_cp(("arbitrary",)), name=name)(dF, proj, bf)


def _fox_scores(qe, kj, fq, fk, r0, c0, tq, tk, diagonal):
    s = lax.dot_general(qe, kj, (((1,), (1,)), ((), ())), preferred_element_type=F32) * FOX_SCALE
    s = s + (fq - fk)
    if not diagonal:
        return s
    rows = r0 + lax.broadcasted_iota(jnp.int32, (tq, tk), 0)
    cols = c0 + lax.broadcasted_iota(jnp.int32, (tq, tk), 1)
    return jnp.where(rows >= cols, s, NEG)


def _fox_fwd(qkv, fcol, frow, name, tq=256):
    S = qkv.shape[0]
    tk = tq

    def body(q_ref, k_ref, v_ref, fc_ref, fr_ref, o_ref, o32_ref, lse_ref):
        i = pl.program_id(1)
        r0 = i * tq
        q = q_ref[...]
        half = lax.broadcasted_iota(jnp.int32, (tq, 128), 1) // 64
        qs = [jnp.where(half == e, q, jnp.zeros_like(q)) for e in (0, 1)]
        fqs = [fc_ref[0, :, e:e + 1] for e in (0, 1)]

        def step(j, carry, diagonal=False):
            c0 = pl.multiple_of(j * tk, tk)
            kj = k_ref[pl.ds(c0, tk), :]
            vj = v_ref[pl.ds(c0, tk), :]
            out = []
            for e in (0, 1):
                m, l, acc = carry[e]
                s = _fox_scores(qs[e], kj, fqs[e], fr_ref[0, e:e + 1, pl.ds(c0, tk)], r0, c0, tq, tk, diagonal)
                m_new = jnp.maximum(m, jnp.max(s, axis=-1, keepdims=True))
                alpha = jnp.exp(m - m_new)
                p = jnp.exp(s - m_new)
                out.append((m_new, alpha * l + jnp.sum(p, axis=-1, keepdims=True),
                            alpha * acc + jnp.dot(p.astype(BF16), vj, preferred_element_type=F32)))
            return tuple(out)

        init = (jnp.full((tq, 1), NEG, F32), jnp.zeros((tq, 1), F32), jnp.zeros((tq, 128), F32))
        carry = lax.fori_loop(0, i, step, (init, init))
        carry = step(i, carry, diagonal=True)
        outs = []
        for e in (0, 1):
            m, l, acc = carry[e]
            outs.append(acc / l)
            lse_ref[0, :, e:e + 1] = m + jnp.log(l)
        o = jnp.where(half == 0, outs[0], outs[1])
        o32_ref[...] = o
        o_ref[...] = o.astype(BF16)

    tile = pl.BlockSpec((tq, 128), lambda h, i: (i, h))
    return pl.pallas_call(
        body, out_shape=(jax.ShapeDtypeStruct((S, FOX_W), BF16), jax.ShapeDtypeStruct((S, FOX_W), F32),
                         jax.ShapeDtypeStruct((4, S, 2), F32)), grid=(4, S // tq),
        in_specs=[tile, pl.BlockSpec((S, 128), lambda h, i: (0, 4 + h)), pl.BlockSpec((S, 128), lambda h, i: (0, 8 + h)),
                  pl.BlockSpec((1, tq, 2), lambda h, i: (h, i, 0)), pl.BlockSpec((1, 2, S), lambda h, i: (h, 0, 0))],
        out_specs=(tile, tile, pl.BlockSpec((1, tq, 2), lambda h, i: (h, i, 0))),
        compiler_params=_cp(("parallel", "parallel")), name=name)(qkv, qkv, qkv, fcol, frow)


def _fox_bwd(qkv, o32, do, lse, fcol, frow, name, tq=256):
    S = qkv.shape[0]
    tk = tq
    nq = S // tq

    def body(q_ref, k_ref, v_ref, o_ref, do_ref, lse_ref, fc_ref, fr_ref, dq_ref, dk_ref, dv_ref, dfr_ref, dfc_ref, dk_acc, dv_acc):
        dk_acc[...] = jnp.zeros_like(dk_acc)
        dv_acc[...] = jnp.zeros_like(dv_acc)
        dfr_ref[...] = jnp.zeros_like(dfr_ref)
        half = lax.broadcasted_iota(jnp.int32, (tq, 128), 1) // 64

        def q_block(i, _):
            r0 = pl.multiple_of(i * tq, tq)
            qi = q_ref[pl.ds(r0, tq), :]
            dob = do_ref[pl.ds(r0, tq), :].astype(BF16)
            row_dot = dob.astype(F32) * o_ref[pl.ds(r0, tq), :]
            qs = [jnp.where(half == e, qi, jnp.zeros_like(qi)) for e in (0, 1)]
            dos = [jnp.where(half == e, dob, jnp.zeros_like(dob)) for e in (0, 1)]
            deltas = [jnp.sum(jnp.where(half == e, row_dot, 0.0), axis=-1, keepdims=True) for e in (0, 1)]
            lses = [lse_ref[0, pl.ds(r0, tq), e:e + 1] for e in (0, 1)]
            fqs = [fc_ref[0, pl.ds(r0, tq), e:e + 1] for e in (0, 1)]

            def step(j, carry, diagonal=False):
                dqs, row_sums = carry
                c0 = pl.multiple_of(j * tk, tk)
                kj = k_ref[pl.ds(c0, tk), :]
                vj = v_ref[pl.ds(c0, tk), :]
                new_dq, new_rows, dkc, dvc = [], [], [], []
                for e in (0, 1):
                    s = _fox_scores(qs[e], kj, fqs[e], fr_ref[0, e:e + 1, pl.ds(c0, tk)], r0, c0, tq, tk, diagonal)
                    p = jnp.exp(s - lses[e])
                    dp = lax.dot_general(dos[e], vj, (((1,), (1,)), ((), ())), preferred_element_type=F32)
                    ds = p * (dp - deltas[e])
                    dfr_ref[0, e:e + 1, pl.ds(c0, tk)] -= jnp.sum(ds, axis=0, keepdims=True)
                    new_rows.append(row_sums[e] + jnp.sum(ds, axis=-1, keepdims=True))
                    dsb = (ds * FOX_SCALE).astype(BF16)
                    dkc.append(lax.dot_general(dsb, qi, (((0,), (0,)), ((), ())), preferred_element_type=F32))
                    dvc.append(lax.dot_general(p.astype(BF16), dob, (((0,), (0,)), ((), ())), preferred_element_type=F32))
                    new_dq.append(dqs[e] + jnp.dot(dsb, kj, preferred_element_type=F32))
                dk_acc[pl.ds(c0, tk), :] += jnp.where(half == 0, dkc[0], dkc[1])
                dv_acc[pl.ds(c0, tk), :] += jnp.where(half == 0, dvc[0], dvc[1])
                return tuple(new_dq), tuple(new_rows)

            zero, zero_col = jnp.zeros((tq, 128), F32), jnp.zeros((tq, 1), F32)
            carry = lax.fori_loop(0, i, step, ((zero, zero), (zero_col, zero_col)))
            dqs, row_sums = step(i, carry, diagonal=True)
            for e in (0, 1):
                dfc_ref[0, pl.ds(r0, tq), e:e + 1] = row_sums[e]
            dq_ref[pl.ds(r0, tq), :] = jnp.where(half == 0, dqs[0], dqs[1]).astype(BF16)
            return 0

        lax.fori_loop(0, nq, q_block, 0)
        dk_ref[...] = dk_acc[...].astype(BF16)
        dv_ref[...] = dv_acc[...].astype(BF16)

    col = lambda off: pl.BlockSpec((S, 128), lambda h: (0, off + h))
    hs2 = pl.BlockSpec((1, S, 2), lambda h: (h, 0, 0))
    h2s = pl.BlockSpec((1, 2, S), lambda h: (h, 0, 0))
    return pl.pallas_call(
        body, out_shape=(jax.ShapeDtypeStruct((S, FOX_W), BF16),) * 3 + (jax.ShapeDtypeStruct((4, 2, S), F32),
                                                                         jax.ShapeDtypeStruct((4, S, 2), F32)), grid=(4,),
        in_specs=[col(0), col(4), col(8), col(0), col(0), hs2, hs2, h2s],
        out_specs=(col(0), col(0), col(0), h2s, hs2),
        scratch_shapes=[pltpu.VMEM((S, 128), F32), pltpu.VMEM((S, 128), F32)],
        compiler_params=_cp(("parallel",)), name=name)(qkv, qkv, qkv, o32, do, lse, fcol, frow)


def _gelu(x):
    return 0.5 * x * (1.0 + jnp.tanh(GELU_K * (x + GELU_C * x * x * x)))


def _gelu_grad(x):
    th = jnp.tanh(GELU_K * (x + GELU_C * x * x * x))
    return 0.5 * (1.0 + th) + 0.5 * x * (1.0 - th * th) * GELU_K * (1.0 + 3.0 * GELU_C * x * x)


def _sgu_parts(c, gn, w_ref, bias):
    zc = _gelu(c)
    u, vv = zc[:, :SGU_W], zc[:, SGU_W:]
    rstd = lax.rsqrt(jnp.mean(vv * vv, axis=-1, keepdims=True) + EPS)
    vhat = vv * rstd
    vnb = (vhat * gn).astype(BF16)
    grp = lax.broadcasted_iota(jnp.int32, (SGU_CHUNK, SGU_W), 1) // 64
    mixed = bias
    for gi in range(4):
        mixed = mixed + jnp.where(grp == gi, jnp.dot(w_ref[gi], vnb, preferred_element_type=F32), 0.0)
    return u, rstd, vhat, vnb, grp, mixed


def _sgu_fwd(proj, gn, wm, bias, name):
    S = proj.shape[0]

    def body(c_ref, g_ref, w_ref, b_ref, o_ref):
        u, _, _, _, _, mixed = _sgu_parts(c_ref[...], g_ref[...], w_ref, b_ref[...])
        o_ref[...] = (u * mixed).astype(BF16)

    return pl.pallas_call(
        body, out_shape=jax.ShapeDtypeStruct((S, SGU_W), BF16), grid=(S // SGU_CHUNK,),
        in_specs=[pl.BlockSpec((SGU_CHUNK, 2 * SGU_W), lambda i: (i, P_C // (2 * SGU_W))),
                  pl.BlockSpec((1, SGU_W), lambda i: (0, 0)), pl.BlockSpec((4, SGU_CHUNK, SGU_CHUNK), lambda i: (0, 0, 0)),
                  pl.BlockSpec((SGU_CHUNK, SGU_W), lambda i: (0, 0))],
        out_specs=pl.BlockSpec((SGU_CHUNK, SGU_W), lambda i: (i, 0)),
        compiler_params=_cp(("parallel",)), name=name)(proj, gn, wm, bias)


def _sgu_bwd(dsg, proj, gn, wm, wmt, bias, name):
    S = proj.shape[0]

    def body(dsg_ref, c_ref, g_ref, w_ref, wt_ref, b_ref, dc_ref, dw_ref, db_ref, dg_ref):
        i = pl.program_id(0)

        @pl.when(i == 0)
        def _():
            dw_ref[...] = jnp.zeros_like(dw_ref)
            db_ref[...] = jnp.zeros_like(db_ref)
            dg_ref[...] = jnp.zeros_like(dg_ref)

        c = c_ref[...]
        gn_v = g_ref[...]
        u, rstd, vhat, vnb, grp, mixed = _sgu_parts(c, gn_v, w_ref, b_ref[...])
        dsg_v = dsg_ref[...]
        du = dsg_v * mixed
        dmix = dsg_v * u
        db_ref[...] += dmix
        dmb = dmix.astype(BF16)
        dvn = jnp.zeros((SGU_CHUNK, SGU_W), F32)
        for gi in range(4):
            dmg = jnp.where(grp == gi, dmb, jnp.zeros_like(dmb))
            dw_ref[gi] += lax.dot_general(dmg, vnb, (((1,), (1,)), ((), ())), preferred_element_type=F32)
            dvn = dvn + jnp.where(grp == gi, jnp.dot(wt_ref[gi], dmb, preferred_element_type=F32), 0.0)
        dg_ref[...] += jnp.sum(dvn * vhat, axis=0, keepdims=True)
        t = dvn * gn_v
        dvv = rstd * (t - vhat * jnp.mean(t * vhat, axis=-1, keepdims=True))
        dc_ref[...] = (jnp.concatenate([du, dvv], axis=1) * _gelu_grad(c)).astype(BF16)

    w_spec = pl.BlockSpec((4, SGU_CHUNK, SGU_CHUNK), lambda i: (0, 0, 0))
    tile = pl.BlockSpec((SGU_CHUNK, SGU_W), lambda i: (0, 0))
    vec = pl.BlockSpec((1, SGU_W), lambda i: (0, 0))
    return pl.pallas_call(
        body, out_shape=(jax.ShapeDtypeStruct((S, 2 * SGU_W), BF16), jax.ShapeDtypeStruct((4, SGU_CHUNK, SGU_CHUNK), F32),
                         jax.ShapeDtypeStruct((SGU_CHUNK, SGU_W), F32), jax.ShapeDtypeStruct((1, SGU_W), F32)),
        grid=(S // SGU_CHUNK,),
        in_specs=[pl.BlockSpec((SGU_CHUNK, SGU_W), lambda i: (i, 0)),
                  pl.BlockSpec((SGU_CHUNK, 2 * SGU_W), lambda i: (i, P_C // (2 * SGU_W))), vec, w_spec, w_spec, tile],
        out_specs=(pl.BlockSpec((SGU_CHUNK, 2 * SGU_W), lambda i: (i, 0)), w_spec, tile, vec),
        compiler_params=_cp(("arbitrary",)), name=name)(dsg, proj, gn, wm, wmt, bias)


def _sigmoid(z):
    return 1.0 / (1.0 + jnp.exp(-z))


def _merge_specs(tm):
    row = lambda n: pl.BlockSpec((tm, n), lambda i: (i, 0))
    gate = lambda b: pl.BlockSpec((tm, D), lambda i: (i, b))
    full = lambda r, c: pl.BlockSpec((r, c), lambda i: (0, 0))
    packed = pl.BlockSpec((4, 256, PACK_COLS), lambda i: (0, R_BRANCH // 256, 0))
    return row, gate, full, packed


def _branch_shards(c_ref, j):
    return c_ref[j, :, 0:256], c_ref[j, :, 256:512], c_ref[j, :, 512:768], c_ref[j, :, 768:1024]


def _merge_fwd(proj, ya, o, sg, packed_w, bg, name, tm=256):
    S = proj.shape[0]
    row, gate, full, packed = _merge_specs(tm)

    def body(g0, g1, g2, ya_ref, o_ref, sg_ref, c_ref, bg_ref, out_ref):
        yav, ov, sgv = ya_ref[...], o_ref[...], sg_ref[...]
        for j in range(4):
            cols = slice(256 * j, 256 * (j + 1))
            wa, wb0, wb1, wc = _branch_shards(c_ref, j)
            y = (jnp.dot(yav, wa, preferred_element_type=F32),
                 jnp.dot(ov[:, :256], wb0, preferred_element_type=F32) + jnp.dot(ov[:, 256:], wb1, preferred_element_type=F32),
                 jnp.dot(sgv, wc, preferred_element_type=F32))
            acc = jnp.zeros((tm, 256), F32)
            for b, g_ref in enumerate((g0, g1, g2)):
                acc = acc + _sigmoid(g_ref[:, cols] + bg_ref[:, b * D + 256 * j:b * D + 256 * (j + 1)]) * y[b]
            out_ref[:, cols] = acc.astype(BF16)

    return pl.pallas_call(
        body, out_shape=jax.ShapeDtypeStruct((S, D), BF16), grid=(S // tm,),
        in_specs=[gate(0), gate(1), gate(2), row(POOL_W), row(FOX_W), row(SGU_W), packed, full(1, 3 * D)],
        out_specs=row(D), compiler_params=_cp(("parallel",)), name=name)(proj, proj, proj, ya, o, sg, packed_w, bg)


def _merge_bwd(dm, proj, ya, o, sg, packed_w, bg, grads, name, tm=256):
    S = proj.shape[0]
    row, gate, full, packed = _merge_specs(tm)
    tn_dims = (((0,), (0,)), ((), ()))
    nt_dims = (((1,), (1,)), ((), ()))

    def body(dm_ref, g0, g1, g2, ya_ref, o_ref, sg_ref, c_ref, bg_ref, _, dg_ref, dya_ref, do_ref, dsg_ref, dc_ref, dbg_ref, acc):
        i = pl.program_id(0)

        @pl.when(i == 0)
        def _():
            acc[...] = jnp.zeros_like(acc)
            dbg_ref[...] = jnp.zeros_like(dbg_ref)

        yav, ov, sgv = ya_ref[...], o_ref[...], sg_ref[...]
        o0, o1 = ov[:, :256], ov[:, 256:]
        dya = jnp.zeros((tm, POOL_W), F32)
        do0 = jnp.zeros((tm, 256), F32)
        do1 = jnp.zeros((tm, 256), F32)
        dsg = jnp.zeros((tm, SGU_W), F32)
        for j in range(4):
            cols = slice(256 * j, 256 * (j + 1))
            wa, wb0, wb1, wc = _branch_shards(c_ref, j)
            y = (jnp.dot(yav, wa, preferred_element_type=F32),
                 jnp.dot(o0, wb0, preferred_element_type=F32) + jnp.dot(o1, wb1, preferred_element_type=F32),
                 jnp.dot(sgv, wc, preferred_element_type=F32))
            dmv = dm_ref[:, cols]
            dy = []
            for b, g_ref in enumerate((g0, g1, g2)):
                bcols = slice(b * D + 256 * j, b * D + 256 * (j + 1))
                gt = _sigmoid(g_ref[:, cols] + bg_ref[:, bcols])
                dgp = dmv * y[b] * gt * (1.0 - gt)
                dg_ref[:, bcols] = dgp.astype(BF16)
                dbg_ref[:, bcols] += jnp.sum(dgp, axis=0, keepdims=True)
                dy.append((dmv * gt).astype(BF16))
            dya = dya + lax.dot_general(dy[0], wa, nt_dims, preferred_element_type=F32)
            do0 = do0 + lax.dot_general(dy[1], wb0, nt_dims, preferred_element_type=F32)
            do1 = do1 + lax.dot_general(dy[1], wb1, nt_dims, preferred_element_type=F32)
            dsg = dsg + lax.dot_general(dy[2], wc, nt_dims, preferred_element_type=F32)
            acc[j, :, 0:256] += lax.dot_general(yav, dy[0], tn_dims, preferred_element_type=F32)
            acc[j, :, 256:512] += lax.dot_general(o0, dy[1], tn_dims, preferred_element_type=F32)
            acc[j, :, 512:768] += lax.dot_general(o1, dy[1], tn_dims, preferred_element_type=F32)
            acc[j, :, 768:1024] += lax.dot_general(sgv, dy[2], tn_dims, preferred_element_type=F32)
        dya_ref[...] = dya
        do_ref[:, :256] = do0
        do_ref[:, 256:] = do1
        dsg_ref[...] = dsg

        @pl.when(i == pl.num_programs(0) - 1)
        def _():
            dc_ref[...] = acc[...].astype(dc_ref.dtype)

    return pl.pallas_call(
        body, out_shape=(jax.ShapeDtypeStruct((S, 3 * D), BF16), jax.ShapeDtypeStruct((S, POOL_W), F32),
                         jax.ShapeDtypeStruct((S, FOX_W), F32), jax.ShapeDtypeStruct((S, SGU_W), F32),
                         jax.ShapeDtypeStruct(grads.shape, grads.dtype), jax.ShapeDtypeStruct((1, 3 * D), F32)),
        grid=(S // tm,),
        in_specs=[row(D), gate(0), gate(1), gate(2), row(POOL_W), row(FOX_W), row(SGU_W), packed, full(1, 3 * D), ANY],
        out_specs=(row(3 * D), row(POOL_W), row(FOX_W), row(SGU_W), packed, full(1, 3 * D)),
        scratch_shapes=[pltpu.VMEM((4, 256, PACK_COLS), F32)], input_output_aliases={9: 4},
        compiler_params=_cp(("arbitrary",)), name=name)(dm, proj, proj, proj, ya, o, sg, packed_w, bg, grads)


def _xattn_probs(qh, kh):
    s = lax.dot_general(qh, kh, (((1,), (1,)), ((), ())), preferred_element_type=F32) * X_SCALE
    p = jnp.exp(s - jnp.max(s, axis=-1, keepdims=True))
    return p / jnp.sum(p, axis=-1, keepdims=True)


def _xattn_fwd(xq, kv, name, tq=256):
    S = xq.shape[0]
    M = kv.shape[0]

    def body(q_ref, k_ref, v_ref, o_ref):
        for h in range(XH):
            sl = slice(h * XHD, (h + 1) * XHD)
            p = _xattn_probs(q_ref[:, sl], k_ref[:, sl])
            o_ref[:, sl] = jnp.dot(p.astype(BF16), v_ref[:, sl], preferred_element_type=F32).astype(BF16)

    return pl.pallas_call(
        body, out_shape=jax.ShapeDtypeStruct((S, D), BF16), grid=(S // tq,),
        in_specs=[pl.BlockSpec((tq, D), lambda i: (i, 0)), pl.BlockSpec((M, D), lambda i: (0, 0)),
                  pl.BlockSpec((M, D), lambda i: (0, 1))],
        out_specs=pl.BlockSpec((tq, D), lambda i: (i, 0)), compiler_params=_cp(("parallel",)), name=name)(xq, kv, kv)


def _xattn_bwd(xq, kv, do, name, tq=256):
    S = xq.shape[0]
    M = kv.shape[0]

    def body(q_ref, k_ref, v_ref, do_ref, dq_ref, dkv_ref, dk_acc, dv_acc):
        i = pl.program_id(0)

        @pl.when(i == 0)
        def _():
            dk_acc[...] = jnp.zeros_like(dk_acc)
            dv_acc[...] = jnp.zeros_like(dv_acc)

        for h in range(XH):
            sl = slice(h * XHD, (h + 1) * XHD)
            qh, kh, vh, doh = q_ref[:, sl], k_ref[:, sl], v_ref[:, sl], do_ref[:, sl]
            p = _xattn_probs(qh, kh)
            dp = lax.dot_general(doh, vh, (((1,), (1,)), ((), ())), preferred_element_type=F32)
            ds = p * (dp - jnp.sum(p * dp, axis=-1, keepdims=True))
            dsb = (ds * X_SCALE).astype(BF16)
            dq_ref[:, sl] = jnp.dot(dsb, kh, preferred_element_type=F32).astype(BF16)
            dk_acc[:, sl] += lax.dot_general(dsb, qh, (((0,), (0,)), ((), ())), preferred_element_type=F32)
            dv_acc[:, sl] += lax.dot_general(p.astype(BF16), doh, (((0,), (0,)), ((), ())), preferred_element_type=F32)

        @pl.when(i == pl.num_programs(0) - 1)
        def _():
            dkv_ref[:, :D] = dk_acc[...].astype(BF16)
            dkv_ref[:, D:] = dv_acc[...].astype(BF16)

    return pl.pallas_call(
        body, out_shape=(jax.ShapeDtypeStruct((S, D), BF16), jax.ShapeDtypeStruct((M, 2 * D), BF16)), grid=(S // tq,),
        in_specs=[pl.BlockSpec((tq, D), lambda i: (i, 0)), pl.BlockSpec((M, D), lambda i: (0, 0)),
                  pl.BlockSpec((M, D), lambda i: (0, 1)), pl.BlockSpec((tq, D), lambda i: (i, 0))],
        out_specs=(pl.BlockSpec((tq, D), lambda i: (i, 0)), pl.BlockSpec((M, 2 * D), lambda i: (0, 0))),
        scratch_shapes=[pltpu.VMEM((M, D), F32), pltpu.VMEM((M, D), F32)],
        compiler_params=_cp(("arbitrary",)), name=name)(xq, kv, kv, do)


def _adam_math(gv, wv, mv, vv):
    c1 = 1.0 - ADAM_B1 ** ADAM_STEP
    c2 = 1.0 - ADAM_B2 ** ADAM_STEP
    nm = ADAM_B1 * mv + (1.0 - ADAM_B1) * gv
    nv = ADAM_B2 * vv + (1.0 - ADAM_B2) * (gv * gv)
    return -ADAM_LR * ((nm / c1) / (jnp.sqrt(nv / c2) + ADAM_EPS) + ADAM_WD * wv), nm, nv


def _adamw(g, w, m, v, name, block=None):
    if block is None:
        block = (1, 256 if g.shape[1] % 256 == 0 else g.shape[1], g.shape[2])
    grid = tuple(s // b for s, b in zip(g.shape, block))

    def body(g_ref, w_ref, m_ref, v_ref, d_ref, nm_ref, nv_ref):
        d_ref[...], nm_ref[...], nv_ref[...] = _adam_math(g_ref[...], w_ref[...], m_ref[...], v_ref[...])

    blk = pl.BlockSpec(block, lambda a, b, c: (a, b, c))
    return pl.pallas_call(
        body, out_shape=(jax.ShapeDtypeStruct(g.shape, F32),) * 3, grid=grid,
        in_specs=[blk] * 4, out_specs=(blk,) * 3, compiler_params=_cp(("parallel",) * 3), name=name)(g, w, m, v)


def _adamw_packed(red, w, m, v, g_index, name, tr=256):
    L, r, c = w.shape
    tr = min(tr, r)

    def body(g0_ref, g1_ref, w_ref, m_ref, v_ref, g_ref, d_ref, nm_ref, nv_ref):
        gv = jnp.where(pl.program_id(0) == 0, g0_ref[...], g1_ref[...])
        g_ref[0] = gv
        d_ref[0], nm_ref[0], nv_ref[0] = _adam_math(gv, w_ref[0], m_ref[0], v_ref[0])

    gblk = pl.BlockSpec((tr, c), lambda l, i: g_index(i))
    blk = pl.BlockSpec((1, tr, c), lambda l, i: (l, i, 0))
    return pl.pallas_call(
        body, out_shape=(jax.ShapeDtypeStruct(w.shape, F32),) * 4, grid=(L, r // tr),
        in_specs=[gblk, gblk, blk, blk, blk], out_specs=(blk,) * 4,
        compiler_params=_cp(("parallel", "parallel")), name=name)(red[0], red[1], w, m, v)


def _sum_slots(a, out_dtype, name, tr=496):
    n, R, C = a.shape
    tr = tr if R % tr == 0 else R

    def body(a_ref, o_ref):
        acc = a_ref[0].astype(F32)
        for k in range(1, n):
            acc = acc + a_ref[k].astype(F32)
        o_ref[...] = acc.astype(out_dtype)

    return pl.pallas_call(
        body, out_shape=jax.ShapeDtypeStruct((R, C), out_dtype), grid=(R // tr,),
        in_specs=[pl.BlockSpec((n, tr, C), lambda i: (0, i, 0))], out_specs=pl.BlockSpec((tr, C), lambda i: (i, 0)),
        compiler_params=_cp(("parallel",)), name=name)(a)


def _add_pair(a, b, name, tr=496):
    n, R, C = a.shape
    tr = tr if R % tr == 0 else R

    def body(a_ref, b_ref, o_ref):
        o_ref[...] = (a_ref[...].astype(F32) + b_ref[...].astype(F32)).astype(BF16)

    blk = pl.BlockSpec((1, tr, C), lambda k, i: (k, i, 0))
    return pl.pallas_call(
        body, out_shape=jax.ShapeDtypeStruct(a.shape, BF16), grid=(n, R // tr), in_specs=[blk, blk], out_specs=blk,
        compiler_params=_cp(("parallel", "parallel")), name=name)(a, b)


LANDING = pl.BlockSpec(memory_space=pltpu.VMEM)


def _landing_params(shape, dtype):
    return pltpu.CompilerParams(vmem_limit_bytes=math.prod(shape) * jnp.dtype(dtype).itemsize + 4 * 1024 * 1024)


def _place():
    return lax.axis_index("x"), lax.axis_index("y"), lax.axis_index("c")


def _other_chips(x, y):
    return [(1 - x, y), (x, 1 - y), (1 - x, 1 - y)]


def _row_chunks(rows, want, align=16):
    n = want
    while n > 1 and rows % (n * align):
        n -= 1
    return n


def _gather_weights(shard, name, nch=5):
    R, C = shard.shape
    half = R // 2
    nch = _row_chunks(half, nch)
    cr = half // nch

    def body(s_ref, o_ref, send_sems, recv_sems, local_sem):
        x, y, c = _place()
        j = 2 * x + y
        mine0 = c * half
        theirs0 = (1 - c) * half

        def rows(jj, r0, q):
            return o_ref.at[jj, pl.ds(pl.multiple_of(r0 + q * cr, 16), cr), :]

        def copy(k, src, dst, to):
            return pltpu.make_async_remote_copy(src_ref=src, dst_ref=dst, send_sem=send_sems.at[k], recv_sem=recv_sems.at[k],
                                                device_id=to, device_id_type=MESH)

        own = pltpu.make_async_copy(s_ref, o_ref.at[j], local_sem)
        own.start()
        chips = _other_chips(x, y)
        first = []
        for q in range(nch):
            for k, (px, py) in enumerate(chips):
                src = s_ref.at[pl.ds(pl.multiple_of(mine0 + q * cr, 16), cr), :]
                first.append(copy(k * nch + q, src, rows(j, mine0, q), (px, py, c)))
        for cp in first:
            cp.start()
        passed = []
        for q in range(nch):
            for k, (px, py) in enumerate(chips):
                jj = 2 * px + py
                copy(k * nch + q, rows(jj, mine0, q), rows(jj, mine0, q), (px, py, c)).wait_recv()
                fw = copy((3 + k) * nch + q, rows(jj, mine0, q), rows(jj, mine0, q), (x, y, 1 - c))
                fw.start()
                passed.append(fw)
        for q in range(nch):
            for k, (px, py) in enumerate(chips):
                jj = 2 * px + py
                copy((3 + k) * nch + q, rows(jj, theirs0, q), rows(jj, theirs0, q), (x, y, 1 - c)).wait_recv()
        for cp in first + passed:
            cp.wait_send()
        own.wait()

    return pl.pallas_call(
        body, out_shape=jax.ShapeDtypeStruct((4, R, C), shard.dtype), in_specs=[ANY], out_specs=LANDING,
        scratch_shapes=[pltpu.SemaphoreType.DMA((6 * nch,)), pltpu.SemaphoreType.DMA((6 * nch,)), pltpu.SemaphoreType.DMA],
        compiler_params=_landing_params((4, R, C), shard.dtype), name=name)(shard)


def _pair_split(g, name, nch=5):
    n, R, C = g.shape
    half = R // 2
    nch = _row_chunks(half, nch)
    cr = half // nch

    def body(g_ref, own_ref, got_ref, send_sems, recv_sems, local_sem):
        x, y, c = _place()
        mine0 = pl.multiple_of(c * half, 16)
        theirs0 = (1 - c) * half
        keep = pltpu.make_async_copy(g_ref.at[:, pl.ds(mine0, half), :], own_ref, local_sem)
        keep.start()
        cps = []
        for s in range(n):
            for q in range(nch):
                src = g_ref.at[s, pl.ds(pl.multiple_of(theirs0 + q * cr, 16), cr), :]
                cps.append(pltpu.make_async_remote_copy(
                    src_ref=src, dst_ref=got_ref.at[s, pl.ds(q * cr, cr), :], send_sem=send_sems.at[s * nch + q],
                    recv_sem=recv_sems.at[s * nch + q], device_id=(x, y, 1 - c), device_id_type=MESH))
        for cp in cps:
            cp.start()
        for cp in cps:
            cp.wait()
        keep.wait()

    sh = jax.ShapeDtypeStruct((n, half, C), g.dtype)
    return pl.pallas_call(
        body, out_shape=(sh, sh), in_specs=[ANY], out_specs=(ANY, LANDING),
        scratch_shapes=[pltpu.SemaphoreType.DMA((n * nch,)), pltpu.SemaphoreType.DMA((n * nch,)), pltpu.SemaphoreType.DMA],
        compiler_params=_landing_params(sh.shape, g.dtype), name=name)(g)


def _chip_all_to_all(p, name, nch=5):
    R = p.shape[1]
    nch = _row_chunks(R, nch)
    cr = R // nch

    def body(p_ref, o_ref, send_sems, recv_sems, local_sem):
        x, y, c = _place()
        j = 2 * x + y
        own = pltpu.make_async_copy(p_ref.at[j], o_ref.at[j], local_sem)
        own.start()
        cps = []
        for q in range(nch):
            for k, (px, py) in enumerate(_other_chips(x, y)):
                cps.append(pltpu.make_async_remote_copy(
                    src_ref=p_ref.at[2 * px + py, pl.ds(q * cr, cr), :], dst_ref=o_ref.at[j, pl.ds(q * cr, cr), :],
                    send_sem=send_sems.at[k * nch + q], recv_sem=recv_sems.at[k * nch + q], device_id=(px, py, c),
                    device_id_type=MESH))
        for cp in cps:
            cp.start()
        for cp in cps:
            cp.wait()
        own.wait()

    return pl.pallas_call(
        body, out_shape=jax.ShapeDtypeStruct(p.shape, p.dtype), in_specs=[ANY], out_specs=LANDING,
        scratch_shapes=[pltpu.SemaphoreType.DMA((3 * nch,)), pltpu.SemaphoreType.DMA((3 * nch,)), pltpu.SemaphoreType.DMA],
        compiler_params=_landing_params(p.shape, p.dtype), name=name)(p)


def _pair_gather(t, name, nch=10):
    R = t.shape[0]
    nch = _row_chunks(R, nch, 8)
    cr = R // nch

    def body(t_ref, o_ref, send_sems, recv_sems, local_sem):
        x, y, c = _place()
        own = pltpu.make_async_copy(t_ref, o_ref.at[c], local_sem)
        own.start()
        cps = [pltpu.make_async_remote_copy(src_ref=t_ref.at[pl.ds(q * cr, cr), :], dst_ref=o_ref.at[c, pl.ds(q * cr, cr), :],
                                            send_sem=send_sems.at[q], recv_sem=recv_sems.at[q], device_id=(x, y, 1 - c),
                                            device_id_type=MESH) for q in range(nch)]
        for cp in cps:
            cp.start()
        for cp in cps:
            cp.wait()
        own.wait()

    return pl.pallas_call(
        body, out_shape=jax.ShapeDtypeStruct((2,) + t.shape, t.dtype), in_specs=[ANY], out_specs=LANDING,
        scratch_shapes=[pltpu.SemaphoreType.DMA((nch,)), pltpu.SemaphoreType.DMA((nch,)), pltpu.SemaphoreType.DMA],
        compiler_params=_landing_params((2,) + t.shape, t.dtype), name=name)(t)


HBM = pl.BlockSpec(memory_space=pltpu.HBM)
SEM = pl.BlockSpec(memory_space=pltpu.SEMAPHORE)
SPLIT_COPY = pltpu.CompilerParams(has_side_effects=pltpu.SideEffectType.DATAFLOW_SIDE_EFFECTING)


def _split_exchange(src, rows, src_of, tag, nch=5):
    C = src.shape[-1]
    nch = _row_chunks(rows, nch)
    cr = rows // nch
    n = 3 * nch
    land_shape = (4, rows, C)

    def copies(src_ref, land_ref, send_sems, recv_sems):
        x, y, c = _place()
        j = 2 * x + y
        out = []
        for q in range(nch):
            for k, (px, py) in enumerate(_other_chips(x, y)):
                out.append(pltpu.make_async_remote_copy(
                    src_ref=src_of(src_ref, px, py, c, q * cr, cr), dst_ref=land_ref.at[j, pl.ds(q * cr, cr), :],
                    send_sem=send_sems.at[k * nch + q], recv_sem=recv_sems.at[k * nch + q], device_id=(px, py, c),
                    device_id_type=MESH))
        return out

    def start(src_ref, land_ref, send_sems, recv_sems, src_thru, land_thru, token):
        for cp in copies(src_ref, land_ref, send_sems, recv_sems):
            cp.start()
        token[...] = jnp.zeros_like(token)

    send_sems, recv_sems, src_thru, land_thru, token = pl.pallas_call(
        start, name=f"{tag}_start",
        out_shape=(pltpu.SemaphoreType.DMA((n,)), pltpu.SemaphoreType.DMA((n,)), pltpu.HBM(src.shape, src.dtype),
                   pltpu.HBM(land_shape, src.dtype), jax.ShapeDtypeStruct((8, 128), F32)),
        in_specs=(HBM, HBM), out_specs=(SEM, SEM, HBM, HBM, pl.BlockSpec(memory_space=pltpu.VMEM)),
        input_output_aliases={0: 2, 1: 3}, compiler_params=SPLIT_COPY)(
            pltpu.with_memory_space_constraint(src, pltpu.HBM),
            pltpu.with_memory_space_constraint(lax.empty(land_shape, src.dtype), pltpu.HBM))

    def finish(after):
        def wait(src_ref, land_ref, send_sems, recv_sems, after_ref, src_dead, got_ref):
            for cp in copies(src_ref, land_ref, send_sems, recv_sems):
                cp.wait_send()
                cp.wait_recv()

        return pl.pallas_call(
            wait, name=f"{tag}_wait", out_shape=(pltpu.HBM(src.shape, src.dtype), pltpu.HBM(land_shape, src.dtype)),
            in_specs=(HBM, HBM, SEM, SEM, ANY), out_specs=(HBM, HBM), input_output_aliases={0: 0, 1: 1},
            compiler_params=SPLIT_COPY)(src_thru, land_thru, send_sems, recv_sems, after)[1]

    return token, finish


def _gather_finish(shard, land, name, nch=5):
    R, C = shard.shape
    half = R // 2
    nch = _row_chunks(half, nch)
    cr = half // nch

    def body(s_ref, l_ref, o_ref, send_sems, recv_sems, local_sems):
        x, y, c = _place()
        j = 2 * x + y
        mine0 = c * half
        local = [pltpu.make_async_copy(s_ref, o_ref.at[j], local_sems.at[0])]
        remote = []
        for k, (px, py) in enumerate(_other_chips(x, y)):
            jj = 2 * px + py
            local.append(pltpu.make_async_copy(l_ref.at[jj], o_ref.at[jj, pl.ds(pl.multiple_of(mine0, 16), half), :],
                                               local_sems.at[1 + k]))
            for q in range(nch):
                remote.append(pltpu.make_async_remote_copy(
                    src_ref=l_ref.at[jj, pl.ds(q * cr, cr), :],
                    dst_ref=o_ref.at[jj, pl.ds(pl.multiple_of(mine0 + q * cr, 16), cr), :], send_sem=send_sems.at[k * nch + q],
                    recv_sem=recv_sems.at[k * nch + q], device_id=(x, y, 1 - c), device_id_type=MESH))
        for cp in local + remote:
            cp.start()
        for cp in remote + local:
            cp.wait()

    return pl.pallas_call(
        body, out_shape=jax.ShapeDtypeStruct((4, R, C), shard.dtype), in_specs=[ANY, ANY], out_specs=LANDING,
        scratch_shapes=[pltpu.SemaphoreType.DMA((3 * nch,)), pltpu.SemaphoreType.DMA((3 * nch,)), pltpu.SemaphoreType.DMA((4,))],
        compiler_params=_landing_params((4, R, C), shard.dtype), name=name)(shard, land)


def _sum_slots_own(land, own, name, tr=496):
    n, R, C = land.shape
    tr = tr if R % tr == 0 else R
    me = (2 * lax.axis_index("x") + lax.axis_index("y")).astype(jnp.int32).reshape(1)

    def body(me_ref, land_ref, own_ref, o_ref):
        acc = None
        for k in range(n):
            v = jnp.where(me_ref[0] == k, own_ref[...], land_ref[k]).astype(F32)
            acc = v if acc is None else acc + v
        o_ref[...] = acc

    return pl.pallas_call(
        body, out_shape=jax.ShapeDtypeStruct((R, C), F32),
        grid_spec=pltpu.PrefetchScalarGridSpec(
            num_scalar_prefetch=1, grid=(R // tr,),
            in_specs=[pl.BlockSpec((n, tr, C), lambda i, me: (0, i, 0)), pl.BlockSpec((None, tr, C), lambda i, me: (me[0], i, 0))],
            out_specs=pl.BlockSpec((tr, C), lambda i, me: (i, 0))),
        compiler_params=_cp(("parallel",)), name=name)(me, land, own)


def _reduce_scatter(g, tag, overlap=None):
    own, got = _pair_split(g, f"rs_pair_{tag}")
    p = _add_pair(own, got, f"rs_add_{tag}")
    if overlap is None:
        t = _sum_slots(_chip_all_to_all(p, f"rs_a2a_{tag}"), F32, f"rs_sum_{tag}")
        result = None
    else:
        token, finish = _split_exchange(p, p.shape[1], lambda ref, px, py, c, r0, cr: ref.at[2 * px + py, pl.ds(r0, cr), :],
                                        f"rs_a2a_{tag}")
        result = overlap(token)
        t = _sum_slots_own(finish(jax.tree.leaves(result)[0]), p, f"rs_sum_{tag}")
    both = _pair_gather(t, f"rs_join_{tag}")
    red = both.reshape(g.shape[1], g.shape[2])
    return red if overlap is None else (red, result)


def _all_reduce_small(v, tag):
    pair = _pair_gather(v, f"ar_pair_{tag}")
    p = _sum_slots(pair, F32, f"ar_add_{tag}")
    q = _chip_all_to_all(jnp.broadcast_to(p[None], (4,) + p.shape), f"ar_a2a_{tag}")
    return _sum_slots(q, F32, f"ar_sum_{tag}")


R_FF1, R_FF2, R_XKV, R_BRANCH, R_OUT, R_XQ, R_XO, R_WIN = 0, 1024, 2048, 2560, 2816, 3072, 3328, 3584
WIN_ROWS = N_IN // 4


def _w_in_t(a):
    return jnp.transpose(a, (2, 0, 1))


def _pack_shard(w, l):
    xkv, wb = w['w_xkv'][l], w['w_branch_b'][l]
    parts = [w['w_ff1'][l], w['w_ff2'][l], jnp.concatenate([xkv[:512], xkv[512:]], axis=1),
             jnp.concatenate([w['w_branch_a'][l], wb[:256], wb[256:], w['w_branch_c'][l]], axis=1),
             w['w_out'][l], w['w_xq'][l], w['w_xo'][l],
             jnp.pad(_w_in_t(w['w_in'])[:, l, :], ((0, PACK_ROWS - R_WIN - WIN_ROWS), (0, 0)))]
    return jnp.concatenate(parts, axis=0).astype(BF16)


def _w_in_rows(gathered):
    t = gathered[:, R_WIN:R_WIN + WIN_ROWS, :].reshape(N_IN, PACK_COLS)
    return jnp.concatenate([t[2312:5384], t[256:1792], t[1800:2312], t[0:256],
                            jnp.pad(t[1792:1800], ((0, NP - P_F - 8), (0, 0)))], axis=0)


def _w_in_grad_rows(grads, dwt):
    t = jnp.concatenate([dwt[P_A:P_A + 256], dwt[P_Q:P_Q + 1536], dwt[P_F:P_F + 8], dwt[P_C:P_C + 512], dwt[P_G:P_G + 3072]],
                        axis=0)
    return lax.dynamic_update_slice(grads, t.reshape(4, WIN_ROWS, PACK_COLS).astype(grads.dtype), (0, R_WIN, 0))


def _small_prep(sw, l):
    eye = jnp.eye(4, dtype=F32)
    bd = jnp.einsum('gh,gcd->gchd', eye, sw['pool_w'][l]).reshape(POOL_W, POOL_W).astype(BF16)
    tril = jnp.tril(jnp.ones((SGU_CHUNK, SGU_CHUNK), F32))
    wm = (sw['sgu_w'][l] * tril[None]).astype(BF16)
    return dict(
        g_mix=sw['norm_mix_g'][l][None], g_x=sw['norm_xattn_g'][l][None], g_mem=sw['norm_mem_g'][l][None],
        g_ffn=sw['norm_ffn_g'][l][None], bd=bd, pool_scale=sw['pool_scale'][l][None],
        bf=jnp.pad(sw['b_forget'][l], (0, FCOLS - 8))[None], sgu_g=sw['sgu_norm_g'][l][None], wm=wm,
        wmt=jnp.transpose(wm, (0, 2, 1)), sgu_bias=jnp.repeat(sw['sgu_b'][l].T, 64, axis=1), bg=sw['b_gate'][l][None])


def _rows4(r0):
    return dict(n=D, k=D, b_block=(4, 256, 512), b_index=lambda i, j, k: (0, r0 // 256, j))


def _rows_t(r0):
    return dict(tb=True, n=D, k=D, tn=256, b_block=(None, 256, PACK_COLS), b_index=lambda i, j, k: (j, r0 // 256, 0))


def _rows_grad(r0):
    return dict(ta=True, tm=256, tn=512, o_block=(None, 256, 512), o_index=lambda i, j, k: (i, r0 // 256, j))


def _add_to(r, e):
    return e + r


def _layer_fwd(x, mem, G, w_in_t, sp, l):
    t = f"l{l}"
    S = x.shape[0]
    h = _rms_fwd(x, sp['g_mix'], f"rms_mix_{t}")
    proj = _mm(h, w_in_t, name=f"proj_{t}", out_dtype=F32, tb=True)
    d, ya = _pool_fwd(proj, sp['bd'], sp['pool_scale'], f"pool_fwd_{t}")
    fcum = _fgate_fwd(proj, sp['bf'], f"fgate_fwd_{t}")
    f8 = fcum[:, :8]
    fcol = f8.reshape(S, 4, 2).transpose(1, 0, 2)
    frow = f8.T.reshape(4, 2, S)
    qkv = proj[:, P_Q:P_Q + 3 * FOX_W].astype(BF16)
    o, o32, lse = _fox_fwd(qkv, fcol, frow, f"fox_fwd_{t}")
    sg = _sgu_fwd(proj, sp['sgu_g'], sp['wm'], sp['sgu_bias'], f"sgu_fwd_{t}")
    merged = _merge_fwd(proj, ya, o, sg, G, sp['bg'], f"merge_fwd_{t}")
    x1 = _mm(merged, G, name=f"out_{t}", out_dtype=F32, extra=x, epi=_add_to, **_rows4(R_OUT))
    hx = _rms_fwd(x1, sp['g_x'], f"rms_x_{t}")
    hm = _rms_fwd(mem, sp['g_mem'], f"rms_mem_{t}")
    xq = _mm(hx, G, name=f"xq_{t}", out_dtype=BF16, **_rows4(R_XQ))
    kv = _mm(hm, G, name=f"xkv_{t}", out_dtype=BF16, n=2 * D, k=D, tn=512, tk=512, b_block=(None, 512, 512),
             b_index=lambda i, j, k: (j, R_XKV // 512, k))
    o2 = _xattn_fwd(xq, kv, f"xattn_fwd_{t}")
    x2 = _mm(o2, G, name=f"xo_{t}", out_dtype=F32, extra=x1, epi=_add_to, **_rows4(R_XO))
    hf = _rms_fwd(x2, sp['g_ffn'], f"rms_ffn_{t}")
    z = _mm(hf, G, name=f"ff1_{t}", out_dtype=F32, n=D_FF, k=D, tn=512, b_block=(None, 1024, 512),
            b_index=lambda i, j, k: (j // 2, R_FF1 // 1024, j % 2))
    x3 = _mm(z, G, name=f"ff2_{t}", out_dtype=F32, a_fn=_relu2, extra=x2, epi=_add_to, n=D, k=D_FF, tk=1024,
             b_block=(None, 1024, 512), b_index=lambda i, j, k: (k, R_FF2 // 1024, j))
    saved = dict(x=x, h=h, proj=proj, d=d, ya=ya, fcol=fcol, frow=frow, qkv=qkv, o=o, o32=o32, lse=lse, sg=sg, merged=merged, x1=x1,
                 hx=hx, hm=hm, xq=xq, kv=kv, o2=o2, x2=x2, hf=hf, z=z)
    return x3, saved


def _layer_bwd(dx3, mem, G, w_in_t, sp, sv, l):
    t = f"l{l}"
    S = dx3.shape[0]
    gs = {}
    gp = jnp.zeros((4, PACK_ROWS, PACK_COLS), BF16)
    dz = _mm(dx3, G, name=f"d_a2_{t}", out_dtype=BF16, tb=True, n=D_FF, k=D, tn=512, b_block=(None, 512, PACK_COLS),
             b_index=lambda i, j, k: (j // 2, R_FF2 // 512 + j % 2, 0), extra=sv['z'],
             epi=lambda r, e: r * (2.0 * jnp.maximum(e, 0.0)))
    gp = _mm(sv['z'], dx3, name=f"dw_ff2_{t}", out_dtype=BF16, ta=True, a_fn=_relu2, into=gp, tm=1024, tn=512,
             o_block=(None, 1024, 512), o_index=lambda i, j, k: (i, R_FF2 // 1024, j))
    gp = _mm(sv['hf'], dz, name=f"dw_ff1_{t}", out_dtype=BF16, ta=True, into=gp, tm=1024, tn=512,
             o_block=(None, 1024, 512), o_index=lambda i, j, k: (j // 2, R_FF1 // 1024, j % 2))
    dhf = _mm(dz, G, name=f"d_hf_{t}", out_dtype=F32, tb=True, n=D, k=D_FF, tn=512, tk=1024, b_block=(None, 512, PACK_COLS),
              b_index=lambda i, j, k: (k, R_FF1 // 512 + j, 0))
    dx2, gs['norm_ffn_g'] = _rms_bwd(dhf, sv['x2'], sp['g_ffn'], dx3, f"rms_ffn_bwd_{t}")
    do2 = _mm(dx2, G, name=f"d_o2_{t}", out_dtype=BF16, **_rows_t(R_XO))
    gp = _mm(sv['o2'], dx2, name=f"dw_xo_{t}", out_dtype=BF16, into=gp, **_rows_grad(R_XO))
    dxq, dkv = _xattn_bwd(sv['xq'], sv['kv'], do2, f"xattn_bwd_{t}")
    gp = _mm(sv['hm'], dkv, name=f"dw_xkv_{t}", out_dtype=BF16, ta=True, into=gp, tm=512, tn=512,
             o_block=(None, 512, 512), o_index=lambda i, j, k: (j, R_XKV // 512, i))
    dhm = _mm(dkv, G, name=f"d_hm_{t}", out_dtype=F32, tb=True, n=D, k=2 * D, tn=512, tk=512, b_block=(None, 512, 512),
              b_index=lambda i, j, k: (k, R_XKV // 512, j))
    gs['norm_mem_g'] = _rms_bwd(dhm, mem, sp['g_mem'], None, f"rms_mem_bwd_{t}")
    gp = _mm(sv['hx'], dxq, name=f"dw_xq_{t}", out_dtype=BF16, into=gp, **_rows_grad(R_XQ))
    dhx = _mm(dxq, G, name=f"d_hx_{t}", out_dtype=F32, **_rows_t(R_XQ))
    dx1, gs['norm_xattn_g'] = _rms_bwd(dhx, sv['x1'], sp['g_x'], dx2, f"rms_x_bwd_{t}")
    gp = _mm(sv['merged'], dx1, name=f"dw_out_{t}", out_dtype=BF16, into=gp, **_rows_grad(R_OUT))
    dm = _mm(dx1, G, name=f"d_merged_{t}", out_dtype=F32, **_rows_t(R_OUT))
    dg, dya, do, dsg, gp, gs['b_gate'] = _merge_bwd(dm, sv['proj'], sv['ya'], sv['o'], sv['sg'], G, sp['bg'], gp, f"merge_bwd_{t}")
    dc, dws, dbias, gs['sgu_norm_g'] = _sgu_bwd(dsg, sv['proj'], sp['sgu_g'], sp['wm'], sp['wmt'], sp['sgu_bias'], f"sgu_bwd_{t}")
    tril = jnp.tril(jnp.ones((SGU_CHUNK, SGU_CHUNK), F32))
    gs['sgu_w'] = dws * tril[None]
    gs['sgu_b'] = dbias.reshape(SGU_CHUNK, 4, 64).sum(-1).T
    dq, dk, dv, dfrow, dfcol = _fox_bwd(sv['qkv'], sv['o32'], do, sv['lse'], sv['fcol'], sv['frow'], f"fox_bwd_{t}")
    dF = jnp.pad(dfrow.reshape(8, S).T + dfcol.transpose(1, 0, 2).reshape(S, 8), ((0, 0), (0, FCOLS - 8)))
    df, dbf = _fgate_bwd(dF, sv['proj'], sp['bf'], f"fgate_bwd_{t}")
    gs['b_forget'] = dbf[:, :8]
    da, dbd, gs['pool_scale'] = _pool_bwd(dya, sv['d'], sp['bd'], sp['pool_scale'], f"pool_bwd_{t}")
    gs['pool_w'] = jnp.stack([dbd[g * 64:(g + 1) * 64, g * 64:(g + 1) * 64] for g in range(4)])
    dproj = jnp.concatenate([dg, dq, dk, dv, dc, da, df], axis=1)
    dwt = _mm(dproj, sv['h'], name=f"dw_in_{t}", out_dtype=BF16, ta=True, tm=512, tn=1024)
    gp = _w_in_grad_rows(gp, dwt)
    dh = _mm(dproj, w_in_t, name=f"d_h_{t}", out_dtype=F32, tk=512)
    dx, gs['norm_mix_g'] = _rms_bwd(dh, sv['x'], sp['g_mix'], dx1, f"rms_mix_bwd_{t}")
    return dx, gp, gs


SMALL_ROWS = 1424
GRAD_BLOCKS = {
    'w_ff1': (1024, lambda i: (R_FF1 // 256 + i, 0)), 'w_ff2': (1024, lambda i: (R_FF2 // 256 + i, 0)),
    'w_out': (1024, lambda i: (R_OUT // 256 + i, 0)), 'w_xq': (1024, lambda i: (R_XQ // 256 + i, 0)),
    'w_xo': (1024, lambda i: (R_XO // 256 + i, 0)), 'w_xkv': (512, lambda i: (R_XKV // 256 + i % 2, i // 2)),
    'w_branch_a': (256, lambda i: (R_BRANCH // 256, 0)), 'w_branch_b': (256, lambda i: (R_BRANCH // 256, 1 + i)),
    'w_branch_c': (256, lambda i: (R_BRANCH // 256, 3)),
}


def _pack_small(parts):
    flat = jnp.concatenate([p.reshape(-1) for p in parts])
    return jnp.pad(flat, (0, SMALL_ROWS * 128 - flat.shape[0])).reshape(SMALL_ROWS, 128)


def _unpack_small(buf, shapes):
    flat, out, r = buf.reshape(-1), [], 0
    for s in shapes:
        n = math.prod(s)
        out.append(flat[r:r + n].reshape(s))
        r += n
    return out


def kernel(x, mem, norm_mix_g, w_in, b_forget, pool_w, pool_scale, sgu_norm_g, sgu_w, sgu_b, w_branch_a, w_branch_b, w_branch_c, b_gate, w_out, norm_xattn_g, norm_mem_g, w_xq, w_xkv, w_xo, norm_ffn_g, w_ff1, w_ff2, final_norm_g, loss_target, m_norm_mix_g, m_w_in, m_b_forget, m_pool_w, m_pool_scale, m_sgu_norm_g, m_sgu_w, m_sgu_b, m_w_branch_a, m_w_branch_b, m_w_branch_c, m_b_gate, m_w_out, m_norm_xattn_g, m_norm_mem_g, m_w_xq, m_w_xkv, m_w_xo, m_norm_ffn_g, m_w_ff1, m_w_ff2, m_final_norm_g, v_norm_mix_g, v_w_in, v_b_forget, v_pool_w, v_pool_scale, v_sgu_norm_g, v_sgu_w, v_sgu_b, v_w_branch_a, v_w_branch_b, v_w_branch_c, v_b_gate, v_w_out, v_norm_xattn_g, v_norm_mem_g, v_w_xq, v_w_xkv, v_w_xo, v_norm_ffn_g, v_w_ff1, v_w_ff2, v_final_norm_g):
    args = (norm_mix_g, w_in, b_forget, pool_w, pool_scale, sgu_norm_g, sgu_w, sgu_b, w_branch_a, w_branch_b, w_branch_c, b_gate,
            w_out, norm_xattn_g, norm_mem_g, w_xq, w_xkv, w_xo, norm_ffn_g, w_ff1, w_ff2, final_norm_g)
    margs = (m_norm_mix_g, m_w_in, m_b_forget, m_pool_w, m_pool_scale, m_sgu_norm_g, m_sgu_w, m_sgu_b, m_w_branch_a, m_w_branch_b,
             m_w_branch_c, m_b_gate, m_w_out, m_norm_xattn_g, m_norm_mem_g, m_w_xq, m_w_xkv, m_w_xo, m_norm_ffn_g, m_w_ff1, m_w_ff2,
             m_final_norm_g)
    vargs = (v_norm_mix_g, v_w_in, v_b_forget, v_pool_w, v_pool_scale, v_sgu_norm_g, v_sgu_w, v_sgu_b, v_w_branch_a, v_w_branch_b,
             v_w_branch_c, v_b_gate, v_w_out, v_norm_xattn_g, v_norm_mem_g, v_w_xq, v_w_xkv, v_w_xo, v_norm_ffn_g, v_w_ff1, v_w_ff2,
             v_final_norm_g)
    w = dict(zip(W_NAMES, args))
    mo = dict(zip(W_NAMES, margs))
    vo = dict(zip(W_NAMES, vargs))
    xs, mems, tgt = x[0], mem[0], loss_target[0]

    shards = [_pack_shard(w, l) for l in range(DEPTH)]
    preps = [_small_prep(w, l) for l in range(DEPTH)]
    half = PACK_ROWS // 2
    G, w_in_t = [None] * DEPTH, [None] * DEPTH
    G[0] = _gather_weights(shards[0], "gather_w_l0")
    act, saved = xs, []
    for l in range(DEPTH):
        w_in_t[l] = _w_in_rows(G[l])
        sp = preps[l]
        if l + 1 < DEPTH:
            token, landed = _split_exchange(
                shards[l + 1], half, lambda ref, px, py, c, r0, cr: ref.at[pl.ds(pl.multiple_of(c * half + r0, 16), cr), :],
                f"gather_w_l{l + 1}")
            sp = dict(sp, g_mix=sp['g_mix'] + token[0, 0])
        act, sv = _layer_fwd(act, mems, G[l], w_in_t[l], sp, l)
        saved.append(sv)
        if l + 1 < DEPTH:
            G[l + 1] = _gather_finish(shards[l + 1], landed(act), f"gather_w_l{l + 1}_finish")
    loss_part, dact, d_final_g = _loss_head(act, w['final_norm_g'][None], tgt, "loss_head")

    red, small_g = [None] * DEPTH, [None] * DEPTH
    dact, gp, small_g[DEPTH - 1] = _layer_bwd(dact, mems, G[DEPTH - 1], w_in_t[DEPTH - 1], preps[DEPTH - 1], saved[DEPTH - 1], DEPTH - 1)
    for l in reversed(range(DEPTH)):
        if l > 0:
            def below(token, l=l, dact=dact):
                return _layer_bwd(dact + token[0, 0], mems, G[l - 1], w_in_t[l - 1], preps[l - 1], saved[l - 1], l - 1)

            red[l], (dact, gp, small_g[l - 1]) = _reduce_scatter(gp, f"l{l}", overlap=below)
        else:
            red[l] = _reduce_scatter(gp, f"l{l}")
    grad_x = dact[None]

    per_layer = [n for n in SMALL_NAMES if n != 'final_norm_g']
    small_shapes = [w[n].shape for n in per_layer] + [(D,), (1,)]
    parts = [jnp.stack([small_g[l][n].reshape(w[n].shape[1:]) for l in range(DEPTH)]) for n in per_layer]
    small_red = _unpack_small(_all_reduce_small(_pack_small(parts + [d_final_g.reshape(D), loss_part.reshape(1)]), "small"), small_shapes)
    grads = dict(zip(per_layer + ['final_norm_g'], small_red[:-1]))
    loss = small_red[-1].reshape(())

    delta, new_m, new_v = {}, {}, {}
    for n, (c, g_index) in GRAD_BLOCKS.items():
        grads[n], delta[n], new_m[n], new_v[n] = _adamw_packed(red, w[n], mo[n], vo[n], g_index, f"adamw_{n}")
    g_t = jnp.stack([r[R_WIN:R_WIN + WIN_ROWS] for r in red], axis=1)
    upd = _adamw(g_t, _w_in_t(w['w_in']), _w_in_t(mo['w_in']), _w_in_t(vo['w_in']), "adamw_w_in", block=(WIN_ROWS, DEPTH, 128))
    grads['w_in'], delta['w_in'], new_m['w_in'], new_v['w_in'] = [jnp.transpose(a, (1, 2, 0)) for a in (g_t,) + tuple(upd)]
    small_all = per_layer + ['final_norm_g']
    shapes_all = [w[n].shape for n in small_all]
    packed = [_pack_small([d[n] for n in small_all])[None] for d in (grads, w, mo, vo)]
    ds, ms, vs = _adamw(*packed, "adamw_small")
    for n, a, b, c in zip(small_all, _unpack_small(ds[0], shapes_all), _unpack_small(ms[0], shapes_all), _unpack_small(vs[0], shapes_all)):
        delta[n], new_m[n], new_v[n] = a, b, c

    return (loss, grad_x, *[grads[n] for n in W_NAMES], *[delta[n] for n in W_NAMES], *[new_m[n] for n in W_NAMES],
            *[new_v[n] for n in W_NAMES])
```

```python
import math

import jax
import jax.numpy as jnp
from jax import lax
from jax.experimental import pallas as pl
from jax.experimental.pallas import tpu as pltpu

F32 = jnp.float32
BF16 = jnp.bfloat16

D = 1024
DEPTH = 2
POOL_W = 256
FOX_W = 512
SGU_W = 256
SGU_CHUNK = 128
N_IN = 5384
P_G, P_Q, P_K, P_V, P_C, P_A, P_F = 0, 3072, 3584, 4096, 4608, 5120, 5376
NP = 5632
XH, XHD = 4, 256
D_FF = 4096
EPS = 1e-6
NEG = -1e30
FOX_SCALE = 64 ** -0.5
X_SCALE = 256 ** -0.5
GELU_K = math.sqrt(2.0 / math.pi)
GELU_C = 0.044715

ADAM_LR, ADAM_B1, ADAM_B2, ADAM_EPS, ADAM_WD, ADAM_STEP = 0.001, 0.9, 0.999, 1e-08, 0.01, 10

VMEM_LIMIT = 48 * 1024 * 1024
MESH = pl.DeviceIdType.MESH

IN_NAMES = ['x', 'mem', 'norm_mix_g', 'w_in', 'b_forget', 'pool_w', 'pool_scale', 'sgu_norm_g', 'sgu_w', 'sgu_b',
            'w_branch_a', 'w_branch_b', 'w_branch_c', 'b_gate', 'w_out', 'norm_xattn_g', 'norm_mem_g', 'w_xq',
            'w_xkv', 'w_xo', 'norm_ffn_g', 'w_ff1', 'w_ff2', 'final_norm_g']
W_NAMES = IN_NAMES[2:]
BIG_NAMES = ['w_in', 'w_branch_a', 'w_branch_b', 'w_branch_c', 'w_out', 'w_xq', 'w_xkv', 'w_xo', 'w_ff1', 'w_ff2']
SMALL_NAMES = [n for n in W_NAMES if n not in BIG_NAMES]
PACK_COLS = 1024


ANY = pl.BlockSpec(memory_space=pl.ANY)


def _cp(sem=None):
    return pltpu.CompilerParams(dimension_semantics=sem, vmem_limit_bytes=VMEM_LIMIT)


def _mm(a, b, *, name, out_dtype, ta=False, tb=False, tm=1024, tn=512, tk=1024, a_fn=None, extra=None, epi=None,
        n=None, k=None, b_block=None, b_index=None, into=None, o_block=None, o_index=None):
    M = a.shape[1] if ta else a.shape[0]
    K = k if k is not None else (a.shape[0] if ta else a.shape[1])
    N = n if n is not None else (b.shape[0] if tb else b.shape[1])
    tm, tn, tk = min(tm, M), min(tn, N), min(tk, K)
    assert M % tm == 0 and N % tn == 0 and K % tk == 0, (name, M, N, K)
    nk = K // tk
    a_spec = pl.BlockSpec((tk, tm), lambda i, j, k: (k, i)) if ta else pl.BlockSpec((tm, tk), lambda i, j, k: (i, k))
    if b_block is not None:
        b_spec = pl.BlockSpec(b_block, b_index)
    else:
        b_spec = pl.BlockSpec((tn, tk), lambda i, j, k: (j, k)) if tb else pl.BlockSpec((tk, tn), lambda i, j, k: (k, j))
    dn = (((0 if ta else 1,), (1 if tb else 0,)), ((), ()))
    tile = pl.BlockSpec((tm, tn), lambda i, j, k: (i, j))
    o_spec = pl.BlockSpec(o_block, o_index) if into is not None else tile
    in_specs = [a_spec, b_spec] + ([tile] if extra is not None else []) + ([ANY] if into is not None else [])
    n_in = len(in_specs)

    def body(*refs):
        a_ref, b_ref = refs[0], refs[1]
        e_ref = refs[2] if extra is not None else None
        o_ref, acc_ref = refs[n_in], refs[n_in + 1]
        kk = pl.program_id(2)

        @pl.when(kk == 0)
        def _():
            acc_ref[...] = jnp.zeros_like(acc_ref)

        av = a_ref[...]
        if a_fn is not None:
            av = a_fn(av)
        bv = b_ref[...]
        if bv.ndim == 3:
            bv = bv.reshape(-1, bv.shape[-1])
        acc_ref[...] += lax.dot_general(av.astype(BF16), bv.astype(BF16), dn, preferred_element_type=F32)

        @pl.when(kk == nk - 1)
        def _():
            r = acc_ref[...]
            if epi is not None:
                r = epi(r, e_ref[...])
            o_ref[...] = r.astype(o_ref.dtype)

    args = (a, b) + ((extra,) if extra is not None else ()) + ((into,) if into is not None else ())
    out_shape = jax.ShapeDtypeStruct(into.shape, into.dtype) if into is not None else jax.ShapeDtypeStruct((M, N), out_dtype)
    return pl.pallas_call(
        body, out_shape=out_shape, grid=(M // tm, N // tn, nk), in_specs=in_specs, out_specs=o_spec,
        scratch_shapes=[pltpu.VMEM((tm, tn), F32)], input_output_aliases={n_in - 1: 0} if into is not None else {},
        compiler_params=_cp(("parallel", "parallel", "arbitrary")), name=name)(*args)


def _relu2(z):
    r = jnp.maximum(z, 0.0)
    return r * r


def _rms_fwd(x, g, name, tr=256):
    R, n = x.shape
    tr = min(tr, R)

    def body(x_ref, g_ref, h_ref):
        xv = x_ref[...]
        rstd = lax.rsqrt(jnp.mean(xv * xv, axis=-1, keepdims=True) + EPS)
        h_ref[...] = (xv * rstd * g_ref[...]).astype(BF16)

    return pl.pallas_call(
        body, out_shape=jax.ShapeDtypeStruct((R, n), BF16), grid=(R // tr,),
        in_specs=[pl.BlockSpec((tr, n), lambda i: (i, 0)), pl.BlockSpec((1, n), lambda i: (0, 0))],
        out_specs=pl.BlockSpec((tr, n), lambda i: (i, 0)), compiler_params=_cp(("parallel",)), name=name)(x, g)


def _rms_bwd(dh, x, g, dres, name, tr=256):
    R, n = x.shape
    tr = min(tr, R)
    need_dx = dres is not None

    def body(*refs):
        if need_dx:
            dh_ref, x_ref, g_ref, r_ref, dx_ref, dg_ref = refs
        else:
            dh_ref, x_ref, g_ref, dg_ref = refs
        i = pl.program_id(0)
        xv = x_ref[...]
        dhv = dh_ref[...].astype(F32)
        rstd = lax.rsqrt(jnp.mean(xv * xv, axis=-1, keepdims=True) + EPS)
        xhat = xv * rstd

        @pl.when(i == 0)
        def _():
            dg_ref[...] = jnp.zeros_like(dg_ref)

        dg_ref[...] += jnp.sum(dhv * xhat, axis=0, keepdims=True)
        if need_dx:
            t = dhv * g_ref[...]
            dx_ref[...] = r_ref[...] + rstd * (t - xhat * jnp.mean(t * xhat, axis=-1, keepdims=True))

    row = pl.BlockSpec((tr, n), lambda i: (i, 0))
    vec = pl.BlockSpec((1, n), lambda i: (0, 0))
    if need_dx:
        return pl.pallas_call(
            body, out_shape=(jax.ShapeDtypeStruct((R, n), F32), jax.ShapeDtypeStruct((1, n), F32)), grid=(R // tr,),
            in_specs=[row, row, vec, row], out_specs=(row, vec), compiler_params=_cp(("arbitrary",)), name=name)(dh, x, g, dres)
    return pl.pallas_call(
        body, out_shape=jax.ShapeDtypeStruct((1, n), F32), grid=(R // tr,),
        in_specs=[row, row, vec], out_specs=vec, compiler_params=_cp(("arbitrary",)), name=name)(dh, x, g)


def _loss_head(x, g, tgt, name, tr=256):
    R, n = x.shape

    def body(x_ref, g_ref, t_ref, loss_ref, dx_ref, dg_ref):
        i = pl.program_id(0)
        xv = x_ref[...]
        gv = g_ref[...]
        rstd = lax.rsqrt(jnp.mean(xv * xv, axis=-1, keepdims=True) + EPS)
        xhat = xv * rstd
        e = xhat * gv - t_ref[...]

        @pl.when(i == 0)
        def _():
            loss_ref[...] = jnp.zeros_like(loss_ref)
            dg_ref[...] = jnp.zeros_like(dg_ref)

        loss_ref[...] += 0.5 * jnp.sum(jnp.sum(e * e, axis=-1, keepdims=True) / n, axis=0, keepdims=True)
        dy = e / n
        dg_ref[...] += jnp.sum(dy * xhat, axis=0, keepdims=True)
        t = dy * gv
        dx_ref[...] = rstd * (t - xhat * jnp.mean(t * xhat, axis=-1, keepdims=True))

    row = pl.BlockSpec((tr, n), lambda i: (i, 0))
    vec = pl.BlockSpec((1, n), lambda i: (0, 0))
    one = pl.BlockSpec((1, 1), lambda i: (0, 0))
    return pl.pallas_call(
        body, out_shape=(jax.ShapeDtypeStruct((1, 1), F32), jax.ShapeDtypeStruct((R, n), F32), jax.ShapeDtypeStruct((1, n), F32)),
        grid=(R // tr,), in_specs=[row, vec, row], out_specs=(one, row, vec),
        compiler_params=_cp(("arbitrary",)), name=name)(x, g, tgt)


def _pool_masks(S):
    row = lax.broadcasted_iota(jnp.int32, (S, POOL_W), 0)
    grp = lax.broadcasted_iota(jnp.int32, (S, POOL_W), 1) // 64
    win = jnp.where(grp == 0, 2, jnp.where(grp == 1, 4, jnp.where(grp == 2, 8, 16)))
    cnt = jnp.minimum(row + 1, win).astype(F32)
    return row, grp, cnt


def _by_group(grp, v0, v1, v2, v3):
    return jnp.where(grp == 0, v0, jnp.where(grp == 1, v1, jnp.where(grp == 2, v2, v3)))


def _pool_fwd(proj, bd, scale, name):
    S = proj.shape[0]

    def body(a_ref, bd_ref, sc_ref, d_ref, y_ref):
        a = a_ref[...]
        row, grp, cnt = _pool_masks(S)

        def back(v, k):
            return jnp.where(row >= k, pltpu.roll(v, k, 0), 0.0)

        s1 = a + back(a, 1)
        s2 = s1 + back(s1, 2)
        s3 = s2 + back(s2, 4)
        s4 = s3 + back(s3, 8)
        d = (_by_group(grp, s1, s2, s3, s4) / cnt - a).astype(BF16)
        d_ref[...] = d
        y_ref[...] = (jnp.dot(d, bd_ref[...], preferred_element_type=F32) * sc_ref[...]).astype(BF16)

    full = lambda r, c: pl.BlockSpec((r, c), lambda i: (0, 0))
    return pl.pallas_call(
        body, out_shape=(jax.ShapeDtypeStruct((S, POOL_W), BF16), jax.ShapeDtypeStruct((S, POOL_W), BF16)), grid=(1,),
        in_specs=[pl.BlockSpec((S, POOL_W), lambda i: (0, P_A // POOL_W)), full(POOL_W, POOL_W), full(1, POOL_W)],
        out_specs=(full(S, POOL_W), full(S, POOL_W)), compiler_params=_cp(("arbitrary",)), name=name)(proj, bd, scale)


def _pool_bwd(dya, d, bd, scale, name):
    S = dya.shape[0]

    def body(dy_ref, d_ref, bd_ref, sc_ref, da_ref, dbd_ref, dsc_ref):
        dy = dy_ref[...]
        dv = d_ref[...]
        bdv = bd_ref[...]
        row, grp, cnt = _pool_masks(S)
        yraw = jnp.dot(dv, bdv, preferred_element_type=F32)
        dsc_ref[...] = jnp.sum(dy * yraw, axis=0, keepdims=True)
        tb = (dy * sc_ref[...]).astype(BF16)
        dbd_ref[...] = lax.dot_general(dv, tb, (((0,), (0,)), ((), ())), preferred_element_type=F32)
        dd = lax.dot_general(tb, bdv, (((1,), (1,)), ((), ())), preferred_element_type=F32)
        e = dd / cnt

        def fwd(v, k):
            return jnp.where(row < S - k, pltpu.roll(v, S - k, 0), 0.0)

        r1 = e + fwd(e, 1)
        r2 = r1 + fwd(r1, 2)
        r3 = r2 + fwd(r2, 4)
        r4 = r3 + fwd(r3, 8)
        da_ref[...] = (_by_group(grp, r1, r2, r3, r4) - dd).astype(BF16)

    full = lambda r, c: pl.BlockSpec((r, c), lambda i: (0, 0))
    return pl.pallas_call(
        body, out_shape=(jax.ShapeDtypeStruct((S, POOL_W), BF16), jax.ShapeDtypeStruct((POOL_W, POOL_W), F32),
                         jax.ShapeDtypeStruct((1, POOL_W), F32)), grid=(1,),
        in_specs=[full(S, POOL_W), full(S, POOL_W), full(POOL_W, POOL_W), full(1, POOL_W)],
        out_specs=(full(S, POOL_W), full(POOL_W, POOL_W), full(1, POOL_W)),
        compiler_params=_cp(("arbitrary",)), name=name)(dya, d, bd, scale)


FCOLS = 128


def _log_sigmoid(z):
    return -(jnp.maximum(-z, 0.0) + jnp.log1p(jnp.exp(-jnp.abs(z))))


def _fgate_fwd(proj, bf, name):
    S = proj.shape[0]

    def body(f_ref, b_ref, o_ref):
        v = _log_sigmoid(f_ref[...] + b_ref[...])
        row = lax.broadcasted_iota(jnp.int32, (S, FCOLS), 0)
        k = 1
        while k < S:
            v = v + jnp.where(row >= k, pltpu.roll(v, k, 0), 0.0)
            k *= 2
        o_ref[...] = v

    return pl.pallas_call(
        body, out_shape=jax.ShapeDtypeStruct((S, FCOLS), F32), grid=(1,),
        in_specs=[pl.BlockSpec((S, FCOLS), lambda i: (0, P_F // FCOLS)), pl.BlockSpec((1, FCOLS), lambda i: (0, 0))],
        out_specs=pl.BlockSpec((S, FCOLS), lambda i: (0, 0)), compiler_params=_cp(("arbitrary",)), name=name)(proj, bf)


def _fgate_bwd(dF, proj, bf, name):
    S = proj.shape[0]

    def body(dF_ref, f_ref, b_ref, df_ref, db_ref):
        v = dF_ref[...]
        row = lax.broadcasted_iota(jnp.int32, (S, FCOLS), 0)
        k = 1
        while k < S:
            v = v + jnp.where(row < S - k, pltpu.roll(v, S - k, 0), 0.0)
            k *= 2
        z = f_ref[...] + b_ref[...]
        df = v * (1.0 / (1.0 + jnp.exp(z)))
        db_ref[...] = jnp.sum(df, axis=0, keepdims=True)
        df_ref[...] = jnp.concatenate([df, jnp.zeros_like(df)], axis=1).astype(BF16)

    return pl.pallas_call(
        body, out_shape=(jax.ShapeDtypeStruct((S, 2 * FCOLS), BF16), jax.ShapeDtypeStruct((1, FCOLS), F32)), grid=(1,),
        in_specs=[pl.BlockSpec((S, FCOLS), lambda i: (0, 0)), pl.BlockSpec((S, FCOLS), lambda i: (0, P_F // FCOLS)),
                  pl.BlockSpec((1, FCOLS), lambda i: (0, 0))],
        out_specs=(pl.BlockSpec((S, 2 * FCOLS), lambda i: (0, 0)), pl.BlockSpec((1, FCOLS), lambda i: (0, 0))),
        compiler_params=_cp(("arbitrary",)), name=name)(dF, proj, bf)


def _fox_scores(qe, kj, fq, fk, r0, c0, tq, tk, diagonal):
    s = lax.dot_general(qe, kj, (((1,), (1,)), ((), ())), preferred_element_type=F32) * FOX_SCALE
    s = s + (fq - fk)
    if not diagonal:
        return s
    rows = r0 + lax.broadcasted_iota(jnp.int32, (tq, tk), 0)
    cols = c0 + lax.broadcasted_iota(jnp.int32, (tq, tk), 1)
    return jnp.where(rows >= cols, s, NEG)


def _fox_fwd(qkv, fcol, frow, name, tq=256):
    S = qkv.shape[0]
    tk = tq

    def body(q_ref, k_ref, v_ref, fc_ref, fr_ref, o_ref, o32_ref, lse_ref):
        i = pl.program_id(1)
        r0 = i * tq
        q = q_ref[...]
        half = lax.broadcasted_iota(jnp.int32, (tq, 128), 1) // 64
        qs = [jnp.where(half == e, q, jnp.zeros_like(q)) for e in (0, 1)]
        fqs = [fc_ref[0, :, e:e + 1] for e in (0, 1)]

        def step(j, carry, diagonal=False):
            c0 = pl.multiple_of(j * tk, tk)
            kj = k_ref[pl.ds(c0, tk), :]
            vj = v_ref[pl.ds(c0, tk), :]
            out = []
            for e in (0, 1):
                m, l, acc = carry[e]
                s = _fox_scores(qs[e], kj, fqs[e], fr_ref[0, e:e + 1, pl.ds(c0, tk)], r0, c0, tq, tk, diagonal)
                m_new = jnp.maximum(m, jnp.max(s, axis=-1, keepdims=True))
                alpha = jnp.exp(m - m_new)
                p = jnp.exp(s - m_new)
                out.append((m_new, alpha * l + jnp.sum(p, axis=-1, keepdims=True),
                            alpha * acc + jnp.dot(p.astype(BF16), vj, preferred_element_type=F32)))
            return tuple(out)

        init = (jnp.full((tq, 1), NEG, F32), jnp.zeros((tq, 1), F32), jnp.zeros((tq, 128), F32))
        carry = lax.fori_loop(0, i, step, (init, init))
        carry = step(i, carry, diagonal=True)
        outs = []
        for e in (0, 1):
            m, l, acc = carry[e]
            outs.append(acc / l)
            lse_ref[0, :, e:e + 1] = m + jnp.log(l)
        o = jnp.where(half == 0, outs[0], outs[1])
        o32_ref[...] = o
        o_ref[...] = o.astype(BF16)

    tile = pl.BlockSpec((tq, 128), lambda h, i: (i, h))
    return pl.pallas_call(
        body, out_shape=(jax.ShapeDtypeStruct((S, FOX_W), BF16), jax.ShapeDtypeStruct((S, FOX_W), F32),
                         jax.ShapeDtypeStruct((4, S, 2), F32)), grid=(4, S // tq),
        in_specs=[tile, pl.BlockSpec((S, 128), lambda h, i: (0, 4 + h)), pl.BlockSpec((S, 128), lambda h, i: (0, 8 + h)),
                  pl.BlockSpec((1, tq, 2), lambda h, i: (h, i, 0)), pl.BlockSpec((1, 2, S), lambda h, i: (h, 0, 0))],
        out_specs=(tile, tile, pl.BlockSpec((1, tq, 2), lambda h, i: (h, i, 0))),
        compiler_params=_cp(("parallel", "parallel")), name=name)(qkv, qkv, qkv, fcol, frow)


def _fox_bwd(qkv, o32, do, lse, fcol, frow, name, tq=256):
    S = qkv.shape[0]
    tk = tq
    nq = S // tq

    def body(q_ref, k_ref, v_ref, o_ref, do_ref, lse_ref, fc_ref, fr_ref, dq_ref, dk_ref, dv_ref, dfr_ref, dfc_ref, dk_acc, dv_acc):
        dk_acc[...] = jnp.zeros_like(dk_acc)
        dv_acc[...] = jnp.zeros_like(dv_acc)
        dfr_ref[...] = jnp.zeros_like(dfr_ref)
        half = lax.broadcasted_iota(jnp.int32, (tq, 128), 1) // 64

        def q_block(i, _):
            r0 = pl.multiple_of(i * tq, tq)
            qi = q_ref[pl.ds(r0, tq), :]
            dob = do_ref[pl.ds(r0, tq), :].astype(BF16)
            row_dot = dob.astype(F32) * o_ref[pl.ds(r0, tq), :]
            qs = [jnp.where(half == e, qi, jnp.zeros_like(qi)) for e in (0, 1)]
            dos = [jnp.where(half == e, dob, jnp.zeros_like(dob)) for e in (0, 1)]
            deltas = [jnp.sum(jnp.where(half == e, row_dot, 0.0), axis=-1, keepdims=True) for e in (0, 1)]
            lses = [lse_ref[0, pl.ds(r0, tq), e:e + 1] for e in (0, 1)]
            fqs = [fc_ref[0, pl.ds(r0, tq), e:e + 1] for e in (0, 1)]

            def step(j, carry, diagonal=False):
                dqs, row_sums = carry
                c0 = pl.multiple_of(j * tk, tk)
                kj = k_ref[pl.ds(c0, tk), :]
                vj = v_ref[pl.ds(c0, tk), :]
                new_dq, new_rows, dkc, dvc = [], [], [], []
                for e in (0, 1):
                    s = _fox_scores(qs[e], kj, fqs[e], fr_ref[0, e:e + 1, pl.ds(c0, tk)], r0, c0, tq, tk, diagonal)
                    p = jnp.exp(s - lses[e])
                    dp = lax.dot_general(dos[e], vj, (((1,), (1,)), ((), ())), preferred_element_type=F32)
                    ds = p * (dp - deltas[e])
                    dfr_ref[0, e:e + 1, pl.ds(c0, tk)] -= jnp.sum(ds, axis=0, keepdims=True)
                    new_rows.append(row_sums[e] + jnp.sum(ds, axis=-1, keepdims=True))
                    dsb = (ds * FOX_SCALE).astype(BF16)
                    dkc.append(lax.dot_general(dsb, qi, (((0,), (0,)), ((), ())), preferred_element_type=F32))
                    dvc.append(lax.dot_general(p.astype(BF16), dob, (((0,), (0,)), ((), ())), preferred_element_type=F32))
                    new_dq.append(dqs[e] + jnp.dot(dsb, kj, preferred_element_type=F32))
                dk_acc[pl.ds(c0, tk), :] += jnp.where(half == 0, dkc[0], dkc[1])
                dv_acc[pl.ds(c0, tk), :] += jnp.where(half == 0, dvc[0], dvc[1])
                return tuple(new_dq), tuple(new_rows)

            zero, zero_col = jnp.zeros((tq, 128), F32), jnp.zeros((tq, 1), F32)
            carry = lax.fori_loop(0, i, step, ((zero, zero), (zero_col, zero_col)))
            dqs, row_sums = step(i, carry, diagonal=True)
            for e in (0, 1):
                dfc_ref[0, pl.ds(r0, tq), e:e + 1] = row_sums[e]
            dq_ref[pl.ds(r0, tq), :] = jnp.where(half == 0, dqs[0], dqs[1]).astype(BF16)
            return 0

        lax.fori_loop(0, nq, q_block, 0)
        dk_ref[...] = dk_acc[...].astype(BF16)
        dv_ref[...] = dv_acc[...].astype(BF16)

    col = lambda off: pl.BlockSpec((S, 128), lambda h: (0, off + h))
    hs2 = pl.BlockSpec((1, S, 2), lambda h: (h, 0, 0))
    h2s = pl.BlockSpec((1, 2, S), lambda h: (h, 0, 0))
    return pl.pallas_call(
        body, out_shape=(jax.ShapeDtypeStruct((S, FOX_W), BF16),) * 3 + (jax.ShapeDtypeStruct((4, 2, S), F32),
                                                                         jax.ShapeDtypeStruct((4, S, 2), F32)), grid=(4,),
        in_specs=[col(0), col(4), col(8), col(0), col(0), hs2, hs2, h2s],
        out_specs=(col(0), col(0), col(0), h2s, hs2),
        scratch_shapes=[pltpu.VMEM((S, 128), F32), pltpu.VMEM((S, 128), F32)],
        compiler_params=_cp(("parallel",)), name=name)(qkv, qkv, qkv, o32, do, lse, fcol, frow)


def _gelu(x):
    return 0.5 * x * (1.0 + jnp.tanh(GELU_K * (x + GELU_C * x * x * x)))


def _gelu_grad(x):
    th = jnp.tanh(GELU_K * (x + GELU_C * x * x * x))
    return 0.5 * (1.0 + th) + 0.5 * x * (1.0 - th * th) * GELU_K * (1.0 + 3.0 * GELU_C * x * x)


def _sgu_parts(c, gn, w_ref, bias):
    zc = _gelu(c)
    u, vv = zc[:, :SGU_W], zc[:, SGU_W:]
    rstd = lax.rsqrt(jnp.mean(vv * vv, axis=-1, keepdims=True) + EPS)
    vhat = vv * rstd
    vnb = (vhat * gn).astype(BF16)
    grp = lax.broadcasted_iota(jnp.int32, (SGU_CHUNK, SGU_W), 1) // 64
    mixed = bias
    for gi in range(4):
        mixed = mixed + jnp.where(grp == gi, jnp.dot(w_ref[gi], vnb, preferred_element_type=F32), 0.0)
    return u, rstd, vhat, vnb, grp, mixed


def _sgu_fwd(proj, gn, wm, bias, name):
    S = proj.shape[0]

    def body(c_ref, g_ref, w_ref, b_ref, o_ref):
        u, _, _, _, _, mixed = _sgu_parts(c_ref[...], g_ref[...], w_ref, b_ref[...])
        o_ref[...] = (u * mixed).astype(BF16)

    return pl.pallas_call(
        body, out_shape=jax.ShapeDtypeStruct((S, SGU_W), BF16), grid=(S // SGU_CHUNK,),
        in_specs=[pl.BlockSpec((SGU_CHUNK, 2 * SGU_W), lambda i: (i, P_C // (2 * SGU_W))),
                  pl.BlockSpec((1, SGU_W), lambda i: (0, 0)), pl.BlockSpec((4, SGU_CHUNK, SGU_CHUNK), lambda i: (0, 0, 0)),
                  pl.BlockSpec((SGU_CHUNK, SGU_W), lambda i: (0, 0))],
        out_specs=pl.BlockSpec((SGU_CHUNK, SGU_W), lambda i: (i, 0)),
        compiler_params=_cp(("parallel",)), name=name)(proj, gn, wm, bias)


def _sgu_bwd(dsg, proj, gn, wm, wmt, bias, name):
    S = proj.shape[0]

    def body(dsg_ref, c_ref, g_ref, w_ref, wt_ref, b_ref, dc_ref, dw_ref, db_ref, dg_ref):
        i = pl.program_id(0)

        @pl.when(i == 0)
        def _():
            dw_ref[...] = jnp.zeros_like(dw_ref)
            db_ref[...] = jnp.zeros_like(db_ref)
            dg_ref[...] = jnp.zeros_like(dg_ref)

        c = c_ref[...]
        gn_v = g_ref[...]
        u, rstd, vhat, vnb, grp, mixed = _sgu_parts(c, gn_v, w_ref, b_ref[...])
        dsg_v = dsg_ref[...]
        du = dsg_v * mixed
        dmix = dsg_v * u
        db_ref[...] += dmix
        dmb = dmix.astype(BF16)
        dvn = jnp.zeros((SGU_CHUNK, SGU_W), F32)
        for gi in range(4):
            dmg = jnp.where(grp == gi, dmb, jnp.zeros_like(dmb))
            dw_ref[gi] += lax.dot_general(dmg, vnb, (((1,), (1,)), ((), ())), preferred_element_type=F32)
            dvn = dvn + jnp.where(grp == gi, jnp.dot(wt_ref[gi], dmb, preferred_element_type=F32), 0.0)
        dg_ref[...] += jnp.sum(dvn * vhat, axis=0, keepdims=True)
        t = dvn * gn_v
        dvv = rstd * (t - vhat * jnp.mean(t * vhat, axis=-1, keepdims=True))
        dc_ref[...] = (jnp.concatenate([du, dvv], axis=1) * _gelu_grad(c)).astype(BF16)

    w_spec = pl.BlockSpec((4, SGU_CHUNK, SGU_CHUNK), lambda i: (0, 0, 0))
    tile = pl.BlockSpec((SGU_CHUNK, SGU_W), lambda i: (0, 0))
    vec = pl.BlockSpec((1, SGU_W), lambda i: (0, 0))
    return pl.pallas_call(
        body, out_shape=(jax.ShapeDtypeStruct((S, 2 * SGU_W), BF16), jax.ShapeDtypeStruct((4, SGU_CHUNK, SGU_CHUNK), F32),
                         jax.ShapeDtypeStruct((SGU_CHUNK, SGU_W), F32), jax.ShapeDtypeStruct((1, SGU_W), F32)),
        grid=(S // SGU_CHUNK,),
        in_specs=[pl.BlockSpec((SGU_CHUNK, SGU_W), lambda i: (i, 0)),
                  pl.BlockSpec((SGU_CHUNK, 2 * SGU_W), lambda i: (i, P_C // (2 * SGU_W))), vec, w_spec, w_spec, tile],
        out_specs=(pl.BlockSpec((SGU_CHUNK, 2 * SGU_W), lambda i: (i, 0)), w_spec, tile, vec),
        compiler_params=_cp(("arbitrary",)), name=name)(dsg, proj, gn, wm, wmt, bias)


def _sigmoid(z):
    return 1.0 / (1.0 + jnp.exp(-z))


def _merge_specs(tm):
    row = lambda n: pl.BlockSpec((tm, n), lambda i: (i, 0))
    gate = lambda b: pl.BlockSpec((tm, D), lambda i: (i, b))
    full = lambda r, c: pl.BlockSpec((r, c), lambda i: (0, 0))
    packed = pl.BlockSpec((4, 256, PACK_COLS), lambda i: (0, R_BRANCH // 256, 0))
    return row, gate, full, packed


def _branch_shards(c_ref, j):
    return c_ref[j, :, 0:256], c_ref[j, :, 256:512], c_ref[j, :, 512:768], c_ref[j, :, 768:1024]


def _merge_fwd(proj, ya, o, sg, packed_w, bg, name, tm=256):
    S = proj.shape[0]
    row, gate, full, packed = _merge_specs(tm)

    def body(g0, g1, g2, ya_ref, o_ref, sg_ref, c_ref, bg_ref, out_ref):
        yav, ov, sgv = ya_ref[...], o_ref[...], sg_ref[...]
        for j in range(4):
            cols = slice(256 * j, 256 * (j + 1))
            wa, wb0, wb1, wc = _branch_shards(c_ref, j)
            y = (jnp.dot(yav, wa, preferred_element_type=F32),
                 jnp.dot(ov[:, :256], wb0, preferred_element_type=F32) + jnp.dot(ov[:, 256:], wb1, preferred_element_type=F32),
                 jnp.dot(sgv, wc, preferred_element_type=F32))
            acc = jnp.zeros((tm, 256), F32)
            for b, g_ref in enumerate((g0, g1, g2)):
                acc = acc + _sigmoid(g_ref[:, cols] + bg_ref[:, b * D + 256 * j:b * D + 256 * (j + 1)]) * y[b]
            out_ref[:, cols] = acc.astype(BF16)

    return pl.pallas_call(
        body, out_shape=jax.ShapeDtypeStruct((S, D), BF16), grid=(S // tm,),
        in_specs=[gate(0), gate(1), gate(2), row(POOL_W), row(FOX_W), row(SGU_W), packed, full(1, 3 * D)],
        out_specs=row(D), compiler_params=_cp(("parallel",)), name=name)(proj, proj, proj, ya, o, sg, packed_w, bg)


def _merge_bwd(dm, proj, ya, o, sg, packed_w, bg, grads, name, tm=256):
    S = proj.shape[0]
    row, gate, full, packed = _merge_specs(tm)
    tn_dims = (((0,), (0,)), ((), ()))
    nt_dims = (((1,), (1,)), ((), ()))

    def body(dm_ref, g0, g1, g2, ya_ref, o_ref, sg_ref, c_ref, bg_ref, _, dg_ref, dya_ref, do_ref, dsg_ref, dc_ref, dbg_ref, acc):
        i = pl.program_id(0)

        @pl.when(i == 0)
        def _():
            acc[...] = jnp.zeros_like(acc)
            dbg_ref[...] = jnp.zeros_like(dbg_ref)

        yav, ov, sgv = ya_ref[...], o_ref[...], sg_ref[...]
        o0, o1 = ov[:, :256], ov[:, 256:]
        dya = jnp.zeros((tm, POOL_W), F32)
        do0 = jnp.zeros((tm, 256), F32)
        do1 = jnp.zeros((tm, 256), F32)
        dsg = jnp.zeros((tm, SGU_W), F32)
        for j in range(4):
            cols = slice(256 * j, 256 * (j + 1))
            wa, wb0, wb1, wc = _branch_shards(c_ref, j)
            y = (jnp.dot(yav, wa, preferred_element_type=F32),
                 jnp.dot(o0, wb0, preferred_element_type=F32) + jnp.dot(o1, wb1, preferred_element_type=F32),
                 jnp.dot(sgv, wc, preferred_element_type=F32))
            dmv = dm_ref[:, cols]
            dy = []
            for b, g_ref in enumerate((g0, g1, g2)):
                bcols = slice(b * D + 256 * j, b * D + 256 * (j + 1))
                gt = _sigmoid(g_ref[:, cols] + bg_ref[:, bcols])
                dgp = dmv * y[b] * gt * (1.0 - gt)
                dg_ref[:, bcols] = dgp.astype(BF16)
                dbg_ref[:, bcols] += jnp.sum(dgp, axis=0, keepdims=True)
                dy.append((dmv * gt).astype(BF16))
            dya = dya + lax.dot_general(dy[0], wa, nt_dims, preferred_element_type=F32)
            do0 = do0 + lax.dot_general(dy[1], wb0, nt_dims, preferred_element_type=F32)
            do1 = do1 + lax.dot_general(dy[1], wb1, nt_dims, preferred_element_type=F32)
            dsg = dsg + lax.dot_general(dy[2], wc, nt_dims, preferred_element_type=F32)
            acc[j, :, 0:256] += lax.dot_general(yav, dy[0], tn_dims, preferred_element_type=F32)
            acc[j, :, 256:512] += lax.dot_general(o0, dy[1], tn_dims, preferred_element_type=F32)
            acc[j, :, 512:768] += lax.dot_general(o1, dy[1], tn_dims, preferred_element_type=F32)
            acc[j, :, 768:1024] += lax.dot_general(sgv, dy[2], tn_dims, preferred_element_type=F32)
        dya_ref[...] = dya
        do_ref[:, :256] = do0
        do_ref[:, 256:] = do1
        dsg_ref[...] = dsg

        @pl.when(i == pl.num_programs(0) - 1)
        def _():
            dc_ref[...] = acc[...].astype(dc_ref.dtype)

    return pl.pallas_call(
        body, out_shape=(jax.ShapeDtypeStruct((S, 3 * D), BF16), jax.ShapeDtypeStruct((S, POOL_W), F32),
                         jax.ShapeDtypeStruct((S, FOX_W), F32), jax.ShapeDtypeStruct((S, SGU_W), F32),
                         jax.ShapeDtypeStruct(grads.shape, grads.dtype), jax.ShapeDtypeStruct((1, 3 * D), F32)),
        grid=(S // tm,),
        in_specs=[row(D), gate(0), gate(1), gate(2), row(POOL_W), row(FOX_W), row(SGU_W), packed, full(1, 3 * D), ANY],
        out_specs=(row(3 * D), row(POOL_W), row(FOX_W), row(SGU_W), packed, full(1, 3 * D)),
        scratch_shapes=[pltpu.VMEM((4, 256, PACK_COLS), F32)], input_output_aliases={9: 4},
        compiler_params=_cp(("arbitrary",)), name=name)(dm, proj, proj, proj, ya, o, sg, packed_w, bg, grads)


def _xattn_probs(qh, kh):
    s = lax.dot_general(qh, kh, (((1,), (1,)), ((), ())), preferred_element_type=F32) * X_SCALE
    p = jnp.exp(s - jnp.max(s, axis=-1, keepdims=True))
    return p / jnp.sum(p, axis=-1, keepdims=True)


def _xattn_fwd(xq, kv, name, tq=256):
    S = xq.shape[0]
    M = kv.shape[0]

    def body(q_ref, k_ref, v_ref, o_ref):
        for h in range(XH):
            sl = slice(h * XHD, (h + 1) * XHD)
            p = _xattn_probs(q_ref[:, sl], k_ref[:, sl])
            o_ref[:, sl] = jnp.dot(p.astype(BF16), v_ref[:, sl], preferred_element_type=F32).astype(BF16)

    return pl.pallas_call(
        body, out_shape=jax.ShapeDtypeStruct((S, D), BF16), grid=(S // tq,),
        in_specs=[pl.BlockSpec((tq, D), lambda i: (i, 0)), pl.BlockSpec((M, D), lambda i: (0, 0)),
                  pl.BlockSpec((M, D), lambda i: (0, 1))],
        out_specs=pl.BlockSpec((tq, D), lambda i: (i, 0)), compiler_params=_cp(("parallel",)), name=name)(xq, kv, kv)


def _xattn_bwd(xq, kv, do, name, tq=256):
    S = xq.shape[0]
    M = kv.shape[0]

    def body(q_ref, k_ref, v_ref, do_ref, dq_ref, dkv_ref, dk_acc, dv_acc):
        i = pl.program_id(0)

        @pl.when(i == 0)
        def _():
            dk_acc[...] = jnp.zeros_like(dk_acc)
            dv_acc[...] = jnp.zeros_like(dv_acc)

        for h in range(XH):
            sl = slice(h * XHD, (h + 1) * XHD)
            qh, kh, vh, doh = q_ref[:, sl], k_ref[:, sl], v_ref[:, sl], do_ref[:, sl]
            p = _xattn_probs(qh, kh)
            dp = lax.dot_general(doh, vh, (((1,), (1,)), ((), ())), preferred_element_type=F32)
            ds = p * (dp - jnp.sum(p * dp, axis=-1, keepdims=True))
            dsb = (ds * X_SCALE).astype(BF16)
            dq_ref[:, sl] = jnp.dot(dsb, kh, preferred_element_type=F32).astype(BF16)
            dk_acc[:, sl] += lax.dot_general(dsb, qh, (((0,), (0,)), ((), ())), preferred_element_type=F32)
            dv_acc[:, sl] += lax.dot_general(p.astype(BF16), doh, (((0,), (0,)), ((), ())), preferred_element_type=F32)

        @pl.when(i == pl.num_programs(0) - 1)
        def _():
            dkv_ref[:, :D] = dk_acc[...].astype(BF16)
            dkv_ref[:, D:] = dv_acc[...].astype(BF16)

    return pl.pallas_call(
        body, out_shape=(jax.ShapeDtypeStruct((S, D), BF16), jax.ShapeDtypeStruct((M, 2 * D), BF16)), grid=(S // tq,),
        in_specs=[pl.BlockSpec((tq, D), lambda i: (i, 0)), pl.BlockSpec((M, D), lambda i: (0, 0)),
                  pl.BlockSpec((M, D), lambda i: (0, 1)), pl.BlockSpec((tq, D), lambda i: (i, 0))],
        out_specs=(pl.BlockSpec((tq, D), lambda i: (i, 0)), pl.BlockSpec((M, 2 * D), lambda i: (0, 0))),
        scratch_shapes=[pltpu.VMEM((M, D), F32), pltpu.VMEM((M, D), F32)],
        compiler_params=_cp(("arbitrary",)), name=name)(xq, kv, kv, do)


def _adam_math(gv, wv, mv, vv):
    c1 = 1.0 - ADAM_B1 ** ADAM_STEP
    c2 = 1.0 - ADAM_B2 ** ADAM_STEP
    nm = ADAM_B1 * mv + (1.0 - ADAM_B1) * gv
    nv = ADAM_B2 * vv + (1.0 - ADAM_B2) * (gv * gv)
    return -ADAM_LR * ((nm / c1) / (jnp.sqrt(nv / c2) + ADAM_EPS) + ADAM_WD * wv), nm, nv


def _adamw(g, w, m, v, name, block=None):
    if block is None:
        block = (1, 256 if g.shape[1] % 256 == 0 else g.shape[1], g.shape[2])
    grid = tuple(s // b for s, b in zip(g.shape, block))

    def body(g_ref, w_ref, m_ref, v_ref, d_ref, nm_ref, nv_ref):
        d_ref[...], nm_ref[...], nv_ref[...] = _adam_math(g_ref[...], w_ref[...], m_ref[...], v_ref[...])

    blk = pl.BlockSpec(block, lambda a, b, c: (a, b, c))
    return pl.pallas_call(
        body, out_shape=(jax.ShapeDtypeStruct(g.shape, F32),) * 3, grid=grid,
        in_specs=[blk] * 4, out_specs=(blk,) * 3, compiler_params=_cp(("parallel",) * 3), name=name)(g, w, m, v)


def _adamw_packed(red, w, m, v, g_index, name, tr=256):
    L, r, c = w.shape
    tr = min(tr, r)

    def body(g0_ref, g1_ref, w_ref, m_ref, v_ref, g_ref, d_ref, nm_ref, nv_ref):
        gv = jnp.where(pl.program_id(0) == 0, g0_ref[...], g1_ref[...])
        g_ref[0] = gv
        d_ref[0], nm_ref[0], nv_ref[0] = _adam_math(gv, w_ref[0], m_ref[0], v_ref[0])

    gblk = pl.BlockSpec((tr, c), lambda l, i: g_index(i))
    blk = pl.BlockSpec((1, tr, c), lambda l, i: (l, i, 0))
    return pl.pallas_call(
        body, out_shape=(jax.ShapeDtypeStruct(w.shape, F32),) * 4, grid=(L, r // tr),
        in_specs=[gblk, gblk, blk, blk, blk], out_specs=(blk,) * 4,
        compiler_params=_cp(("parallel", "parallel")), name=name)(red[0], red[1], w, m, v)


def _row_tile(R):
    return next((t for t in (512, 496, 384, 256) if R % t == 0), R)


def _sum_slots(a, out_dtype, name):
    n, R, C = a.shape
    tr = _row_tile(R)

    def body(a_ref, o_ref):
        acc = a_ref[0].astype(F32)
        for k in range(1, n):
            acc = acc + a_ref[k].astype(F32)
        o_ref[...] = acc.astype(out_dtype)

    return pl.pallas_call(
        body, out_shape=jax.ShapeDtypeStruct((R, C), out_dtype), grid=(R // tr,),
        in_specs=[pl.BlockSpec((n, tr, C), lambda i: (0, i, 0))], out_specs=pl.BlockSpec((tr, C), lambda i: (i, 0)),
        compiler_params=_cp(("parallel",)), name=name)(a)


def _add_pair(a, b, name):
    n, R, C = a.shape
    tr = _row_tile(R)

    def body(a_ref, b_ref, o_ref):
        o_ref[...] = (a_ref[...].astype(F32) + b_ref[...].astype(F32)).astype(BF16)

    blk = pl.BlockSpec((1, tr, C), lambda k, i: (k, i, 0))
    return pl.pallas_call(
        body, out_shape=jax.ShapeDtypeStruct(a.shape, BF16), grid=(n, R // tr), in_specs=[blk, blk], out_specs=blk,
        compiler_params=_cp(("parallel", "parallel")), name=name)(a, b)


LANDING = pl.BlockSpec(memory_space=pltpu.VMEM)


def _landing_params(shape, dtype):
    return pltpu.CompilerParams(vmem_limit_bytes=math.prod(shape) * jnp.dtype(dtype).itemsize + 4 * 1024 * 1024)


def _place():
    return lax.axis_index("x"), lax.axis_index("y"), lax.axis_index("c")


def _other_chips(x, y):
    return [(1 - x, y), (x, 1 - y), (1 - x, 1 - y)]


def _row_chunks(rows, want, align=16):
    n = want
    while n > 1 and rows % (n * align):
        n -= 1
    return n


def _gather_weights(shard, name, nch=5):
    R, C = shard.shape
    half = R // 2
    nch = _row_chunks(half, nch)
    cr = half // nch

    def body(s_ref, o_ref, send_sems, recv_sems, local_sem):
        x, y, c = _place()
        j = 2 * x + y
        mine0 = c * half
        theirs0 = (1 - c) * half

        def rows(jj, r0, q):
            return o_ref.at[jj, pl.ds(pl.multiple_of(r0 + q * cr, 16), cr), :]

        def copy(k, src, dst, to):
            return pltpu.make_async_remote_copy(src_ref=src, dst_ref=dst, send_sem=send_sems.at[k], recv_sem=recv_sems.at[k],
                                                device_id=to, device_id_type=MESH)

        own = pltpu.make_async_copy(s_ref, o_ref.at[j], local_sem)
        own.start()
        chips = _other_chips(x, y)
        first = []
        for q in range(nch):
            for k, (px, py) in enumerate(chips):
                src = s_ref.at[pl.ds(pl.multiple_of(mine0 + q * cr, 16), cr), :]
                first.append(copy(k * nch + q, src, rows(j, mine0, q), (px, py, c)))
        for cp in first:
            cp.start()
        passed = []
        for q in range(nch):
            for k, (px, py) in enumerate(chips):
                jj = 2 * px + py
                copy(k * nch + q, rows(jj, mine0, q), rows(jj, mine0, q), (px, py, c)).wait_recv()
                fw = copy((3 + k) * nch + q, rows(jj, mine0, q), rows(jj, mine0, q), (x, y, 1 - c))
                fw.start()
                passed.append(fw)
        for q in range(nch):
            for k, (px, py) in enumerate(chips):
                jj = 2 * px + py
                copy((3 + k) * nch + q, rows(jj, theirs0, q), rows(jj, theirs0, q), (x, y, 1 - c)).wait_recv()
        for cp in first + passed:
            cp.wait_send()
        own.wait()

    return pl.pallas_call(
        body, out_shape=jax.ShapeDtypeStruct((4, R, C), shard.dtype), in_specs=[ANY], out_specs=LANDING,
        scratch_shapes=[pltpu.SemaphoreType.DMA((6 * nch,)), pltpu.SemaphoreType.DMA((6 * nch,)), pltpu.SemaphoreType.DMA],
        compiler_params=_landing_params((4, R, C), shard.dtype), name=name)(shard)


def _pair_split(g, name, nch=5):
    n, R, C = g.shape
    half = R // 2
    nch = _row_chunks(half, nch)
    cr = half // nch

    def body(g_ref, own_ref, got_ref, send_sems, recv_sems, local_sem):
        x, y, c = _place()
        mine0 = pl.multiple_of(c * half, 16)
        theirs0 = (1 - c) * half
        keep = pltpu.make_async_copy(g_ref.at[:, pl.ds(mine0, half), :], own_ref, local_sem)
        keep.start()
        cps = []
        for s in range(n):
            for q in range(nch):
                src = g_ref.at[s, pl.ds(pl.multiple_of(theirs0 + q * cr, 16), cr), :]
                cps.append(pltpu.make_async_remote_copy(
                    src_ref=src, dst_ref=got_ref.at[s, pl.ds(q * cr, cr), :], send_sem=send_sems.at[s * nch + q],
                    recv_sem=recv_sems.at[s * nch + q], device_id=(x, y, 1 - c), device_id_type=MESH))
        for cp in cps:
            cp.start()
        for cp in cps:
            cp.wait()
        keep.wait()

    sh = jax.ShapeDtypeStruct((n, half, C), g.dtype)
    return pl.pallas_call(
        body, out_shape=(sh, sh), in_specs=[ANY], out_specs=(ANY, LANDING),
        scratch_shapes=[pltpu.SemaphoreType.DMA((n * nch,)), pltpu.SemaphoreType.DMA((n * nch,)), pltpu.SemaphoreType.DMA],
        compiler_params=_landing_params(sh.shape, g.dtype), name=name)(g)


def _chip_all_to_all(p, name, nch=5):
    R = p.shape[1]
    nch = _row_chunks(R, nch)
    cr = R // nch

    def body(p_ref, o_ref, send_sems, recv_sems, local_sem):
        x, y, c = _place()
        j = 2 * x + y
        own = pltpu.make_async_copy(p_ref.at[j], o_ref.at[j], local_sem)
        own.start()
        cps = []
        for q in range(nch):
            for k, (px, py) in enumerate(_other_chips(x, y)):
                cps.append(pltpu.make_async_remote_copy(
                    src_ref=p_ref.at[2 * px + py, pl.ds(q * cr, cr), :], dst_ref=o_ref.at[j, pl.ds(q * cr, cr), :],
                    send_sem=send_sems.at[k * nch + q], recv_sem=recv_sems.at[k * nch + q], device_id=(px, py, c),
                    device_id_type=MESH))
        for cp in cps:
            cp.start()
        for cp in cps:
            cp.wait()
        own.wait()

    return pl.pallas_call(
        body, out_shape=jax.ShapeDtypeStruct(p.shape, p.dtype), in_specs=[ANY], out_specs=LANDING,
        scratch_shapes=[pltpu.SemaphoreType.DMA((3 * nch,)), pltpu.SemaphoreType.DMA((3 * nch,)), pltpu.SemaphoreType.DMA],
        compiler_params=_landing_params(p.shape, p.dtype), name=name)(p)


def _pair_gather(t, name, nch=10):
    R = t.shape[0]
    nch = _row_chunks(R, nch, 8)
    cr = R // nch

    def body(t_ref, o_ref, send_sems, recv_sems, local_sem):
        x, y, c = _place()
        own = pltpu.make_async_copy(t_ref, o_ref.at[c], local_sem)
        own.start()
        cps = [pltpu.make_async_remote_copy(src_ref=t_ref.at[pl.ds(q * cr, cr), :], dst_ref=o_ref.at[c, pl.ds(q * cr, cr), :],
                                            send_sem=send_sems.at[q], recv_sem=recv_sems.at[q], device_id=(x, y, 1 - c),
                                            device_id_type=MESH) for q in range(nch)]
        for cp in cps:
            cp.start()
        for cp in cps:
            cp.wait()
        own.wait()

    return pl.pallas_call(
        body, out_shape=jax.ShapeDtypeStruct((2,) + t.shape, t.dtype), in_specs=[ANY], out_specs=LANDING,
        scratch_shapes=[pltpu.SemaphoreType.DMA((nch,)), pltpu.SemaphoreType.DMA((nch,)), pltpu.SemaphoreType.DMA],
        compiler_params=_landing_params((2,) + t.shape, t.dtype), name=name)(t)


HBM = pl.BlockSpec(memory_space=pltpu.HBM)
SEM = pl.BlockSpec(memory_space=pltpu.SEMAPHORE)
SPLIT_COPY = pltpu.CompilerParams(has_side_effects=pltpu.SideEffectType.DATAFLOW_SIDE_EFFECTING)


def _split_exchange(src, rows, src_of, tag, nch=5):
    C = src.shape[-1]
    nch = _row_chunks(rows, nch)
    cr = rows // nch
    n = 3 * nch
    land_shape = (4, rows, C)

    def copies(src_ref, land_ref, send_sems, recv_sems):
        x, y, c = _place()
        j = 2 * x + y
        out = []
        for q in range(nch):
            for k, (px, py) in enumerate(_other_chips(x, y)):
                out.append(pltpu.make_async_remote_copy(
                    src_ref=src_of(src_ref, px, py, c, q * cr, cr), dst_ref=land_ref.at[j, pl.ds(q * cr, cr), :],
                    send_sem=send_sems.at[k * nch + q], recv_sem=recv_sems.at[k * nch + q], device_id=(px, py, c),
                    device_id_type=MESH))
        return out

    def start(src_ref, land_ref, send_sems, recv_sems, src_thru, land_thru, token):
        for cp in copies(src_ref, land_ref, send_sems, recv_sems):
            cp.start()
        token[...] = jnp.zeros_like(token)

    send_sems, recv_sems, src_thru, land_thru, token = pl.pallas_call(
        start, name=f"{tag}_start",
        out_shape=(pltpu.SemaphoreType.DMA((n,)), pltpu.SemaphoreType.DMA((n,)), pltpu.HBM(src.shape, src.dtype),
                   pltpu.HBM(land_shape, src.dtype), jax.ShapeDtypeStruct((8, 128), F32)),
        in_specs=(HBM, HBM), out_specs=(SEM, SEM, HBM, HBM, pl.BlockSpec(memory_space=pltpu.VMEM)),
        input_output_aliases={0: 2, 1: 3}, compiler_params=SPLIT_COPY)(
            pltpu.with_memory_space_constraint(src, pltpu.HBM),
            pltpu.with_memory_space_constraint(lax.empty(land_shape, src.dtype), pltpu.HBM))

    def finish(after):
        def wait(src_ref, land_ref, send_sems, recv_sems, after_ref, src_dead, got_ref):
            for cp in copies(src_ref, land_ref, send_sems, recv_sems):
                cp.wait_send()
                cp.wait_recv()

        return pl.pallas_call(
            wait, name=f"{tag}_wait", out_shape=(pltpu.HBM(src.shape, src.dtype), pltpu.HBM(land_shape, src.dtype)),
            in_specs=(HBM, HBM, SEM, SEM, ANY), out_specs=(HBM, HBM), input_output_aliases={0: 0, 1: 1},
            compiler_params=SPLIT_COPY)(src_thru, land_thru, send_sems, recv_sems, after)[1]

    return token, finish


def _gather_finish(shard, land, name, nch=5):
    R, C = shard.shape
    half = R // 2
    nch = _row_chunks(half, nch)
    cr = half // nch

    def body(s_ref, l_ref, o_ref, send_sems, recv_sems, local_sems):
        x, y, c = _place()
        j = 2 * x + y
        mine0 = c * half
        local = [pltpu.make_async_copy(s_ref, o_ref.at[j], local_sems.at[0])]
        remote = []
        for k, (px, py) in enumerate(_other_chips(x, y)):
            jj = 2 * px + py
            local.append(pltpu.make_async_copy(l_ref.at[jj], o_ref.at[jj, pl.ds(pl.multiple_of(mine0, 16), half), :],
                                               local_sems.at[1 + k]))
            for q in range(nch):
                remote.append(pltpu.make_async_remote_copy(
                    src_ref=l_ref.at[jj, pl.ds(q * cr, cr), :],
                    dst_ref=o_ref.at[jj, pl.ds(pl.multiple_of(mine0 + q * cr, 16), cr), :], send_sem=send_sems.at[k * nch + q],
                    recv_sem=recv_sems.at[k * nch + q], device_id=(x, y, 1 - c), device_id_type=MESH))
        for cp in local + remote:
            cp.start()
        for cp in remote + local:
            cp.wait()

    return pl.pallas_call(
        body, out_shape=jax.ShapeDtypeStruct((4, R, C), shard.dtype), in_specs=[ANY, ANY], out_specs=LANDING,
        scratch_shapes=[pltpu.SemaphoreType.DMA((3 * nch,)), pltpu.SemaphoreType.DMA((3 * nch,)), pltpu.SemaphoreType.DMA((4,))],
        compiler_params=_landing_params((4, R, C), shard.dtype), name=name)(shard, land)


def _sum_slots_own(land, own, name):
    n, R, C = land.shape
    tr = _row_tile(R)
    me = (2 * lax.axis_index("x") + lax.axis_index("y")).astype(jnp.int32).reshape(1)

    def body(me_ref, land_ref, own_ref, o_ref):
        acc = None
        for k in range(n):
            v = jnp.where(me_ref[0] == k, own_ref[...], land_ref[k]).astype(F32)
            acc = v if acc is None else acc + v
        o_ref[...] = acc

    return pl.pallas_call(
        body, out_shape=jax.ShapeDtypeStruct((R, C), F32),
        grid_spec=pltpu.PrefetchScalarGridSpec(
            num_scalar_prefetch=1, grid=(R // tr,),
            in_specs=[pl.BlockSpec((n, tr, C), lambda i, me: (0, i, 0)), pl.BlockSpec((None, tr, C), lambda i, me: (me[0], i, 0))],
            out_specs=pl.BlockSpec((tr, C), lambda i, me: (i, 0))),
        compiler_params=_cp(("parallel",)), name=name)(me, land, own)


def _reduce_begin(g, tag):
    own, got = _pair_split(g, f"rs_pair_{tag}")
    p = _add_pair(own, got, f"rs_add_{tag}")
    token, finish = _split_exchange(p, p.shape[1], lambda ref, px, py, c, r0, cr: ref.at[2 * px + py, pl.ds(r0, cr), :],
                                    f"rs_a2a_{tag}")
    return (p, finish, g.shape, tag), token


def _reduce_end(state, after):
    p, finish, shape, tag = state
    t = _sum_slots_own(finish(after), p, f"rs_sum_{tag}")
    return _pair_gather(t, f"rs_join_{tag}").reshape(shape[1], shape[2])


def _reduce_scatter(g, tag):
    own, got = _pair_split(g, f"rs_pair_{tag}")
    p = _add_pair(own, got, f"rs_add_{tag}")
    t = _sum_slots(_chip_all_to_all(p, f"rs_a2a_{tag}"), F32, f"rs_sum_{tag}")
    return _pair_gather(t, f"rs_join_{tag}").reshape(g.shape[1], g.shape[2])


def _all_reduce_small(v, tag):
    pair = _pair_gather(v, f"ar_pair_{tag}")
    p = _sum_slots(pair, F32, f"ar_add_{tag}")
    q = _chip_all_to_all(jnp.broadcast_to(p[None], (4,) + p.shape), f"ar_a2a_{tag}")
    return _sum_slots(q, F32, f"ar_sum_{tag}")


def _gather_begin(shard, tag):
    half = shard.shape[0] // 2
    token, finish = _split_exchange(
        shard, half, lambda ref, px, py, c, r0, cr: ref.at[pl.ds(pl.multiple_of(c * half + r0, 16), cr), :], f"gather_{tag}")
    return (shard, finish, tag), token


def _gather_end(state, after):
    shard, finish, tag = state
    return _gather_finish(shard, finish(after), f"gather_{tag}_finish")


R_BRANCH, R_OUT, R_WIN, ROWS_A = 0, 256, 512, 1888
R_FF1, R_FF2, R_XKV, R_XQ, R_XO, ROWS_B = 0, 1024, 2048, 2560, 2816, 3072
WIN_ROWS = N_IN // 4


def _w_in_t(a):
    return jnp.transpose(a, (2, 0, 1))


def _pack_shard(w, l):
    xkv, wb = w['w_xkv'][l], w['w_branch_b'][l]
    a = [jnp.concatenate([w['w_branch_a'][l], wb[:256], wb[256:], w['w_branch_c'][l]], axis=1), w['w_out'][l],
         jnp.pad(_w_in_t(w['w_in'])[:, l, :], ((0, ROWS_A - R_WIN - WIN_ROWS), (0, 0)))]
    b = [w['w_ff1'][l], w['w_ff2'][l], jnp.concatenate([xkv[:512], xkv[512:]], axis=1), w['w_xq'][l], w['w_xo'][l]]
    return jnp.concatenate(a, axis=0).astype(BF16), jnp.concatenate(b, axis=0).astype(BF16)


def _w_in_rows(gathered):
    t = gathered[:, R_WIN:R_WIN + WIN_ROWS, :].reshape(N_IN, PACK_COLS)
    return jnp.concatenate([t[2312:5384], t[256:1792], t[1800:2312], t[0:256],
                            jnp.pad(t[1792:1800], ((0, NP - P_F - 8), (0, 0)))], axis=0)


def _w_in_grad_rows(grads, dwt):
    t = jnp.concatenate([dwt[P_A:P_A + 256], dwt[P_Q:P_Q + 1536], dwt[P_F:P_F + 8], dwt[P_C:P_C + 512], dwt[P_G:P_G + 3072]],
                        axis=0)
    return lax.dynamic_update_slice(grads, t.reshape(4, WIN_ROWS, PACK_COLS).astype(grads.dtype), (0, R_WIN, 0))


def _small_prep(sw, l):
    eye = jnp.eye(4, dtype=F32)
    bd = jnp.einsum('gh,gcd->gchd', eye, sw['pool_w'][l]).reshape(POOL_W, POOL_W).astype(BF16)
    tril = jnp.tril(jnp.ones((SGU_CHUNK, SGU_CHUNK), F32))
    wm = (sw['sgu_w'][l] * tril[None]).astype(BF16)
    return dict(
        g_mix=sw['norm_mix_g'][l][None], g_x=sw['norm_xattn_g'][l][None], g_mem=sw['norm_mem_g'][l][None],
        g_ffn=sw['norm_ffn_g'][l][None], bd=bd, pool_scale=sw['pool_scale'][l][None],
        bf=jnp.pad(sw['b_forget'][l], (0, FCOLS - 8))[None], sgu_g=sw['sgu_norm_g'][l][None], wm=wm,
        wmt=jnp.transpose(wm, (0, 2, 1)), sgu_bias=jnp.repeat(sw['sgu_b'][l].T, 64, axis=1), bg=sw['b_gate'][l][None])


def _rows4(r0):
    return dict(n=D, k=D, b_block=(4, 256, 512), b_index=lambda i, j, k: (0, r0 // 256, j))


def _rows_t(r0):
    return dict(tb=True, n=D, k=D, tn=256, b_block=(None, 256, PACK_COLS), b_index=lambda i, j, k: (j, r0 // 256, 0))


def _rows_grad(r0):
    return dict(ta=True, tm=256, tn=512, o_block=(None, 256, 512), o_index=lambda i, j, k: (i, r0 // 256, j))


def _add_to(r, e):
    return e + r


def _after(v, token):
    return v if token is None else v + token[0, 0]


def _layer_fwd(x, mem, GA, w_in_t, sp, l, token, second):
    t = f"l{l}"
    S = x.shape[0]
    h = _rms_fwd(x, _after(sp['g_mix'], token), f"rms_mix_{t}")
    proj = _mm(h, w_in_t, name=f"proj_{t}", out_dtype=F32, tb=True)
    d, ya = _pool_fwd(proj, sp['bd'], sp['pool_scale'], f"pool_fwd_{t}")
    fcum = _fgate_fwd(proj, sp['bf'], f"fgate_fwd_{t}")
    f8 = fcum[:, :8]
    fcol = f8.reshape(S, 4, 2).transpose(1, 0, 2)
    frow = f8.T.reshape(4, 2, S)
    qkv = proj[:, P_Q:P_Q + 3 * FOX_W].astype(BF16)
    o, o32, lse = _fox_fwd(qkv, fcol, frow, f"fox_fwd_{t}")
    sg = _sgu_fwd(proj, sp['sgu_g'], sp['wm'], sp['sgu_bias'], f"sgu_fwd_{t}")
    merged = _merge_fwd(proj, ya, o, sg, GA, sp['bg'], f"merge_fwd_{t}")
    x1 = _mm(merged, GA, name=f"out_{t}", out_dtype=F32, extra=x, epi=_add_to, **_rows4(R_OUT))
    GB, token = second(x1)
    hx = _rms_fwd(x1, _after(sp['g_x'], token), f"rms_x_{t}")
    hm = _rms_fwd(mem, sp['g_mem'], f"rms_mem_{t}")
    xq = _mm(hx, GB, name=f"xq_{t}", out_dtype=BF16, **_rows4(R_XQ))
    kv = _mm(hm, GB, name=f"xkv_{t}", out_dtype=BF16, n=2 * D, k=D, tn=512, tk=512, b_block=(None, 512, 512),
             b_index=lambda i, j, k: (j, R_XKV // 512, k))
    o2 = _xattn_fwd(xq, kv, f"xattn_fwd_{t}")
    x2 = _mm(o2, GB, name=f"xo_{t}", out_dtype=F32, extra=x1, epi=_add_to, **_rows4(R_XO))
    hf = _rms_fwd(x2, sp['g_ffn'], f"rms_ffn_{t}")
    z = _mm(hf, GB, name=f"ff1_{t}", out_dtype=F32, n=D_FF, k=D, tn=512, b_block=(None, 1024, 512),
            b_index=lambda i, j, k: (j // 2, R_FF1 // 1024, j % 2))
    x3 = _mm(z, GB, name=f"ff2_{t}", out_dtype=F32, a_fn=_relu2, extra=x2, epi=_add_to, n=D, k=D_FF, tk=1024,
             b_block=(None, 1024, 512), b_index=lambda i, j, k: (k, R_FF2 // 1024, j))
    saved = dict(x=x, h=h, proj=proj, d=d, ya=ya, fcol=fcol, frow=frow, qkv=qkv, o=o, o32=o32, lse=lse, sg=sg, merged=merged,
                 x1=x1, hx=hx, hm=hm, xq=xq, kv=kv, o2=o2, x2=x2, hf=hf, z=z, GA=GA, GB=GB, w_in_t=w_in_t)
    return x3, saved


def _layer_bwd(dx3, mem, sp, sv, l, token, early):
    t = f"l{l}"
    S = dx3.shape[0]
    GA, GB, w_in_t = sv['GA'], sv['GB'], sv['w_in_t']
    gs = {}
    dx3 = _after(dx3, token)
    gb = jnp.zeros((4, ROWS_B, PACK_COLS), BF16)
    dz = _mm(dx3, GB, name=f"d_a2_{t}", out_dtype=BF16, tb=True, n=D_FF, k=D, tn=512, b_block=(None, 512, PACK_COLS),
             b_index=lambda i, j, k: (j // 2, R_FF2 // 512 + j % 2, 0), extra=sv['z'],
             epi=lambda r, e: r * (2.0 * jnp.maximum(e, 0.0)))
    gb = _mm(sv['z'], dx3, name=f"dw_ff2_{t}", out_dtype=BF16, ta=True, a_fn=_relu2, into=gb, tm=1024, tn=512,
             o_block=(None, 1024, 512), o_index=lambda i, j, k: (i, R_FF2 // 1024, j))
    gb = _mm(sv['hf'], dz, name=f"dw_ff1_{t}", out_dtype=BF16, ta=True, into=gb, tm=1024, tn=512,
             o_block=(None, 1024, 512), o_index=lambda i, j, k: (j // 2, R_FF1 // 1024, j % 2))
    dhf = _mm(dz, GB, name=f"d_hf_{t}", out_dtype=F32, tb=True, n=D, k=D_FF, tn=512, tk=1024, b_block=(None, 512, PACK_COLS),
              b_index=lambda i, j, k: (k, R_FF1 // 512 + j, 0))
    dx2, gs['norm_ffn_g'] = _rms_bwd(dhf, sv['x2'], sp['g_ffn'], dx3, f"rms_ffn_bwd_{t}")
    do2 = _mm(dx2, GB, name=f"d_o2_{t}", out_dtype=BF16, **_rows_t(R_XO))
    gb = _mm(sv['o2'], dx2, name=f"dw_xo_{t}", out_dtype=BF16, into=gb, **_rows_grad(R_XO))
    dxq, dkv = _xattn_bwd(sv['xq'], sv['kv'], do2, f"xattn_bwd_{t}")
    gb = _mm(sv['hm'], dkv, name=f"dw_xkv_{t}", out_dtype=BF16, ta=True, into=gb, tm=512, tn=512,
             o_block=(None, 512, 512), o_index=lambda i, j, k: (j, R_XKV // 512, i))
    dhm = _mm(dkv, GB, name=f"d_hm_{t}", out_dtype=F32, tb=True, n=D, k=2 * D, tn=512, tk=512, b_block=(None, 512, 512),
              b_index=lambda i, j, k: (k, R_XKV // 512, j))
    gs['norm_mem_g'] = _rms_bwd(dhm, mem, sp['g_mem'], None, f"rms_mem_bwd_{t}")
    gb = _mm(sv['hx'], dxq, name=f"dw_xq_{t}", out_dtype=BF16, into=gb, **_rows_grad(R_XQ))
    token = early(gb)
    dhx = _mm(dxq, GB, name=f"d_hx_{t}", out_dtype=F32, **_rows_t(R_XQ))
    dx1, gs['norm_xattn_g'] = _rms_bwd(dhx, sv['x1'], _after(sp['g_x'], token), dx2, f"rms_x_bwd_{t}")
    ga = jnp.zeros((4, ROWS_A, PACK_COLS), BF16)
    ga = _mm(sv['merged'], dx1, name=f"dw_out_{t}", out_dtype=BF16, into=ga, **_rows_grad(R_OUT))
    dm = _mm(dx1, GA, name=f"d_merged_{t}", out_dtype=F32, **_rows_t(R_OUT))
    dg, dya, do, dsg, ga, gs['b_gate'] = _merge_bwd(dm, sv['proj'], sv['ya'], sv['o'], sv['sg'], GA, sp['bg'], ga, f"merge_bwd_{t}")
    dc, dws, dbias, gs['sgu_norm_g'] = _sgu_bwd(dsg, sv['proj'], sp['sgu_g'], sp['wm'], sp['wmt'], sp['sgu_bias'], f"sgu_bwd_{t}")
    tril = jnp.tril(jnp.ones((SGU_CHUNK, SGU_CHUNK), F32))
    gs['sgu_w'] = dws * tril[None]
    gs['sgu_b'] = dbias.reshape(SGU_CHUNK, 4, 64).sum(-1).T
    dq, dk, dv, dfrow, dfcol = _fox_bwd(sv['qkv'], sv['o32'], do, sv['lse'], sv['fcol'], sv['frow'], f"fox_bwd_{t}")
    dF = jnp.pad(dfrow.reshape(8, S).T + dfcol.transpose(1, 0, 2).reshape(S, 8), ((0, 0), (0, FCOLS - 8)))
    df, dbf = _fgate_bwd(dF, sv['proj'], sp['bf'], f"fgate_bwd_{t}")
    gs['b_forget'] = dbf[:, :8]
    da, dbd, gs['pool_scale'] = _pool_bwd(dya, sv['d'], sp['bd'], sp['pool_scale'], f"pool_bwd_{t}")
    gs['pool_w'] = jnp.stack([dbd[g * 64:(g + 1) * 64, g * 64:(g + 1) * 64] for g in range(4)])
    dproj = jnp.concatenate([dg, dq, dk, dv, dc, da, df], axis=1)
    dwt = _mm(dproj, sv['h'], name=f"dw_in_{t}", out_dtype=BF16, ta=True, tm=512, tn=1024)
    ga = _w_in_grad_rows(ga, dwt)
    dh = _mm(dproj, w_in_t, name=f"d_h_{t}", out_dtype=F32, tk=512)
    dx, gs['norm_mix_g'] = _rms_bwd(dh, sv['x'], sp['g_mix'], dx1, f"rms_mix_bwd_{t}")
    return dx, ga, gs


SMALL_ROWS = 1424
GRAD_BLOCKS = {
    'w_ff1': ('b', lambda i: (R_FF1 // 256 + i, 0)), 'w_ff2': ('b', lambda i: (R_FF2 // 256 + i, 0)),
    'w_xq': ('b', lambda i: (R_XQ // 256 + i, 0)), 'w_xo': ('b', lambda i: (R_XO // 256 + i, 0)),
    'w_xkv': ('b', lambda i: (R_XKV // 256 + i % 2, i // 2)), 'w_out': ('a', lambda i: (R_OUT // 256 + i, 0)),
    'w_branch_a': ('a', lambda i: (R_BRANCH // 256, 0)), 'w_branch_b': ('a', lambda i: (R_BRANCH // 256, 1 + i)),
    'w_branch_c': ('a', lambda i: (R_BRANCH // 256, 3)),
}


def _pack_small(parts):
    flat = jnp.concatenate([p.reshape(-1) for p in parts])
    return jnp.pad(flat, (0, SMALL_ROWS * 128 - flat.shape[0])).reshape(SMALL_ROWS, 128)


def _unpack_small(buf, shapes):
    flat, out, r = buf.reshape(-1), [], 0
    for s in shapes:
        n = math.prod(s)
        out.append(flat[r:r + n].reshape(s))
        r += n
    return out


def kernel(x, mem, norm_mix_g, w_in, b_forget, pool_w, pool_scale, sgu_norm_g, sgu_w, sgu_b, w_branch_a, w_branch_b, w_branch_c, b_gate, w_out, norm_xattn_g, norm_mem_g, w_xq, w_xkv, w_xo, norm_ffn_g, w_ff1, w_ff2, final_norm_g, loss_target, m_norm_mix_g, m_w_in, m_b_forget, m_pool_w, m_pool_scale, m_sgu_norm_g, m_sgu_w, m_sgu_b, m_w_branch_a, m_w_branch_b, m_w_branch_c, m_b_gate, m_w_out, m_norm_xattn_g, m_norm_mem_g, m_w_xq, m_w_xkv, m_w_xo, m_norm_ffn_g, m_w_ff1, m_w_ff2, m_final_norm_g, v_norm_mix_g, v_w_in, v_b_forget, v_pool_w, v_pool_scale, v_sgu_norm_g, v_sgu_w, v_sgu_b, v_w_branch_a, v_w_branch_b, v_w_branch_c, v_b_gate, v_w_out, v_norm_xattn_g, v_norm_mem_g, v_w_xq, v_w_xkv, v_w_xo, v_norm_ffn_g, v_w_ff1, v_w_ff2, v_final_norm_g):
    args = (norm_mix_g, w_in, b_forget, pool_w, pool_scale, sgu_norm_g, sgu_w, sgu_b, w_branch_a, w_branch_b, w_branch_c, b_gate,
            w_out, norm_xattn_g, norm_mem_g, w_xq, w_xkv, w_xo, norm_ffn_g, w_ff1, w_ff2, final_norm_g)
    margs = (m_norm_mix_g, m_w_in, m_b_forget, m_pool_w, m_pool_scale, m_sgu_norm_g, m_sgu_w, m_sgu_b, m_w_branch_a, m_w_branch_b,
             m_w_branch_c, m_b_gate, m_w_out, m_norm_xattn_g, m_norm_mem_g, m_w_xq, m_w_xkv, m_w_xo, m_norm_ffn_g, m_w_ff1, m_w_ff2,
             m_final_norm_g)
    vargs = (v_norm_mix_g, v_w_in, v_b_forget, v_pool_w, v_pool_scale, v_sgu_norm_g, v_sgu_w, v_sgu_b, v_w_branch_a, v_w_branch_b,
             v_w_branch_c, v_b_gate, v_w_out, v_norm_xattn_g, v_norm_mem_g, v_w_xq, v_w_xkv, v_w_xo, v_norm_ffn_g, v_w_ff1, v_w_ff2,
             v_final_norm_g)
    w = dict(zip(W_NAMES, args))
    mo = dict(zip(W_NAMES, margs))
    vo = dict(zip(W_NAMES, vargs))
    xs, mems, tgt = x[0], mem[0], loss_target[0]
    shards = [_pack_shard(w, l) for l in range(DEPTH)]
    preps = [_small_prep(w, l) for l in range(DEPTH)]

    GA = _gather_weights(shards[0][0], "gather_a_l0")
    pending_b, token = _gather_begin(shards[0][1], "b_l0")
    act, saved = xs, []
    for l in range(DEPTH):
        nxt = {}

        def second(x1, l=l, pending_b=pending_b, nxt=nxt):
            GB = _gather_end(pending_b, x1)
            if l + 1 == DEPTH:
                return GB, None
            nxt['a'], ta = _gather_begin(shards[l + 1][0], f"a_l{l + 1}")
            nxt['b'], tb = _gather_begin(shards[l + 1][1], f"b_l{l + 1}")
            return GB, ta + tb

        act, sv = _layer_fwd(act, mems, GA, _w_in_rows(GA), preps[l], l, token, second)
        saved.append(sv)
        if l + 1 < DEPTH:
            GA = _gather_end(nxt['a'], act)
            pending_b, token = nxt['b'], None
    loss_part, dact, d_final_g = _loss_head(act, w['final_norm_g'][None], tgt, "loss_head")

    red_a, red_b, small_g = [None] * DEPTH, [None] * DEPTH, [None] * DEPTH
    token, state_a = None, None
    for l in reversed(range(DEPTH)):
        early = {}

        def start_b(gb, l=l, early=early):
            early['state'], tok = _reduce_begin(gb, f"b_l{l}")
            return tok

        dact, ga, small_g[l] = _layer_bwd(dact, mems, preps[l], saved[l], l, token, start_b)
        if state_a is not None:
            red_a[l + 1] = _reduce_end(state_a, dact)
        red_b[l] = _reduce_end(early['state'], dact)
        if l > 0:
            state_a, token = _reduce_begin(ga, f"a_l{l}")
        else:
            red_a[l] = _reduce_scatter(ga, f"a_l{l}")
    grad_x = dact[None]

    per_layer = [n for n in SMALL_NAMES if n != 'final_norm_g']
    small_shapes = [w[n].shape for n in per_layer] + [(D,), (1,)]
    parts = [jnp.stack([small_g[l][n].reshape(w[n].shape[1:]) for l in range(DEPTH)]) for n in per_layer]
    small_red = _unpack_small(_all_reduce_small(_pack_small(parts + [d_final_g.reshape(D), loss_part.reshape(1)]), "small"), small_shapes)
    grads = dict(zip(per_layer + ['final_norm_g'], small_red[:-1]))
    loss = small_red[-1].reshape(())

    delta, new_m, new_v = {}, {}, {}
    for n, (buf, g_index) in GRAD_BLOCKS.items():
        grads[n], delta[n], new_m[n], new_v[n] = _adamw_packed(red_a if buf == 'a' else red_b, w[n], mo[n], vo[n], g_index, f"adamw_{n}")
    g_t = jnp.stack([r[R_WIN:R_WIN + WIN_ROWS] for r in red_a], axis=1)
    upd = _adamw(g_t, _w_in_t(w['w_in']), _w_in_t(mo['w_in']), _w_in_t(vo['w_in']), "adamw_w_in", block=(WIN_ROWS, DEPTH, 128))
    grads['w_in'], delta['w_in'], new_m['w_in'], new_v['w_in'] = [jnp.transpose(a, (1, 2, 0)) for a in (g_t,) + tuple(upd)]
    small_all = per_layer + ['final_norm_g']
    shapes_all = [w[n].shape for n in small_all]
    packed = [_pack_small([d[n] for n in small_all])[None] for d in (grads, w, mo, vo)]
    ds, ms, vs = _adamw(*packed, "adamw_small")
    for n, a, b, c in zip(small_all, _unpack_small(ds[0], shapes_all), _unpack_small(ms[0], shapes_all), _unpack_small(vs[0], shapes_all)):
        delta[n], new_m[n], new_v[n] = a, b, c

    return (loss, grad_x, *[grads[n] for n in W_NAMES], *[delta[n] for n in W_NAMES], *[new_m[n] for n in W_NAMES],
            *[new_v[n] for n in W_NAMES])
```

```python
import math

import jax
import jax.numpy as jnp
from jax import lax
from jax.experimental import pallas as pl
from jax.experimental.pallas import tpu as pltpu

F32 = jnp.float32
BF16 = jnp.bfloat16

D = 1024
DEPTH = 2
POOL_W = 256
FOX_W = 512
SGU_W = 256
SGU_CHUNK = 128
N_IN = 5384
P_G, P_Q, P_K, P_V, P_C, P_A, P_F = 0, 3072, 3584, 4096, 4608, 5120, 5376
NP = 5632
XH, XHD = 4, 256
D_FF = 4096
EPS = 1e-6
NEG = -1e30
FOX_SCALE = 64 ** -0.5
X_SCALE = 256 ** -0.5
GELU_K = math.sqrt(2.0 / math.pi)
GELU_C = 0.044715

ADAM_LR, ADAM_B1, ADAM_B2, ADAM_EPS, ADAM_WD, ADAM_STEP = 0.001, 0.9, 0.999, 1e-08, 0.01, 10

VMEM_LIMIT = 48 * 1024 * 1024
MESH = pl.DeviceIdType.MESH

IN_NAMES = ['x', 'mem', 'norm_mix_g', 'w_in', 'b_forget', 'pool_w', 'pool_scale', 'sgu_norm_g', 'sgu_w', 'sgu_b',
            'w_branch_a', 'w_branch_b', 'w_branch_c', 'b_gate', 'w_out', 'norm_xattn_g', 'norm_mem_g', 'w_xq',
            'w_xkv', 'w_xo', 'norm_ffn_g', 'w_ff1', 'w_ff2', 'final_norm_g']
W_NAMES = IN_NAMES[2:]
BIG_NAMES = ['w_in', 'w_branch_a', 'w_branch_b', 'w_branch_c', 'w_out', 'w_xq', 'w_xkv', 'w_xo', 'w_ff1', 'w_ff2']
SMALL_NAMES = [n for n in W_NAMES if n not in BIG_NAMES]
PACK_COLS = 1024


ANY = pl.BlockSpec(memory_space=pl.ANY)


def _cp(sem=None):
    return pltpu.CompilerParams(dimension_semantics=sem, vmem_limit_bytes=VMEM_LIMIT)


def _mm(a, b, *, name, out_dtype, ta=False, tb=False, tm=1024, tn=512, tk=1024, a_fn=None, extra=None, epi=None,
        n=None, k=None, b_block=None, b_index=None, into=None, o_block=None, o_index=None):
    M = a.shape[1] if ta else a.shape[0]
    K = k if k is not None else (a.shape[0] if ta else a.shape[1])
    N = n if n is not None else (b.shape[0] if tb else b.shape[1])
    tm, tn, tk = min(tm, M), min(tn, N), min(tk, K)
    assert M % tm == 0 and N % tn == 0 and K % tk == 0, (name, M, N, K)
    nk = K // tk
    a_spec = pl.BlockSpec((tk, tm), lambda i, j, k: (k, i)) if ta else pl.BlockSpec((tm, tk), lambda i, j, k: (i, k))
    if b_block is not None:
        b_spec = pl.BlockSpec(b_block, b_index)
    else:
        b_spec = pl.BlockSpec((tn, tk), lambda i, j, k: (j, k)) if tb else pl.BlockSpec((tk, tn), lambda i, j, k: (k, j))
    dn = (((0 if ta else 1,), (1 if tb else 0,)), ((), ()))
    tile = pl.BlockSpec((tm, tn), lambda i, j, k: (i, j))
    o_spec = pl.BlockSpec(o_block, o_index) if into is not None else tile
    in_specs = [a_spec, b_spec] + ([tile] if extra is not None else []) + ([ANY] if into is not None else [])
    n_in = len(in_specs)

    def body(*refs):
        a_ref, b_ref = refs[0], refs[1]
        e_ref = refs[2] if extra is not None else None
        o_ref, acc_ref = refs[n_in], refs[n_in + 1]
        kk = pl.program_id(2)

        @pl.when(kk == 0)
        def _():
            acc_ref[...] = jnp.zeros_like(acc_ref)

        av = a_ref[...]
        if a_fn is not None:
            av = a_fn(av)
        bv = b_ref[...]
        if bv.ndim == 3:
            bv = bv.reshape(-1, bv.shape[-1])
        acc_ref[...] += lax.dot_general(av.astype(BF16), bv.astype(BF16), dn, preferred_element_type=F32)

        @pl.when(kk == nk - 1)
        def _():
            r = acc_ref[...]
            if epi is not None:
                r = epi(r, e_ref[...])
            o_ref[...] = r.astype(o_ref.dtype)

    args = (a, b) + ((extra,) if extra is not None else ()) + ((into,) if into is not None else ())
    out_shape = jax.ShapeDtypeStruct(into.shape, into.dtype) if into is not None else jax.ShapeDtypeStruct((M, N), out_dtype)
    return pl.pallas_call(
        body, out_shape=out_shape, grid=(M // tm, N // tn, nk), in_specs=in_specs, out_specs=o_spec,
        scratch_shapes=[pltpu.VMEM((tm, tn), F32)], input_output_aliases={n_in - 1: 0} if into is not None else {},
        compiler_params=_cp(("parallel", "parallel", "arbitrary")), name=name)(*args)


def _relu2(z):
    r = jnp.maximum(z, 0.0)
    return r * r


def _rms_fwd(x, g, name, tr=256):
    R, n = x.shape
    tr = min(tr, R)

    def body(x_ref, g_ref, h_ref):
        xv = x_ref[...]
        rstd = lax.rsqrt(jnp.mean(xv * xv, axis=-1, keepdims=True) + EPS)
        h_ref[...] = (xv * rstd * g_ref[...]).astype(BF16)

    return pl.pallas_call(
        body, out_shape=jax.ShapeDtypeStruct((R, n), BF16), grid=(R // tr,),
        in_specs=[pl.BlockSpec((tr, n), lambda i: (i, 0)), pl.BlockSpec((1, n), lambda i: (0, 0))],
        out_specs=pl.BlockSpec((tr, n), lambda i: (i, 0)), compiler_params=_cp(("parallel",)), name=name)(x, g)


def _rms_bwd(dh, x, g, dres, name, tr=256):
    R, n = x.shape
    tr = min(tr, R)
    need_dx = dres is not None

    def body(*refs):
        if need_dx:
            dh_ref, x_ref, g_ref, r_ref, dx_ref, dg_ref = refs
        else:
            dh_ref, x_ref, g_ref, dg_ref = refs
        i = pl.program_id(0)
        xv = x_ref[...]
        dhv = dh_ref[...].astype(F32)
        rstd = lax.rsqrt(jnp.mean(xv * xv, axis=-1, keepdims=True) + EPS)
        xhat = xv * rstd

        @pl.when(i == 0)
        def _():
            dg_ref[...] = jnp.zeros_like(dg_ref)

        dg_ref[...] += jnp.sum(dhv * xhat, axis=0, keepdims=True)
        if need_dx:
            t = dhv * g_ref[...]
            dx_ref[...] = r_ref[...] + rstd * (t - xhat * jnp.mean(t * xhat, axis=-1, keepdims=True))

    row = pl.BlockSpec((tr, n), lambda i: (i, 0))
    vec = pl.BlockSpec((1, n), lambda i: (0, 0))
    if need_dx:
        return pl.pallas_call(
            body, out_shape=(jax.ShapeDtypeStruct((R, n), F32), jax.ShapeDtypeStruct((1, n), F32)), grid=(R // tr,),
            in_specs=[row, row, vec, row], out_specs=(row, vec), compiler_params=_cp(("arbitrary",)), name=name)(dh, x, g, dres)
    return pl.pallas_call(
        body, out_shape=jax.ShapeDtypeStruct((1, n), F32), grid=(R // tr,),
        in_specs=[row, row, vec], out_specs=vec, compiler_params=_cp(("arbitrary",)), name=name)(dh, x, g)


def _loss_head(x, g, tgt, name, tr=256):
    R, n = x.shape

    def body(x_ref, g_ref, t_ref, loss_ref, dx_ref, dg_ref):
        i = pl.program_id(0)
        xv = x_ref[...]
        gv = g_ref[...]
        rstd = lax.rsqrt(jnp.mean(xv * xv, axis=-1, keepdims=True) + EPS)
        xhat = xv * rstd
        e = xhat * gv - t_ref[...]

        @pl.when(i == 0)
        def _():
            loss_ref[...] = jnp.zeros_like(loss_ref)
            dg_ref[...] = jnp.zeros_like(dg_ref)

        loss_ref[...] += 0.5 * jnp.sum(jnp.sum(e * e, axis=-1, keepdims=True) / n, axis=0, keepdims=True)
        dy = e / n
        dg_ref[...] += jnp.sum(dy * xhat, axis=0, keepdims=True)
        t = dy * gv
        dx_ref[...] = rstd * (t - xhat * jnp.mean(t * xhat, axis=-1, keepdims=True))

    row = pl.BlockSpec((tr, n), lambda i: (i, 0))
    vec = pl.BlockSpec((1, n), lambda i: (0, 0))
    one = pl.BlockSpec((1, 1), lambda i: (0, 0))
    return pl.pallas_call(
        body, out_shape=(jax.ShapeDtypeStruct((1, 1), F32), jax.ShapeDtypeStruct((R, n), F32), jax.ShapeDtypeStruct((1, n), F32)),
        grid=(R // tr,), in_specs=[row, vec, row], out_specs=(one, row, vec),
        compiler_params=_cp(("arbitrary",)), name=name)(x, g, tgt)


def _pool_masks(S):
    row = lax.broadcasted_iota(jnp.int32, (S, POOL_W), 0)
    grp = lax.broadcasted_iota(jnp.int32, (S, POOL_W), 1) // 64
    win = jnp.where(grp == 0, 2, jnp.where(grp == 1, 4, jnp.where(grp == 2, 8, 16)))
    cnt = jnp.minimum(row + 1, win).astype(F32)
    return row, grp, cnt


def _by_group(grp, v0, v1, v2, v3):
    return jnp.where(grp == 0, v0, jnp.where(grp == 1, v1, jnp.where(grp == 2, v2, v3)))


def _pool_fwd(proj, bd, scale, name):
    S = proj.shape[0]

    def body(a_ref, bd_ref, sc_ref, d_ref, y_ref):
        a = a_ref[...]
        row, grp, cnt = _pool_masks(S)

        def back(v, k):
            return jnp.where(row >= k, pltpu.roll(v, k, 0), 0.0)

        s1 = a + back(a, 1)
        s2 = s1 + back(s1, 2)
        s3 = s2 + back(s2, 4)
        s4 = s3 + back(s3, 8)
        d = (_by_group(grp, s1, s2, s3, s4) / cnt - a).astype(BF16)
        d_ref[...] = d
        y_ref[...] = (jnp.dot(d, bd_ref[...], preferred_element_type=F32) * sc_ref[...]).astype(BF16)

    full = lambda r, c: pl.BlockSpec((r, c), lambda i: (0, 0))
    return pl.pallas_call(
        body, out_shape=(jax.ShapeDtypeStruct((S, POOL_W), BF16), jax.ShapeDtypeStruct((S, POOL_W), BF16)), grid=(1,),
        in_specs=[pl.BlockSpec((S, POOL_W), lambda i: (0, P_A // POOL_W)), full(POOL_W, POOL_W), full(1, POOL_W)],
        out_specs=(full(S, POOL_W), full(S, POOL_W)), compiler_params=_cp(("arbitrary",)), name=name)(proj, bd, scale)


def _pool_bwd(dya, d, bd, scale, name):
    S = dya.shape[0]

    def body(dy_ref, d_ref, bd_ref, sc_ref, da_ref, dbd_ref, dsc_ref):
        dy = dy_ref[...]
        dv = d_ref[...]
        bdv = bd_ref[...]
        row, grp, cnt = _pool_masks(S)
        yraw = jnp.dot(dv, bdv, preferred_element_type=F32)
        dsc_ref[...] = jnp.sum(dy * yraw, axis=0, keepdims=True)
        tb = (dy * sc_ref[...]).astype(BF16)
        dbd_ref[...] = lax.dot_general(dv, tb, (((0,), (0,)), ((), ())), preferred_element_type=F32)
        dd = lax.dot_general(tb, bdv, (((1,), (1,)), ((), ())), preferred_element_type=F32)
        e = dd / cnt

        def fwd(v, k):
            return jnp.where(row < S - k, pltpu.roll(v, S - k, 0), 0.0)

        r1 = e + fwd(e, 1)
        r2 = r1 + fwd(r1, 2)
        r3 = r2 + fwd(r2, 4)
        r4 = r3 + fwd(r3, 8)
        da_ref[...] = (_by_group(grp, r1, r2, r3, r4) - dd).astype(BF16)

    full = lambda r, c: pl.BlockSpec((r, c), lambda i: (0, 0))
    return pl.pallas_call(
        body, out_shape=(jax.ShapeDtypeStruct((S, POOL_W), BF16), jax.ShapeDtypeStruct((POOL_W, POOL_W), F32),
                         jax.ShapeDtypeStruct((1, POOL_W), F32)), grid=(1,),
        in_specs=[full(S, POOL_W), full(S, POOL_W), full(POOL_W, POOL_W), full(1, POOL_W)],
        out_specs=(full(S, POOL_W), full(POOL_W, POOL_W), full(1, POOL_W)),
        compiler_params=_cp(("arbitrary",)), name=name)(dya, d, bd, scale)


FCOLS = 128


def _log_sigmoid(z):
    return -(jnp.maximum(-z, 0.0) + jnp.log1p(jnp.exp(-jnp.abs(z))))


def _fgate_fwd(proj, bf, name):
    S = proj.shape[0]

    def body(f_ref, b_ref, o_ref):
        v = _log_sigmoid(f_ref[...] + b_ref[...])
        row = lax.broadcasted_iota(jnp.int32, (S, FCOLS), 0)
        k = 1
        while k < S:
            v = v + jnp.where(row >= k, pltpu.roll(v, k, 0), 0.0)
            k *= 2
        o_ref[...] = v

    return pl.pallas_call(
        body, out_shape=jax.ShapeDtypeStruct((S, FCOLS), F32), grid=(1,),
        in_specs=[pl.BlockSpec((S, FCOLS), lambda i: (0, P_F // FCOLS)), pl.BlockSpec((1, FCOLS), lambda i: (0, 0))],
        out_specs=pl.BlockSpec((S, FCOLS), lambda i: (0, 0)), compiler_params=_cp(("arbitrary",)), name=name)(proj, bf)


def _fgate_bwd(dF, proj, bf, name):
    S = proj.shape[0]

    def body(dF_ref, f_ref, b_ref, df_ref, db_ref):
        v = dF_ref[...]
        row = lax.broadcasted_iota(jnp.int32, (S, FCOLS), 0)
        k = 1
        while k < S:
            v = v + jnp.where(row < S - k, pltpu.roll(v, S - k, 0), 0.0)
            k *= 2
        z = f_ref[...] + b_ref[...]
        df = v * (1.0 / (1.0 + jnp.exp(z)))
        db_ref[...] = jnp.sum(df, axis=0, keepdims=True)
        df_ref[...] = jnp.concatenate([df, jnp.zeros_like(df)], axis=1).astype(BF16)

    return pl.pallas_call(
        body, out_shape=(jax.ShapeDtypeStruct((S, 2 * FCOLS), BF16), jax.ShapeDtypeStruct((1, FCOLS), F32)), grid=(1,),
        in_specs=[pl.BlockSpec((S, FCOLS), lambda i: (0, 0)), pl.BlockSpec((S, FCOLS), lambda i: (0, P_F // FCOLS)),
                  pl.BlockSpec((1, FCOLS), lambda i: (0, 0))],
        out_specs=(pl.BlockSpec((S, 2 * FCOLS), lambda i: (0, 0)), pl.BlockSpec((1, FCOLS), lambda i: (0, 0))),
        compiler_params=_cp(("arbitrary",)), name=name)(dF, proj, bf)


def _fox_scores(qe, kj, fq, fk, r0, c0, tq, tk, diagonal):
    s = lax.dot_general(qe, kj, (((1,), (1,)), ((), ())), preferred_element_type=F32) * FOX_SCALE
    s = s + (fq - fk)
    if not diagonal:
        return s
    rows = r0 + lax.broadcasted_iota(jnp.int32, (tq, tk), 0)
    cols = c0 + lax.broadcasted_iota(jnp.int32, (tq, tk), 1)
    return jnp.where(rows >= cols, s, NEG)


def _fox_fwd(qkv, fcol, frow, name, tq=256):
    S = qkv.shape[0]
    tk = tq

    def body(q_ref, k_ref, v_ref, fc_ref, fr_ref, o_ref, o32_ref, lse_ref):
        i = pl.program_id(1)
        r0 = i * tq
        q = q_ref[...]
        half = lax.broadcasted_iota(jnp.int32, (tq, 128), 1) // 64
        qs = [jnp.where(half == e, q, jnp.zeros_like(q)) for e in (0, 1)]
        fqs = [fc_ref[0, :, e:e + 1] for e in (0, 1)]

        def step(j, carry, diagonal=False):
            c0 = pl.multiple_of(j * tk, tk)
            kj = k_ref[pl.ds(c0, tk), :]
            vj = v_ref[pl.ds(c0, tk), :]
            out = []
            for e in (0, 1):
                m, l, acc = carry[e]
                s = _fox_scores(qs[e], kj, fqs[e], fr_ref[0, e:e + 1, pl.ds(c0, tk)], r0, c0, tq, tk, diagonal)
                m_new = jnp.maximum(m, jnp.max(s, axis=-1, keepdims=True))
                alpha = jnp.exp(m - m_new)
                p = jnp.exp(s - m_new)
                out.append((m_new, alpha * l + jnp.sum(p, axis=-1, keepdims=True),
                            alpha * acc + jnp.dot(p.astype(BF16), vj, preferred_element_type=F32)))
            return tuple(out)

        init = (jnp.full((tq, 1), NEG, F32), jnp.zeros((tq, 1), F32), jnp.zeros((tq, 128), F32))
        carry = lax.fori_loop(0, i, step, (init, init))
        carry = step(i, carry, diagonal=True)
        outs = []
        for e in (0, 1):
            m, l, acc = carry[e]
            outs.append(acc / l)
            lse_ref[0, :, e:e + 1] = m + jnp.log(l)
        o = jnp.where(half == 0, outs[0], outs[1])
        o32_ref[...] = o
        o_ref[...] = o.astype(BF16)

    tile = pl.BlockSpec((tq, 128), lambda h, i: (i, h))
    return pl.pallas_call(
        body, out_shape=(jax.ShapeDtypeStruct((S, FOX_W), BF16), jax.ShapeDtypeStruct((S, FOX_W), F32),
                         jax.ShapeDtypeStruct((4, S, 2), F32)), grid=(4, S // tq),
        in_specs=[tile, pl.BlockSpec((S, 128), lambda h, i: (0, 4 + h)), pl.BlockSpec((S, 128), lambda h, i: (0, 8 + h)),
                  pl.BlockSpec((1, tq, 2), lambda h, i: (h, i, 0)), pl.BlockSpec((1, 2, S), lambda h, i: (h, 0, 0))],
        out_specs=(tile, tile, pl.BlockSpec((1, tq, 2), lambda h, i: (h, i, 0))),
        compiler_params=_cp(("parallel", "parallel")), name=name)(qkv, qkv, qkv, fcol, frow)


def _fox_bwd(qkv, o32, do, lse, fcol, frow, name, tq=256):
    S = qkv.shape[0]
    tk = tq
    nq = S // tq

    def body(q_ref, k_ref, v_ref, o_ref, do_ref, lse_ref, fc_ref, fr_ref, dq_ref, dk_ref, dv_ref, dfr_ref, dfc_ref, dk_acc, dv_acc):
        dk_acc[...] = jnp.zeros_like(dk_acc)
        dv_acc[...] = jnp.zeros_like(dv_acc)
        dfr_ref[...] = jnp.zeros_like(dfr_ref)
        half = lax.broadcasted_iota(jnp.int32, (tq, 128), 1) // 64

        def q_block(i, _):
            r0 = pl.multiple_of(i * tq, tq)
            qi = q_ref[pl.ds(r0, tq), :]
            dob = do_ref[pl.ds(r0, tq), :].astype(BF16)
            row_dot = dob.astype(F32) * o_ref[pl.ds(r0, tq), :]
            qs = [jnp.where(half == e, qi, jnp.zeros_like(qi)) for e in (0, 1)]
            dos = [jnp.where(half == e, dob, jnp.zeros_like(dob)) for e in (0, 1)]
            deltas = [jnp.sum(jnp.where(half == e, row_dot, 0.0), axis=-1, keepdims=True) for e in (0, 1)]
            lses = [lse_ref[0, pl.ds(r0, tq), e:e + 1] for e in (0, 1)]
            fqs = [fc_ref[0, pl.ds(r0, tq), e:e + 1] for e in (0, 1)]

            def step(j, carry, diagonal=False):
                dqs, row_sums = carry
                c0 = pl.multiple_of(j * tk, tk)
                kj = k_ref[pl.ds(c0, tk), :]
                vj = v_ref[pl.ds(c0, tk), :]
                new_dq, new_rows, dkc, dvc = [], [], [], []
                for e in (0, 1):
                    s = _fox_scores(qs[e], kj, fqs[e], fr_ref[0, e:e + 1, pl.ds(c0, tk)], r0, c0, tq, tk, diagonal)
                    p = jnp.exp(s - lses[e])
                    dp = lax.dot_general(dos[e], vj, (((1,), (1,)), ((), ())), preferred_element_type=F32)
                    ds = p * (dp - deltas[e])
                    dfr_ref[0, e:e + 1, pl.ds(c0, tk)] -= jnp.sum(ds, axis=0, keepdims=True)
                    new_rows.append(row_sums[e] + jnp.sum(ds, axis=-1, keepdims=True))
                    dsb = (ds * FOX_SCALE).astype(BF16)
                    dkc.append(lax.dot_general(dsb, qi, (((0,), (0,)), ((), ())), preferred_element_type=F32))
                    dvc.append(lax.dot_general(p.astype(BF16), dob, (((0,), (0,)), ((), ())), preferred_element_type=F32))
                    new_dq.append(dqs[e] + jnp.dot(dsb, kj, preferred_element_type=F32))
                dk_acc[pl.ds(c0, tk), :] += jnp.where(half == 0, dkc[0], dkc[1])
                dv_acc[pl.ds(c0, tk), :] += jnp.where(half == 0, dvc[0], dvc[1])
                return tuple(new_dq), tuple(new_rows)

            zero, zero_col = jnp.zeros((tq, 128), F32), jnp.zeros((tq, 1), F32)
            carry = lax.fori_loop(0, i, step, ((zero, zero), (zero_col, zero_col)))
            dqs, row_sums = step(i, carry, diagonal=True)
            for e in (0, 1):
                dfc_ref[0, pl.ds(r0, tq), e:e + 1] = row_sums[e]
            dq_ref[pl.ds(r0, tq), :] = jnp.where(half == 0, dqs[0], dqs[1]).astype(BF16)
            return 0

        lax.fori_loop(0, nq, q_block, 0)
        dk_ref[...] = dk_acc[...].astype(BF16)
        dv_ref[...] = dv_acc[...].astype(BF16)

    col = lambda off: pl.BlockSpec((S, 128), lambda h: (0, off + h))
    hs2 = pl.BlockSpec((1, S, 2), lambda h: (h, 0, 0))
    h2s = pl.BlockSpec((1, 2, S), lambda h: (h, 0, 0))
    return pl.pallas_call(
        body, out_shape=(jax.ShapeDtypeStruct((S, FOX_W), BF16),) * 3 + (jax.ShapeDtypeStruct((4, 2, S), F32),
                                                                         jax.ShapeDtypeStruct((4, S, 2), F32)), grid=(4,),
        in_specs=[col(0), col(4), col(8), col(0), col(0), hs2, hs2, h2s],
        out_specs=(col(0), col(0), col(0), h2s, hs2),
        scratch_shapes=[pltpu.VMEM((S, 128), F32), pltpu.VMEM((S, 128), F32)],
        compiler_params=_cp(("parallel",)), name=name)(qkv, qkv, qkv, o32, do, lse, fcol, frow)


def _gelu(x):
    return 0.5 * x * (1.0 + jnp.tanh(GELU_K * (x + GELU_C * x * x * x)))


def _gelu_grad(x):
    th = jnp.tanh(GELU_K * (x + GELU_C * x * x * x))
    return 0.5 * (1.0 + th) + 0.5 * x * (1.0 - th * th) * GELU_K * (1.0 + 3.0 * GELU_C * x * x)


def _sgu_parts(c, gn, w_ref, bias):
    zc = _gelu(c)
    u, vv = zc[:, :SGU_W], zc[:, SGU_W:]
    rstd = lax.rsqrt(jnp.mean(vv * vv, axis=-1, keepdims=True) + EPS)
    vhat = vv * rstd
    vnb = (vhat * gn).astype(BF16)
    grp = lax.broadcasted_iota(jnp.int32, (SGU_CHUNK, SGU_W), 1) // 64
    mixed = bias
    for gi in range(4):
        mixed = mixed + jnp.where(grp == gi, jnp.dot(w_ref[gi], vnb, preferred_element_type=F32), 0.0)
    return u, rstd, vhat, vnb, grp, mixed


def _sgu_fwd(proj, gn, wm, bias, name):
    S = proj.shape[0]

    def body(c_ref, g_ref, w_ref, b_ref, o_ref):
        u, _, _, _, _, mixed = _sgu_parts(c_ref[...], g_ref[...], w_ref, b_ref[...])
        o_ref[...] = (u * mixed).astype(BF16)

    return pl.pallas_call(
        body, out_shape=jax.ShapeDtypeStruct((S, SGU_W), BF16), grid=(S // SGU_CHUNK,),
        in_specs=[pl.BlockSpec((SGU_CHUNK, 2 * SGU_W), lambda i: (i, P_C // (2 * SGU_W))),
                  pl.BlockSpec((1, SGU_W), lambda i: (0, 0)), pl.BlockSpec((4, SGU_CHUNK, SGU_CHUNK), lambda i: (0, 0, 0)),
                  pl.BlockSpec((SGU_CHUNK, SGU_W), lambda i: (0, 0))],
        out_specs=pl.BlockSpec((SGU_CHUNK, SGU_W), lambda i: (i, 0)),
        compiler_params=_cp(("parallel",)), name=name)(proj, gn, wm, bias)


def _sgu_bwd(dsg, proj, gn, wm, wmt, bias, name):
    S = proj.shape[0]

    def body(dsg_ref, c_ref, g_ref, w_ref, wt_ref, b_ref, dc_ref, dw_ref, db_ref, dg_ref):
        i = pl.program_id(0)

        @pl.when(i == 0)
        def _():
            dw_ref[...] = jnp.zeros_like(dw_ref)
            db_ref[...] = jnp.zeros_like(db_ref)
            dg_ref[...] = jnp.zeros_like(dg_ref)

        c = c_ref[...]
        gn_v = g_ref[...]
        u, rstd, vhat, vnb, grp, mixed = _sgu_parts(c, gn_v, w_ref, b_ref[...])
        dsg_v = dsg_ref[...]
        du = dsg_v * mixed
        dmix = dsg_v * u
        db_ref[...] += dmix
        dmb = dmix.astype(BF16)
        dvn = jnp.zeros((SGU_CHUNK, SGU_W), F32)
        for gi in range(4):
            dmg = jnp.where(grp == gi, dmb, jnp.zeros_like(dmb))
            dw_ref[gi] += lax.dot_general(dmg, vnb, (((1,), (1,)), ((), ())), preferred_element_type=F32)
            dvn = dvn + jnp.where(grp == gi, jnp.dot(wt_ref[gi], dmb, preferred_element_type=F32), 0.0)
        dg_ref[...] += jnp.sum(dvn * vhat, axis=0, keepdims=True)
        t = dvn * gn_v
        dvv = rstd * (t - vhat * jnp.mean(t * vhat, axis=-1, keepdims=True))
        dc_ref[...] = (jnp.concatenate([du, dvv], axis=1) * _gelu_grad(c)).astype(BF16)

    w_spec = pl.BlockSpec((4, SGU_CHUNK, SGU_CHUNK), lambda i: (0, 0, 0))
    tile = pl.BlockSpec((SGU_CHUNK, SGU_W), lambda i: (0, 0))
    vec = pl.BlockSpec((1, SGU_W), lambda i: (0, 0))
    return pl.pallas_call(
        body, out_shape=(jax.ShapeDtypeStruct((S, 2 * SGU_W), BF16), jax.ShapeDtypeStruct((4, SGU_CHUNK, SGU_CHUNK), F32),
                         jax.ShapeDtypeStruct((SGU_CHUNK, SGU_W), F32), jax.ShapeDtypeStruct((1, SGU_W), F32)),
        grid=(S // SGU_CHUNK,),
        in_specs=[pl.BlockSpec((SGU_CHUNK, SGU_W), lambda i: (i, 0)),
                  pl.BlockSpec((SGU_CHUNK, 2 * SGU_W), lambda i: (i, P_C // (2 * SGU_W))), vec, w_spec, w_spec, tile],
        out_specs=(pl.BlockSpec((SGU_CHUNK, 2 * SGU_W), lambda i: (i, 0)), w_spec, tile, vec),
        compiler_params=_cp(("arbitrary",)), name=name)(dsg, proj, gn, wm, wmt, bias)


def _sigmoid(z):
    return 1.0 / (1.0 + jnp.exp(-z))


def _merge_specs(tm):
    row = lambda n: pl.BlockSpec((tm, n), lambda i: (i, 0))
    gate = lambda b: pl.BlockSpec((tm, D), lambda i: (i, b))
    full = lambda r, c: pl.BlockSpec((r, c), lambda i: (0, 0))
    packed = pl.BlockSpec((4, 256, PACK_COLS), lambda i: (0, R_BRANCH // 256, 0))
    return row, gate, full, packed


def _branch_shards(c_ref, j):
    return c_ref[j, :, 0:256], c_ref[j, :, 256:512], c_ref[j, :, 512:768], c_ref[j, :, 768:1024]


def _merge_fwd(proj, ya, o, sg, packed_w, bg, name, tm=256):
    S = proj.shape[0]
    row, gate, full, packed = _merge_specs(tm)

    def body(g0, g1, g2, ya_ref, o_ref, sg_ref, c_ref, bg_ref, out_ref):
        yav, ov, sgv = ya_ref[...], o_ref[...], sg_ref[...]
        for j in range(4):
            cols = slice(256 * j, 256 * (j + 1))
            wa, wb0, wb1, wc = _branch_shards(c_ref, j)
            y = (jnp.dot(yav, wa, preferred_element_type=F32),
                 jnp.dot(ov[:, :256], wb0, preferred_element_type=F32) + jnp.dot(ov[:, 256:], wb1, preferred_element_type=F32),
                 jnp.dot(sgv, wc, preferred_element_type=F32))
            acc = jnp.zeros((tm, 256), F32)
            for b, g_ref in enumerate((g0, g1, g2)):
                acc = acc + _sigmoid(g_ref[:, cols] + bg_ref[:, b * D + 256 * j:b * D + 256 * (j + 1)]) * y[b]
            out_ref[:, cols] = acc.astype(BF16)

    return pl.pallas_call(
        body, out_shape=jax.ShapeDtypeStruct((S, D), BF16), grid=(S // tm,),
        in_specs=[gate(0), gate(1), gate(2), row(POOL_W), row(FOX_W), row(SGU_W), packed, full(1, 3 * D)],
        out_specs=row(D), compiler_params=_cp(("parallel",)), name=name)(proj, proj, proj, ya, o, sg, packed_w, bg)


def _merge_bwd(dm, proj, ya, o, sg, packed_w, bg, grads, name, tm=256):
    S = proj.shape[0]
    row, gate, full, packed = _merge_specs(tm)
    tn_dims = (((0,), (0,)), ((), ()))
    nt_dims = (((1,), (1,)), ((), ()))

    def body(dm_ref, g0, g1, g2, ya_ref, o_ref, sg_ref, c_ref, bg_ref, _, dg_ref, dya_ref, do_ref, dsg_ref, dc_ref, dbg_ref, acc):
        i = pl.program_id(0)

        @pl.when(i == 0)
        def _():
            acc[...] = jnp.zeros_like(acc)
            dbg_ref[...] = jnp.zeros_like(dbg_ref)

        yav, ov, sgv = ya_ref[...], o_ref[...], sg_ref[...]
        o0, o1 = ov[:, :256], ov[:, 256:]
        dya = jnp.zeros((tm, POOL_W), F32)
        do0 = jnp.zeros((tm, 256), F32)
        do1 = jnp.zeros((tm, 256), F32)
        dsg = jnp.zeros((tm, SGU_W), F32)
        for j in range(4):
            cols = slice(256 * j, 256 * (j + 1))
            wa, wb0, wb1, wc = _branch_shards(c_ref, j)
            y = (jnp.dot(yav, wa, preferred_element_type=F32),
                 jnp.dot(o0, wb0, preferred_element_type=F32) + jnp.dot(o1, wb1, preferred_element_type=F32),
                 jnp.dot(sgv, wc, preferred_element_type=F32))
            dmv = dm_ref[:, cols]
            dy = []
            for b, g_ref in enumerate((g0, g1, g2)):
                bcols = slice(b * D + 256 * j, b * D + 256 * (j + 1))
                gt = _sigmoid(g_ref[:, cols] + bg_ref[:, bcols])
                dgp = dmv * y[b] * gt * (1.0 - gt)
                dg_ref[:, bcols] = dgp.astype(BF16)
                dbg_ref[:, bcols] += jnp.sum(dgp, axis=0, keepdims=True)
                dy.append((dmv * gt).astype(BF16))
            dya = dya + lax.dot_general(dy[0], wa, nt_dims, preferred_element_type=F32)
            do0 = do0 + lax.dot_general(dy[1], wb0, nt_dims, preferred_element_type=F32)
            do1 = do1 + lax.dot_general(dy[1], wb1, nt_dims, preferred_element_type=F32)
            dsg = dsg + lax.dot_general(dy[2], wc, nt_dims, preferred_element_type=F32)
            acc[j, :, 0:256] += lax.dot_general(yav, dy[0], tn_dims, preferred_element_type=F32)
            acc[j, :, 256:512] += lax.dot_general(o0, dy[1], tn_dims, preferred_element_type=F32)
            acc[j, :, 512:768] += lax.dot_general(o1, dy[1], tn_dims, preferred_element_type=F32)
            acc[j, :, 768:1024] += lax.dot_general(sgv, dy[2], tn_dims, preferred_element_type=F32)
        dya_ref[...] = dya
        do_ref[:, :256] = do0
        do_ref[:, 256:] = do1
        dsg_ref[...] = dsg

        @pl.when(i == pl.num_programs(0) - 1)
        def _():
            dc_ref[...] = acc[...].astype(dc_ref.dtype)

    return pl.pallas_call(
        body, out_shape=(jax.ShapeDtypeStruct((S, 3 * D), BF16), jax.ShapeDtypeStruct((S, POOL_W), F32),
                         jax.ShapeDtypeStruct((S, FOX_W), F32), jax.ShapeDtypeStruct((S, SGU_W), F32),
                         jax.ShapeDtypeStruct(grads.shape, grads.dtype), jax.ShapeDtypeStruct((1, 3 * D), F32)),
        grid=(S // tm,),
        in_specs=[row(D), gate(0), gate(1), gate(2), row(POOL_W), row(FOX_W), row(SGU_W), packed, full(1, 3 * D), ANY],
        out_specs=(row(3 * D), row(POOL_W), row(FOX_W), row(SGU_W), packed, full(1, 3 * D)),
        scratch_shapes=[pltpu.VMEM((4, 256, PACK_COLS), F32)], input_output_aliases={9: 4},
        compiler_params=_cp(("arbitrary",)), name=name)(dm, proj, proj, proj, ya, o, sg, packed_w, bg, grads)


def _xattn_probs(qh, kh):
    s = lax.dot_general(qh, kh, (((1,), (1,)), ((), ())), preferred_element_type=F32) * X_SCALE
    p = jnp.exp(s - jnp.max(s, axis=-1, keepdims=True))
    return p / jnp.sum(p, axis=-1, keepdims=True)


def _xattn_fwd(xq, kv, name, tq=256):
    S = xq.shape[0]
    M = kv.shape[0]

    def body(q_ref, k_ref, v_ref, o_ref):
        for h in range(XH):
            sl = slice(h * XHD, (h + 1) * XHD)
            p = _xattn_probs(q_ref[:, sl], k_ref[:, sl])
            o_ref[:, sl] = jnp.dot(p.astype(BF16), v_ref[:, sl], preferred_element_type=F32).astype(BF16)

    return pl.pallas_call(
        body, out_shape=jax.ShapeDtypeStruct((S, D), BF16), grid=(S // tq,),
        in_specs=[pl.BlockSpec((tq, D), lambda i: (i, 0)), pl.BlockSpec((M, D), lambda i: (0, 0)),
                  pl.BlockSpec((M, D), lambda i: (0, 1))],
        out_specs=pl.BlockSpec((tq, D), lambda i: (i, 0)), compiler_params=_cp(("parallel",)), name=name)(xq, kv, kv)


def _xattn_bwd(xq, kv, do, name, tq=256):
    S = xq.shape[0]
    M = kv.shape[0]

    def body(q_ref, k_ref, v_ref, do_ref, dq_ref, dkv_ref, dk_acc, dv_acc):
        i = pl.program_id(0)

        @pl.when(i == 0)
        def _():
            dk_acc[...] = jnp.zeros_like(dk_acc)
            dv_acc[...] = jnp.zeros_like(dv_acc)

        for h in range(XH):
            sl = slice(h * XHD, (h + 1) * XHD)
            qh, kh, vh, doh = q_ref[:, sl], k_ref[:, sl], v_ref[:, sl], do_ref[:, sl]
            p = _xattn_probs(qh, kh)
            dp = lax.dot_general(doh, vh, (((1,), (1,)), ((), ())), preferred_element_type=F32)
            ds = p * (dp - jnp.sum(p * dp, axis=-1, keepdims=True))
            dsb = (ds * X_SCALE).astype(BF16)
            dq_ref[:, sl] = jnp.dot(dsb, kh, preferred_element_type=F32).astype(BF16)
            dk_acc[:, sl] += lax.dot_general(dsb, qh, (((0,), (0,)), ((), ())), preferred_element_type=F32)
            dv_acc[:, sl] += lax.dot_general(p.astype(BF16), doh, (((0,), (0,)), ((), ())), preferred_element_type=F32)

        @pl.when(i == pl.num_programs(0) - 1)
        def _():
            dkv_ref[:, :D] = dk_acc[...].astype(BF16)
            dkv_ref[:, D:] = dv_acc[...].astype(BF16)

    return pl.pallas_call(
        body, out_shape=(jax.ShapeDtypeStruct((S, D), BF16), jax.ShapeDtypeStruct((M, 2 * D), BF16)), grid=(S // tq,),
        in_specs=[pl.BlockSpec((tq, D), lambda i: (i, 0)), pl.BlockSpec((M, D), lambda i: (0, 0)),
                  pl.BlockSpec((M, D), lambda i: (0, 1)), pl.BlockSpec((tq, D), lambda i: (i, 0))],
        out_specs=(pl.BlockSpec((tq, D), lambda i: (i, 0)), pl.BlockSpec((M, 2 * D), lambda i: (0, 0))),
        scratch_shapes=[pltpu.VMEM((M, D), F32), pltpu.VMEM((M, D), F32)],
        compiler_params=_cp(("arbitrary",)), name=name)(xq, kv, kv, do)


def _adam_math(gv, wv, mv, vv):
    c1 = 1.0 - ADAM_B1 ** ADAM_STEP
    c2 = 1.0 - ADAM_B2 ** ADAM_STEP
    nm = ADAM_B1 * mv + (1.0 - ADAM_B1) * gv
    nv = ADAM_B2 * vv + (1.0 - ADAM_B2) * (gv * gv)
    return -ADAM_LR * ((nm / c1) / (jnp.sqrt(nv / c2) + ADAM_EPS) + ADAM_WD * wv), nm, nv


def _adamw(g, w, m, v, name, block=None):
    if block is None:
        block = (1, 256 if g.shape[1] % 256 == 0 else g.shape[1], g.shape[2])
    grid = tuple(s // b for s, b in zip(g.shape, block))

    def body(g_ref, w_ref, m_ref, v_ref, d_ref, nm_ref, nv_ref):
        d_ref[...], nm_ref[...], nv_ref[...] = _adam_math(g_ref[...], w_ref[...], m_ref[...], v_ref[...])

    blk = pl.BlockSpec(block, lambda a, b, c: (a, b, c))
    return pl.pallas_call(
        body, out_shape=(jax.ShapeDtypeStruct(g.shape, F32),) * 3, grid=grid,
        in_specs=[blk] * 4, out_specs=(blk,) * 3, compiler_params=_cp(("parallel",) * 3), name=name)(g, w, m, v)


def _adamw_packed(red, w, m, v, g_index, name, tr=256):
    L, r, c = w.shape
    tr = min(tr, r)

    def body(g0_ref, g1_ref, w_ref, m_ref, v_ref, g_ref, d_ref, nm_ref, nv_ref):
        gv = jnp.where(pl.program_id(0) == 0, g0_ref[...], g1_ref[...])
        g_ref[0] = gv
        d_ref[0], nm_ref[0], nv_ref[0] = _adam_math(gv, w_ref[0], m_ref[0], v_ref[0])

    gblk = pl.BlockSpec((tr, c), lambda l, i: g_index(i))
    blk = pl.BlockSpec((1, tr, c), lambda l, i: (l, i, 0))
    return pl.pallas_call(
        body, out_shape=(jax.ShapeDtypeStruct(w.shape, F32),) * 4, grid=(L, r // tr),
        in_specs=[gblk, gblk, blk, blk, blk], out_specs=(blk,) * 4,
        compiler_params=_cp(("parallel", "parallel")), name=name)(red[0], red[1], w, m, v)


def _row_tile(R):
    return next((t for t in (512, 496, 384, 256) if R % t == 0), R)


def _sum_slots(a, out_dtype, name):
    n, R, C = a.shape
    tr = _row_tile(R)

    def body(a_ref, o_ref):
        acc = a_ref[0].astype(F32)
        for k in range(1, n):
            acc = acc + a_ref[k].astype(F32)
        o_ref[...] = acc.astype(out_dtype)

    return pl.pallas_call(
        body, out_shape=jax.ShapeDtypeStruct((R, C), out_dtype), grid=(R // tr,),
        in_specs=[pl.BlockSpec((n, tr, C), lambda i: (0, i, 0))], out_specs=pl.BlockSpec((tr, C), lambda i: (i, 0)),
        compiler_params=_cp(("parallel",)), name=name)(a)


def _add_pair(a, b, name):
    n, R, C = a.shape
    tr = _row_tile(R)

    def body(a_ref, b_ref, o_ref):
        o_ref[...] = (a_ref[...].astype(F32) + b_ref[...].astype(F32)).astype(BF16)

    blk = pl.BlockSpec((1, tr, C), lambda k, i: (k, i, 0))
    return pl.pallas_call(
        body, out_shape=jax.ShapeDtypeStruct(a.shape, BF16), grid=(n, R // tr), in_specs=[blk, blk], out_specs=blk,
        compiler_params=_cp(("parallel", "parallel")), name=name)(a, b)


LANDING = pl.BlockSpec(memory_space=pltpu.VMEM)


def _landing_params(shape, dtype):
    return pltpu.CompilerParams(vmem_limit_bytes=math.prod(shape) * jnp.dtype(dtype).itemsize + 4 * 1024 * 1024)


def _place():
    return lax.axis_index("x"), lax.axis_index("y"), lax.axis_index("c")


def _other_chips(x, y):
    return [(1 - x, y), (x, 1 - y), (1 - x, 1 - y)]


def _row_chunks(rows, want, align=16):
    n = want
    while n > 1 and rows % (n * align):
        n -= 1
    return n


def _pair_split(g, name, nch=5):
    n, R, C = g.shape
    half = R // 2
    nch = _row_chunks(half, nch)
    cr = half // nch

    def body(g_ref, own_ref, got_ref, send_sems, recv_sems, local_sem):
        x, y, c = _place()
        mine0 = pl.multiple_of(c * half, 16)
        theirs0 = (1 - c) * half
        keep = pltpu.make_async_copy(g_ref.at[:, pl.ds(mine0, half), :], own_ref, local_sem)
        keep.start()
        cps = []
        for s in range(n):
            for q in range(nch):
                src = g_ref.at[s, pl.ds(pl.multiple_of(theirs0 + q * cr, 16), cr), :]
                cps.append(pltpu.make_async_remote_copy(
                    src_ref=src, dst_ref=got_ref.at[s, pl.ds(q * cr, cr), :], send_sem=send_sems.at[s * nch + q],
                    recv_sem=recv_sems.at[s * nch + q], device_id=(x, y, 1 - c), device_id_type=MESH))
        for cp in cps:
            cp.start()
        for cp in cps:
            cp.wait()
        keep.wait()

    sh = jax.ShapeDtypeStruct((n, half, C), g.dtype)
    return pl.pallas_call(
        body, out_shape=(sh, sh), in_specs=[ANY], out_specs=(ANY, LANDING),
        scratch_shapes=[pltpu.SemaphoreType.DMA((n * nch,)), pltpu.SemaphoreType.DMA((n * nch,)), pltpu.SemaphoreType.DMA],
        compiler_params=_landing_params(sh.shape, g.dtype), name=name)(g)


def _pair_gather(t, name, nch=10):
    R = t.shape[0]
    nch = _row_chunks(R, nch, 8)
    cr = R // nch

    def body(t_ref, o_ref, send_sems, recv_sems, local_sem):
        x, y, c = _place()
        own = pltpu.make_async_copy(t_ref, o_ref.at[c], local_sem)
        own.start()
        cps = [pltpu.make_async_remote_copy(src_ref=t_ref.at[pl.ds(q * cr, cr), :], dst_ref=o_ref.at[c, pl.ds(q * cr, cr), :],
                                            send_sem=send_sems.at[q], recv_sem=recv_sems.at[q], device_id=(x, y, 1 - c),
                                            device_id_type=MESH) for q in range(nch)]
        for cp in cps:
            cp.start()
        for cp in cps:
            cp.wait()
        own.wait()

    return pl.pallas_call(
        body, out_shape=jax.ShapeDtypeStruct((2,) + t.shape, t.dtype), in_specs=[ANY], out_specs=LANDING,
        scratch_shapes=[pltpu.SemaphoreType.DMA((nch,)), pltpu.SemaphoreType.DMA((nch,)), pltpu.SemaphoreType.DMA],
        compiler_params=_landing_params((2,) + t.shape, t.dtype), name=name)(t)


HBM = pl.BlockSpec(memory_space=pltpu.HBM)
SEM = pl.BlockSpec(memory_space=pltpu.SEMAPHORE)
SPLIT_COPY = pltpu.CompilerParams(has_side_effects=pltpu.SideEffectType.DATAFLOW_SIDE_EFFECTING)


def _split_exchange(src, rows, src_of, tag, nch=5):
    C = src.shape[-1]
    nch = _row_chunks(rows, nch)
    cr = rows // nch
    n = 3 * nch
    land_shape = (4, rows, C)

    def copies(src_ref, land_ref, send_sems, recv_sems):
        x, y, c = _place()
        j = 2 * x + y
        out = []
        for q in range(nch):
            for k, (px, py) in enumerate(_other_chips(x, y)):
                out.append(pltpu.make_async_remote_copy(
                    src_ref=src_of(src_ref, px, py, c, q * cr, cr), dst_ref=land_ref.at[j, pl.ds(q * cr, cr), :],
                    send_sem=send_sems.at[k * nch + q], recv_sem=recv_sems.at[k * nch + q], device_id=(px, py, c),
                    device_id_type=MESH))
        return out

    def start(src_ref, land_ref, send_sems, recv_sems, src_thru, land_thru, token):
        for cp in copies(src_ref, land_ref, send_sems, recv_sems):
            cp.start()
        token[...] = jnp.zeros_like(token)

    send_sems, recv_sems, src_thru, land_thru, token = pl.pallas_call(
        start, name=f"{tag}_start",
        out_shape=(pltpu.SemaphoreType.DMA((n,)), pltpu.SemaphoreType.DMA((n,)), pltpu.HBM(src.shape, src.dtype),
                   pltpu.HBM(land_shape, src.dtype), jax.ShapeDtypeStruct((8, 128), F32)),
        in_specs=(HBM, HBM), out_specs=(SEM, SEM, HBM, HBM, pl.BlockSpec(memory_space=pltpu.VMEM)),
        input_output_aliases={0: 2, 1: 3}, compiler_params=SPLIT_COPY)(
            pltpu.with_memory_space_constraint(src, pltpu.HBM),
            pltpu.with_memory_space_constraint(lax.empty(land_shape, src.dtype), pltpu.HBM))

    def finish(after):
        def wait(src_ref, land_ref, send_sems, recv_sems, after_ref, src_dead, got_ref):
            for cp in copies(src_ref, land_ref, send_sems, recv_sems):
                cp.wait_send()
                cp.wait_recv()

        return pl.pallas_call(
            wait, name=f"{tag}_wait", out_shape=(pltpu.HBM(src.shape, src.dtype), pltpu.HBM(land_shape, src.dtype)),
            in_specs=(HBM, HBM, SEM, SEM, ANY), out_specs=(HBM, HBM), input_output_aliases={0: 0, 1: 1},
            compiler_params=SPLIT_COPY)(src_thru, land_thru, send_sems, recv_sems, after)

    return token, finish


def _gather_finish(shard, land, name, nch=5):
    R, C = shard.shape
    half = R // 2
    nch = _row_chunks(half, nch)
    cr = half // nch

    def body(s_ref, l_ref, o_ref, send_sems, recv_sems, local_sems):
        x, y, c = _place()
        j = 2 * x + y
        mine0 = c * half
        local = [pltpu.make_async_copy(s_ref, o_ref.at[j], local_sems.at[0])]
        remote = []
        for k, (px, py) in enumerate(_other_chips(x, y)):
            jj = 2 * px + py
            local.append(pltpu.make_async_copy(l_ref.at[jj], o_ref.at[jj, pl.ds(pl.multiple_of(mine0, 16), half), :],
                                               local_sems.at[1 + k]))
            for q in range(nch):
                remote.append(pltpu.make_async_remote_copy(
                    src_ref=l_ref.at[jj, pl.ds(q * cr, cr), :],
                    dst_ref=o_ref.at[jj, pl.ds(pl.multiple_of(mine0 + q * cr, 16), cr), :], send_sem=send_sems.at[k * nch + q],
                    recv_sem=recv_sems.at[k * nch + q], device_id=(x, y, 1 - c), device_id_type=MESH))
        for cp in local + remote:
            cp.start()
        for cp in remote + local:
            cp.wait()

    return pl.pallas_call(
        body, out_shape=jax.ShapeDtypeStruct((4, R, C), shard.dtype), in_specs=[ANY, ANY], out_specs=LANDING,
        scratch_shapes=[pltpu.SemaphoreType.DMA((3 * nch,)), pltpu.SemaphoreType.DMA((3 * nch,)), pltpu.SemaphoreType.DMA((4,))],
        compiler_params=_landing_params((4, R, C), shard.dtype), name=name)(shard, land)


def _sum_slots_own(land, own, name):
    n, R, C = land.shape
    tr = _row_tile(R)
    me = (2 * lax.axis_index("x") + lax.axis_index("y")).astype(jnp.int32).reshape(1)
    if own.ndim == 3:
        own_spec = pl.BlockSpec((None, tr, C), lambda i, me: (me[0], i, 0))
    else:
        own_spec = pl.BlockSpec((tr, C), lambda i, me: (i, 0))

    def body(me_ref, land_ref, own_ref, o_ref):
        acc = None
        for k in range(n):
            v = jnp.where(me_ref[0] == k, own_ref[...], land_ref[k]).astype(F32)
            acc = v if acc is None else acc + v
        o_ref[...] = acc

    return pl.pallas_call(
        body, out_shape=jax.ShapeDtypeStruct((R, C), F32),
        grid_spec=pltpu.PrefetchScalarGridSpec(
            num_scalar_prefetch=1, grid=(R // tr,),
            in_specs=[pl.BlockSpec((n, tr, C), lambda i, me: (0, i, 0)), own_spec],
            out_specs=pl.BlockSpec((tr, C), lambda i, me: (i, 0))),
        compiler_params=_cp(("parallel",)), name=name)(me, land, own)


def _reduce_begin(g, tag):
    own, got = _pair_split(g, f"rs_pair_{tag}")
    p = _add_pair(own, got, f"rs_add_{tag}")
    token, finish = _split_exchange(p, p.shape[1], lambda ref, px, py, c, r0, cr: ref.at[2 * px + py, pl.ds(r0, cr), :],
                                    f"rs_a2a_{tag}")
    return (finish, g.shape, tag), token


def _reduce_end(state, after):
    finish, shape, tag = state
    p, land = finish(after)
    t = _sum_slots_own(land, p, f"rs_sum_{tag}")
    return _pair_gather(t, f"rs_join_{tag}").reshape(shape[1], shape[2])


def _all_reduce_begin(v, tag):
    p = _sum_slots(_pair_gather(v, f"ar_pair_{tag}"), F32, f"ar_add_{tag}")
    _, finish = _split_exchange(p, p.shape[0], lambda ref, px, py, c, r0, cr: ref.at[pl.ds(r0, cr), :], f"ar_a2a_{tag}")
    return finish, tag


def _all_reduce_end(state, after):
    finish, tag = state
    p, land = finish(after)
    return _sum_slots_own(land, p, f"ar_sum_{tag}")


def _gather_begin(shard, tag):
    half = shard.shape[0] // 2
    token, finish = _split_exchange(
        shard, half, lambda ref, px, py, c, r0, cr: ref.at[pl.ds(pl.multiple_of(c * half + r0, 16), cr), :], f"gather_{tag}")
    return (finish, tag), token


def _gather_end(state, after):
    finish, tag = state
    shard, land = finish(after)
    return _gather_finish(shard, land, f"gather_{tag}_finish")


R_BRANCH, R_OUT, R_WIN, ROWS_A = 0, 256, 512, 1888
R_FF1, R_FF2, R_XKV, R_XQ, R_XO, ROWS_B = 0, 1024, 2048, 2560, 2816, 3072
WIN_ROWS = N_IN // 4


def _w_in_t(a):
    return jnp.transpose(a, (2, 0, 1))


def _pack_shard(w, l):
    xkv, wb = w['w_xkv'][l], w['w_branch_b'][l]
    a = [jnp.concatenate([w['w_branch_a'][l], wb[:256], wb[256:], w['w_branch_c'][l]], axis=1), w['w_out'][l],
         jnp.pad(_w_in_t(w['w_in'])[:, l, :], ((0, ROWS_A - R_WIN - WIN_ROWS), (0, 0)))]
    b = [w['w_ff1'][l], w['w_ff2'][l], jnp.concatenate([xkv[:512], xkv[512:]], axis=1), w['w_xq'][l], w['w_xo'][l]]
    return jnp.concatenate(a, axis=0).astype(BF16), jnp.concatenate(b, axis=0).astype(BF16)


def _w_in_rows(gathered):
    t = gathered[:, R_WIN:R_WIN + WIN_ROWS, :].reshape(N_IN, PACK_COLS)
    return jnp.concatenate([t[2312:5384], t[256:1792], t[1800:2312], t[0:256],
                            jnp.pad(t[1792:1800], ((0, NP - P_F - 8), (0, 0)))], axis=0)


def _w_in_grad_rows(grads, dwt):
    t = jnp.concatenate([dwt[P_A:P_A + 256], dwt[P_Q:P_Q + 1536], dwt[P_F:P_F + 8], dwt[P_C:P_C + 512], dwt[P_G:P_G + 3072]],
                        axis=0)
    return lax.dynamic_update_slice(grads, t.reshape(4, WIN_ROWS, PACK_COLS).astype(grads.dtype), (0, R_WIN, 0))


def _small_prep(sw, l):
    eye = jnp.eye(4, dtype=F32)
    bd = jnp.einsum('gh,gcd->gchd', eye, sw['pool_w'][l]).reshape(POOL_W, POOL_W).astype(BF16)
    tril = jnp.tril(jnp.ones((SGU_CHUNK, SGU_CHUNK), F32))
    wm = (sw['sgu_w'][l] * tril[None]).astype(BF16)
    return dict(
        g_mix=sw['norm_mix_g'][l][None], g_x=sw['norm_xattn_g'][l][None], g_mem=sw['norm_mem_g'][l][None],
        g_ffn=sw['norm_ffn_g'][l][None], bd=bd, pool_scale=sw['pool_scale'][l][None],
        bf=jnp.pad(sw['b_forget'][l], (0, FCOLS - 8))[None], sgu_g=sw['sgu_norm_g'][l][None], wm=wm,
        wmt=jnp.transpose(wm, (0, 2, 1)), sgu_bias=jnp.repeat(sw['sgu_b'][l].T, 64, axis=1), bg=sw['b_gate'][l][None])


def _rows4(r0):
    return dict(n=D, k=D, b_block=(4, 256, 512), b_index=lambda i, j, k: (0, r0 // 256, j))


def _rows_t(r0):
    return dict(tb=True, n=D, k=D, tn=256, b_block=(None, 256, PACK_COLS), b_index=lambda i, j, k: (j, r0 // 256, 0))


def _rows_grad(r0):
    return dict(ta=True, tm=256, tn=512, o_block=(None, 256, 512), o_index=lambda i, j, k: (i, r0 // 256, j))


def _add_to(r, e):
    return e + r


def _after(v, token):
    return v if token is None else v + token[0, 0]


def _layer_fwd(x, mem, GA, w_in_t, sp, l, token, second):
    t = f"l{l}"
    S = x.shape[0]
    h = _rms_fwd(x, _after(sp['g_mix'], token), f"rms_mix_{t}")
    proj = _mm(h, w_in_t, name=f"proj_{t}", out_dtype=F32, tb=True)
    d, ya = _pool_fwd(proj, sp['bd'], sp['pool_scale'], f"pool_fwd_{t}")
    fcum = _fgate_fwd(proj, sp['bf'], f"fgate_fwd_{t}")
    f8 = fcum[:, :8]
    fcol = f8.reshape(S, 4, 2).transpose(1, 0, 2)
    frow = f8.T.reshape(4, 2, S)
    qkv = proj[:, P_Q:P_Q + 3 * FOX_W].astype(BF16)
    o, o32, lse = _fox_fwd(qkv, fcol, frow, f"fox_fwd_{t}")
    sg = _sgu_fwd(proj, sp['sgu_g'], sp['wm'], sp['sgu_bias'], f"sgu_fwd_{t}")
    merged = _merge_fwd(proj, ya, o, sg, GA, sp['bg'], f"merge_fwd_{t}")
    x1 = _mm(merged, GA, name=f"out_{t}", out_dtype=F32, extra=x, epi=_add_to, **_rows4(R_OUT))
    GB, token = second(x1)
    hx = _rms_fwd(x1, _after(sp['g_x'], token), f"rms_x_{t}")
    hm = _rms_fwd(mem, sp['g_mem'], f"rms_mem_{t}")
    xq = _mm(hx, GB, name=f"xq_{t}", out_dtype=BF16, **_rows4(R_XQ))
    kv = _mm(hm, GB, name=f"xkv_{t}", out_dtype=BF16, n=2 * D, k=D, tn=512, tk=512, b_block=(None, 512, 512),
             b_index=lambda i, j, k: (j, R_XKV // 512, k))
    o2 = _xattn_fwd(xq, kv, f"xattn_fwd_{t}")
    x2 = _mm(o2, GB, name=f"xo_{t}", out_dtype=F32, extra=x1, epi=_add_to, **_rows4(R_XO))
    hf = _rms_fwd(x2, sp['g_ffn'], f"rms_ffn_{t}")
    z = _mm(hf, GB, name=f"ff1_{t}", out_dtype=F32, n=D_FF, k=D, tn=512, b_block=(None, 1024, 512),
            b_index=lambda i, j, k: (j // 2, R_FF1 // 1024, j % 2))
    x3 = _mm(z, GB, name=f"ff2_{t}", out_dtype=F32, a_fn=_relu2, extra=x2, epi=_add_to, n=D, k=D_FF, tk=1024,
             b_block=(None, 1024, 512), b_index=lambda i, j, k: (k, R_FF2 // 1024, j))
    saved = dict(x=x, h=h, proj=proj, d=d, ya=ya, fcol=fcol, frow=frow, qkv=qkv, o=o, o32=o32, lse=lse, sg=sg, merged=merged,
                 x1=x1, hx=hx, hm=hm, xq=xq, kv=kv, o2=o2, x2=x2, hf=hf, z=z, GA=GA, GB=GB, w_in_t=w_in_t)
    return x3, saved


def _layer_bwd(dx3, mem, sp, sv, l, token, early):
    t = f"l{l}"
    S = dx3.shape[0]
    GA, GB, w_in_t = sv['GA'], sv['GB'], sv['w_in_t']
    gs = {}
    dx3 = _after(dx3, token)
    gb = lax.empty((4, ROWS_B, PACK_COLS), BF16)
    dz = _mm(dx3, GB, name=f"d_a2_{t}", out_dtype=BF16, tb=True, n=D_FF, k=D, tn=512, b_block=(None, 512, PACK_COLS),
             b_index=lambda i, j, k: (j // 2, R_FF2 // 512 + j % 2, 0), extra=sv['z'],
             epi=lambda r, e: r * (2.0 * jnp.maximum(e, 0.0)))
    gb = _mm(sv['z'], dx3, name=f"dw_ff2_{t}", out_dtype=BF16, ta=True, a_fn=_relu2, into=gb, tm=1024, tn=512,
             o_block=(None, 1024, 512), o_index=lambda i, j, k: (i, R_FF2 // 1024, j))
    gb = _mm(sv['hf'], dz, name=f"dw_ff1_{t}", out_dtype=BF16, ta=True, into=gb, tm=1024, tn=512,
             o_block=(None, 1024, 512), o_index=lambda i, j, k: (j // 2, R_FF1 // 1024, j % 2))
    dhf = _mm(dz, GB, name=f"d_hf_{t}", out_dtype=F32, tb=True, n=D, k=D_FF, tn=512, tk=1024, b_block=(None, 512, PACK_COLS),
              b_index=lambda i, j, k: (k, R_FF1 // 512 + j, 0))
    dx2, gs['norm_ffn_g'] = _rms_bwd(dhf, sv['x2'], sp['g_ffn'], dx3, f"rms_ffn_bwd_{t}")
    do2 = _mm(dx2, GB, name=f"d_o2_{t}", out_dtype=BF16, **_rows_t(R_XO))
    gb = _mm(sv['o2'], dx2, name=f"dw_xo_{t}", out_dtype=BF16, into=gb, **_rows_grad(R_XO))
    dxq, dkv = _xattn_bwd(sv['xq'], sv['kv'], do2, f"xattn_bwd_{t}")
    gb = _mm(sv['hm'], dkv, name=f"dw_xkv_{t}", out_dtype=BF16, ta=True, into=gb, tm=512, tn=512,
             o_block=(None, 512, 512), o_index=lambda i, j, k: (j, R_XKV // 512, i))
    dhm = _mm(dkv, GB, name=f"d_hm_{t}", out_dtype=F32, tb=True, n=D, k=2 * D, tn=512, tk=512, b_block=(None, 512, 512),
              b_index=lambda i, j, k: (k, R_XKV // 512, j))
    gs['norm_mem_g'] = _rms_bwd(dhm, mem, sp['g_mem'], None, f"rms_mem_bwd_{t}")
    gb = _mm(sv['hx'], dxq, name=f"dw_xq_{t}", out_dtype=BF16, into=gb, **_rows_grad(R_XQ))
    token = early(gb)
    dhx = _mm(dxq, GB, name=f"d_hx_{t}", out_dtype=F32, **_rows_t(R_XQ))
    dx1, gs['norm_xattn_g'] = _rms_bwd(dhx, sv['x1'], _after(sp['g_x'], token), dx2, f"rms_x_bwd_{t}")
    ga = jnp.zeros((4, ROWS_A, PACK_COLS), BF16)
    ga = _mm(sv['merged'], dx1, name=f"dw_out_{t}", out_dtype=BF16, into=ga, **_rows_grad(R_OUT))
    dm = _mm(dx1, GA, name=f"d_merged_{t}", out_dtype=F32, **_rows_t(R_OUT))
    dg, dya, do, dsg, ga, gs['b_gate'] = _merge_bwd(dm, sv['proj'], sv['ya'], sv['o'], sv['sg'], GA, sp['bg'], ga, f"merge_bwd_{t}")
    dc, dws, dbias, gs['sgu_norm_g'] = _sgu_bwd(dsg, sv['proj'], sp['sgu_g'], sp['wm'], sp['wmt'], sp['sgu_bias'], f"sgu_bwd_{t}")
    tril = jnp.tril(jnp.ones((SGU_CHUNK, SGU_CHUNK), F32))
    gs['sgu_w'] = dws * tril[None]
    gs['sgu_b'] = dbias.reshape(SGU_CHUNK, 4, 64).sum(-1).T
    dq, dk, dv, dfrow, dfcol = _fox_bwd(sv['qkv'], sv['o32'], do, sv['lse'], sv['fcol'], sv['frow'], f"fox_bwd_{t}")
    dF = jnp.pad(dfrow.reshape(8, S).T + dfcol.transpose(1, 0, 2).reshape(S, 8), ((0, 0), (0, FCOLS - 8)))
    df, dbf = _fgate_bwd(dF, sv['proj'], sp['bf'], f"fgate_bwd_{t}")
    gs['b_forget'] = dbf[:, :8]
    da, dbd, gs['pool_scale'] = _pool_bwd(dya, sv['d'], sp['bd'], sp['pool_scale'], f"pool_bwd_{t}")
    gs['pool_w'] = jnp.stack([dbd[g * 64:(g + 1) * 64, g * 64:(g + 1) * 64] for g in range(4)])
    dproj = jnp.concatenate([dg, dq, dk, dv, dc, da, df], axis=1)
    dwt = _mm(dproj, sv['h'], name=f"dw_in_{t}", out_dtype=BF16, ta=True, tm=512, tn=1024)
    ga = _w_in_grad_rows(ga, dwt)
    dh = _mm(dproj, w_in_t, name=f"d_h_{t}", out_dtype=F32, tk=512)
    dx, gs['norm_mix_g'] = _rms_bwd(dh, sv['x'], sp['g_mix'], dx1, f"rms_mix_bwd_{t}")
    return dx, ga, gs


SMALL_ROWS = 1424
GRAD_BLOCKS = {
    'w_ff1': ('b', lambda i: (R_FF1 // 256 + i, 0)), 'w_ff2': ('b', lambda i: (R_FF2 // 256 + i, 0)),
    'w_xq': ('b', lambda i: (R_XQ // 256 + i, 0)), 'w_xo': ('b', lambda i: (R_XO // 256 + i, 0)),
    'w_xkv': ('b', lambda i: (R_XKV // 256 + i % 2, i // 2)), 'w_out': ('a', lambda i: (R_OUT // 256 + i, 0)),
    'w_branch_a': ('a', lambda i: (R_BRANCH // 256, 0)), 'w_branch_b': ('a', lambda i: (R_BRANCH // 256, 1 + i)),
    'w_branch_c': ('a', lambda i: (R_BRANCH // 256, 3)),
}


def _pack_small(parts):
    flat = jnp.concatenate([p.reshape(-1) for p in parts])
    return jnp.pad(flat, (0, SMALL_ROWS * 128 - flat.shape[0])).reshape(SMALL_ROWS, 128)


def _unpack_small(buf, shapes):
    flat, out, r = buf.reshape(-1), [], 0
    for s in shapes:
        n = math.prod(s)
        out.append(flat[r:r + n].reshape(s))
        r += n
    return out


def kernel(x, mem, norm_mix_g, w_in, b_forget, pool_w, pool_scale, sgu_norm_g, sgu_w, sgu_b, w_branch_a, w_branch_b, w_branch_c, b_gate, w_out, norm_xattn_g, norm_mem_g, w_xq, w_xkv, w_xo, norm_ffn_g, w_ff1, w_ff2, final_norm_g, loss_target, m_norm_mix_g, m_w_in, m_b_forget, m_pool_w, m_pool_scale, m_sgu_norm_g, m_sgu_w, m_sgu_b, m_w_branch_a, m_w_branch_b, m_w_branch_c, m_b_gate, m_w_out, m_norm_xattn_g, m_norm_mem_g, m_w_xq, m_w_xkv, m_w_xo, m_norm_ffn_g, m_w_ff1, m_w_ff2, m_final_norm_g, v_norm_mix_g, v_w_in, v_b_forget, v_pool_w, v_pool_scale, v_sgu_norm_g, v_sgu_w, v_sgu_b, v_w_branch_a, v_w_branch_b, v_w_branch_c, v_b_gate, v_w_out, v_norm_xattn_g, v_norm_mem_g, v_w_xq, v_w_xkv, v_w_xo, v_norm_ffn_g, v_w_ff1, v_w_ff2, v_final_norm_g):
    args = (norm_mix_g, w_in, b_forget, pool_w, pool_scale, sgu_norm_g, sgu_w, sgu_b, w_branch_a, w_branch_b, w_branch_c, b_gate,
            w_out, norm_xattn_g, norm_mem_g, w_xq, w_xkv, w_xo, norm_ffn_g, w_ff1, w_ff2, final_norm_g)
    margs = (m_norm_mix_g, m_w_in, m_b_forget, m_pool_w, m_pool_scale, m_sgu_norm_g, m_sgu_w, m_sgu_b, m_w_branch_a, m_w_branch_b,
             m_w_branch_c, m_b_gate, m_w_out, m_norm_xattn_g, m_norm_mem_g, m_w_xq, m_w_xkv, m_w_xo, m_norm_ffn_g, m_w_ff1, m_w_ff2,
             m_final_norm_g)
    vargs = (v_norm_mix_g, v_w_in, v_b_forget, v_pool_w, v_pool_scale, v_sgu_norm_g, v_sgu_w, v_sgu_b, v_w_branch_a, v_w_branch_b,
             v_w_branch_c, v_b_gate, v_w_out, v_norm_xattn_g, v_norm_mem_g, v_w_xq, v_w_xkv, v_w_xo, v_norm_ffn_g, v_w_ff1, v_w_ff2,
             v_final_norm_g)
    w = dict(zip(W_NAMES, args))
    mo = dict(zip(W_NAMES, margs))
    vo = dict(zip(W_NAMES, vargs))
    xs, mems, tgt = x[0], mem[0], loss_target[0]
    shards = [_pack_shard(w, l) for l in range(DEPTH)]
    preps = [_small_prep(w, l) for l in range(DEPTH)]

    first_a, _ = _gather_begin(shards[0][0], "a_l0")
    pending_b, token = _gather_begin(shards[0][1], "b_l0")
    GA = None
    act, saved = xs, []
    for l in range(DEPTH):
        nxt = {}
        if l + 1 < DEPTH:
            nxt['a'], ta = _gather_begin(shards[l + 1][0], f"a_l{l + 1}")
            token = ta if token is None else token + ta
        if l == 0:
            GA = _gather_end(first_a, shards[DEPTH - 1][1])

        def second(x1, l=l, pending_b=pending_b, nxt=nxt):
            GB = _gather_end(pending_b, x1)
            if l + 1 == DEPTH:
                return GB, None
            nxt['b'], tb = _gather_begin(shards[l + 1][1], f"b_l{l + 1}")
            return GB, tb

        act, sv = _layer_fwd(act, mems, GA, _w_in_rows(GA), preps[l], l, token, second)
        saved.append(sv)
        if l + 1 < DEPTH:
            GA = _gather_end(nxt['a'], act)
            pending_b, token = nxt['b'], None
    loss_part, dact, d_final_g = _loss_head(act, w['final_norm_g'][None], tgt, "loss_head")

    red_a, red_b, small_g = [None] * DEPTH, [None] * DEPTH, [None] * DEPTH
    token, state_a = None, None
    for l in reversed(range(DEPTH)):
        early = {}

        def start_b(gb, l=l, early=early):
            early['state'], tok = _reduce_begin(gb, f"b_l{l}")
            return tok

        dact, ga, small_g[l] = _layer_bwd(dact, mems, preps[l], saved[l], l, token, start_b)
        if state_a is not None:
            red_a[l + 1] = _reduce_end(state_a, dact)
        red_b[l] = _reduce_end(early['state'], dact)
        state_a, token = _reduce_begin(ga, f"a_l{l}")
    grad_x = dact[None]
    per_layer = [n for n in SMALL_NAMES if n != 'final_norm_g']
    small_shapes = [w[n].shape for n in per_layer] + [(D,), (1,)]
    parts = [jnp.stack([small_g[l][n].reshape(w[n].shape[1:]) for l in range(DEPTH)]) for n in per_layer]
    state_small = _all_reduce_begin(_pack_small(parts + [d_final_g.reshape(D), loss_part.reshape(1)]), "small")

    grads, delta, new_m, new_v = {}, {}, {}, {}
    for n, (buf, g_index) in GRAD_BLOCKS.items():
        if buf == 'b':
            grads[n], delta[n], new_m[n], new_v[n] = _adamw_packed(red_b, w[n], mo[n], vo[n], g_index, f"adamw_{n}")
    red_a[0] = _reduce_end(state_a, new_v['w_xkv'])
    small_red = _unpack_small(_all_reduce_end(state_small, red_a[0]), small_shapes)
    grads.update(zip(per_layer + ['final_norm_g'], small_red[:-1]))
    loss = small_red[-1].reshape(())
    for n, (buf, g_index) in GRAD_BLOCKS.items():
        if buf == 'a':
            grads[n], delta[n], new_m[n], new_v[n] = _adamw_packed(red_a, w[n], mo[n], vo[n], g_index, f"adamw_{n}")
    g_t = jnp.stack([r[R_WIN:R_WIN + WIN_ROWS] for r in red_a], axis=1)
    upd = _adamw(g_t, _w_in_t(w['w_in']), _w_in_t(mo['w_in']), _w_in_t(vo['w_in']), "adamw_w_in", block=(WIN_ROWS, DEPTH, 128))
    grads['w_in'], delta['w_in'], new_m['w_in'], new_v['w_in'] = [jnp.transpose(a, (1, 2, 0)) for a in (g_t,) + tuple(upd)]
    small_all = per_layer + ['final_norm_g']
    shapes_all = [w[n].shape for n in small_all]
    packed = [_pack_small([d[n] for n in small_all])[None] for d in (grads, w, mo, vo)]
    ds, ms, vs = _adamw(*packed, "adamw_small")
    for n, a, b, c in zip(small_all, _unpack_small(ds[0], shapes_all), _unpack_small(ms[0], shapes_all), _unpack_small(vs[0], shapes_all)):
        delta[n], new_m[n], new_v[n] = a, b, c

    return (loss, grad_x, *[grads[n] for n in W_NAMES], *[delta[n] for n in W_NAMES], *[new_m[n] for n in W_NAMES],
            *[new_v[n] for n in W_NAMES])
```

```python
import math

import jax
import jax.numpy as jnp
from jax import lax
from jax.experimental import pallas as pl
from jax.experimental.pallas import tpu as pltpu

F32 = jnp.float32
BF16 = jnp.bfloat16

D = 1024
DEPTH = 2
POOL_W = 256
FOX_W = 512
SGU_W = 256
SGU_CHUNK = 128
N_IN = 5384
P_G, P_Q, P_K, P_V, P_C, P_A, P_F = 0, 3072, 3584, 4096, 4608, 5120, 5376
NP = 5632
XH, XHD = 4, 256
D_FF = 4096
EPS = 1e-6
NEG = -1e30
FOX_SCALE = 64 ** -0.5
X_SCALE = 256 ** -0.5
GELU_K = math.sqrt(2.0 / math.pi)
GELU_C = 0.044715

ADAM_LR, ADAM_B1, ADAM_B2, ADAM_EPS, ADAM_WD, ADAM_STEP = 0.001, 0.9, 0.999, 1e-08, 0.01, 10

VMEM_LIMIT = 48 * 1024 * 1024
MESH = pl.DeviceIdType.MESH

IN_NAMES = ['x', 'mem', 'norm_mix_g', 'w_in', 'b_forget', 'pool_w', 'pool_scale', 'sgu_norm_g', 'sgu_w', 'sgu_b',
            'w_branch_a', 'w_branch_b', 'w_branch_c', 'b_gate', 'w_out', 'norm_xattn_g', 'norm_mem_g', 'w_xq',
            'w_xkv', 'w_xo', 'norm_ffn_g', 'w_ff1', 'w_ff2', 'final_norm_g']
W_NAMES = IN_NAMES[2:]
BIG_NAMES = ['w_in', 'w_branch_a', 'w_branch_b', 'w_branch_c', 'w_out', 'w_xq', 'w_xkv', 'w_xo', 'w_ff1', 'w_ff2']
SMALL_NAMES = [n for n in W_NAMES if n not in BIG_NAMES]
PACK_COLS = 1024


ANY = pl.BlockSpec(memory_space=pl.ANY)


def _cp(sem=None):
    return pltpu.CompilerParams(dimension_semantics=sem, vmem_limit_bytes=VMEM_LIMIT)


def _mm(a, b, *, name, out_dtype, ta=False, tb=False, tm=1024, tn=512, tk=1024, a_fn=None, extra=None, epi=None,
        n=None, k=None, b_block=None, b_index=None, into=None, o_block=None, o_index=None):
    M = a.shape[1] if ta else a.shape[0]
    K = k if k is not None else (a.shape[0] if ta else a.shape[1])
    N = n if n is not None else (b.shape[0] if tb else b.shape[1])
    tm, tn, tk = min(tm, M), min(tn, N), min(tk, K)
    assert M % tm == 0 and N % tn == 0 and K % tk == 0, (name, M, N, K)
    nk = K // tk
    a_spec = pl.BlockSpec((tk, tm), lambda i, j, k: (k, i)) if ta else pl.BlockSpec((tm, tk), lambda i, j, k: (i, k))
    if b_block is not None:
        b_spec = pl.BlockSpec(b_block, b_index)
    else:
        b_spec = pl.BlockSpec((tn, tk), lambda i, j, k: (j, k)) if tb else pl.BlockSpec((tk, tn), lambda i, j, k: (k, j))
    dn = (((0 if ta else 1,), (1 if tb else 0,)), ((), ()))
    tile = pl.BlockSpec((tm, tn), lambda i, j, k: (i, j))
    o_spec = pl.BlockSpec(o_block, o_index) if into is not None else tile
    in_specs = [a_spec, b_spec] + ([tile] if extra is not None else []) + ([ANY] if into is not None else [])
    n_in = len(in_specs)

    def body(*refs):
        a_ref, b_ref = refs[0], refs[1]
        e_ref = refs[2] if extra is not None else None
        o_ref, acc_ref = refs[n_in], refs[n_in + 1]
        kk = pl.program_id(2)

        @pl.when(kk == 0)
        def _():
            acc_ref[...] = jnp.zeros_like(acc_ref)

        av = a_ref[...]
        if a_fn is not None:
            av = a_fn(av)
        bv = b_ref[...]
        if bv.ndim == 3:
            bv = bv.reshape(-1, bv.shape[-1])
        acc_ref[...] += lax.dot_general(av.astype(BF16), bv.astype(BF16), dn, preferred_element_type=F32)

        @pl.when(kk == nk - 1)
        def _():
            r = acc_ref[...]
            if epi is not None:
                r = epi(r, e_ref[...])
            o_ref[...] = r.astype(o_ref.dtype)

    args = (a, b) + ((extra,) if extra is not None else ()) + ((into,) if into is not None else ())
    out_shape = jax.ShapeDtypeStruct(into.shape, into.dtype) if into is not None else jax.ShapeDtypeStruct((M, N), out_dtype)
    return pl.pallas_call(
        body, out_shape=out_shape, grid=(M // tm, N // tn, nk), in_specs=in_specs, out_specs=o_spec,
        scratch_shapes=[pltpu.VMEM((tm, tn), F32)], input_output_aliases={n_in - 1: 0} if into is not None else {},
        compiler_params=_cp(("parallel", "parallel", "arbitrary")), name=name)(*args)


def _relu2(z):
    r = jnp.maximum(z, 0.0)
    return r * r


def _rms_fwd(x, g, name, tr=256):
    R, n = x.shape
    tr = min(tr, R)

    def body(x_ref, g_ref, h_ref):
        xv = x_ref[...]
        rstd = lax.rsqrt(jnp.mean(xv * xv, axis=-1, keepdims=True) + EPS)
        h_ref[...] = (xv * rstd * g_ref[...]).astype(BF16)

    return pl.pallas_call(
        body, out_shape=jax.ShapeDtypeStruct((R, n), BF16), grid=(R // tr,),
        in_specs=[pl.BlockSpec((tr, n), lambda i: (i, 0)), pl.BlockSpec((1, n), lambda i: (0, 0))],
        out_specs=pl.BlockSpec((tr, n), lambda i: (i, 0)), compiler_params=_cp(("parallel",)), name=name)(x, g)


def _rms_bwd(dh, x, g, dres, name, tr=256):
    R, n = x.shape
    tr = min(tr, R)
    need_dx = dres is not None

    def body(*refs):
        if need_dx:
            dh_ref, x_ref, g_ref, r_ref, dx_ref, dg_ref = refs
        else:
            dh_ref, x_ref, g_ref, dg_ref = refs
        i = pl.program_id(0)
        xv = x_ref[...]
        dhv = dh_ref[...].astype(F32)
        rstd = lax.rsqrt(jnp.mean(xv * xv, axis=-1, keepdims=True) + EPS)
        xhat = xv * rstd

        @pl.when(i == 0)
        def _():
            dg_ref[...] = jnp.zeros_like(dg_ref)

        dg_ref[...] += jnp.sum(dhv * xhat, axis=0, keepdims=True)
        if need_dx:
            t = dhv * g_ref[...]
            dx_ref[...] = r_ref[...] + rstd * (t - xhat * jnp.mean(t * xhat, axis=-1, keepdims=True))

    row = pl.BlockSpec((tr, n), lambda i: (i, 0))
    vec = pl.BlockSpec((1, n), lambda i: (0, 0))
    if need_dx:
        return pl.pallas_call(
            body, out_shape=(jax.ShapeDtypeStruct((R, n), F32), jax.ShapeDtypeStruct((1, n), F32)), grid=(R // tr,),
            in_specs=[row, row, vec, row], out_specs=(row, vec), compiler_params=_cp(("arbitrary",)), name=name)(dh, x, g, dres)
    return pl.pallas_call(
        body, out_shape=jax.ShapeDtypeStruct((1, n), F32), grid=(R // tr,),
        in_specs=[row, row, vec], out_specs=vec, compiler_params=_cp(("arbitrary",)), name=name)(dh, x, g)


def _loss_head(x, g, tgt, name, tr=256):
    R, n = x.shape

    def body(x_ref, g_ref, t_ref, loss_ref, dx_ref, dg_ref):
        i = pl.program_id(0)
        xv = x_ref[...]
        gv = g_ref[...]
        rstd = lax.rsqrt(jnp.mean(xv * xv, axis=-1, keepdims=True) + EPS)
        xhat = xv * rstd
        e = xhat * gv - t_ref[...]

        @pl.when(i == 0)
        def _():
            loss_ref[...] = jnp.zeros_like(loss_ref)
            dg_ref[...] = jnp.zeros_like(dg_ref)

        loss_ref[...] += 0.5 * jnp.sum(jnp.sum(e * e, axis=-1, keepdims=True) / n, axis=0, keepdims=True)
        dy = e / n
        dg_ref[...] += jnp.sum(dy * xhat, axis=0, keepdims=True)
        t = dy * gv
        dx_ref[...] = rstd * (t - xhat * jnp.mean(t * xhat, axis=-1, keepdims=True))

    row = pl.BlockSpec((tr, n), lambda i: (i, 0))
    vec = pl.BlockSpec((1, n), lambda i: (0, 0))
    one = pl.BlockSpec((1, 1), lambda i: (0, 0))
    return pl.pallas_call(
        body, out_shape=(jax.ShapeDtypeStruct((1, 1), F32), jax.ShapeDtypeStruct((R, n), F32), jax.ShapeDtypeStruct((1, n), F32)),
        grid=(R // tr,), in_specs=[row, vec, row], out_specs=(one, row, vec),
        compiler_params=_cp(("arbitrary",)), name=name)(x, g, tgt)


def _pool_masks(S):
    row = lax.broadcasted_iota(jnp.int32, (S, POOL_W), 0)
    grp = lax.broadcasted_iota(jnp.int32, (S, POOL_W), 1) // 64
    win = jnp.where(grp == 0, 2, jnp.where(grp == 1, 4, jnp.where(grp == 2, 8, 16)))
    cnt = jnp.minimum(row + 1, win).astype(F32)
    return row, grp, cnt


def _by_group(grp, v0, v1, v2, v3):
    return jnp.where(grp == 0, v0, jnp.where(grp == 1, v1, jnp.where(grp == 2, v2, v3)))


def _pool_fwd(proj, bd, scale, name):
    S = proj.shape[0]

    def body(a_ref, bd_ref, sc_ref, d_ref, y_ref):
        a = a_ref[...]
        row, grp, cnt = _pool_masks(S)

        def back(v, k):
            return jnp.where(row >= k, pltpu.roll(v, k, 0), 0.0)

        s1 = a + back(a, 1)
        s2 = s1 + back(s1, 2)
        s3 = s2 + back(s2, 4)
        s4 = s3 + back(s3, 8)
        d = (_by_group(grp, s1, s2, s3, s4) / cnt - a).astype(BF16)
        d_ref[...] = d
        y_ref[...] = (jnp.dot(d, bd_ref[...], preferred_element_type=F32) * sc_ref[...]).astype(BF16)

    full = lambda r, c: pl.BlockSpec((r, c), lambda i: (0, 0))
    return pl.pallas_call(
        body, out_shape=(jax.ShapeDtypeStruct((S, POOL_W), BF16), jax.ShapeDtypeStruct((S, POOL_W), BF16)), grid=(1,),
        in_specs=[pl.BlockSpec((S, POOL_W), lambda i: (0, P_A // POOL_W)), full(POOL_W, POOL_W), full(1, POOL_W)],
        out_specs=(full(S, POOL_W), full(S, POOL_W)), compiler_params=_cp(("arbitrary",)), name=name)(proj, bd, scale)


def _pool_bwd(dya, d, bd, scale, name):
    S = dya.shape[0]

    def body(dy_ref, d_ref, bd_ref, sc_ref, da_ref, dbd_ref, dsc_ref):
        dy = dy_ref[...]
        dv = d_ref[...]
        bdv = bd_ref[...]
        row, grp, cnt = _pool_masks(S)
        yraw = jnp.dot(dv, bdv, preferred_element_type=F32)
        dsc_ref[...] = jnp.sum(dy * yraw, axis=0, keepdims=True)
        tb = (dy * sc_ref[...]).astype(BF16)
        dbd_ref[...] = lax.dot_general(dv, tb, (((0,), (0,)), ((), ())), preferred_element_type=F32)
        dd = lax.dot_general(tb, bdv, (((1,), (1,)), ((), ())), preferred_element_type=F32)
        e = dd / cnt

        def fwd(v, k):
            return jnp.where(row < S - k, pltpu.roll(v, S - k, 0), 0.0)

        r1 = e + fwd(e, 1)
        r2 = r1 + fwd(r1, 2)
        r3 = r2 + fwd(r2, 4)
        r4 = r3 + fwd(r3, 8)
        da_ref[...] = (_by_group(grp, r1, r2, r3, r4) - dd).astype(BF16)

    full = lambda r, c: pl.BlockSpec((r, c), lambda i: (0, 0))
    return pl.pallas_call(
        body, out_shape=(jax.ShapeDtypeStruct((S, POOL_W), BF16), jax.ShapeDtypeStruct((POOL_W, POOL_W), F32),
                         jax.ShapeDtypeStruct((1, POOL_W), F32)), grid=(1,),
        in_specs=[full(S, POOL_W), full(S, POOL_W), full(POOL_W, POOL_W), full(1, POOL_W)],
        out_specs=(full(S, POOL_W), full(POOL_W, POOL_W), full(1, POOL_W)),
        compiler_params=_cp(("arbitrary",)), name=name)(dya, d, bd, scale)


FCOLS = 128


def _log_sigmoid(z):
    return -(jnp.maximum(-z, 0.0) + jnp.log1p(jnp.exp(-jnp.abs(z))))


def _fgate_fwd(proj, bf, name):
    S = proj.shape[0]

    def body(f_ref, b_ref, o_ref):
        v = _log_sigmoid(f_ref[...] + b_ref[...])
        row = lax.broadcasted_iota(jnp.int32, (S, FCOLS), 0)
        k = 1
        while k < S:
            v = v + jnp.where(row >= k, pltpu.roll(v, k, 0), 0.0)
            k *= 2
        o_ref[...] = v

    return pl.pallas_call(
        body, out_shape=jax.ShapeDtypeStruct((S, FCOLS), F32), grid=(1,),
        in_specs=[pl.BlockSpec((S, FCOLS), lambda i: (0, P_F // FCOLS)), pl.BlockSpec((1, FCOLS), lambda i: (0, 0))],
        out_specs=pl.BlockSpec((S, FCOLS), lambda i: (0, 0)), compiler_params=_cp(("arbitrary",)), name=name)(proj, bf)


def _fgate_bwd(dF, proj, bf, name):
    S = proj.shape[0]

    def body(dF_ref, f_ref, b_ref, df_ref, db_ref):
        v = dF_ref[...]
        row = lax.broadcasted_iota(jnp.int32, (S, FCOLS), 0)
        k = 1
        while k < S:
            v = v + jnp.where(row < S - k, pltpu.roll(v, S - k, 0), 0.0)
            k *= 2
        z = f_ref[...] + b_ref[...]
        df = v * (1.0 / (1.0 + jnp.exp(z)))
        db_ref[...] = jnp.sum(df, axis=0, keepdims=True)
        df_ref[...] = jnp.concatenate([df, jnp.zeros_like(df)], axis=1).astype(BF16)

    return pl.pallas_call(
        body, out_shape=(jax.ShapeDtypeStruct((S, 2 * FCOLS), BF16), jax.ShapeDtypeStruct((1, FCOLS), F32)), grid=(1,),
        in_specs=[pl.BlockSpec((S, FCOLS), lambda i: (0, 0)), pl.BlockSpec((S, FCOLS), lambda i: (0, P_F // FCOLS)),
                  pl.BlockSpec((1, FCOLS), lambda i: (0, 0))],
        out_specs=(pl.BlockSpec((S, 2 * FCOLS), lambda i: (0, 0)), pl.BlockSpec((1, FCOLS), lambda i: (0, 0))),
        compiler_params=_cp(("arbitrary",)), name=name)(dF, proj, bf)


def _fox_scores(qe, kj, fq, fk, r0, c0, tq, tk, diagonal):
    s = lax.dot_general(qe, kj, (((1,), (1,)), ((), ())), preferred_element_type=F32) * FOX_SCALE
    s = s + (fq - fk)
    if not diagonal:
        return s
    rows = r0 + lax.broadcasted_iota(jnp.int32, (tq, tk), 0)
    cols = c0 + lax.broadcasted_iota(jnp.int32, (tq, tk), 1)
    return jnp.where(rows >= cols, s, NEG)


def _fox_fwd(qkv, fcol, frow, name, tq=256):
    S = qkv.shape[0]
    tk = tq

    def body(q_ref, k_ref, v_ref, fc_ref, fr_ref, o_ref, o32_ref, lse_ref):
        i = pl.program_id(1)
        r0 = i * tq
        q = q_ref[...]
        half = lax.broadcasted_iota(jnp.int32, (tq, 128), 1) // 64
        qs = [jnp.where(half == e, q, jnp.zeros_like(q)) for e in (0, 1)]
        fqs = [fc_ref[0, :, e:e + 1] for e in (0, 1)]

        def step(j, carry, diagonal=False):
            c0 = pl.multiple_of(j * tk, tk)
            kj = k_ref[pl.ds(c0, tk), :]
            vj = v_ref[pl.ds(c0, tk), :]
            out = []
            for e in (0, 1):
                m, l, acc = carry[e]
                s = _fox_scores(qs[e], kj, fqs[e], fr_ref[0, e:e + 1, pl.ds(c0, tk)], r0, c0, tq, tk, diagonal)
                m_new = jnp.maximum(m, jnp.max(s, axis=-1, keepdims=True))
                alpha = jnp.exp(m - m_new)
                p = jnp.exp(s - m_new)
                out.append((m_new, alpha * l + jnp.sum(p, axis=-1, keepdims=True),
                            alpha * acc + jnp.dot(p.astype(BF16), vj, preferred_element_type=F32)))
            return tuple(out)

        init = (jnp.full((tq, 1), NEG, F32), jnp.zeros((tq, 1), F32), jnp.zeros((tq, 128), F32))
        carry = lax.fori_loop(0, i, step, (init, init))
        carry = step(i, carry, diagonal=True)
        outs = []
        for e in (0, 1):
            m, l, acc = carry[e]
            outs.append(acc / l)
            lse_ref[0, :, e:e + 1] = m + jnp.log(l)
        o = jnp.where(half == 0, outs[0], outs[1])
        o32_ref[...] = o
        o_ref[...] = o.astype(BF16)

    tile = pl.BlockSpec((tq, 128), lambda h, i: (i, h))
    return pl.pallas_call(
        body, out_shape=(jax.ShapeDtypeStruct((S, FOX_W), BF16), jax.ShapeDtypeStruct((S, FOX_W), F32),
                         jax.ShapeDtypeStruct((4, S, 2), F32)), grid=(4, S // tq),
        in_specs=[tile, pl.BlockSpec((S, 128), lambda h, i: (0, 4 + h)), pl.BlockSpec((S, 128), lambda h, i: (0, 8 + h)),
                  pl.BlockSpec((1, tq, 2), lambda h, i: (h, i, 0)), pl.BlockSpec((1, 2, S), lambda h, i: (h, 0, 0))],
        out_specs=(tile, tile, pl.BlockSpec((1, tq, 2), lambda h, i: (h, i, 0))),
        compiler_params=_cp(("parallel", "parallel")), name=name)(qkv, qkv, qkv, fcol, frow)


def _fox_bwd(qkv, o32, do, lse, fcol, frow, name, tq=256):
    S = qkv.shape[0]
    tk = tq
    nq = S // tq

    def body(q_ref, k_ref, v_ref, o_ref, do_ref, lse_ref, fc_ref, fr_ref, dq_ref, dk_ref, dv_ref, dfr_ref, dfc_ref, dk_acc, dv_acc):
        dk_acc[...] = jnp.zeros_like(dk_acc)
        dv_acc[...] = jnp.zeros_like(dv_acc)
        dfr_ref[...] = jnp.zeros_like(dfr_ref)
        half = lax.broadcasted_iota(jnp.int32, (tq, 128), 1) // 64

        def q_block(i, _):
            r0 = pl.multiple_of(i * tq, tq)
            qi = q_ref[pl.ds(r0, tq), :]
            dob = do_ref[pl.ds(r0, tq), :].astype(BF16)
            row_dot = dob.astype(F32) * o_ref[pl.ds(r0, tq), :]
            qs = [jnp.where(half == e, qi, jnp.zeros_like(qi)) for e in (0, 1)]
            dos = [jnp.where(half == e, dob, jnp.zeros_like(dob)) for e in (0, 1)]
            deltas = [jnp.sum(jnp.where(half == e, row_dot, 0.0), axis=-1, keepdims=True) for e in (0, 1)]
            lses = [lse_ref[0, pl.ds(r0, tq), e:e + 1] for e in (0, 1)]
            fqs = [fc_ref[0, pl.ds(r0, tq), e:e + 1] for e in (0, 1)]

            def step(j, carry, diagonal=False):
                dqs, row_sums = carry
                c0 = pl.multiple_of(j * tk, tk)
                kj = k_ref[pl.ds(c0, tk), :]
                vj = v_ref[pl.ds(c0, tk), :]
                new_dq, new_rows, dkc, dvc = [], [], [], []
                for e in (0, 1):
                    s = _fox_scores(qs[e], kj, fqs[e], fr_ref[0, e:e + 1, pl.ds(c0, tk)], r0, c0, tq, tk, diagonal)
                    p = jnp.exp(s - lses[e])
                    dp = lax.dot_general(dos[e], vj, (((1,), (1,)), ((), ())), preferred_element_type=F32)
                    ds = p * (dp - deltas[e])
                    dfr_ref[0, e:e + 1, pl.ds(c0, tk)] -= jnp.sum(ds, axis=0, keepdims=True)
                    new_rows.append(row_sums[e] + jnp.sum(ds, axis=-1, keepdims=True))
                    dsb = (ds * FOX_SCALE).astype(BF16)
                    dkc.append(lax.dot_general(dsb, qi, (((0,), (0,)), ((), ())), preferred_element_type=F32))
                    dvc.append(lax.dot_general(p.astype(BF16), dob, (((0,), (0,)), ((), ())), preferred_element_type=F32))
                    new_dq.append(dqs[e] + jnp.dot(dsb, kj, preferred_element_type=F32))
                dk_acc[pl.ds(c0, tk), :] += jnp.where(half == 0, dkc[0], dkc[1])
                dv_acc[pl.ds(c0, tk), :] += jnp.where(half == 0, dvc[0], dvc[1])
                return tuple(new_dq), tuple(new_rows)

            zero, zero_col = jnp.zeros((tq, 128), F32), jnp.zeros((tq, 1), F32)
            carry = lax.fori_loop(0, i, step, ((zero, zero), (zero_col, zero_col)))
            dqs, row_sums = step(i, carry, diagonal=True)
            for e in (0, 1):
                dfc_ref[0, pl.ds(r0, tq), e:e + 1] = row_sums[e]
            dq_ref[pl.ds(r0, tq), :] = jnp.where(half == 0, dqs[0], dqs[1]).astype(BF16)
            return 0

        lax.fori_loop(0, nq, q_block, 0)
        dk_ref[...] = dk_acc[...].astype(BF16)
        dv_ref[...] = dv_acc[...].astype(BF16)

    col = lambda off: pl.BlockSpec((S, 128), lambda h: (0, off + h))
    hs2 = pl.BlockSpec((1, S, 2), lambda h: (h, 0, 0))
    h2s = pl.BlockSpec((1, 2, S), lambda h: (h, 0, 0))
    return pl.pallas_call(
        body, out_shape=(jax.ShapeDtypeStruct((S, FOX_W), BF16),) * 3 + (jax.ShapeDtypeStruct((4, 2, S), F32),
                                                                         jax.ShapeDtypeStruct((4, S, 2), F32)), grid=(4,),
        in_specs=[col(0), col(4), col(8), col(0), col(0), hs2, hs2, h2s],
        out_specs=(col(0), col(0), col(0), h2s, hs2),
        scratch_shapes=[pltpu.VMEM((S, 128), F32), pltpu.VMEM((S, 128), F32)],
        compiler_params=_cp(("parallel",)), name=name)(qkv, qkv, qkv, o32, do, lse, fcol, frow)


def _gelu(x):
    return 0.5 * x * (1.0 + jnp.tanh(GELU_K * (x + GELU_C * x * x * x)))


def _gelu_grad(x):
    th = jnp.tanh(GELU_K * (x + GELU_C * x * x * x))
    return 0.5 * (1.0 + th) + 0.5 * x * (1.0 - th * th) * GELU_K * (1.0 + 3.0 * GELU_C * x * x)


def _sgu_parts(c, gn, w_ref, bias):
    zc = _gelu(c)
    u, vv = zc[:, :SGU_W], zc[:, SGU_W:]
    rstd = lax.rsqrt(jnp.mean(vv * vv, axis=-1, keepdims=True) + EPS)
    vhat = vv * rstd
    vnb = (vhat * gn).astype(BF16)
    grp = lax.broadcasted_iota(jnp.int32, (SGU_CHUNK, SGU_W), 1) // 64
    mixed = bias
    for gi in range(4):
        mixed = mixed + jnp.where(grp == gi, jnp.dot(w_ref[gi], vnb, preferred_element_type=F32), 0.0)
    return u, rstd, vhat, vnb, grp, mixed


def _sgu_fwd(proj, gn, wm, bias, name):
    S = proj.shape[0]

    def body(c_ref, g_ref, w_ref, b_ref, o_ref):
        u, _, _, _, _, mixed = _sgu_parts(c_ref[...], g_ref[...], w_ref, b_ref[...])
        o_ref[...] = (u * mixed).astype(BF16)

    return pl.pallas_call(
        body, out_shape=jax.ShapeDtypeStruct((S, SGU_W), BF16), grid=(S // SGU_CHUNK,),
        in_specs=[pl.BlockSpec((SGU_CHUNK, 2 * SGU_W), lambda i: (i, P_C // (2 * SGU_W))),
                  pl.BlockSpec((1, SGU_W), lambda i: (0, 0)), pl.BlockSpec((4, SGU_CHUNK, SGU_CHUNK), lambda i: (0, 0, 0)),
                  pl.BlockSpec((SGU_CHUNK, SGU_W), lambda i: (0, 0))],
        out_specs=pl.BlockSpec((SGU_CHUNK, SGU_W), lambda i: (i, 0)),
        compiler_params=_cp(("parallel",)), name=name)(proj, gn, wm, bias)


def _sgu_bwd(dsg, proj, gn, wm, wmt, bias, name):
    S = proj.shape[0]

    def body(dsg_ref, c_ref, g_ref, w_ref, wt_ref, b_ref, dc_ref, dw_ref, db_ref, dg_ref):
        i = pl.program_id(0)

        @pl.when(i == 0)
        def _():
            dw_ref[...] = jnp.zeros_like(dw_ref)
            db_ref[...] = jnp.zeros_like(db_ref)
            dg_ref[...] = jnp.zeros_like(dg_ref)

        c = c_ref[...]
        gn_v = g_ref[...]
        u, rstd, vhat, vnb, grp, mixed = _sgu_parts(c, gn_v, w_ref, b_ref[...])
        dsg_v = dsg_ref[...]
        du = dsg_v * mixed
        dmix = dsg_v * u
        db_ref[...] += dmix
        dmb = dmix.astype(BF16)
        dvn = jnp.zeros((SGU_CHUNK, SGU_W), F32)
        for gi in range(4):
            dmg = jnp.where(grp == gi, dmb, jnp.zeros_like(dmb))
            dw_ref[gi] += lax.dot_general(dmg, vnb, (((1,), (1,)), ((), ())), preferred_element_type=F32)
            dvn = dvn + jnp.where(grp == gi, jnp.dot(wt_ref[gi], dmb, preferred_element_type=F32), 0.0)
        dg_ref[...] += jnp.sum(dvn * vhat, axis=0, keepdims=True)
        t = dvn * gn_v
        dvv = rstd * (t - vhat * jnp.mean(t * vhat, axis=-1, keepdims=True))
        dc_ref[...] = (jnp.concatenate([du, dvv], axis=1) * _gelu_grad(c)).astype(BF16)

    w_spec = pl.BlockSpec((4, SGU_CHUNK, SGU_CHUNK), lambda i: (0, 0, 0))
    tile = pl.BlockSpec((SGU_CHUNK, SGU_W), lambda i: (0, 0))
    vec = pl.BlockSpec((1, SGU_W), lambda i: (0, 0))
    return pl.pallas_call(
        body, out_shape=(jax.ShapeDtypeStruct((S, 2 * SGU_W), BF16), jax.ShapeDtypeStruct((4, SGU_CHUNK, SGU_CHUNK), F32),
                         jax.ShapeDtypeStruct((SGU_CHUNK, SGU_W), F32), jax.ShapeDtypeStruct((1, SGU_W), F32)),
        grid=(S // SGU_CHUNK,),
        in_specs=[pl.BlockSpec((SGU_CHUNK, SGU_W), lambda i: (i, 0)),
                  pl.BlockSpec((SGU_CHUNK, 2 * SGU_W), lambda i: (i, P_C // (2 * SGU_W))), vec, w_spec, w_spec, tile],
        out_specs=(pl.BlockSpec((SGU_CHUNK, 2 * SGU_W), lambda i: (i, 0)), w_spec, tile, vec),
        compiler_params=_cp(("arbitrary",)), name=name)(dsg, proj, gn, wm, wmt, bias)


def _sigmoid(z):
    return 1.0 / (1.0 + jnp.exp(-z))


def _merge_specs(tm):
    row = lambda n: pl.BlockSpec((tm, n), lambda i: (i, 0))
    gate = lambda b: pl.BlockSpec((tm, D), lambda i: (i, b))
    full = lambda r, c: pl.BlockSpec((r, c), lambda i: (0, 0))
    packed = pl.BlockSpec((4, 256, PACK_COLS), lambda i: (0, R_BRANCH // 256, 0))
    return row, gate, full, packed


def _branch_shards(c_ref, j):
    return c_ref[j, :, 0:256], c_ref[j, :, 256:512], c_ref[j, :, 512:768], c_ref[j, :, 768:1024]


def _merge_fwd(proj, ya, o, sg, packed_w, bg, name, tm=256):
    S = proj.shape[0]
    row, gate, full, packed = _merge_specs(tm)

    def body(g0, g1, g2, ya_ref, o_ref, sg_ref, c_ref, bg_ref, out_ref):
        yav, ov, sgv = ya_ref[...], o_ref[...], sg_ref[...]
        for j in range(4):
            cols = slice(256 * j, 256 * (j + 1))
            wa, wb0, wb1, wc = _branch_shards(c_ref, j)
            y = (jnp.dot(yav, wa, preferred_element_type=F32),
                 jnp.dot(ov[:, :256], wb0, preferred_element_type=F32) + jnp.dot(ov[:, 256:], wb1, preferred_element_type=F32),
                 jnp.dot(sgv, wc, preferred_element_type=F32))
            acc = jnp.zeros((tm, 256), F32)
            for b, g_ref in enumerate((g0, g1, g2)):
                acc = acc + _sigmoid(g_ref[:, cols] + bg_ref[:, b * D + 256 * j:b * D + 256 * (j + 1)]) * y[b]
            out_ref[:, cols] = acc.astype(BF16)

    return pl.pallas_call(
        body, out_shape=jax.ShapeDtypeStruct((S, D), BF16), grid=(S // tm,),
        in_specs=[gate(0), gate(1), gate(2), row(POOL_W), row(FOX_W), row(SGU_W), packed, full(1, 3 * D)],
        out_specs=row(D), compiler_params=_cp(("parallel",)), name=name)(proj, proj, proj, ya, o, sg, packed_w, bg)


def _merge_bwd(dm, proj, ya, o, sg, packed_w, bg, grads, name, tm=256):
    S = proj.shape[0]
    row, gate, full, packed = _merge_specs(tm)
    tn_dims = (((0,), (0,)), ((), ()))
    nt_dims = (((1,), (1,)), ((), ()))

    def body(dm_ref, g0, g1, g2, ya_ref, o_ref, sg_ref, c_ref, bg_ref, _, dg_ref, dya_ref, do_ref, dsg_ref, dc_ref, dbg_ref, acc):
        i = pl.program_id(0)

        @pl.when(i == 0)
        def _():
            acc[...] = jnp.zeros_like(acc)
            dbg_ref[...] = jnp.zeros_like(dbg_ref)

        yav, ov, sgv = ya_ref[...], o_ref[...], sg_ref[...]
        o0, o1 = ov[:, :256], ov[:, 256:]
        dya = jnp.zeros((tm, POOL_W), F32)
        do0 = jnp.zeros((tm, 256), F32)
        do1 = jnp.zeros((tm, 256), F32)
        dsg = jnp.zeros((tm, SGU_W), F32)
        for j in range(4):
            cols = slice(256 * j, 256 * (j + 1))
            wa, wb0, wb1, wc = _branch_shards(c_ref, j)
            y = (jnp.dot(yav, wa, preferred_element_type=F32),
                 jnp.dot(o0, wb0, preferred_element_type=F32) + jnp.dot(o1, wb1, preferred_element_type=F32),
                 jnp.dot(sgv, wc, preferred_element_type=F32))
            dmv = dm_ref[:, cols]
            dy = []
            for b, g_ref in enumerate((g0, g1, g2)):
                bcols = slice(b * D + 256 * j, b * D + 256 * (j + 1))
                gt = _sigmoid(g_ref[:, cols] + bg_ref[:, bcols])
                dgp = dmv * y[b] * gt * (1.0 - gt)
                dg_ref[:, bcols] = dgp.astype(BF16)
                dbg_ref[:, bcols] += jnp.sum(dgp, axis=0, keepdims=True)
                dy.append((dmv * gt).astype(BF16))
            dya = dya + lax.dot_general(dy[0], wa, nt_dims, preferred_element_type=F32)
            do0 = do0 + lax.dot_general(dy[1], wb0, nt_dims, preferred_element_type=F32)
            do1 = do1 + lax.dot_general(dy[1], wb1, nt_dims, preferred_element_type=F32)
            dsg = dsg + lax.dot_general(dy[2], wc, nt_dims, preferred_element_type=F32)
            acc[j, :, 0:256] += lax.dot_general(yav, dy[0], tn_dims, preferred_element_type=F32)
            acc[j, :, 256:512] += lax.dot_general(o0, dy[1], tn_dims, preferred_element_type=F32)
            acc[j, :, 512:768] += lax.dot_general(o1, dy[1], tn_dims, preferred_element_type=F32)
            acc[j, :, 768:1024] += lax.dot_general(sgv, dy[2], tn_dims, preferred_element_type=F32)
        dya_ref[...] = dya
        do_ref[:, :256] = do0
        do_ref[:, 256:] = do1
        dsg_ref[...] = dsg

        @pl.when(i == pl.num_programs(0) - 1)
        def _():
            dc_ref[...] = acc[...].astype(dc_ref.dtype)

    return pl.pallas_call(
        body, out_shape=(jax.ShapeDtypeStruct((S, 3 * D), BF16), jax.ShapeDtypeStruct((S, POOL_W), F32),
                         jax.ShapeDtypeStruct((S, FOX_W), F32), jax.ShapeDtypeStruct((S, SGU_W), F32),
                         jax.ShapeDtypeStruct(grads.shape, grads.dtype), jax.ShapeDtypeStruct((1, 3 * D), F32)),
        grid=(S // tm,),
        in_specs=[row(D), gate(0), gate(1), gate(2), row(POOL_W), row(FOX_W), row(SGU_W), packed, full(1, 3 * D), ANY],
        out_specs=(row(3 * D), row(POOL_W), row(FOX_W), row(SGU_W), packed, full(1, 3 * D)),
        scratch_shapes=[pltpu.VMEM((4, 256, PACK_COLS), F32)], input_output_aliases={9: 4},
        compiler_params=_cp(("arbitrary",)), name=name)(dm, proj, proj, proj, ya, o, sg, packed_w, bg, grads)


def _xattn_probs(qh, kh):
    s = lax.dot_general(qh, kh, (((1,), (1,)), ((), ())), preferred_element_type=F32) * X_SCALE
    p = jnp.exp(s - jnp.max(s, axis=-1, keepdims=True))
    return p / jnp.sum(p, axis=-1, keepdims=True)


def _xattn_fwd(xq, kv, name, tq=256):
    S = xq.shape[0]
    M = kv.shape[0]

    def body(q_ref, k_ref, v_ref, o_ref):
        for h in range(XH):
            sl = slice(h * XHD, (h + 1) * XHD)
            p = _xattn_probs(q_ref[:, sl], k_ref[:, sl])
            o_ref[:, sl] = jnp.dot(p.astype(BF16), v_ref[:, sl], preferred_element_type=F32).astype(BF16)

    return pl.pallas_call(
        body, out_shape=jax.ShapeDtypeStruct((S, D), BF16), grid=(S // tq,),
        in_specs=[pl.BlockSpec((tq, D), lambda i: (i, 0)), pl.BlockSpec((M, D), lambda i: (0, 0)),
                  pl.BlockSpec((M, D), lambda i: (0, 1))],
        out_specs=pl.BlockSpec((tq, D), lambda i: (i, 0)), compiler_params=_cp(("parallel",)), name=name)(xq, kv, kv)


def _xattn_bwd(xq, kv, do, name, tq=256):
    S = xq.shape[0]
    M = kv.shape[0]

    def body(q_ref, k_ref, v_ref, do_ref, dq_ref, dkv_ref, dk_acc, dv_acc):
        i = pl.program_id(0)

        @pl.when(i == 0)
        def _():
            dk_acc[...] = jnp.zeros_like(dk_acc)
            dv_acc[...] = jnp.zeros_like(dv_acc)

        for h in range(XH):
            sl = slice(h * XHD, (h + 1) * XHD)
            qh, kh, vh, doh = q_ref[:, sl], k_ref[:, sl], v_ref[:, sl], do_ref[:, sl]
            p = _xattn_probs(qh, kh)
            dp = lax.dot_general(doh, vh, (((1,), (1,)), ((), ())), preferred_element_type=F32)
            ds = p * (dp - jnp.sum(p * dp, axis=-1, keepdims=True))
            dsb = (ds * X_SCALE).astype(BF16)
            dq_ref[:, sl] = jnp.dot(dsb, kh, preferred_element_type=F32).astype(BF16)
            dk_acc[:, sl] += lax.dot_general(dsb, qh, (((0,), (0,)), ((), ())), preferred_element_type=F32)
            dv_acc[:, sl] += lax.dot_general(p.astype(BF16), doh, (((0,), (0,)), ((), ())), preferred_element_type=F32)

        @pl.when(i == pl.num_programs(0) - 1)
        def _():
            dkv_ref[:, :D] = dk_acc[...].astype(BF16)
            dkv_ref[:, D:] = dv_acc[...].astype(BF16)

    return pl.pallas_call(
        body, out_shape=(jax.ShapeDtypeStruct((S, D), BF16), jax.ShapeDtypeStruct((M, 2 * D), BF16)), grid=(S // tq,),
        in_specs=[pl.BlockSpec((tq, D), lambda i: (i, 0)), pl.BlockSpec((M, D), lambda i: (0, 0)),
                  pl.BlockSpec((M, D), lambda i: (0, 1)), pl.BlockSpec((tq, D), lambda i: (i, 0))],
        out_specs=(pl.BlockSpec((tq, D), lambda i: (i, 0)), pl.BlockSpec((M, 2 * D), lambda i: (0, 0))),
        scratch_shapes=[pltpu.VMEM((M, D), F32), pltpu.VMEM((M, D), F32)],
        compiler_params=_cp(("arbitrary",)), name=name)(xq, kv, kv, do)


def _adam_math(gv, wv, mv, vv):
    c1 = 1.0 - ADAM_B1 ** ADAM_STEP
    c2 = 1.0 - ADAM_B2 ** ADAM_STEP
    nm = ADAM_B1 * mv + (1.0 - ADAM_B1) * gv
    nv = ADAM_B2 * vv + (1.0 - ADAM_B2) * (gv * gv)
    return -ADAM_LR * ((nm / c1) / (jnp.sqrt(nv / c2) + ADAM_EPS) + ADAM_WD * wv), nm, nv


def _adamw(g, w, m, v, name, block=None):
    if block is None:
        block = (1, 256 if g.shape[1] % 256 == 0 else g.shape[1], g.shape[2])
    grid = tuple(s // b for s, b in zip(g.shape, block))

    def body(g_ref, w_ref, m_ref, v_ref, d_ref, nm_ref, nv_ref):
        d_ref[...], nm_ref[...], nv_ref[...] = _adam_math(g_ref[...], w_ref[...], m_ref[...], v_ref[...])

    blk = pl.BlockSpec(block, lambda a, b, c: (a, b, c))
    return pl.pallas_call(
        body, out_shape=(jax.ShapeDtypeStruct(g.shape, F32),) * 3, grid=grid,
        in_specs=[blk] * 4, out_specs=(blk,) * 3, compiler_params=_cp(("parallel",) * 3), name=name)(g, w, m, v)


def _adamw_packed(red, w, m, v, g_index, name, token, tr=256):
    L, r, c = w.shape
    tr = min(tr, r)

    def body(g0_ref, g1_ref, w_ref, m_ref, v_ref, _, g_ref, d_ref, nm_ref, nv_ref):
        gv = jnp.where(pl.program_id(0) == 0, g0_ref[...], g1_ref[...])
        g_ref[0] = gv
        d_ref[0], nm_ref[0], nv_ref[0] = _adam_math(gv, w_ref[0], m_ref[0], v_ref[0])

    gblk = pl.BlockSpec((tr, c), lambda l, i: g_index(i))
    blk = pl.BlockSpec((1, tr, c), lambda l, i: (l, i, 0))
    return pl.pallas_call(
        body, out_shape=(jax.ShapeDtypeStruct(w.shape, F32),) * 4, grid=(L, r // tr),
        in_specs=[gblk, gblk, blk, blk, blk, pl.BlockSpec((8, 128), lambda l, i: (0, 0))], out_specs=(blk,) * 4,
        compiler_params=_cp(("parallel", "parallel")), name=name)(red[0], red[1], w, m, v, token)


def _row_tile(R):
    return next((t for t in (512, 496, 384, 256) if R % t == 0), R)


def _sum_slots(a, out_dtype, name):
    n, R, C = a.shape
    tr = _row_tile(R)

    def body(a_ref, o_ref):
        acc = a_ref[0].astype(F32)
        for k in range(1, n):
            acc = acc + a_ref[k].astype(F32)
        o_ref[...] = acc.astype(out_dtype)

    return pl.pallas_call(
        body, out_shape=jax.ShapeDtypeStruct((R, C), out_dtype), grid=(R // tr,),
        in_specs=[pl.BlockSpec((n, tr, C), lambda i: (0, i, 0))], out_specs=pl.BlockSpec((tr, C), lambda i: (i, 0)),
        compiler_params=_cp(("parallel",)), name=name)(a)


def _add_pair(a, b, name):
    n, R, C = a.shape
    tr = _row_tile(R)

    def body(a_ref, b_ref, o_ref):
        o_ref[...] = (a_ref[...].astype(F32) + b_ref[...].astype(F32)).astype(BF16)

    blk = pl.BlockSpec((1, tr, C), lambda k, i: (k, i, 0))
    return pl.pallas_call(
        body, out_shape=jax.ShapeDtypeStruct(a.shape, BF16), grid=(n, R // tr), in_specs=[blk, blk], out_specs=blk,
        compiler_params=_cp(("parallel", "parallel")), name=name)(a, b)


LANDING = pl.BlockSpec(memory_space=pltpu.VMEM)


def _landing_params(shape, dtype):
    return pltpu.CompilerParams(vmem_limit_bytes=math.prod(shape) * jnp.dtype(dtype).itemsize + 4 * 1024 * 1024)


def _place():
    return lax.axis_index("x"), lax.axis_index("y"), lax.axis_index("c")


def _other_chips(x, y):
    return [(1 - x, y), (x, 1 - y), (1 - x, 1 - y)]


def _row_chunks(rows, want, align=16):
    n = want
    while n > 1 and rows % (n * align):
        n -= 1
    return n


def _pair_split(g, name, nch=5):
    n, R, C = g.shape
    half = R // 2
    nch = _row_chunks(half, nch)
    cr = half // nch

    def body(g_ref, own_ref, got_ref, send_sems, recv_sems, local_sem):
        x, y, c = _place()
        mine0 = pl.multiple_of(c * half, 16)
        theirs0 = (1 - c) * half
        keep = pltpu.make_async_copy(g_ref.at[:, pl.ds(mine0, half), :], own_ref, local_sem)
        keep.start()
        cps = []
        for s in range(n):
            for q in range(nch):
                src = g_ref.at[s, pl.ds(pl.multiple_of(theirs0 + q * cr, 16), cr), :]
                cps.append(pltpu.make_async_remote_copy(
                    src_ref=src, dst_ref=got_ref.at[s, pl.ds(q * cr, cr), :], send_sem=send_sems.at[s * nch + q],
                    recv_sem=recv_sems.at[s * nch + q], device_id=(x, y, 1 - c), device_id_type=MESH))
        for cp in cps:
            cp.start()
        for cp in cps:
            cp.wait()
        keep.wait()

    sh = jax.ShapeDtypeStruct((n, half, C), g.dtype)
    return pl.pallas_call(
        body, out_shape=(sh, sh), in_specs=[ANY], out_specs=(ANY, LANDING),
        scratch_shapes=[pltpu.SemaphoreType.DMA((n * nch,)), pltpu.SemaphoreType.DMA((n * nch,)), pltpu.SemaphoreType.DMA],
        compiler_params=_landing_params(sh.shape, g.dtype), name=name)(g)


def _pair_gather(t, name, nch=10):
    R = t.shape[0]
    nch = _row_chunks(R, nch, 8)
    cr = R // nch

    def body(t_ref, o_ref, send_sems, recv_sems, local_sem):
        x, y, c = _place()
        own = pltpu.make_async_copy(t_ref, o_ref.at[c], local_sem)
        own.start()
        cps = [pltpu.make_async_remote_copy(src_ref=t_ref.at[pl.ds(q * cr, cr), :], dst_ref=o_ref.at[c, pl.ds(q * cr, cr), :],
                                            send_sem=send_sems.at[q], recv_sem=recv_sems.at[q], device_id=(x, y, 1 - c),
                                            device_id_type=MESH) for q in range(nch)]
        for cp in cps:
            cp.start()
        for cp in cps:
            cp.wait()
        own.wait()

    return pl.pallas_call(
        body, out_shape=jax.ShapeDtypeStruct((2,) + t.shape, t.dtype), in_specs=[ANY], out_specs=LANDING,
        scratch_shapes=[pltpu.SemaphoreType.DMA((nch,)), pltpu.SemaphoreType.DMA((nch,)), pltpu.SemaphoreType.DMA],
        compiler_params=_landing_params((2,) + t.shape, t.dtype), name=name)(t)


HBM = pl.BlockSpec(memory_space=pltpu.HBM)
SEM = pl.BlockSpec(memory_space=pltpu.SEMAPHORE)
SPLIT_COPY = pltpu.CompilerParams(has_side_effects=pltpu.SideEffectType.DATAFLOW_SIDE_EFFECTING)


def _split_exchange(src, rows, src_of, tag, nch=5):
    C = src.shape[-1]
    nch = _row_chunks(rows, nch)
    cr = rows // nch
    n = 3 * nch
    land_shape = (4, rows, C)

    def copies(src_ref, land_ref, send_sems, recv_sems):
        x, y, c = _place()
        j = 2 * x + y
        out = []
        for q in range(nch):
            for k, (px, py) in enumerate(_other_chips(x, y)):
                out.append(pltpu.make_async_remote_copy(
                    src_ref=src_of(src_ref, px, py, c, q * cr, cr), dst_ref=land_ref.at[j, pl.ds(q * cr, cr), :],
                    send_sem=send_sems.at[k * nch + q], recv_sem=recv_sems.at[k * nch + q], device_id=(px, py, c),
                    device_id_type=MESH))
        return out

    def start(src_ref, land_ref, send_sems, recv_sems, src_thru, land_thru, token):
        for cp in copies(src_ref, land_ref, send_sems, recv_sems):
            cp.start()
        token[...] = jnp.zeros_like(token)

    send_sems, recv_sems, src_thru, land_thru, token = pl.pallas_call(
        start, name=f"{tag}_start",
        out_shape=(pltpu.SemaphoreType.DMA((n,)), pltpu.SemaphoreType.DMA((n,)), pltpu.HBM(src.shape, src.dtype),
                   pltpu.HBM(land_shape, src.dtype), jax.ShapeDtypeStruct((8, 128), F32)),
        in_specs=(HBM, HBM), out_specs=(SEM, SEM, HBM, HBM, pl.BlockSpec(memory_space=pltpu.VMEM)),
        input_output_aliases={0: 2, 1: 3}, compiler_params=SPLIT_COPY)(
            pltpu.with_memory_space_constraint(src, pltpu.HBM),
            pltpu.with_memory_space_constraint(lax.empty(land_shape, src.dtype), pltpu.HBM))

    def finish(after):
        def wait(src_ref, land_ref, send_sems, recv_sems, after_ref, src_dead, got_ref):
            for cp in copies(src_ref, land_ref, send_sems, recv_sems):
                cp.wait_send()
                cp.wait_recv()

        return pl.pallas_call(
            wait, name=f"{tag}_wait", out_shape=(pltpu.HBM(src.shape, src.dtype), pltpu.HBM(land_shape, src.dtype)),
            in_specs=(HBM, HBM, SEM, SEM, ANY), out_specs=(HBM, HBM), input_output_aliases={0: 0, 1: 1},
            compiler_params=SPLIT_COPY)(src_thru, land_thru, send_sems, recv_sems, after)

    return token, finish


def _gather_finish(shard, land, name, nch=5):
    R, C = shard.shape
    half = R // 2
    nch = _row_chunks(half, nch)
    cr = half // nch

    def body(s_ref, l_ref, o_ref, send_sems, recv_sems, local_sems):
        x, y, c = _place()
        j = 2 * x + y
        mine0 = c * half
        local = [pltpu.make_async_copy(s_ref, o_ref.at[j], local_sems.at[0])]
        remote = []
        for k, (px, py) in enumerate(_other_chips(x, y)):
            jj = 2 * px + py
            local.append(pltpu.make_async_copy(l_ref.at[jj], o_ref.at[jj, pl.ds(pl.multiple_of(mine0, 16), half), :],
                                               local_sems.at[1 + k]))
            for q in range(nch):
                remote.append(pltpu.make_async_remote_copy(
                    src_ref=l_ref.at[jj, pl.ds(q * cr, cr), :],
                    dst_ref=o_ref.at[jj, pl.ds(pl.multiple_of(mine0 + q * cr, 16), cr), :], send_sem=send_sems.at[k * nch + q],
                    recv_sem=recv_sems.at[k * nch + q], device_id=(x, y, 1 - c), device_id_type=MESH))
        for cp in local + remote:
            cp.start()
        for cp in remote + local:
            cp.wait()

    return pl.pallas_call(
        body, out_shape=jax.ShapeDtypeStruct((4, R, C), shard.dtype), in_specs=[ANY, ANY], out_specs=LANDING,
        scratch_shapes=[pltpu.SemaphoreType.DMA((3 * nch,)), pltpu.SemaphoreType.DMA((3 * nch,)), pltpu.SemaphoreType.DMA((4,))],
        compiler_params=_landing_params((4, R, C), shard.dtype), name=name)(shard, land)


def _sum_slots_own(land, own, name):
    n, R, C = land.shape
    tr = _row_tile(R)
    me = (2 * lax.axis_index("x") + lax.axis_index("y")).astype(jnp.int32).reshape(1)
    if own.ndim == 3:
        own_spec = pl.BlockSpec((None, tr, C), lambda i, me: (me[0], i, 0))
    else:
        own_spec = pl.BlockSpec((tr, C), lambda i, me: (i, 0))

    def body(me_ref, land_ref, own_ref, o_ref):
        acc = None
        for k in range(n):
            v = jnp.where(me_ref[0] == k, own_ref[...], land_ref[k]).astype(F32)
            acc = v if acc is None else acc + v
        o_ref[...] = acc

    return pl.pallas_call(
        body, out_shape=jax.ShapeDtypeStruct((R, C), F32),
        grid_spec=pltpu.PrefetchScalarGridSpec(
            num_scalar_prefetch=1, grid=(R // tr,),
            in_specs=[pl.BlockSpec((n, tr, C), lambda i, me: (0, i, 0)), own_spec],
            out_specs=pl.BlockSpec((tr, C), lambda i, me: (i, 0))),
        compiler_params=_cp(("parallel",)), name=name)(me, land, own)


def _reduce_begin(g, tag):
    own, got = _pair_split(g, f"rs_pair_{tag}")
    p = _add_pair(own, got, f"rs_add_{tag}")
    token, finish = _split_exchange(p, p.shape[1], lambda ref, px, py, c, r0, cr: ref.at[2 * px + py, pl.ds(r0, cr), :],
                                    f"rs_a2a_{tag}")
    return (finish, g.shape, tag), token


def _reduce_end(state, after):
    finish, shape, tag = state
    p, land = finish(after)
    t = _sum_slots_own(land, p, f"rs_sum_{tag}")
    return _pair_gather(t, f"rs_join_{tag}").reshape(shape[1], shape[2])


def _all_reduce_begin(v, tag):
    p = _sum_slots(_pair_gather(v, f"ar_pair_{tag}"), F32, f"ar_add_{tag}")
    token, finish = _split_exchange(p, p.shape[0], lambda ref, px, py, c, r0, cr: ref.at[pl.ds(r0, cr), :], f"ar_a2a_{tag}")
    return (finish, tag), token


def _all_reduce_end(state, after):
    finish, tag = state
    p, land = finish(after)
    return _sum_slots_own(land, p, f"ar_sum_{tag}")


def _gather_begin(shard, tag):
    half = shard.shape[0] // 2
    token, finish = _split_exchange(
        shard, half, lambda ref, px, py, c, r0, cr: ref.at[pl.ds(pl.multiple_of(c * half + r0, 16), cr), :], f"gather_{tag}")
    return (finish, tag), token


def _gather_end(state, after):
    finish, tag = state
    shard, land = finish(after)
    return _gather_finish(shard, land, f"gather_{tag}_finish")


R_BRANCH, R_OUT, R_WIN, ROWS_A = 0, 256, 512, 1920
R_FF1, R_FF2, R_XKV, R_XQ, R_XO, ROWS_B = 0, 1024, 2048, 2560, 2816, 3072
WIN_ROWS = N_IN // 4
WIN_PAD = NP // 4


def _w_in_t(a):
    return jnp.transpose(a, (2, 0, 1))


def _pack_shard(w, l):
    xkv, wb = w['w_xkv'][l], w['w_branch_b'][l]
    a = [jnp.concatenate([w['w_branch_a'][l], wb[:256], wb[256:], w['w_branch_c'][l]], axis=1), w['w_out'][l],
         jnp.pad(_w_in_t(w['w_in'])[:, l, :], ((0, ROWS_A - R_WIN - WIN_ROWS), (0, 0)))]
    b = [w['w_ff1'][l], w['w_ff2'][l], jnp.concatenate([xkv[:512], xkv[512:]], axis=1), w['w_xq'][l], w['w_xo'][l]]
    return jnp.concatenate(a, axis=0).astype(BF16), jnp.concatenate(b, axis=0).astype(BF16)


def _w_in_rows(gathered):
    t = gathered[:, R_WIN:R_WIN + WIN_ROWS, :].reshape(N_IN, PACK_COLS)
    return jnp.concatenate([t[2312:5384], t[256:1792], t[1800:2312], t[0:256],
                            jnp.pad(t[1792:1800], ((0, NP - P_F - 8), (0, 0)))], axis=0)


def _w_in_by_shard(gathered):
    return gathered[:, R_WIN:R_WIN + WIN_PAD, :].reshape(NP, PACK_COLS)


def _cols_by_shard(parts):
    cols = jnp.concatenate(parts, axis=1)
    S = cols.shape[0]
    return jnp.pad(cols.reshape(S, 4, WIN_ROWS), ((0, 0), (0, 0), (0, WIN_PAD - WIN_ROWS))).reshape(S, NP)


def _place_w_in_rows(grads, rows, name):
    def body(rows_ref, _, out_ref, sem):
        cp = pltpu.make_async_copy(rows_ref, out_ref.at[:, pl.ds(R_WIN, WIN_PAD), :], sem)
        cp.start()
        cp.wait()

    return pl.pallas_call(
        body, out_shape=jax.ShapeDtypeStruct(grads.shape, grads.dtype), in_specs=[ANY, ANY], out_specs=ANY,
        scratch_shapes=[pltpu.SemaphoreType.DMA], input_output_aliases={1: 0}, name=name)(rows, grads)


def _small_prep(sw, l):
    eye = jnp.eye(4, dtype=F32)
    bd = jnp.einsum('gh,gcd->gchd', eye, sw['pool_w'][l]).reshape(POOL_W, POOL_W).astype(BF16)
    tril = jnp.tril(jnp.ones((SGU_CHUNK, SGU_CHUNK), F32))
    wm = (sw['sgu_w'][l] * tril[None]).astype(BF16)
    return dict(
        g_mix=sw['norm_mix_g'][l][None], g_x=sw['norm_xattn_g'][l][None], g_mem=sw['norm_mem_g'][l][None],
        g_ffn=sw['norm_ffn_g'][l][None], bd=bd, pool_scale=sw['pool_scale'][l][None],
        bf=jnp.pad(sw['b_forget'][l], (0, FCOLS - 8))[None], sgu_g=sw['sgu_norm_g'][l][None], wm=wm,
        wmt=jnp.transpose(wm, (0, 2, 1)), sgu_bias=jnp.repeat(sw['sgu_b'][l].T, 64, axis=1), bg=sw['b_gate'][l][None])


def _rows4(r0):
    return dict(n=D, k=D, b_block=(4, 256, 512), b_index=lambda i, j, k: (0, r0 // 256, j))


def _rows_t(r0):
    return dict(tb=True, n=D, k=D, tn=256, b_block=(None, 256, PACK_COLS), b_index=lambda i, j, k: (j, r0 // 256, 0))


def _rows_grad(r0):
    return dict(ta=True, tm=256, tn=512, o_block=(None, 256, 512), o_index=lambda i, j, k: (i, r0 // 256, j))


def _add_to(r, e):
    return e + r


def _after(v, token):
    return v if token is None else v + token[0, 0]


def _layer_fwd(x, mem, GA, w_in_t, sp, l, token, second):
    t = f"l{l}"
    S = x.shape[0]
    h = _rms_fwd(x, _after(sp['g_mix'], token), f"rms_mix_{t}")
    proj = _mm(h, w_in_t, name=f"proj_{t}", out_dtype=F32, tb=True)
    d, ya = _pool_fwd(proj, sp['bd'], sp['pool_scale'], f"pool_fwd_{t}")
    fcum = _fgate_fwd(proj, sp['bf'], f"fgate_fwd_{t}")
    f8 = fcum[:, :8]
    fcol = f8.reshape(S, 4, 2).transpose(1, 0, 2)
    frow = f8.T.reshape(4, 2, S)
    qkv = proj[:, P_Q:P_Q + 3 * FOX_W].astype(BF16)
    o, o32, lse = _fox_fwd(qkv, fcol, frow, f"fox_fwd_{t}")
    sg = _sgu_fwd(proj, sp['sgu_g'], sp['wm'], sp['sgu_bias'], f"sgu_fwd_{t}")
    merged = _merge_fwd(proj, ya, o, sg, GA, sp['bg'], f"merge_fwd_{t}")
    x1 = _mm(merged, GA, name=f"out_{t}", out_dtype=F32, extra=x, epi=_add_to, **_rows4(R_OUT))
    GB, token = second(x1)
    hx = _rms_fwd(x1, _after(sp['g_x'], token), f"rms_x_{t}")
    hm = _rms_fwd(mem, sp['g_mem'], f"rms_mem_{t}")
    xq = _mm(hx, GB, name=f"xq_{t}", out_dtype=BF16, **_rows4(R_XQ))
    kv = _mm(hm, GB, name=f"xkv_{t}", out_dtype=BF16, n=2 * D, k=D, tn=512, tk=512, b_block=(None, 512, 512),
             b_index=lambda i, j, k: (j, R_XKV // 512, k))
    o2 = _xattn_fwd(xq, kv, f"xattn_fwd_{t}")
    x2 = _mm(o2, GB, name=f"xo_{t}", out_dtype=F32, extra=x1, epi=_add_to, **_rows4(R_XO))
    hf = _rms_fwd(x2, sp['g_ffn'], f"rms_ffn_{t}")
    z = _mm(hf, GB, name=f"ff1_{t}", out_dtype=F32, n=D_FF, k=D, tn=512, b_block=(None, 1024, 512),
            b_index=lambda i, j, k: (j // 2, R_FF1 // 1024, j % 2))
    x3 = _mm(z, GB, name=f"ff2_{t}", out_dtype=F32, a_fn=_relu2, extra=x2, epi=_add_to, n=D, k=D_FF, tk=1024,
             b_block=(None, 1024, 512), b_index=lambda i, j, k: (k, R_FF2 // 1024, j))
    saved = dict(x=x, h=h, proj=proj, d=d, ya=ya, fcol=fcol, frow=frow, qkv=qkv, o=o, o32=o32, lse=lse, sg=sg, merged=merged,
                 x1=x1, hx=hx, hm=hm, xq=xq, kv=kv, o2=o2, x2=x2, hf=hf, z=z, GA=GA, GB=GB)
    return x3, saved


def _layer_bwd(dx3, mem, sp, sv, l, token, early):
    t = f"l{l}"
    S = dx3.shape[0]
    GA, GB = sv['GA'], sv['GB']
    gs = {}
    dx3 = _after(dx3, token)
    gb = lax.empty((4, ROWS_B, PACK_COLS), BF16)
    dz = _mm(dx3, GB, name=f"d_a2_{t}", out_dtype=BF16, tb=True, n=D_FF, k=D, tn=512, b_block=(None, 512, PACK_COLS),
             b_index=lambda i, j, k: (j // 2, R_FF2 // 512 + j % 2, 0), extra=sv['z'],
             epi=lambda r, e: r * (2.0 * jnp.maximum(e, 0.0)))
    gb = _mm(sv['z'], dx3, name=f"dw_ff2_{t}", out_dtype=BF16, ta=True, a_fn=_relu2, into=gb, tm=1024, tn=512,
             o_block=(None, 1024, 512), o_index=lambda i, j, k: (i, R_FF2 // 1024, j))
    gb = _mm(sv['hf'], dz, name=f"dw_ff1_{t}", out_dtype=BF16, ta=True, into=gb, tm=1024, tn=512,
             o_block=(None, 1024, 512), o_index=lambda i, j, k: (j // 2, R_FF1 // 1024, j % 2))
    dhf = _mm(dz, GB, name=f"d_hf_{t}", out_dtype=F32, tb=True, n=D, k=D_FF, tn=512, tk=1024, b_block=(None, 512, PACK_COLS),
              b_index=lambda i, j, k: (k, R_FF1 // 512 + j, 0))
    dx2, gs['norm_ffn_g'] = _rms_bwd(dhf, sv['x2'], sp['g_ffn'], dx3, f"rms_ffn_bwd_{t}")
    do2 = _mm(dx2, GB, name=f"d_o2_{t}", out_dtype=BF16, **_rows_t(R_XO))
    gb = _mm(sv['o2'], dx2, name=f"dw_xo_{t}", out_dtype=BF16, into=gb, **_rows_grad(R_XO))
    dxq, dkv = _xattn_bwd(sv['xq'], sv['kv'], do2, f"xattn_bwd_{t}")
    gb = _mm(sv['hm'], dkv, name=f"dw_xkv_{t}", out_dtype=BF16, ta=True, into=gb, tm=512, tn=512,
             o_block=(None, 512, 512), o_index=lambda i, j, k: (j, R_XKV // 512, i))
    dhm = _mm(dkv, GB, name=f"d_hm_{t}", out_dtype=F32, tb=True, n=D, k=2 * D, tn=512, tk=512, b_block=(None, 512, 512),
              b_index=lambda i, j, k: (k, R_XKV // 512, j))
    gs['norm_mem_g'] = _rms_bwd(dhm, mem, sp['g_mem'], None, f"rms_mem_bwd_{t}")
    gb = _mm(sv['hx'], dxq, name=f"dw_xq_{t}", out_dtype=BF16, into=gb, **_rows_grad(R_XQ))
    token = early(gb)
    dhx = _mm(dxq, GB, name=f"d_hx_{t}", out_dtype=F32, **_rows_t(R_XQ))
    dx1, gs['norm_xattn_g'] = _rms_bwd(dhx, sv['x1'], _after(sp['g_x'], token), dx2, f"rms_x_bwd_{t}")
    ga = jnp.zeros((4, ROWS_A, PACK_COLS), BF16)
    ga = _mm(sv['merged'], dx1, name=f"dw_out_{t}", out_dtype=BF16, into=ga, **_rows_grad(R_OUT))
    dm = _mm(dx1, GA, name=f"d_merged_{t}", out_dtype=F32, **_rows_t(R_OUT))
    dg, dya, do, dsg, ga, gs['b_gate'] = _merge_bwd(dm, sv['proj'], sv['ya'], sv['o'], sv['sg'], GA, sp['bg'], ga, f"merge_bwd_{t}")
    dc, dws, dbias, gs['sgu_norm_g'] = _sgu_bwd(dsg, sv['proj'], sp['sgu_g'], sp['wm'], sp['wmt'], sp['sgu_bias'], f"sgu_bwd_{t}")
    tril = jnp.tril(jnp.ones((SGU_CHUNK, SGU_CHUNK), F32))
    gs['sgu_w'] = dws * tril[None]
    gs['sgu_b'] = dbias.reshape(SGU_CHUNK, 4, 64).sum(-1).T
    dq, dk, dv, dfrow, dfcol = _fox_bwd(sv['qkv'], sv['o32'], do, sv['lse'], sv['fcol'], sv['frow'], f"fox_bwd_{t}")
    dF = jnp.pad(dfrow.reshape(8, S).T + dfcol.transpose(1, 0, 2).reshape(S, 8), ((0, 0), (0, FCOLS - 8)))
    df, dbf = _fgate_bwd(dF, sv['proj'], sp['bf'], f"fgate_bwd_{t}")
    gs['b_forget'] = dbf[:, :8]
    da, dbd, gs['pool_scale'] = _pool_bwd(dya, sv['d'], sp['bd'], sp['pool_scale'], f"pool_bwd_{t}")
    gs['pool_w'] = jnp.stack([dbd[g * 64:(g + 1) * 64, g * 64:(g + 1) * 64] for g in range(4)])
    dproj = _cols_by_shard([da, dq, dk, dv, df[:, :8], dc, dg])
    dwt = _mm(dproj, sv['h'], name=f"dw_in_{t}", out_dtype=BF16, ta=True, tm=512, tn=1024)
    ga = _place_w_in_rows(ga, dwt.reshape(4, WIN_PAD, PACK_COLS), f"dw_in_rows_{t}")
    dh = _mm(dproj, _w_in_by_shard(GA), name=f"d_h_{t}", out_dtype=F32, tk=512)
    dx, gs['norm_mix_g'] = _rms_bwd(dh, sv['x'], sp['g_mix'], dx1, f"rms_mix_bwd_{t}")
    return dx, ga, gs


SMALL_ROWS = 1424
GRAD_BLOCKS = {
    'w_ff1': ('b', lambda i: (R_FF1 // 256 + i, 0)), 'w_ff2': ('b', lambda i: (R_FF2 // 256 + i, 0)),
    'w_xq': ('b', lambda i: (R_XQ // 256 + i, 0)), 'w_xo': ('b', lambda i: (R_XO // 256 + i, 0)),
    'w_xkv': ('b', lambda i: (R_XKV // 256 + i % 2, i // 2)), 'w_out': ('a', lambda i: (R_OUT // 256 + i, 0)),
    'w_branch_a': ('a', lambda i: (R_BRANCH // 256, 0)), 'w_branch_b': ('a', lambda i: (R_BRANCH // 256, 1 + i)),
    'w_branch_c': ('a', lambda i: (R_BRANCH // 256, 3)),
}


def _pack_small(parts):
    flat = jnp.concatenate([p.reshape(-1) for p in parts])
    return jnp.pad(flat, (0, SMALL_ROWS * 128 - flat.shape[0])).reshape(SMALL_ROWS, 128)


def _unpack_small(buf, shapes):
    flat, out, r = buf.reshape(-1), [], 0
    for s in shapes:
        n = math.prod(s)
        out.append(flat[r:r + n].reshape(s))
        r += n
    return out


def kernel(x, mem, norm_mix_g, w_in, b_forget, pool_w, pool_scale, sgu_norm_g, sgu_w, sgu_b, w_branch_a, w_branch_b, w_branch_c, b_gate, w_out, norm_xattn_g, norm_mem_g, w_xq, w_xkv, w_xo, norm_ffn_g, w_ff1, w_ff2, final_norm_g, loss_target, m_norm_mix_g, m_w_in, m_b_forget, m_pool_w, m_pool_scale, m_sgu_norm_g, m_sgu_w, m_sgu_b, m_w_branch_a, m_w_branch_b, m_w_branch_c, m_b_gate, m_w_out, m_norm_xattn_g, m_norm_mem_g, m_w_xq, m_w_xkv, m_w_xo, m_norm_ffn_g, m_w_ff1, m_w_ff2, m_final_norm_g, v_norm_mix_g, v_w_in, v_b_forget, v_pool_w, v_pool_scale, v_sgu_norm_g, v_sgu_w, v_sgu_b, v_w_branch_a, v_w_branch_b, v_w_branch_c, v_b_gate, v_w_out, v_norm_xattn_g, v_norm_mem_g, v_w_xq, v_w_xkv, v_w_xo, v_norm_ffn_g, v_w_ff1, v_w_ff2, v_final_norm_g):
    args = (norm_mix_g, w_in, b_forget, pool_w, pool_scale, sgu_norm_g, sgu_w, sgu_b, w_branch_a, w_branch_b, w_branch_c, b_gate,
            w_out, norm_xattn_g, norm_mem_g, w_xq, w_xkv, w_xo, norm_ffn_g, w_ff1, w_ff2, final_norm_g)
    margs = (m_norm_mix_g, m_w_in, m_b_forget, m_pool_w, m_pool_scale, m_sgu_norm_g, m_sgu_w, m_sgu_b, m_w_branch_a, m_w_branch_b,
             m_w_branch_c, m_b_gate, m_w_out, m_norm_xattn_g, m_norm_mem_g, m_w_xq, m_w_xkv, m_w_xo, m_norm_ffn_g, m_w_ff1, m_w_ff2,
             m_final_norm_g)
    vargs = (v_norm_mix_g, v_w_in, v_b_forget, v_pool_w, v_pool_scale, v_sgu_norm_g, v_sgu_w, v_sgu_b, v_w_branch_a, v_w_branch_b,
             v_w_branch_c, v_b_gate, v_w_out, v_norm_xattn_g, v_norm_mem_g, v_w_xq, v_w_xkv, v_w_xo, v_norm_ffn_g, v_w_ff1, v_w_ff2,
             v_final_norm_g)
    w = dict(zip(W_NAMES, args))
    mo = dict(zip(W_NAMES, margs))
    vo = dict(zip(W_NAMES, vargs))
    xs, mems, tgt = x[0], mem[0], loss_target[0]
    shards = [_pack_shard(w, l) for l in range(DEPTH)]
    preps = [_small_prep(w, l) for l in range(DEPTH)]

    first_a, _ = _gather_begin(shards[0][0], "a_l0")
    pending_b, token = _gather_begin(shards[0][1], "b_l0")
    GA = None
    act, saved = xs, []
    for l in range(DEPTH):
        nxt = {}
        if l + 1 < DEPTH:
            nxt['a'], ta = _gather_begin(shards[l + 1][0], f"a_l{l + 1}")
            token = ta if token is None else token + ta
        if l == 0:
            GA = _gather_end(first_a, shards[DEPTH - 1][1])

        def second(x1, l=l, pending_b=pending_b, nxt=nxt):
            GB = _gather_end(pending_b, x1)
            if l + 1 == DEPTH:
                return GB, None
            nxt['b'], tb = _gather_begin(shards[l + 1][1], f"b_l{l + 1}")
            return GB, tb

        act, sv = _layer_fwd(act, mems, GA, _w_in_rows(GA), preps[l], l, token, second)
        saved.append(sv)
        if l + 1 < DEPTH:
            GA = _gather_end(nxt['a'], act)
            pending_b, token = nxt['b'], None
    loss_part, dact, d_final_g = _loss_head(act, w['final_norm_g'][None], tgt, "loss_head")

    red_a, red_b, small_g = [None] * DEPTH, [None] * DEPTH, [None] * DEPTH
    token, state_a = None, None
    for l in reversed(range(DEPTH)):
        early = {}

        def start_b(gb, l=l, early=early):
            early['state'], tok = _reduce_begin(gb, f"b_l{l}")
            return tok

        dact, ga, small_g[l] = _layer_bwd(dact, mems, preps[l], saved[l], l, token, start_b)
        if state_a is not None:
            red_a[l + 1] = _reduce_end(state_a, dact)
        red_b[l] = _reduce_end(early['state'], dact)
        state_a, token = _reduce_begin(ga, f"a_l{l}")
    grad_x = dact[None]
    per_layer = [n for n in SMALL_NAMES if n != 'final_norm_g']
    small_shapes = [w[n].shape for n in per_layer] + [(D,), (1,)]
    parts = [jnp.stack([small_g[l][n].reshape(w[n].shape[1:]) for l in range(DEPTH)]) for n in per_layer]
    state_small, token_small = _all_reduce_begin(_pack_small(parts + [d_final_g.reshape(D), loss_part.reshape(1)]), "small")
    token = token + token_small

    grads, delta, new_m, new_v = {}, {}, {}, {}
    for n, (buf, g_index) in GRAD_BLOCKS.items():
        if buf == 'b':
            grads[n], delta[n], new_m[n], new_v[n] = _adamw_packed(red_b, w[n], mo[n], vo[n], g_index, f"adamw_{n}", token)
    red_a[0] = _reduce_end(state_a, new_v['w_xkv'])
    small_red = _unpack_small(_all_reduce_end(state_small, red_a[0]), small_shapes)
    grads.update(zip(per_layer + ['final_norm_g'], small_red[:-1]))
    loss = small_red[-1].reshape(())
    for n, (buf, g_index) in GRAD_BLOCKS.items():
        if buf == 'a':
            grads[n], delta[n], new_m[n], new_v[n] = _adamw_packed(red_a, w[n], mo[n], vo[n], g_index, f"adamw_{n}", token)
    g_t = jnp.stack([r[R_WIN:R_WIN + WIN_ROWS] for r in red_a], axis=1)
    upd = _adamw(g_t, _w_in_t(w['w_in']), _w_in_t(mo['w_in']), _w_in_t(vo['w_in']), "adamw_w_in", block=(WIN_ROWS, DEPTH, 128))
    grads['w_in'], delta['w_in'], new_m['w_in'], new_v['w_in'] = [jnp.transpose(a, (1, 2, 0)) for a in (g_t,) + tuple(upd)]
    small_all = per_layer + ['final_norm_g']
    shapes_all = [w[n].shape for n in small_all]
    packed = [_pack_small([d[n] for n in small_all])[None] for d in (grads, w, mo, vo)]
    ds, ms, vs = _adamw(*packed, "adamw_small")
    for n, a, b, c in zip(small_all, _unpack_small(ds[0], shapes_all), _unpack_small(ms[0], shapes_all), _unpack_small(vs[0], shapes_all)):
        delta[n], new_m[n], new_v[n] = a, b, c

    return (loss, grad_x, *[grads[n] for n in W_NAMES], *[delta[n] for n in W_NAMES], *[new_m[n] for n in W_NAMES],
            *[new_v[n] for n in W_NAMES])
```

```python
import math

import jax
import jax.numpy as jnp
from jax import lax
from jax.experimental import pallas as pl
from jax.experimental.pallas import tpu as pltpu

F32 = jnp.float32
BF16 = jnp.bfloat16

D = 1024
DEPTH = 2
POOL_W = 256
FOX_W = 512
SGU_W = 256
SGU_CHUNK = 128
N_IN = 5384
P_G, P_Q, P_K, P_V, P_C, P_A, P_F = 0, 3072, 3584, 4096, 4608, 5120, 5376
NP = 5632
XH, XHD = 4, 256
D_FF = 4096
EPS = 1e-6
NEG = -1e30
FOX_SCALE = 64 ** -0.5
X_SCALE = 256 ** -0.5
GELU_K = math.sqrt(2.0 / math.pi)
GELU_C = 0.044715

ADAM_LR, ADAM_B1, ADAM_B2, ADAM_EPS, ADAM_WD, ADAM_STEP = 0.001, 0.9, 0.999, 1e-08, 0.01, 10

VMEM_LIMIT = 48 * 1024 * 1024
MESH = pl.DeviceIdType.MESH

IN_NAMES = ['x', 'mem', 'norm_mix_g', 'w_in', 'b_forget', 'pool_w', 'pool_scale', 'sgu_norm_g', 'sgu_w', 'sgu_b',
            'w_branch_a', 'w_branch_b', 'w_branch_c', 'b_gate', 'w_out', 'norm_xattn_g', 'norm_mem_g', 'w_xq',
            'w_xkv', 'w_xo', 'norm_ffn_g', 'w_ff1', 'w_ff2', 'final_norm_g']
W_NAMES = IN_NAMES[2:]
BIG_NAMES = ['w_in', 'w_branch_a', 'w_branch_b', 'w_branch_c', 'w_out', 'w_xq', 'w_xkv', 'w_xo', 'w_ff1', 'w_ff2']
SMALL_NAMES = [n for n in W_NAMES if n not in BIG_NAMES]
PACK_COLS = 1024


ANY = pl.BlockSpec(memory_space=pl.ANY)


def _cp(sem=None):
    return pltpu.CompilerParams(dimension_semantics=sem, vmem_limit_bytes=VMEM_LIMIT)


def _mm(a, b, *, name, out_dtype, ta=False, tb=False, tm=1024, tn=512, tk=1024, a_fn=None, extra=None, epi=None,
        n=None, k=None, b_block=None, b_index=None, into=None, o_block=None, o_index=None):
    M = a.shape[1] if ta else a.shape[0]
    K = k if k is not None else (a.shape[0] if ta else a.shape[1])
    N = n if n is not None else (b.shape[0] if tb else b.shape[1])
    tm, tn, tk = min(tm, M), min(tn, N), min(tk, K)
    assert M % tm == 0 and N % tn == 0 and K % tk == 0, (name, M, N, K)
    nk = K // tk
    a_spec = pl.BlockSpec((tk, tm), lambda i, j, k: (k, i)) if ta else pl.BlockSpec((tm, tk), lambda i, j, k: (i, k))
    if b_block is not None:
        b_spec = pl.BlockSpec(b_block, b_index)
    else:
        b_spec = pl.BlockSpec((tn, tk), lambda i, j, k: (j, k)) if tb else pl.BlockSpec((tk, tn), lambda i, j, k: (k, j))
    dn = (((0 if ta else 1,), (1 if tb else 0,)), ((), ()))
    tile = pl.BlockSpec((tm, tn), lambda i, j, k: (i, j))
    o_spec = pl.BlockSpec(o_block, o_index) if into is not None else tile
    in_specs = [a_spec, b_spec] + ([tile] if extra is not None else []) + ([ANY] if into is not None else [])
    n_in = len(in_specs)

    def body(*refs):
        a_ref, b_ref = refs[0], refs[1]
        e_ref = refs[2] if extra is not None else None
        o_ref, acc_ref = refs[n_in], refs[n_in + 1]
        kk = pl.program_id(2)

        @pl.when(kk == 0)
        def _():
            acc_ref[...] = jnp.zeros_like(acc_ref)

        av = a_ref[...]
        if a_fn is not None:
            av = a_fn(av)
        bv = b_ref[...]
        if bv.ndim == 3:
            bv = bv.reshape(-1, bv.shape[-1])
        acc_ref[...] += lax.dot_general(av.astype(BF16), bv.astype(BF16), dn, preferred_element_type=F32)

        @pl.when(kk == nk - 1)
        def _():
            r = acc_ref[...]
            if epi is not None:
                r = epi(r, e_ref[...])
            o_ref[...] = r.astype(o_ref.dtype).reshape(o_ref.shape)

    args = (a, b) + ((extra,) if extra is not None else ()) + ((into,) if into is not None else ())
    out_shape = jax.ShapeDtypeStruct(into.shape, into.dtype) if into is not None else jax.ShapeDtypeStruct((M, N), out_dtype)
    return pl.pallas_call(
        body, out_shape=out_shape, grid=(M // tm, N // tn, nk), in_specs=in_specs, out_specs=o_spec,
        scratch_shapes=[pltpu.VMEM((tm, tn), F32)], input_output_aliases={n_in - 1: 0} if into is not None else {},
        compiler_params=_cp(("parallel", "parallel", "arbitrary")), name=name)(*args)


def _relu2(z):
    r = jnp.maximum(z, 0.0)
    return r * r


def _rms_fwd(x, g, name, tr=256):
    R, n = x.shape
    tr = min(tr, R)

    def body(x_ref, g_ref, h_ref):
        xv = x_ref[...]
        rstd = lax.rsqrt(jnp.mean(xv * xv, axis=-1, keepdims=True) + EPS)
        h_ref[...] = (xv * rstd * g_ref[...]).astype(BF16)

    return pl.pallas_call(
        body, out_shape=jax.ShapeDtypeStruct((R, n), BF16), grid=(R // tr,),
        in_specs=[pl.BlockSpec((tr, n), lambda i: (i, 0)), pl.BlockSpec((1, n), lambda i: (0, 0))],
        out_specs=pl.BlockSpec((tr, n), lambda i: (i, 0)), compiler_params=_cp(("parallel",)), name=name)(x, g)


def _rms_bwd(dh, x, g, dres, name, tr=256):
    R, n = x.shape
    tr = min(tr, R)
    need_dx = dres is not None

    def body(*refs):
        if need_dx:
            dh_ref, x_ref, g_ref, r_ref, dx_ref, dg_ref = refs
        else:
            dh_ref, x_ref, g_ref, dg_ref = refs
        i = pl.program_id(0)
        xv = x_ref[...]
        dhv = dh_ref[...].astype(F32)
        rstd = lax.rsqrt(jnp.mean(xv * xv, axis=-1, keepdims=True) + EPS)
        xhat = xv * rstd

        @pl.when(i == 0)
        def _():
            dg_ref[...] = jnp.zeros_like(dg_ref)

        dg_ref[...] += jnp.sum(dhv * xhat, axis=0, keepdims=True)
        if need_dx:
            t = dhv * g_ref[...]
            dx_ref[...] = r_ref[...] + rstd * (t - xhat * jnp.mean(t * xhat, axis=-1, keepdims=True))

    row = pl.BlockSpec((tr, n), lambda i: (i, 0))
    vec = pl.BlockSpec((1, n), lambda i: (0, 0))
    if need_dx:
        return pl.pallas_call(
            body, out_shape=(jax.ShapeDtypeStruct((R, n), F32), jax.ShapeDtypeStruct((1, n), F32)), grid=(R // tr,),
            in_specs=[row, row, vec, row], out_specs=(row, vec), compiler_params=_cp(("arbitrary",)), name=name)(dh, x, g, dres)
    return pl.pallas_call(
        body, out_shape=jax.ShapeDtypeStruct((1, n), F32), grid=(R // tr,),
        in_specs=[row, row, vec], out_specs=vec, compiler_params=_cp(("arbitrary",)), name=name)(dh, x, g)


def _loss_head(x, g, tgt, name, tr=256):
    R, n = x.shape

    def body(x_ref, g_ref, t_ref, loss_ref, dx_ref, dg_ref):
        i = pl.program_id(0)
        xv = x_ref[...]
        gv = g_ref[...]
        rstd = lax.rsqrt(jnp.mean(xv * xv, axis=-1, keepdims=True) + EPS)
        xhat = xv * rstd
        e = xhat * gv - t_ref[...]

        @pl.when(i == 0)
        def _():
            loss_ref[...] = jnp.zeros_like(loss_ref)
            dg_ref[...] = jnp.zeros_like(dg_ref)

        loss_ref[...] += 0.5 * jnp.sum(jnp.sum(e * e, axis=-1, keepdims=True) / n, axis=0, keepdims=True)
        dy = e / n
        dg_ref[...] += jnp.sum(dy * xhat, axis=0, keepdims=True)
        t = dy * gv
        dx_ref[...] = rstd * (t - xhat * jnp.mean(t * xhat, axis=-1, keepdims=True))

    row = pl.BlockSpec((tr, n), lambda i: (i, 0))
    vec = pl.BlockSpec((1, n), lambda i: (0, 0))
    one = pl.BlockSpec((1, 1), lambda i: (0, 0))
    return pl.pallas_call(
        body, out_shape=(jax.ShapeDtypeStruct((1, 1), F32), jax.ShapeDtypeStruct((R, n), F32), jax.ShapeDtypeStruct((1, n), F32)),
        grid=(R // tr,), in_specs=[row, vec, row], out_specs=(one, row, vec),
        compiler_params=_cp(("arbitrary",)), name=name)(x, g, tgt)


def _pool_masks(S):
    row = lax.broadcasted_iota(jnp.int32, (S, POOL_W), 0)
    grp = lax.broadcasted_iota(jnp.int32, (S, POOL_W), 1) // 64
    win = jnp.where(grp == 0, 2, jnp.where(grp == 1, 4, jnp.where(grp == 2, 8, 16)))
    cnt = jnp.minimum(row + 1, win).astype(F32)
    return row, grp, cnt


def _by_group(grp, v0, v1, v2, v3):
    return jnp.where(grp == 0, v0, jnp.where(grp == 1, v1, jnp.where(grp == 2, v2, v3)))


def _pool_fwd(proj, bd, scale, name):
    S = proj.shape[0]

    def body(a_ref, bd_ref, sc_ref, d_ref, y_ref):
        a = a_ref[...]
        row, grp, cnt = _pool_masks(S)

        def back(v, k):
            return jnp.where(row >= k, pltpu.roll(v, k, 0), 0.0)

        s1 = a + back(a, 1)
        s2 = s1 + back(s1, 2)
        s3 = s2 + back(s2, 4)
        s4 = s3 + back(s3, 8)
        d = (_by_group(grp, s1, s2, s3, s4) / cnt - a).astype(BF16)
        d_ref[...] = d
        y_ref[...] = (jnp.dot(d, bd_ref[...], preferred_element_type=F32) * sc_ref[...]).astype(BF16)

    full = lambda r, c: pl.BlockSpec((r, c), lambda i: (0, 0))
    return pl.pallas_call(
        body, out_shape=(jax.ShapeDtypeStruct((S, POOL_W), BF16), jax.ShapeDtypeStruct((S, POOL_W), BF16)), grid=(1,),
        in_specs=[pl.BlockSpec((S, POOL_W), lambda i: (0, P_A // POOL_W)), full(POOL_W, POOL_W), full(1, POOL_W)],
        out_specs=(full(S, POOL_W), full(S, POOL_W)), compiler_params=_cp(("arbitrary",)), name=name)(proj, bd, scale)


def _pool_bwd(dya, d, bd, scale, name):
    S = dya.shape[0]

    def body(dy_ref, d_ref, bd_ref, sc_ref, da_ref, dbd_ref, dsc_ref):
        dy = dy_ref[...]
        dv = d_ref[...]
        bdv = bd_ref[...]
        row, grp, cnt = _pool_masks(S)
        yraw = jnp.dot(dv, bdv, preferred_element_type=F32)
        dsc_ref[...] = jnp.sum(dy * yraw, axis=0, keepdims=True)
        tb = (dy * sc_ref[...]).astype(BF16)
        dbd_ref[...] = lax.dot_general(dv, tb, (((0,), (0,)), ((), ())), preferred_element_type=F32)
        dd = lax.dot_general(tb, bdv, (((1,), (1,)), ((), ())), preferred_element_type=F32)
        e = dd / cnt

        def fwd(v, k):
            return jnp.where(row < S - k, pltpu.roll(v, S - k, 0), 0.0)

        r1 = e + fwd(e, 1)
        r2 = r1 + fwd(r1, 2)
        r3 = r2 + fwd(r2, 4)
        r4 = r3 + fwd(r3, 8)
        da_ref[...] = (_by_group(grp, r1, r2, r3, r4) - dd).astype(BF16)

    full = lambda r, c: pl.BlockSpec((r, c), lambda i: (0, 0))
    return pl.pallas_call(
        body, out_shape=(jax.ShapeDtypeStruct((S, POOL_W), BF16), jax.ShapeDtypeStruct((POOL_W, POOL_W), F32),
                         jax.ShapeDtypeStruct((1, POOL_W), F32)), grid=(1,),
        in_specs=[full(S, POOL_W), full(S, POOL_W), full(POOL_W, POOL_W), full(1, POOL_W)],
        out_specs=(full(S, POOL_W), full(POOL_W, POOL_W), full(1, POOL_W)),
        compiler_params=_cp(("arbitrary",)), name=name)(dya, d, bd, scale)


FCOLS = 128


def _log_sigmoid(z):
    return -(jnp.maximum(-z, 0.0) + jnp.log1p(jnp.exp(-jnp.abs(z))))


def _fgate_fwd(proj, bf, name):
    S = proj.shape[0]

    def body(f_ref, b_ref, o_ref):
        v = _log_sigmoid(f_ref[...] + b_ref[...])
        row = lax.broadcasted_iota(jnp.int32, (S, FCOLS), 0)
        k = 1
        while k < S:
            v = v + jnp.where(row >= k, pltpu.roll(v, k, 0), 0.0)
            k *= 2
        o_ref[...] = v

    return pl.pallas_call(
        body, out_shape=jax.ShapeDtypeStruct((S, FCOLS), F32), grid=(1,),
        in_specs=[pl.BlockSpec((S, FCOLS), lambda i: (0, P_F // FCOLS)), pl.BlockSpec((1, FCOLS), lambda i: (0, 0))],
        out_specs=pl.BlockSpec((S, FCOLS), lambda i: (0, 0)), compiler_params=_cp(("arbitrary",)), name=name)(proj, bf)


def _fgate_bwd(dF, proj, bf, name):
    S = proj.shape[0]

    def body(dF_ref, f_ref, b_ref, df_ref, db_ref):
        v = dF_ref[...]
        row = lax.broadcasted_iota(jnp.int32, (S, FCOLS), 0)
        k = 1
        while k < S:
            v = v + jnp.where(row < S - k, pltpu.roll(v, S - k, 0), 0.0)
            k *= 2
        z = f_ref[...] + b_ref[...]
        df = v * (1.0 / (1.0 + jnp.exp(z)))
        db_ref[...] = jnp.sum(df, axis=0, keepdims=True)
        df_ref[...] = jnp.concatenate([df, jnp.zeros_like(df)], axis=1).astype(BF16)

    return pl.pallas_call(
        body, out_shape=(jax.ShapeDtypeStruct((S, 2 * FCOLS), BF16), jax.ShapeDtypeStruct((1, FCOLS), F32)), grid=(1,),
        in_specs=[pl.BlockSpec((S, FCOLS), lambda i: (0, 0)), pl.BlockSpec((S, FCOLS), lambda i: (0, P_F // FCOLS)),
                  pl.BlockSpec((1, FCOLS), lambda i: (0, 0))],
        out_specs=(pl.BlockSpec((S, 2 * FCOLS), lambda i: (0, 0)), pl.BlockSpec((1, FCOLS), lambda i: (0, 0))),
        compiler_params=_cp(("arbitrary",)), name=name)(dF, proj, bf)


def _fox_scores(qe, kj, fq, fk, r0, c0, tq, tk, diagonal):
    s = lax.dot_general(qe, kj, (((1,), (1,)), ((), ())), preferred_element_type=F32) * FOX_SCALE
    s = s + (fq - fk)
    if not diagonal:
        return s
    rows = r0 + lax.broadcasted_iota(jnp.int32, (tq, tk), 0)
    cols = c0 + lax.broadcasted_iota(jnp.int32, (tq, tk), 1)
    return jnp.where(rows >= cols, s, NEG)


def _fox_fwd(qkv, fcol, frow, name, tq=256):
    S = qkv.shape[0]
    tk = tq

    def body(q_ref, k_ref, v_ref, fc_ref, fr_ref, o_ref, o32_ref, lse_ref):
        i = pl.program_id(1)
        r0 = i * tq
        q = q_ref[...]
        half = lax.broadcasted_iota(jnp.int32, (tq, 128), 1) // 64
        qs = [jnp.where(half == e, q, jnp.zeros_like(q)) for e in (0, 1)]
        fqs = [fc_ref[0, :, e:e + 1] for e in (0, 1)]

        def step(j, carry, diagonal=False):
            c0 = pl.multiple_of(j * tk, tk)
            kj = k_ref[pl.ds(c0, tk), :]
            vj = v_ref[pl.ds(c0, tk), :]
            out = []
            for e in (0, 1):
                m, l, acc = carry[e]
                s = _fox_scores(qs[e], kj, fqs[e], fr_ref[0, e:e + 1, pl.ds(c0, tk)], r0, c0, tq, tk, diagonal)
                m_new = jnp.maximum(m, jnp.max(s, axis=-1, keepdims=True))
                alpha = jnp.exp(m - m_new)
                p = jnp.exp(s - m_new)
                out.append((m_new, alpha * l + jnp.sum(p, axis=-1, keepdims=True),
                            alpha * acc + jnp.dot(p.astype(BF16), vj, preferred_element_type=F32)))
            return tuple(out)

        init = (jnp.full((tq, 1), NEG, F32), jnp.zeros((tq, 1), F32), jnp.zeros((tq, 128), F32))
        carry = lax.fori_loop(0, i, step, (init, init))
        carry = step(i, carry, diagonal=True)
        outs = []
        for e in (0, 1):
            m, l, acc = carry[e]
            outs.append(acc / l)
            lse_ref[0, :, e:e + 1] = m + jnp.log(l)
        o = jnp.where(half == 0, outs[0], outs[1])
        o32_ref[...] = o
        o_ref[...] = o.astype(BF16)

    tile = pl.BlockSpec((tq, 128), lambda h, i: (i, h))
    return pl.pallas_call(
        body, out_shape=(jax.ShapeDtypeStruct((S, FOX_W), BF16), jax.ShapeDtypeStruct((S, FOX_W), F32),
                         jax.ShapeDtypeStruct((4, S, 2), F32)), grid=(4, S // tq),
        in_specs=[tile, pl.BlockSpec((S, 128), lambda h, i: (0, 4 + h)), pl.BlockSpec((S, 128), lambda h, i: (0, 8 + h)),
                  pl.BlockSpec((1, tq, 2), lambda h, i: (h, i, 0)), pl.BlockSpec((1, 2, S), lambda h, i: (h, 0, 0))],
        out_specs=(tile, tile, pl.BlockSpec((1, tq, 2), lambda h, i: (h, i, 0))),
        compiler_params=_cp(("parallel", "parallel")), name=name)(qkv, qkv, qkv, fcol, frow)


def _fox_bwd(qkv, o32, do, lse, fcol, frow, name, tq=256):
    S = qkv.shape[0]
    tk = tq
    nq = S // tq

    def body(q_ref, k_ref, v_ref, o_ref, do_ref, lse_ref, fc_ref, fr_ref, dq_ref, dk_ref, dv_ref, dfr_ref, dfc_ref, dk_acc, dv_acc):
        dk_acc[...] = jnp.zeros_like(dk_acc)
        dv_acc[...] = jnp.zeros_like(dv_acc)
        dfr_ref[...] = jnp.zeros_like(dfr_ref)
        half = lax.broadcasted_iota(jnp.int32, (tq, 128), 1) // 64

        def q_block(i, _):
            r0 = pl.multiple_of(i * tq, tq)
            qi = q_ref[pl.ds(r0, tq), :]
            dob = do_ref[pl.ds(r0, tq), :].astype(BF16)
            row_dot = dob.astype(F32) * o_ref[pl.ds(r0, tq), :]
            qs = [jnp.where(half == e, qi, jnp.zeros_like(qi)) for e in (0, 1)]
            dos = [jnp.where(half == e, dob, jnp.zeros_like(dob)) for e in (0, 1)]
            deltas = [jnp.sum(jnp.where(half == e, row_dot, 0.0), axis=-1, keepdims=True) for e in (0, 1)]
            lses = [lse_ref[0, pl.ds(r0, tq), e:e + 1] for e in (0, 1)]
            fqs = [fc_ref[0, pl.ds(r0, tq), e:e + 1] for e in (0, 1)]

            def step(j, carry, diagonal=False):
                dqs, row_sums = carry
                c0 = pl.multiple_of(j * tk, tk)
                kj = k_ref[pl.ds(c0, tk), :]
                vj = v_ref[pl.ds(c0, tk), :]
                new_dq, new_rows, dkc, dvc = [], [], [], []
                for e in (0, 1):
                    s = _fox_scores(qs[e], kj, fqs[e], fr_ref[0, e:e + 1, pl.ds(c0, tk)], r0, c0, tq, tk, diagonal)
                    p = jnp.exp(s - lses[e])
                    dp = lax.dot_general(dos[e], vj, (((1,), (1,)), ((), ())), preferred_element_type=F32)
                    ds = p * (dp - deltas[e])
                    dfr_ref[0, e:e + 1, pl.ds(c0, tk)] -= jnp.sum(ds, axis=0, keepdims=True)
                    new_rows.append(row_sums[e] + jnp.sum(ds, axis=-1, keepdims=True))
                    dsb = (ds * FOX_SCALE).astype(BF16)
                    dkc.append(lax.dot_general(dsb, qi, (((0,), (0,)), ((), ())), preferred_element_type=F32))
                    dvc.append(lax.dot_general(p.astype(BF16), dob, (((0,), (0,)), ((), ())), preferred_element_type=F32))
                    new_dq.append(dqs[e] + jnp.dot(dsb, kj, preferred_element_type=F32))
                dk_acc[pl.ds(c0, tk), :] += jnp.where(half == 0, dkc[0], dkc[1])
                dv_acc[pl.ds(c0, tk), :] += jnp.where(half == 0, dvc[0], dvc[1])
                return tuple(new_dq), tuple(new_rows)

            zero, zero_col = jnp.zeros((tq, 128), F32), jnp.zeros((tq, 1), F32)
            carry = lax.fori_loop(0, i, step, ((zero, zero), (zero_col, zero_col)))
            dqs, row_sums = step(i, carry, diagonal=True)
            for e in (0, 1):
                dfc_ref[0, pl.ds(r0, tq), e:e + 1] = row_sums[e]
            dq_ref[pl.ds(r0, tq), :] = jnp.where(half == 0, dqs[0], dqs[1]).astype(BF16)
            return 0

        lax.fori_loop(0, nq, q_block, 0)
        dk_ref[...] = dk_acc[...].astype(BF16)
        dv_ref[...] = dv_acc[...].astype(BF16)

    col = lambda off: pl.BlockSpec((S, 128), lambda h: (0, off + h))
    hs2 = pl.BlockSpec((1, S, 2), lambda h: (h, 0, 0))
    h2s = pl.BlockSpec((1, 2, S), lambda h: (h, 0, 0))
    return pl.pallas_call(
        body, out_shape=(jax.ShapeDtypeStruct((S, FOX_W), BF16),) * 3 + (jax.ShapeDtypeStruct((4, 2, S), F32),
                                                                         jax.ShapeDtypeStruct((4, S, 2), F32)), grid=(4,),
        in_specs=[col(0), col(4), col(8), col(0), col(0), hs2, hs2, h2s],
        out_specs=(col(0), col(0), col(0), h2s, hs2),
        scratch_shapes=[pltpu.VMEM((S, 128), F32), pltpu.VMEM((S, 128), F32)],
        compiler_params=_cp(("parallel",)), name=name)(qkv, qkv, qkv, o32, do, lse, fcol, frow)


def _gelu(x):
    return 0.5 * x * (1.0 + jnp.tanh(GELU_K * (x + GELU_C * x * x * x)))


def _gelu_grad(x):
    th = jnp.tanh(GELU_K * (x + GELU_C * x * x * x))
    return 0.5 * (1.0 + th) + 0.5 * x * (1.0 - th * th) * GELU_K * (1.0 + 3.0 * GELU_C * x * x)


def _sgu_parts(c, gn, w_ref, bias):
    zc = _gelu(c)
    u, vv = zc[:, :SGU_W], zc[:, SGU_W:]
    rstd = lax.rsqrt(jnp.mean(vv * vv, axis=-1, keepdims=True) + EPS)
    vhat = vv * rstd
    vnb = (vhat * gn).astype(BF16)
    grp = lax.broadcasted_iota(jnp.int32, (SGU_CHUNK, SGU_W), 1) // 64
    mixed = bias
    for gi in range(4):
        mixed = mixed + jnp.where(grp == gi, jnp.dot(w_ref[gi], vnb, preferred_element_type=F32), 0.0)
    return u, rstd, vhat, vnb, grp, mixed


def _sgu_fwd(proj, gn, wm, bias, name):
    S = proj.shape[0]

    def body(c_ref, g_ref, w_ref, b_ref, o_ref):
        u, _, _, _, _, mixed = _sgu_parts(c_ref[...], g_ref[...], w_ref, b_ref[...])
        o_ref[...] = (u * mixed).astype(BF16)

    return pl.pallas_call(
        body, out_shape=jax.ShapeDtypeStruct((S, SGU_W), BF16), grid=(S // SGU_CHUNK,),
        in_specs=[pl.BlockSpec((SGU_CHUNK, 2 * SGU_W), lambda i: (i, P_C // (2 * SGU_W))),
                  pl.BlockSpec((1, SGU_W), lambda i: (0, 0)), pl.BlockSpec((4, SGU_CHUNK, SGU_CHUNK), lambda i: (0, 0, 0)),
                  pl.BlockSpec((SGU_CHUNK, SGU_W), lambda i: (0, 0))],
        out_specs=pl.BlockSpec((SGU_CHUNK, SGU_W), lambda i: (i, 0)),
        compiler_params=_cp(("parallel",)), name=name)(proj, gn, wm, bias)


def _sgu_bwd(dsg, proj, gn, wm, wmt, bias, name):
    S = proj.shape[0]

    def body(dsg_ref, c_ref, g_ref, w_ref, wt_ref, b_ref, dc_ref, dw_ref, db_ref, dg_ref):
        i = pl.program_id(0)

        @pl.when(i == 0)
        def _():
            dw_ref[...] = jnp.zeros_like(dw_ref)
            db_ref[...] = jnp.zeros_like(db_ref)
            dg_ref[...] = jnp.zeros_like(dg_ref)

        c = c_ref[...]
        gn_v = g_ref[...]
        u, rstd, vhat, vnb, grp, mixed = _sgu_parts(c, gn_v, w_ref, b_ref[...])
        dsg_v = dsg_ref[...]
        du = dsg_v * mixed
        dmix = dsg_v * u
        db_ref[...] += dmix
        dmb = dmix.astype(BF16)
        dvn = jnp.zeros((SGU_CHUNK, SGU_W), F32)
        for gi in range(4):
            dmg = jnp.where(grp == gi, dmb, jnp.zeros_like(dmb))
            dw_ref[gi] += lax.dot_general(dmg, vnb, (((1,), (1,)), ((), ())), preferred_element_type=F32)
            dvn = dvn + jnp.where(grp == gi, jnp.dot(wt_ref[gi], dmb, preferred_element_type=F32), 0.0)
        dg_ref[...] += jnp.sum(dvn * vhat, axis=0, keepdims=True)
        t = dvn * gn_v
        dvv = rstd * (t - vhat * jnp.mean(t * vhat, axis=-1, keepdims=True))
        dc_ref[...] = (jnp.concatenate([du, dvv], axis=1) * _gelu_grad(c)).astype(BF16)

    w_spec = pl.BlockSpec((4, SGU_CHUNK, SGU_CHUNK), lambda i: (0, 0, 0))
    tile = pl.BlockSpec((SGU_CHUNK, SGU_W), lambda i: (0, 0))
    vec = pl.BlockSpec((1, SGU_W), lambda i: (0, 0))
    return pl.pallas_call(
        body, out_shape=(jax.ShapeDtypeStruct((S, 2 * SGU_W), BF16), jax.ShapeDtypeStruct((4, SGU_CHUNK, SGU_CHUNK), F32),
                         jax.ShapeDtypeStruct((SGU_CHUNK, SGU_W), F32), jax.ShapeDtypeStruct((1, SGU_W), F32)),
        grid=(S // SGU_CHUNK,),
        in_specs=[pl.BlockSpec((SGU_CHUNK, SGU_W), lambda i: (i, 0)),
                  pl.BlockSpec((SGU_CHUNK, 2 * SGU_W), lambda i: (i, P_C // (2 * SGU_W))), vec, w_spec, w_spec, tile],
        out_specs=(pl.BlockSpec((SGU_CHUNK, 2 * SGU_W), lambda i: (i, 0)), w_spec, tile, vec),
        compiler_params=_cp(("arbitrary",)), name=name)(dsg, proj, gn, wm, wmt, bias)


def _sigmoid(z):
    return 1.0 / (1.0 + jnp.exp(-z))


def _merge_specs(tm):
    row = lambda n: pl.BlockSpec((tm, n), lambda i: (i, 0))
    gate = lambda b: pl.BlockSpec((tm, D), lambda i: (i, b))
    full = lambda r, c: pl.BlockSpec((r, c), lambda i: (0, 0))
    packed = pl.BlockSpec((4, 256, PACK_COLS), lambda i: (0, R_BRANCH // 256, 0))
    return row, gate, full, packed


def _branch_shards(c_ref, j):
    return c_ref[j, :, 0:256], c_ref[j, :, 256:512], c_ref[j, :, 512:768], c_ref[j, :, 768:1024]


def _merge_fwd(proj, ya, o, sg, packed_w, bg, name, tm=256):
    S = proj.shape[0]
    row, gate, full, packed = _merge_specs(tm)

    def body(g0, g1, g2, ya_ref, o_ref, sg_ref, c_ref, bg_ref, out_ref):
        yav, ov, sgv = ya_ref[...], o_ref[...], sg_ref[...]
        for j in range(4):
            cols = slice(256 * j, 256 * (j + 1))
            wa, wb0, wb1, wc = _branch_shards(c_ref, j)
            y = (jnp.dot(yav, wa, preferred_element_type=F32),
                 jnp.dot(ov[:, :256], wb0, preferred_element_type=F32) + jnp.dot(ov[:, 256:], wb1, preferred_element_type=F32),
                 jnp.dot(sgv, wc, preferred_element_type=F32))
            acc = jnp.zeros((tm, 256), F32)
            for b, g_ref in enumerate((g0, g1, g2)):
                acc = acc + _sigmoid(g_ref[:, cols] + bg_ref[:, b * D + 256 * j:b * D + 256 * (j + 1)]) * y[b]
            out_ref[:, cols] = acc.astype(BF16)

    return pl.pallas_call(
        body, out_shape=jax.ShapeDtypeStruct((S, D), BF16), grid=(S // tm,),
        in_specs=[gate(0), gate(1), gate(2), row(POOL_W), row(FOX_W), row(SGU_W), packed, full(1, 3 * D)],
        out_specs=row(D), compiler_params=_cp(("parallel",)), name=name)(proj, proj, proj, ya, o, sg, packed_w, bg)


def _merge_bwd(dm, proj, ya, o, sg, packed_w, bg, grads, name, tm=256):
    S = proj.shape[0]
    row, gate, full, packed = _merge_specs(tm)
    tn_dims = (((0,), (0,)), ((), ()))
    nt_dims = (((1,), (1,)), ((), ()))

    def body(dm_ref, g0, g1, g2, ya_ref, o_ref, sg_ref, c_ref, bg_ref, _, dg_ref, dya_ref, do_ref, dsg_ref, dc_ref, dbg_ref, acc):
        i = pl.program_id(0)

        @pl.when(i == 0)
        def _():
            acc[...] = jnp.zeros_like(acc)
            dbg_ref[...] = jnp.zeros_like(dbg_ref)

        yav, ov, sgv = ya_ref[...], o_ref[...], sg_ref[...]
        o0, o1 = ov[:, :256], ov[:, 256:]
        dya = jnp.zeros((tm, POOL_W), F32)
        do0 = jnp.zeros((tm, 256), F32)
        do1 = jnp.zeros((tm, 256), F32)
        dsg = jnp.zeros((tm, SGU_W), F32)
        for j in range(4):
            cols = slice(256 * j, 256 * (j + 1))
            wa, wb0, wb1, wc = _branch_shards(c_ref, j)
            y = (jnp.dot(yav, wa, preferred_element_type=F32),
                 jnp.dot(o0, wb0, preferred_element_type=F32) + jnp.dot(o1, wb1, preferred_element_type=F32),
                 jnp.dot(sgv, wc, preferred_element_type=F32))
            dmv = dm_ref[:, cols]
            dy = []
            for b, g_ref in enumerate((g0, g1, g2)):
                bcols = slice(b * D + 256 * j, b * D + 256 * (j + 1))
                gt = _sigmoid(g_ref[:, cols] + bg_ref[:, bcols])
                dgp = dmv * y[b] * gt * (1.0 - gt)
                dg_ref[:, bcols] = dgp.astype(BF16)
                dbg_ref[:, bcols] += jnp.sum(dgp, axis=0, keepdims=True)
                dy.append((dmv * gt).astype(BF16))
            dya = dya + lax.dot_general(dy[0], wa, nt_dims, preferred_element_type=F32)
            do0 = do0 + lax.dot_general(dy[1], wb0, nt_dims, preferred_element_type=F32)
            do1 = do1 + lax.dot_general(dy[1], wb1, nt_dims, preferred_element_type=F32)
            dsg = dsg + lax.dot_general(dy[2], wc, nt_dims, preferred_element_type=F32)
            acc[j, :, 0:256] += lax.dot_general(yav, dy[0], tn_dims, preferred_element_type=F32)
            acc[j, :, 256:512] += lax.dot_general(o0, dy[1], tn_dims, preferred_element_type=F32)
            acc[j, :, 512:768] += lax.dot_general(o1, dy[1], tn_dims, preferred_element_type=F32)
            acc[j, :, 768:1024] += lax.dot_general(sgv, dy[2], tn_dims, preferred_element_type=F32)
        dya_ref[...] = dya
        do_ref[:, :256] = do0
        do_ref[:, 256:] = do1
        dsg_ref[...] = dsg

        @pl.when(i == pl.num_programs(0) - 1)
        def _():
            dc_ref[...] = acc[...].astype(dc_ref.dtype)

    return pl.pallas_call(
        body, out_shape=(jax.ShapeDtypeStruct((S, 3 * D), BF16), jax.ShapeDtypeStruct((S, POOL_W), F32),
                         jax.ShapeDtypeStruct((S, FOX_W), F32), jax.ShapeDtypeStruct((S, SGU_W), F32),
                         jax.ShapeDtypeStruct(grads.shape, grads.dtype), jax.ShapeDtypeStruct((1, 3 * D), F32)),
        grid=(S // tm,),
        in_specs=[row(D), gate(0), gate(1), gate(2), row(POOL_W), row(FOX_W), row(SGU_W), packed, full(1, 3 * D), ANY],
        out_specs=(row(3 * D), row(POOL_W), row(FOX_W), row(SGU_W), packed, full(1, 3 * D)),
        scratch_shapes=[pltpu.VMEM((4, 256, PACK_COLS), F32)], input_output_aliases={9: 4},
        compiler_params=_cp(("arbitrary",)), name=name)(dm, proj, proj, proj, ya, o, sg, packed_w, bg, grads)


def _xattn_probs(qh, kh):
    s = lax.dot_general(qh, kh, (((1,), (1,)), ((), ())), preferred_element_type=F32) * X_SCALE
    p = jnp.exp(s - jnp.max(s, axis=-1, keepdims=True))
    return p / jnp.sum(p, axis=-1, keepdims=True)


def _xattn_fwd(xq, kv, name, tq=256):
    S = xq.shape[0]
    M = kv.shape[0]

    def body(q_ref, k_ref, v_ref, o_ref):
        for h in range(XH):
            sl = slice(h * XHD, (h + 1) * XHD)
            p = _xattn_probs(q_ref[:, sl], k_ref[:, sl])
            o_ref[:, sl] = jnp.dot(p.astype(BF16), v_ref[:, sl], preferred_element_type=F32).astype(BF16)

    return pl.pallas_call(
        body, out_shape=jax.ShapeDtypeStruct((S, D), BF16), grid=(S // tq,),
        in_specs=[pl.BlockSpec((tq, D), lambda i: (i, 0)), pl.BlockSpec((M, D), lambda i: (0, 0)),
                  pl.BlockSpec((M, D), lambda i: (0, 1))],
        out_specs=pl.BlockSpec((tq, D), lambda i: (i, 0)), compiler_params=_cp(("parallel",)), name=name)(xq, kv, kv)


def _xattn_bwd(xq, kv, do, name, tq=256):
    S = xq.shape[0]
    M = kv.shape[0]

    def body(q_ref, k_ref, v_ref, do_ref, dq_ref, dkv_ref, dk_acc, dv_acc):
        i = pl.program_id(0)

        @pl.when(i == 0)
        def _():
            dk_acc[...] = jnp.zeros_like(dk_acc)
            dv_acc[...] = jnp.zeros_like(dv_acc)

        for h in range(XH):
            sl = slice(h * XHD, (h + 1) * XHD)
            qh, kh, vh, doh = q_ref[:, sl], k_ref[:, sl], v_ref[:, sl], do_ref[:, sl]
            p = _xattn_probs(qh, kh)
            dp = lax.dot_general(doh, vh, (((1,), (1,)), ((), ())), preferred_element_type=F32)
            ds = p * (dp - jnp.sum(p * dp, axis=-1, keepdims=True))
            dsb = (ds * X_SCALE).astype(BF16)
            dq_ref[:, sl] = jnp.dot(dsb, kh, preferred_element_type=F32).astype(BF16)
            dk_acc[:, sl] += lax.dot_general(dsb, qh, (((0,), (0,)), ((), ())), preferred_element_type=F32)
            dv_acc[:, sl] += lax.dot_general(p.astype(BF16), doh, (((0,), (0,)), ((), ())), preferred_element_type=F32)

        @pl.when(i == pl.num_programs(0) - 1)
        def _():
            dkv_ref[:, :D] = dk_acc[...].astype(BF16)
            dkv_ref[:, D:] = dv_acc[...].astype(BF16)

    return pl.pallas_call(
        body, out_shape=(jax.ShapeDtypeStruct((S, D), BF16), jax.ShapeDtypeStruct((M, 2 * D), BF16)), grid=(S // tq,),
        in_specs=[pl.BlockSpec((tq, D), lambda i: (i, 0)), pl.BlockSpec((M, D), lambda i: (0, 0)),
                  pl.BlockSpec((M, D), lambda i: (0, 1)), pl.BlockSpec((tq, D), lambda i: (i, 0))],
        out_specs=(pl.BlockSpec((tq, D), lambda i: (i, 0)), pl.BlockSpec((M, 2 * D), lambda i: (0, 0))),
        scratch_shapes=[pltpu.VMEM((M, D), F32), pltpu.VMEM((M, D), F32)],
        compiler_params=_cp(("arbitrary",)), name=name)(xq, kv, kv, do)


def _adam_math(gv, wv, mv, vv):
    c1 = 1.0 - ADAM_B1 ** ADAM_STEP
    c2 = 1.0 - ADAM_B2 ** ADAM_STEP
    nm = ADAM_B1 * mv + (1.0 - ADAM_B1) * gv
    nv = ADAM_B2 * vv + (1.0 - ADAM_B2) * (gv * gv)
    return -ADAM_LR * ((nm / c1) / (jnp.sqrt(nv / c2) + ADAM_EPS) + ADAM_WD * wv), nm, nv


def _adamw(g, w, m, v, name, block=None):
    if block is None:
        block = (1, 256 if g.shape[1] % 256 == 0 else g.shape[1], g.shape[2])
    grid = tuple(s // b for s, b in zip(g.shape, block))

    def body(g_ref, w_ref, m_ref, v_ref, d_ref, nm_ref, nv_ref):
        d_ref[...], nm_ref[...], nv_ref[...] = _adam_math(g_ref[...], w_ref[...], m_ref[...], v_ref[...])

    blk = pl.BlockSpec(block, lambda a, b, c: (a, b, c))
    return pl.pallas_call(
        body, out_shape=(jax.ShapeDtypeStruct(g.shape, F32),) * 3, grid=grid,
        in_specs=[blk] * 4, out_specs=(blk,) * 3, compiler_params=_cp(("parallel",) * 3), name=name)(g, w, m, v)


def _adamw_packed(red, w, m, v, g_index, name, token, tr=256):
    L, r, c = w.shape
    tr = min(tr, r)

    def body(g0_ref, g1_ref, w_ref, m_ref, v_ref, _, g_ref, d_ref, nm_ref, nv_ref):
        gv = jnp.where(pl.program_id(0) == 0, g0_ref[...], g1_ref[...])
        g_ref[0] = gv
        d_ref[0], nm_ref[0], nv_ref[0] = _adam_math(gv, w_ref[0], m_ref[0], v_ref[0])

    gblk = pl.BlockSpec((tr, c), lambda l, i: g_index(i))
    blk = pl.BlockSpec((1, tr, c), lambda l, i: (l, i, 0))
    return pl.pallas_call(
        body, out_shape=(jax.ShapeDtypeStruct(w.shape, F32),) * 4, grid=(L, r // tr),
        in_specs=[gblk, gblk, blk, blk, blk, pl.BlockSpec((8, 128), lambda l, i: (0, 0))], out_specs=(blk,) * 4,
        compiler_params=_cp(("parallel", "parallel")), name=name)(red[0], red[1], w, m, v, token)


def _row_tile(R):
    return next((t for t in (512, 496, 384, 256) if R % t == 0), R)


def _sum_slots(a, out_dtype, name):
    n, R, C = a.shape
    tr = _row_tile(R)

    def body(a_ref, o_ref):
        acc = a_ref[0].astype(F32)
        for k in range(1, n):
            acc = acc + a_ref[k].astype(F32)
        o_ref[...] = acc.astype(out_dtype)

    return pl.pallas_call(
        body, out_shape=jax.ShapeDtypeStruct((R, C), out_dtype), grid=(R // tr,),
        in_specs=[pl.BlockSpec((n, tr, C), lambda i: (0, i, 0))], out_specs=pl.BlockSpec((tr, C), lambda i: (i, 0)),
        compiler_params=_cp(("parallel",)), name=name)(a)


def _add_pair(a, b, name):
    n, R, C = a.shape
    tr = _row_tile(R)

    def body(a_ref, b_ref, o_ref):
        o_ref[...] = (a_ref[...].astype(F32) + b_ref[...].astype(F32)).astype(BF16)

    blk = pl.BlockSpec((1, tr, C), lambda k, i: (k, i, 0))
    return pl.pallas_call(
        body, out_shape=jax.ShapeDtypeStruct(a.shape, BF16), grid=(n, R // tr), in_specs=[blk, blk], out_specs=blk,
        compiler_params=_cp(("parallel", "parallel")), name=name)(a, b)


LANDING = pl.BlockSpec(memory_space=pltpu.VMEM)


def _landing_params(shape, dtype):
    return pltpu.CompilerParams(vmem_limit_bytes=math.prod(shape) * jnp.dtype(dtype).itemsize + 4 * 1024 * 1024)


def _place():
    return lax.axis_index("x"), lax.axis_index("y"), lax.axis_index("c")


def _other_chips(x, y):
    return [(1 - x, y), (x, 1 - y), (1 - x, 1 - y)]


def _row_chunks(rows, want, align=16):
    n = want
    while n > 1 and rows % (n * align):
        n -= 1
    return n


def _pair_split(g, name, nch=5):
    n, R, C = g.shape
    half = R // 2
    nch = _row_chunks(half, nch)
    cr = half // nch

    def body(g_ref, own_ref, got_ref, send_sems, recv_sems, local_sem):
        x, y, c = _place()
        mine0 = pl.multiple_of(c * half, 16)
        theirs0 = (1 - c) * half
        keep = pltpu.make_async_copy(g_ref.at[:, pl.ds(mine0, half), :], own_ref, local_sem)
        keep.start()
        cps = []
        for s in range(n):
            for q in range(nch):
                src = g_ref.at[s, pl.ds(pl.multiple_of(theirs0 + q * cr, 16), cr), :]
                cps.append(pltpu.make_async_remote_copy(
                    src_ref=src, dst_ref=got_ref.at[s, pl.ds(q * cr, cr), :], send_sem=send_sems.at[s * nch + q],
                    recv_sem=recv_sems.at[s * nch + q], device_id=(x, y, 1 - c), device_id_type=MESH))
        for cp in cps:
            cp.start()
        for cp in cps:
            cp.wait()
        keep.wait()

    sh = jax.ShapeDtypeStruct((n, half, C), g.dtype)
    return pl.pallas_call(
        body, out_shape=(sh, sh), in_specs=[ANY], out_specs=(ANY, LANDING),
        scratch_shapes=[pltpu.SemaphoreType.DMA((n * nch,)), pltpu.SemaphoreType.DMA((n * nch,)), pltpu.SemaphoreType.DMA],
        compiler_params=_landing_params(sh.shape, g.dtype), name=name)(g)


def _pair_gather(t, name, nch=10):
    R = t.shape[0]
    nch = _row_chunks(R, nch, 8)
    cr = R // nch

    def body(t_ref, o_ref, send_sems, recv_sems, local_sem):
        x, y, c = _place()
        own = pltpu.make_async_copy(t_ref, o_ref.at[c], local_sem)
        own.start()
        cps = [pltpu.make_async_remote_copy(src_ref=t_ref.at[pl.ds(q * cr, cr), :], dst_ref=o_ref.at[c, pl.ds(q * cr, cr), :],
                                            send_sem=send_sems.at[q], recv_sem=recv_sems.at[q], device_id=(x, y, 1 - c),
                                            device_id_type=MESH) for q in range(nch)]
        for cp in cps:
            cp.start()
        for cp in cps:
            cp.wait()
        own.wait()

    return pl.pallas_call(
        body, out_shape=jax.ShapeDtypeStruct((2,) + t.shape, t.dtype), in_specs=[ANY], out_specs=LANDING,
        scratch_shapes=[pltpu.SemaphoreType.DMA((nch,)), pltpu.SemaphoreType.DMA((nch,)), pltpu.SemaphoreType.DMA],
        compiler_params=_landing_params((2,) + t.shape, t.dtype), name=name)(t)


HBM = pl.BlockSpec(memory_space=pltpu.HBM)
SEM = pl.BlockSpec(memory_space=pltpu.SEMAPHORE)
SPLIT_COPY = pltpu.CompilerParams(has_side_effects=pltpu.SideEffectType.DATAFLOW_SIDE_EFFECTING)


def _split_exchange(src, rows, src_of, tag, nch=5):
    C = src.shape[-1]
    nch = _row_chunks(rows, nch)
    cr = rows // nch
    n = 3 * nch
    land_shape = (4, rows, C)

    def copies(src_ref, land_ref, send_sems, recv_sems):
        x, y, c = _place()
        j = 2 * x + y
        out = []
        for q in range(nch):
            for k, (px, py) in enumerate(_other_chips(x, y)):
                out.append(pltpu.make_async_remote_copy(
                    src_ref=src_of(src_ref, px, py, c, q * cr, cr), dst_ref=land_ref.at[j, pl.ds(q * cr, cr), :],
                    send_sem=send_sems.at[k * nch + q], recv_sem=recv_sems.at[k * nch + q], device_id=(px, py, c),
                    device_id_type=MESH))
        return out

    def start(src_ref, land_ref, send_sems, recv_sems, src_thru, land_thru, token):
        for cp in copies(src_ref, land_ref, send_sems, recv_sems):
            cp.start()
        token[...] = jnp.zeros_like(token)

    send_sems, recv_sems, src_thru, land_thru, token = pl.pallas_call(
        start, name=f"{tag}_start",
        out_shape=(pltpu.SemaphoreType.DMA((n,)), pltpu.SemaphoreType.DMA((n,)), pltpu.HBM(src.shape, src.dtype),
                   pltpu.HBM(land_shape, src.dtype), jax.ShapeDtypeStruct((8, 128), F32)),
        in_specs=(HBM, HBM), out_specs=(SEM, SEM, HBM, HBM, pl.BlockSpec(memory_space=pltpu.VMEM)),
        input_output_aliases={0: 2, 1: 3}, compiler_params=SPLIT_COPY)(
            pltpu.with_memory_space_constraint(src, pltpu.HBM),
            pltpu.with_memory_space_constraint(lax.empty(land_shape, src.dtype), pltpu.HBM))

    def finish(after):
        def wait(src_ref, land_ref, send_sems, recv_sems, after_ref, src_dead, got_ref):
            for cp in copies(src_ref, land_ref, send_sems, recv_sems):
                cp.wait_send()
                cp.wait_recv()

        return pl.pallas_call(
            wait, name=f"{tag}_wait", out_shape=(pltpu.HBM(src.shape, src.dtype), pltpu.HBM(land_shape, src.dtype)),
            in_specs=(HBM, HBM, SEM, SEM, ANY), out_specs=(HBM, HBM), input_output_aliases={0: 0, 1: 1},
            compiler_params=SPLIT_COPY)(src_thru, land_thru, send_sems, recv_sems, after)

    return token, finish


def _gather_finish(shard, land, name, nch=5):
    R, C = shard.shape
    half = R // 2
    nch = _row_chunks(half, nch)
    cr = half // nch

    def body(s_ref, l_ref, o_ref, send_sems, recv_sems, local_sems):
        x, y, c = _place()
        j = 2 * x + y
        mine0 = c * half
        local = [pltpu.make_async_copy(s_ref, o_ref.at[j], local_sems.at[0])]
        remote = []
        for k, (px, py) in enumerate(_other_chips(x, y)):
            jj = 2 * px + py
            local.append(pltpu.make_async_copy(l_ref.at[jj], o_ref.at[jj, pl.ds(pl.multiple_of(mine0, 16), half), :],
                                               local_sems.at[1 + k]))
            for q in range(nch):
                remote.append(pltpu.make_async_remote_copy(
                    src_ref=l_ref.at[jj, pl.ds(q * cr, cr), :],
                    dst_ref=o_ref.at[jj, pl.ds(pl.multiple_of(mine0 + q * cr, 16), cr), :], send_sem=send_sems.at[k * nch + q],
                    recv_sem=recv_sems.at[k * nch + q], device_id=(x, y, 1 - c), device_id_type=MESH))
        for cp in local + remote:
            cp.start()
        for cp in remote + local:
            cp.wait()

    return pl.pallas_call(
        body, out_shape=jax.ShapeDtypeStruct((4, R, C), shard.dtype), in_specs=[ANY, ANY], out_specs=LANDING,
        scratch_shapes=[pltpu.SemaphoreType.DMA((3 * nch,)), pltpu.SemaphoreType.DMA((3 * nch,)), pltpu.SemaphoreType.DMA((4,))],
        compiler_params=_landing_params((4, R, C), shard.dtype), name=name)(shard, land)


def _sum_slots_own(land, own, name):
    n, R, C = land.shape
    tr = _row_tile(R)
    me = (2 * lax.axis_index("x") + lax.axis_index("y")).astype(jnp.int32).reshape(1)
    if own.ndim == 3:
        own_spec = pl.BlockSpec((None, tr, C), lambda i, me: (me[0], i, 0))
    else:
        own_spec = pl.BlockSpec((tr, C), lambda i, me: (i, 0))

    def body(me_ref, land_ref, own_ref, o_ref):
        acc = None
        for k in range(n):
            v = jnp.where(me_ref[0] == k, own_ref[...], land_ref[k]).astype(F32)
            acc = v if acc is None else acc + v
        o_ref[...] = acc

    return pl.pallas_call(
        body, out_shape=jax.ShapeDtypeStruct((R, C), F32),
        grid_spec=pltpu.PrefetchScalarGridSpec(
            num_scalar_prefetch=1, grid=(R // tr,),
            in_specs=[pl.BlockSpec((n, tr, C), lambda i, me: (0, i, 0)), own_spec],
            out_specs=pl.BlockSpec((tr, C), lambda i, me: (i, 0))),
        compiler_params=_cp(("parallel",)), name=name)(me, land, own)


def _reduce_begin(g, tag):
    own, got = _pair_split(g, f"rs_pair_{tag}")
    p = _add_pair(own, got, f"rs_add_{tag}")
    token, finish = _split_exchange(p, p.shape[1], lambda ref, px, py, c, r0, cr: ref.at[2 * px + py, pl.ds(r0, cr), :],
                                    f"rs_a2a_{tag}")
    return (finish, g.shape, tag), token


def _reduce_end(state, after):
    finish, shape, tag = state
    p, land = finish(after)
    t = _sum_slots_own(land, p, f"rs_sum_{tag}")
    return _pair_gather(t, f"rs_join_{tag}").reshape(shape[1], shape[2])


def _all_reduce_begin(v, tag):
    p = _sum_slots(_pair_gather(v, f"ar_pair_{tag}"), F32, f"ar_add_{tag}")
    token, finish = _split_exchange(p, p.shape[0], lambda ref, px, py, c, r0, cr: ref.at[pl.ds(r0, cr), :], f"ar_a2a_{tag}")
    return (finish, tag), token


def _all_reduce_end(state, after):
    finish, tag = state
    p, land = finish(after)
    return _sum_slots_own(land, p, f"ar_sum_{tag}")


def _gather_begin(shard, tag):
    half = shard.shape[0] // 2
    token, finish = _split_exchange(
        shard, half, lambda ref, px, py, c, r0, cr: ref.at[pl.ds(pl.multiple_of(c * half + r0, 16), cr), :], f"gather_{tag}")
    return (finish, tag), token


def _gather_end(state, after):
    finish, tag = state
    shard, land = finish(after)
    return _gather_finish(shard, land, f"gather_{tag}_finish")


R_BRANCH, R_OUT, R_WIN, ROWS_A = 0, 256, 512, 1888
R_FF1, R_FF2, R_XKV, R_XQ, R_XO, ROWS_B = 0, 1024, 2048, 2560, 2816, 3072
WIN_ROWS = N_IN // 4


def _w_in_t(a):
    return jnp.transpose(a, (2, 0, 1))


def _pack_shard(w, l):
    xkv, wb = w['w_xkv'][l], w['w_branch_b'][l]
    a = [jnp.concatenate([w['w_branch_a'][l], wb[:256], wb[256:], w['w_branch_c'][l]], axis=1), w['w_out'][l],
         jnp.pad(_w_in_t(w['w_in'])[:, l, :], ((0, ROWS_A - R_WIN - WIN_ROWS), (0, 0)))]
    b = [w['w_ff1'][l], w['w_ff2'][l], jnp.concatenate([xkv[:512], xkv[512:]], axis=1), w['w_xq'][l], w['w_xo'][l]]
    return jnp.concatenate(a, axis=0).astype(BF16), jnp.concatenate(b, axis=0).astype(BF16)


def _w_in_rows(gathered):
    t = gathered[:, R_WIN:R_WIN + WIN_ROWS, :].reshape(N_IN, PACK_COLS)
    return jnp.concatenate([t[2312:5384], t[256:1792], t[1800:2312], t[0:256],
                            jnp.pad(t[1792:1800], ((0, NP - P_F - 8), (0, 0)))], axis=0)


def _w_in_grad_rows(grads, dwt):
    t = jnp.concatenate([dwt[P_A:P_A + 256], dwt[P_Q:P_Q + 1536], dwt[P_F:P_F + 8], dwt[P_C:P_C + 512], dwt[P_G:P_G + 3072]],
                        axis=0)
    return lax.dynamic_update_slice(grads, t.reshape(4, WIN_ROWS, PACK_COLS).astype(grads.dtype), (0, R_WIN, 0))


def _small_prep(sw, l):
    eye = jnp.eye(4, dtype=F32)
    bd = jnp.einsum('gh,gcd->gchd', eye, sw['pool_w'][l]).reshape(POOL_W, POOL_W).astype(BF16)
    tril = jnp.tril(jnp.ones((SGU_CHUNK, SGU_CHUNK), F32))
    wm = (sw['sgu_w'][l] * tril[None]).astype(BF16)
    return dict(
        g_mix=sw['norm_mix_g'][l][None], g_x=sw['norm_xattn_g'][l][None], g_mem=sw['norm_mem_g'][l][None],
        g_ffn=sw['norm_ffn_g'][l][None], bd=bd, pool_scale=sw['pool_scale'][l][None],
        bf=jnp.pad(sw['b_forget'][l], (0, FCOLS - 8))[None], sgu_g=sw['sgu_norm_g'][l][None], wm=wm,
        wmt=jnp.transpose(wm, (0, 2, 1)), sgu_bias=jnp.repeat(sw['sgu_b'][l].T, 64, axis=1), bg=sw['b_gate'][l][None])


def _rows4(r0):
    return dict(n=D, k=D, b_block=(4, 256, 512), b_index=lambda i, j, k: (0, r0 // 256, j))


def _rows_t(r0):
    return dict(tb=True, n=D, k=D, tn=D, b_block=(4, 256, PACK_COLS), b_index=lambda i, j, k: (0, r0 // 256, 0))


def _rows_grad(r0):
    return dict(ta=True, tm=D, tn=512, o_block=(4, 256, 512), o_index=lambda i, j, k: (0, r0 // 256, j))


def _add_to(r, e):
    return e + r


def _after(v, token):
    return v if token is None else v + token[0, 0]


def _layer_fwd(x, mem, GA, w_in_t, sp, l, token, second):
    t = f"l{l}"
    S = x.shape[0]
    h = _rms_fwd(x, _after(sp['g_mix'], token), f"rms_mix_{t}")
    proj = _mm(h, w_in_t, name=f"proj_{t}", out_dtype=F32, tb=True)
    d, ya = _pool_fwd(proj, sp['bd'], sp['pool_scale'], f"pool_fwd_{t}")
    fcum = _fgate_fwd(proj, sp['bf'], f"fgate_fwd_{t}")
    f8 = fcum[:, :8]
    fcol = f8.reshape(S, 4, 2).transpose(1, 0, 2)
    frow = f8.T.reshape(4, 2, S)
    qkv = proj[:, P_Q:P_Q + 3 * FOX_W].astype(BF16)
    o, o32, lse = _fox_fwd(qkv, fcol, frow, f"fox_fwd_{t}")
    sg = _sgu_fwd(proj, sp['sgu_g'], sp['wm'], sp['sgu_bias'], f"sgu_fwd_{t}")
    merged = _merge_fwd(proj, ya, o, sg, GA, sp['bg'], f"merge_fwd_{t}")
    x1 = _mm(merged, GA, name=f"out_{t}", out_dtype=F32, extra=x, epi=_add_to, **_rows4(R_OUT))
    GB, token = second(x1)
    hx = _rms_fwd(x1, _after(sp['g_x'], token), f"rms_x_{t}")
    hm = _rms_fwd(mem, sp['g_mem'], f"rms_mem_{t}")
    xq = _mm(hx, GB, name=f"xq_{t}", out_dtype=BF16, **_rows4(R_XQ))
    kv = _mm(hm, GB, name=f"xkv_{t}", out_dtype=BF16, n=2 * D, k=D, tn=512, tk=512, b_block=(None, 512, 512),
             b_index=lambda i, j, k: (j, R_XKV // 512, k))
    o2 = _xattn_fwd(xq, kv, f"xattn_fwd_{t}")
    x2 = _mm(o2, GB, name=f"xo_{t}", out_dtype=F32, extra=x1, epi=_add_to, **_rows4(R_XO))
    hf = _rms_fwd(x2, sp['g_ffn'], f"rms_ffn_{t}")
    z = _mm(hf, GB, name=f"ff1_{t}", out_dtype=F32, n=D_FF, k=D, tn=512, b_block=(None, 1024, 512),
            b_index=lambda i, j, k: (j // 2, R_FF1 // 1024, j % 2))
    x3 = _mm(z, GB, name=f"ff2_{t}", out_dtype=F32, a_fn=_relu2, extra=x2, epi=_add_to, n=D, k=D_FF, tk=1024,
             b_block=(None, 1024, 512), b_index=lambda i, j, k: (k, R_FF2 // 1024, j))
    saved = dict(x=x, h=h, proj=proj, d=d, ya=ya, fcol=fcol, frow=frow, qkv=qkv, o=o, o32=o32, lse=lse, sg=sg, merged=merged,
                 x1=x1, hx=hx, hm=hm, xq=xq, kv=kv, o2=o2, x2=x2, hf=hf, z=z, GA=GA, GB=GB, w_in_t=w_in_t)
    return x3, saved


def _layer_bwd(dx3, mem, sp, sv, l, token, early):
    t = f"l{l}"
    S = dx3.shape[0]
    GA, GB = sv['GA'], sv['GB']
    gs = {}
    dx3 = _after(dx3, token)
    gb = lax.empty((4, ROWS_B, PACK_COLS), BF16)
    dz = _mm(dx3, GB, name=f"d_a2_{t}", out_dtype=BF16, tb=True, n=D_FF, k=D, tn=512, b_block=(None, 512, PACK_COLS),
             b_index=lambda i, j, k: (j // 2, R_FF2 // 512 + j % 2, 0), extra=sv['z'],
             epi=lambda r, e: r * (2.0 * jnp.maximum(e, 0.0)))
    gb = _mm(sv['z'], dx3, name=f"dw_ff2_{t}", out_dtype=BF16, ta=True, a_fn=_relu2, into=gb, tm=1024, tn=512,
             o_block=(None, 1024, 512), o_index=lambda i, j, k: (i, R_FF2 // 1024, j))
    gb = _mm(sv['hf'], dz, name=f"dw_ff1_{t}", out_dtype=BF16, ta=True, into=gb, tm=1024, tn=512,
             o_block=(None, 1024, 512), o_index=lambda i, j, k: (j // 2, R_FF1 // 1024, j % 2))
    dhf = _mm(dz, GB, name=f"d_hf_{t}", out_dtype=F32, tb=True, n=D, k=D_FF, tn=512, tk=1024, b_block=(None, 512, PACK_COLS),
              b_index=lambda i, j, k: (k, R_FF1 // 512 + j, 0))
    dx2, gs['norm_ffn_g'] = _rms_bwd(dhf, sv['x2'], sp['g_ffn'], dx3, f"rms_ffn_bwd_{t}")
    do2 = _mm(dx2, GB, name=f"d_o2_{t}", out_dtype=BF16, **_rows_t(R_XO))
    gb = _mm(sv['o2'], dx2, name=f"dw_xo_{t}", out_dtype=BF16, into=gb, **_rows_grad(R_XO))
    dxq, dkv = _xattn_bwd(sv['xq'], sv['kv'], do2, f"xattn_bwd_{t}")
    gb = _mm(sv['hm'], dkv, name=f"dw_xkv_{t}", out_dtype=BF16, ta=True, into=gb, tm=512, tn=512,
             o_block=(None, 512, 512), o_index=lambda i, j, k: (j, R_XKV // 512, i))
    dhm = _mm(dkv, GB, name=f"d_hm_{t}", out_dtype=F32, tb=True, n=D, k=2 * D, tn=512, tk=512, b_block=(None, 512, 512),
              b_index=lambda i, j, k: (k, R_XKV // 512, j))
    gs['norm_mem_g'] = _rms_bwd(dhm, mem, sp['g_mem'], None, f"rms_mem_bwd_{t}")
    gb = _mm(sv['hx'], dxq, name=f"dw_xq_{t}", out_dtype=BF16, into=gb, **_rows_grad(R_XQ))
    token = early(gb)
    dhx = _mm(dxq, GB, name=f"d_hx_{t}", out_dtype=F32, **_rows_t(R_XQ))
    dx1, gs['norm_xattn_g'] = _rms_bwd(dhx, sv['x1'], _after(sp['g_x'], token), dx2, f"rms_x_bwd_{t}")
    ga = jnp.zeros((4, ROWS_A, PACK_COLS), BF16)
    ga = _mm(sv['merged'], dx1, name=f"dw_out_{t}", out_dtype=BF16, into=ga, **_rows_grad(R_OUT))
    dm = _mm(dx1, GA, name=f"d_merged_{t}", out_dtype=F32, **_rows_t(R_OUT))
    dg, dya, do, dsg, ga, gs['b_gate'] = _merge_bwd(dm, sv['proj'], sv['ya'], sv['o'], sv['sg'], GA, sp['bg'], ga, f"merge_bwd_{t}")
    dc, dws, dbias, gs['sgu_norm_g'] = _sgu_bwd(dsg, sv['proj'], sp['sgu_g'], sp['wm'], sp['wmt'], sp['sgu_bias'], f"sgu_bwd_{t}")
    tril = jnp.tril(jnp.ones((SGU_CHUNK, SGU_CHUNK), F32))
    gs['sgu_w'] = dws * tril[None]
    gs['sgu_b'] = dbias.reshape(SGU_CHUNK, 4, 64).sum(-1).T
    dq, dk, dv, dfrow, dfcol = _fox_bwd(sv['qkv'], sv['o32'], do, sv['lse'], sv['fcol'], sv['frow'], f"fox_bwd_{t}")
    dF = jnp.pad(dfrow.reshape(8, S).T + dfcol.transpose(1, 0, 2).reshape(S, 8), ((0, 0), (0, FCOLS - 8)))
    df, dbf = _fgate_bwd(dF, sv['proj'], sp['bf'], f"fgate_bwd_{t}")
    gs['b_forget'] = dbf[:, :8]
    da, dbd, gs['pool_scale'] = _pool_bwd(dya, sv['d'], sp['bd'], sp['pool_scale'], f"pool_bwd_{t}")
    gs['pool_w'] = jnp.stack([dbd[g * 64:(g + 1) * 64, g * 64:(g + 1) * 64] for g in range(4)])
    dproj = jnp.concatenate([dg, dq, dk, dv, dc, da, df], axis=1)
    dwt = _mm(dproj, sv['h'], name=f"dw_in_{t}", out_dtype=BF16, ta=True, tm=512, tn=1024)
    ga = _w_in_grad_rows(ga, dwt)
    dh = _mm(dproj, sv['w_in_t'], name=f"d_h_{t}", out_dtype=F32, tk=512, tn=D)
    dx, gs['norm_mix_g'] = _rms_bwd(dh, sv['x'], sp['g_mix'], dx1, f"rms_mix_bwd_{t}")
    return dx, ga, gs


SMALL_ROWS = 1424
GRAD_BLOCKS = {
    'w_ff1': ('b', lambda i: (R_FF1 // 256 + i, 0)), 'w_ff2': ('b', lambda i: (R_FF2 // 256 + i, 0)),
    'w_xq': ('b', lambda i: (R_XQ // 256 + i, 0)), 'w_xo': ('b', lambda i: (R_XO // 256 + i, 0)),
    'w_xkv': ('b', lambda i: (R_XKV // 256 + i % 2, i // 2)), 'w_out': ('a', lambda i: (R_OUT // 256 + i, 0)),
    'w_branch_a': ('a', lambda i: (R_BRANCH // 256, 0)), 'w_branch_b': ('a', lambda i: (R_BRANCH // 256, 1 + i)),
    'w_branch_c': ('a', lambda i: (R_BRANCH // 256, 3)),
}


def _pack_small(parts):
    flat = jnp.concatenate([p.reshape(-1) for p in parts])
    return jnp.pad(flat, (0, SMALL_ROWS * 128 - flat.shape[0])).reshape(SMALL_ROWS, 128)


def _unpack_small(buf, shapes):
    flat, out, r = buf.reshape(-1), [], 0
    for s in shapes:
        n = math.prod(s)
        out.append(flat[r:r + n].reshape(s))
        r += n
    return out


def kernel(x, mem, norm_mix_g, w_in, b_forget, pool_w, pool_scale, sgu_norm_g, sgu_w, sgu_b, w_branch_a, w_branch_b, w_branch_c, b_gate, w_out, norm_xattn_g, norm_mem_g, w_xq, w_xkv, w_xo, norm_ffn_g, w_ff1, w_ff2, final_norm_g, loss_target, m_norm_mix_g, m_w_in, m_b_forget, m_pool_w, m_pool_scale, m_sgu_norm_g, m_sgu_w, m_sgu_b, m_w_branch_a, m_w_branch_b, m_w_branch_c, m_b_gate, m_w_out, m_norm_xattn_g, m_norm_mem_g, m_w_xq, m_w_xkv, m_w_xo, m_norm_ffn_g, m_w_ff1, m_w_ff2, m_final_norm_g, v_norm_mix_g, v_w_in, v_b_forget, v_pool_w, v_pool_scale, v_sgu_norm_g, v_sgu_w, v_sgu_b, v_w_branch_a, v_w_branch_b, v_w_branch_c, v_b_gate, v_w_out, v_norm_xattn_g, v_norm_mem_g, v_w_xq, v_w_xkv, v_w_xo, v_norm_ffn_g, v_w_ff1, v_w_ff2, v_final_norm_g):
    args = (norm_mix_g, w_in, b_forget, pool_w, pool_scale, sgu_norm_g, sgu_w, sgu_b, w_branch_a, w_branch_b, w_branch_c, b_gate,
            w_out, norm_xattn_g, norm_mem_g, w_xq, w_xkv, w_xo, norm_ffn_g, w_ff1, w_ff2, final_norm_g)
    margs = (m_norm_mix_g, m_w_in, m_b_forget, m_pool_w, m_pool_scale, m_sgu_norm_g, m_sgu_w, m_sgu_b, m_w_branch_a, m_w_branch_b,
             m_w_branch_c, m_b_gate, m_w_out, m_norm_xattn_g, m_norm_mem_g, m_w_xq, m_w_xkv, m_w_xo, m_norm_ffn_g, m_w_ff1, m_w_ff2,
             m_final_norm_g)
    vargs = (v_norm_mix_g, v_w_in, v_b_forget, v_pool_w, v_pool_scale, v_sgu_norm_g, v_sgu_w, v_sgu_b, v_w_branch_a, v_w_branch_b,
             v_w_branch_c, v_b_gate, v_w_out, v_norm_xattn_g, v_norm_mem_g, v_w_xq, v_w_xkv, v_w_xo, v_norm_ffn_g, v_w_ff1, v_w_ff2,
             v_final_norm_g)
    w = dict(zip(W_NAMES, args))
    mo = dict(zip(W_NAMES, margs))
    vo = dict(zip(W_NAMES, vargs))
    xs, mems, tgt = x[0], mem[0], loss_target[0]
    shards = [_pack_shard(w, l) for l in range(DEPTH)]
    preps = [_small_prep(w, l) for l in range(DEPTH)]

    first_a, _ = _gather_begin(shards[0][0], "a_l0")
    pending_b, token = _gather_begin(shards[0][1], "b_l0")
    GA = None
    act, saved = xs, []
    for l in range(DEPTH):
        nxt = {}
        if l + 1 < DEPTH:
            nxt['a'], ta = _gather_begin(shards[l + 1][0], f"a_l{l + 1}")
            token = ta if token is None else token + ta
        if l == 0:
            GA = _gather_end(first_a, shards[DEPTH - 1][1])

        def second(x1, l=l, pending_b=pending_b, nxt=nxt):
            GB = _gather_end(pending_b, x1)
            if l + 1 == DEPTH:
                return GB, None
            nxt['b'], tb = _gather_begin(shards[l + 1][1], f"b_l{l + 1}")
            return GB, tb

        act, sv = _layer_fwd(act, mems, GA, _w_in_rows(GA), preps[l], l, token, second)
        saved.append(sv)
        if l + 1 < DEPTH:
            GA = _gather_end(nxt['a'], act)
            pending_b, token = nxt['b'], None
    loss_part, dact, d_final_g = _loss_head(act, w['final_norm_g'][None], tgt, "loss_head")

    red_a, red_b, small_g = [None] * DEPTH, [None] * DEPTH, [None] * DEPTH
    token, state_a = None, None
    for l in reversed(range(DEPTH)):
        early = {}

        def start_b(gb, l=l, early=early):
            early['state'], tok = _reduce_begin(gb, f"b_l{l}")
            return tok

        dact, ga, small_g[l] = _layer_bwd(dact, mems, preps[l], saved[l], l, token, start_b)
        if state_a is not None:
            red_a[l + 1] = _reduce_end(state_a, dact)
        red_b[l] = _reduce_end(early['state'], dact)
        state_a, token = _reduce_begin(ga, f"a_l{l}")
    grad_x = dact[None]
    per_layer = [n for n in SMALL_NAMES if n != 'final_norm_g']
    small_shapes = [w[n].shape for n in per_layer] + [(D,), (1,)]
    parts = [jnp.stack([small_g[l][n].reshape(w[n].shape[1:]) for l in range(DEPTH)]) for n in per_layer]
    state_small, token_small = _all_reduce_begin(_pack_small(parts + [d_final_g.reshape(D), loss_part.reshape(1)]), "small")
    token = token + token_small

    grads, delta, new_m, new_v = {}, {}, {}, {}
    for n, (buf, g_index) in GRAD_BLOCKS.items():
        if buf == 'b':
            grads[n], delta[n], new_m[n], new_v[n] = _adamw_packed(red_b, w[n], mo[n], vo[n], g_index, f"adamw_{n}", token)
    red_a[0] = _reduce_end(state_a, new_v['w_xkv'])
    small_red = _unpack_small(_all_reduce_end(state_small, red_a[0]), small_shapes)
    grads.update(zip(per_layer + ['final_norm_g'], small_red[:-1]))
    loss = small_red[-1].reshape(())
    for n, (buf, g_index) in GRAD_BLOCKS.items():
        if buf == 'a':
            grads[n], delta[n], new_m[n], new_v[n] = _adamw_packed(red_a, w[n], mo[n], vo[n], g_index, f"adamw_{n}", token)
    g_t = jnp.stack([r[R_WIN:R_WIN + WIN_ROWS] for r in red_a], axis=1)
    upd = _adamw(g_t, _w_in_t(w['w_in']), _w_in_t(mo['w_in']), _w_in_t(vo['w_in']), "adamw_w_in", block=(WIN_ROWS, DEPTH, 128))
    grads['w_in'], delta['w_in'], new_m['w_in'], new_v['w_in'] = [jnp.transpose(a, (1, 2, 0)) for a in (g_t,) + tuple(upd)]
    small_all = per_layer + ['final_norm_g']
    shapes_all = [w[n].shape for n in small_all]
    packed = [_pack_small([d[n] for n in small_all])[None] for d in (grads, w, mo, vo)]
    ds, ms, vs = _adamw(*packed, "adamw_small")
    for n, a, b, c in zip(small_all, _unpack_small(ds[0], shapes_all), _unpack_small(ms[0], shapes_all), _unpack_small(vs[0], shapes_all)):
        delta[n], new_m[n], new_v[n] = a, b, c

    return (loss, grad_x, *[grads[n] for n in W_NAMES], *[delta[n] for n in W_NAMES], *[new_m[n] for n in W_NAMES],
            *[new_v[n] for n in W_NAMES])
```

```python
import math

import jax
import jax.numpy as jnp
from jax import lax
from jax.experimental import pallas as pl
from jax.experimental.pallas import tpu as pltpu

F32 = jnp.float32
BF16 = jnp.bfloat16

D = 1024
DEPTH = 2
POOL_W = 256
FOX_W = 512
SGU_W = 256
SGU_CHUNK = 128
N_IN = 5384
P_G, P_Q, P_K, P_V, P_C, P_A, P_F = 0, 3072, 3584, 4096, 4608, 5120, 5376
NP = 5632
XH, XHD = 4, 256
D_FF = 4096
EPS = 1e-6
NEG = -1e30
FOX_SCALE = 64 ** -0.5
X_SCALE = 256 ** -0.5
GELU_K = math.sqrt(2.0 / math.pi)
GELU_C = 0.044715

ADAM_LR, ADAM_B1, ADAM_B2, ADAM_EPS, ADAM_WD, ADAM_STEP = 0.001, 0.9, 0.999, 1e-08, 0.01, 10

VMEM_LIMIT = 48 * 1024 * 1024
MESH = pl.DeviceIdType.MESH

IN_NAMES = ['x', 'mem', 'norm_mix_g', 'w_in', 'b_forget', 'pool_w', 'pool_scale', 'sgu_norm_g', 'sgu_w', 'sgu_b',
            'w_branch_a', 'w_branch_b', 'w_branch_c', 'b_gate', 'w_out', 'norm_xattn_g', 'norm_mem_g', 'w_xq',
            'w_xkv', 'w_xo', 'norm_ffn_g', 'w_ff1', 'w_ff2', 'final_norm_g']
W_NAMES = IN_NAMES[2:]
BIG_NAMES = ['w_in', 'w_branch_a', 'w_branch_b', 'w_branch_c', 'w_out', 'w_xq', 'w_xkv', 'w_xo', 'w_ff1', 'w_ff2']
SMALL_NAMES = [n for n in W_NAMES if n not in BIG_NAMES]
PACK_COLS = 1024


ANY = pl.BlockSpec(memory_space=pl.ANY)


def _cp(sem=None):
    return pltpu.CompilerParams(dimension_semantics=sem, vmem_limit_bytes=VMEM_LIMIT)


def _mm(a, b, *, name, out_dtype, ta=False, tb=False, tm=1024, tn=512, tk=1024, a_fn=None, extra=None, epi=None,
        n=None, k=None, b_block=None, b_index=None, into=None, o_block=None, o_index=None):
    M = a.shape[1] if ta else a.shape[0]
    K = k if k is not None else (a.shape[0] if ta else a.shape[1])
    N = n if n is not None else (b.shape[0] if tb else b.shape[1])
    tm, tn, tk = min(tm, M), min(tn, N), min(tk, K)
    assert M % tm == 0 and N % tn == 0 and K % tk == 0, (name, M, N, K)
    nk = K // tk
    a_spec = pl.BlockSpec((tk, tm), lambda i, j, k: (k, i)) if ta else pl.BlockSpec((tm, tk), lambda i, j, k: (i, k))
    if b_block is not None:
        b_spec = pl.BlockSpec(b_block, b_index)
    else:
        b_spec = pl.BlockSpec((tn, tk), lambda i, j, k: (j, k)) if tb else pl.BlockSpec((tk, tn), lambda i, j, k: (k, j))
    dn = (((0 if ta else 1,), (1 if tb else 0,)), ((), ()))
    tile = pl.BlockSpec((tm, tn), lambda i, j, k: (i, j))
    o_spec = pl.BlockSpec(o_block, o_index) if into is not None else tile
    in_specs = [a_spec, b_spec] + ([tile] if extra is not None else []) + ([ANY] if into is not None else [])
    n_in = len(in_specs)

    def body(*refs):
        a_ref, b_ref = refs[0], refs[1]
        e_ref = refs[2] if extra is not None else None
        o_ref, acc_ref = refs[n_in], refs[n_in + 1]
        kk = pl.program_id(2)

        @pl.when(kk == 0)
        def _():
            acc_ref[...] = jnp.zeros_like(acc_ref)

        av = a_ref[...]
        if a_fn is not None:
            av = a_fn(av)
        bv = b_ref[...]
        if bv.ndim == 3:
            bv = bv.reshape(-1, bv.shape[-1])
        acc_ref[...] += lax.dot_general(av.astype(BF16), bv.astype(BF16), dn, preferred_element_type=F32)

        @pl.when(kk == nk - 1)
        def _():
            r = acc_ref[...]
            if epi is not None:
                r = epi(r, e_ref[...])
            o_ref[...] = r.astype(o_ref.dtype).reshape(o_ref.shape)

    args = (a, b) + ((extra,) if extra is not None else ()) + ((into,) if into is not None else ())
    out_shape = jax.ShapeDtypeStruct(into.shape, into.dtype) if into is not None else jax.ShapeDtypeStruct((M, N), out_dtype)
    return pl.pallas_call(
        body, out_shape=out_shape, grid=(M // tm, N // tn, nk), in_specs=in_specs, out_specs=o_spec,
        scratch_shapes=[pltpu.VMEM((tm, tn), F32)], input_output_aliases={n_in - 1: 0} if into is not None else {},
        compiler_params=_cp(("parallel", "parallel", "arbitrary")), name=name)(*args)


def _relu2(z):
    r = jnp.maximum(z, 0.0)
    return r * r


def _rms_fwd(x, g, name, tr=256):
    R, n = x.shape
    tr = min(tr, R)

    def body(x_ref, g_ref, h_ref):
        xv = x_ref[...]
        rstd = lax.rsqrt(jnp.mean(xv * xv, axis=-1, keepdims=True) + EPS)
        h_ref[...] = (xv * rstd * g_ref[...]).astype(BF16)

    return pl.pallas_call(
        body, out_shape=jax.ShapeDtypeStruct((R, n), BF16), grid=(R // tr,),
        in_specs=[pl.BlockSpec((tr, n), lambda i: (i, 0)), pl.BlockSpec((1, n), lambda i: (0, 0))],
        out_specs=pl.BlockSpec((tr, n), lambda i: (i, 0)), compiler_params=_cp(("parallel",)), name=name)(x, g)


def _rms_bwd(dh, x, g, dres, name, tr=256):
    R, n = x.shape
    tr = min(tr, R)
    need_dx = dres is not None

    def body(*refs):
        if need_dx:
            dh_ref, x_ref, g_ref, r_ref, dx_ref, dg_ref = refs
        else:
            dh_ref, x_ref, g_ref, dg_ref = refs
        i = pl.program_id(0)
        xv = x_ref[...]
        dhv = dh_ref[...].astype(F32)
        rstd = lax.rsqrt(jnp.mean(xv * xv, axis=-1, keepdims=True) + EPS)
        xhat = xv * rstd

        @pl.when(i == 0)
        def _():
            dg_ref[...] = jnp.zeros_like(dg_ref)

        dg_ref[...] += jnp.sum(dhv * xhat, axis=0, keepdims=True)
        if need_dx:
            t = dhv * g_ref[...]
            dx_ref[...] = r_ref[...] + rstd * (t - xhat * jnp.mean(t * xhat, axis=-1, keepdims=True))

    row = pl.BlockSpec((tr, n), lambda i: (i, 0))
    vec = pl.BlockSpec((1, n), lambda i: (0, 0))
    if need_dx:
        return pl.pallas_call(
            body, out_shape=(jax.ShapeDtypeStruct((R, n), F32), jax.ShapeDtypeStruct((1, n), F32)), grid=(R // tr,),
            in_specs=[row, row, vec, row], out_specs=(row, vec), compiler_params=_cp(("arbitrary",)), name=name)(dh, x, g, dres)
    return pl.pallas_call(
        body, out_shape=jax.ShapeDtypeStruct((1, n), F32), grid=(R // tr,),
        in_specs=[row, row, vec], out_specs=vec, compiler_params=_cp(("arbitrary",)), name=name)(dh, x, g)


def _loss_head(x, g, tgt, name, tr=256):
    R, n = x.shape

    def body(x_ref, g_ref, t_ref, loss_ref, dx_ref, dg_ref):
        i = pl.program_id(0)
        xv = x_ref[...]
        gv = g_ref[...]
        rstd = lax.rsqrt(jnp.mean(xv * xv, axis=-1, keepdims=True) + EPS)
        xhat = xv * rstd
        e = xhat * gv - t_ref[...]

        @pl.when(i == 0)
        def _():
            loss_ref[...] = jnp.zeros_like(loss_ref)
            dg_ref[...] = jnp.zeros_like(dg_ref)

        loss_ref[...] += 0.5 * jnp.sum(jnp.sum(e * e, axis=-1, keepdims=True) / n, axis=0, keepdims=True)
        dy = e / n
        dg_ref[...] += jnp.sum(dy * xhat, axis=0, keepdims=True)
        t = dy * gv
        dx_ref[...] = rstd * (t - xhat * jnp.mean(t * xhat, axis=-1, keepdims=True))

    row = pl.BlockSpec((tr, n), lambda i: (i, 0))
    vec = pl.BlockSpec((1, n), lambda i: (0, 0))
    one = pl.BlockSpec((1, 1), lambda i: (0, 0))
    return pl.pallas_call(
        body, out_shape=(jax.ShapeDtypeStruct((1, 1), F32), jax.ShapeDtypeStruct((R, n), F32), jax.ShapeDtypeStruct((1, n), F32)),
        grid=(R // tr,), in_specs=[row, vec, row], out_specs=(one, row, vec),
        compiler_params=_cp(("arbitrary",)), name=name)(x, g, tgt)


def _pool_masks(S):
    row = lax.broadcasted_iota(jnp.int32, (S, POOL_W), 0)
    grp = lax.broadcasted_iota(jnp.int32, (S, POOL_W), 1) // 64
    win = jnp.where(grp == 0, 2, jnp.where(grp == 1, 4, jnp.where(grp == 2, 8, 16)))
    cnt = jnp.minimum(row + 1, win).astype(F32)
    return row, grp, cnt


def _by_group(grp, v0, v1, v2, v3):
    return jnp.where(grp == 0, v0, jnp.where(grp == 1, v1, jnp.where(grp == 2, v2, v3)))


def _pool_fwd(proj, bd, scale, name):
    S = proj.shape[0]

    def body(a_ref, bd_ref, sc_ref, d_ref, y_ref):
        a = a_ref[...]
        row, grp, cnt = _pool_masks(S)

        def back(v, k):
            return jnp.where(row >= k, pltpu.roll(v, k, 0), 0.0)

        s1 = a + back(a, 1)
        s2 = s1 + back(s1, 2)
        s3 = s2 + back(s2, 4)
        s4 = s3 + back(s3, 8)
        d = (_by_group(grp, s1, s2, s3, s4) / cnt - a).astype(BF16)
        d_ref[...] = d
        y_ref[...] = (jnp.dot(d, bd_ref[...], preferred_element_type=F32) * sc_ref[...]).astype(BF16)

    full = lambda r, c: pl.BlockSpec((r, c), lambda i: (0, 0))
    return pl.pallas_call(
        body, out_shape=(jax.ShapeDtypeStruct((S, POOL_W), BF16), jax.ShapeDtypeStruct((S, POOL_W), BF16)), grid=(1,),
        in_specs=[pl.BlockSpec((S, POOL_W), lambda i: (0, P_A // POOL_W)), full(POOL_W, POOL_W), full(1, POOL_W)],
        out_specs=(full(S, POOL_W), full(S, POOL_W)), compiler_params=_cp(("arbitrary",)), name=name)(proj, bd, scale)


def _pool_bwd(dya, d, bd, scale, name):
    S = dya.shape[0]

    def body(dy_ref, d_ref, bd_ref, sc_ref, da_ref, dbd_ref, dsc_ref):
        dy = dy_ref[...]
        dv = d_ref[...]
        bdv = bd_ref[...]
        row, grp, cnt = _pool_masks(S)
        yraw = jnp.dot(dv, bdv, preferred_element_type=F32)
        dsc_ref[...] = jnp.sum(dy * yraw, axis=0, keepdims=True)
        tb = (dy * sc_ref[...]).astype(BF16)
        dbd_ref[...] = lax.dot_general(dv, tb, (((0,), (0,)), ((), ())), preferred_element_type=F32)
        dd = lax.dot_general(tb, bdv, (((1,), (1,)), ((), ())), preferred_element_type=F32)
        e = dd / cnt

        def fwd(v, k):
            return jnp.where(row < S - k, pltpu.roll(v, S - k, 0), 0.0)

        r1 = e + fwd(e, 1)
        r2 = r1 + fwd(r1, 2)
        r3 = r2 + fwd(r2, 4)
        r4 = r3 + fwd(r3, 8)
        da_ref[...] = (_by_group(grp, r1, r2, r3, r4) - dd).astype(BF16)

    full = lambda r, c: pl.BlockSpec((r, c), lambda i: (0, 0))
    return pl.pallas_call(
        body, out_shape=(jax.ShapeDtypeStruct((S, POOL_W), BF16), jax.ShapeDtypeStruct((POOL_W, POOL_W), F32),
                         jax.ShapeDtypeStruct((1, POOL_W), F32)), grid=(1,),
        in_specs=[full(S, POOL_W), full(S, POOL_W), full(POOL_W, POOL_W), full(1, POOL_W)],
        out_specs=(full(S, POOL_W), full(POOL_W, POOL_W), full(1, POOL_W)),
        compiler_params=_cp(("arbitrary",)), name=name)(dya, d, bd, scale)


FCOLS = 128


def _log_sigmoid(z):
    return -(jnp.maximum(-z, 0.0) + jnp.log1p(jnp.exp(-jnp.abs(z))))


def _fgate_fwd(proj, bf, name):
    S = proj.shape[0]

    def body(f_ref, b_ref, o_ref):
        v = _log_sigmoid(f_ref[...] + b_ref[...])
        row = lax.broadcasted_iota(jnp.int32, (S, FCOLS), 0)
        k = 1
        while k < S:
            v = v + jnp.where(row >= k, pltpu.roll(v, k, 0), 0.0)
            k *= 2
        o_ref[...] = v

    return pl.pallas_call(
        body, out_shape=jax.ShapeDtypeStruct((S, FCOLS), F32), grid=(1,),
        in_specs=[pl.BlockSpec((S, FCOLS), lambda i: (0, P_F // FCOLS)), pl.BlockSpec((1, FCOLS), lambda i: (0, 0))],
        out_specs=pl.BlockSpec((S, FCOLS), lambda i: (0, 0)), compiler_params=_cp(("arbitrary",)), name=name)(proj, bf)


def _fgate_bwd(dF, proj, bf, name):
    S = proj.shape[0]

    def body(dF_ref, f_ref, b_ref, df_ref, db_ref):
        v = dF_ref[...]
        row = lax.broadcasted_iota(jnp.int32, (S, FCOLS), 0)
        k = 1
        while k < S:
            v = v + jnp.where(row < S - k, pltpu.roll(v, S - k, 0), 0.0)
            k *= 2
        z = f_ref[...] + b_ref[...]
        df = v * (1.0 / (1.0 + jnp.exp(z)))
        db_ref[...] = jnp.sum(df, axis=0, keepdims=True)
        df_ref[...] = jnp.concatenate([df, jnp.zeros_like(df)], axis=1).astype(BF16)

    return pl.pallas_call(
        body, out_shape=(jax.ShapeDtypeStruct((S, 2 * FCOLS), BF16), jax.ShapeDtypeStruct((1, FCOLS), F32)), grid=(1,),
        in_specs=[pl.BlockSpec((S, FCOLS), lambda i: (0, 0)), pl.BlockSpec((S, FCOLS), lambda i: (0, P_F // FCOLS)),
                  pl.BlockSpec((1, FCOLS), lambda i: (0, 0))],
        out_specs=(pl.BlockSpec((S, 2 * FCOLS), lambda i: (0, 0)), pl.BlockSpec((1, FCOLS), lambda i: (0, 0))),
        compiler_params=_cp(("arbitrary",)), name=name)(dF, proj, bf)


def _fox_scores(qe, kj, fq, fk, r0, c0, tq, tk, diagonal):
    s = lax.dot_general(qe, kj, (((1,), (1,)), ((), ())), preferred_element_type=F32) * FOX_SCALE
    s = s + (fq - fk)
    if not diagonal:
        return s
    rows = r0 + lax.broadcasted_iota(jnp.int32, (tq, tk), 0)
    cols = c0 + lax.broadcasted_iota(jnp.int32, (tq, tk), 1)
    return jnp.where(rows >= cols, s, NEG)


def _fox_fwd(qkv, fcol, frow, name, tq=256):
    S = qkv.shape[0]
    tk = tq

    def body(q_ref, k_ref, v_ref, fc_ref, fr_ref, o_ref, o32_ref, lse_ref):
        i = pl.program_id(1)
        r0 = i * tq
        q = q_ref[...]
        half = lax.broadcasted_iota(jnp.int32, (tq, 128), 1) // 64
        qs = [jnp.where(half == e, q, jnp.zeros_like(q)) for e in (0, 1)]
        fqs = [fc_ref[0, :, e:e + 1] for e in (0, 1)]

        def step(j, carry, diagonal=False):
            c0 = pl.multiple_of(j * tk, tk)
            kj = k_ref[pl.ds(c0, tk), :]
            vj = v_ref[pl.ds(c0, tk), :]
            out = []
            for e in (0, 1):
                m, l, acc = carry[e]
                s = _fox_scores(qs[e], kj, fqs[e], fr_ref[0, e:e + 1, pl.ds(c0, tk)], r0, c0, tq, tk, diagonal)
                m_new = jnp.maximum(m, jnp.max(s, axis=-1, keepdims=True))
                alpha = jnp.exp(m - m_new)
                p = jnp.exp(s - m_new)
                out.append((m_new, alpha * l + jnp.sum(p, axis=-1, keepdims=True),
                            alpha * acc + jnp.dot(p.astype(BF16), vj, preferred_element_type=F32)))
            return tuple(out)

        init = (jnp.full((tq, 1), NEG, F32), jnp.zeros((tq, 1), F32), jnp.zeros((tq, 128), F32))
        carry = lax.fori_loop(0, i, step, (init, init))
        carry = step(i, carry, diagonal=True)
        outs = []
        for e in (0, 1):
            m, l, acc = carry[e]
            outs.append(acc / l)
            lse_ref[0, :, e:e + 1] = m + jnp.log(l)
        o = jnp.where(half == 0, outs[0], outs[1])
        o32_ref[...] = o
        o_ref[...] = o.astype(BF16)

    tile = pl.BlockSpec((tq, 128), lambda h, i: (i, h))
    return pl.pallas_call(
        body, out_shape=(jax.ShapeDtypeStruct((S, FOX_W), BF16), jax.ShapeDtypeStruct((S, FOX_W), F32),
                         jax.ShapeDtypeStruct((4, S, 2), F32)), grid=(4, S // tq),
        in_specs=[tile, pl.BlockSpec((S, 128), lambda h, i: (0, 4 + h)), pl.BlockSpec((S, 128), lambda h, i: (0, 8 + h)),
                  pl.BlockSpec((1, tq, 2), lambda h, i: (h, i, 0)), pl.BlockSpec((1, 2, S), lambda h, i: (h, 0, 0))],
        out_specs=(tile, tile, pl.BlockSpec((1, tq, 2), lambda h, i: (h, i, 0))),
        compiler_params=_cp(("parallel", "parallel")), name=name)(qkv, qkv, qkv, fcol, frow)


def _fox_bwd(qkv, o32, do, lse, fcol, frow, name, tq=256):
    S = qkv.shape[0]
    tk = tq
    nq = S // tq

    def body(q_ref, k_ref, v_ref, o_ref, do_ref, lse_ref, fc_ref, fr_ref, dq_ref, dk_ref, dv_ref, dfr_ref, dfc_ref, dk_acc, dv_acc):
        dk_acc[...] = jnp.zeros_like(dk_acc)
        dv_acc[...] = jnp.zeros_like(dv_acc)
        dfr_ref[...] = jnp.zeros_like(dfr_ref)
        half = lax.broadcasted_iota(jnp.int32, (tq, 128), 1) // 64

        def q_block(i, _):
            r0 = pl.multiple_of(i * tq, tq)
            qi = q_ref[pl.ds(r0, tq), :]
            dob = do_ref[pl.ds(r0, tq), :].astype(BF16)
            row_dot = dob.astype(F32) * o_ref[pl.ds(r0, tq), :]
            qs = [jnp.where(half == e, qi, jnp.zeros_like(qi)) for e in (0, 1)]
            dos = [jnp.where(half == e, dob, jnp.zeros_like(dob)) for e in (0, 1)]
            deltas = [jnp.sum(jnp.where(half == e, row_dot, 0.0), axis=-1, keepdims=True) for e in (0, 1)]
            lses = [lse_ref[0, pl.ds(r0, tq), e:e + 1] for e in (0, 1)]
            fqs = [fc_ref[0, pl.ds(r0, tq), e:e + 1] for e in (0, 1)]

            def step(j, carry, diagonal=False):
                dqs, row_sums = carry
                c0 = pl.multiple_of(j * tk, tk)
                kj = k_ref[pl.ds(c0, tk), :]
                vj = v_ref[pl.ds(c0, tk), :]
                new_dq, new_rows, dkc, dvc = [], [], [], []
                for e in (0, 1):
                    s = _fox_scores(qs[e], kj, fqs[e], fr_ref[0, e:e + 1, pl.ds(c0, tk)], r0, c0, tq, tk, diagonal)
                    p = jnp.exp(s - lses[e])
                    dp = lax.dot_general(dos[e], vj, (((1,), (1,)), ((), ())), preferred_element_type=F32)
                    ds = p * (dp - deltas[e])
                    dfr_ref[0, e:e + 1, pl.ds(c0, tk)] -= jnp.sum(ds, axis=0, keepdims=True)
                    new_rows.append(row_sums[e] + jnp.sum(ds, axis=-1, keepdims=True))
                    dsb = (ds * FOX_SCALE).astype(BF16)
                    dkc.append(lax.dot_general(dsb, qi, (((0,), (0,)), ((), ())), preferred_element_type=F32))
                    dvc.append(lax.dot_general(p.astype(BF16), dob, (((0,), (0,)), ((), ())), preferred_element_type=F32))
                    new_dq.append(dqs[e] + jnp.dot(dsb, kj, preferred_element_type=F32))
                dk_acc[pl.ds(c0, tk), :] += jnp.where(half == 0, dkc[0], dkc[1])
                dv_acc[pl.ds(c0, tk), :] += jnp.where(half == 0, dvc[0], dvc[1])
                return tuple(new_dq), tuple(new_rows)

            zero, zero_col = jnp.zeros((tq, 128), F32), jnp.zeros((tq, 1), F32)
            carry = lax.fori_loop(0, i, step, ((zero, zero), (zero_col, zero_col)))
            dqs, row_sums = step(i, carry, diagonal=True)
            for e in (0, 1):
                dfc_ref[0, pl.ds(r0, tq), e:e + 1] = row_sums[e]
            dq_ref[pl.ds(r0, tq), :] = jnp.where(half == 0, dqs[0], dqs[1]).astype(BF16)
            return 0

        lax.fori_loop(0, nq, q_block, 0)
        dk_ref[...] = dk_acc[...].astype(BF16)
        dv_ref[...] = dv_acc[...].astype(BF16)

    col = lambda off: pl.BlockSpec((S, 128), lambda h: (0, off + h))
    hs2 = pl.BlockSpec((1, S, 2), lambda h: (h, 0, 0))
    h2s = pl.BlockSpec((1, 2, S), lambda h: (h, 0, 0))
    return pl.pallas_call(
        body, out_shape=(jax.ShapeDtypeStruct((S, FOX_W), BF16),) * 3 + (jax.ShapeDtypeStruct((4, 2, S), F32),
                                                                         jax.ShapeDtypeStruct((4, S, 2), F32)), grid=(4,),
        in_specs=[col(0), col(4), col(8), col(0), col(0), hs2, hs2, h2s],
        out_specs=(col(0), col(0), col(0), h2s, hs2),
        scratch_shapes=[pltpu.VMEM((S, 128), F32), pltpu.VMEM((S, 128), F32)],
        compiler_params=_cp(("parallel",)), name=name)(qkv, qkv, qkv, o32, do, lse, fcol, frow)


def _gelu(x):
    return 0.5 * x * (1.0 + jnp.tanh(GELU_K * (x + GELU_C * x * x * x)))


def _gelu_grad(x):
    th = jnp.tanh(GELU_K * (x + GELU_C * x * x * x))
    return 0.5 * (1.0 + th) + 0.5 * x * (1.0 - th * th) * GELU_K * (1.0 + 3.0 * GELU_C * x * x)


def _sgu_parts(c, gn, w_ref, bias):
    zc = _gelu(c)
    u, vv = zc[:, :SGU_W], zc[:, SGU_W:]
    rstd = lax.rsqrt(jnp.mean(vv * vv, axis=-1, keepdims=True) + EPS)
    vhat = vv * rstd
    vnb = (vhat * gn).astype(BF16)
    grp = lax.broadcasted_iota(jnp.int32, (SGU_CHUNK, SGU_W), 1) // 64
    mixed = bias
    for gi in range(4):
        mixed = mixed + jnp.where(grp == gi, jnp.dot(w_ref[gi], vnb, preferred_element_type=F32), 0.0)
    return u, rstd, vhat, vnb, grp, mixed


def _sgu_fwd(proj, gn, wm, bias, name):
    S = proj.shape[0]

    def body(c_ref, g_ref, w_ref, b_ref, o_ref):
        u, _, _, _, _, mixed = _sgu_parts(c_ref[...], g_ref[...], w_ref, b_ref[...])
        o_ref[...] = (u * mixed).astype(BF16)

    return pl.pallas_call(
        body, out_shape=jax.ShapeDtypeStruct((S, SGU_W), BF16), grid=(S // SGU_CHUNK,),
        in_specs=[pl.BlockSpec((SGU_CHUNK, 2 * SGU_W), lambda i: (i, P_C // (2 * SGU_W))),
                  pl.BlockSpec((1, SGU_W), lambda i: (0, 0)), pl.BlockSpec((4, SGU_CHUNK, SGU_CHUNK), lambda i: (0, 0, 0)),
                  pl.BlockSpec((SGU_CHUNK, SGU_W), lambda i: (0, 0))],
        out_specs=pl.BlockSpec((SGU_CHUNK, SGU_W), lambda i: (i, 0)),
        compiler_params=_cp(("parallel",)), name=name)(proj, gn, wm, bias)


def _sgu_bwd(dsg, proj, gn, wm, wmt, bias, name):
    S = proj.shape[0]

    def body(dsg_ref, c_ref, g_ref, w_ref, wt_ref, b_ref, dc_ref, dw_ref, db_ref, dg_ref):
        i = pl.program_id(0)

        @pl.when(i == 0)
        def _():
            dw_ref[...] = jnp.zeros_like(dw_ref)
            db_ref[...] = jnp.zeros_like(db_ref)
            dg_ref[...] = jnp.zeros_like(dg_ref)

        c = c_ref[...]
        gn_v = g_ref[...]
        u, rstd, vhat, vnb, grp, mixed = _sgu_parts(c, gn_v, w_ref, b_ref[...])
        dsg_v = dsg_ref[...]
        du = dsg_v * mixed
        dmix = dsg_v * u
        db_ref[...] += dmix
        dmb = dmix.astype(BF16)
        dvn = jnp.zeros((SGU_CHUNK, SGU_W), F32)
        for gi in range(4):
            dmg = jnp.where(grp == gi, dmb, jnp.zeros_like(dmb))
            dw_ref[gi] += lax.dot_general(dmg, vnb, (((1,), (1,)), ((), ())), preferred_element_type=F32)
            dvn = dvn + jnp.where(grp == gi, jnp.dot(wt_ref[gi], dmb, preferred_element_type=F32), 0.0)
        dg_ref[...] += jnp.sum(dvn * vhat, axis=0, keepdims=True)
        t = dvn * gn_v
        dvv = rstd * (t - vhat * jnp.mean(t * vhat, axis=-1, keepdims=True))
        dc_ref[...] = (jnp.concatenate([du, dvv], axis=1) * _gelu_grad(c)).astype(BF16)

    w_spec = pl.BlockSpec((4, SGU_CHUNK, SGU_CHUNK), lambda i: (0, 0, 0))
    tile = pl.BlockSpec((SGU_CHUNK, SGU_W), lambda i: (0, 0))
    vec = pl.BlockSpec((1, SGU_W), lambda i: (0, 0))
    return pl.pallas_call(
        body, out_shape=(jax.ShapeDtypeStruct((S, 2 * SGU_W), BF16), jax.ShapeDtypeStruct((4, SGU_CHUNK, SGU_CHUNK), F32),
                         jax.ShapeDtypeStruct((SGU_CHUNK, SGU_W), F32), jax.ShapeDtypeStruct((1, SGU_W), F32)),
        grid=(S // SGU_CHUNK,),
        in_specs=[pl.BlockSpec((SGU_CHUNK, SGU_W), lambda i: (i, 0)),
                  pl.BlockSpec((SGU_CHUNK, 2 * SGU_W), lambda i: (i, P_C // (2 * SGU_W))), vec, w_spec, w_spec, tile],
        out_specs=(pl.BlockSpec((SGU_CHUNK, 2 * SGU_W), lambda i: (i, 0)), w_spec, tile, vec),
        compiler_params=_cp(("arbitrary",)), name=name)(dsg, proj, gn, wm, wmt, bias)


def _sigmoid(z):
    return 1.0 / (1.0 + jnp.exp(-z))


def _merge_specs(tm):
    row = lambda n: pl.BlockSpec((tm, n), lambda i: (i, 0))
    gate = lambda b: pl.BlockSpec((tm, D), lambda i: (i, b))
    full = lambda r, c: pl.BlockSpec((r, c), lambda i: (0, 0))
    packed = pl.BlockSpec((4, 256, PACK_COLS), lambda i: (0, R_BRANCH // 256, 0))
    return row, gate, full, packed


def _branch_shards(c_ref, j):
    return c_ref[j, :, 0:256], c_ref[j, :, 256:512], c_ref[j, :, 512:768], c_ref[j, :, 768:1024]


def _merge_fwd(proj, ya, o, sg, packed_w, bg, name, tm=256):
    S = proj.shape[0]
    row, gate, full, packed = _merge_specs(tm)

    def body(g0, g1, g2, ya_ref, o_ref, sg_ref, c_ref, bg_ref, out_ref):
        yav, ov, sgv = ya_ref[...], o_ref[...], sg_ref[...]
        for j in range(4):
            cols = slice(256 * j, 256 * (j + 1))
            wa, wb0, wb1, wc = _branch_shards(c_ref, j)
            y = (jnp.dot(yav, wa, preferred_element_type=F32),
                 jnp.dot(ov[:, :256], wb0, preferred_element_type=F32) + jnp.dot(ov[:, 256:], wb1, preferred_element_type=F32),
                 jnp.dot(sgv, wc, preferred_element_type=F32))
            acc = jnp.zeros((tm, 256), F32)
            for b, g_ref in enumerate((g0, g1, g2)):
                acc = acc + _sigmoid(g_ref[:, cols] + bg_ref[:, b * D + 256 * j:b * D + 256 * (j + 1)]) * y[b]
            out_ref[:, cols] = acc.astype(BF16)

    return pl.pallas_call(
        body, out_shape=jax.ShapeDtypeStruct((S, D), BF16), grid=(S // tm,),
        in_specs=[gate(0), gate(1), gate(2), row(POOL_W), row(FOX_W), row(SGU_W), packed, full(1, 3 * D)],
        out_specs=row(D), compiler_params=_cp(("parallel",)), name=name)(proj, proj, proj, ya, o, sg, packed_w, bg)


def _merge_bwd(dm, proj, ya, o, sg, packed_w, bg, grads, name, tm=256):
    S = proj.shape[0]
    row, gate, full, packed = _merge_specs(tm)
    tn_dims = (((0,), (0,)), ((), ()))
    nt_dims = (((1,), (1,)), ((), ()))

    def body(dm_ref, g0, g1, g2, ya_ref, o_ref, sg_ref, c_ref, bg_ref, _, dg_ref, dya_ref, do_ref, dsg_ref, dc_ref, dbg_ref, acc):
        i = pl.program_id(0)

        @pl.when(i == 0)
        def _():
            acc[...] = jnp.zeros_like(acc)
            dbg_ref[...] = jnp.zeros_like(dbg_ref)

        yav, ov, sgv = ya_ref[...], o_ref[...], sg_ref[...]
        o0, o1 = ov[:, :256], ov[:, 256:]
        dya = jnp.zeros((tm, POOL_W), F32)
        do0 = jnp.zeros((tm, 256), F32)
        do1 = jnp.zeros((tm, 256), F32)
        dsg = jnp.zeros((tm, SGU_W), F32)
        for j in range(4):
            cols = slice(256 * j, 256 * (j + 1))
            wa, wb0, wb1, wc = _branch_shards(c_ref, j)
            y = (jnp.dot(yav, wa, preferred_element_type=F32),
                 jnp.dot(o0, wb0, preferred_element_type=F32) + jnp.dot(o1, wb1, preferred_element_type=F32),
                 jnp.dot(sgv, wc, preferred_element_type=F32))
            dmv = dm_ref[:, cols]
            dy = []
            for b, g_ref in enumerate((g0, g1, g2)):
                bcols = slice(b * D + 256 * j, b * D + 256 * (j + 1))
                gt = _sigmoid(g_ref[:, cols] + bg_ref[:, bcols])
                dgp = dmv * y[b] * gt * (1.0 - gt)
                dg_ref[:, bcols] = dgp.astype(BF16)
                dbg_ref[:, bcols] += jnp.sum(dgp, axis=0, keepdims=True)
                dy.append((dmv * gt).astype(BF16))
            dya = dya + lax.dot_general(dy[0], wa, nt_dims, preferred_element_type=F32)
            do0 = do0 + lax.dot_general(dy[1], wb0, nt_dims, preferred_element_type=F32)
            do1 = do1 + lax.dot_general(dy[1], wb1, nt_dims, preferred_element_type=F32)
            dsg = dsg + lax.dot_general(dy[2], wc, nt_dims, preferred_element_type=F32)
            acc[j, :, 0:256] += lax.dot_general(yav, dy[0], tn_dims, preferred_element_type=F32)
            acc[j, :, 256:512] += lax.dot_general(o0, dy[1], tn_dims, preferred_element_type=F32)
            acc[j, :, 512:768] += lax.dot_general(o1, dy[1], tn_dims, preferred_element_type=F32)
            acc[j, :, 768:1024] += lax.dot_general(sgv, dy[2], tn_dims, preferred_element_type=F32)
        dya_ref[...] = dya
        do_ref[:, :256] = do0
        do_ref[:, 256:] = do1
        dsg_ref[...] = dsg

        @pl.when(i == pl.num_programs(0) - 1)
        def _():
            dc_ref[...] = acc[...].astype(dc_ref.dtype)

    return pl.pallas_call(
        body, out_shape=(jax.ShapeDtypeStruct((S, 3 * D), BF16), jax.ShapeDtypeStruct((S, POOL_W), F32),
                         jax.ShapeDtypeStruct((S, FOX_W), F32), jax.ShapeDtypeStruct((S, SGU_W), F32),
                         jax.ShapeDtypeStruct(grads.shape, grads.dtype), jax.ShapeDtypeStruct((1, 3 * D), F32)),
        grid=(S // tm,),
        in_specs=[row(D), gate(0), gate(1), gate(2), row(POOL_W), row(FOX_W), row(SGU_W), packed, full(1, 3 * D), ANY],
        out_specs=(row(3 * D), row(POOL_W), row(FOX_W), row(SGU_W), packed, full(1, 3 * D)),
        scratch_shapes=[pltpu.VMEM((4, 256, PACK_COLS), F32)], input_output_aliases={9: 4},
        compiler_params=_cp(("arbitrary",)), name=name)(dm, proj, proj, proj, ya, o, sg, packed_w, bg, grads)


def _xattn_probs(qh, kh):
    s = lax.dot_general(qh, kh, (((1,), (1,)), ((), ())), preferred_element_type=F32) * X_SCALE
    p = jnp.exp(s - jnp.max(s, axis=-1, keepdims=True))
    return p / jnp.sum(p, axis=-1, keepdims=True)


def _xattn_fwd(xq, kv, name, tq=256):
    S = xq.shape[0]
    M = kv.shape[0]

    def body(q_ref, k_ref, v_ref, o_ref):
        for h in range(XH):
            sl = slice(h * XHD, (h + 1) * XHD)
            p = _xattn_probs(q_ref[:, sl], k_ref[:, sl])
            o_ref[:, sl] = jnp.dot(p.astype(BF16), v_ref[:, sl], preferred_element_type=F32).astype(BF16)

    return pl.pallas_call(
        body, out_shape=jax.ShapeDtypeStruct((S, D), BF16), grid=(S // tq,),
        in_specs=[pl.BlockSpec((tq, D), lambda i: (i, 0)), pl.BlockSpec((M, D), lambda i: (0, 0)),
                  pl.BlockSpec((M, D), lambda i: (0, 1))],
        out_specs=pl.BlockSpec((tq, D), lambda i: (i, 0)), compiler_params=_cp(("parallel",)), name=name)(xq, kv, kv)


def _xattn_bwd(xq, kv, do, name, tq=256):
    S = xq.shape[0]
    M = kv.shape[0]

    def body(q_ref, k_ref, v_ref, do_ref, dq_ref, dkv_ref, dk_acc, dv_acc):
        i = pl.program_id(0)

        @pl.when(i == 0)
        def _():
            dk_acc[...] = jnp.zeros_like(dk_acc)
            dv_acc[...] = jnp.zeros_like(dv_acc)

        for h in range(XH):
            sl = slice(h * XHD, (h + 1) * XHD)
            qh, kh, vh, doh = q_ref[:, sl], k_ref[:, sl], v_ref[:, sl], do_ref[:, sl]
            p = _xattn_probs(qh, kh)
            dp = lax.dot_general(doh, vh, (((1,), (1,)), ((), ())), preferred_element_type=F32)
            ds = p * (dp - jnp.sum(p * dp, axis=-1, keepdims=True))
            dsb = (ds * X_SCALE).astype(BF16)
            dq_ref[:, sl] = jnp.dot(dsb, kh, preferred_element_type=F32).astype(BF16)
            dk_acc[:, sl] += lax.dot_general(dsb, qh, (((0,), (0,)), ((), ())), preferred_element_type=F32)
            dv_acc[:, sl] += lax.dot_general(p.astype(BF16), doh, (((0,), (0,)), ((), ())), preferred_element_type=F32)

        @pl.when(i == pl.num_programs(0) - 1)
        def _():
            dkv_ref[:, :D] = dk_acc[...].astype(BF16)
            dkv_ref[:, D:] = dv_acc[...].astype(BF16)

    return pl.pallas_call(
        body, out_shape=(jax.ShapeDtypeStruct((S, D), BF16), jax.ShapeDtypeStruct((M, 2 * D), BF16)), grid=(S // tq,),
        in_specs=[pl.BlockSpec((tq, D), lambda i: (i, 0)), pl.BlockSpec((M, D), lambda i: (0, 0)),
                  pl.BlockSpec((M, D), lambda i: (0, 1)), pl.BlockSpec((tq, D), lambda i: (i, 0))],
        out_specs=(pl.BlockSpec((tq, D), lambda i: (i, 0)), pl.BlockSpec((M, 2 * D), lambda i: (0, 0))),
        scratch_shapes=[pltpu.VMEM((M, D), F32), pltpu.VMEM((M, D), F32)],
        compiler_params=_cp(("arbitrary",)), name=name)(xq, kv, kv, do)


def _adam_math(gv, wv, mv, vv):
    c1 = 1.0 - ADAM_B1 ** ADAM_STEP
    c2 = 1.0 - ADAM_B2 ** ADAM_STEP
    nm = ADAM_B1 * mv + (1.0 - ADAM_B1) * gv
    nv = ADAM_B2 * vv + (1.0 - ADAM_B2) * (gv * gv)
    return -ADAM_LR * ((nm / c1) / (jnp.sqrt(nv / c2) + ADAM_EPS) + ADAM_WD * wv), nm, nv


def _adamw(g, w, m, v, name, block=None):
    if block is None:
        block = (1, 256 if g.shape[1] % 256 == 0 else g.shape[1], g.shape[2])
    grid = tuple(s // b for s, b in zip(g.shape, block))

    def body(g_ref, w_ref, m_ref, v_ref, d_ref, nm_ref, nv_ref):
        d_ref[...], nm_ref[...], nv_ref[...] = _adam_math(g_ref[...], w_ref[...], m_ref[...], v_ref[...])

    blk = pl.BlockSpec(block, lambda a, b, c: (a, b, c))
    return pl.pallas_call(
        body, out_shape=(jax.ShapeDtypeStruct(g.shape, F32),) * 3, grid=grid,
        in_specs=[blk] * 4, out_specs=(blk,) * 3, compiler_params=_cp(("parallel",) * 3), name=name)(g, w, m, v)


def _adamw_packed(red, w, m, v, g_index, name, token, tr=256):
    L, r, c = w.shape
    tr = min(tr, r)

    def body(g0_ref, g1_ref, w_ref, m_ref, v_ref, _, g_ref, d_ref, nm_ref, nv_ref):
        gv = jnp.where(pl.program_id(0) == 0, g0_ref[...], g1_ref[...])
        g_ref[0] = gv
        d_ref[0], nm_ref[0], nv_ref[0] = _adam_math(gv, w_ref[0], m_ref[0], v_ref[0])

    gblk = pl.BlockSpec((tr, c), lambda l, i: g_index(i))
    blk = pl.BlockSpec((1, tr, c), lambda l, i: (l, i, 0))
    return pl.pallas_call(
        body, out_shape=(jax.ShapeDtypeStruct(w.shape, F32),) * 4, grid=(L, r // tr),
        in_specs=[gblk, gblk, blk, blk, blk, pl.BlockSpec((8, 128), lambda l, i: (0, 0))], out_specs=(blk,) * 4,
        compiler_params=_cp(("parallel", "parallel")), name=name)(red[0], red[1], w, m, v, token)


def _row_tile(R):
    return next((t for t in (512, 496, 384, 256) if R % t == 0), R)


def _sum_slots(a, out_dtype, name):
    n, R, C = a.shape
    tr = _row_tile(R)

    def body(a_ref, o_ref):
        acc = a_ref[0].astype(F32)
        for k in range(1, n):
            acc = acc + a_ref[k].astype(F32)
        o_ref[...] = acc.astype(out_dtype)

    return pl.pallas_call(
        body, out_shape=jax.ShapeDtypeStruct((R, C), out_dtype), grid=(R // tr,),
        in_specs=[pl.BlockSpec((n, tr, C), lambda i: (0, i, 0))], out_specs=pl.BlockSpec((tr, C), lambda i: (i, 0)),
        compiler_params=_cp(("parallel",)), name=name)(a)


def _add_pair(a, b, name):
    n, R, C = a.shape
    tr = _row_tile(R)

    def body(a_ref, b_ref, o_ref):
        o_ref[...] = (a_ref[...].astype(F32) + b_ref[...].astype(F32)).astype(BF16)

    blk = pl.BlockSpec((1, tr, C), lambda k, i: (k, i, 0))
    return pl.pallas_call(
        body, out_shape=jax.ShapeDtypeStruct(a.shape, BF16), grid=(n, R // tr), in_specs=[blk, blk], out_specs=blk,
        compiler_params=_cp(("parallel", "parallel")), name=name)(a, b)


LANDING = pl.BlockSpec(memory_space=pltpu.VMEM)


def _landing_params(shape, dtype):
    return pltpu.CompilerParams(vmem_limit_bytes=math.prod(shape) * jnp.dtype(dtype).itemsize + 4 * 1024 * 1024)


def _place():
    return lax.axis_index("x"), lax.axis_index("y"), lax.axis_index("c")


def _other_chips(x, y):
    return [(1 - x, y), (x, 1 - y), (1 - x, 1 - y)]


def _row_chunks(rows, want, align=16):
    n = want
    while n > 1 and rows % (n * align):
        n -= 1
    return n


def _pair_split(g, name, nch=5):
    n, R, C = g.shape
    half = R // 2
    nch = _row_chunks(half, nch)
    cr = half // nch

    def body(g_ref, own_ref, got_ref, send_sems, recv_sems, local_sem):
        x, y, c = _place()
        mine0 = pl.multiple_of(c * half, 16)
        theirs0 = (1 - c) * half
        keep = pltpu.make_async_copy(g_ref.at[:, pl.ds(mine0, half), :], own_ref, local_sem)
        keep.start()
        cps = []
        for s in range(n):
            for q in range(nch):
                src = g_ref.at[s, pl.ds(pl.multiple_of(theirs0 + q * cr, 16), cr), :]
                cps.append(pltpu.make_async_remote_copy(
                    src_ref=src, dst_ref=got_ref.at[s, pl.ds(q * cr, cr), :], send_sem=send_sems.at[s * nch + q],
                    recv_sem=recv_sems.at[s * nch + q], device_id=(x, y, 1 - c), device_id_type=MESH))
        for cp in cps:
            cp.start()
        for cp in cps:
            cp.wait()
        keep.wait()

    sh = jax.ShapeDtypeStruct((n, half, C), g.dtype)
    return pl.pallas_call(
        body, out_shape=(sh, sh), in_specs=[ANY], out_specs=(ANY, LANDING),
        scratch_shapes=[pltpu.SemaphoreType.DMA((n * nch,)), pltpu.SemaphoreType.DMA((n * nch,)), pltpu.SemaphoreType.DMA],
        compiler_params=_landing_params(sh.shape, g.dtype), name=name)(g)


def _pair_gather(t, name, nch=10):
    R = t.shape[0]
    nch = _row_chunks(R, nch, 8)
    cr = R // nch

    def body(t_ref, o_ref, send_sems, recv_sems, local_sem):
        x, y, c = _place()
        own = pltpu.make_async_copy(t_ref, o_ref.at[c], local_sem)
        own.start()
        cps = [pltpu.make_async_remote_copy(src_ref=t_ref.at[pl.ds(q * cr, cr), :], dst_ref=o_ref.at[c, pl.ds(q * cr, cr), :],
                                            send_sem=send_sems.at[q], recv_sem=recv_sems.at[q], device_id=(x, y, 1 - c),
                                            device_id_type=MESH) for q in range(nch)]
        for cp in cps:
            cp.start()
        for cp in cps:
            cp.wait()
        own.wait()

    return pl.pallas_call(
        body, out_shape=jax.ShapeDtypeStruct((2,) + t.shape, t.dtype), in_specs=[ANY], out_specs=LANDING,
        scratch_shapes=[pltpu.SemaphoreType.DMA((nch,)), pltpu.SemaphoreType.DMA((nch,)), pltpu.SemaphoreType.DMA],
        compiler_params=_landing_params((2,) + t.shape, t.dtype), name=name)(t)


HBM = pl.BlockSpec(memory_space=pltpu.HBM)
SEM = pl.BlockSpec(memory_space=pltpu.SEMAPHORE)
SPLIT_COPY = pltpu.CompilerParams(has_side_effects=pltpu.SideEffectType.DATAFLOW_SIDE_EFFECTING)


def _split_exchange(src, rows, src_of, tag, nch=5):
    C = src.shape[-1]
    nch = _row_chunks(rows, nch)
    cr = rows // nch
    n = 3 * nch
    land_shape = (4, rows, C)

    def copies(src_ref, land_ref, send_sems, recv_sems):
        x, y, c = _place()
        j = 2 * x + y
        out = []
        for q in range(nch):
            for k, (px, py) in enumerate(_other_chips(x, y)):
                out.append(pltpu.make_async_remote_copy(
                    src_ref=src_of(src_ref, px, py, c, q * cr, cr), dst_ref=land_ref.at[j, pl.ds(q * cr, cr), :],
                    send_sem=send_sems.at[k * nch + q], recv_sem=recv_sems.at[k * nch + q], device_id=(px, py, c),
                    device_id_type=MESH))
        return out

    def start(src_ref, land_ref, send_sems, recv_sems, src_thru, land_thru, token):
        for cp in copies(src_ref, land_ref, send_sems, recv_sems):
            cp.start()
        token[...] = jnp.zeros_like(token)

    send_sems, recv_sems, src_thru, land_thru, token = pl.pallas_call(
        start, name=f"{tag}_start",
        out_shape=(pltpu.SemaphoreType.DMA((n,)), pltpu.SemaphoreType.DMA((n,)), pltpu.HBM(src.shape, src.dtype),
                   pltpu.HBM(land_shape, src.dtype), jax.ShapeDtypeStruct((8, 128), F32)),
        in_specs=(HBM, HBM), out_specs=(SEM, SEM, HBM, HBM, pl.BlockSpec(memory_space=pltpu.VMEM)),
        input_output_aliases={0: 2, 1: 3}, compiler_params=SPLIT_COPY)(
            pltpu.with_memory_space_constraint(src, pltpu.HBM),
            pltpu.with_memory_space_constraint(lax.empty(land_shape, src.dtype), pltpu.HBM))

    def finish(after):
        def wait(src_ref, land_ref, send_sems, recv_sems, after_ref, src_dead, got_ref):
            for cp in copies(src_ref, land_ref, send_sems, recv_sems):
                cp.wait_send()
                cp.wait_recv()

        return pl.pallas_call(
            wait, name=f"{tag}_wait", out_shape=(pltpu.HBM(src.shape, src.dtype), pltpu.HBM(land_shape, src.dtype)),
            in_specs=(HBM, HBM, SEM, SEM, ANY), out_specs=(HBM, HBM), input_output_aliases={0: 0, 1: 1},
            compiler_params=SPLIT_COPY)(src_thru, land_thru, send_sems, recv_sems, after)

    return token, finish


def _gather_finish(shard, land, name, nch=5):
    R, C = shard.shape
    half = R // 2
    nch = _row_chunks(half, nch)
    cr = half // nch

    def body(s_ref, l_ref, o_ref, send_sems, recv_sems, local_sems):
        x, y, c = _place()
        j = 2 * x + y
        mine0 = c * half
        local = [pltpu.make_async_copy(s_ref, o_ref.at[j], local_sems.at[0])]
        remote = []
        for k, (px, py) in enumerate(_other_chips(x, y)):
            jj = 2 * px + py
            local.append(pltpu.make_async_copy(l_ref.at[jj], o_ref.at[jj, pl.ds(pl.multiple_of(mine0, 16), half), :],
                                               local_sems.at[1 + k]))
            for q in range(nch):
                remote.append(pltpu.make_async_remote_copy(
                    src_ref=l_ref.at[jj, pl.ds(q * cr, cr), :],
                    dst_ref=o_ref.at[jj, pl.ds(pl.multiple_of(mine0 + q * cr, 16), cr), :], send_sem=send_sems.at[k * nch + q],
                    recv_sem=recv_sems.at[k * nch + q], device_id=(x, y, 1 - c), device_id_type=MESH))
        for cp in local + remote:
            cp.start()
        for cp in remote + local:
            cp.wait()

    return pl.pallas_call(
        body, out_shape=jax.ShapeDtypeStruct((4, R, C), shard.dtype), in_specs=[ANY, ANY], out_specs=LANDING,
        scratch_shapes=[pltpu.SemaphoreType.DMA((3 * nch,)), pltpu.SemaphoreType.DMA((3 * nch,)), pltpu.SemaphoreType.DMA((4,))],
        compiler_params=_landing_params((4, R, C), shard.dtype), name=name)(shard, land)


def _sum_slots_own(land, own, name):
    n, R, C = land.shape
    tr = _row_tile(R)
    me = (2 * lax.axis_index("x") + lax.axis_index("y")).astype(jnp.int32).reshape(1)
    if own.ndim == 3:
        own_spec = pl.BlockSpec((None, tr, C), lambda i, me: (me[0], i, 0))
    else:
        own_spec = pl.BlockSpec((tr, C), lambda i, me: (i, 0))

    def body(me_ref, land_ref, own_ref, o_ref):
        acc = None
        for k in range(n):
            v = jnp.where(me_ref[0] == k, own_ref[...], land_ref[k]).astype(F32)
            acc = v if acc is None else acc + v
        o_ref[...] = acc

    return pl.pallas_call(
        body, out_shape=jax.ShapeDtypeStruct((R, C), F32),
        grid_spec=pltpu.PrefetchScalarGridSpec(
            num_scalar_prefetch=1, grid=(R // tr,),
            in_specs=[pl.BlockSpec((n, tr, C), lambda i, me: (0, i, 0)), own_spec],
            out_specs=pl.BlockSpec((tr, C), lambda i, me: (i, 0))),
        compiler_params=_cp(("parallel",)), name=name)(me, land, own)


def _reduce_begin(g, tag):
    own, got = _pair_split(g, f"rs_pair_{tag}")
    p = _add_pair(own, got, f"rs_add_{tag}")
    token, finish = _split_exchange(p, p.shape[1], lambda ref, px, py, c, r0, cr: ref.at[2 * px + py, pl.ds(r0, cr), :],
                                    f"rs_a2a_{tag}")
    return (finish, g.shape, tag), token


def _reduce_end(state, after):
    finish, shape, tag = state
    p, land = finish(after)
    t = _sum_slots_own(land, p, f"rs_sum_{tag}")
    return _pair_gather(t, f"rs_join_{tag}").reshape(shape[1], shape[2])


def _all_reduce_begin(v, tag):
    p = _sum_slots(_pair_gather(v, f"ar_pair_{tag}"), F32, f"ar_add_{tag}")
    token, finish = _split_exchange(p, p.shape[0], lambda ref, px, py, c, r0, cr: ref.at[pl.ds(r0, cr), :], f"ar_a2a_{tag}")
    return (finish, tag), token


def _all_reduce_end(state, after):
    finish, tag = state
    p, land = finish(after)
    return _sum_slots_own(land, p, f"ar_sum_{tag}")


def _gather_begin(shard, tag):
    half = shard.shape[0] // 2
    token, finish = _split_exchange(
        shard, half, lambda ref, px, py, c, r0, cr: ref.at[pl.ds(pl.multiple_of(c * half + r0, 16), cr), :], f"gather_{tag}")
    return (finish, tag), token


def _gather_end(state, after):
    finish, tag = state
    shard, land = finish(after)
    return _gather_finish(shard, land, f"gather_{tag}_finish")


R_BRANCH, R_OUT, R_WIN, ROWS_A = 0, 256, 512, 1888
R_FF1, R_FF2, R_XKV, R_XQ, R_XO, ROWS_B = 0, 1024, 2048, 2560, 2816, 3072
WIN_ROWS = N_IN // 4


def _w_in_t(a):
    return jnp.transpose(a, (2, 0, 1))


def _pack_shard(w, l):
    xkv, wb = w['w_xkv'][l], w['w_branch_b'][l]
    a = [jnp.concatenate([w['w_branch_a'][l], wb[:256], wb[256:], w['w_branch_c'][l]], axis=1), w['w_out'][l],
         jnp.pad(_w_in_t(w['w_in'])[:, l, :], ((0, ROWS_A - R_WIN - WIN_ROWS), (0, 0)))]
    b = [w['w_ff1'][l], w['w_ff2'][l], jnp.concatenate([xkv[:512], xkv[512:]], axis=1), w['w_xq'][l], w['w_xo'][l]]
    return jnp.concatenate(a, axis=0).astype(BF16), jnp.concatenate(b, axis=0).astype(BF16)


def _w_in_rows(gathered):
    t = gathered[:, R_WIN:R_WIN + WIN_ROWS, :].reshape(N_IN, PACK_COLS)
    return jnp.concatenate([t[2312:5384], t[256:1792], t[1800:2312], t[0:256],
                            jnp.pad(t[1792:1800], ((0, NP - P_F - 8), (0, 0)))], axis=0)


def _w_in_grad_rows(grads, dwt):
    t = jnp.concatenate([dwt[P_A:P_A + 256], dwt[P_Q:P_Q + 1536], dwt[P_F:P_F + 8], dwt[P_C:P_C + 512], dwt[P_G:P_G + 3072]],
                        axis=0)
    return lax.dynamic_update_slice(grads, t.reshape(4, WIN_ROWS, PACK_COLS).astype(grads.dtype), (0, R_WIN, 0))


def _small_prep(sw, l):
    eye = jnp.eye(4, dtype=F32)
    bd = jnp.einsum('gh,gcd->gchd', eye, sw['pool_w'][l]).reshape(POOL_W, POOL_W).astype(BF16)
    tril = jnp.tril(jnp.ones((SGU_CHUNK, SGU_CHUNK), F32))
    wm = (sw['sgu_w'][l] * tril[None]).astype(BF16)
    return dict(
        g_mix=sw['norm_mix_g'][l][None], g_x=sw['norm_xattn_g'][l][None], g_mem=sw['norm_mem_g'][l][None],
        g_ffn=sw['norm_ffn_g'][l][None], bd=bd, pool_scale=sw['pool_scale'][l][None],
        bf=jnp.pad(sw['b_forget'][l], (0, FCOLS - 8))[None], sgu_g=sw['sgu_norm_g'][l][None], wm=wm,
        wmt=jnp.transpose(wm, (0, 2, 1)), sgu_bias=jnp.repeat(sw['sgu_b'][l].T, 64, axis=1), bg=sw['b_gate'][l][None])


def _rows4(r0):
    return dict(n=D, k=D, tn=D, b_block=(4, 256, PACK_COLS), b_index=lambda i, j, k: (0, r0 // 256, 0))


def _rows_t(r0):
    return dict(tb=True, n=D, k=D, tn=D, b_block=(4, 256, PACK_COLS), b_index=lambda i, j, k: (0, r0 // 256, 0))


def _rows_grad(r0):
    return dict(ta=True, tm=D, tn=512, o_block=(4, 256, 512), o_index=lambda i, j, k: (0, r0 // 256, j))


def _add_to(r, e):
    return e + r


def _after(v, token):
    return v if token is None else v + token[0, 0]


def _layer_fwd(x, mem, GA, w_in_t, sp, l, token, second):
    t = f"l{l}"
    S = x.shape[0]
    h = _rms_fwd(x, _after(sp['g_mix'], token), f"rms_mix_{t}")
    proj = _mm(h, w_in_t, name=f"proj_{t}", out_dtype=F32, tb=True)
    d, ya = _pool_fwd(proj, sp['bd'], sp['pool_scale'], f"pool_fwd_{t}")
    fcum = _fgate_fwd(proj, sp['bf'], f"fgate_fwd_{t}")
    f8 = fcum[:, :8]
    fcol = f8.reshape(S, 4, 2).transpose(1, 0, 2)
    frow = f8.T.reshape(4, 2, S)
    qkv = proj[:, P_Q:P_Q + 3 * FOX_W].astype(BF16)
    o, o32, lse = _fox_fwd(qkv, fcol, frow, f"fox_fwd_{t}")
    sg = _sgu_fwd(proj, sp['sgu_g'], sp['wm'], sp['sgu_bias'], f"sgu_fwd_{t}")
    merged = _merge_fwd(proj, ya, o, sg, GA, sp['bg'], f"merge_fwd_{t}")
    x1 = _mm(merged, GA, name=f"out_{t}", out_dtype=F32, extra=x, epi=_add_to, **_rows4(R_OUT))
    GB, token = second(x1)
    hx = _rms_fwd(x1, _after(sp['g_x'], token), f"rms_x_{t}")
    hm = _rms_fwd(mem, sp['g_mem'], f"rms_mem_{t}")
    xq = _mm(hx, GB, name=f"xq_{t}", out_dtype=BF16, **_rows4(R_XQ))
    kv = _mm(hm, GB, name=f"xkv_{t}", out_dtype=BF16, n=2 * D, k=D, tn=512, tk=512, b_block=(None, 512, 512),
             b_index=lambda i, j, k: (j, R_XKV // 512, k))
    o2 = _xattn_fwd(xq, kv, f"xattn_fwd_{t}")
    x2 = _mm(o2, GB, name=f"xo_{t}", out_dtype=F32, extra=x1, epi=_add_to, **_rows4(R_XO))
    hf = _rms_fwd(x2, sp['g_ffn'], f"rms_ffn_{t}")
    z = _mm(hf, GB, name=f"ff1_{t}", out_dtype=F32, n=D_FF, k=D, tn=512, b_block=(None, 1024, 512),
            b_index=lambda i, j, k: (j // 2, R_FF1 // 1024, j % 2))
    x3 = _mm(z, GB, name=f"ff2_{t}", out_dtype=F32, a_fn=_relu2, extra=x2, epi=_add_to, n=D, k=D_FF, tk=1024, tn=D,
             b_block=(None, 1024, PACK_COLS), b_index=lambda i, j, k: (k, R_FF2 // 1024, 0))
    saved = dict(x=x, h=h, proj=proj, d=d, ya=ya, fcol=fcol, frow=frow, qkv=qkv, o=o, o32=o32, lse=lse, sg=sg, merged=merged,
                 x1=x1, hx=hx, hm=hm, xq=xq, kv=kv, o2=o2, x2=x2, hf=hf, z=z, GA=GA, GB=GB, w_in_t=w_in_t)
    return x3, saved


def _layer_bwd(dx3, mem, sp, sv, l, token, early):
    t = f"l{l}"
    S = dx3.shape[0]
    GA, GB = sv['GA'], sv['GB']
    gs = {}
    dx3 = _after(dx3, token)
    gb = lax.empty((4, ROWS_B, PACK_COLS), BF16)
    dz = _mm(dx3, GB, name=f"d_a2_{t}", out_dtype=BF16, tb=True, n=D_FF, k=D, tn=512, b_block=(None, 512, PACK_COLS),
             b_index=lambda i, j, k: (j // 2, R_FF2 // 512 + j % 2, 0), extra=sv['z'],
             epi=lambda r, e: r * (2.0 * jnp.maximum(e, 0.0)))
    gb = _mm(sv['z'], dx3, name=f"dw_ff2_{t}", out_dtype=BF16, ta=True, a_fn=_relu2, into=gb, tm=1024, tn=D,
             o_block=(None, 1024, PACK_COLS), o_index=lambda i, j, k: (i, R_FF2 // 1024, 0))
    gb = _mm(sv['hf'], dz, name=f"dw_ff1_{t}", out_dtype=BF16, ta=True, into=gb, tm=1024, tn=512,
             o_block=(None, 1024, 512), o_index=lambda i, j, k: (j // 2, R_FF1 // 1024, j % 2))
    dhf = _mm(dz, GB, name=f"d_hf_{t}", out_dtype=F32, tb=True, n=D, k=D_FF, tn=D, tk=1024, b_block=(None, 1024, PACK_COLS),
              b_index=lambda i, j, k: (k, R_FF1 // 1024, 0))
    dx2, gs['norm_ffn_g'] = _rms_bwd(dhf, sv['x2'], sp['g_ffn'], dx3, f"rms_ffn_bwd_{t}")
    do2 = _mm(dx2, GB, name=f"d_o2_{t}", out_dtype=BF16, **_rows_t(R_XO))
    gb = _mm(sv['o2'], dx2, name=f"dw_xo_{t}", out_dtype=BF16, into=gb, **_rows_grad(R_XO))
    dxq, dkv = _xattn_bwd(sv['xq'], sv['kv'], do2, f"xattn_bwd_{t}")
    gb = _mm(sv['hm'], dkv, name=f"dw_xkv_{t}", out_dtype=BF16, ta=True, into=gb, tm=512, tn=512,
             o_block=(None, 512, 512), o_index=lambda i, j, k: (j, R_XKV // 512, i))
    dhm = _mm(dkv, GB, name=f"d_hm_{t}", out_dtype=F32, tb=True, n=D, k=2 * D, tn=512, tk=512, b_block=(None, 512, 512),
              b_index=lambda i, j, k: (k, R_XKV // 512, j))
    gs['norm_mem_g'] = _rms_bwd(dhm, mem, sp['g_mem'], None, f"rms_mem_bwd_{t}")
    gb = _mm(sv['hx'], dxq, name=f"dw_xq_{t}", out_dtype=BF16, into=gb, **_rows_grad(R_XQ))
    token = early(gb)
    dhx = _mm(dxq, GB, name=f"d_hx_{t}", out_dtype=F32, **_rows_t(R_XQ))
    dx1, gs['norm_xattn_g'] = _rms_bwd(dhx, sv['x1'], _after(sp['g_x'], token), dx2, f"rms_x_bwd_{t}")
    ga = jnp.zeros((4, ROWS_A, PACK_COLS), BF16)
    ga = _mm(sv['merged'], dx1, name=f"dw_out_{t}", out_dtype=BF16, into=ga, **_rows_grad(R_OUT))
    dm = _mm(dx1, GA, name=f"d_merged_{t}", out_dtype=F32, **_rows_t(R_OUT))
    dg, dya, do, dsg, ga, gs['b_gate'] = _merge_bwd(dm, sv['proj'], sv['ya'], sv['o'], sv['sg'], GA, sp['bg'], ga, f"merge_bwd_{t}")
    dc, dws, dbias, gs['sgu_norm_g'] = _sgu_bwd(dsg, sv['proj'], sp['sgu_g'], sp['wm'], sp['wmt'], sp['sgu_bias'], f"sgu_bwd_{t}")
    tril = jnp.tril(jnp.ones((SGU_CHUNK, SGU_CHUNK), F32))
    gs['sgu_w'] = dws * tril[None]
    gs['sgu_b'] = dbias.reshape(SGU_CHUNK, 4, 64).sum(-1).T
    dq, dk, dv, dfrow, dfcol = _fox_bwd(sv['qkv'], sv['o32'], do, sv['lse'], sv['fcol'], sv['frow'], f"fox_bwd_{t}")
    dF = jnp.pad(dfrow.reshape(8, S).T + dfcol.transpose(1, 0, 2).reshape(S, 8), ((0, 0), (0, FCOLS - 8)))
    df, dbf = _fgate_bwd(dF, sv['proj'], sp['bf'], f"fgate_bwd_{t}")
    gs['b_forget'] = dbf[:, :8]
    da, dbd, gs['pool_scale'] = _pool_bwd(dya, sv['d'], sp['bd'], sp['pool_scale'], f"pool_bwd_{t}")
    gs['pool_w'] = jnp.stack([dbd[g * 64:(g + 1) * 64, g * 64:(g + 1) * 64] for g in range(4)])
    dproj = jnp.concatenate([dg, dq, dk, dv, dc, da, df], axis=1)
    dwt = _mm(dproj, sv['h'], name=f"dw_in_{t}", out_dtype=BF16, ta=True, tm=512, tn=1024)
    ga = _w_in_grad_rows(ga, dwt)
    dh = _mm(dproj, sv['w_in_t'], name=f"d_h_{t}", out_dtype=F32, tk=512, tn=D)
    dx, gs['norm_mix_g'] = _rms_bwd(dh, sv['x'], sp['g_mix'], dx1, f"rms_mix_bwd_{t}")
    return dx, ga, gs


SMALL_ROWS = 1424
GRAD_BLOCKS = {
    'w_ff1': ('b', lambda i: (R_FF1 // 256 + i, 0)), 'w_ff2': ('b', lambda i: (R_FF2 // 256 + i, 0)),
    'w_xq': ('b', lambda i: (R_XQ // 256 + i, 0)), 'w_xo': ('b', lambda i: (R_XO // 256 + i, 0)),
    'w_xkv': ('b', lambda i: (R_XKV // 256 + i % 2, i // 2)), 'w_out': ('a', lambda i: (R_OUT // 256 + i, 0)),
    'w_branch_a': ('a', lambda i: (R_BRANCH // 256, 0)), 'w_branch_b': ('a', lambda i: (R_BRANCH // 256, 1 + i)),
    'w_branch_c': ('a', lambda i: (R_BRANCH // 256, 3)),
}


def _pack_small(parts):
    flat = jnp.concatenate([p.reshape(-1) for p in parts])
    return jnp.pad(flat, (0, SMALL_ROWS * 128 - flat.shape[0])).reshape(SMALL_ROWS, 128)


def _unpack_small(buf, shapes):
    flat, out, r = buf.reshape(-1), [], 0
    for s in shapes:
        n = math.prod(s)
        out.append(flat[r:r + n].reshape(s))
        r += n
    return out


def kernel(x, mem, norm_mix_g, w_in, b_forget, pool_w, pool_scale, sgu_norm_g, sgu_w, sgu_b, w_branch_a, w_branch_b, w_branch_c, b_gate, w_out, norm_xattn_g, norm_mem_g, w_xq, w_xkv, w_xo, norm_ffn_g, w_ff1, w_ff2, final_norm_g, loss_target, m_norm_mix_g, m_w_in, m_b_forget, m_pool_w, m_pool_scale, m_sgu_norm_g, m_sgu_w, m_sgu_b, m_w_branch_a, m_w_branch_b, m_w_branch_c, m_b_gate, m_w_out, m_norm_xattn_g, m_norm_mem_g, m_w_xq, m_w_xkv, m_w_xo, m_norm_ffn_g, m_w_ff1, m_w_ff2, m_final_norm_g, v_norm_mix_g, v_w_in, v_b_forget, v_pool_w, v_pool_scale, v_sgu_norm_g, v_sgu_w, v_sgu_b, v_w_branch_a, v_w_branch_b, v_w_branch_c, v_b_gate, v_w_out, v_norm_xattn_g, v_norm_mem_g, v_w_xq, v_w_xkv, v_w_xo, v_norm_ffn_g, v_w_ff1, v_w_ff2, v_final_norm_g):
    args = (norm_mix_g, w_in, b_forget, pool_w, pool_scale, sgu_norm_g, sgu_w, sgu_b, w_branch_a, w_branch_b, w_branch_c, b_gate,
            w_out, norm_xattn_g, norm_mem_g, w_xq, w_xkv, w_xo, norm_ffn_g, w_ff1, w_ff2, final_norm_g)
    margs = (m_norm_mix_g, m_w_in, m_b_forget, m_pool_w, m_pool_scale, m_sgu_norm_g, m_sgu_w, m_sgu_b, m_w_branch_a, m_w_branch_b,
             m_w_branch_c, m_b_gate, m_w_out, m_norm_xattn_g, m_norm_mem_g, m_w_xq, m_w_xkv, m_w_xo, m_norm_ffn_g, m_w_ff1, m_w_ff2,
             m_final_norm_g)
    vargs = (v_norm_mix_g, v_w_in, v_b_forget, v_pool_w, v_pool_scale, v_sgu_norm_g, v_sgu_w, v_sgu_b, v_w_branch_a, v_w_branch_b,
             v_w_branch_c, v_b_gate, v_w_out, v_norm_xattn_g, v_norm_mem_g, v_w_xq, v_w_xkv, v_w_xo, v_norm_ffn_g, v_w_ff1, v_w_ff2,
             v_final_norm_g)
    w = dict(zip(W_NAMES, args))
    mo = dict(zip(W_NAMES, margs))
    vo = dict(zip(W_NAMES, vargs))
    xs, mems, tgt = x[0], mem[0], loss_target[0]
    shards = [_pack_shard(w, l) for l in range(DEPTH)]
    preps = [_small_prep(w, l) for l in range(DEPTH)]

    first_a, _ = _gather_begin(shards[0][0], "a_l0")
    pending_b, token = _gather_begin(shards[0][1], "b_l0")
    GA = None
    act, saved = xs, []
    for l in range(DEPTH):
        nxt = {}
        if l + 1 < DEPTH:
            nxt['a'], ta = _gather_begin(shards[l + 1][0], f"a_l{l + 1}")
            token = ta if token is None else token + ta
        if l == 0:
            GA = _gather_end(first_a, shards[DEPTH - 1][1])

        def second(x1, l=l, pending_b=pending_b, nxt=nxt):
            GB = _gather_end(pending_b, x1)
            if l + 1 == DEPTH:
                return GB, None
            nxt['b'], tb = _gather_begin(shards[l + 1][1], f"b_l{l + 1}")
            return GB, tb

        act, sv = _layer_fwd(act, mems, GA, _w_in_rows(GA), preps[l], l, token, second)
        saved.append(sv)
        if l + 1 < DEPTH:
            GA = _gather_end(nxt['a'], act)
            pending_b, token = nxt['b'], None
    loss_part, dact, d_final_g = _loss_head(act, w['final_norm_g'][None], tgt, "loss_head")

    red_a, red_b, small_g = [None] * DEPTH, [None] * DEPTH, [None] * DEPTH
    token, state_a = None, None
    for l in reversed(range(DEPTH)):
        early = {}

        def start_b(gb, l=l, early=early):
            early['state'], tok = _reduce_begin(gb, f"b_l{l}")
            return tok

        dact, ga, small_g[l] = _layer_bwd(dact, mems, preps[l], saved[l], l, token, start_b)
        if state_a is not None:
            red_a[l + 1] = _reduce_end(state_a, dact)
        red_b[l] = _reduce_end(early['state'], dact)
        state_a, token = _reduce_begin(ga, f"a_l{l}")
    grad_x = dact[None]
    per_layer = [n for n in SMALL_NAMES if n != 'final_norm_g']
    small_shapes = [w[n].shape for n in per_layer] + [(D,), (1,)]
    parts = [jnp.stack([small_g[l][n].reshape(w[n].shape[1:]) for l in range(DEPTH)]) for n in per_layer]
    state_small, token_small = _all_reduce_begin(_pack_small(parts + [d_final_g.reshape(D), loss_part.reshape(1)]), "small")
    token = token + token_small

    grads, delta, new_m, new_v = {}, {}, {}, {}
    for n, (buf, g_index) in GRAD_BLOCKS.items():
        if buf == 'b':
            grads[n], delta[n], new_m[n], new_v[n] = _adamw_packed(red_b, w[n], mo[n], vo[n], g_index, f"adamw_{n}", token)
    red_a[0] = _reduce_end(state_a, new_v['w_xkv'])
    small_red = _unpack_small(_all_reduce_end(state_small, red_a[0]), small_shapes)
    grads.update(zip(per_layer + ['final_norm_g'], small_red[:-1]))
    loss = small_red[-1].reshape(())
    for n, (buf, g_index) in GRAD_BLOCKS.items():
        if buf == 'a':
            grads[n], delta[n], new_m[n], new_v[n] = _adamw_packed(red_a, w[n], mo[n], vo[n], g_index, f"adamw_{n}", token)
    g_t = jnp.stack([r[R_WIN:R_WIN + WIN_ROWS] for r in red_a], axis=1)
    upd = _adamw(g_t, _w_in_t(w['w_in']), _w_in_t(mo['w_in']), _w_in_t(vo['w_in']), "adamw_w_in", block=(WIN_ROWS, DEPTH, 128))
    grads['w_in'], delta['w_in'], new_m['w_in'], new_v['w_in'] = [jnp.transpose(a, (1, 2, 0)) for a in (g_t,) + tuple(upd)]
    small_all = per_layer + ['final_norm_g']
    shapes_all = [w[n].shape for n in small_all]
    packed = [_pack_small([d[n] for n in small_all])[None] for d in (grads, w, mo, vo)]
    ds, ms, vs = _adamw(*packed, "adamw_small")
    for n, a, b, c in zip(small_all, _unpack_small(ds[0], shapes_all), _unpack_small(ms[0], shapes_all), _unpack_small(vs[0], shapes_all)):
        delta[n], new_m[n], new_v[n] = a, b, c

    return (loss, grad_x, *[grads[n] for n in W_NAMES], *[delta[n] for n in W_NAMES], *[new_m[n] for n in W_NAMES],
            *[new_v[n] for n in W_NAMES])
```

```python
import math

import jax
import jax.numpy as jnp
from jax import lax
from jax.experimental import pallas as pl
from jax.experimental.pallas import tpu as pltpu

F32 = jnp.float32
BF16 = jnp.bfloat16

D = 1024
DEPTH = 2
POOL_W = 256
FOX_W = 512
SGU_W = 256
SGU_CHUNK = 128
N_IN = 5384
P_G, P_Q, P_K, P_V, P_C, P_A, P_F = 0, 3072, 3584, 4096, 4608, 5120, 5376
NP = 5632
XH, XHD = 4, 256
D_FF = 4096
EPS = 1e-6
NEG = -1e30
FOX_SCALE = 64 ** -0.5
X_SCALE = 256 ** -0.5
GELU_K = math.sqrt(2.0 / math.pi)
GELU_C = 0.044715

ADAM_LR, ADAM_B1, ADAM_B2, ADAM_EPS, ADAM_WD, ADAM_STEP = 0.001, 0.9, 0.999, 1e-08, 0.01, 10

VMEM_LIMIT = 48 * 1024 * 1024
MESH = pl.DeviceIdType.MESH

IN_NAMES = ['x', 'mem', 'norm_mix_g', 'w_in', 'b_forget', 'pool_w', 'pool_scale', 'sgu_norm_g', 'sgu_w', 'sgu_b',
            'w_branch_a', 'w_branch_b', 'w_branch_c', 'b_gate', 'w_out', 'norm_xattn_g', 'norm_mem_g', 'w_xq',
            'w_xkv', 'w_xo', 'norm_ffn_g', 'w_ff1', 'w_ff2', 'final_norm_g']
W_NAMES = IN_NAMES[2:]
BIG_NAMES = ['w_in', 'w_branch_a', 'w_branch_b', 'w_branch_c', 'w_out', 'w_xq', 'w_xkv', 'w_xo', 'w_ff1', 'w_ff2']
SMALL_NAMES = [n for n in W_NAMES if n not in BIG_NAMES]
PACK_COLS = 1024


ANY = pl.BlockSpec(memory_space=pl.ANY)


def _cp(sem=None):
    return pltpu.CompilerParams(dimension_semantics=sem, vmem_limit_bytes=VMEM_LIMIT)


def _mm(a, b, *, name, out_dtype, ta=False, tb=False, tm=1024, tn=512, tk=1024, a_fn=None, extra=None, epi=None,
        n=None, k=None, b_block=None, b_index=None, into=None, o_block=None, o_index=None):
    M = a.shape[1] if ta else a.shape[0]
    K = k if k is not None else (a.shape[0] if ta else a.shape[1])
    N = n if n is not None else (b.shape[0] if tb else b.shape[1])
    tm, tn, tk = min(tm, M), min(tn, N), min(tk, K)
    assert M % tm == 0 and N % tn == 0 and K % tk == 0, (name, M, N, K)
    nk = K // tk
    a_spec = pl.BlockSpec((tk, tm), lambda i, j, k: (k, i)) if ta else pl.BlockSpec((tm, tk), lambda i, j, k: (i, k))
    if b_block is not None:
        b_spec = pl.BlockSpec(b_block, b_index)
    else:
        b_spec = pl.BlockSpec((tn, tk), lambda i, j, k: (j, k)) if tb else pl.BlockSpec((tk, tn), lambda i, j, k: (k, j))
    dn = (((0 if ta else 1,), (1 if tb else 0,)), ((), ()))
    tile = pl.BlockSpec((tm, tn), lambda i, j, k: (i, j))
    o_spec = pl.BlockSpec(o_block, o_index) if into is not None else tile
    in_specs = [a_spec, b_spec] + ([tile] if extra is not None else []) + ([ANY] if into is not None else [])
    n_in = len(in_specs)

    def body(*refs):
        a_ref, b_ref = refs[0], refs[1]
        e_ref = refs[2] if extra is not None else None
        o_ref, acc_ref = refs[n_in], refs[n_in + 1]
        kk = pl.program_id(2)

        @pl.when(kk == 0)
        def _():
            acc_ref[...] = jnp.zeros_like(acc_ref)

        av = a_ref[...]
        if a_fn is not None:
            av = a_fn(av)
        bv = b_ref[...]
        if bv.ndim == 3:
            bv = bv.reshape(-1, bv.shape[-1])
        acc_ref[...] += lax.dot_general(av.astype(BF16), bv.astype(BF16), dn, preferred_element_type=F32)

        @pl.when(kk == nk - 1)
        def _():
            r = acc_ref[...]
            if epi is not None:
                r = epi(r, e_ref[...])
            o_ref[...] = r.astype(o_ref.dtype).reshape(o_ref.shape)

    args = (a, b) + ((extra,) if extra is not None else ()) + ((into,) if into is not None else ())
    out_shape = jax.ShapeDtypeStruct(into.shape, into.dtype) if into is not None else jax.ShapeDtypeStruct((M, N), out_dtype)
    return pl.pallas_call(
        body, out_shape=out_shape, grid=(M // tm, N // tn, nk), in_specs=in_specs, out_specs=o_spec,
        scratch_shapes=[pltpu.VMEM((tm, tn), F32)], input_output_aliases={n_in - 1: 0} if into is not None else {},
        compiler_params=_cp(("parallel", "parallel", "arbitrary")), name=name)(*args)


def _relu2(z):
    r = jnp.maximum(z, 0.0)
    return r * r


def _rms_fwd(x, g, name, tr=512):
    R, n = x.shape
    tr = min(tr, R)

    def body(x_ref, g_ref, h_ref):
        xv = x_ref[...]
        rstd = lax.rsqrt(jnp.mean(xv * xv, axis=-1, keepdims=True) + EPS)
        h_ref[...] = (xv * rstd * g_ref[...]).astype(BF16)

    return pl.pallas_call(
        body, out_shape=jax.ShapeDtypeStruct((R, n), BF16), grid=(R // tr,),
        in_specs=[pl.BlockSpec((tr, n), lambda i: (i, 0)), pl.BlockSpec((1, n), lambda i: (0, 0))],
        out_specs=pl.BlockSpec((tr, n), lambda i: (i, 0)), compiler_params=_cp(("parallel",)), name=name)(x, g)


def _rms_bwd(dh, x, g, dres, name, tr=512):
    R, n = x.shape
    tr = min(tr, R)
    need_dx = dres is not None

    def body(*refs):
        if need_dx:
            dh_ref, x_ref, g_ref, r_ref, dx_ref, dg_ref = refs
        else:
            dh_ref, x_ref, g_ref, dg_ref = refs
        i = pl.program_id(0)
        xv = x_ref[...]
        dhv = dh_ref[...].astype(F32)
        rstd = lax.rsqrt(jnp.mean(xv * xv, axis=-1, keepdims=True) + EPS)
        xhat = xv * rstd

        @pl.when(i == 0)
        def _():
            dg_ref[...] = jnp.zeros_like(dg_ref)

        dg_ref[...] += jnp.sum(dhv * xhat, axis=0, keepdims=True)
        if need_dx:
            t = dhv * g_ref[...]
            dx_ref[...] = r_ref[...] + rstd * (t - xhat * jnp.mean(t * xhat, axis=-1, keepdims=True))

    row = pl.BlockSpec((tr, n), lambda i: (i, 0))
    vec = pl.BlockSpec((1, n), lambda i: (0, 0))
    if need_dx:
        return pl.pallas_call(
            body, out_shape=(jax.ShapeDtypeStruct((R, n), F32), jax.ShapeDtypeStruct((1, n), F32)), grid=(R // tr,),
            in_specs=[row, row, vec, row], out_specs=(row, vec), compiler_params=_cp(("arbitrary",)), name=name)(dh, x, g, dres)
    return pl.pallas_call(
        body, out_shape=jax.ShapeDtypeStruct((1, n), F32), grid=(R // tr,),
        in_specs=[row, row, vec], out_specs=vec, compiler_params=_cp(("arbitrary",)), name=name)(dh, x, g)


def _loss_head(x, g, tgt, name, tr=512):
    R, n = x.shape

    def body(x_ref, g_ref, t_ref, loss_ref, dx_ref, dg_ref):
        i = pl.program_id(0)
        xv = x_ref[...]
        gv = g_ref[...]
        rstd = lax.rsqrt(jnp.mean(xv * xv, axis=-1, keepdims=True) + EPS)
        xhat = xv * rstd
        e = xhat * gv - t_ref[...]

        @pl.when(i == 0)
        def _():
            loss_ref[...] = jnp.zeros_like(loss_ref)
            dg_ref[...] = jnp.zeros_like(dg_ref)

        loss_ref[...] += 0.5 * jnp.sum(jnp.sum(e * e, axis=-1, keepdims=True) / n, axis=0, keepdims=True)
        dy = e / n
        dg_ref[...] += jnp.sum(dy * xhat, axis=0, keepdims=True)
        t = dy * gv
        dx_ref[...] = rstd * (t - xhat * jnp.mean(t * xhat, axis=-1, keepdims=True))

    row = pl.BlockSpec((tr, n), lambda i: (i, 0))
    vec = pl.BlockSpec((1, n), lambda i: (0, 0))
    one = pl.BlockSpec((1, 1), lambda i: (0, 0))
    return pl.pallas_call(
        body, out_shape=(jax.ShapeDtypeStruct((1, 1), F32), jax.ShapeDtypeStruct((R, n), F32), jax.ShapeDtypeStruct((1, n), F32)),
        grid=(R // tr,), in_specs=[row, vec, row], out_specs=(one, row, vec),
        compiler_params=_cp(("arbitrary",)), name=name)(x, g, tgt)


def _pool_masks(S):
    row = lax.broadcasted_iota(jnp.int32, (S, POOL_W), 0)
    grp = lax.broadcasted_iota(jnp.int32, (S, POOL_W), 1) // 64
    win = jnp.where(grp == 0, 2, jnp.where(grp == 1, 4, jnp.where(grp == 2, 8, 16)))
    cnt = jnp.minimum(row + 1, win).astype(F32)
    return row, grp, cnt


def _by_group(grp, v0, v1, v2, v3):
    return jnp.where(grp == 0, v0, jnp.where(grp == 1, v1, jnp.where(grp == 2, v2, v3)))


def _pool_fwd(proj, bd, scale, name):
    S = proj.shape[0]

    def body(a_ref, bd_ref, sc_ref, d_ref, y_ref):
        a = a_ref[...]
        row, grp, cnt = _pool_masks(S)

        def back(v, k):
            return jnp.where(row >= k, pltpu.roll(v, k, 0), 0.0)

        s1 = a + back(a, 1)
        s2 = s1 + back(s1, 2)
        s3 = s2 + back(s2, 4)
        s4 = s3 + back(s3, 8)
        d = (_by_group(grp, s1, s2, s3, s4) / cnt - a).astype(BF16)
        d_ref[...] = d
        y_ref[...] = (jnp.dot(d, bd_ref[...], preferred_element_type=F32) * sc_ref[...]).astype(BF16)

    full = lambda r, c: pl.BlockSpec((r, c), lambda i: (0, 0))
    return pl.pallas_call(
        body, out_shape=(jax.ShapeDtypeStruct((S, POOL_W), BF16), jax.ShapeDtypeStruct((S, POOL_W), BF16)), grid=(1,),
        in_specs=[pl.BlockSpec((S, POOL_W), lambda i: (0, P_A // POOL_W)), full(POOL_W, POOL_W), full(1, POOL_W)],
        out_specs=(full(S, POOL_W), full(S, POOL_W)), compiler_params=_cp(("arbitrary",)), name=name)(proj, bd, scale)


def _pool_bwd(dya, d, bd, scale, name):
    S = dya.shape[0]

    def body(dy_ref, d_ref, bd_ref, sc_ref, da_ref, dbd_ref, dsc_ref):
        dy = dy_ref[...]
        dv = d_ref[...]
        bdv = bd_ref[...]
        row, grp, cnt = _pool_masks(S)
        yraw = jnp.dot(dv, bdv, preferred_element_type=F32)
        dsc_ref[...] = jnp.sum(dy * yraw, axis=0, keepdims=True)
        tb = (dy * sc_ref[...]).astype(BF16)
        dbd_ref[...] = lax.dot_general(dv, tb, (((0,), (0,)), ((), ())), preferred_element_type=F32)
        dd = lax.dot_general(tb, bdv, (((1,), (1,)), ((), ())), preferred_element_type=F32)
        e = dd / cnt

        def fwd(v, k):
            return jnp.where(row < S - k, pltpu.roll(v, S - k, 0), 0.0)

        r1 = e + fwd(e, 1)
        r2 = r1 + fwd(r1, 2)
        r3 = r2 + fwd(r2, 4)
        r4 = r3 + fwd(r3, 8)
        da_ref[...] = (_by_group(grp, r1, r2, r3, r4) - dd).astype(BF16)

    full = lambda r, c: pl.BlockSpec((r, c), lambda i: (0, 0))
    return pl.pallas_call(
        body, out_shape=(jax.ShapeDtypeStruct((S, POOL_W), BF16), jax.ShapeDtypeStruct((POOL_W, POOL_W), F32),
                         jax.ShapeDtypeStruct((1, POOL_W), F32)), grid=(1,),
        in_specs=[full(S, POOL_W), full(S, POOL_W), full(POOL_W, POOL_W), full(1, POOL_W)],
        out_specs=(full(S, POOL_W), full(POOL_W, POOL_W), full(1, POOL_W)),
        compiler_params=_cp(("arbitrary",)), name=name)(dya, d, bd, scale)


FCOLS = 128


def _log_sigmoid(z):
    return -(jnp.maximum(-z, 0.0) + jnp.log1p(jnp.exp(-jnp.abs(z))))


def _fgate_fwd(proj, bf, name):
    S = proj.shape[0]

    def body(f_ref, b_ref, o_ref):
        v = _log_sigmoid(f_ref[...] + b_ref[...])
        row = lax.broadcasted_iota(jnp.int32, (S, FCOLS), 0)
        k = 1
        while k < S:
            v = v + jnp.where(row >= k, pltpu.roll(v, k, 0), 0.0)
            k *= 2
        o_ref[...] = v

    return pl.pallas_call(
        body, out_shape=jax.ShapeDtypeStruct((S, FCOLS), F32), grid=(1,),
        in_specs=[pl.BlockSpec((S, FCOLS), lambda i: (0, P_F // FCOLS)), pl.BlockSpec((1, FCOLS), lambda i: (0, 0))],
        out_specs=pl.BlockSpec((S, FCOLS), lambda i: (0, 0)), compiler_params=_cp(("arbitrary",)), name=name)(proj, bf)


def _fgate_bwd(dF, proj, bf, name):
    S = proj.shape[0]

    def body(dF_ref, f_ref, b_ref, df_ref, db_ref):
        v = dF_ref[...]
        row = lax.broadcasted_iota(jnp.int32, (S, FCOLS), 0)
        k = 1
        while k < S:
            v = v + jnp.where(row < S - k, pltpu.roll(v, S - k, 0), 0.0)
            k *= 2
        z = f_ref[...] + b_ref[...]
        df = v * (1.0 / (1.0 + jnp.exp(z)))
        db_ref[...] = jnp.sum(df, axis=0, keepdims=True)
        df_ref[...] = jnp.concatenate([df, jnp.zeros_like(df)], axis=1).astype(BF16)

    return pl.pallas_call(
        body, out_shape=(jax.ShapeDtypeStruct((S, 2 * FCOLS), BF16), jax.ShapeDtypeStruct((1, FCOLS), F32)), grid=(1,),
        in_specs=[pl.BlockSpec((S, FCOLS), lambda i: (0, 0)), pl.BlockSpec((S, FCOLS), lambda i: (0, P_F // FCOLS)),
                  pl.BlockSpec((1, FCOLS), lambda i: (0, 0))],
        out_specs=(pl.BlockSpec((S, 2 * FCOLS), lambda i: (0, 0)), pl.BlockSpec((1, FCOLS), lambda i: (0, 0))),
        compiler_params=_cp(("arbitrary",)), name=name)(dF, proj, bf)


def _fox_scores(qe, kj, fq, fk, r0, c0, tq, tk, diagonal):
    s = lax.dot_general(qe, kj, (((1,), (1,)), ((), ())), preferred_element_type=F32) * FOX_SCALE
    s = s + (fq - fk)
    if not diagonal:
        return s
    rows = r0 + lax.broadcasted_iota(jnp.int32, (tq, tk), 0)
    cols = c0 + lax.broadcasted_iota(jnp.int32, (tq, tk), 1)
    return jnp.where(rows >= cols, s, NEG)


FOX_TQ, FOX_TK = 256, 512


def _fox_fwd(qkv, fcol, frow, name):
    S = qkv.shape[0]
    tq, tk = FOX_TQ, min(FOX_TK, S)

    def body(q_ref, k_ref, v_ref, fc_ref, fr_ref, o_ref, o32_ref, lse_ref):
        i = pl.program_id(1)
        r0 = i * tq
        q = q_ref[...]
        half = lax.broadcasted_iota(jnp.int32, (tq, 128), 1) // 64
        qs = [jnp.where(half == e, q, jnp.zeros_like(q)) for e in (0, 1)]
        fqs = [fc_ref[0, :, e:e + 1] for e in (0, 1)]

        def step(j, carry, diagonal=False):
            c0 = pl.multiple_of(j * tk, tk)
            kj = k_ref[pl.ds(c0, tk), :]
            vj = v_ref[pl.ds(c0, tk), :]
            out = []
            for e in (0, 1):
                m, l, acc = carry[e]
                s = _fox_scores(qs[e], kj, fqs[e], fr_ref[0, e:e + 1, pl.ds(c0, tk)], r0, c0, tq, tk, diagonal)
                m_new = jnp.maximum(m, jnp.max(s, axis=-1, keepdims=True))
                alpha = jnp.exp(m - m_new)
                p = jnp.exp(s - m_new)
                out.append((m_new, alpha * l + jnp.sum(p, axis=-1, keepdims=True),
                            alpha * acc + jnp.dot(p.astype(BF16), vj, preferred_element_type=F32)))
            return tuple(out)

        init = (jnp.full((tq, 1), NEG, F32), jnp.zeros((tq, 1), F32), jnp.zeros((tq, 128), F32))
        below = r0 // tk
        carry = lax.fori_loop(0, below, step, (init, init))
        carry = step(below, carry, diagonal=True)
        outs = []
        for e in (0, 1):
            m, l, acc = carry[e]
            outs.append(acc / l)
            lse_ref[0, :, e:e + 1] = m + jnp.log(l)
        o = jnp.where(half == 0, outs[0], outs[1])
        o32_ref[...] = o
        o_ref[...] = o.astype(BF16)

    tile = pl.BlockSpec((tq, 128), lambda h, i: (i, h))
    return pl.pallas_call(
        body, out_shape=(jax.ShapeDtypeStruct((S, FOX_W), BF16), jax.ShapeDtypeStruct((S, FOX_W), F32),
                         jax.ShapeDtypeStruct((4, S, 2), F32)), grid=(4, S // tq),
        in_specs=[tile, pl.BlockSpec((S, 128), lambda h, i: (0, 4 + h)), pl.BlockSpec((S, 128), lambda h, i: (0, 8 + h)),
                  pl.BlockSpec((1, tq, 2), lambda h, i: (h, i, 0)), pl.BlockSpec((1, 2, S), lambda h, i: (h, 0, 0))],
        out_specs=(tile, tile, pl.BlockSpec((1, tq, 2), lambda h, i: (h, i, 0))),
        compiler_params=_cp(("parallel", "parallel")), name=name)(qkv, qkv, qkv, fcol, frow)


def _fox_bwd(qkv, o32, do, lse, fcol, frow, name):
    S = qkv.shape[0]
    tq, tk = FOX_TQ, min(FOX_TK, S)
    nq = S // tq

    def body(q_ref, k_ref, v_ref, o_ref, do_ref, lse_ref, fc_ref, fr_ref, dq_ref, dk_ref, dv_ref, dfr_ref, dfc_ref, dk_acc, dv_acc):
        dk_acc[...] = jnp.zeros_like(dk_acc)
        dv_acc[...] = jnp.zeros_like(dv_acc)
        dfr_ref[...] = jnp.zeros_like(dfr_ref)
        half = lax.broadcasted_iota(jnp.int32, (tq, 128), 1) // 64

        def q_block(i, _):
            r0 = pl.multiple_of(i * tq, tq)
            qi = q_ref[pl.ds(r0, tq), :]
            dob = do_ref[pl.ds(r0, tq), :].astype(BF16)
            row_dot = dob.astype(F32) * o_ref[pl.ds(r0, tq), :]
            qs = [jnp.where(half == e, qi, jnp.zeros_like(qi)) for e in (0, 1)]
            dos = [jnp.where(half == e, dob, jnp.zeros_like(dob)) for e in (0, 1)]
            deltas = [jnp.sum(jnp.where(half == e, row_dot, 0.0), axis=-1, keepdims=True) for e in (0, 1)]
            lses = [lse_ref[0, pl.ds(r0, tq), e:e + 1] for e in (0, 1)]
            fqs = [fc_ref[0, pl.ds(r0, tq), e:e + 1] for e in (0, 1)]

            def step(j, carry, diagonal=False):
                dqs, row_sums = carry
                c0 = pl.multiple_of(j * tk, tk)
                kj = k_ref[pl.ds(c0, tk), :]
                vj = v_ref[pl.ds(c0, tk), :]
                new_dq, new_rows, dkc, dvc = [], [], [], []
                for e in (0, 1):
                    s = _fox_scores(qs[e], kj, fqs[e], fr_ref[0, e:e + 1, pl.ds(c0, tk)], r0, c0, tq, tk, diagonal)
                    p = jnp.exp(s - lses[e])
                    dp = lax.dot_general(dos[e], vj, (((1,), (1,)), ((), ())), preferred_element_type=F32)
                    ds = p * (dp - deltas[e])
                    dfr_ref[0, e:e + 1, pl.ds(c0, tk)] -= jnp.sum(ds, axis=0, keepdims=True)
                    new_rows.append(row_sums[e] + jnp.sum(ds, axis=-1, keepdims=True))
                    dsb = (ds * FOX_SCALE).astype(BF16)
                    dkc.append(lax.dot_general(dsb, qi, (((0,), (0,)), ((), ())), preferred_element_type=F32))
                    dvc.append(lax.dot_general(p.astype(BF16), dob, (((0,), (0,)), ((), ())), preferred_element_type=F32))
                    new_dq.append(dqs[e] + jnp.dot(dsb, kj, preferred_element_type=F32))
                half_k = lax.broadcasted_iota(jnp.int32, (tk, 128), 1) // 64
                dk_acc[pl.ds(c0, tk), :] += jnp.where(half_k == 0, dkc[0], dkc[1])
                dv_acc[pl.ds(c0, tk), :] += jnp.where(half_k == 0, dvc[0], dvc[1])
                return tuple(new_dq), tuple(new_rows)

            zero, zero_col = jnp.zeros((tq, 128), F32), jnp.zeros((tq, 1), F32)
            below = r0 // tk
            carry = lax.fori_loop(0, below, step, ((zero, zero), (zero_col, zero_col)))
            dqs, row_sums = step(below, carry, diagonal=True)
            for e in (0, 1):
                dfc_ref[0, pl.ds(r0, tq), e:e + 1] = row_sums[e]
            dq_ref[pl.ds(r0, tq), :] = jnp.where(half == 0, dqs[0], dqs[1]).astype(BF16)
            return 0

        lax.fori_loop(0, nq, q_block, 0)
        dk_ref[...] = dk_acc[...].astype(BF16)
        dv_ref[...] = dv_acc[...].astype(BF16)

    col = lambda off: pl.BlockSpec((S, 128), lambda h: (0, off + h))
    hs2 = pl.BlockSpec((1, S, 2), lambda h: (h, 0, 0))
    h2s = pl.BlockSpec((1, 2, S), lambda h: (h, 0, 0))
    return pl.pallas_call(
        body, out_shape=(jax.ShapeDtypeStruct((S, FOX_W), BF16),) * 3 + (jax.ShapeDtypeStruct((4, 2, S), F32),
                                                                         jax.ShapeDtypeStruct((4, S, 2), F32)), grid=(4,),
        in_specs=[col(0), col(4), col(8), col(0), col(0), hs2, hs2, h2s],
        out_specs=(col(0), col(0), col(0), h2s, hs2),
        scratch_shapes=[pltpu.VMEM((S, 128), F32), pltpu.VMEM((S, 128), F32)],
        compiler_params=_cp(("parallel",)), name=name)(qkv, qkv, qkv, o32, do, lse, fcol, frow)


def _gelu(x):
    return 0.5 * x * (1.0 + jnp.tanh(GELU_K * (x + GELU_C * x * x * x)))


def _gelu_grad(x):
    th = jnp.tanh(GELU_K * (x + GELU_C * x * x * x))
    return 0.5 * (1.0 + th) + 0.5 * x * (1.0 - th * th) * GELU_K * (1.0 + 3.0 * GELU_C * x * x)


def _sgu_parts(c, gn, w_ref, bias):
    zc = _gelu(c)
    u, vv = zc[:, :SGU_W], zc[:, SGU_W:]
    rstd = lax.rsqrt(jnp.mean(vv * vv, axis=-1, keepdims=True) + EPS)
    vhat = vv * rstd
    vnb = (vhat * gn).astype(BF16)
    grp = lax.broadcasted_iota(jnp.int32, (SGU_CHUNK, SGU_W), 1) // 64
    mixed = bias
    for gi in range(4):
        mixed = mixed + jnp.where(grp == gi, jnp.dot(w_ref[gi], vnb, preferred_element_type=F32), 0.0)
    return u, rstd, vhat, vnb, grp, mixed


def _sgu_fwd(proj, gn, wm, bias, name):
    S = proj.shape[0]

    def body(c_ref, g_ref, w_ref, b_ref, o_ref):
        u, _, _, _, _, mixed = _sgu_parts(c_ref[...], g_ref[...], w_ref, b_ref[...])
        o_ref[...] = (u * mixed).astype(BF16)

    return pl.pallas_call(
        body, out_shape=jax.ShapeDtypeStruct((S, SGU_W), BF16), grid=(S // SGU_CHUNK,),
        in_specs=[pl.BlockSpec((SGU_CHUNK, 2 * SGU_W), lambda i: (i, P_C // (2 * SGU_W))),
                  pl.BlockSpec((1, SGU_W), lambda i: (0, 0)), pl.BlockSpec((4, SGU_CHUNK, SGU_CHUNK), lambda i: (0, 0, 0)),
                  pl.BlockSpec((SGU_CHUNK, SGU_W), lambda i: (0, 0))],
        out_specs=pl.BlockSpec((SGU_CHUNK, SGU_W), lambda i: (i, 0)),
        compiler_params=_cp(("parallel",)), name=name)(proj, gn, wm, bias)


def _sgu_bwd(dsg, proj, gn, wm, wmt, bias, name):
    S = proj.shape[0]

    def body(dsg_ref, c_ref, g_ref, w_ref, wt_ref, b_ref, dc_ref, dw_ref, db_ref, dg_ref):
        i = pl.program_id(0)

        @pl.when(i == 0)
        def _():
            dw_ref[...] = jnp.zeros_like(dw_ref)
            db_ref[...] = jnp.zeros_like(db_ref)
            dg_ref[...] = jnp.zeros_like(dg_ref)

        c = c_ref[...]
        gn_v = g_ref[...]
        u, rstd, vhat, vnb, grp, mixed = _sgu_parts(c, gn_v, w_ref, b_ref[...])
        dsg_v = dsg_ref[...]
        du = dsg_v * mixed
        dmix = dsg_v * u
        db_ref[...] += dmix
        dmb = dmix.astype(BF16)
        dvn = jnp.zeros((SGU_CHUNK, SGU_W), F32)
        for gi in range(4):
            dmg = jnp.where(grp == gi, dmb, jnp.zeros_like(dmb))
            dw_ref[gi] += lax.dot_general(dmg, vnb, (((1,), (1,)), ((), ())), preferred_element_type=F32)
            dvn = dvn + jnp.where(grp == gi, jnp.dot(wt_ref[gi], dmb, preferred_element_type=F32), 0.0)
        dg_ref[...] += jnp.sum(dvn * vhat, axis=0, keepdims=True)
        t = dvn * gn_v
        dvv = rstd * (t - vhat * jnp.mean(t * vhat, axis=-1, keepdims=True))
        dc_ref[...] = (jnp.concatenate([du, dvv], axis=1) * _gelu_grad(c)).astype(BF16)

    w_spec = pl.BlockSpec((4, SGU_CHUNK, SGU_CHUNK), lambda i: (0, 0, 0))
    tile = pl.BlockSpec((SGU_CHUNK, SGU_W), lambda i: (0, 0))
    vec = pl.BlockSpec((1, SGU_W), lambda i: (0, 0))
    return pl.pallas_call(
        body, out_shape=(jax.ShapeDtypeStruct((S, 2 * SGU_W), BF16), jax.ShapeDtypeStruct((4, SGU_CHUNK, SGU_CHUNK), F32),
                         jax.ShapeDtypeStruct((SGU_CHUNK, SGU_W), F32), jax.ShapeDtypeStruct((1, SGU_W), F32)),
        grid=(S // SGU_CHUNK,),
        in_specs=[pl.BlockSpec((SGU_CHUNK, SGU_W), lambda i: (i, 0)),
                  pl.BlockSpec((SGU_CHUNK, 2 * SGU_W), lambda i: (i, P_C // (2 * SGU_W))), vec, w_spec, w_spec, tile],
        out_specs=(pl.BlockSpec((SGU_CHUNK, 2 * SGU_W), lambda i: (i, 0)), w_spec, tile, vec),
        compiler_params=_cp(("arbitrary",)), name=name)(dsg, proj, gn, wm, wmt, bias)


def _sigmoid(z):
    return 1.0 / (1.0 + jnp.exp(-z))


def _merge_specs(tm):
    row = lambda n: pl.BlockSpec((tm, n), lambda i: (i, 0))
    gate = lambda b: pl.BlockSpec((tm, D), lambda i: (i, b))
    full = lambda r, c: pl.BlockSpec((r, c), lambda i: (0, 0))
    packed = pl.BlockSpec((4, 256, PACK_COLS), lambda i: (0, R_BRANCH // 256, 0))
    return row, gate, full, packed


def _branch_shards(c_ref, j):
    return c_ref[j, :, 0:256], c_ref[j, :, 256:512], c_ref[j, :, 512:768], c_ref[j, :, 768:1024]


def _merge_fwd(proj, ya, o, sg, packed_w, bg, name, tm=512):
    S = proj.shape[0]
    row, gate, full, packed = _merge_specs(tm)

    def body(g0, g1, g2, ya_ref, o_ref, sg_ref, c_ref, bg_ref, out_ref):
        yav, ov, sgv = ya_ref[...], o_ref[...], sg_ref[...]
        for j in range(4):
            cols = slice(256 * j, 256 * (j + 1))
            wa, wb0, wb1, wc = _branch_shards(c_ref, j)
            y = (jnp.dot(yav, wa, preferred_element_type=F32),
                 jnp.dot(ov[:, :256], wb0, preferred_element_type=F32) + jnp.dot(ov[:, 256:], wb1, preferred_element_type=F32),
                 jnp.dot(sgv, wc, preferred_element_type=F32))
            acc = jnp.zeros((tm, 256), F32)
            for b, g_ref in enumerate((g0, g1, g2)):
                acc = acc + _sigmoid(g_ref[:, cols] + bg_ref[:, b * D + 256 * j:b * D + 256 * (j + 1)]) * y[b]
            out_ref[:, cols] = acc.astype(BF16)

    return pl.pallas_call(
        body, out_shape=jax.ShapeDtypeStruct((S, D), BF16), grid=(S // tm,),
        in_specs=[gate(0), gate(1), gate(2), row(POOL_W), row(FOX_W), row(SGU_W), packed, full(1, 3 * D)],
        out_specs=row(D), compiler_params=_cp(("parallel",)), name=name)(proj, proj, proj, ya, o, sg, packed_w, bg)


def _merge_bwd(dm, proj, ya, o, sg, packed_w, bg, grads, name, tm=256):
    S = proj.shape[0]
    row, gate, full, packed = _merge_specs(tm)
    tn_dims = (((0,), (0,)), ((), ()))
    nt_dims = (((1,), (1,)), ((), ()))

    def body(dm_ref, g0, g1, g2, ya_ref, o_ref, sg_ref, c_ref, bg_ref, _, dg_ref, dya_ref, do_ref, dsg_ref, dc_ref, dbg_ref, acc):
        i = pl.program_id(0)

        @pl.when(i == 0)
        def _():
            acc[...] = jnp.zeros_like(acc)
            dbg_ref[...] = jnp.zeros_like(dbg_ref)

        yav, ov, sgv = ya_ref[...], o_ref[...], sg_ref[...]
        o0, o1 = ov[:, :256], ov[:, 256:]
        dya = jnp.zeros((tm, POOL_W), F32)
        do0 = jnp.zeros((tm, 256), F32)
        do1 = jnp.zeros((tm, 256), F32)
        dsg = jnp.zeros((tm, SGU_W), F32)
        for j in range(4):
            cols = slice(256 * j, 256 * (j + 1))
            wa, wb0, wb1, wc = _branch_shards(c_ref, j)
            y = (jnp.dot(yav, wa, preferred_element_type=F32),
                 jnp.dot(o0, wb0, preferred_element_type=F32) + jnp.dot(o1, wb1, preferred_element_type=F32),
                 jnp.dot(sgv, wc, preferred_element_type=F32))
            dmv = dm_ref[:, cols]
            dy = []
            for b, g_ref in enumerate((g0, g1, g2)):
                bcols = slice(b * D + 256 * j, b * D + 256 * (j + 1))
                gt = _sigmoid(g_ref[:, cols] + bg_ref[:, bcols])
                dgp = dmv * y[b] * gt * (1.0 - gt)
                dg_ref[:, bcols] = dgp.astype(BF16)
                dbg_ref[:, bcols] += jnp.sum(dgp, axis=0, keepdims=True)
                dy.append((dmv * gt).astype(BF16))
            dya = dya + lax.dot_general(dy[0], wa, nt_dims, preferred_element_type=F32)
            do0 = do0 + lax.dot_general(dy[1], wb0, nt_dims, preferred_element_type=F32)
            do1 = do1 + lax.dot_general(dy[1], wb1, nt_dims, preferred_element_type=F32)
            dsg = dsg + lax.dot_general(dy[2], wc, nt_dims, preferred_element_type=F32)
            acc[j, :, 0:256] += lax.dot_general(yav, dy[0], tn_dims, preferred_element_type=F32)
            acc[j, :, 256:512] += lax.dot_general(o0, dy[1], tn_dims, preferred_element_type=F32)
            acc[j, :, 512:768] += lax.dot_general(o1, dy[1], tn_dims, preferred_element_type=F32)
            acc[j, :, 768:1024] += lax.dot_general(sgv, dy[2], tn_dims, preferred_element_type=F32)
        dya_ref[...] = dya
        do_ref[:, :256] = do0
        do_ref[:, 256:] = do1
        dsg_ref[...] = dsg

        @pl.when(i == pl.num_programs(0) - 1)
        def _():
            dc_ref[...] = acc[...].astype(dc_ref.dtype)

    return pl.pallas_call(
        body, out_shape=(jax.ShapeDtypeStruct((S, 3 * D), BF16), jax.ShapeDtypeStruct((S, POOL_W), F32),
                         jax.ShapeDtypeStruct((S, FOX_W), F32), jax.ShapeDtypeStruct((S, SGU_W), F32),
                         jax.ShapeDtypeStruct(grads.shape, grads.dtype), jax.ShapeDtypeStruct((1, 3 * D), F32)),
        grid=(S // tm,),
        in_specs=[row(D), gate(0), gate(1), gate(2), row(POOL_W), row(FOX_W), row(SGU_W), packed, full(1, 3 * D), ANY],
        out_specs=(row(3 * D), row(POOL_W), row(FOX_W), row(SGU_W), packed, full(1, 3 * D)),
        scratch_shapes=[pltpu.VMEM((4, 256, PACK_COLS), F32)], input_output_aliases={9: 4},
        compiler_params=_cp(("arbitrary",)), name=name)(dm, proj, proj, proj, ya, o, sg, packed_w, bg, grads)


def _xattn_probs(qh, kh):
    s = lax.dot_general(qh, kh, (((1,), (1,)), ((), ())), preferred_element_type=F32) * X_SCALE
    p = jnp.exp(s - jnp.max(s, axis=-1, keepdims=True))
    return p / jnp.sum(p, axis=-1, keepdims=True)


def _xattn_fwd(xq, kv, name, tq=256):
    S = xq.shape[0]
    M = kv.shape[0]

    def body(q_ref, k_ref, v_ref, o_ref):
        for h in range(XH):
            sl = slice(h * XHD, (h + 1) * XHD)
            p = _xattn_probs(q_ref[:, sl], k_ref[:, sl])
            o_ref[:, sl] = jnp.dot(p.astype(BF16), v_ref[:, sl], preferred_element_type=F32).astype(BF16)

    return pl.pallas_call(
        body, out_shape=jax.ShapeDtypeStruct((S, D), BF16), grid=(S // tq,),
        in_specs=[pl.BlockSpec((tq, D), lambda i: (i, 0)), pl.BlockSpec((M, D), lambda i: (0, 0)),
                  pl.BlockSpec((M, D), lambda i: (0, 1))],
        out_specs=pl.BlockSpec((tq, D), lambda i: (i, 0)), compiler_params=_cp(("parallel",)), name=name)(xq, kv, kv)


def _xattn_bwd(xq, kv, do, name, tq=256):
    S = xq.shape[0]
    M = kv.shape[0]

    def body(q_ref, k_ref, v_ref, do_ref, dq_ref, dkv_ref, dk_acc, dv_acc):
        i = pl.program_id(0)

        @pl.when(i == 0)
        def _():
            dk_acc[...] = jnp.zeros_like(dk_acc)
            dv_acc[...] = jnp.zeros_like(dv_acc)

        for h in range(XH):
            sl = slice(h * XHD, (h + 1) * XHD)
            qh, kh, vh, doh = q_ref[:, sl], k_ref[:, sl], v_ref[:, sl], do_ref[:, sl]
            p = _xattn_probs(qh, kh)
            dp = lax.dot_general(doh, vh, (((1,), (1,)), ((), ())), preferred_element_type=F32)
            ds = p * (dp - jnp.sum(p * dp, axis=-1, keepdims=True))
            dsb = (ds * X_SCALE).astype(BF16)
            dq_ref[:, sl] = jnp.dot(dsb, kh, preferred_element_type=F32).astype(BF16)
            dk_acc[:, sl] += lax.dot_general(dsb, qh, (((0,), (0,)), ((), ())), preferred_element_type=F32)
            dv_acc[:, sl] += lax.dot_general(p.astype(BF16), doh, (((0,), (0,)), ((), ())), preferred_element_type=F32)

        @pl.when(i == pl.num_programs(0) - 1)
        def _():
            dkv_ref[:, :D] = dk_acc[...].astype(BF16)
            dkv_ref[:, D:] = dv_acc[...].astype(BF16)

    return pl.pallas_call(
        body, out_shape=(jax.ShapeDtypeStruct((S, D), BF16), jax.ShapeDtypeStruct((M, 2 * D), BF16)), grid=(S // tq,),
        in_specs=[pl.BlockSpec((tq, D), lambda i: (i, 0)), pl.BlockSpec((M, D), lambda i: (0, 0)),
                  pl.BlockSpec((M, D), lambda i: (0, 1)), pl.BlockSpec((tq, D), lambda i: (i, 0))],
        out_specs=(pl.BlockSpec((tq, D), lambda i: (i, 0)), pl.BlockSpec((M, 2 * D), lambda i: (0, 0))),
        scratch_shapes=[pltpu.VMEM((M, D), F32), pltpu.VMEM((M, D), F32)],
        compiler_params=_cp(("arbitrary",)), name=name)(xq, kv, kv, do)


def _adam_math(gv, wv, mv, vv):
    c1 = 1.0 - ADAM_B1 ** ADAM_STEP
    c2 = 1.0 - ADAM_B2 ** ADAM_STEP
    nm = ADAM_B1 * mv + (1.0 - ADAM_B1) * gv
    nv = ADAM_B2 * vv + (1.0 - ADAM_B2) * (gv * gv)
    return -ADAM_LR * ((nm / c1) / (jnp.sqrt(nv / c2) + ADAM_EPS) + ADAM_WD * wv), nm, nv


def _adamw(g, w, m, v, name, block=None):
    if block is None:
        block = (1, 256 if g.shape[1] % 256 == 0 else g.shape[1], g.shape[2])
    grid = tuple(s // b for s, b in zip(g.shape, block))

    def body(g_ref, w_ref, m_ref, v_ref, d_ref, nm_ref, nv_ref):
        d_ref[...], nm_ref[...], nv_ref[...] = _adam_math(g_ref[...], w_ref[...], m_ref[...], v_ref[...])

    blk = pl.BlockSpec(block, lambda a, b, c: (a, b, c))
    return pl.pallas_call(
        body, out_shape=(jax.ShapeDtypeStruct(g.shape, F32),) * 3, grid=grid,
        in_specs=[blk] * 4, out_specs=(blk,) * 3, compiler_params=_cp(("parallel",) * 3), name=name)(g, w, m, v)


def _adamw_packed(red, w, m, v, g_index, name, token, tr=256):
    L, r, c = w.shape
    tr = min(tr, r)

    def body(g0_ref, g1_ref, w_ref, m_ref, v_ref, _, g_ref, d_ref, nm_ref, nv_ref):
        gv = jnp.where(pl.program_id(0) == 0, g0_ref[...], g1_ref[...])
        g_ref[0] = gv
        d_ref[0], nm_ref[0], nv_ref[0] = _adam_math(gv, w_ref[0], m_ref[0], v_ref[0])

    gblk = pl.BlockSpec((tr, c), lambda l, i: g_index(i))
    blk = pl.BlockSpec((1, tr, c), lambda l, i: (l, i, 0))
    return pl.pallas_call(
        body, out_shape=(jax.ShapeDtypeStruct(w.shape, F32),) * 4, grid=(L, r // tr),
        in_specs=[gblk, gblk, blk, blk, blk, pl.BlockSpec((8, 128), lambda l, i: (0, 0))], out_specs=(blk,) * 4,
        compiler_params=_cp(("parallel", "parallel")), name=name)(red[0], red[1], w, m, v, token)


def _row_tile(R):
    return next((t for t in (512, 496, 384, 256) if R % t == 0), R)


def _sum_slots(a, out_dtype, name):
    n, R, C = a.shape
    tr = _row_tile(R)

    def body(a_ref, o_ref):
        acc = a_ref[0].astype(F32)
        for k in range(1, n):
            acc = acc + a_ref[k].astype(F32)
        o_ref[...] = acc.astype(out_dtype)

    return pl.pallas_call(
        body, out_shape=jax.ShapeDtypeStruct((R, C), out_dtype), grid=(R // tr,),
        in_specs=[pl.BlockSpec((n, tr, C), lambda i: (0, i, 0))], out_specs=pl.BlockSpec((tr, C), lambda i: (i, 0)),
        compiler_params=_cp(("parallel",)), name=name)(a)


def _add_pair(a, b, name):
    n, R, C = a.shape
    tr = _row_tile(R)

    def body(a_ref, b_ref, o_ref):
        o_ref[...] = (a_ref[...].astype(F32) + b_ref[...].astype(F32)).astype(BF16)

    blk = pl.BlockSpec((1, tr, C), lambda k, i: (k, i, 0))
    return pl.pallas_call(
        body, out_shape=jax.ShapeDtypeStruct(a.shape, BF16), grid=(n, R // tr), in_specs=[blk, blk], out_specs=blk,
        compiler_params=_cp(("parallel", "parallel")), name=name)(a, b)


LANDING = pl.BlockSpec(memory_space=pltpu.VMEM)


def _landing_params(shape, dtype):
    return pltpu.CompilerParams(vmem_limit_bytes=math.prod(shape) * jnp.dtype(dtype).itemsize + 4 * 1024 * 1024)


def _place():
    return lax.axis_index("x"), lax.axis_index("y"), lax.axis_index("c")


def _other_chips(x, y):
    return [(1 - x, y), (x, 1 - y), (1 - x, 1 - y)]


def _row_chunks(rows, want, align=16):
    n = want
    while n > 1 and rows % (n * align):
        n -= 1
    return n


def _pair_split(g, name, nch=5):
    n, R, C = g.shape
    half = R // 2
    nch = _row_chunks(half, nch)
    cr = half // nch

    def body(g_ref, own_ref, got_ref, send_sems, recv_sems, local_sem):
        x, y, c = _place()
        mine0 = pl.multiple_of(c * half, 16)
        theirs0 = (1 - c) * half
        keep = pltpu.make_async_copy(g_ref.at[:, pl.ds(mine0, half), :], own_ref, local_sem)
        keep.start()
        cps = []
        for s in range(n):
            for q in range(nch):
                src = g_ref.at[s, pl.ds(pl.multiple_of(theirs0 + q * cr, 16), cr), :]
                cps.append(pltpu.make_async_remote_copy(
                    src_ref=src, dst_ref=got_ref.at[s, pl.ds(q * cr, cr), :], send_sem=send_sems.at[s * nch + q],
                    recv_sem=recv_sems.at[s * nch + q], device_id=(x, y, 1 - c), device_id_type=MESH))
        for cp in cps:
            cp.start()
        for cp in cps:
            cp.wait()
        keep.wait()

    sh = jax.ShapeDtypeStruct((n, half, C), g.dtype)
    return pl.pallas_call(
        body, out_shape=(sh, sh), in_specs=[ANY], out_specs=(ANY, LANDING),
        scratch_shapes=[pltpu.SemaphoreType.DMA((n * nch,)), pltpu.SemaphoreType.DMA((n * nch,)), pltpu.SemaphoreType.DMA],
        compiler_params=_landing_params(sh.shape, g.dtype), name=name)(g)


def _pair_gather(t, name, nch=10):
    R = t.shape[0]
    nch = _row_chunks(R, nch, 8)
    cr = R // nch

    def body(t_ref, o_ref, send_sems, recv_sems, local_sem):
        x, y, c = _place()
        own = pltpu.make_async_copy(t_ref, o_ref.at[c], local_sem)
        own.start()
        cps = [pltpu.make_async_remote_copy(src_ref=t_ref.at[pl.ds(q * cr, cr), :], dst_ref=o_ref.at[c, pl.ds(q * cr, cr), :],
                                            send_sem=send_sems.at[q], recv_sem=recv_sems.at[q], device_id=(x, y, 1 - c),
                                            device_id_type=MESH) for q in range(nch)]
        for cp in cps:
            cp.start()
        for cp in cps:
            cp.wait()
        own.wait()

    return pl.pallas_call(
        body, out_shape=jax.ShapeDtypeStruct((2,) + t.shape, t.dtype), in_specs=[ANY], out_specs=LANDING,
        scratch_shapes=[pltpu.SemaphoreType.DMA((nch,)), pltpu.SemaphoreType.DMA((nch,)), pltpu.SemaphoreType.DMA],
        compiler_params=_landing_params((2,) + t.shape, t.dtype), name=name)(t)


HBM = pl.BlockSpec(memory_space=pltpu.HBM)
SEM = pl.BlockSpec(memory_space=pltpu.SEMAPHORE)
SPLIT_COPY = pltpu.CompilerParams(has_side_effects=pltpu.SideEffectType.DATAFLOW_SIDE_EFFECTING)


def _split_exchange(src, rows, src_of, tag, nch=5):
    C = src.shape[-1]
    nch = _row_chunks(rows, nch)
    cr = rows // nch
    n = 3 * nch
    land_shape = (4, rows, C)

    def copies(src_ref, land_ref, send_sems, recv_sems):
        x, y, c = _place()
        j = 2 * x + y
        out = []
        for q in range(nch):
            for k, (px, py) in enumerate(_other_chips(x, y)):
                out.append(pltpu.make_async_remote_copy(
                    src_ref=src_of(src_ref, px, py, c, q * cr, cr), dst_ref=land_ref.at[j, pl.ds(q * cr, cr), :],
                    send_sem=send_sems.at[k * nch + q], recv_sem=recv_sems.at[k * nch + q], device_id=(px, py, c),
                    device_id_type=MESH))
        return out

    def start(src_ref, land_ref, send_sems, recv_sems, src_thru, land_thru, token):
        for cp in copies(src_ref, land_ref, send_sems, recv_sems):
            cp.start()
        token[...] = jnp.zeros_like(token)

    send_sems, recv_sems, src_thru, land_thru, token = pl.pallas_call(
        start, name=f"{tag}_start",
        out_shape=(pltpu.SemaphoreType.DMA((n,)), pltpu.SemaphoreType.DMA((n,)), pltpu.HBM(src.shape, src.dtype),
                   pltpu.HBM(land_shape, src.dtype), jax.ShapeDtypeStruct((8, 128), F32)),
        in_specs=(HBM, HBM), out_specs=(SEM, SEM, HBM, HBM, pl.BlockSpec(memory_space=pltpu.VMEM)),
        input_output_aliases={0: 2, 1: 3}, compiler_params=SPLIT_COPY)(
            pltpu.with_memory_space_constraint(src, pltpu.HBM),
            pltpu.with_memory_space_constraint(lax.empty(land_shape, src.dtype), pltpu.HBM))

    def finish(after):
        def wait(src_ref, land_ref, send_sems, recv_sems, after_ref, src_dead, got_ref):
            for cp in copies(src_ref, land_ref, send_sems, recv_sems):
                cp.wait_send()
                cp.wait_recv()

        return pl.pallas_call(
            wait, name=f"{tag}_wait", out_shape=(pltpu.HBM(src.shape, src.dtype), pltpu.HBM(land_shape, src.dtype)),
            in_specs=(HBM, HBM, SEM, SEM, ANY), out_specs=(HBM, HBM), input_output_aliases={0: 0, 1: 1},
            compiler_params=SPLIT_COPY)(src_thru, land_thru, send_sems, recv_sems, after)

    return token, finish


def _gather_finish(shard, land, name, nch=5):
    R, C = shard.shape
    half = R // 2
    nch = _row_chunks(half, nch)
    cr = half // nch

    def body(s_ref, l_ref, o_ref, send_sems, recv_sems, local_sems):
        x, y, c = _place()
        j = 2 * x + y
        mine0 = c * half
        local = [pltpu.make_async_copy(s_ref, o_ref.at[j], local_sems.at[0])]
        remote = []
        for k, (px, py) in enumerate(_other_chips(x, y)):
            jj = 2 * px + py
            local.append(pltpu.make_async_copy(l_ref.at[jj], o_ref.at[jj, pl.ds(pl.multiple_of(mine0, 16), half), :],
                                               local_sems.at[1 + k]))
            for q in range(nch):
                remote.append(pltpu.make_async_remote_copy(
                    src_ref=l_ref.at[jj, pl.ds(q * cr, cr), :],
                    dst_ref=o_ref.at[jj, pl.ds(pl.multiple_of(mine0 + q * cr, 16), cr), :], send_sem=send_sems.at[k * nch + q],
                    recv_sem=recv_sems.at[k * nch + q], device_id=(x, y, 1 - c), device_id_type=MESH))
        for cp in local + remote:
            cp.start()
        for cp in remote + local:
            cp.wait()

    return pl.pallas_call(
        body, out_shape=jax.ShapeDtypeStruct((4, R, C), shard.dtype), in_specs=[ANY, ANY], out_specs=LANDING,
        scratch_shapes=[pltpu.SemaphoreType.DMA((3 * nch,)), pltpu.SemaphoreType.DMA((3 * nch,)), pltpu.SemaphoreType.DMA((4,))],
        compiler_params=_landing_params((4, R, C), shard.dtype), name=name)(shard, land)


def _sum_slots_own(land, own, name):
    n, R, C = land.shape
    tr = _row_tile(R)
    me = (2 * lax.axis_index("x") + lax.axis_index("y")).astype(jnp.int32).reshape(1)
    if own.ndim == 3:
        own_spec = pl.BlockSpec((None, tr, C), lambda i, me: (me[0], i, 0))
    else:
        own_spec = pl.BlockSpec((tr, C), lambda i, me: (i, 0))

    def body(me_ref, land_ref, own_ref, o_ref):
        acc = None
        for k in range(n):
            v = jnp.where(me_ref[0] == k, own_ref[...], land_ref[k]).astype(F32)
            acc = v if acc is None else acc + v
        o_ref[...] = acc

    return pl.pallas_call(
        body, out_shape=jax.ShapeDtypeStruct((R, C), F32),
        grid_spec=pltpu.PrefetchScalarGridSpec(
            num_scalar_prefetch=1, grid=(R // tr,),
            in_specs=[pl.BlockSpec((n, tr, C), lambda i, me: (0, i, 0)), own_spec],
            out_specs=pl.BlockSpec((tr, C), lambda i, me: (i, 0))),
        compiler_params=_cp(("parallel",)), name=name)(me, land, own)


def _reduce_begin(g, tag):
    own, got = _pair_split(g, f"rs_pair_{tag}")
    p = _add_pair(own, got, f"rs_add_{tag}")
    token, finish = _split_exchange(p, p.shape[1], lambda ref, px, py, c, r0, cr: ref.at[2 * px + py, pl.ds(r0, cr), :],
                                    f"rs_a2a_{tag}")
    return (finish, g.shape, tag), token


def _reduce_end(state, after):
    finish, shape, tag = state
    p, land = finish(after)
    t = _sum_slots_own(land, p, f"rs_sum_{tag}")
    return _pair_gather(t, f"rs_join_{tag}").reshape(shape[1], shape[2])


def _all_reduce_begin(v, tag):
    p = _sum_slots(_pair_gather(v, f"ar_pair_{tag}"), F32, f"ar_add_{tag}")
    token, finish = _split_exchange(p, p.shape[0], lambda ref, px, py, c, r0, cr: ref.at[pl.ds(r0, cr), :], f"ar_a2a_{tag}")
    return (finish, tag), token


def _all_reduce_end(state, after):
    finish, tag = state
    p, land = finish(after)
    return _sum_slots_own(land, p, f"ar_sum_{tag}")


def _gather_begin(shard, tag):
    half = shard.shape[0] // 2
    token, finish = _split_exchange(
        shard, half, lambda ref, px, py, c, r0, cr: ref.at[pl.ds(pl.multiple_of(c * half + r0, 16), cr), :], f"gather_{tag}")
    return (finish, tag), token


def _gather_end(state, after):
    finish, tag = state
    shard, land = finish(after)
    return _gather_finish(shard, land, f"gather_{tag}_finish")


R_BRANCH, R_OUT, R_WIN, ROWS_A = 0, 256, 512, 1888
R_FF1, R_FF2, R_XKV, R_XQ, R_XO, ROWS_B = 0, 1024, 2048, 2560, 2816, 3072
WIN_ROWS = N_IN // 4


def _w_in_t(a):
    return jnp.transpose(a, (2, 0, 1))


def _pack_shard(w, l):
    xkv, wb = w['w_xkv'][l], w['w_branch_b'][l]
    a = [jnp.concatenate([w['w_branch_a'][l], wb[:256], wb[256:], w['w_branch_c'][l]], axis=1), w['w_out'][l],
         jnp.pad(_w_in_t(w['w_in'])[:, l, :], ((0, ROWS_A - R_WIN - WIN_ROWS), (0, 0)))]
    b = [w['w_ff1'][l], w['w_ff2'][l], jnp.concatenate([xkv[:512], xkv[512:]], axis=1), w['w_xq'][l], w['w_xo'][l]]
    return jnp.concatenate(a, axis=0).astype(BF16), jnp.concatenate(b, axis=0).astype(BF16)


def _w_in_rows(gathered):
    t = gathered[:, R_WIN:R_WIN + WIN_ROWS, :].reshape(N_IN, PACK_COLS)
    return jnp.concatenate([t[2312:5384], t[256:1792], t[1800:2312], t[0:256],
                            jnp.pad(t[1792:1800], ((0, NP - P_F - 8), (0, 0)))], axis=0)


def _w_in_grad_rows(grads, dwt):
    t = jnp.concatenate([dwt[P_A:P_A + 256], dwt[P_Q:P_Q + 1536], dwt[P_F:P_F + 8], dwt[P_C:P_C + 512], dwt[P_G:P_G + 3072]],
                        axis=0)
    return lax.dynamic_update_slice(grads, t.reshape(4, WIN_ROWS, PACK_COLS).astype(grads.dtype), (0, R_WIN, 0))


def _small_prep(sw, l):
    eye = jnp.eye(4, dtype=F32)
    bd = jnp.einsum('gh,gcd->gchd', eye, sw['pool_w'][l]).reshape(POOL_W, POOL_W).astype(BF16)
    tril = jnp.tril(jnp.ones((SGU_CHUNK, SGU_CHUNK), F32))
    wm = (sw['sgu_w'][l] * tril[None]).astype(BF16)
    return dict(
        g_mix=sw['norm_mix_g'][l][None], g_x=sw['norm_xattn_g'][l][None], g_mem=sw['norm_mem_g'][l][None],
        g_ffn=sw['norm_ffn_g'][l][None], bd=bd, pool_scale=sw['pool_scale'][l][None],
        bf=jnp.pad(sw['b_forget'][l], (0, FCOLS - 8))[None], sgu_g=sw['sgu_norm_g'][l][None], wm=wm,
        wmt=jnp.transpose(wm, (0, 2, 1)), sgu_bias=jnp.repeat(sw['sgu_b'][l].T, 64, axis=1), bg=sw['b_gate'][l][None])


def _rows4(r0):
    return dict(n=D, k=D, tn=D, b_block=(4, 256, PACK_COLS), b_index=lambda i, j, k: (0, r0 // 256, 0))


def _rows_t(r0):
    return dict(tb=True, n=D, k=D, tn=D, b_block=(4, 256, PACK_COLS), b_index=lambda i, j, k: (0, r0 // 256, 0))


def _rows_grad(r0):
    return dict(ta=True, tm=D, tn=512, o_block=(4, 256, 512), o_index=lambda i, j, k: (0, r0 // 256, j))


def _add_to(r, e):
    return e + r


def _after(v, token):
    return v if token is None else v + token[0, 0]


def _layer_fwd(x, mem, GA, w_in_t, sp, l, token, second):
    t = f"l{l}"
    S = x.shape[0]
    h = _rms_fwd(x, _after(sp['g_mix'], token), f"rms_mix_{t}")
    proj = _mm(h, w_in_t, name=f"proj_{t}", out_dtype=F32, tb=True)
    d, ya = _pool_fwd(proj, sp['bd'], sp['pool_scale'], f"pool_fwd_{t}")
    fcum = _fgate_fwd(proj, sp['bf'], f"fgate_fwd_{t}")
    f8 = fcum[:, :8]
    fcol = f8.reshape(S, 4, 2).transpose(1, 0, 2)
    frow = f8.T.reshape(4, 2, S)
    qkv = proj[:, P_Q:P_Q + 3 * FOX_W].astype(BF16)
    o, o32, lse = _fox_fwd(qkv, fcol, frow, f"fox_fwd_{t}")
    sg = _sgu_fwd(proj, sp['sgu_g'], sp['wm'], sp['sgu_bias'], f"sgu_fwd_{t}")
    merged = _merge_fwd(proj, ya, o, sg, GA, sp['bg'], f"merge_fwd_{t}")
    x1 = _mm(merged, GA, name=f"out_{t}", out_dtype=F32, extra=x, epi=_add_to, **_rows4(R_OUT))
    GB, token = second(x1)
    hx = _rms_fwd(x1, _after(sp['g_x'], token), f"rms_x_{t}")
    hm = _rms_fwd(mem, sp['g_mem'], f"rms_mem_{t}")
    xq = _mm(hx, GB, name=f"xq_{t}", out_dtype=BF16, **_rows4(R_XQ))
    kv = _mm(hm, GB, name=f"xkv_{t}", out_dtype=BF16, n=2 * D, k=D, tn=512, tk=512, b_block=(None, 512, 512),
             b_index=lambda i, j, k: (j, R_XKV // 512, k))
    o2 = _xattn_fwd(xq, kv, f"xattn_fwd_{t}")
    x2 = _mm(o2, GB, name=f"xo_{t}", out_dtype=F32, extra=x1, epi=_add_to, **_rows4(R_XO))
    hf = _rms_fwd(x2, sp['g_ffn'], f"rms_ffn_{t}")
    z = _mm(hf, GB, name=f"ff1_{t}", out_dtype=F32, n=D_FF, k=D, tn=512, b_block=(None, 1024, 512),
            b_index=lambda i, j, k: (j // 2, R_FF1 // 1024, j % 2))
    x3 = _mm(z, GB, name=f"ff2_{t}", out_dtype=F32, a_fn=_relu2, extra=x2, epi=_add_to, n=D, k=D_FF, tk=1024, tn=D,
             b_block=(None, 1024, PACK_COLS), b_index=lambda i, j, k: (k, R_FF2 // 1024, 0))
    saved = dict(x=x, h=h, proj=proj, d=d, ya=ya, fcol=fcol, frow=frow, qkv=qkv, o=o, o32=o32, lse=lse, sg=sg, merged=merged,
                 x1=x1, hx=hx, hm=hm, xq=xq, kv=kv, o2=o2, x2=x2, hf=hf, z=z, GA=GA, GB=GB, w_in_t=w_in_t)
    return x3, saved


def _layer_bwd(dx3, mem, sp, sv, l, token, early):
    t = f"l{l}"
    S = dx3.shape[0]
    GA, GB = sv['GA'], sv['GB']
    gs = {}
    dx3 = _after(dx3, token)
    gb = lax.empty((4, ROWS_B, PACK_COLS), BF16)
    dz = _mm(dx3, GB, name=f"d_a2_{t}", out_dtype=BF16, tb=True, n=D_FF, k=D, tn=512, b_block=(None, 512, PACK_COLS),
             b_index=lambda i, j, k: (j // 2, R_FF2 // 512 + j % 2, 0), extra=sv['z'],
             epi=lambda r, e: r * (2.0 * jnp.maximum(e, 0.0)))
    gb = _mm(sv['z'], dx3, name=f"dw_ff2_{t}", out_dtype=BF16, ta=True, a_fn=_relu2, into=gb, tm=1024, tn=D,
             o_block=(None, 1024, PACK_COLS), o_index=lambda i, j, k: (i, R_FF2 // 1024, 0))
    gb = _mm(sv['hf'], dz, name=f"dw_ff1_{t}", out_dtype=BF16, ta=True, into=gb, tm=1024, tn=512,
             o_block=(None, 1024, 512), o_index=lambda i, j, k: (j // 2, R_FF1 // 1024, j % 2))
    dhf = _mm(dz, GB, name=f"d_hf_{t}", out_dtype=F32, tb=True, n=D, k=D_FF, tn=D, tk=1024, b_block=(None, 1024, PACK_COLS),
              b_index=lambda i, j, k: (k, R_FF1 // 1024, 0))
    dx2, gs['norm_ffn_g'] = _rms_bwd(dhf, sv['x2'], sp['g_ffn'], dx3, f"rms_ffn_bwd_{t}")
    do2 = _mm(dx2, GB, name=f"d_o2_{t}", out_dtype=BF16, **_rows_t(R_XO))
    gb = _mm(sv['o2'], dx2, name=f"dw_xo_{t}", out_dtype=BF16, into=gb, **_rows_grad(R_XO))
    dxq, dkv = _xattn_bwd(sv['xq'], sv['kv'], do2, f"xattn_bwd_{t}")
    gb = _mm(sv['hm'], dkv, name=f"dw_xkv_{t}", out_dtype=BF16, ta=True, into=gb, tm=512, tn=512,
             o_block=(None, 512, 512), o_index=lambda i, j, k: (j, R_XKV // 512, i))
    dhm = _mm(dkv, GB, name=f"d_hm_{t}", out_dtype=F32, tb=True, n=D, k=2 * D, tn=512, tk=512, b_block=(None, 512, 512),
              b_index=lambda i, j, k: (k, R_XKV // 512, j))
    gs['norm_mem_g'] = _rms_bwd(dhm, mem, sp['g_mem'], None, f"rms_mem_bwd_{t}")
    gb = _mm(sv['hx'], dxq, name=f"dw_xq_{t}", out_dtype=BF16, into=gb, **_rows_grad(R_XQ))
    token = early(gb)
    dhx = _mm(dxq, GB, name=f"d_hx_{t}", out_dtype=F32, **_rows_t(R_XQ))
    dx1, gs['norm_xattn_g'] = _rms_bwd(dhx, sv['x1'], _after(sp['g_x'], token), dx2, f"rms_x_bwd_{t}")
    ga = jnp.zeros((4, ROWS_A, PACK_COLS), BF16)
    ga = _mm(sv['merged'], dx1, name=f"dw_out_{t}", out_dtype=BF16, into=ga, **_rows_grad(R_OUT))
    dm = _mm(dx1, GA, name=f"d_merged_{t}", out_dtype=F32, **_rows_t(R_OUT))
    dg, dya, do, dsg, ga, gs['b_gate'] = _merge_bwd(dm, sv['proj'], sv['ya'], sv['o'], sv['sg'], GA, sp['bg'], ga, f"merge_bwd_{t}")
    dc, dws, dbias, gs['sgu_norm_g'] = _sgu_bwd(dsg, sv['proj'], sp['sgu_g'], sp['wm'], sp['wmt'], sp['sgu_bias'], f"sgu_bwd_{t}")
    tril = jnp.tril(jnp.ones((SGU_CHUNK, SGU_CHUNK), F32))
    gs['sgu_w'] = dws * tril[None]
    gs['sgu_b'] = dbias.reshape(SGU_CHUNK, 4, 64).sum(-1).T
    dq, dk, dv, dfrow, dfcol = _fox_bwd(sv['qkv'], sv['o32'], do, sv['lse'], sv['fcol'], sv['frow'], f"fox_bwd_{t}")
    dF = jnp.pad(dfrow.reshape(8, S).T + dfcol.transpose(1, 0, 2).reshape(S, 8), ((0, 0), (0, FCOLS - 8)))
    df, dbf = _fgate_bwd(dF, sv['proj'], sp['bf'], f"fgate_bwd_{t}")
    gs['b_forget'] = dbf[:, :8]
    da, dbd, gs['pool_scale'] = _pool_bwd(dya, sv['d'], sp['bd'], sp['pool_scale'], f"pool_bwd_{t}")
    gs['pool_w'] = jnp.stack([dbd[g * 64:(g + 1) * 64, g * 64:(g + 1) * 64] for g in range(4)])
    dproj = jnp.concatenate([dg, dq, dk, dv, dc, da, df], axis=1)
    dwt = _mm(dproj, sv['h'], name=f"dw_in_{t}", out_dtype=BF16, ta=True, tm=512, tn=1024)
    ga = _w_in_grad_rows(ga, dwt)
    dh = _mm(dproj, sv['w_in_t'], name=f"d_h_{t}", out_dtype=F32, tk=512, tn=D)
    dx, gs['norm_mix_g'] = _rms_bwd(dh, sv['x'], sp['g_mix'], dx1, f"rms_mix_bwd_{t}")
    return dx, ga, gs


SMALL_ROWS = 1424
GRAD_BLOCKS = {
    'w_ff1': ('b', lambda i: (R_FF1 // 256 + i, 0)), 'w_ff2': ('b', lambda i: (R_FF2 // 256 + i, 0)),
    'w_xq': ('b', lambda i: (R_XQ // 256 + i, 0)), 'w_xo': ('b', lambda i: (R_XO // 256 + i, 0)),
    'w_xkv': ('b', lambda i: (R_XKV // 256 + i % 2, i // 2)), 'w_out': ('a', lambda i: (R_OUT // 256 + i, 0)),
    'w_branch_a': ('a', lambda i: (R_BRANCH // 256, 0)), 'w_branch_b': ('a', lambda i: (R_BRANCH // 256, 1 + i)),
    'w_branch_c': ('a', lambda i: (R_BRANCH // 256, 3)),
}


def _pack_small(parts):
    flat = jnp.concatenate([p.reshape(-1) for p in parts])
    return jnp.pad(flat, (0, SMALL_ROWS * 128 - flat.shape[0])).reshape(SMALL_ROWS, 128)


def _unpack_small(buf, shapes):
    flat, out, r = buf.reshape(-1), [], 0
    for s in shapes:
        n = math.prod(s)
        out.append(flat[r:r + n].reshape(s))
        r += n
    return out


def kernel(x, mem, norm_mix_g, w_in, b_forget, pool_w, pool_scale, sgu_norm_g, sgu_w, sgu_b, w_branch_a, w_branch_b, w_branch_c, b_gate, w_out, norm_xattn_g, norm_mem_g, w_xq, w_xkv, w_xo, norm_ffn_g, w_ff1, w_ff2, final_norm_g, loss_target, m_norm_mix_g, m_w_in, m_b_forget, m_pool_w, m_pool_scale, m_sgu_norm_g, m_sgu_w, m_sgu_b, m_w_branch_a, m_w_branch_b, m_w_branch_c, m_b_gate, m_w_out, m_norm_xattn_g, m_norm_mem_g, m_w_xq, m_w_xkv, m_w_xo, m_norm_ffn_g, m_w_ff1, m_w_ff2, m_final_norm_g, v_norm_mix_g, v_w_in, v_b_forget, v_pool_w, v_pool_scale, v_sgu_norm_g, v_sgu_w, v_sgu_b, v_w_branch_a, v_w_branch_b, v_w_branch_c, v_b_gate, v_w_out, v_norm_xattn_g, v_norm_mem_g, v_w_xq, v_w_xkv, v_w_xo, v_norm_ffn_g, v_w_ff1, v_w_ff2, v_final_norm_g):
    args = (norm_mix_g, w_in, b_forget, pool_w, pool_scale, sgu_norm_g, sgu_w, sgu_b, w_branch_a, w_branch_b, w_branch_c, b_gate,
            w_out, norm_xattn_g, norm_mem_g, w_xq, w_xkv, w_xo, norm_ffn_g, w_ff1, w_ff2, final_norm_g)
    margs = (m_norm_mix_g, m_w_in, m_b_forget, m_pool_w, m_pool_scale, m_sgu_norm_g, m_sgu_w, m_sgu_b, m_w_branch_a, m_w_branch_b,
             m_w_branch_c, m_b_gate, m_w_out, m_norm_xattn_g, m_norm_mem_g, m_w_xq, m_w_xkv, m_w_xo, m_norm_ffn_g, m_w_ff1, m_w_ff2,
             m_final_norm_g)
    vargs = (v_norm_mix_g, v_w_in, v_b_forget, v_pool_w, v_pool_scale, v_sgu_norm_g, v_sgu_w, v_sgu_b, v_w_branch_a, v_w_branch_b,
             v_w_branch_c, v_b_gate, v_w_out, v_norm_xattn_g, v_norm_mem_g, v_w_xq, v_w_xkv, v_w_xo, v_norm_ffn_g, v_w_ff1, v_w_ff2,
             v_final_norm_g)
    w = dict(zip(W_NAMES, args))
    mo = dict(zip(W_NAMES, margs))
    vo = dict(zip(W_NAMES, vargs))
    xs, mems, tgt = x[0], mem[0], loss_target[0]
    shards = [_pack_shard(w, l) for l in range(DEPTH)]
    preps = [_small_prep(w, l) for l in range(DEPTH)]

    first_a, _ = _gather_begin(shards[0][0], "a_l0")
    pending_b, token = _gather_begin(shards[0][1], "b_l0")
    GA = None
    act, saved = xs, []
    for l in range(DEPTH):
        nxt = {}
        if l + 1 < DEPTH:
            nxt['a'], ta = _gather_begin(shards[l + 1][0], f"a_l{l + 1}")
            token = ta if token is None else token + ta
        if l == 0:
            GA = _gather_end(first_a, shards[DEPTH - 1][1])

        def second(x1, l=l, pending_b=pending_b, nxt=nxt):
            GB = _gather_end(pending_b, x1)
            if l + 1 == DEPTH:
                return GB, None
            nxt['b'], tb = _gather_begin(shards[l + 1][1], f"b_l{l + 1}")
            return GB, tb

        act, sv = _layer_fwd(act, mems, GA, _w_in_rows(GA), preps[l], l, token, second)
        saved.append(sv)
        if l + 1 < DEPTH:
            GA = _gather_end(nxt['a'], act)
            pending_b, token = nxt['b'], None
    loss_part, dact, d_final_g = _loss_head(act, w['final_norm_g'][None], tgt, "loss_head")

    red_a, red_b, small_g = [None] * DEPTH, [None] * DEPTH, [None] * DEPTH
    token, state_a = None, None
    for l in reversed(range(DEPTH)):
        early = {}

        def start_b(gb, l=l, early=early):
            early['state'], tok = _reduce_begin(gb, f"b_l{l}")
            return tok

        dact, ga, small_g[l] = _layer_bwd(dact, mems, preps[l], saved[l], l, token, start_b)
        if state_a is not None:
            red_a[l + 1] = _reduce_end(state_a, dact)
        red_b[l] = _reduce_end(early['state'], dact)
        state_a, token = _reduce_begin(ga, f"a_l{l}")
    grad_x = dact[None]
    per_layer = [n for n in SMALL_NAMES if n != 'final_norm_g']
    small_shapes = [w[n].shape for n in per_layer] + [(D,), (1,)]
    parts = [jnp.stack([small_g[l][n].reshape(w[n].shape[1:]) for l in range(DEPTH)]) for n in per_layer]
    state_small, token_small = _all_reduce_begin(_pack_small(parts + [d_final_g.reshape(D), loss_part.reshape(1)]), "small")
    token = token + token_small

    grads, delta, new_m, new_v = {}, {}, {}, {}
    for n, (buf, g_index) in GRAD_BLOCKS.items():
        if buf == 'b':
            grads[n], delta[n], new_m[n], new_v[n] = _adamw_packed(red_b, w[n], mo[n], vo[n], g_index, f"adamw_{n}", token)
    red_a[0] = _reduce_end(state_a, new_v['w_xkv'])
    small_red = _unpack_small(_all_reduce_end(state_small, red_a[0]), small_shapes)
    grads.update(zip(per_layer + ['final_norm_g'], small_red[:-1]))
    loss = small_red[-1].reshape(())
    for n, (buf, g_index) in GRAD_BLOCKS.items():
        if buf == 'a':
            grads[n], delta[n], new_m[n], new_v[n] = _adamw_packed(red_a, w[n], mo[n], vo[n], g_index, f"adamw_{n}", token)
    g_t = jnp.stack([r[R_WIN:R_WIN + WIN_ROWS] for r in red_a], axis=1)
    upd = _adamw(g_t, _w_in_t(w['w_in']), _w_in_t(mo['w_in']), _w_in_t(vo['w_in']), "adamw_w_in", block=(WIN_ROWS, DEPTH, 128))
    grads['w_in'], delta['w_in'], new_m['w_in'], new_v['w_in'] = [jnp.transpose(a, (1, 2, 0)) for a in (g_t,) + tuple(upd)]
    small_all = per_layer + ['final_norm_g']
    shapes_all = [w[n].shape for n in small_all]
    packed = [_pack_small([d[n] for n in small_all])[None] for d in (grads, w, mo, vo)]
    ds, ms, vs = _adamw(*packed, "adamw_small")
    for n, a, b, c in zip(small_all, _unpack_small(ds[0], shapes_all), _unpack_small(ms[0], shapes_all), _unpack_small(vs[0], shapes_all)):
        delta[n], new_m[n], new_v[n] = a, b, c

    return (loss, grad_x, *[grads[n] for n in W_NAMES], *[delta[n] for n in W_NAMES], *[new_m[n] for n in W_NAMES],
            *[new_v[n] for n in W_NAMES])
```

```python
import math

import jax
import jax.numpy as jnp
from jax import lax
from jax.experimental import pallas as pl
from jax.experimental.pallas import tpu as pltpu

F32 = jnp.float32
BF16 = jnp.bfloat16

D = 1024
DEPTH = 2
POOL_W = 256
FOX_W = 512
SGU_W = 256
SGU_CHUNK = 128
N_IN = 5384
P_G, P_Q, P_K, P_V, P_C, P_A, P_F = 0, 3072, 3584, 4096, 4608, 5120, 5376
NP = 5632
XH, XHD = 4, 256
D_FF = 4096
EPS = 1e-6
NEG = -1e30
FOX_SCALE = 64 ** -0.5
X_SCALE = 256 ** -0.5
GELU_K = math.sqrt(2.0 / math.pi)
GELU_C = 0.044715

ADAM_LR, ADAM_B1, ADAM_B2, ADAM_EPS, ADAM_WD, ADAM_STEP = 0.001, 0.9, 0.999, 1e-08, 0.01, 10

VMEM_LIMIT = 48 * 1024 * 1024
MESH = pl.DeviceIdType.MESH

IN_NAMES = ['x', 'mem', 'norm_mix_g', 'w_in', 'b_forget', 'pool_w', 'pool_scale', 'sgu_norm_g', 'sgu_w', 'sgu_b',
            'w_branch_a', 'w_branch_b', 'w_branch_c', 'b_gate', 'w_out', 'norm_xattn_g', 'norm_mem_g', 'w_xq',
            'w_xkv', 'w_xo', 'norm_ffn_g', 'w_ff1', 'w_ff2', 'final_norm_g']
W_NAMES = IN_NAMES[2:]
BIG_NAMES = ['w_in', 'w_branch_a', 'w_branch_b', 'w_branch_c', 'w_out', 'w_xq', 'w_xkv', 'w_xo', 'w_ff1', 'w_ff2']
SMALL_NAMES = [n for n in W_NAMES if n not in BIG_NAMES]
PACK_COLS = 1024


ANY = pl.BlockSpec(memory_space=pl.ANY)


def _cp(sem=None):
    return pltpu.CompilerParams(dimension_semantics=sem, vmem_limit_bytes=VMEM_LIMIT)


def _mm(a, b, *, name, out_dtype, ta=False, tb=False, tm=1024, tn=512, tk=1024, a_fn=None, extra=None, epi=None,
        n=None, k=None, b_block=None, b_index=None, into=None, o_block=None, o_index=None):
    M = a.shape[1] if ta else a.shape[0]
    K = k if k is not None else (a.shape[0] if ta else a.shape[1])
    N = n if n is not None else (b.shape[0] if tb else b.shape[1])
    tm, tn, tk = min(tm, M), min(tn, N), min(tk, K)
    assert M % tm == 0 and N % tn == 0 and K % tk == 0, (name, M, N, K)
    nk = K // tk
    a_spec = pl.BlockSpec((tk, tm), lambda i, j, k: (k, i)) if ta else pl.BlockSpec((tm, tk), lambda i, j, k: (i, k))
    if b_block is not None:
        b_spec = pl.BlockSpec(b_block, b_index)
    else:
        b_spec = pl.BlockSpec((tn, tk), lambda i, j, k: (j, k)) if tb else pl.BlockSpec((tk, tn), lambda i, j, k: (k, j))
    dn = (((0 if ta else 1,), (1 if tb else 0,)), ((), ()))
    tile = pl.BlockSpec((tm, tn), lambda i, j, k: (i, j))
    o_spec = pl.BlockSpec(o_block, o_index) if into is not None else tile
    in_specs = [a_spec, b_spec] + ([tile] if extra is not None else []) + ([ANY] if into is not None else [])
    n_in = len(in_specs)

    def body(*refs):
        a_ref, b_ref = refs[0], refs[1]
        e_ref = refs[2] if extra is not None else None
        o_ref, acc_ref = refs[n_in], refs[n_in + 1]
        kk = pl.program_id(2)

        @pl.when(kk == 0)
        def _():
            acc_ref[...] = jnp.zeros_like(acc_ref)

        av = a_ref[...]
        if a_fn is not None:
            av = a_fn(av)
        bv = b_ref[...]
        if bv.ndim == 3:
            bv = bv.reshape(-1, bv.shape[-1])
        acc_ref[...] += lax.dot_general(av.astype(BF16), bv.astype(BF16), dn, preferred_element_type=F32)

        @pl.when(kk == nk - 1)
        def _():
            r = acc_ref[...]
            if epi is not None:
                r = epi(r, e_ref[...])
            o_ref[...] = r.astype(o_ref.dtype).reshape(o_ref.shape)

    args = (a, b) + ((extra,) if extra is not None else ()) + ((into,) if into is not None else ())
    out_shape = jax.ShapeDtypeStruct(into.shape, into.dtype) if into is not None else jax.ShapeDtypeStruct((M, N), out_dtype)
    return pl.pallas_call(
        body, out_shape=out_shape, grid=(M // tm, N // tn, nk), in_specs=in_specs, out_specs=o_spec,
        scratch_shapes=[pltpu.VMEM((tm, tn), F32)], input_output_aliases={n_in - 1: 0} if into is not None else {},
        compiler_params=_cp(("parallel", "parallel", "arbitrary")), name=name)(*args)


def _relu2(z):
    r = jnp.maximum(z, 0.0)
    return r * r


def _rms_fwd(x, g, name, tr=512):
    R, n = x.shape
    tr = min(tr, R)

    def body(x_ref, g_ref, h_ref):
        xv = x_ref[...]
        rstd = lax.rsqrt(jnp.mean(xv * xv, axis=-1, keepdims=True) + EPS)
        h_ref[...] = (xv * rstd * g_ref[...]).astype(BF16)

    return pl.pallas_call(
        body, out_shape=jax.ShapeDtypeStruct((R, n), BF16), grid=(R // tr,),
        in_specs=[pl.BlockSpec((tr, n), lambda i: (i, 0)), pl.BlockSpec((1, n), lambda i: (0, 0))],
        out_specs=pl.BlockSpec((tr, n), lambda i: (i, 0)), compiler_params=_cp(("parallel",)), name=name)(x, g)


def _rms_bwd(dh, x, g, dres, name, tr=512):
    R, n = x.shape
    tr = min(tr, R)
    need_dx = dres is not None

    def body(*refs):
        if need_dx:
            dh_ref, x_ref, g_ref, r_ref, dx_ref, dg_ref = refs
        else:
            dh_ref, x_ref, g_ref, dg_ref = refs
        i = pl.program_id(0)
        xv = x_ref[...]
        dhv = dh_ref[...].astype(F32)
        rstd = lax.rsqrt(jnp.mean(xv * xv, axis=-1, keepdims=True) + EPS)
        xhat = xv * rstd

        @pl.when(i == 0)
        def _():
            dg_ref[...] = jnp.zeros_like(dg_ref)

        dg_ref[...] += jnp.sum(dhv * xhat, axis=0, keepdims=True)
        if need_dx:
            t = dhv * g_ref[...]
            dx_ref[...] = r_ref[...] + rstd * (t - xhat * jnp.mean(t * xhat, axis=-1, keepdims=True))

    row = pl.BlockSpec((tr, n), lambda i: (i, 0))
    vec = pl.BlockSpec((1, n), lambda i: (0, 0))
    if need_dx:
        return pl.pallas_call(
            body, out_shape=(jax.ShapeDtypeStruct((R, n), F32), jax.ShapeDtypeStruct((1, n), F32)), grid=(R // tr,),
            in_specs=[row, row, vec, row], out_specs=(row, vec), compiler_params=_cp(("arbitrary",)), name=name)(dh, x, g, dres)
    return pl.pallas_call(
        body, out_shape=jax.ShapeDtypeStruct((1, n), F32), grid=(R // tr,),
        in_specs=[row, row, vec], out_specs=vec, compiler_params=_cp(("arbitrary",)), name=name)(dh, x, g)


def _loss_head(x, g, tgt, name, tr=512):
    R, n = x.shape

    def body(x_ref, g_ref, t_ref, loss_ref, dx_ref, dg_ref):
        i = pl.program_id(0)
        xv = x_ref[...]
        gv = g_ref[...]
        rstd = lax.rsqrt(jnp.mean(xv * xv, axis=-1, keepdims=True) + EPS)
        xhat = xv * rstd
        e = xhat * gv - t_ref[...]

        @pl.when(i == 0)
        def _():
            loss_ref[...] = jnp.zeros_like(loss_ref)
            dg_ref[...] = jnp.zeros_like(dg_ref)

        loss_ref[...] += 0.5 * jnp.sum(jnp.sum(e * e, axis=-1, keepdims=True) / n, axis=0, keepdims=True)
        dy = e / n
        dg_ref[...] += jnp.sum(dy * xhat, axis=0, keepdims=True)
        t = dy * gv
        dx_ref[...] = rstd * (t - xhat * jnp.mean(t * xhat, axis=-1, keepdims=True))

    row = pl.BlockSpec((tr, n), lambda i: (i, 0))
    vec = pl.BlockSpec((1, n), lambda i: (0, 0))
    one = pl.BlockSpec((1, 1), lambda i: (0, 0))
    return pl.pallas_call(
        body, out_shape=(jax.ShapeDtypeStruct((1, 1), F32), jax.ShapeDtypeStruct((R, n), F32), jax.ShapeDtypeStruct((1, n), F32)),
        grid=(R // tr,), in_specs=[row, vec, row], out_specs=(one, row, vec),
        compiler_params=_cp(("arbitrary",)), name=name)(x, g, tgt)


def _pool_masks(S):
    row = lax.broadcasted_iota(jnp.int32, (S, POOL_W), 0)
    grp = lax.broadcasted_iota(jnp.int32, (S, POOL_W), 1) // 64
    win = jnp.where(grp == 0, 2, jnp.where(grp == 1, 4, jnp.where(grp == 2, 8, 16)))
    cnt = jnp.minimum(row + 1, win).astype(F32)
    return row, grp, cnt


def _by_group(grp, v0, v1, v2, v3):
    return jnp.where(grp == 0, v0, jnp.where(grp == 1, v1, jnp.where(grp == 2, v2, v3)))


def _pool_fwd(proj, bd, scale, name):
    S = proj.shape[0]

    def body(a_ref, bd_ref, sc_ref, d_ref, y_ref):
        a = a_ref[...]
        row, grp, cnt = _pool_masks(S)

        def back(v, k):
            return jnp.where(row >= k, pltpu.roll(v, k, 0), 0.0)

        s1 = a + back(a, 1)
        s2 = s1 + back(s1, 2)
        s3 = s2 + back(s2, 4)
        s4 = s3 + back(s3, 8)
        d = (_by_group(grp, s1, s2, s3, s4) / cnt - a).astype(BF16)
        d_ref[...] = d
        y_ref[...] = (jnp.dot(d, bd_ref[...], preferred_element_type=F32) * sc_ref[...]).astype(BF16)

    full = lambda r, c: pl.BlockSpec((r, c), lambda i: (0, 0))
    return pl.pallas_call(
        body, out_shape=(jax.ShapeDtypeStruct((S, POOL_W), BF16), jax.ShapeDtypeStruct((S, POOL_W), BF16)), grid=(1,),
        in_specs=[pl.BlockSpec((S, POOL_W), lambda i: (0, P_A // POOL_W)), full(POOL_W, POOL_W), full(1, POOL_W)],
        out_specs=(full(S, POOL_W), full(S, POOL_W)), compiler_params=_cp(("arbitrary",)), name=name)(proj, bd, scale)


def _pool_bwd(dya, d, bd, scale, name):
    S = dya.shape[0]

    def body(dy_ref, d_ref, bd_ref, sc_ref, da_ref, dbd_ref, dsc_ref):
        dy = dy_ref[...]
        dv = d_ref[...]
        bdv = bd_ref[...]
        row, grp, cnt = _pool_masks(S)
        yraw = jnp.dot(dv, bdv, preferred_element_type=F32)
        dsc_ref[...] = jnp.sum(dy * yraw, axis=0, keepdims=True)
        tb = (dy * sc_ref[...]).astype(BF16)
        dbd_ref[...] = lax.dot_general(dv, tb, (((0,), (0,)), ((), ())), preferred_element_type=F32)
        dd = lax.dot_general(tb, bdv, (((1,), (1,)), ((), ())), preferred_element_type=F32)
        e = dd / cnt

        def fwd(v, k):
            return jnp.where(row < S - k, pltpu.roll(v, S - k, 0), 0.0)

        r1 = e + fwd(e, 1)
        r2 = r1 + fwd(r1, 2)
        r3 = r2 + fwd(r2, 4)
        r4 = r3 + fwd(r3, 8)
        da_ref[...] = (_by_group(grp, r1, r2, r3, r4) - dd).astype(BF16)

    full = lambda r, c: pl.BlockSpec((r, c), lambda i: (0, 0))
    return pl.pallas_call(
        body, out_shape=(jax.ShapeDtypeStruct((S, POOL_W), BF16), jax.ShapeDtypeStruct((POOL_W, POOL_W), F32),
                         jax.ShapeDtypeStruct((1, POOL_W), F32)), grid=(1,),
        in_specs=[full(S, POOL_W), full(S, POOL_W), full(POOL_W, POOL_W), full(1, POOL_W)],
        out_specs=(full(S, POOL_W), full(POOL_W, POOL_W), full(1, POOL_W)),
        compiler_params=_cp(("arbitrary",)), name=name)(dya, d, bd, scale)


FCOLS = 128


def _log_sigmoid(z):
    return -(jnp.maximum(-z, 0.0) + jnp.log1p(jnp.exp(-jnp.abs(z))))


def _fgate_fwd(proj, bf, name):
    S = proj.shape[0]

    def body(f_ref, b_ref, o_ref):
        v = _log_sigmoid(f_ref[...] + b_ref[...])
        row = lax.broadcasted_iota(jnp.int32, (S, FCOLS), 0)
        k = 1
        while k < S:
            v = v + jnp.where(row >= k, pltpu.roll(v, k, 0), 0.0)
            k *= 2
        o_ref[...] = v

    return pl.pallas_call(
        body, out_shape=jax.ShapeDtypeStruct((S, FCOLS), F32), grid=(1,),
        in_specs=[pl.BlockSpec((S, FCOLS), lambda i: (0, P_F // FCOLS)), pl.BlockSpec((1, FCOLS), lambda i: (0, 0))],
        out_specs=pl.BlockSpec((S, FCOLS), lambda i: (0, 0)), compiler_params=_cp(("arbitrary",)), name=name)(proj, bf)


def _fgate_bwd(dF, proj, bf, name):
    S = proj.shape[0]

    def body(dF_ref, f_ref, b_ref, df_ref, db_ref):
        v = dF_ref[...]
        row = lax.broadcasted_iota(jnp.int32, (S, FCOLS), 0)
        k = 1
        while k < S:
            v = v + jnp.where(row < S - k, pltpu.roll(v, S - k, 0), 0.0)
            k *= 2
        z = f_ref[...] + b_ref[...]
        df = v * (1.0 / (1.0 + jnp.exp(z)))
        db_ref[...] = jnp.sum(df, axis=0, keepdims=True)
        df_ref[...] = jnp.concatenate([df, jnp.zeros_like(df)], axis=1).astype(BF16)

    return pl.pallas_call(
        body, out_shape=(jax.ShapeDtypeStruct((S, 2 * FCOLS), BF16), jax.ShapeDtypeStruct((1, FCOLS), F32)), grid=(1,),
        in_specs=[pl.BlockSpec((S, FCOLS), lambda i: (0, 0)), pl.BlockSpec((S, FCOLS), lambda i: (0, P_F // FCOLS)),
                  pl.BlockSpec((1, FCOLS), lambda i: (0, 0))],
        out_specs=(pl.BlockSpec((S, 2 * FCOLS), lambda i: (0, 0)), pl.BlockSpec((1, FCOLS), lambda i: (0, 0))),
        compiler_params=_cp(("arbitrary",)), name=name)(dF, proj, bf)


def _fox_scores(qe, kj, fq, fk, r0, c0, tq, tk, diagonal):
    s = lax.dot_general(qe, kj, (((1,), (1,)), ((), ())), preferred_element_type=F32) * FOX_SCALE
    s = s + (fq - fk)
    if not diagonal:
        return s
    rows = r0 + lax.broadcasted_iota(jnp.int32, (tq, tk), 0)
    cols = c0 + lax.broadcasted_iota(jnp.int32, (tq, tk), 1)
    return jnp.where(rows >= cols, s, NEG)


FOX_TQ, FOX_TK = 512, 512


def _fox_fwd(qkv, fcol, frow, name):
    S = qkv.shape[0]
    tq, tk = FOX_TQ, min(FOX_TK, S)

    def body(q_ref, k_ref, v_ref, fc_ref, fr_ref, o_ref, o32_ref, lse_ref):
        i = pl.program_id(1)
        r0 = i * tq
        q = q_ref[...]
        half = lax.broadcasted_iota(jnp.int32, (tq, 128), 1) // 64
        qs = [jnp.where(half == e, q, jnp.zeros_like(q)) for e in (0, 1)]
        fqs = [fc_ref[0, :, e:e + 1] for e in (0, 1)]

        def step(j, carry, diagonal=False):
            c0 = pl.multiple_of(j * tk, tk)
            kj = k_ref[pl.ds(c0, tk), :]
            vj = v_ref[pl.ds(c0, tk), :]
            out = []
            for e in (0, 1):
                m, l, acc = carry[e]
                s = _fox_scores(qs[e], kj, fqs[e], fr_ref[0, e:e + 1, pl.ds(c0, tk)], r0, c0, tq, tk, diagonal)
                m_new = jnp.maximum(m, jnp.max(s, axis=-1, keepdims=True))
                alpha = jnp.exp(m - m_new)
                p = jnp.exp(s - m_new)
                out.append((m_new, alpha * l + jnp.sum(p, axis=-1, keepdims=True),
                            alpha * acc + jnp.dot(p.astype(BF16), vj, preferred_element_type=F32)))
            return tuple(out)

        init = (jnp.full((tq, 1), NEG, F32), jnp.zeros((tq, 1), F32), jnp.zeros((tq, 128), F32))
        below = r0 // tk
        carry = lax.fori_loop(0, below, step, (init, init))
        carry = step(below, carry, diagonal=True)
        outs = []
        for e in (0, 1):
            m, l, acc = carry[e]
            outs.append(acc / l)
            lse_ref[0, :, e:e + 1] = m + jnp.log(l)
        o = jnp.where(half == 0, outs[0], outs[1])
        o32_ref[...] = o
        o_ref[...] = o.astype(BF16)

    tile = pl.BlockSpec((tq, 128), lambda h, i: (i, h))
    return pl.pallas_call(
        body, out_shape=(jax.ShapeDtypeStruct((S, FOX_W), BF16), jax.ShapeDtypeStruct((S, FOX_W), F32),
                         jax.ShapeDtypeStruct((4, S, 2), F32)), grid=(4, S // tq),
        in_specs=[tile, pl.BlockSpec((S, 128), lambda h, i: (0, 4 + h)), pl.BlockSpec((S, 128), lambda h, i: (0, 8 + h)),
                  pl.BlockSpec((1, tq, 2), lambda h, i: (h, i, 0)), pl.BlockSpec((1, 2, S), lambda h, i: (h, 0, 0))],
        out_specs=(tile, tile, pl.BlockSpec((1, tq, 2), lambda h, i: (h, i, 0))),
        compiler_params=_cp(("parallel", "parallel")), name=name)(qkv, qkv, qkv, fcol, frow)


def _fox_bwd(qkv, o32, do, lse, fcol, frow, name):
    S = qkv.shape[0]
    tq, tk = FOX_TQ, min(FOX_TK, S)
    nq = S // tq

    def body(q_ref, k_ref, v_ref, o_ref, do_ref, lse_ref, fc_ref, fr_ref, dq_ref, dk_ref, dv_ref, dfr_ref, dfc_ref, dk_acc, dv_acc):
        dk_acc[...] = jnp.zeros_like(dk_acc)
        dv_acc[...] = jnp.zeros_like(dv_acc)
        dfr_ref[...] = jnp.zeros_like(dfr_ref)
        half = lax.broadcasted_iota(jnp.int32, (tq, 128), 1) // 64

        def q_block(i, _):
            r0 = pl.multiple_of(i * tq, tq)
            qi = q_ref[pl.ds(r0, tq), :]
            dob = do_ref[pl.ds(r0, tq), :].astype(BF16)
            row_dot = dob.astype(F32) * o_ref[pl.ds(r0, tq), :]
            qs = [jnp.where(half == e, qi, jnp.zeros_like(qi)) for e in (0, 1)]
            dos = [jnp.where(half == e, dob, jnp.zeros_like(dob)) for e in (0, 1)]
            deltas = [jnp.sum(jnp.where(half == e, row_dot, 0.0), axis=-1, keepdims=True) for e in (0, 1)]
            lses = [lse_ref[0, pl.ds(r0, tq), e:e + 1] for e in (0, 1)]
            fqs = [fc_ref[0, pl.ds(r0, tq), e:e + 1] for e in (0, 1)]

            def step(j, carry, diagonal=False):
                dqs, row_sums = carry
                c0 = pl.multiple_of(j * tk, tk)
                kj = k_ref[pl.ds(c0, tk), :]
                vj = v_ref[pl.ds(c0, tk), :]
                new_dq, new_rows, dkc, dvc = [], [], [], []
                for e in (0, 1):
                    s = _fox_scores(qs[e], kj, fqs[e], fr_ref[0, e:e + 1, pl.ds(c0, tk)], r0, c0, tq, tk, diagonal)
                    p = jnp.exp(s - lses[e])
                    dp = lax.dot_general(dos[e], vj, (((1,), (1,)), ((), ())), preferred_element_type=F32)
                    ds = p * (dp - deltas[e])
                    dfr_ref[0, e:e + 1, pl.ds(c0, tk)] -= jnp.sum(ds, axis=0, keepdims=True)
                    new_rows.append(row_sums[e] + jnp.sum(ds, axis=-1, keepdims=True))
                    dsb = (ds * FOX_SCALE).astype(BF16)
                    dkc.append(lax.dot_general(dsb, qi, (((0,), (0,)), ((), ())), preferred_element_type=F32))
                    dvc.append(lax.dot_general(p.astype(BF16), dob, (((0,), (0,)), ((), ())), preferred_element_type=F32))
                    new_dq.append(dqs[e] + jnp.dot(dsb, kj, preferred_element_type=F32))
                half_k = lax.broadcasted_iota(jnp.int32, (tk, 128), 1) // 64
                dk_acc[pl.ds(c0, tk), :] += jnp.where(half_k == 0, dkc[0], dkc[1])
                dv_acc[pl.ds(c0, tk), :] += jnp.where(half_k == 0, dvc[0], dvc[1])
                return tuple(new_dq), tuple(new_rows)

            zero, zero_col = jnp.zeros((tq, 128), F32), jnp.zeros((tq, 1), F32)
            below = r0 // tk
            carry = lax.fori_loop(0, below, step, ((zero, zero), (zero_col, zero_col)))
            dqs, row_sums = step(below, carry, diagonal=True)
            for e in (0, 1):
                dfc_ref[0, pl.ds(r0, tq), e:e + 1] = row_sums[e]
            dq_ref[pl.ds(r0, tq), :] = jnp.where(half == 0, dqs[0], dqs[1]).astype(BF16)
            return 0

        lax.fori_loop(0, nq, q_block, 0)
        dk_ref[...] = dk_acc[...].astype(BF16)
        dv_ref[...] = dv_acc[...].astype(BF16)

    col = lambda off: pl.BlockSpec((S, 128), lambda h: (0, off + h))
    hs2 = pl.BlockSpec((1, S, 2), lambda h: (h, 0, 0))
    h2s = pl.BlockSpec((1, 2, S), lambda h: (h, 0, 0))
    return pl.pallas_call(
        body, out_shape=(jax.ShapeDtypeStruct((S, FOX_W), BF16),) * 3 + (jax.ShapeDtypeStruct((4, 2, S), F32),
                                                                         jax.ShapeDtypeStruct((4, S, 2), F32)), grid=(4,),
        in_specs=[col(0), col(4), col(8), col(0), col(0), hs2, hs2, h2s],
        out_specs=(col(0), col(0), col(0), h2s, hs2),
        scratch_shapes=[pltpu.VMEM((S, 128), F32), pltpu.VMEM((S, 128), F32)],
        compiler_params=_cp(("parallel",)), name=name)(qkv, qkv, qkv, o32, do, lse, fcol, frow)


def _gelu(x):
    return 0.5 * x * (1.0 + jnp.tanh(GELU_K * (x + GELU_C * x * x * x)))


def _gelu_grad(x):
    th = jnp.tanh(GELU_K * (x + GELU_C * x * x * x))
    return 0.5 * (1.0 + th) + 0.5 * x * (1.0 - th * th) * GELU_K * (1.0 + 3.0 * GELU_C * x * x)


def _sgu_parts(c, gn, w_ref, bias):
    zc = _gelu(c)
    u, vv = zc[:, :SGU_W], zc[:, SGU_W:]
    rstd = lax.rsqrt(jnp.mean(vv * vv, axis=-1, keepdims=True) + EPS)
    vhat = vv * rstd
    vnb = (vhat * gn).astype(BF16)
    grp = lax.broadcasted_iota(jnp.int32, (SGU_CHUNK, SGU_W), 1) // 64
    mixed = bias
    for gi in range(4):
        mixed = mixed + jnp.where(grp == gi, jnp.dot(w_ref[gi], vnb, preferred_element_type=F32), 0.0)
    return u, rstd, vhat, vnb, grp, mixed


def _sgu_fwd(proj, gn, wm, bias, name):
    S = proj.shape[0]

    def body(c_ref, g_ref, w_ref, b_ref, o_ref):
        u, _, _, _, _, mixed = _sgu_parts(c_ref[...], g_ref[...], w_ref, b_ref[...])
        o_ref[...] = (u * mixed).astype(BF16)

    return pl.pallas_call(
        body, out_shape=jax.ShapeDtypeStruct((S, SGU_W), BF16), grid=(S // SGU_CHUNK,),
        in_specs=[pl.BlockSpec((SGU_CHUNK, 2 * SGU_W), lambda i: (i, P_C // (2 * SGU_W))),
                  pl.BlockSpec((1, SGU_W), lambda i: (0, 0)), pl.BlockSpec((4, SGU_CHUNK, SGU_CHUNK), lambda i: (0, 0, 0)),
                  pl.BlockSpec((SGU_CHUNK, SGU_W), lambda i: (0, 0))],
        out_specs=pl.BlockSpec((SGU_CHUNK, SGU_W), lambda i: (i, 0)),
        compiler_params=_cp(("parallel",)), name=name)(proj, gn, wm, bias)


def _sgu_bwd(dsg, proj, gn, wm, wmt, bias, name):
    S = proj.shape[0]

    def body(dsg_ref, c_ref, g_ref, w_ref, wt_ref, b_ref, dc_ref, dw_ref, db_ref, dg_ref):
        i = pl.program_id(0)

        @pl.when(i == 0)
        def _():
            dw_ref[...] = jnp.zeros_like(dw_ref)
            db_ref[...] = jnp.zeros_like(db_ref)
            dg_ref[...] = jnp.zeros_like(dg_ref)

        c = c_ref[...]
        gn_v = g_ref[...]
        u, rstd, vhat, vnb, grp, mixed = _sgu_parts(c, gn_v, w_ref, b_ref[...])
        dsg_v = dsg_ref[...]
        du = dsg_v * mixed
        dmix = dsg_v * u
        db_ref[...] += dmix
        dmb = dmix.astype(BF16)
        dvn = jnp.zeros((SGU_CHUNK, SGU_W), F32)
        for gi in range(4):
            dmg = jnp.where(grp == gi, dmb, jnp.zeros_like(dmb))
            dw_ref[gi] += lax.dot_general(dmg, vnb, (((1,), (1,)), ((), ())), preferred_element_type=F32)
            dvn = dvn + jnp.where(grp == gi, jnp.dot(wt_ref[gi], dmb, preferred_element_type=F32), 0.0)
        dg_ref[...] += jnp.sum(dvn * vhat, axis=0, keepdims=True)
        t = dvn * gn_v
        dvv = rstd * (t - vhat * jnp.mean(t * vhat, axis=-1, keepdims=True))
        dc_ref[...] = (jnp.concatenate([du, dvv], axis=1) * _gelu_grad(c)).astype(BF16)

    w_spec = pl.BlockSpec((4, SGU_CHUNK, SGU_CHUNK), lambda i: (0, 0, 0))
    tile = pl.BlockSpec((SGU_CHUNK, SGU_W), lambda i: (0, 0))
    vec = pl.BlockSpec((1, SGU_W), lambda i: (0, 0))
    return pl.pallas_call(
        body, out_shape=(jax.ShapeDtypeStruct((S, 2 * SGU_W), BF16), jax.ShapeDtypeStruct((4, SGU_CHUNK, SGU_CHUNK), F32),
                         jax.ShapeDtypeStruct((SGU_CHUNK, SGU_W), F32), jax.ShapeDtypeStruct((1, SGU_W), F32)),
        grid=(S // SGU_CHUNK,),
        in_specs=[pl.BlockSpec((SGU_CHUNK, SGU_W), lambda i: (i, 0)),
                  pl.BlockSpec((SGU_CHUNK, 2 * SGU_W), lambda i: (i, P_C // (2 * SGU_W))), vec, w_spec, w_spec, tile],
        out_specs=(pl.BlockSpec((SGU_CHUNK, 2 * SGU_W), lambda i: (i, 0)), w_spec, tile, vec),
        compiler_params=_cp(("arbitrary",)), name=name)(dsg, proj, gn, wm, wmt, bias)


def _sigmoid(z):
    return 1.0 / (1.0 + jnp.exp(-z))


def _merge_specs(tm):
    row = lambda n: pl.BlockSpec((tm, n), lambda i: (i, 0))
    gate = lambda b: pl.BlockSpec((tm, D), lambda i: (i, b))
    full = lambda r, c: pl.BlockSpec((r, c), lambda i: (0, 0))
    packed = pl.BlockSpec((4, 256, PACK_COLS), lambda i: (0, R_BRANCH // 256, 0))
    return row, gate, full, packed


def _branch_shards(c_ref, j):
    return c_ref[j, :, 0:256], c_ref[j, :, 256:512], c_ref[j, :, 512:768], c_ref[j, :, 768:1024]


def _merge_fwd(proj, ya, o, sg, packed_w, bg, name, tm=512):
    S = proj.shape[0]
    row, gate, full, packed = _merge_specs(tm)

    def body(g0, g1, g2, ya_ref, o_ref, sg_ref, c_ref, bg_ref, out_ref):
        yav, ov, sgv = ya_ref[...], o_ref[...], sg_ref[...]
        for j in range(4):
            cols = slice(256 * j, 256 * (j + 1))
            wa, wb0, wb1, wc = _branch_shards(c_ref, j)
            y = (jnp.dot(yav, wa, preferred_element_type=F32),
                 jnp.dot(ov[:, :256], wb0, preferred_element_type=F32) + jnp.dot(ov[:, 256:], wb1, preferred_element_type=F32),
                 jnp.dot(sgv, wc, preferred_element_type=F32))
            acc = jnp.zeros((tm, 256), F32)
            for b, g_ref in enumerate((g0, g1, g2)):
                acc = acc + _sigmoid(g_ref[:, cols] + bg_ref[:, b * D + 256 * j:b * D + 256 * (j + 1)]) * y[b]
            out_ref[:, cols] = acc.astype(BF16)

    return pl.pallas_call(
        body, out_shape=jax.ShapeDtypeStruct((S, D), BF16), grid=(S // tm,),
        in_specs=[gate(0), gate(1), gate(2), row(POOL_W), row(FOX_W), row(SGU_W), packed, full(1, 3 * D)],
        out_specs=row(D), compiler_params=_cp(("parallel",)), name=name)(proj, proj, proj, ya, o, sg, packed_w, bg)


def _merge_bwd(dm, proj, ya, o, sg, packed_w, bg, grads, name, tm=256):
    S = proj.shape[0]
    row, gate, full, packed = _merge_specs(tm)
    tn_dims = (((0,), (0,)), ((), ()))
    nt_dims = (((1,), (1,)), ((), ()))

    def body(dm_ref, g0, g1, g2, ya_ref, o_ref, sg_ref, c_ref, bg_ref, _, dg_ref, dya_ref, do_ref, dsg_ref, dc_ref, dbg_ref, acc):
        i = pl.program_id(0)

        @pl.when(i == 0)
        def _():
            acc[...] = jnp.zeros_like(acc)
            dbg_ref[...] = jnp.zeros_like(dbg_ref)

        yav, ov, sgv = ya_ref[...], o_ref[...], sg_ref[...]
        o0, o1 = ov[:, :256], ov[:, 256:]
        dya = jnp.zeros((tm, POOL_W), F32)
        do0 = jnp.zeros((tm, 256), F32)
        do1 = jnp.zeros((tm, 256), F32)
        dsg = jnp.zeros((tm, SGU_W), F32)
        for j in range(4):
            cols = slice(256 * j, 256 * (j + 1))
            wa, wb0, wb1, wc = _branch_shards(c_ref, j)
            y = (jnp.dot(yav, wa, preferred_element_type=F32),
                 jnp.dot(o0, wb0, preferred_element_type=F32) + jnp.dot(o1, wb1, preferred_element_type=F32),
                 jnp.dot(sgv, wc, preferred_element_type=F32))
            dmv = dm_ref[:, cols]
            dy = []
            for b, g_ref in enumerate((g0, g1, g2)):
                bcols = slice(b * D + 256 * j, b * D + 256 * (j + 1))
                gt = _sigmoid(g_ref[:, cols] + bg_ref[:, bcols])
                dgp = dmv * y[b] * gt * (1.0 - gt)
                dg_ref[:, bcols] = dgp.astype(BF16)
                dbg_ref[:, bcols] += jnp.sum(dgp, axis=0, keepdims=True)
                dy.append((dmv * gt).astype(BF16))
            dya = dya + lax.dot_general(dy[0], wa, nt_dims, preferred_element_type=F32)
            do0 = do0 + lax.dot_general(dy[1], wb0, nt_dims, preferred_element_type=F32)
            do1 = do1 + lax.dot_general(dy[1], wb1, nt_dims, preferred_element_type=F32)
            dsg = dsg + lax.dot_general(dy[2], wc, nt_dims, preferred_element_type=F32)
            acc[j, :, 0:256] += lax.dot_general(yav, dy[0], tn_dims, preferred_element_type=F32)
            acc[j, :, 256:512] += lax.dot_general(o0, dy[1], tn_dims, preferred_element_type=F32)
            acc[j, :, 512:768] += lax.dot_general(o1, dy[1], tn_dims, preferred_element_type=F32)
            acc[j, :, 768:1024] += lax.dot_general(sgv, dy[2], tn_dims, preferred_element_type=F32)
        dya_ref[...] = dya
        do_ref[:, :256] = do0
        do_ref[:, 256:] = do1
        dsg_ref[...] = dsg

        @pl.when(i == pl.num_programs(0) - 1)
        def _():
            dc_ref[...] = acc[...].astype(dc_ref.dtype)

    return pl.pallas_call(
        body, out_shape=(jax.ShapeDtypeStruct((S, 3 * D), BF16), jax.ShapeDtypeStruct((S, POOL_W), F32),
                         jax.ShapeDtypeStruct((S, FOX_W), F32), jax.ShapeDtypeStruct((S, SGU_W), F32),
                         jax.ShapeDtypeStruct(grads.shape, grads.dtype), jax.ShapeDtypeStruct((1, 3 * D), F32)),
        grid=(S // tm,),
        in_specs=[row(D), gate(0), gate(1), gate(2), row(POOL_W), row(FOX_W), row(SGU_W), packed, full(1, 3 * D), ANY],
        out_specs=(row(3 * D), row(POOL_W), row(FOX_W), row(SGU_W), packed, full(1, 3 * D)),
        scratch_shapes=[pltpu.VMEM((4, 256, PACK_COLS), F32)], input_output_aliases={9: 4},
        compiler_params=_cp(("arbitrary",)), name=name)(dm, proj, proj, proj, ya, o, sg, packed_w, bg, grads)


def _xattn_probs(qh, kh):
    s = lax.dot_general(qh, kh, (((1,), (1,)), ((), ())), preferred_element_type=F32) * X_SCALE
    p = jnp.exp(s - jnp.max(s, axis=-1, keepdims=True))
    return p / jnp.sum(p, axis=-1, keepdims=True)


def _xattn_fwd(xq, kv, name, tq=512):
    S = xq.shape[0]
    M = kv.shape[0]

    def body(q_ref, k_ref, v_ref, o_ref):
        for h in range(XH):
            sl = slice(h * XHD, (h + 1) * XHD)
            p = _xattn_probs(q_ref[:, sl], k_ref[:, sl])
            o_ref[:, sl] = jnp.dot(p.astype(BF16), v_ref[:, sl], preferred_element_type=F32).astype(BF16)

    return pl.pallas_call(
        body, out_shape=jax.ShapeDtypeStruct((S, D), BF16), grid=(S // tq,),
        in_specs=[pl.BlockSpec((tq, D), lambda i: (i, 0)), pl.BlockSpec((M, D), lambda i: (0, 0)),
                  pl.BlockSpec((M, D), lambda i: (0, 1))],
        out_specs=pl.BlockSpec((tq, D), lambda i: (i, 0)), compiler_params=_cp(("parallel",)), name=name)(xq, kv, kv)


def _xattn_bwd(xq, kv, do, name, tq=512):
    S = xq.shape[0]
    M = kv.shape[0]

    def body(q_ref, k_ref, v_ref, do_ref, dq_ref, dkv_ref, dk_acc, dv_acc):
        i = pl.program_id(0)

        @pl.when(i == 0)
        def _():
            dk_acc[...] = jnp.zeros_like(dk_acc)
            dv_acc[...] = jnp.zeros_like(dv_acc)

        for h in range(XH):
            sl = slice(h * XHD, (h + 1) * XHD)
            qh, kh, vh, doh = q_ref[:, sl], k_ref[:, sl], v_ref[:, sl], do_ref[:, sl]
            p = _xattn_probs(qh, kh)
            dp = lax.dot_general(doh, vh, (((1,), (1,)), ((), ())), preferred_element_type=F32)
            ds = p * (dp - jnp.sum(p * dp, axis=-1, keepdims=True))
            dsb = (ds * X_SCALE).astype(BF16)
            dq_ref[:, sl] = jnp.dot(dsb, kh, preferred_element_type=F32).astype(BF16)
            dk_acc[:, sl] += lax.dot_general(dsb, qh, (((0,), (0,)), ((), ())), preferred_element_type=F32)
            dv_acc[:, sl] += lax.dot_general(p.astype(BF16), doh, (((0,), (0,)), ((), ())), preferred_element_type=F32)

        @pl.when(i == pl.num_programs(0) - 1)
        def _():
            dkv_ref[:, :D] = dk_acc[...].astype(BF16)
            dkv_ref[:, D:] = dv_acc[...].astype(BF16)

    return pl.pallas_call(
        body, out_shape=(jax.ShapeDtypeStruct((S, D), BF16), jax.ShapeDtypeStruct((M, 2 * D), BF16)), grid=(S // tq,),
        in_specs=[pl.BlockSpec((tq, D), lambda i: (i, 0)), pl.BlockSpec((M, D), lambda i: (0, 0)),
                  pl.BlockSpec((M, D), lambda i: (0, 1)), pl.BlockSpec((tq, D), lambda i: (i, 0))],
        out_specs=(pl.BlockSpec((tq, D), lambda i: (i, 0)), pl.BlockSpec((M, 2 * D), lambda i: (0, 0))),
        scratch_shapes=[pltpu.VMEM((M, D), F32), pltpu.VMEM((M, D), F32)],
        compiler_params=_cp(("arbitrary",)), name=name)(xq, kv, kv, do)


def _adam_math(gv, wv, mv, vv):
    c1 = 1.0 - ADAM_B1 ** ADAM_STEP
    c2 = 1.0 - ADAM_B2 ** ADAM_STEP
    nm = ADAM_B1 * mv + (1.0 - ADAM_B1) * gv
    nv = ADAM_B2 * vv + (1.0 - ADAM_B2) * (gv * gv)
    return -ADAM_LR * ((nm / c1) / (jnp.sqrt(nv / c2) + ADAM_EPS) + ADAM_WD * wv), nm, nv


def _adamw(g, w, m, v, name, block=None):
    if block is None:
        block = (1, 256 if g.shape[1] % 256 == 0 else g.shape[1], g.shape[2])
    grid = tuple(s // b for s, b in zip(g.shape, block))

    def body(g_ref, w_ref, m_ref, v_ref, d_ref, nm_ref, nv_ref):
        d_ref[...], nm_ref[...], nv_ref[...] = _adam_math(g_ref[...], w_ref[...], m_ref[...], v_ref[...])

    blk = pl.BlockSpec(block, lambda a, b, c: (a, b, c))
    return pl.pallas_call(
        body, out_shape=(jax.ShapeDtypeStruct(g.shape, F32),) * 3, grid=grid,
        in_specs=[blk] * 4, out_specs=(blk,) * 3, compiler_params=_cp(("parallel",) * 3), name=name)(g, w, m, v)


def _adamw_packed(red, w, m, v, g_index, name, token, tr=256):
    L, r, c = w.shape
    tr = min(tr, r)

    def body(g0_ref, g1_ref, w_ref, m_ref, v_ref, _, g_ref, d_ref, nm_ref, nv_ref):
        gv = jnp.where(pl.program_id(0) == 0, g0_ref[...], g1_ref[...])
        g_ref[0] = gv
        d_ref[0], nm_ref[0], nv_ref[0] = _adam_math(gv, w_ref[0], m_ref[0], v_ref[0])

    gblk = pl.BlockSpec((tr, c), lambda l, i: g_index(i))
    blk = pl.BlockSpec((1, tr, c), lambda l, i: (l, i, 0))
    return pl.pallas_call(
        body, out_shape=(jax.ShapeDtypeStruct(w.shape, F32),) * 4, grid=(L, r // tr),
        in_specs=[gblk, gblk, blk, blk, blk, pl.BlockSpec((8, 128), lambda l, i: (0, 0))], out_specs=(blk,) * 4,
        compiler_params=_cp(("parallel", "parallel")), name=name)(red[0], red[1], w, m, v, token)


def _row_tile(R):
    return next((t for t in (512, 496, 384, 256) if R % t == 0), R)


def _sum_slots(a, out_dtype, name):
    n, R, C = a.shape
    tr = _row_tile(R)

    def body(a_ref, o_ref):
        acc = a_ref[0].astype(F32)
        for k in range(1, n):
            acc = acc + a_ref[k].astype(F32)
        o_ref[...] = acc.astype(out_dtype)

    return pl.pallas_call(
        body, out_shape=jax.ShapeDtypeStruct((R, C), out_dtype), grid=(R // tr,),
        in_specs=[pl.BlockSpec((n, tr, C), lambda i: (0, i, 0))], out_specs=pl.BlockSpec((tr, C), lambda i: (i, 0)),
        compiler_params=_cp(("parallel",)), name=name)(a)


def _add_pair(a, b, name):
    n, R, C = a.shape
    tr = _row_tile(R)

    def body(a_ref, b_ref, o_ref):
        o_ref[...] = (a_ref[...].astype(F32) + b_ref[...].astype(F32)).astype(BF16)

    blk = pl.BlockSpec((1, tr, C), lambda k, i: (k, i, 0))
    return pl.pallas_call(
        body, out_shape=jax.ShapeDtypeStruct(a.shape, BF16), grid=(n, R // tr), in_specs=[blk, blk], out_specs=blk,
        compiler_params=_cp(("parallel", "parallel")), name=name)(a, b)


LANDING = pl.BlockSpec(memory_space=pltpu.VMEM)


def _landing_params(shape, dtype):
    return pltpu.CompilerParams(vmem_limit_bytes=math.prod(shape) * jnp.dtype(dtype).itemsize + 4 * 1024 * 1024)


def _place():
    return lax.axis_index("x"), lax.axis_index("y"), lax.axis_index("c")


def _other_chips(x, y):
    return [(1 - x, y), (x, 1 - y), (1 - x, 1 - y)]


def _row_chunks(rows, want, align=16):
    n = want
    while n > 1 and rows % (n * align):
        n -= 1
    return n


def _pair_split(g, name, nch=5):
    n, R, C = g.shape
    half = R // 2
    nch = _row_chunks(half, nch)
    cr = half // nch

    def body(g_ref, own_ref, got_ref, send_sems, recv_sems, local_sem):
        x, y, c = _place()
        mine0 = pl.multiple_of(c * half, 16)
        theirs0 = (1 - c) * half
        keep = pltpu.make_async_copy(g_ref.at[:, pl.ds(mine0, half), :], own_ref, local_sem)
        keep.start()
        cps = []
        for s in range(n):
            for q in range(nch):
                src = g_ref.at[s, pl.ds(pl.multiple_of(theirs0 + q * cr, 16), cr), :]
                cps.append(pltpu.make_async_remote_copy(
                    src_ref=src, dst_ref=got_ref.at[s, pl.ds(q * cr, cr), :], send_sem=send_sems.at[s * nch + q],
                    recv_sem=recv_sems.at[s * nch + q], device_id=(x, y, 1 - c), device_id_type=MESH))
        for cp in cps:
            cp.start()
        for cp in cps:
            cp.wait()
        keep.wait()

    sh = jax.ShapeDtypeStruct((n, half, C), g.dtype)
    return pl.pallas_call(
        body, out_shape=(sh, sh), in_specs=[ANY], out_specs=(ANY, LANDING),
        scratch_shapes=[pltpu.SemaphoreType.DMA((n * nch,)), pltpu.SemaphoreType.DMA((n * nch,)), pltpu.SemaphoreType.DMA],
        compiler_params=_landing_params(sh.shape, g.dtype), name=name)(g)


def _pair_gather(t, name, nch=10):
    R = t.shape[0]
    nch = _row_chunks(R, nch, 8)
    cr = R // nch

    def body(t_ref, o_ref, send_sems, recv_sems, local_sem):
        x, y, c = _place()
        own = pltpu.make_async_copy(t_ref, o_ref.at[c], local_sem)
        own.start()
        cps = [pltpu.make_async_remote_copy(src_ref=t_ref.at[pl.ds(q * cr, cr), :], dst_ref=o_ref.at[c, pl.ds(q * cr, cr), :],
                                            send_sem=send_sems.at[q], recv_sem=recv_sems.at[q], device_id=(x, y, 1 - c),
                                            device_id_type=MESH) for q in range(nch)]
        for cp in cps:
            cp.start()
        for cp in cps:
            cp.wait()
        own.wait()

    return pl.pallas_call(
        body, out_shape=jax.ShapeDtypeStruct((2,) + t.shape, t.dtype), in_specs=[ANY], out_specs=LANDING,
        scratch_shapes=[pltpu.SemaphoreType.DMA((nch,)), pltpu.SemaphoreType.DMA((nch,)), pltpu.SemaphoreType.DMA],
        compiler_params=_landing_params((2,) + t.shape, t.dtype), name=name)(t)


HBM = pl.BlockSpec(memory_space=pltpu.HBM)
SEM = pl.BlockSpec(memory_space=pltpu.SEMAPHORE)
SPLIT_COPY = pltpu.CompilerParams(has_side_effects=pltpu.SideEffectType.DATAFLOW_SIDE_EFFECTING)


def _split_exchange(src, rows, src_of, tag, nch=5):
    C = src.shape[-1]
    nch = _row_chunks(rows, nch)
    cr = rows // nch
    n = 3 * nch
    land_shape = (4, rows, C)

    def copies(src_ref, land_ref, send_sems, recv_sems):
        x, y, c = _place()
        j = 2 * x + y
        out = []
        for q in range(nch):
            for k, (px, py) in enumerate(_other_chips(x, y)):
                out.append(pltpu.make_async_remote_copy(
                    src_ref=src_of(src_ref, px, py, c, q * cr, cr), dst_ref=land_ref.at[j, pl.ds(q * cr, cr), :],
                    send_sem=send_sems.at[k * nch + q], recv_sem=recv_sems.at[k * nch + q], device_id=(px, py, c),
                    device_id_type=MESH))
        return out

    def start(src_ref, land_ref, send_sems, recv_sems, src_thru, land_thru, token):
        for cp in copies(src_ref, land_ref, send_sems, recv_sems):
            cp.start()
        token[...] = jnp.zeros_like(token)

    send_sems, recv_sems, src_thru, land_thru, token = pl.pallas_call(
        start, name=f"{tag}_start",
        out_shape=(pltpu.SemaphoreType.DMA((n,)), pltpu.SemaphoreType.DMA((n,)), pltpu.HBM(src.shape, src.dtype),
                   pltpu.HBM(land_shape, src.dtype), jax.ShapeDtypeStruct((8, 128), F32)),
        in_specs=(HBM, HBM), out_specs=(SEM, SEM, HBM, HBM, pl.BlockSpec(memory_space=pltpu.VMEM)),
        input_output_aliases={0: 2, 1: 3}, compiler_params=SPLIT_COPY)(
            pltpu.with_memory_space_constraint(src, pltpu.HBM),
            pltpu.with_memory_space_constraint(lax.empty(land_shape, src.dtype), pltpu.HBM))

    def finish(after):
        def wait(src_ref, land_ref, send_sems, recv_sems, after_ref, src_dead, got_ref):
            for cp in copies(src_ref, land_ref, send_sems, recv_sems):
                cp.wait_send()
                cp.wait_recv()

        return pl.pallas_call(
            wait, name=f"{tag}_wait", out_shape=(pltpu.HBM(src.shape, src.dtype), pltpu.HBM(land_shape, src.dtype)),
            in_specs=(HBM, HBM, SEM, SEM, ANY), out_specs=(HBM, HBM), input_output_aliases={0: 0, 1: 1},
            compiler_params=SPLIT_COPY)(src_thru, land_thru, send_sems, recv_sems, after)

    return token, finish


def _gather_finish(shard, land, name, nch=5):
    R, C = shard.shape
    half = R // 2
    nch = _row_chunks(half, nch)
    cr = half // nch

    def body(s_ref, l_ref, o_ref, send_sems, recv_sems, local_sems):
        x, y, c = _place()
        j = 2 * x + y
        mine0 = c * half
        local = [pltpu.make_async_copy(s_ref, o_ref.at[j], local_sems.at[0])]
        remote = []
        for k, (px, py) in enumerate(_other_chips(x, y)):
            jj = 2 * px + py
            local.append(pltpu.make_async_copy(l_ref.at[jj], o_ref.at[jj, pl.ds(pl.multiple_of(mine0, 16), half), :],
                                               local_sems.at[1 + k]))
            for q in range(nch):
                remote.append(pltpu.make_async_remote_copy(
                    src_ref=l_ref.at[jj, pl.ds(q * cr, cr), :],
                    dst_ref=o_ref.at[jj, pl.ds(pl.multiple_of(mine0 + q * cr, 16), cr), :], send_sem=send_sems.at[k * nch + q],
                    recv_sem=recv_sems.at[k * nch + q], device_id=(x, y, 1 - c), device_id_type=MESH))
        for cp in local + remote:
            cp.start()
        for cp in remote + local:
            cp.wait()

    return pl.pallas_call(
        body, out_shape=jax.ShapeDtypeStruct((4, R, C), shard.dtype), in_specs=[ANY, ANY], out_specs=LANDING,
        scratch_shapes=[pltpu.SemaphoreType.DMA((3 * nch,)), pltpu.SemaphoreType.DMA((3 * nch,)), pltpu.SemaphoreType.DMA((4,))],
        compiler_params=_landing_params((4, R, C), shard.dtype), name=name)(shard, land)


def _sum_slots_own(land, own, name):
    n, R, C = land.shape
    tr = _row_tile(R)
    me = (2 * lax.axis_index("x") + lax.axis_index("y")).astype(jnp.int32).reshape(1)
    if own.ndim == 3:
        own_spec = pl.BlockSpec((None, tr, C), lambda i, me: (me[0], i, 0))
    else:
        own_spec = pl.BlockSpec((tr, C), lambda i, me: (i, 0))

    def body(me_ref, land_ref, own_ref, o_ref):
        acc = None
        for k in range(n):
            v = jnp.where(me_ref[0] == k, own_ref[...], land_ref[k]).astype(F32)
            acc = v if acc is None else acc + v
        o_ref[...] = acc

    return pl.pallas_call(
        body, out_shape=jax.ShapeDtypeStruct((R, C), F32),
        grid_spec=pltpu.PrefetchScalarGridSpec(
            num_scalar_prefetch=1, grid=(R // tr,),
            in_specs=[pl.BlockSpec((n, tr, C), lambda i, me: (0, i, 0)), own_spec],
            out_specs=pl.BlockSpec((tr, C), lambda i, me: (i, 0))),
        compiler_params=_cp(("parallel",)), name=name)(me, land, own)


def _reduce_begin(g, tag):
    own, got = _pair_split(g, f"rs_pair_{tag}")
    p = _add_pair(own, got, f"rs_add_{tag}")
    token, finish = _split_exchange(p, p.shape[1], lambda ref, px, py, c, r0, cr: ref.at[2 * px + py, pl.ds(r0, cr), :],
                                    f"rs_a2a_{tag}")
    return (finish, g.shape, tag), token


def _reduce_end(state, after):
    finish, shape, tag = state
    p, land = finish(after)
    t = _sum_slots_own(land, p, f"rs_sum_{tag}")
    return _pair_gather(t, f"rs_join_{tag}").reshape(shape[1], shape[2])


def _all_reduce_begin(v, tag):
    p = _sum_slots(_pair_gather(v, f"ar_pair_{tag}"), F32, f"ar_add_{tag}")
    token, finish = _split_exchange(p, p.shape[0], lambda ref, px, py, c, r0, cr: ref.at[pl.ds(r0, cr), :], f"ar_a2a_{tag}")
    return (finish, tag), token


def _all_reduce_end(state, after):
    finish, tag = state
    p, land = finish(after)
    return _sum_slots_own(land, p, f"ar_sum_{tag}")


def _gather_begin(shard, tag):
    half = shard.shape[0] // 2
    token, finish = _split_exchange(
        shard, half, lambda ref, px, py, c, r0, cr: ref.at[pl.ds(pl.multiple_of(c * half + r0, 16), cr), :], f"gather_{tag}")
    return (finish, tag), token


def _gather_end(state, after):
    finish, tag = state
    shard, land = finish(after)
    return _gather_finish(shard, land, f"gather_{tag}_finish")


R_BRANCH, R_OUT, R_WIN, ROWS_A = 0, 256, 512, 1888
R_FF1, R_FF2, R_XKV, R_XQ, R_XO, ROWS_B = 0, 1024, 2048, 2560, 2816, 3072
WIN_ROWS = N_IN // 4


def _w_in_t(a):
    return jnp.transpose(a, (2, 0, 1))


def _pack_shard(w, l):
    xkv, wb = w['w_xkv'][l], w['w_branch_b'][l]
    a = [jnp.concatenate([w['w_branch_a'][l], wb[:256], wb[256:], w['w_branch_c'][l]], axis=1), w['w_out'][l],
         jnp.pad(_w_in_t(w['w_in'])[:, l, :], ((0, ROWS_A - R_WIN - WIN_ROWS), (0, 0)))]
    b = [w['w_ff1'][l], w['w_ff2'][l], jnp.concatenate([xkv[:512], xkv[512:]], axis=1), w['w_xq'][l], w['w_xo'][l]]
    return jnp.concatenate(a, axis=0).astype(BF16), jnp.concatenate(b, axis=0).astype(BF16)


def _w_in_rows(gathered):
    t = gathered[:, R_WIN:R_WIN + WIN_ROWS, :].reshape(N_IN, PACK_COLS)
    return jnp.concatenate([t[2312:5384], t[256:1792], t[1800:2312], t[0:256],
                            jnp.pad(t[1792:1800], ((0, NP - P_F - 8), (0, 0)))], axis=0)


def _w_in_grad_rows(grads, dwt):
    t = jnp.concatenate([dwt[P_A:P_A + 256], dwt[P_Q:P_Q + 1536], dwt[P_F:P_F + 8], dwt[P_C:P_C + 512], dwt[P_G:P_G + 3072]],
                        axis=0)
    return lax.dynamic_update_slice(grads, t.reshape(4, WIN_ROWS, PACK_COLS).astype(grads.dtype), (0, R_WIN, 0))


def _small_prep(sw, l):
    eye = jnp.eye(4, dtype=F32)
    bd = jnp.einsum('gh,gcd->gchd', eye, sw['pool_w'][l]).reshape(POOL_W, POOL_W).astype(BF16)
    tril = jnp.tril(jnp.ones((SGU_CHUNK, SGU_CHUNK), F32))
    wm = (sw['sgu_w'][l] * tril[None]).astype(BF16)
    return dict(
        g_mix=sw['norm_mix_g'][l][None], g_x=sw['norm_xattn_g'][l][None], g_mem=sw['norm_mem_g'][l][None],
        g_ffn=sw['norm_ffn_g'][l][None], bd=bd, pool_scale=sw['pool_scale'][l][None],
        bf=jnp.pad(sw['b_forget'][l], (0, FCOLS - 8))[None], sgu_g=sw['sgu_norm_g'][l][None], wm=wm,
        wmt=jnp.transpose(wm, (0, 2, 1)), sgu_bias=jnp.repeat(sw['sgu_b'][l].T, 64, axis=1), bg=sw['b_gate'][l][None])


def _rows4(r0):
    return dict(n=D, k=D, tn=D, b_block=(4, 256, PACK_COLS), b_index=lambda i, j, k: (0, r0 // 256, 0))


def _rows_t(r0):
    return dict(tb=True, n=D, k=D, tn=D, b_block=(4, 256, PACK_COLS), b_index=lambda i, j, k: (0, r0 // 256, 0))


def _rows_grad(r0):
    return dict(ta=True, tm=D, tn=512, o_block=(4, 256, 512), o_index=lambda i, j, k: (0, r0 // 256, j))


def _add_to(r, e):
    return e + r


def _after(v, token):
    return v if token is None else v + token[0, 0]


def _layer_fwd(x, mem, GA, w_in_t, sp, l, token, second):
    t = f"l{l}"
    S = x.shape[0]
    h = _rms_fwd(x, _after(sp['g_mix'], token), f"rms_mix_{t}")
    proj = _mm(h, w_in_t, name=f"proj_{t}", out_dtype=F32, tb=True)
    d, ya = _pool_fwd(proj, sp['bd'], sp['pool_scale'], f"pool_fwd_{t}")
    fcum = _fgate_fwd(proj, sp['bf'], f"fgate_fwd_{t}")
    f8 = fcum[:, :8]
    fcol = f8.reshape(S, 4, 2).transpose(1, 0, 2)
    frow = f8.T.reshape(4, 2, S)
    qkv = proj[:, P_Q:P_Q + 3 * FOX_W].astype(BF16)
    o, o32, lse = _fox_fwd(qkv, fcol, frow, f"fox_fwd_{t}")
    sg = _sgu_fwd(proj, sp['sgu_g'], sp['wm'], sp['sgu_bias'], f"sgu_fwd_{t}")
    merged = _merge_fwd(proj, ya, o, sg, GA, sp['bg'], f"merge_fwd_{t}")
    x1 = _mm(merged, GA, name=f"out_{t}", out_dtype=F32, extra=x, epi=_add_to, **_rows4(R_OUT))
    GB, token = second(x1)
    hx = _rms_fwd(x1, _after(sp['g_x'], token), f"rms_x_{t}")
    hm = _rms_fwd(mem, sp['g_mem'], f"rms_mem_{t}")
    xq = _mm(hx, GB, name=f"xq_{t}", out_dtype=BF16, **_rows4(R_XQ))
    kv = _mm(hm, GB, name=f"xkv_{t}", out_dtype=BF16, n=2 * D, k=D, tn=512, tk=512, b_block=(None, 512, 512),
             b_index=lambda i, j, k: (j, R_XKV // 512, k))
    o2 = _xattn_fwd(xq, kv, f"xattn_fwd_{t}")
    x2 = _mm(o2, GB, name=f"xo_{t}", out_dtype=F32, extra=x1, epi=_add_to, **_rows4(R_XO))
    hf = _rms_fwd(x2, sp['g_ffn'], f"rms_ffn_{t}")
    z = _mm(hf, GB, name=f"ff1_{t}", out_dtype=F32, n=D_FF, k=D, tn=D, b_block=(None, 1024, PACK_COLS),
            b_index=lambda i, j, k: (j, R_FF1 // 1024, 0))
    x3 = _mm(z, GB, name=f"ff2_{t}", out_dtype=F32, a_fn=_relu2, extra=x2, epi=_add_to, n=D, k=D_FF, tk=1024, tn=D,
             b_block=(None, 1024, PACK_COLS), b_index=lambda i, j, k: (k, R_FF2 // 1024, 0))
    saved = dict(x=x, h=h, proj=proj, d=d, ya=ya, fcol=fcol, frow=frow, qkv=qkv, o=o, o32=o32, lse=lse, sg=sg, merged=merged,
                 x1=x1, hx=hx, hm=hm, xq=xq, kv=kv, o2=o2, x2=x2, hf=hf, z=z, GA=GA, GB=GB, w_in_t=w_in_t)
    return x3, saved


def _layer_bwd(dx3, mem, sp, sv, l, token, early):
    t = f"l{l}"
    S = dx3.shape[0]
    GA, GB = sv['GA'], sv['GB']
    gs = {}
    dx3 = _after(dx3, token)
    gb = lax.empty((4, ROWS_B, PACK_COLS), BF16)
    dz = _mm(dx3, GB, name=f"d_a2_{t}", out_dtype=BF16, tb=True, n=D_FF, k=D, tn=D, b_block=(None, 1024, PACK_COLS),
             b_index=lambda i, j, k: (j, R_FF2 // 1024, 0), extra=sv['z'],
             epi=lambda r, e: r * (2.0 * jnp.maximum(e, 0.0)))
    gb = _mm(sv['z'], dx3, name=f"dw_ff2_{t}", out_dtype=BF16, ta=True, a_fn=_relu2, into=gb, tm=1024, tn=D,
             o_block=(None, 1024, PACK_COLS), o_index=lambda i, j, k: (i, R_FF2 // 1024, 0))
    gb = _mm(sv['hf'], dz, name=f"dw_ff1_{t}", out_dtype=BF16, ta=True, into=gb, tm=1024, tn=D,
             o_block=(None, 1024, PACK_COLS), o_index=lambda i, j, k: (j, R_FF1 // 1024, 0))
    dhf = _mm(dz, GB, name=f"d_hf_{t}", out_dtype=F32, tb=True, n=D, k=D_FF, tn=D, tk=1024, b_block=(None, 1024, PACK_COLS),
              b_index=lambda i, j, k: (k, R_FF1 // 1024, 0))
    dx2, gs['norm_ffn_g'] = _rms_bwd(dhf, sv['x2'], sp['g_ffn'], dx3, f"rms_ffn_bwd_{t}")
    do2 = _mm(dx2, GB, name=f"d_o2_{t}", out_dtype=BF16, **_rows_t(R_XO))
    gb = _mm(sv['o2'], dx2, name=f"dw_xo_{t}", out_dtype=BF16, into=gb, **_rows_grad(R_XO))
    dxq, dkv = _xattn_bwd(sv['xq'], sv['kv'], do2, f"xattn_bwd_{t}")
    gb = _mm(sv['hm'], dkv, name=f"dw_xkv_{t}", out_dtype=BF16, ta=True, into=gb, tm=512, tn=512,
             o_block=(None, 512, 512), o_index=lambda i, j, k: (j, R_XKV // 512, i))
    dhm = _mm(dkv, GB, name=f"d_hm_{t}", out_dtype=F32, tb=True, n=D, k=2 * D, tn=512, tk=512, b_block=(None, 512, 512),
              b_index=lambda i, j, k: (k, R_XKV // 512, j))
    gs['norm_mem_g'] = _rms_bwd(dhm, mem, sp['g_mem'], None, f"rms_mem_bwd_{t}")
    gb = _mm(sv['hx'], dxq, name=f"dw_xq_{t}", out_dtype=BF16, into=gb, **_rows_grad(R_XQ))
    token = early(gb)
    dhx = _mm(dxq, GB, name=f"d_hx_{t}", out_dtype=F32, **_rows_t(R_XQ))
    dx1, gs['norm_xattn_g'] = _rms_bwd(dhx, sv['x1'], _after(sp['g_x'], token), dx2, f"rms_x_bwd_{t}")
    ga = jnp.zeros((4, ROWS_A, PACK_COLS), BF16)
    ga = _mm(sv['merged'], dx1, name=f"dw_out_{t}", out_dtype=BF16, into=ga, **_rows_grad(R_OUT))
    dm = _mm(dx1, GA, name=f"d_merged_{t}", out_dtype=F32, **_rows_t(R_OUT))
    dg, dya, do, dsg, ga, gs['b_gate'] = _merge_bwd(dm, sv['proj'], sv['ya'], sv['o'], sv['sg'], GA, sp['bg'], ga, f"merge_bwd_{t}")
    dc, dws, dbias, gs['sgu_norm_g'] = _sgu_bwd(dsg, sv['proj'], sp['sgu_g'], sp['wm'], sp['wmt'], sp['sgu_bias'], f"sgu_bwd_{t}")
    tril = jnp.tril(jnp.ones((SGU_CHUNK, SGU_CHUNK), F32))
    gs['sgu_w'] = dws * tril[None]
    gs['sgu_b'] = dbias.reshape(SGU_CHUNK, 4, 64).sum(-1).T
    dq, dk, dv, dfrow, dfcol = _fox_bwd(sv['qkv'], sv['o32'], do, sv['lse'], sv['fcol'], sv['frow'], f"fox_bwd_{t}")
    dF = jnp.pad(dfrow.reshape(8, S).T + dfcol.transpose(1, 0, 2).reshape(S, 8), ((0, 0), (0, FCOLS - 8)))
    df, dbf = _fgate_bwd(dF, sv['proj'], sp['bf'], f"fgate_bwd_{t}")
    gs['b_forget'] = dbf[:, :8]
    da, dbd, gs['pool_scale'] = _pool_bwd(dya, sv['d'], sp['bd'], sp['pool_scale'], f"pool_bwd_{t}")
    gs['pool_w'] = jnp.stack([dbd[g * 64:(g + 1) * 64, g * 64:(g + 1) * 64] for g in range(4)])
    dproj = jnp.concatenate([dg, dq, dk, dv, dc, da, df], axis=1)
    dwt = _mm(dproj, sv['h'], name=f"dw_in_{t}", out_dtype=BF16, ta=True, tm=512, tn=1024)
    ga = _w_in_grad_rows(ga, dwt)
    dh = _mm(dproj, sv['w_in_t'], name=f"d_h_{t}", out_dtype=F32, tk=512, tn=D)
    dx, gs['norm_mix_g'] = _rms_bwd(dh, sv['x'], sp['g_mix'], dx1, f"rms_mix_bwd_{t}")
    return dx, ga, gs


SMALL_ROWS = 1424
GRAD_BLOCKS = {
    'w_ff1': ('b', lambda i: (R_FF1 // 256 + i, 0)), 'w_ff2': ('b', lambda i: (R_FF2 // 256 + i, 0)),
    'w_xq': ('b', lambda i: (R_XQ // 256 + i, 0)), 'w_xo': ('b', lambda i: (R_XO // 256 + i, 0)),
    'w_xkv': ('b', lambda i: (R_XKV // 256 + i % 2, i // 2)), 'w_out': ('a', lambda i: (R_OUT // 256 + i, 0)),
    'w_branch_a': ('a', lambda i: (R_BRANCH // 256, 0)), 'w_branch_b': ('a', lambda i: (R_BRANCH // 256, 1 + i)),
    'w_branch_c': ('a', lambda i: (R_BRANCH // 256, 3)),
}


def _pack_small(parts):
    flat = jnp.concatenate([p.reshape(-1) for p in parts])
    return jnp.pad(flat, (0, SMALL_ROWS * 128 - flat.shape[0])).reshape(SMALL_ROWS, 128)


def _unpack_small(buf, shapes):
    flat, out, r = buf.reshape(-1), [], 0
    for s in shapes:
        n = math.prod(s)
        out.append(flat[r:r + n].reshape(s))
        r += n
    return out


def kernel(x, mem, norm_mix_g, w_in, b_forget, pool_w, pool_scale, sgu_norm_g, sgu_w, sgu_b, w_branch_a, w_branch_b, w_branch_c, b_gate, w_out, norm_xattn_g, norm_mem_g, w_xq, w_xkv, w_xo, norm_ffn_g, w_ff1, w_ff2, final_norm_g, loss_target, m_norm_mix_g, m_w_in, m_b_forget, m_pool_w, m_pool_scale, m_sgu_norm_g, m_sgu_w, m_sgu_b, m_w_branch_a, m_w_branch_b, m_w_branch_c, m_b_gate, m_w_out, m_norm_xattn_g, m_norm_mem_g, m_w_xq, m_w_xkv, m_w_xo, m_norm_ffn_g, m_w_ff1, m_w_ff2, m_final_norm_g, v_norm_mix_g, v_w_in, v_b_forget, v_pool_w, v_pool_scale, v_sgu_norm_g, v_sgu_w, v_sgu_b, v_w_branch_a, v_w_branch_b, v_w_branch_c, v_b_gate, v_w_out, v_norm_xattn_g, v_norm_mem_g, v_w_xq, v_w_xkv, v_w_xo, v_norm_ffn_g, v_w_ff1, v_w_ff2, v_final_norm_g):
    args = (norm_mix_g, w_in, b_forget, pool_w, pool_scale, sgu_norm_g, sgu_w, sgu_b, w_branch_a, w_branch_b, w_branch_c, b_gate,
            w_out, norm_xattn_g, norm_mem_g, w_xq, w_xkv, w_xo, norm_ffn_g, w_ff1, w_ff2, final_norm_g)
    margs = (m_norm_mix_g, m_w_in, m_b_forget, m_pool_w, m_pool_scale, m_sgu_norm_g, m_sgu_w, m_sgu_b, m_w_branch_a, m_w_branch_b,
             m_w_branch_c, m_b_gate, m_w_out, m_norm_xattn_g, m_norm_mem_g, m_w_xq, m_w_xkv, m_w_xo, m_norm_ffn_g, m_w_ff1, m_w_ff2,
             m_final_norm_g)
    vargs = (v_norm_mix_g, v_w_in, v_b_forget, v_pool_w, v_pool_scale, v_sgu_norm_g, v_sgu_w, v_sgu_b, v_w_branch_a, v_w_branch_b,
             v_w_branch_c, v_b_gate, v_w_out, v_norm_xattn_g, v_norm_mem_g, v_w_xq, v_w_xkv, v_w_xo, v_norm_ffn_g, v_w_ff1, v_w_ff2,
             v_final_norm_g)
    w = dict(zip(W_NAMES, args))
    mo = dict(zip(W_NAMES, margs))
    vo = dict(zip(W_NAMES, vargs))
    xs, mems, tgt = x[0], mem[0], loss_target[0]
    shards = [_pack_shard(w, l) for l in range(DEPTH)]
    preps = [_small_prep(w, l) for l in range(DEPTH)]

    first_a, _ = _gather_begin(shards[0][0], "a_l0")
    pending_b, token = _gather_begin(shards[0][1], "b_l0")
    GA = None
    act, saved = xs, []
    for l in range(DEPTH):
        nxt = {}
        if l + 1 < DEPTH:
            nxt['a'], ta = _gather_begin(shards[l + 1][0], f"a_l{l + 1}")
            token = ta if token is None else token + ta
        if l == 0:
            GA = _gather_end(first_a, shards[DEPTH - 1][1])

        def second(x1, l=l, pending_b=pending_b, nxt=nxt):
            GB = _gather_end(pending_b, x1)
            if l + 1 == DEPTH:
                return GB, None
            nxt['b'], tb = _gather_begin(shards[l + 1][1], f"b_l{l + 1}")
            return GB, tb

        act, sv = _layer_fwd(act, mems, GA, _w_in_rows(GA), preps[l], l, token, second)
        saved.append(sv)
        if l + 1 < DEPTH:
            GA = _gather_end(nxt['a'], act)
            pending_b, token = nxt['b'], None
    loss_part, dact, d_final_g = _loss_head(act, w['final_norm_g'][None], tgt, "loss_head")

    red_a, red_b, small_g = [None] * DEPTH, [None] * DEPTH, [None] * DEPTH
    token, state_a = None, None
    for l in reversed(range(DEPTH)):
        early = {}

        def start_b(gb, l=l, early=early):
            early['state'], tok = _reduce_begin(gb, f"b_l{l}")
            return tok

        dact, ga, small_g[l] = _layer_bwd(dact, mems, preps[l], saved[l], l, token, start_b)
        if state_a is not None:
            red_a[l + 1] = _reduce_end(state_a, dact)
        red_b[l] = _reduce_end(early['state'], dact)
        state_a, token = _reduce_begin(ga, f"a_l{l}")
    grad_x = dact[None]
    per_layer = [n for n in SMALL_NAMES if n != 'final_norm_g']
    small_shapes = [w[n].shape for n in per_layer] + [(D,), (1,)]
    parts = [jnp.stack([small_g[l][n].reshape(w[n].shape[1:]) for l in range(DEPTH)]) for n in per_layer]
    state_small, token_small = _all_reduce_begin(_pack_small(parts + [d_final_g.reshape(D), loss_part.reshape(1)]), "small")
    token = token + token_small

    grads, delta, new_m, new_v = {}, {}, {}, {}
    for n, (buf, g_index) in GRAD_BLOCKS.items():
        if buf == 'b':
            grads[n], delta[n], new_m[n], new_v[n] = _adamw_packed(red_b, w[n], mo[n], vo[n], g_index, f"adamw_{n}", token)
    red_a[0] = _reduce_end(state_a, new_v['w_xkv'])
    small_red = _unpack_small(_all_reduce_end(state_small, red_a[0]), small_shapes)
    grads.update(zip(per_layer + ['final_norm_g'], small_red[:-1]))
    loss = small_red[-1].reshape(())
    for n, (buf, g_index) in GRAD_BLOCKS.items():
        if buf == 'a':
            grads[n], delta[n], new_m[n], new_v[n] = _adamw_packed(red_a, w[n], mo[n], vo[n], g_index, f"adamw_{n}", token)
    g_t = jnp.stack([r[R_WIN:R_WIN + WIN_ROWS] for r in red_a], axis=1)
    upd = _adamw(g_t, _w_in_t(w['w_in']), _w_in_t(mo['w_in']), _w_in_t(vo['w_in']), "adamw_w_in", block=(WIN_ROWS, DEPTH, 128))
    grads['w_in'], delta['w_in'], new_m['w_in'], new_v['w_in'] = [jnp.transpose(a, (1, 2, 0)) for a in (g_t,) + tuple(upd)]
    small_all = per_layer + ['final_norm_g']
    shapes_all = [w[n].shape for n in small_all]
    packed = [_pack_small([d[n] for n in small_all])[None] for d in (grads, w, mo, vo)]
    ds, ms, vs = _adamw(*packed, "adamw_small")
    for n, a, b, c in zip(small_all, _unpack_small(ds[0], shapes_all), _unpack_small(ms[0], shapes_all), _unpack_small(vs[0], shapes_all)):
        delta[n], new_m[n], new_v[n] = a, b, c

    return (loss, grad_x, *[grads[n] for n in W_NAMES], *[delta[n] for n in W_NAMES], *[new_m[n] for n in W_NAMES],
            *[new_v[n] for n in W_NAMES])
```

```python
import math

import jax
import jax.numpy as jnp
from jax import lax
from jax.experimental import pallas as pl
from jax.experimental.pallas import tpu as pltpu

F32 = jnp.float32
BF16 = jnp.bfloat16

D = 1024
DEPTH = 2
POOL_W = 256
FOX_W = 512
SGU_W = 256
SGU_CHUNK = 128
N_IN = 5384
P_G, P_Q, P_K, P_V, P_C, P_A, P_F = 0, 3072, 3584, 4096, 4608, 5120, 5376
NP = 5632
XH, XHD = 4, 256
D_FF = 4096
EPS = 1e-6
NEG = -1e30
FOX_SCALE = 64 ** -0.5
X_SCALE = 256 ** -0.5
GELU_K = math.sqrt(2.0 / math.pi)
GELU_C = 0.044715

ADAM_LR, ADAM_B1, ADAM_B2, ADAM_EPS, ADAM_WD, ADAM_STEP = 0.001, 0.9, 0.999, 1e-08, 0.01, 10

VMEM_LIMIT = 48 * 1024 * 1024
MESH = pl.DeviceIdType.MESH

IN_NAMES = ['x', 'mem', 'norm_mix_g', 'w_in', 'b_forget', 'pool_w', 'pool_scale', 'sgu_norm_g', 'sgu_w', 'sgu_b',
            'w_branch_a', 'w_branch_b', 'w_branch_c', 'b_gate', 'w_out', 'norm_xattn_g', 'norm_mem_g', 'w_xq',
            'w_xkv', 'w_xo', 'norm_ffn_g', 'w_ff1', 'w_ff2', 'final_norm_g']
W_NAMES = IN_NAMES[2:]
BIG_NAMES = ['w_in', 'w_branch_a', 'w_branch_b', 'w_branch_c', 'w_out', 'w_xq', 'w_xkv', 'w_xo', 'w_ff1', 'w_ff2']
SMALL_NAMES = [n for n in W_NAMES if n not in BIG_NAMES]
PACK_COLS = 1024


ANY = pl.BlockSpec(memory_space=pl.ANY)


def _cp(sem=None):
    return pltpu.CompilerParams(dimension_semantics=sem, vmem_limit_bytes=VMEM_LIMIT)


def _mm(a, b, *, name, out_dtype, ta=False, tb=False, tm=1024, tn=512, tk=1024, a_fn=None, extra=None, epi=None,
        n=None, k=None, b_block=None, b_index=None, into=None, o_block=None, o_index=None):
    M = a.shape[1] if ta else a.shape[0]
    K = k if k is not None else (a.shape[0] if ta else a.shape[1])
    N = n if n is not None else (b.shape[0] if tb else b.shape[1])
    tm, tn, tk = min(tm, M), min(tn, N), min(tk, K)
    assert M % tm == 0 and N % tn == 0 and K % tk == 0, (name, M, N, K)
    nk = K // tk
    a_spec = pl.BlockSpec((tk, tm), lambda i, j, k: (k, i)) if ta else pl.BlockSpec((tm, tk), lambda i, j, k: (i, k))
    if b_block is not None:
        b_spec = pl.BlockSpec(b_block, b_index)
    else:
        b_spec = pl.BlockSpec((tn, tk), lambda i, j, k: (j, k)) if tb else pl.BlockSpec((tk, tn), lambda i, j, k: (k, j))
    dn = (((0 if ta else 1,), (1 if tb else 0,)), ((), ()))
    tile = pl.BlockSpec((tm, tn), lambda i, j, k: (i, j))
    o_spec = pl.BlockSpec(o_block, o_index) if into is not None else tile
    in_specs = [a_spec, b_spec] + ([tile] if extra is not None else []) + ([ANY] if into is not None else [])
    n_in = len(in_specs)

    def body(*refs):
        a_ref, b_ref = refs[0], refs[1]
        e_ref = refs[2] if extra is not None else None
        o_ref, acc_ref = refs[n_in], refs[n_in + 1]
        kk = pl.program_id(2)

        @pl.when(kk == 0)
        def _():
            acc_ref[...] = jnp.zeros_like(acc_ref)

        av = a_ref[...]
        if a_fn is not None:
            av = a_fn(av)
        bv = b_ref[...]
        if bv.ndim == 3:
            bv = bv.reshape(-1, bv.shape[-1])
        acc_ref[...] += lax.dot_general(av.astype(BF16), bv.astype(BF16), dn, preferred_element_type=F32)

        @pl.when(kk == nk - 1)
        def _():
            r = acc_ref[...]
            if epi is not None:
                r = epi(r, e_ref[...])
            o_ref[...] = r.astype(o_ref.dtype).reshape(o_ref.shape)

    args = (a, b) + ((extra,) if extra is not None else ()) + ((into,) if into is not None else ())
    out_shape = jax.ShapeDtypeStruct(into.shape, into.dtype) if into is not None else jax.ShapeDtypeStruct((M, N), out_dtype)
    return pl.pallas_call(
        body, out_shape=out_shape, grid=(M // tm, N // tn, nk), in_specs=in_specs, out_specs=o_spec,
        scratch_shapes=[pltpu.VMEM((tm, tn), F32)], input_output_aliases={n_in - 1: 0} if into is not None else {},
        compiler_params=_cp(("parallel", "parallel", "arbitrary")), name=name)(*args)


def _relu2(z):
    r = jnp.maximum(z, 0.0)
    return r * r


def _rms_fwd(x, g, name, tr=512):
    R, n = x.shape
    tr = min(tr, R)

    def body(x_ref, g_ref, h_ref):
        xv = x_ref[...]
        rstd = lax.rsqrt(jnp.mean(xv * xv, axis=-1, keepdims=True) + EPS)
        h_ref[...] = (xv * rstd * g_ref[...]).astype(BF16)

    return pl.pallas_call(
        body, out_shape=jax.ShapeDtypeStruct((R, n), BF16), grid=(R // tr,),
        in_specs=[pl.BlockSpec((tr, n), lambda i: (i, 0)), pl.BlockSpec((1, n), lambda i: (0, 0))],
        out_specs=pl.BlockSpec((tr, n), lambda i: (i, 0)), compiler_params=_cp(("parallel",)), name=name)(x, g)


def _rms_bwd(dh, x, g, dres, name, tr=512):
    R, n = x.shape
    tr = min(tr, R)
    need_dx = dres is not None

    def body(*refs):
        if need_dx:
            dh_ref, x_ref, g_ref, r_ref, dx_ref, dg_ref = refs
        else:
            dh_ref, x_ref, g_ref, dg_ref = refs
        i = pl.program_id(0)
        xv = x_ref[...]
        dhv = dh_ref[...].astype(F32)
        rstd = lax.rsqrt(jnp.mean(xv * xv, axis=-1, keepdims=True) + EPS)
        xhat = xv * rstd

        @pl.when(i == 0)
        def _():
            dg_ref[...] = jnp.zeros_like(dg_ref)

        dg_ref[...] += jnp.sum(dhv * xhat, axis=0, keepdims=True)
        if need_dx:
            t = dhv * g_ref[...]
            dx_ref[...] = r_ref[...] + rstd * (t - xhat * jnp.mean(t * xhat, axis=-1, keepdims=True))

    row = pl.BlockSpec((tr, n), lambda i: (i, 0))
    vec = pl.BlockSpec((1, n), lambda i: (0, 0))
    if need_dx:
        return pl.pallas_call(
            body, out_shape=(jax.ShapeDtypeStruct((R, n), F32), jax.ShapeDtypeStruct((1, n), F32)), grid=(R // tr,),
            in_specs=[row, row, vec, row], out_specs=(row, vec), compiler_params=_cp(("arbitrary",)), name=name)(dh, x, g, dres)
    return pl.pallas_call(
        body, out_shape=jax.ShapeDtypeStruct((1, n), F32), grid=(R // tr,),
        in_specs=[row, row, vec], out_specs=vec, compiler_params=_cp(("arbitrary",)), name=name)(dh, x, g)


def _loss_head(x, g, tgt, name, tr=512):
    R, n = x.shape

    def body(x_ref, g_ref, t_ref, loss_ref, dx_ref, dg_ref):
        i = pl.program_id(0)
        xv = x_ref[...]
        gv = g_ref[...]
        rstd = lax.rsqrt(jnp.mean(xv * xv, axis=-1, keepdims=True) + EPS)
        xhat = xv * rstd
        e = xhat * gv - t_ref[...]

        @pl.when(i == 0)
        def _():
            loss_ref[...] = jnp.zeros_like(loss_ref)
            dg_ref[...] = jnp.zeros_like(dg_ref)

        loss_ref[...] += 0.5 * jnp.sum(jnp.sum(e * e, axis=-1, keepdims=True) / n, axis=0, keepdims=True)
        dy = e / n
        dg_ref[...] += jnp.sum(dy * xhat, axis=0, keepdims=True)
        t = dy * gv
        dx_ref[...] = rstd * (t - xhat * jnp.mean(t * xhat, axis=-1, keepdims=True))

    row = pl.BlockSpec((tr, n), lambda i: (i, 0))
    vec = pl.BlockSpec((1, n), lambda i: (0, 0))
    one = pl.BlockSpec((1, 1), lambda i: (0, 0))
    return pl.pallas_call(
        body, out_shape=(jax.ShapeDtypeStruct((1, 1), F32), jax.ShapeDtypeStruct((R, n), F32), jax.ShapeDtypeStruct((1, n), F32)),
        grid=(R // tr,), in_specs=[row, vec, row], out_specs=(one, row, vec),
        compiler_params=_cp(("arbitrary",)), name=name)(x, g, tgt)


def _pool_masks(S):
    row = lax.broadcasted_iota(jnp.int32, (S, POOL_W), 0)
    grp = lax.broadcasted_iota(jnp.int32, (S, POOL_W), 1) // 64
    win = jnp.where(grp == 0, 2, jnp.where(grp == 1, 4, jnp.where(grp == 2, 8, 16)))
    cnt = jnp.minimum(row + 1, win).astype(F32)
    return row, grp, cnt


def _by_group(grp, v0, v1, v2, v3):
    return jnp.where(grp == 0, v0, jnp.where(grp == 1, v1, jnp.where(grp == 2, v2, v3)))


def _pool_fwd(proj, bd, scale, name):
    S = proj.shape[0]

    def body(a_ref, bd_ref, sc_ref, d_ref, y_ref):
        a = a_ref[...]
        row, grp, cnt = _pool_masks(S)

        def back(v, k):
            return jnp.where(row >= k, pltpu.roll(v, k, 0), 0.0)

        s1 = a + back(a, 1)
        s2 = s1 + back(s1, 2)
        s3 = s2 + back(s2, 4)
        s4 = s3 + back(s3, 8)
        d = (_by_group(grp, s1, s2, s3, s4) / cnt - a).astype(BF16)
        d_ref[...] = d
        y_ref[...] = (jnp.dot(d, bd_ref[...], preferred_element_type=F32) * sc_ref[...]).astype(BF16)

    full = lambda r, c: pl.BlockSpec((r, c), lambda i: (0, 0))
    return pl.pallas_call(
        body, out_shape=(jax.ShapeDtypeStruct((S, POOL_W), BF16), jax.ShapeDtypeStruct((S, POOL_W), BF16)), grid=(1,),
        in_specs=[pl.BlockSpec((S, POOL_W), lambda i: (0, P_A // POOL_W)), full(POOL_W, POOL_W), full(1, POOL_W)],
        out_specs=(full(S, POOL_W), full(S, POOL_W)), compiler_params=_cp(("arbitrary",)), name=name)(proj, bd, scale)


def _pool_bwd(dya, d, bd, scale, name):
    S = dya.shape[0]

    def body(dy_ref, d_ref, bd_ref, sc_ref, da_ref, dbd_ref, dsc_ref):
        dy = dy_ref[...]
        dv = d_ref[...]
        bdv = bd_ref[...]
        row, grp, cnt = _pool_masks(S)
        yraw = jnp.dot(dv, bdv, preferred_element_type=F32)
        dsc_ref[...] = jnp.sum(dy * yraw, axis=0, keepdims=True)
        tb = (dy * sc_ref[...]).astype(BF16)
        dbd_ref[...] = lax.dot_general(dv, tb, (((0,), (0,)), ((), ())), preferred_element_type=F32)
        dd = lax.dot_general(tb, bdv, (((1,), (1,)), ((), ())), preferred_element_type=F32)
        e = dd / cnt

        def fwd(v, k):
            return jnp.where(row < S - k, pltpu.roll(v, S - k, 0), 0.0)

        r1 = e + fwd(e, 1)
        r2 = r1 + fwd(r1, 2)
        r3 = r2 + fwd(r2, 4)
        r4 = r3 + fwd(r3, 8)
        da_ref[...] = (_by_group(grp, r1, r2, r3, r4) - dd).astype(BF16)

    full = lambda r, c: pl.BlockSpec((r, c), lambda i: (0, 0))
    return pl.pallas_call(
        body, out_shape=(jax.ShapeDtypeStruct((S, POOL_W), BF16), jax.ShapeDtypeStruct((POOL_W, POOL_W), F32),
                         jax.ShapeDtypeStruct((1, POOL_W), F32)), grid=(1,),
        in_specs=[full(S, POOL_W), full(S, POOL_W), full(POOL_W, POOL_W), full(1, POOL_W)],
        out_specs=(full(S, POOL_W), full(POOL_W, POOL_W), full(1, POOL_W)),
        compiler_params=_cp(("arbitrary",)), name=name)(dya, d, bd, scale)


FCOLS = 128


def _log_sigmoid(z):
    return -(jnp.maximum(-z, 0.0) + jnp.log1p(jnp.exp(-jnp.abs(z))))


def _fgate_fwd(proj, bf, name):
    S = proj.shape[0]

    def body(f_ref, b_ref, o_ref):
        v = _log_sigmoid(f_ref[...] + b_ref[...])
        row = lax.broadcasted_iota(jnp.int32, (S, FCOLS), 0)
        k = 1
        while k < S:
            v = v + jnp.where(row >= k, pltpu.roll(v, k, 0), 0.0)
            k *= 2
        o_ref[...] = v

    return pl.pallas_call(
        body, out_shape=jax.ShapeDtypeStruct((S, FCOLS), F32), grid=(1,),
        in_specs=[pl.BlockSpec((S, FCOLS), lambda i: (0, P_F // FCOLS)), pl.BlockSpec((1, FCOLS), lambda i: (0, 0))],
        out_specs=pl.BlockSpec((S, FCOLS), lambda i: (0, 0)), compiler_params=_cp(("arbitrary",)), name=name)(proj, bf)


def _fgate_bwd(dF, proj, bf, name):
    S = proj.shape[0]

    def body(dF_ref, f_ref, b_ref, df_ref, db_ref):
        v = dF_ref[...]
        row = lax.broadcasted_iota(jnp.int32, (S, FCOLS), 0)
        k = 1
        while k < S:
            v = v + jnp.where(row < S - k, pltpu.roll(v, S - k, 0), 0.0)
            k *= 2
        z = f_ref[...] + b_ref[...]
        df = v * (1.0 / (1.0 + jnp.exp(z)))
        db_ref[...] = jnp.sum(df, axis=0, keepdims=True)
        df_ref[...] = jnp.concatenate([df, jnp.zeros_like(df)], axis=1).astype(BF16)

    return pl.pallas_call(
        body, out_shape=(jax.ShapeDtypeStruct((S, 2 * FCOLS), BF16), jax.ShapeDtypeStruct((1, FCOLS), F32)), grid=(1,),
        in_specs=[pl.BlockSpec((S, FCOLS), lambda i: (0, 0)), pl.BlockSpec((S, FCOLS), lambda i: (0, P_F // FCOLS)),
                  pl.BlockSpec((1, FCOLS), lambda i: (0, 0))],
        out_specs=(pl.BlockSpec((S, 2 * FCOLS), lambda i: (0, 0)), pl.BlockSpec((1, FCOLS), lambda i: (0, 0))),
        compiler_params=_cp(("arbitrary",)), name=name)(dF, proj, bf)


def _fox_scores(qe, kj, fq, fk, r0, c0, tq, tk, diagonal):
    s = lax.dot_general(qe, kj, (((1,), (1,)), ((), ())), preferred_element_type=F32) * FOX_SCALE
    s = s + (fq - fk)
    if not diagonal:
        return s
    rows = r0 + lax.broadcasted_iota(jnp.int32, (tq, tk), 0)
    cols = c0 + lax.broadcasted_iota(jnp.int32, (tq, tk), 1)
    return jnp.where(rows >= cols, s, NEG)


FOX_TQ, FOX_TK = 512, 512


def _fox_fwd(qkv, fcol, frow, name):
    S = qkv.shape[0]
    tq, tk = FOX_TQ, min(FOX_TK, S)

    def body(q_ref, k_ref, v_ref, fc_ref, fr_ref, o_ref, o32_ref, lse_ref):
        i = pl.program_id(1)
        r0 = i * tq
        q = q_ref[...]
        half = lax.broadcasted_iota(jnp.int32, (tq, 128), 1) // 64
        qs = [jnp.where(half == e, q, jnp.zeros_like(q)) for e in (0, 1)]
        fqs = [fc_ref[0, :, e:e + 1] for e in (0, 1)]

        def step(j, carry, diagonal=False):
            c0 = pl.multiple_of(j * tk, tk)
            kj = k_ref[pl.ds(c0, tk), :]
            vj = v_ref[pl.ds(c0, tk), :]
            out = []
            for e in (0, 1):
                m, l, acc = carry[e]
                s = _fox_scores(qs[e], kj, fqs[e], fr_ref[0, e:e + 1, pl.ds(c0, tk)], r0, c0, tq, tk, diagonal)
                m_new = jnp.maximum(m, jnp.max(s, axis=-1, keepdims=True))
                alpha = jnp.exp(m - m_new)
                p = jnp.exp(s - m_new)
                out.append((m_new, alpha * l + jnp.sum(p, axis=-1, keepdims=True),
                            alpha * acc + jnp.dot(p.astype(BF16), vj, preferred_element_type=F32)))
            return tuple(out)

        init = (jnp.full((tq, 1), NEG, F32), jnp.zeros((tq, 1), F32), jnp.zeros((tq, 128), F32))
        below = r0 // tk
        carry = lax.fori_loop(0, below, step, (init, init))
        carry = step(below, carry, diagonal=True)
        outs = []
        for e in (0, 1):
            m, l, acc = carry[e]
            outs.append(acc / l)
            lse_ref[0, :, e:e + 1] = m + jnp.log(l)
        o = jnp.where(half == 0, outs[0], outs[1])
        o32_ref[...] = o
        o_ref[...] = o.astype(BF16)

    tile = pl.BlockSpec((tq, 128), lambda h, i: (i, h))
    return pl.pallas_call(
        body, out_shape=(jax.ShapeDtypeStruct((S, FOX_W), BF16), jax.ShapeDtypeStruct((S, FOX_W), F32),
                         jax.ShapeDtypeStruct((4, S, 2), F32)), grid=(4, S // tq),
        in_specs=[tile, pl.BlockSpec((S, 128), lambda h, i: (0, 4 + h)), pl.BlockSpec((S, 128), lambda h, i: (0, 8 + h)),
                  pl.BlockSpec((1, tq, 2), lambda h, i: (h, i, 0)), pl.BlockSpec((1, 2, S), lambda h, i: (h, 0, 0))],
        out_specs=(tile, tile, pl.BlockSpec((1, tq, 2), lambda h, i: (h, i, 0))),
        compiler_params=_cp(("parallel", "parallel")), name=name)(qkv, qkv, qkv, fcol, frow)


def _fox_bwd(qkv, o32, do, lse, fcol, frow, name):
    S = qkv.shape[0]
    tq, tk = FOX_TQ, min(FOX_TK, S)
    nq = S // tq

    def body(q_ref, k_ref, v_ref, o_ref, do_ref, lse_ref, fc_ref, fr_ref, dq_ref, dk_ref, dv_ref, dfr_ref, dfc_ref, dk_acc, dv_acc):
        dk_acc[...] = jnp.zeros_like(dk_acc)
        dv_acc[...] = jnp.zeros_like(dv_acc)
        dfr_ref[...] = jnp.zeros_like(dfr_ref)
        half = lax.broadcasted_iota(jnp.int32, (tq, 128), 1) // 64

        def q_block(i, _):
            r0 = pl.multiple_of(i * tq, tq)
            qi = q_ref[pl.ds(r0, tq), :]
            dob = do_ref[pl.ds(r0, tq), :].astype(BF16)
            row_dot = dob.astype(F32) * o_ref[pl.ds(r0, tq), :]
            qs = [jnp.where(half == e, qi, jnp.zeros_like(qi)) for e in (0, 1)]
            dos = [jnp.where(half == e, dob, jnp.zeros_like(dob)) for e in (0, 1)]
            deltas = [jnp.sum(jnp.where(half == e, row_dot, 0.0), axis=-1, keepdims=True) for e in (0, 1)]
            lses = [lse_ref[0, pl.ds(r0, tq), e:e + 1] for e in (0, 1)]
            fqs = [fc_ref[0, pl.ds(r0, tq), e:e + 1] for e in (0, 1)]

            def step(j, carry, diagonal=False):
                dqs, row_sums = carry
                c0 = pl.multiple_of(j * tk, tk)
                kj = k_ref[pl.ds(c0, tk), :]
                vj = v_ref[pl.ds(c0, tk), :]
                new_dq, new_rows, dkc, dvc = [], [], [], []
                for e in (0, 1):
                    s = _fox_scores(qs[e], kj, fqs[e], fr_ref[0, e:e + 1, pl.ds(c0, tk)], r0, c0, tq, tk, diagonal)
                    p = jnp.exp(s - lses[e])
                    dp = lax.dot_general(dos[e], vj, (((1,), (1,)), ((), ())), preferred_element_type=F32)
                    ds = p * (dp - deltas[e])
                    dfr_ref[0, e:e + 1, pl.ds(c0, tk)] -= jnp.sum(ds, axis=0, keepdims=True)
                    new_rows.append(row_sums[e] + jnp.sum(ds, axis=-1, keepdims=True))
                    dsb = (ds * FOX_SCALE).astype(BF16)
                    dkc.append(lax.dot_general(dsb, qi, (((0,), (0,)), ((), ())), preferred_element_type=F32))
                    dvc.append(lax.dot_general(p.astype(BF16), dob, (((0,), (0,)), ((), ())), preferred_element_type=F32))
                    new_dq.append(dqs[e] + jnp.dot(dsb, kj, preferred_element_type=F32))
                half_k = lax.broadcasted_iota(jnp.int32, (tk, 128), 1) // 64
                dk_acc[pl.ds(c0, tk), :] += jnp.where(half_k == 0, dkc[0], dkc[1])
                dv_acc[pl.ds(c0, tk), :] += jnp.where(half_k == 0, dvc[0], dvc[1])
                return tuple(new_dq), tuple(new_rows)

            zero, zero_col = jnp.zeros((tq, 128), F32), jnp.zeros((tq, 1), F32)
            below = r0 // tk
            carry = lax.fori_loop(0, below, step, ((zero, zero), (zero_col, zero_col)))
            dqs, row_sums = step(below, carry, diagonal=True)
            for e in (0, 1):
                dfc_ref[0, pl.ds(r0, tq), e:e + 1] = row_sums[e]
            dq_ref[pl.ds(r0, tq), :] = jnp.where(half == 0, dqs[0], dqs[1]).astype(BF16)
            return 0

        lax.fori_loop(0, nq, q_block, 0)
        dk_ref[...] = dk_acc[...].astype(BF16)
        dv_ref[...] = dv_acc[...].astype(BF16)

    col = lambda off: pl.BlockSpec((S, 128), lambda h: (0, off + h))
    hs2 = pl.BlockSpec((1, S, 2), lambda h: (h, 0, 0))
    h2s = pl.BlockSpec((1, 2, S), lambda h: (h, 0, 0))
    return pl.pallas_call(
        body, out_shape=(jax.ShapeDtypeStruct((S, FOX_W), BF16),) * 3 + (jax.ShapeDtypeStruct((4, 2, S), F32),
                                                                         jax.ShapeDtypeStruct((4, S, 2), F32)), grid=(4,),
        in_specs=[col(0), col(4), col(8), col(0), col(0), hs2, hs2, h2s],
        out_specs=(col(0), col(0), col(0), h2s, hs2),
        scratch_shapes=[pltpu.VMEM((S, 128), F32), pltpu.VMEM((S, 128), F32)],
        compiler_params=_cp(("parallel",)), name=name)(qkv, qkv, qkv, o32, do, lse, fcol, frow)


def _gelu(x):
    return 0.5 * x * (1.0 + jnp.tanh(GELU_K * (x + GELU_C * x * x * x)))


def _gelu_grad(x):
    th = jnp.tanh(GELU_K * (x + GELU_C * x * x * x))
    return 0.5 * (1.0 + th) + 0.5 * x * (1.0 - th * th) * GELU_K * (1.0 + 3.0 * GELU_C * x * x)


def _sgu_parts(c, gn, w_ref, bias):
    zc = _gelu(c)
    u, vv = zc[:, :SGU_W], zc[:, SGU_W:]
    rstd = lax.rsqrt(jnp.mean(vv * vv, axis=-1, keepdims=True) + EPS)
    vhat = vv * rstd
    vnb = (vhat * gn).astype(BF16)
    grp = lax.broadcasted_iota(jnp.int32, (SGU_CHUNK, SGU_W), 1) // 64
    mixed = bias
    for gi in range(4):
        mixed = mixed + jnp.where(grp == gi, jnp.dot(w_ref[gi], vnb, preferred_element_type=F32), 0.0)
    return u, rstd, vhat, vnb, grp, mixed


def _sgu_fwd(proj, gn, wm, bias, name):
    S = proj.shape[0]

    def body(c_ref, g_ref, w_ref, b_ref, o_ref):
        u, _, _, _, _, mixed = _sgu_parts(c_ref[...], g_ref[...], w_ref, b_ref[...])
        o_ref[...] = (u * mixed).astype(BF16)

    return pl.pallas_call(
        body, out_shape=jax.ShapeDtypeStruct((S, SGU_W), BF16), grid=(S // SGU_CHUNK,),
        in_specs=[pl.BlockSpec((SGU_CHUNK, 2 * SGU_W), lambda i: (i, P_C // (2 * SGU_W))),
                  pl.BlockSpec((1, SGU_W), lambda i: (0, 0)), pl.BlockSpec((4, SGU_CHUNK, SGU_CHUNK), lambda i: (0, 0, 0)),
                  pl.BlockSpec((SGU_CHUNK, SGU_W), lambda i: (0, 0))],
        out_specs=pl.BlockSpec((SGU_CHUNK, SGU_W), lambda i: (i, 0)),
        compiler_params=_cp(("parallel",)), name=name)(proj, gn, wm, bias)


def _sgu_bwd(dsg, proj, gn, wm, wmt, bias, name):
    S = proj.shape[0]

    def body(dsg_ref, c_ref, g_ref, w_ref, wt_ref, b_ref, dc_ref, dw_ref, db_ref, dg_ref):
        i = pl.program_id(0)

        @pl.when(i == 0)
        def _():
            dw_ref[...] = jnp.zeros_like(dw_ref)
            db_ref[...] = jnp.zeros_like(db_ref)
            dg_ref[...] = jnp.zeros_like(dg_ref)

        c = c_ref[...]
        gn_v = g_ref[...]
        u, rstd, vhat, vnb, grp, mixed = _sgu_parts(c, gn_v, w_ref, b_ref[...])
        dsg_v = dsg_ref[...]
        du = dsg_v * mixed
        dmix = dsg_v * u
        db_ref[...] += dmix
        dmb = dmix.astype(BF16)
        dvn = jnp.zeros((SGU_CHUNK, SGU_W), F32)
        for gi in range(4):
            dmg = jnp.where(grp == gi, dmb, jnp.zeros_like(dmb))
            dw_ref[gi] += lax.dot_general(dmg, vnb, (((1,), (1,)), ((), ())), preferred_element_type=F32)
            dvn = dvn + jnp.where(grp == gi, jnp.dot(wt_ref[gi], dmb, preferred_element_type=F32), 0.0)
        dg_ref[...] += jnp.sum(dvn * vhat, axis=0, keepdims=True)
        t = dvn * gn_v
        dvv = rstd * (t - vhat * jnp.mean(t * vhat, axis=-1, keepdims=True))
        dc_ref[...] = (jnp.concatenate([du, dvv], axis=1) * _gelu_grad(c)).astype(BF16)

    w_spec = pl.BlockSpec((4, SGU_CHUNK, SGU_CHUNK), lambda i: (0, 0, 0))
    tile = pl.BlockSpec((SGU_CHUNK, SGU_W), lambda i: (0, 0))
    vec = pl.BlockSpec((1, SGU_W), lambda i: (0, 0))
    return pl.pallas_call(
        body, out_shape=(jax.ShapeDtypeStruct((S, 2 * SGU_W), BF16), jax.ShapeDtypeStruct((4, SGU_CHUNK, SGU_CHUNK), F32),
                         jax.ShapeDtypeStruct((SGU_CHUNK, SGU_W), F32), jax.ShapeDtypeStruct((1, SGU_W), F32)),
        grid=(S // SGU_CHUNK,),
        in_specs=[pl.BlockSpec((SGU_CHUNK, SGU_W), lambda i: (i, 0)),
                  pl.BlockSpec((SGU_CHUNK, 2 * SGU_W), lambda i: (i, P_C // (2 * SGU_W))), vec, w_spec, w_spec, tile],
        out_specs=(pl.BlockSpec((SGU_CHUNK, 2 * SGU_W), lambda i: (i, 0)), w_spec, tile, vec),
        compiler_params=_cp(("arbitrary",)), name=name)(dsg, proj, gn, wm, wmt, bias)


def _sigmoid(z):
    return 1.0 / (1.0 + jnp.exp(-z))


def _merge_specs(tm):
    row = lambda n: pl.BlockSpec((tm, n), lambda i: (i, 0))
    gate = lambda b: pl.BlockSpec((tm, D), lambda i: (i, b))
    full = lambda r, c: pl.BlockSpec((r, c), lambda i: (0, 0))
    packed = pl.BlockSpec((4, 256, PACK_COLS), lambda i: (0, R_BRANCH // 256, 0))
    return row, gate, full, packed


def _branch_shards(c_ref, j):
    return c_ref[j, :, 0:256], c_ref[j, :, 256:512], c_ref[j, :, 512:768], c_ref[j, :, 768:1024]


def _merge_fwd(proj, ya, o, sg, packed_w, bg, name, tm=512):
    S = proj.shape[0]
    row, gate, full, packed = _merge_specs(tm)

    def body(g0, g1, g2, ya_ref, o_ref, sg_ref, c_ref, bg_ref, out_ref):
        yav, ov, sgv = ya_ref[...], o_ref[...], sg_ref[...]
        for j in range(4):
            cols = slice(256 * j, 256 * (j + 1))
            wa, wb0, wb1, wc = _branch_shards(c_ref, j)
            y = (jnp.dot(yav, wa, preferred_element_type=F32),
                 jnp.dot(ov[:, :256], wb0, preferred_element_type=F32) + jnp.dot(ov[:, 256:], wb1, preferred_element_type=F32),
                 jnp.dot(sgv, wc, preferred_element_type=F32))
            acc = jnp.zeros((tm, 256), F32)
            for b, g_ref in enumerate((g0, g1, g2)):
                acc = acc + _sigmoid(g_ref[:, cols] + bg_ref[:, b * D + 256 * j:b * D + 256 * (j + 1)]) * y[b]
            out_ref[:, cols] = acc.astype(BF16)

    return pl.pallas_call(
        body, out_shape=jax.ShapeDtypeStruct((S, D), BF16), grid=(S // tm,),
        in_specs=[gate(0), gate(1), gate(2), row(POOL_W), row(FOX_W), row(SGU_W), packed, full(1, 3 * D)],
        out_specs=row(D), compiler_params=_cp(("parallel",)), name=name)(proj, proj, proj, ya, o, sg, packed_w, bg)


def _merge_bwd(dm, proj, ya, o, sg, packed_w, bg, grads, name, tm=512):
    S = proj.shape[0]
    row, gate, full, packed = _merge_specs(tm)
    tn_dims = (((0,), (0,)), ((), ()))
    nt_dims = (((1,), (1,)), ((), ()))

    def body(dm_ref, g0, g1, g2, ya_ref, o_ref, sg_ref, c_ref, bg_ref, _, dg_ref, dya_ref, do_ref, dsg_ref, dc_ref, dbg_ref, acc):
        i = pl.program_id(0)

        @pl.when(i == 0)
        def _():
            acc[...] = jnp.zeros_like(acc)
            dbg_ref[...] = jnp.zeros_like(dbg_ref)

        yav, ov, sgv = ya_ref[...], o_ref[...], sg_ref[...]
        o0, o1 = ov[:, :256], ov[:, 256:]
        dya = jnp.zeros((tm, POOL_W), F32)
        do0 = jnp.zeros((tm, 256), F32)
        do1 = jnp.zeros((tm, 256), F32)
        dsg = jnp.zeros((tm, SGU_W), F32)
        for j in range(4):
            cols = slice(256 * j, 256 * (j + 1))
            wa, wb0, wb1, wc = _branch_shards(c_ref, j)
            y = (jnp.dot(yav, wa, preferred_element_type=F32),
                 jnp.dot(o0, wb0, preferred_element_type=F32) + jnp.dot(o1, wb1, preferred_element_type=F32),
                 jnp.dot(sgv, wc, preferred_element_type=F32))
            dmv = dm_ref[:, cols]
            dy = []
            for b, g_ref in enumerate((g0, g1, g2)):
                bcols = slice(b * D + 256 * j, b * D + 256 * (j + 1))
                gt = _sigmoid(g_ref[:, cols] + bg_ref[:, bcols])
                dgp = dmv * y[b] * gt * (1.0 - gt)
                dg_ref[:, bcols] = dgp.astype(BF16)
                dbg_ref[:, bcols] += jnp.sum(dgp, axis=0, keepdims=True)
                dy.append((dmv * gt).astype(BF16))
            dya = dya + lax.dot_general(dy[0], wa, nt_dims, preferred_element_type=F32)
            do0 = do0 + lax.dot_general(dy[1], wb0, nt_dims, preferred_element_type=F32)
            do1 = do1 + lax.dot_general(dy[1], wb1, nt_dims, preferred_element_type=F32)
            dsg = dsg + lax.dot_general(dy[2], wc, nt_dims, preferred_element_type=F32)
            acc[j, :, 0:256] += lax.dot_general(yav, dy[0], tn_dims, preferred_element_type=F32)
            acc[j, :, 256:512] += lax.dot_general(o0, dy[1], tn_dims, preferred_element_type=F32)
            acc[j, :, 512:768] += lax.dot_general(o1, dy[1], tn_dims, preferred_element_type=F32)
            acc[j, :, 768:1024] += lax.dot_general(sgv, dy[2], tn_dims, preferred_element_type=F32)
        dya_ref[...] = dya
        do_ref[:, :256] = do0
        do_ref[:, 256:] = do1
        dsg_ref[...] = dsg

        @pl.when(i == pl.num_programs(0) - 1)
        def _():
            dc_ref[...] = acc[...].astype(dc_ref.dtype)

    return pl.pallas_call(
        body, out_shape=(jax.ShapeDtypeStruct((S, 3 * D), BF16), jax.ShapeDtypeStruct((S, POOL_W), F32),
                         jax.ShapeDtypeStruct((S, FOX_W), F32), jax.ShapeDtypeStruct((S, SGU_W), F32),
                         jax.ShapeDtypeStruct(grads.shape, grads.dtype), jax.ShapeDtypeStruct((1, 3 * D), F32)),
        grid=(S // tm,),
        in_specs=[row(D), gate(0), gate(1), gate(2), row(POOL_W), row(FOX_W), row(SGU_W), packed, full(1, 3 * D), ANY],
        out_specs=(row(3 * D), row(POOL_W), row(FOX_W), row(SGU_W), packed, full(1, 3 * D)),
        scratch_shapes=[pltpu.VMEM((4, 256, PACK_COLS), F32)], input_output_aliases={9: 4},
        compiler_params=_cp(("arbitrary",)), name=name)(dm, proj, proj, proj, ya, o, sg, packed_w, bg, grads)


def _xattn_probs(qh, kh):
    s = lax.dot_general(qh, kh, (((1,), (1,)), ((), ())), preferred_element_type=F32) * X_SCALE
    p = jnp.exp(s - jnp.max(s, axis=-1, keepdims=True))
    return p / jnp.sum(p, axis=-1, keepdims=True)


def _xattn_fwd(xq, kv, name, tq=512):
    S = xq.shape[0]
    M = kv.shape[0]

    def body(q_ref, k_ref, v_ref, o_ref):
        for h in range(XH):
            sl = slice(h * XHD, (h + 1) * XHD)
            p = _xattn_probs(q_ref[:, sl], k_ref[:, sl])
            o_ref[:, sl] = jnp.dot(p.astype(BF16), v_ref[:, sl], preferred_element_type=F32).astype(BF16)

    return pl.pallas_call(
        body, out_shape=jax.ShapeDtypeStruct((S, D), BF16), grid=(S // tq,),
        in_specs=[pl.BlockSpec((tq, D), lambda i: (i, 0)), pl.BlockSpec((M, D), lambda i: (0, 0)),
                  pl.BlockSpec((M, D), lambda i: (0, 1))],
        out_specs=pl.BlockSpec((tq, D), lambda i: (i, 0)), compiler_params=_cp(("parallel",)), name=name)(xq, kv, kv)


def _xattn_bwd(xq, kv, do, name, tq=512):
    S = xq.shape[0]
    M = kv.shape[0]

    def body(q_ref, k_ref, v_ref, do_ref, dq_ref, dkv_ref, dk_acc, dv_acc):
        i = pl.program_id(0)

        @pl.when(i == 0)
        def _():
            dk_acc[...] = jnp.zeros_like(dk_acc)
            dv_acc[...] = jnp.zeros_like(dv_acc)

        for h in range(XH):
            sl = slice(h * XHD, (h + 1) * XHD)
            qh, kh, vh, doh = q_ref[:, sl], k_ref[:, sl], v_ref[:, sl], do_ref[:, sl]
            p = _xattn_probs(qh, kh)
            dp = lax.dot_general(doh, vh, (((1,), (1,)), ((), ())), preferred_element_type=F32)
            ds = p * (dp - jnp.sum(p * dp, axis=-1, keepdims=True))
            dsb = (ds * X_SCALE).astype(BF16)
            dq_ref[:, sl] = jnp.dot(dsb, kh, preferred_element_type=F32).astype(BF16)
            dk_acc[:, sl] += lax.dot_general(dsb, qh, (((0,), (0,)), ((), ())), preferred_element_type=F32)
            dv_acc[:, sl] += lax.dot_general(p.astype(BF16), doh, (((0,), (0,)), ((), ())), preferred_element_type=F32)

        @pl.when(i == pl.num_programs(0) - 1)
        def _():
            dkv_ref[:, :D] = dk_acc[...].astype(BF16)
            dkv_ref[:, D:] = dv_acc[...].astype(BF16)

    return pl.pallas_call(
        body, out_shape=(jax.ShapeDtypeStruct((S, D), BF16), jax.ShapeDtypeStruct((M, 2 * D), BF16)), grid=(S // tq,),
        in_specs=[pl.BlockSpec((tq, D), lambda i: (i, 0)), pl.BlockSpec((M, D), lambda i: (0, 0)),
                  pl.BlockSpec((M, D), lambda i: (0, 1)), pl.BlockSpec((tq, D), lambda i: (i, 0))],
        out_specs=(pl.BlockSpec((tq, D), lambda i: (i, 0)), pl.BlockSpec((M, 2 * D), lambda i: (0, 0))),
        scratch_shapes=[pltpu.VMEM((M, D), F32), pltpu.VMEM((M, D), F32)],
        compiler_params=_cp(("arbitrary",)), name=name)(xq, kv, kv, do)


def _adam_math(gv, wv, mv, vv):
    c1 = 1.0 - ADAM_B1 ** ADAM_STEP
    c2 = 1.0 - ADAM_B2 ** ADAM_STEP
    nm = ADAM_B1 * mv + (1.0 - ADAM_B1) * gv
    nv = ADAM_B2 * vv + (1.0 - ADAM_B2) * (gv * gv)
    return -ADAM_LR * ((nm / c1) / (jnp.sqrt(nv / c2) + ADAM_EPS) + ADAM_WD * wv), nm, nv


def _adamw(g, w, m, v, name, block=None):
    if block is None:
        block = (1, 256 if g.shape[1] % 256 == 0 else g.shape[1], g.shape[2])
    grid = tuple(s // b for s, b in zip(g.shape, block))

    def body(g_ref, w_ref, m_ref, v_ref, d_ref, nm_ref, nv_ref):
        d_ref[...], nm_ref[...], nv_ref[...] = _adam_math(g_ref[...], w_ref[...], m_ref[...], v_ref[...])

    blk = pl.BlockSpec(block, lambda a, b, c: (a, b, c))
    return pl.pallas_call(
        body, out_shape=(jax.ShapeDtypeStruct(g.shape, F32),) * 3, grid=grid,
        in_specs=[blk] * 4, out_specs=(blk,) * 3, compiler_params=_cp(("parallel",) * 3), name=name)(g, w, m, v)


def _adamw_packed(red, w, m, v, g_index, name, token, tr=256):
    L, r, c = w.shape
    tr = min(tr, r)

    def body(g0_ref, g1_ref, w_ref, m_ref, v_ref, _, g_ref, d_ref, nm_ref, nv_ref):
        gv = jnp.where(pl.program_id(0) == 0, g0_ref[...], g1_ref[...])
        g_ref[0] = gv
        d_ref[0], nm_ref[0], nv_ref[0] = _adam_math(gv, w_ref[0], m_ref[0], v_ref[0])

    gblk = pl.BlockSpec((tr, c), lambda l, i: g_index(i))
    blk = pl.BlockSpec((1, tr, c), lambda l, i: (l, i, 0))
    return pl.pallas_call(
        body, out_shape=(jax.ShapeDtypeStruct(w.shape, F32),) * 4, grid=(L, r // tr),
        in_specs=[gblk, gblk, blk, blk, blk, pl.BlockSpec((8, 128), lambda l, i: (0, 0))], out_specs=(blk,) * 4,
        compiler_params=_cp(("parallel", "parallel")), name=name)(red[0], red[1], w, m, v, token)


def _row_tile(R):
    return next((t for t in (512, 496, 384, 256) if R % t == 0), R)


def _sum_slots(a, out_dtype, name):
    n, R, C = a.shape
    tr = _row_tile(R)

    def body(a_ref, o_ref):
        acc = a_ref[0].astype(F32)
        for k in range(1, n):
            acc = acc + a_ref[k].astype(F32)
        o_ref[...] = acc.astype(out_dtype)

    return pl.pallas_call(
        body, out_shape=jax.ShapeDtypeStruct((R, C), out_dtype), grid=(R // tr,),
        in_specs=[pl.BlockSpec((n, tr, C), lambda i: (0, i, 0))], out_specs=pl.BlockSpec((tr, C), lambda i: (i, 0)),
        compiler_params=_cp(("parallel",)), name=name)(a)


LANDING = pl.BlockSpec(memory_space=pltpu.VMEM)


def _landing_params(shape, dtype):
    return pltpu.CompilerParams(vmem_limit_bytes=math.prod(shape) * jnp.dtype(dtype).itemsize + 4 * 1024 * 1024)


def _place():
    return lax.axis_index("x"), lax.axis_index("y"), lax.axis_index("c")


def _other_chips(x, y):
    return [(1 - x, y), (x, 1 - y), (1 - x, 1 - y)]


def _row_chunks(rows, want, align=16):
    n = want
    while n > 1 and rows % (n * align):
        n -= 1
    return n


def _pair_add(g, name, nch=5):
    n, R, C = g.shape
    half = R // 2
    nch = _row_chunks(half, nch)
    cr = half // nch
    rb = next(t for t in (512, 256, 128, 64, 32, 16) if half % t == 0)

    def body(g_ref, p_ref, got, send_sems, recv_sems, local_sem):
        x, y, c = _place()
        mine0 = pl.multiple_of(c * half, 16)
        theirs0 = (1 - c) * half
        keep = pltpu.make_async_copy(g_ref.at[:, pl.ds(mine0, half), :], p_ref, local_sem)
        keep.start()
        cps = []
        for s in range(n):
            for q in range(nch):
                src = g_ref.at[s, pl.ds(pl.multiple_of(theirs0 + q * cr, 16), cr), :]
                cps.append(pltpu.make_async_remote_copy(
                    src_ref=src, dst_ref=got.at[s, pl.ds(q * cr, cr), :], send_sem=send_sems.at[s * nch + q],
                    recv_sem=recv_sems.at[s * nch + q], device_id=(x, y, 1 - c), device_id_type=MESH))
        for cp in cps:
            cp.start()
        for cp in cps:
            cp.wait()
        keep.wait()

        def add(i, _):
            rows = pl.ds(pl.multiple_of(i * rb, rb), rb)
            for s in range(n):
                p_ref[s, rows, :] = (p_ref[s, rows, :].astype(F32) + got[s, rows, :].astype(F32)).astype(BF16)
            return 0

        lax.fori_loop(0, half // rb, add, 0)

    shape = (n, half, C)
    return pl.pallas_call(
        body, out_shape=jax.ShapeDtypeStruct(shape, g.dtype), in_specs=[ANY], out_specs=LANDING,
        scratch_shapes=[pltpu.VMEM(shape, g.dtype), pltpu.SemaphoreType.DMA((n * nch,)), pltpu.SemaphoreType.DMA((n * nch,)),
                        pltpu.SemaphoreType.DMA],
        compiler_params=_landing_params((2,) + shape, g.dtype), name=name)(g)


def _pair_gather(t, name, nch=10):
    R = t.shape[0]
    nch = _row_chunks(R, nch, 8)
    cr = R // nch

    def body(t_ref, o_ref, send_sems, recv_sems, local_sem):
        x, y, c = _place()
        own = pltpu.make_async_copy(t_ref, o_ref.at[c], local_sem)
        own.start()
        cps = [pltpu.make_async_remote_copy(src_ref=t_ref.at[pl.ds(q * cr, cr), :], dst_ref=o_ref.at[c, pl.ds(q * cr, cr), :],
                                            send_sem=send_sems.at[q], recv_sem=recv_sems.at[q], device_id=(x, y, 1 - c),
                                            device_id_type=MESH) for q in range(nch)]
        for cp in cps:
            cp.start()
        for cp in cps:
            cp.wait()
        own.wait()

    return pl.pallas_call(
        body, out_shape=jax.ShapeDtypeStruct((2,) + t.shape, t.dtype), in_specs=[ANY], out_specs=LANDING,
        scratch_shapes=[pltpu.SemaphoreType.DMA((nch,)), pltpu.SemaphoreType.DMA((nch,)), pltpu.SemaphoreType.DMA],
        compiler_params=_landing_params((2,) + t.shape, t.dtype), name=name)(t)


HBM = pl.BlockSpec(memory_space=pltpu.HBM)
SEM = pl.BlockSpec(memory_space=pltpu.SEMAPHORE)
SPLIT_COPY = pltpu.CompilerParams(has_side_effects=pltpu.SideEffectType.DATAFLOW_SIDE_EFFECTING)


def _split_exchange(src, rows, src_of, tag, nch=5):
    C = src.shape[-1]
    nch = _row_chunks(rows, nch)
    cr = rows // nch
    n = 3 * nch
    land_shape = (4, rows, C)

    def copies(src_ref, land_ref, send_sems, recv_sems):
        x, y, c = _place()
        j = 2 * x + y
        out = []
        for q in range(nch):
            for k, (px, py) in enumerate(_other_chips(x, y)):
                out.append(pltpu.make_async_remote_copy(
                    src_ref=src_of(src_ref, px, py, c, q * cr, cr), dst_ref=land_ref.at[j, pl.ds(q * cr, cr), :],
                    send_sem=send_sems.at[k * nch + q], recv_sem=recv_sems.at[k * nch + q], device_id=(px, py, c),
                    device_id_type=MESH))
        return out

    def start(src_ref, land_ref, send_sems, recv_sems, src_thru, land_thru, token):
        for cp in copies(src_ref, land_ref, send_sems, recv_sems):
            cp.start()
        token[...] = jnp.zeros_like(token)

    send_sems, recv_sems, src_thru, land_thru, token = pl.pallas_call(
        start, name=f"{tag}_start",
        out_shape=(pltpu.SemaphoreType.DMA((n,)), pltpu.SemaphoreType.DMA((n,)), pltpu.HBM(src.shape, src.dtype),
                   pltpu.HBM(land_shape, src.dtype), jax.ShapeDtypeStruct((8, 128), F32)),
        in_specs=(HBM, HBM), out_specs=(SEM, SEM, HBM, HBM, pl.BlockSpec(memory_space=pltpu.VMEM)),
        input_output_aliases={0: 2, 1: 3}, compiler_params=SPLIT_COPY)(
            pltpu.with_memory_space_constraint(src, pltpu.HBM),
            pltpu.with_memory_space_constraint(lax.empty(land_shape, src.dtype), pltpu.HBM))

    def finish(after):
        def wait(src_ref, land_ref, send_sems, recv_sems, after_ref, src_dead, got_ref):
            for cp in copies(src_ref, land_ref, send_sems, recv_sems):
                cp.wait_send()
                cp.wait_recv()

        return pl.pallas_call(
            wait, name=f"{tag}_wait", out_shape=(pltpu.HBM(src.shape, src.dtype), pltpu.HBM(land_shape, src.dtype)),
            in_specs=(HBM, HBM, SEM, SEM, ANY), out_specs=(HBM, HBM), input_output_aliases={0: 0, 1: 1},
            compiler_params=SPLIT_COPY)(src_thru, land_thru, send_sems, recv_sems, after)

    return token, finish


def _gather_finish(shard, land, name, nch=5):
    R, C = shard.shape
    half = R // 2
    nch = _row_chunks(half, nch)
    cr = half // nch

    def body(s_ref, l_ref, o_ref, send_sems, recv_sems, local_sems):
        x, y, c = _place()
        j = 2 * x + y
        mine0 = c * half
        local = [pltpu.make_async_copy(s_ref, o_ref.at[j], local_sems.at[0])]
        remote = []
        for k, (px, py) in enumerate(_other_chips(x, y)):
            jj = 2 * px + py
            local.append(pltpu.make_async_copy(l_ref.at[jj], o_ref.at[jj, pl.ds(pl.multiple_of(mine0, 16), half), :],
                                               local_sems.at[1 + k]))
            for q in range(nch):
                remote.append(pltpu.make_async_remote_copy(
                    src_ref=l_ref.at[jj, pl.ds(q * cr, cr), :],
                    dst_ref=o_ref.at[jj, pl.ds(pl.multiple_of(mine0 + q * cr, 16), cr), :], send_sem=send_sems.at[k * nch + q],
                    recv_sem=recv_sems.at[k * nch + q], device_id=(x, y, 1 - c), device_id_type=MESH))
        for cp in local + remote:
            cp.start()
        for cp in remote + local:
            cp.wait()

    return pl.pallas_call(
        body, out_shape=jax.ShapeDtypeStruct((4, R, C), shard.dtype), in_specs=[ANY, ANY], out_specs=LANDING,
        scratch_shapes=[pltpu.SemaphoreType.DMA((3 * nch,)), pltpu.SemaphoreType.DMA((3 * nch,)), pltpu.SemaphoreType.DMA((4,))],
        compiler_params=_landing_params((4, R, C), shard.dtype), name=name)(shard, land)


def _sum_slots_own(land, own, name):
    n, R, C = land.shape
    tr = _row_tile(R)
    me = (2 * lax.axis_index("x") + lax.axis_index("y")).astype(jnp.int32).reshape(1)
    if own.ndim == 3:
        own_spec = pl.BlockSpec((None, tr, C), lambda i, me: (me[0], i, 0))
    else:
        own_spec = pl.BlockSpec((tr, C), lambda i, me: (i, 0))

    def body(me_ref, land_ref, own_ref, o_ref):
        acc = None
        for k in range(n):
            v = jnp.where(me_ref[0] == k, own_ref[...], land_ref[k]).astype(F32)
            acc = v if acc is None else acc + v
        o_ref[...] = acc

    return pl.pallas_call(
        body, out_shape=jax.ShapeDtypeStruct((R, C), F32),
        grid_spec=pltpu.PrefetchScalarGridSpec(
            num_scalar_prefetch=1, grid=(R // tr,),
            in_specs=[pl.BlockSpec((n, tr, C), lambda i, me: (0, i, 0)), own_spec],
            out_specs=pl.BlockSpec((tr, C), lambda i, me: (i, 0))),
        compiler_params=_cp(("parallel",)), name=name)(me, land, own)


def _reduce_begin(g, tag):
    p = _pair_add(g, f"rs_pair_{tag}")
    token, finish = _split_exchange(p, p.shape[1], lambda ref, px, py, c, r0, cr: ref.at[2 * px + py, pl.ds(r0, cr), :],
                                    f"rs_a2a_{tag}")
    return (finish, g.shape, tag), token


def _reduce_end(state, after):
    finish, shape, tag = state
    p, land = finish(after)
    t = _sum_slots_own(land, p, f"rs_sum_{tag}")
    return _pair_gather(t, f"rs_join_{tag}").reshape(shape[1], shape[2])


def _all_reduce_begin(v, tag):
    p = _sum_slots(_pair_gather(v, f"ar_pair_{tag}"), F32, f"ar_add_{tag}")
    token, finish = _split_exchange(p, p.shape[0], lambda ref, px, py, c, r0, cr: ref.at[pl.ds(r0, cr), :], f"ar_a2a_{tag}")
    return (finish, tag), token


def _all_reduce_end(state, after):
    finish, tag = state
    p, land = finish(after)
    return _sum_slots_own(land, p, f"ar_sum_{tag}")


def _gather_begin(shard, tag):
    half = shard.shape[0] // 2
    token, finish = _split_exchange(
        shard, half, lambda ref, px, py, c, r0, cr: ref.at[pl.ds(pl.multiple_of(c * half + r0, 16), cr), :], f"gather_{tag}")
    return (finish, tag), token


def _gather_end(state, after):
    finish, tag = state
    shard, land = finish(after)
    return _gather_finish(shard, land, f"gather_{tag}_finish")


R_BRANCH, R_OUT, R_WIN, ROWS_A = 0, 256, 512, 1888
R_FF1, R_FF2, R_XKV, R_XQ, R_XO, ROWS_B = 0, 1024, 2048, 2560, 2816, 3072
WIN_ROWS = N_IN // 4


def _w_in_t(a):
    return jnp.transpose(a, (2, 0, 1))


def _pack_shard(w, l):
    xkv, wb = w['w_xkv'][l], w['w_branch_b'][l]
    a = [jnp.concatenate([w['w_branch_a'][l], wb[:256], wb[256:], w['w_branch_c'][l]], axis=1), w['w_out'][l],
         jnp.pad(_w_in_t(w['w_in'])[:, l, :], ((0, ROWS_A - R_WIN - WIN_ROWS), (0, 0)))]
    b = [w['w_ff1'][l], w['w_ff2'][l], jnp.concatenate([xkv[:512], xkv[512:]], axis=1), w['w_xq'][l], w['w_xo'][l]]
    return jnp.concatenate(a, axis=0).astype(BF16), jnp.concatenate(b, axis=0).astype(BF16)


def _w_in_rows(gathered):
    t = gathered[:, R_WIN:R_WIN + WIN_ROWS, :].reshape(N_IN, PACK_COLS)
    return jnp.concatenate([t[2312:5384], t[256:1792], t[1800:2312], t[0:256],
                            jnp.pad(t[1792:1800], ((0, NP - P_F - 8), (0, 0)))], axis=0)


def _w_in_grad_rows(grads, dwt):
    t = jnp.concatenate([dwt[P_A:P_A + 256], dwt[P_Q:P_Q + 1536], dwt[P_F:P_F + 8], dwt[P_C:P_C + 512], dwt[P_G:P_G + 3072]],
                        axis=0)
    return lax.dynamic_update_slice(grads, t.reshape(4, WIN_ROWS, PACK_COLS).astype(grads.dtype), (0, R_WIN, 0))


def _small_prep(sw, l):
    eye = jnp.eye(4, dtype=F32)
    bd = jnp.einsum('gh,gcd->gchd', eye, sw['pool_w'][l]).reshape(POOL_W, POOL_W).astype(BF16)
    tril = jnp.tril(jnp.ones((SGU_CHUNK, SGU_CHUNK), F32))
    wm = (sw['sgu_w'][l] * tril[None]).astype(BF16)
    return dict(
        g_mix=sw['norm_mix_g'][l][None], g_x=sw['norm_xattn_g'][l][None], g_mem=sw['norm_mem_g'][l][None],
        g_ffn=sw['norm_ffn_g'][l][None], bd=bd, pool_scale=sw['pool_scale'][l][None],
        bf=jnp.pad(sw['b_forget'][l], (0, FCOLS - 8))[None], sgu_g=sw['sgu_norm_g'][l][None], wm=wm,
        wmt=jnp.transpose(wm, (0, 2, 1)), sgu_bias=jnp.repeat(sw['sgu_b'][l].T, 64, axis=1), bg=sw['b_gate'][l][None])


def _rows4(r0):
    return dict(n=D, k=D, tn=D, b_block=(4, 256, PACK_COLS), b_index=lambda i, j, k: (0, r0 // 256, 0))


def _rows_t(r0):
    return dict(tb=True, n=D, k=D, tn=D, b_block=(4, 256, PACK_COLS), b_index=lambda i, j, k: (0, r0 // 256, 0))


def _rows_grad(r0):
    return dict(ta=True, tm=D, tn=512, o_block=(4, 256, 512), o_index=lambda i, j, k: (0, r0 // 256, j))


def _add_to(r, e):
    return e + r


def _after(v, token):
    return v if token is None else v + token[0, 0]


def _layer_fwd(x, mem, GA, w_in_t, sp, l, token, second):
    t = f"l{l}"
    S = x.shape[0]
    h = _rms_fwd(x, _after(sp['g_mix'], token), f"rms_mix_{t}")
    proj = _mm(h, w_in_t, name=f"proj_{t}", out_dtype=F32, tb=True)
    d, ya = _pool_fwd(proj, sp['bd'], sp['pool_scale'], f"pool_fwd_{t}")
    fcum = _fgate_fwd(proj, sp['bf'], f"fgate_fwd_{t}")
    f8 = fcum[:, :8]
    fcol = f8.reshape(S, 4, 2).transpose(1, 0, 2)
    frow = f8.T.reshape(4, 2, S)
    qkv = proj[:, P_Q:P_Q + 3 * FOX_W].astype(BF16)
    o, o32, lse = _fox_fwd(qkv, fcol, frow, f"fox_fwd_{t}")
    sg = _sgu_fwd(proj, sp['sgu_g'], sp['wm'], sp['sgu_bias'], f"sgu_fwd_{t}")
    merged = _merge_fwd(proj, ya, o, sg, GA, sp['bg'], f"merge_fwd_{t}")
    x1 = _mm(merged, GA, name=f"out_{t}", out_dtype=F32, extra=x, epi=_add_to, **_rows4(R_OUT))
    GB, token = second(x1)
    hx = _rms_fwd(x1, _after(sp['g_x'], token), f"rms_x_{t}")
    hm = _rms_fwd(mem, sp['g_mem'], f"rms_mem_{t}")
    xq = _mm(hx, GB, name=f"xq_{t}", out_dtype=BF16, **_rows4(R_XQ))
    kv = _mm(hm, GB, name=f"xkv_{t}", out_dtype=BF16, n=2 * D, k=D, tn=512, tk=512, b_block=(None, 512, 512),
             b_index=lambda i, j, k: (j, R_XKV // 512, k))
    o2 = _xattn_fwd(xq, kv, f"xattn_fwd_{t}")
    x2 = _mm(o2, GB, name=f"xo_{t}", out_dtype=F32, extra=x1, epi=_add_to, **_rows4(R_XO))
    hf = _rms_fwd(x2, sp['g_ffn'], f"rms_ffn_{t}")
    z = _mm(hf, GB, name=f"ff1_{t}", out_dtype=F32, n=D_FF, k=D, tn=D, b_block=(None, 1024, PACK_COLS),
            b_index=lambda i, j, k: (j, R_FF1 // 1024, 0))
    x3 = _mm(z, GB, name=f"ff2_{t}", out_dtype=F32, a_fn=_relu2, extra=x2, epi=_add_to, n=D, k=D_FF, tk=1024, tn=D,
             b_block=(None, 1024, PACK_COLS), b_index=lambda i, j, k: (k, R_FF2 // 1024, 0))
    saved = dict(x=x, h=h, proj=proj, d=d, ya=ya, fcol=fcol, frow=frow, qkv=qkv, o=o, o32=o32, lse=lse, sg=sg, merged=merged,
                 x1=x1, hx=hx, hm=hm, xq=xq, kv=kv, o2=o2, x2=x2, hf=hf, z=z, GA=GA, GB=GB, w_in_t=w_in_t)
    return x3, saved


def _layer_bwd(dx3, mem, sp, sv, l, token, early):
    t = f"l{l}"
    S = dx3.shape[0]
    GA, GB = sv['GA'], sv['GB']
    gs = {}
    dx3 = _after(dx3, token)
    gb = lax.empty((4, ROWS_B, PACK_COLS), BF16)
    dz = _mm(dx3, GB, name=f"d_a2_{t}", out_dtype=BF16, tb=True, n=D_FF, k=D, tn=D, b_block=(None, 1024, PACK_COLS),
             b_index=lambda i, j, k: (j, R_FF2 // 1024, 0), extra=sv['z'],
             epi=lambda r, e: r * (2.0 * jnp.maximum(e, 0.0)))
    gb = _mm(sv['z'], dx3, name=f"dw_ff2_{t}", out_dtype=BF16, ta=True, a_fn=_relu2, into=gb, tm=1024, tn=D,
             o_block=(None, 1024, PACK_COLS), o_index=lambda i, j, k: (i, R_FF2 // 1024, 0))
    gb = _mm(sv['hf'], dz, name=f"dw_ff1_{t}", out_dtype=BF16, ta=True, into=gb, tm=1024, tn=D,
             o_block=(None, 1024, PACK_COLS), o_index=lambda i, j, k: (j, R_FF1 // 1024, 0))
    dhf = _mm(dz, GB, name=f"d_hf_{t}", out_dtype=F32, tb=True, n=D, k=D_FF, tn=D, tk=1024, b_block=(None, 1024, PACK_COLS),
              b_index=lambda i, j, k: (k, R_FF1 // 1024, 0))
    dx2, gs['norm_ffn_g'] = _rms_bwd(dhf, sv['x2'], sp['g_ffn'], dx3, f"rms_ffn_bwd_{t}")
    do2 = _mm(dx2, GB, name=f"d_o2_{t}", out_dtype=BF16, **_rows_t(R_XO))
    gb = _mm(sv['o2'], dx2, name=f"dw_xo_{t}", out_dtype=BF16, into=gb, **_rows_grad(R_XO))
    dxq, dkv = _xattn_bwd(sv['xq'], sv['kv'], do2, f"xattn_bwd_{t}")
    gb = _mm(sv['hm'], dkv, name=f"dw_xkv_{t}", out_dtype=BF16, ta=True, into=gb, tm=512, tn=512,
             o_block=(None, 512, 512), o_index=lambda i, j, k: (j, R_XKV // 512, i))
    dhm = _mm(dkv, GB, name=f"d_hm_{t}", out_dtype=F32, tb=True, n=D, k=2 * D, tn=512, tk=512, b_block=(None, 512, 512),
              b_index=lambda i, j, k: (k, R_XKV // 512, j))
    gs['norm_mem_g'] = _rms_bwd(dhm, mem, sp['g_mem'], None, f"rms_mem_bwd_{t}")
    gb = _mm(sv['hx'], dxq, name=f"dw_xq_{t}", out_dtype=BF16, into=gb, **_rows_grad(R_XQ))
    token = early(gb)
    dhx = _mm(dxq, GB, name=f"d_hx_{t}", out_dtype=F32, **_rows_t(R_XQ))
    dx1, gs['norm_xattn_g'] = _rms_bwd(dhx, sv['x1'], _after(sp['g_x'], token), dx2, f"rms_x_bwd_{t}")
    ga = jnp.zeros((4, ROWS_A, PACK_COLS), BF16)
    ga = _mm(sv['merged'], dx1, name=f"dw_out_{t}", out_dtype=BF16, into=ga, **_rows_grad(R_OUT))
    dm = _mm(dx1, GA, name=f"d_merged_{t}", out_dtype=F32, **_rows_t(R_OUT))
    dg, dya, do, dsg, ga, gs['b_gate'] = _merge_bwd(dm, sv['proj'], sv['ya'], sv['o'], sv['sg'], GA, sp['bg'], ga, f"merge_bwd_{t}")
    dc, dws, dbias, gs['sgu_norm_g'] = _sgu_bwd(dsg, sv['proj'], sp['sgu_g'], sp['wm'], sp['wmt'], sp['sgu_bias'], f"sgu_bwd_{t}")
    tril = jnp.tril(jnp.ones((SGU_CHUNK, SGU_CHUNK), F32))
    gs['sgu_w'] = dws * tril[None]
    gs['sgu_b'] = dbias.reshape(SGU_CHUNK, 4, 64).sum(-1).T
    dq, dk, dv, dfrow, dfcol = _fox_bwd(sv['qkv'], sv['o32'], do, sv['lse'], sv['fcol'], sv['frow'], f"fox_bwd_{t}")
    dF = jnp.pad(dfrow.reshape(8, S).T + dfcol.transpose(1, 0, 2).reshape(S, 8), ((0, 0), (0, FCOLS - 8)))
    df, dbf = _fgate_bwd(dF, sv['proj'], sp['bf'], f"fgate_bwd_{t}")
    gs['b_forget'] = dbf[:, :8]
    da, dbd, gs['pool_scale'] = _pool_bwd(dya, sv['d'], sp['bd'], sp['pool_scale'], f"pool_bwd_{t}")
    gs['pool_w'] = jnp.stack([dbd[g * 64:(g + 1) * 64, g * 64:(g + 1) * 64] for g in range(4)])
    dproj = jnp.concatenate([dg, dq, dk, dv, dc, da, df], axis=1)
    dwt = _mm(dproj, sv['h'], name=f"dw_in_{t}", out_dtype=BF16, ta=True, tm=512, tn=1024)
    ga = _w_in_grad_rows(ga, dwt)
    dh = _mm(dproj, sv['w_in_t'], name=f"d_h_{t}", out_dtype=F32, tk=512, tn=D)
    dx, gs['norm_mix_g'] = _rms_bwd(dh, sv['x'], sp['g_mix'], dx1, f"rms_mix_bwd_{t}")
    return dx, ga, gs


SMALL_ROWS = 1424
GRAD_BLOCKS = {
    'w_ff1': ('b', lambda i: (R_FF1 // 256 + i, 0)), 'w_ff2': ('b', lambda i: (R_FF2 // 256 + i, 0)),
    'w_xq': ('b', lambda i: (R_XQ // 256 + i, 0)), 'w_xo': ('b', lambda i: (R_XO // 256 + i, 0)),
    'w_xkv': ('b', lambda i: (R_XKV // 256 + i % 2, i // 2)), 'w_out': ('a', lambda i: (R_OUT // 256 + i, 0)),
    'w_branch_a': ('a', lambda i: (R_BRANCH // 256, 0)), 'w_branch_b': ('a', lambda i: (R_BRANCH // 256, 1 + i)),
    'w_branch_c': ('a', lambda i: (R_BRANCH // 256, 3)),
}


def _pack_small(parts):
    flat = jnp.concatenate([p.reshape(-1) for p in parts])
    return jnp.pad(flat, (0, SMALL_ROWS * 128 - flat.shape[0])).reshape(SMALL_ROWS, 128)


def _unpack_small(buf, shapes):
    flat, out, r = buf.reshape(-1), [], 0
    for s in shapes:
        n = math.prod(s)
        out.append(flat[r:r + n].reshape(s))
        r += n
    return out


def kernel(x, mem, norm_mix_g, w_in, b_forget, pool_w, pool_scale, sgu_norm_g, sgu_w, sgu_b, w_branch_a, w_branch_b, w_branch_c, b_gate, w_out, norm_xattn_g, norm_mem_g, w_xq, w_xkv, w_xo, norm_ffn_g, w_ff1, w_ff2, final_norm_g, loss_target, m_norm_mix_g, m_w_in, m_b_forget, m_pool_w, m_pool_scale, m_sgu_norm_g, m_sgu_w, m_sgu_b, m_w_branch_a, m_w_branch_b, m_w_branch_c, m_b_gate, m_w_out, m_norm_xattn_g, m_norm_mem_g, m_w_xq, m_w_xkv, m_w_xo, m_norm_ffn_g, m_w_ff1, m_w_ff2, m_final_norm_g, v_norm_mix_g, v_w_in, v_b_forget, v_pool_w, v_pool_scale, v_sgu_norm_g, v_sgu_w, v_sgu_b, v_w_branch_a, v_w_branch_b, v_w_branch_c, v_b_gate, v_w_out, v_norm_xattn_g, v_norm_mem_g, v_w_xq, v_w_xkv, v_w_xo, v_norm_ffn_g, v_w_ff1, v_w_ff2, v_final_norm_g):
    args = (norm_mix_g, w_in, b_forget, pool_w, pool_scale, sgu_norm_g, sgu_w, sgu_b, w_branch_a, w_branch_b, w_branch_c, b_gate,
            w_out, norm_xattn_g, norm_mem_g, w_xq, w_xkv, w_xo, norm_ffn_g, w_ff1, w_ff2, final_norm_g)
    margs = (m_norm_mix_g, m_w_in, m_b_forget, m_pool_w, m_pool_scale, m_sgu_norm_g, m_sgu_w, m_sgu_b, m_w_branch_a, m_w_branch_b,
             m_w_branch_c, m_b_gate, m_w_out, m_norm_xattn_g, m_norm_mem_g, m_w_xq, m_w_xkv, m_w_xo, m_norm_ffn_g, m_w_ff1, m_w_ff2,
             m_final_norm_g)
    vargs = (v_norm_mix_g, v_w_in, v_b_forget, v_pool_w, v_pool_scale, v_sgu_norm_g, v_sgu_w, v_sgu_b, v_w_branch_a, v_w_branch_b,
             v_w_branch_c, v_b_gate, v_w_out, v_norm_xattn_g, v_norm_mem_g, v_w_xq, v_w_xkv, v_w_xo, v_norm_ffn_g, v_w_ff1, v_w_ff2,
             v_final_norm_g)
    w = dict(zip(W_NAMES, args))
    mo = dict(zip(W_NAMES, margs))
    vo = dict(zip(W_NAMES, vargs))
    xs, mems, tgt = x[0], mem[0], loss_target[0]
    shards = [_pack_shard(w, l) for l in range(DEPTH)]
    preps = [_small_prep(w, l) for l in range(DEPTH)]

    first_a, _ = _gather_begin(shards[0][0], "a_l0")
    pending_b, token = _gather_begin(shards[0][1], "b_l0")
    GA = None
    act, saved = xs, []
    for l in range(DEPTH):
        nxt = {}
        if l + 1 < DEPTH:
            nxt['a'], ta = _gather_begin(shards[l + 1][0], f"a_l{l + 1}")
            token = ta if token is None else token + ta
        if l == 0:
            GA = _gather_end(first_a, shards[DEPTH - 1][1])

        def second(x1, l=l, pending_b=pending_b, nxt=nxt):
            GB = _gather_end(pending_b, x1)
            if l + 1 == DEPTH:
                return GB, None
            nxt['b'], tb = _gather_begin(shards[l + 1][1], f"b_l{l + 1}")
            return GB, tb

        act, sv = _layer_fwd(act, mems, GA, _w_in_rows(GA), preps[l], l, token, second)
        saved.append(sv)
        if l + 1 < DEPTH:
            GA = _gather_end(nxt['a'], act)
            pending_b, token = nxt['b'], None
    loss_part, dact, d_final_g = _loss_head(act, w['final_norm_g'][None], tgt, "loss_head")

    red_a, red_b, small_g = [None] * DEPTH, [None] * DEPTH, [None] * DEPTH
    token, state_a = None, None
    for l in reversed(range(DEPTH)):
        early = {}

        def start_b(gb, l=l, early=early):
            early['state'], tok = _reduce_begin(gb, f"b_l{l}")
            return tok

        dact, ga, small_g[l] = _layer_bwd(dact, mems, preps[l], saved[l], l, token, start_b)
        if state_a is not None:
            red_a[l + 1] = _reduce_end(state_a, dact)
        red_b[l] = _reduce_end(early['state'], dact)
        state_a, token = _reduce_begin(ga, f"a_l{l}")
    grad_x = dact[None]
    per_layer = [n for n in SMALL_NAMES if n != 'final_norm_g']
    small_shapes = [w[n].shape for n in per_layer] + [(D,), (1,)]
    parts = [jnp.stack([small_g[l][n].reshape(w[n].shape[1:]) for l in range(DEPTH)]) for n in per_layer]
    state_small, token_small = _all_reduce_begin(_pack_small(parts + [d_final_g.reshape(D), loss_part.reshape(1)]), "small")
    token = token + token_small

    grads, delta, new_m, new_v = {}, {}, {}, {}
    for n, (buf, g_index) in GRAD_BLOCKS.items():
        if buf == 'b':
            grads[n], delta[n], new_m[n], new_v[n] = _adamw_packed(red_b, w[n], mo[n], vo[n], g_index, f"adamw_{n}", token)
    red_a[0] = _reduce_end(state_a, new_v['w_xkv'])
    small_red = _unpack_small(_all_reduce_end(state_small, red_a[0]), small_shapes)
    grads.update(zip(per_layer + ['final_norm_g'], small_red[:-1]))
    loss = small_red[-1].reshape(())
    for n, (buf, g_index) in GRAD_BLOCKS.items():
        if buf == 'a':
            grads[n], delta[n], new_m[n], new_v[n] = _adamw_packed(red_a, w[n], mo[n], vo[n], g_index, f"adamw_{n}", token)
    g_t = jnp.stack([r[R_WIN:R_WIN + WIN_ROWS] for r in red_a], axis=1)
    upd = _adamw(g_t, _w_in_t(w['w_in']), _w_in_t(mo['w_in']), _w_in_t(vo['w_in']), "adamw_w_in", block=(WIN_ROWS, DEPTH, 128))
    grads['w_in'], delta['w_in'], new_m['w_in'], new_v['w_in'] = [jnp.transpose(a, (1, 2, 0)) for a in (g_t,) + tuple(upd)]
    small_all = per_layer + ['final_norm_g']
    shapes_all = [w[n].shape for n in small_all]
    packed = [_pack_small([d[n] for n in small_all])[None] for d in (grads, w, mo, vo)]
    ds, ms, vs = _adamw(*packed, "adamw_small")
    for n, a, b, c in zip(small_all, _unpack_small(ds[0], shapes_all), _unpack_small(ms[0], shapes_all), _unpack_small(vs[0], shapes_all)):
        delta[n], new_m[n], new_v[n] = a, b, c

    return (loss, grad_x, *[grads[n] for n in W_NAMES], *[delta[n] for n in W_NAMES], *[new_m[n] for n in W_NAMES],
            *[new_v[n] for n in W_NAMES])
```

```python
import math

import jax
import jax.numpy as jnp
from jax import lax
from jax.experimental import pallas as pl
from jax.experimental.pallas import tpu as pltpu

F32 = jnp.float32
BF16 = jnp.bfloat16

D = 1024
DEPTH = 2
POOL_W = 256
FOX_W = 512
SGU_W = 256
SGU_CHUNK = 128
N_IN = 5384
P_G, P_Q, P_K, P_V, P_C, P_A, P_F = 0, 3072, 3584, 4096, 4608, 5120, 5376
NP = 5632
XH, XHD = 4, 256
D_FF = 4096
EPS = 1e-6
NEG = -1e30
FOX_SCALE = 64 ** -0.5
X_SCALE = 256 ** -0.5
GELU_K = math.sqrt(2.0 / math.pi)
GELU_C = 0.044715

ADAM_LR, ADAM_B1, ADAM_B2, ADAM_EPS, ADAM_WD, ADAM_STEP = 0.001, 0.9, 0.999, 1e-08, 0.01, 10

VMEM_LIMIT = 48 * 1024 * 1024
MESH = pl.DeviceIdType.MESH

IN_NAMES = ['x', 'mem', 'norm_mix_g', 'w_in', 'b_forget', 'pool_w', 'pool_scale', 'sgu_norm_g', 'sgu_w', 'sgu_b',
            'w_branch_a', 'w_branch_b', 'w_branch_c', 'b_gate', 'w_out', 'norm_xattn_g', 'norm_mem_g', 'w_xq',
            'w_xkv', 'w_xo', 'norm_ffn_g', 'w_ff1', 'w_ff2', 'final_norm_g']
W_NAMES = IN_NAMES[2:]
BIG_NAMES = ['w_in', 'w_branch_a', 'w_branch_b', 'w_branch_c', 'w_out', 'w_xq', 'w_xkv', 'w_xo', 'w_ff1', 'w_ff2']
SMALL_NAMES = [n for n in W_NAMES if n not in BIG_NAMES]
PACK_COLS = 1024


ANY = pl.BlockSpec(memory_space=pl.ANY)


def _cp(sem=None):
    return pltpu.CompilerParams(dimension_semantics=sem, vmem_limit_bytes=VMEM_LIMIT)


def _mm(a, b, *, name, out_dtype, ta=False, tb=False, tm=1024, tn=512, tk=1024, a_fn=None, extra=None, epi=None,
        n=None, k=None, b_block=None, b_index=None, into=None, o_block=None, o_index=None):
    M = a.shape[1] if ta else a.shape[0]
    K = k if k is not None else (a.shape[0] if ta else a.shape[1])
    N = n if n is not None else (b.shape[0] if tb else b.shape[1])
    tm, tn, tk = min(tm, M), min(tn, N), min(tk, K)
    assert M % tm == 0 and N % tn == 0 and K % tk == 0, (name, M, N, K)
    nk = K // tk
    a_spec = pl.BlockSpec((tk, tm), lambda i, j, k: (k, i)) if ta else pl.BlockSpec((tm, tk), lambda i, j, k: (i, k))
    if b_block is not None:
        b_spec = pl.BlockSpec(b_block, b_index)
    else:
        b_spec = pl.BlockSpec((tn, tk), lambda i, j, k: (j, k)) if tb else pl.BlockSpec((tk, tn), lambda i, j, k: (k, j))
    dn = (((0 if ta else 1,), (1 if tb else 0,)), ((), ()))
    tile = pl.BlockSpec((tm, tn), lambda i, j, k: (i, j))
    o_spec = pl.BlockSpec(o_block, o_index) if into is not None else tile
    in_specs = [a_spec, b_spec] + ([tile] if extra is not None else []) + ([ANY] if into is not None else [])
    n_in = len(in_specs)

    def body(*refs):
        a_ref, b_ref = refs[0], refs[1]
        e_ref = refs[2] if extra is not None else None
        o_ref, acc_ref = refs[n_in], refs[n_in + 1]
        kk = pl.program_id(2)

        @pl.when(kk == 0)
        def _():
            acc_ref[...] = jnp.zeros_like(acc_ref)

        av = a_ref[...]
        if a_fn is not None:
            av = a_fn(av)
        bv = b_ref[...]
        if bv.ndim == 3:
            bv = bv.reshape(-1, bv.shape[-1])
        acc_ref[...] += lax.dot_general(av.astype(BF16), bv.astype(BF16), dn, preferred_element_type=F32)

        @pl.when(kk == nk - 1)
        def _():
            r = acc_ref[...]
            if epi is not None:
                r = epi(r, e_ref[...])
            o_ref[...] = r.astype(o_ref.dtype).reshape(o_ref.shape)

    args = (a, b) + ((extra,) if extra is not None else ()) + ((into,) if into is not None else ())
    out_shape = jax.ShapeDtypeStruct(into.shape, into.dtype) if into is not None else jax.ShapeDtypeStruct((M, N), out_dtype)
    return pl.pallas_call(
        body, out_shape=out_shape, grid=(M // tm, N // tn, nk), in_specs=in_specs, out_specs=o_spec,
        scratch_shapes=[pltpu.VMEM((tm, tn), F32)], input_output_aliases={n_in - 1: 0} if into is not None else {},
        compiler_params=_cp(("parallel", "parallel", "arbitrary")), name=name)(*args)


def _relu2(z):
    r = jnp.maximum(z, 0.0)
    return r * r


def _rms_fwd(x, g, name, tr=512):
    R, n = x.shape
    tr = min(tr, R)

    def body(x_ref, g_ref, h_ref):
        xv = x_ref[...]
        rstd = lax.rsqrt(jnp.mean(xv * xv, axis=-1, keepdims=True) + EPS)
        h_ref[...] = (xv * rstd * g_ref[...]).astype(BF16)

    return pl.pallas_call(
        body, out_shape=jax.ShapeDtypeStruct((R, n), BF16), grid=(R // tr,),
        in_specs=[pl.BlockSpec((tr, n), lambda i: (i, 0)), pl.BlockSpec((1, n), lambda i: (0, 0))],
        out_specs=pl.BlockSpec((tr, n), lambda i: (i, 0)), compiler_params=_cp(("parallel",)), name=name)(x, g)


def _rms_bwd(dh, x, g, dres, name, tr=512):
    R, n = x.shape
    tr = min(tr, R)
    need_dx = dres is not None

    def body(*refs):
        if need_dx:
            dh_ref, x_ref, g_ref, r_ref, dx_ref, dg_ref = refs
        else:
            dh_ref, x_ref, g_ref, dg_ref = refs
        i = pl.program_id(0)
        xv = x_ref[...]
        dhv = dh_ref[...].astype(F32)
        rstd = lax.rsqrt(jnp.mean(xv * xv, axis=-1, keepdims=True) + EPS)
        xhat = xv * rstd

        @pl.when(i == 0)
        def _():
            dg_ref[...] = jnp.zeros_like(dg_ref)

        dg_ref[...] += jnp.sum(dhv * xhat, axis=0, keepdims=True)
        if need_dx:
            t = dhv * g_ref[...]
            dx_ref[...] = r_ref[...] + rstd * (t - xhat * jnp.mean(t * xhat, axis=-1, keepdims=True))

    row = pl.BlockSpec((tr, n), lambda i: (i, 0))
    vec = pl.BlockSpec((1, n), lambda i: (0, 0))
    if need_dx:
        return pl.pallas_call(
            body, out_shape=(jax.ShapeDtypeStruct((R, n), F32), jax.ShapeDtypeStruct((1, n), F32)), grid=(R // tr,),
            in_specs=[row, row, vec, row], out_specs=(row, vec), compiler_params=_cp(("arbitrary",)), name=name)(dh, x, g, dres)
    return pl.pallas_call(
        body, out_shape=jax.ShapeDtypeStruct((1, n), F32), grid=(R // tr,),
        in_specs=[row, row, vec], out_specs=vec, compiler_params=_cp(("arbitrary",)), name=name)(dh, x, g)


def _loss_head(x, g, tgt, name, tr=512):
    R, n = x.shape

    def body(x_ref, g_ref, t_ref, loss_ref, dx_ref, dg_ref):
        i = pl.program_id(0)
        xv = x_ref[...]
        gv = g_ref[...]
        rstd = lax.rsqrt(jnp.mean(xv * xv, axis=-1, keepdims=True) + EPS)
        xhat = xv * rstd
        e = xhat * gv - t_ref[...]

        @pl.when(i == 0)
        def _():
            loss_ref[...] = jnp.zeros_like(loss_ref)
            dg_ref[...] = jnp.zeros_like(dg_ref)

        loss_ref[...] += 0.5 * jnp.sum(jnp.sum(e * e, axis=-1, keepdims=True) / n, axis=0, keepdims=True)
        dy = e / n
        dg_ref[...] += jnp.sum(dy * xhat, axis=0, keepdims=True)
        t = dy * gv
        dx_ref[...] = rstd * (t - xhat * jnp.mean(t * xhat, axis=-1, keepdims=True))

    row = pl.BlockSpec((tr, n), lambda i: (i, 0))
    vec = pl.BlockSpec((1, n), lambda i: (0, 0))
    one = pl.BlockSpec((1, 1), lambda i: (0, 0))
    return pl.pallas_call(
        body, out_shape=(jax.ShapeDtypeStruct((1, 1), F32), jax.ShapeDtypeStruct((R, n), F32), jax.ShapeDtypeStruct((1, n), F32)),
        grid=(R // tr,), in_specs=[row, vec, row], out_specs=(one, row, vec),
        compiler_params=_cp(("arbitrary",)), name=name)(x, g, tgt)


def _pool_masks(S):
    row = lax.broadcasted_iota(jnp.int32, (S, POOL_W), 0)
    grp = lax.broadcasted_iota(jnp.int32, (S, POOL_W), 1) // 64
    win = jnp.where(grp == 0, 2, jnp.where(grp == 1, 4, jnp.where(grp == 2, 8, 16)))
    cnt = jnp.minimum(row + 1, win).astype(F32)
    return row, grp, cnt


def _by_group(grp, v0, v1, v2, v3):
    return jnp.where(grp == 0, v0, jnp.where(grp == 1, v1, jnp.where(grp == 2, v2, v3)))


def _pool_fwd(proj, bd, scale, name):
    S = proj.shape[0]

    def body(a_ref, bd_ref, sc_ref, d_ref, y_ref):
        a = a_ref[...]
        row, grp, cnt = _pool_masks(S)

        def back(v, k):
            return jnp.where(row >= k, pltpu.roll(v, k, 0), 0.0)

        s1 = a + back(a, 1)
        s2 = s1 + back(s1, 2)
        s3 = s2 + back(s2, 4)
        s4 = s3 + back(s3, 8)
        d = (_by_group(grp, s1, s2, s3, s4) / cnt - a).astype(BF16)
        d_ref[...] = d
        y_ref[...] = (jnp.dot(d, bd_ref[...], preferred_element_type=F32) * sc_ref[...]).astype(BF16)

    full = lambda r, c: pl.BlockSpec((r, c), lambda i: (0, 0))
    return pl.pallas_call(
        body, out_shape=(jax.ShapeDtypeStruct((S, POOL_W), BF16), jax.ShapeDtypeStruct((S, POOL_W), BF16)), grid=(1,),
        in_specs=[pl.BlockSpec((S, POOL_W), lambda i: (0, P_A // POOL_W)), full(POOL_W, POOL_W), full(1, POOL_W)],
        out_specs=(full(S, POOL_W), full(S, POOL_W)), compiler_params=_cp(("arbitrary",)), name=name)(proj, bd, scale)


def _pool_bwd(dya, d, bd, scale, name):
    S = dya.shape[0]

    def body(dy_ref, d_ref, bd_ref, sc_ref, da_ref, dbd_ref, dsc_ref):
        dy = dy_ref[...]
        dv = d_ref[...]
        bdv = bd_ref[...]
        row, grp, cnt = _pool_masks(S)
        yraw = jnp.dot(dv, bdv, preferred_element_type=F32)
        dsc_ref[...] = jnp.sum(dy * yraw, axis=0, keepdims=True)
        tb = (dy * sc_ref[...]).astype(BF16)
        dbd_ref[...] = lax.dot_general(dv, tb, (((0,), (0,)), ((), ())), preferred_element_type=F32)
        dd = lax.dot_general(tb, bdv, (((1,), (1,)), ((), ())), preferred_element_type=F32)
        e = dd / cnt

        def fwd(v, k):
            return jnp.where(row < S - k, pltpu.roll(v, S - k, 0), 0.0)

        r1 = e + fwd(e, 1)
        r2 = r1 + fwd(r1, 2)
        r3 = r2 + fwd(r2, 4)
        r4 = r3 + fwd(r3, 8)
        da_ref[...] = (_by_group(grp, r1, r2, r3, r4) - dd).astype(BF16)

    full = lambda r, c: pl.BlockSpec((r, c), lambda i: (0, 0))
    return pl.pallas_call(
        body, out_shape=(jax.ShapeDtypeStruct((S, POOL_W), BF16), jax.ShapeDtypeStruct((POOL_W, POOL_W), F32),
                         jax.ShapeDtypeStruct((1, POOL_W), F32)), grid=(1,),
        in_specs=[full(S, POOL_W), full(S, POOL_W), full(POOL_W, POOL_W), full(1, POOL_W)],
        out_specs=(full(S, POOL_W), full(POOL_W, POOL_W), full(1, POOL_W)),
        compiler_params=_cp(("arbitrary",)), name=name)(dya, d, bd, scale)


FCOLS = 128


def _log_sigmoid(z):
    return -(jnp.maximum(-z, 0.0) + jnp.log1p(jnp.exp(-jnp.abs(z))))


def _fgate_fwd(proj, bf, name):
    S = proj.shape[0]

    def body(f_ref, b_ref, o_ref):
        v = _log_sigmoid(f_ref[...] + b_ref[...])
        row = lax.broadcasted_iota(jnp.int32, (S, FCOLS), 0)
        k = 1
        while k < S:
            v = v + jnp.where(row >= k, pltpu.roll(v, k, 0), 0.0)
            k *= 2
        o_ref[...] = v

    return pl.pallas_call(
        body, out_shape=jax.ShapeDtypeStruct((S, FCOLS), F32), grid=(1,),
        in_specs=[pl.BlockSpec((S, FCOLS), lambda i: (0, P_F // FCOLS)), pl.BlockSpec((1, FCOLS), lambda i: (0, 0))],
        out_specs=pl.BlockSpec((S, FCOLS), lambda i: (0, 0)), compiler_params=_cp(("arbitrary",)), name=name)(proj, bf)


def _fgate_bwd(dF, proj, bf, name):
    S = proj.shape[0]

    def body(dF_ref, f_ref, b_ref, df_ref, db_ref):
        v = dF_ref[...]
        row = lax.broadcasted_iota(jnp.int32, (S, FCOLS), 0)
        k = 1
        while k < S:
            v = v + jnp.where(row < S - k, pltpu.roll(v, S - k, 0), 0.0)
            k *= 2
        z = f_ref[...] + b_ref[...]
        df = v * (1.0 / (1.0 + jnp.exp(z)))
        db_ref[...] = jnp.sum(df, axis=0, keepdims=True)
        df_ref[...] = jnp.concatenate([df, jnp.zeros_like(df)], axis=1).astype(BF16)

    return pl.pallas_call(
        body, out_shape=(jax.ShapeDtypeStruct((S, 2 * FCOLS), BF16), jax.ShapeDtypeStruct((1, FCOLS), F32)), grid=(1,),
        in_specs=[pl.BlockSpec((S, FCOLS), lambda i: (0, 0)), pl.BlockSpec((S, FCOLS), lambda i: (0, P_F // FCOLS)),
                  pl.BlockSpec((1, FCOLS), lambda i: (0, 0))],
        out_specs=(pl.BlockSpec((S, 2 * FCOLS), lambda i: (0, 0)), pl.BlockSpec((1, FCOLS), lambda i: (0, 0))),
        compiler_params=_cp(("arbitrary",)), name=name)(dF, proj, bf)


def _fox_scores(qe, kj, fq, fk, r0, c0, tq, tk, diagonal):
    s = lax.dot_general(qe, kj, (((1,), (1,)), ((), ())), preferred_element_type=F32) * FOX_SCALE
    s = s + (fq - fk)
    if not diagonal:
        return s
    rows = r0 + lax.broadcasted_iota(jnp.int32, (tq, tk), 0)
    cols = c0 + lax.broadcasted_iota(jnp.int32, (tq, tk), 1)
    return jnp.where(rows >= cols, s, NEG)


FOX_TQ, FOX_TK = 512, 512


def _fox_fwd(qkv, fcol, frow, name):
    S = qkv.shape[0]
    tq, tk = FOX_TQ, min(FOX_TK, S)

    def body(q_ref, k_ref, v_ref, fc_ref, fr_ref, o_ref, o32_ref, lse_ref):
        i = pl.program_id(1)
        r0 = i * tq
        q = q_ref[...]
        half = lax.broadcasted_iota(jnp.int32, (tq, 128), 1) // 64
        qs = [jnp.where(half == e, q, jnp.zeros_like(q)) for e in (0, 1)]
        fqs = [fc_ref[0, :, e:e + 1] for e in (0, 1)]

        def step(j, carry, diagonal=False):
            c0 = pl.multiple_of(j * tk, tk)
            kj = k_ref[pl.ds(c0, tk), :]
            vj = v_ref[pl.ds(c0, tk), :]
            out = []
            for e in (0, 1):
                m, l, acc = carry[e]
                s = _fox_scores(qs[e], kj, fqs[e], fr_ref[0, e:e + 1, pl.ds(c0, tk)], r0, c0, tq, tk, diagonal)
                m_new = jnp.maximum(m, jnp.max(s, axis=-1, keepdims=True))
                alpha = jnp.exp(m - m_new)
                p = jnp.exp(s - m_new)
                out.append((m_new, alpha * l + jnp.sum(p, axis=-1, keepdims=True),
                            alpha * acc + jnp.dot(p.astype(BF16), vj, preferred_element_type=F32)))
            return tuple(out)

        init = (jnp.full((tq, 1), NEG, F32), jnp.zeros((tq, 1), F32), jnp.zeros((tq, 128), F32))
        below = r0 // tk
        carry = lax.fori_loop(0, below, step, (init, init))
        carry = step(below, carry, diagonal=True)
        outs = []
        for e in (0, 1):
            m, l, acc = carry[e]
            outs.append(acc / l)
            lse_ref[0, :, e:e + 1] = m + jnp.log(l)
        o = jnp.where(half == 0, outs[0], outs[1])
        o32_ref[...] = o
        o_ref[...] = o.astype(BF16)

    tile = pl.BlockSpec((tq, 128), lambda h, i: (i, h))
    return pl.pallas_call(
        body, out_shape=(jax.ShapeDtypeStruct((S, FOX_W), BF16), jax.ShapeDtypeStruct((S, FOX_W), F32),
                         jax.ShapeDtypeStruct((4, S, 2), F32)), grid=(4, S // tq),
        in_specs=[tile, pl.BlockSpec((S, 128), lambda h, i: (0, 4 + h)), pl.BlockSpec((S, 128), lambda h, i: (0, 8 + h)),
                  pl.BlockSpec((1, tq, 2), lambda h, i: (h, i, 0)), pl.BlockSpec((1, 2, S), lambda h, i: (h, 0, 0))],
        out_specs=(tile, tile, pl.BlockSpec((1, tq, 2), lambda h, i: (h, i, 0))),
        compiler_params=_cp(("parallel", "parallel")), name=name)(qkv, qkv, qkv, fcol, frow)


def _fox_bwd(qkv, o32, do, lse, fcol, frow, name):
    S = qkv.shape[0]
    tq, tk = FOX_TQ, min(FOX_TK, S)
    nq = S // tq

    def body(q_ref, k_ref, v_ref, o_ref, do_ref, lse_ref, fc_ref, fr_ref, dq_ref, dk_ref, dv_ref, dfr_ref, dfc_ref, dk_acc, dv_acc):
        dk_acc[...] = jnp.zeros_like(dk_acc)
        dv_acc[...] = jnp.zeros_like(dv_acc)
        dfr_ref[...] = jnp.zeros_like(dfr_ref)
        half = lax.broadcasted_iota(jnp.int32, (tq, 128), 1) // 64

        def q_block(i, _):
            r0 = pl.multiple_of(i * tq, tq)
            qi = q_ref[pl.ds(r0, tq), :]
            dob = do_ref[pl.ds(r0, tq), :].astype(BF16)
            row_dot = dob.astype(F32) * o_ref[pl.ds(r0, tq), :]
            qs = [jnp.where(half == e, qi, jnp.zeros_like(qi)) for e in (0, 1)]
            dos = [jnp.where(half == e, dob, jnp.zeros_like(dob)) for e in (0, 1)]
            deltas = [jnp.sum(jnp.where(half == e, row_dot, 0.0), axis=-1, keepdims=True) for e in (0, 1)]
            lses = [lse_ref[0, pl.ds(r0, tq), e:e + 1] for e in (0, 1)]
            fqs = [fc_ref[0, pl.ds(r0, tq), e:e + 1] for e in (0, 1)]

            def step(j, carry, diagonal=False):
                dqs, row_sums = carry
                c0 = pl.multiple_of(j * tk, tk)
                kj = k_ref[pl.ds(c0, tk), :]
                vj = v_ref[pl.ds(c0, tk), :]
                new_dq, new_rows, dkc, dvc = [], [], [], []
                for e in (0, 1):
                    s = _fox_scores(qs[e], kj, fqs[e], fr_ref[0, e:e + 1, pl.ds(c0, tk)], r0, c0, tq, tk, diagonal)
                    p = jnp.exp(s - lses[e])
                    dp = lax.dot_general(dos[e], vj, (((1,), (1,)), ((), ())), preferred_element_type=F32)
                    ds = p * (dp - deltas[e])
                    dfr_ref[0, e:e + 1, pl.ds(c0, tk)] -= jnp.sum(ds, axis=0, keepdims=True)
                    new_rows.append(row_sums[e] + jnp.sum(ds, axis=-1, keepdims=True))
                    dsb = (ds * FOX_SCALE).astype(BF16)
                    dkc.append(lax.dot_general(dsb, qi, (((0,), (0,)), ((), ())), preferred_element_type=F32))
                    dvc.append(lax.dot_general(p.astype(BF16), dob, (((0,), (0,)), ((), ())), preferred_element_type=F32))
                    new_dq.append(dqs[e] + jnp.dot(dsb, kj, preferred_element_type=F32))
                half_k = lax.broadcasted_iota(jnp.int32, (tk, 128), 1) // 64
                dk_acc[pl.ds(c0, tk), :] += jnp.where(half_k == 0, dkc[0], dkc[1])
                dv_acc[pl.ds(c0, tk), :] += jnp.where(half_k == 0, dvc[0], dvc[1])
                return tuple(new_dq), tuple(new_rows)

            zero, zero_col = jnp.zeros((tq, 128), F32), jnp.zeros((tq, 1), F32)
            below = r0 // tk
            carry = lax.fori_loop(0, below, step, ((zero, zero), (zero_col, zero_col)))
            dqs, row_sums = step(below, carry, diagonal=True)
            for e in (0, 1):
                dfc_ref[0, pl.ds(r0, tq), e:e + 1] = row_sums[e]
            dq_ref[pl.ds(r0, tq), :] = jnp.where(half == 0, dqs[0], dqs[1]).astype(BF16)
            return 0

        lax.fori_loop(0, nq, q_block, 0)
        dk_ref[...] = dk_acc[...].astype(BF16)
        dv_ref[...] = dv_acc[...].astype(BF16)

    col = lambda off: pl.BlockSpec((S, 128), lambda h: (0, off + h))
    hs2 = pl.BlockSpec((1, S, 2), lambda h: (h, 0, 0))
    h2s = pl.BlockSpec((1, 2, S), lambda h: (h, 0, 0))
    return pl.pallas_call(
        body, out_shape=(jax.ShapeDtypeStruct((S, FOX_W), BF16),) * 3 + (jax.ShapeDtypeStruct((4, 2, S), F32),
                                                                         jax.ShapeDtypeStruct((4, S, 2), F32)), grid=(4,),
        in_specs=[col(0), col(4), col(8), col(0), col(0), hs2, hs2, h2s],
        out_specs=(col(0), col(0), col(0), h2s, hs2),
        scratch_shapes=[pltpu.VMEM((S, 128), F32), pltpu.VMEM((S, 128), F32)],
        compiler_params=_cp(("parallel",)), name=name)(qkv, qkv, qkv, o32, do, lse, fcol, frow)


def _gelu(x):
    return 0.5 * x * (1.0 + jnp.tanh(GELU_K * (x + GELU_C * x * x * x)))


def _gelu_grad(x):
    th = jnp.tanh(GELU_K * (x + GELU_C * x * x * x))
    return 0.5 * (1.0 + th) + 0.5 * x * (1.0 - th * th) * GELU_K * (1.0 + 3.0 * GELU_C * x * x)


def _sgu_parts(c, gn, w_ref, bias):
    zc = _gelu(c)
    u, vv = zc[:, :SGU_W], zc[:, SGU_W:]
    rstd = lax.rsqrt(jnp.mean(vv * vv, axis=-1, keepdims=True) + EPS)
    vhat = vv * rstd
    vnb = (vhat * gn).astype(BF16)
    grp = lax.broadcasted_iota(jnp.int32, (SGU_CHUNK, SGU_W), 1) // 64
    mixed = bias
    for gi in range(4):
        mixed = mixed + jnp.where(grp == gi, jnp.dot(w_ref[gi], vnb, preferred_element_type=F32), 0.0)
    return u, rstd, vhat, vnb, grp, mixed


def _sgu_fwd(proj, gn, wm, bias, name):
    S = proj.shape[0]

    def body(c_ref, g_ref, w_ref, b_ref, o_ref):
        u, _, _, _, _, mixed = _sgu_parts(c_ref[...], g_ref[...], w_ref, b_ref[...])
        o_ref[...] = (u * mixed).astype(BF16)

    return pl.pallas_call(
        body, out_shape=jax.ShapeDtypeStruct((S, SGU_W), BF16), grid=(S // SGU_CHUNK,),
        in_specs=[pl.BlockSpec((SGU_CHUNK, 2 * SGU_W), lambda i: (i, P_C // (2 * SGU_W))),
                  pl.BlockSpec((1, SGU_W), lambda i: (0, 0)), pl.BlockSpec((4, SGU_CHUNK, SGU_CHUNK), lambda i: (0, 0, 0)),
                  pl.BlockSpec((SGU_CHUNK, SGU_W), lambda i: (0, 0))],
        out_specs=pl.BlockSpec((SGU_CHUNK, SGU_W), lambda i: (i, 0)),
        compiler_params=_cp(("parallel",)), name=name)(proj, gn, wm, bias)


def _sgu_bwd(dsg, proj, gn, wm, wmt, bias, name):
    S = proj.shape[0]

    def body(dsg_ref, c_ref, g_ref, w_ref, wt_ref, b_ref, dc_ref, dw_ref, db_ref, dg_ref):
        i = pl.program_id(0)

        @pl.when(i == 0)
        def _():
            dw_ref[...] = jnp.zeros_like(dw_ref)
            db_ref[...] = jnp.zeros_like(db_ref)
            dg_ref[...] = jnp.zeros_like(dg_ref)

        c = c_ref[...]
        gn_v = g_ref[...]
        u, rstd, vhat, vnb, grp, mixed = _sgu_parts(c, gn_v, w_ref, b_ref[...])
        dsg_v = dsg_ref[...]
        du = dsg_v * mixed
        dmix = dsg_v * u
        db_ref[...] += dmix
        dmb = dmix.astype(BF16)
        dvn = jnp.zeros((SGU_CHUNK, SGU_W), F32)
        for gi in range(4):
            dmg = jnp.where(grp == gi, dmb, jnp.zeros_like(dmb))
            dw_ref[gi] += lax.dot_general(dmg, vnb, (((1,), (1,)), ((), ())), preferred_element_type=F32)
            dvn = dvn + jnp.where(grp == gi, jnp.dot(wt_ref[gi], dmb, preferred_element_type=F32), 0.0)
        dg_ref[...] += jnp.sum(dvn * vhat, axis=0, keepdims=True)
        t = dvn * gn_v
        dvv = rstd * (t - vhat * jnp.mean(t * vhat, axis=-1, keepdims=True))
        dc_ref[...] = (jnp.concatenate([du, dvv], axis=1) * _gelu_grad(c)).astype(BF16)

    w_spec = pl.BlockSpec((4, SGU_CHUNK, SGU_CHUNK), lambda i: (0, 0, 0))
    tile = pl.BlockSpec((SGU_CHUNK, SGU_W), lambda i: (0, 0))
    vec = pl.BlockSpec((1, SGU_W), lambda i: (0, 0))
    return pl.pallas_call(
        body, out_shape=(jax.ShapeDtypeStruct((S, 2 * SGU_W), BF16), jax.ShapeDtypeStruct((4, SGU_CHUNK, SGU_CHUNK), F32),
                         jax.ShapeDtypeStruct((SGU_CHUNK, SGU_W), F32), jax.ShapeDtypeStruct((1, SGU_W), F32)),
        grid=(S // SGU_CHUNK,),
        in_specs=[pl.BlockSpec((SGU_CHUNK, SGU_W), lambda i: (i, 0)),
                  pl.BlockSpec((SGU_CHUNK, 2 * SGU_W), lambda i: (i, P_C // (2 * SGU_W))), vec, w_spec, w_spec, tile],
        out_specs=(pl.BlockSpec((SGU_CHUNK, 2 * SGU_W), lambda i: (i, 0)), w_spec, tile, vec),
        compiler_params=_cp(("arbitrary",)), name=name)(dsg, proj, gn, wm, wmt, bias)


def _sigmoid(z):
    return 1.0 / (1.0 + jnp.exp(-z))


def _merge_specs(tm):
    row = lambda n: pl.BlockSpec((tm, n), lambda i: (i, 0))
    gate = lambda b: pl.BlockSpec((tm, D), lambda i: (i, b))
    full = lambda r, c: pl.BlockSpec((r, c), lambda i: (0, 0))
    packed = pl.BlockSpec((4, 256, PACK_COLS), lambda i: (0, R_BRANCH // 256, 0))
    return row, gate, full, packed


def _branch_shards(c_ref, j):
    return c_ref[j, :, 0:256], c_ref[j, :, 256:512], c_ref[j, :, 512:768], c_ref[j, :, 768:1024]


def _merge_fwd(proj, ya, o, sg, packed_w, bg, name, tm=512):
    S = proj.shape[0]
    row, gate, full, packed = _merge_specs(tm)

    def body(g0, g1, g2, ya_ref, o_ref, sg_ref, c_ref, bg_ref, out_ref):
        yav, ov, sgv = ya_ref[...], o_ref[...], sg_ref[...]
        for j in range(4):
            cols = slice(256 * j, 256 * (j + 1))
            wa, wb0, wb1, wc = _branch_shards(c_ref, j)
            y = (jnp.dot(yav, wa, preferred_element_type=F32),
                 jnp.dot(ov[:, :256], wb0, preferred_element_type=F32) + jnp.dot(ov[:, 256:], wb1, preferred_element_type=F32),
                 jnp.dot(sgv, wc, preferred_element_type=F32))
            acc = jnp.zeros((tm, 256), F32)
            for b, g_ref in enumerate((g0, g1, g2)):
                acc = acc + _sigmoid(g_ref[:, cols] + bg_ref[:, b * D + 256 * j:b * D + 256 * (j + 1)]) * y[b]
            out_ref[:, cols] = acc.astype(BF16)

    return pl.pallas_call(
        body, out_shape=jax.ShapeDtypeStruct((S, D), BF16), grid=(S // tm,),
        in_specs=[gate(0), gate(1), gate(2), row(POOL_W), row(FOX_W), row(SGU_W), packed, full(1, 3 * D)],
        out_specs=row(D), compiler_params=_cp(("parallel",)), name=name)(proj, proj, proj, ya, o, sg, packed_w, bg)


def _merge_bwd(dm, proj, ya, o, sg, packed_w, bg, grads, name, tm=512):
    S = proj.shape[0]
    row, gate, full, packed = _merge_specs(tm)
    tn_dims = (((0,), (0,)), ((), ()))
    nt_dims = (((1,), (1,)), ((), ()))

    def body(dm_ref, g0, g1, g2, ya_ref, o_ref, sg_ref, c_ref, bg_ref, _, dg_ref, dya_ref, do_ref, dsg_ref, dc_ref, dbg_ref, acc):
        i = pl.program_id(0)

        @pl.when(i == 0)
        def _():
            acc[...] = jnp.zeros_like(acc)
            dbg_ref[...] = jnp.zeros_like(dbg_ref)

        yav, ov, sgv = ya_ref[...], o_ref[...], sg_ref[...]
        o0, o1 = ov[:, :256], ov[:, 256:]
        dya = jnp.zeros((tm, POOL_W), F32)
        do0 = jnp.zeros((tm, 256), F32)
        do1 = jnp.zeros((tm, 256), F32)
        dsg = jnp.zeros((tm, SGU_W), F32)
        for j in range(4):
            cols = slice(256 * j, 256 * (j + 1))
            wa, wb0, wb1, wc = _branch_shards(c_ref, j)
            y = (jnp.dot(yav, wa, preferred_element_type=F32),
                 jnp.dot(o0, wb0, preferred_element_type=F32) + jnp.dot(o1, wb1, preferred_element_type=F32),
                 jnp.dot(sgv, wc, preferred_element_type=F32))
            dmv = dm_ref[:, cols]
            dy = []
            for b, g_ref in enumerate((g0, g1, g2)):
                bcols = slice(b * D + 256 * j, b * D + 256 * (j + 1))
                gt = _sigmoid(g_ref[:, cols] + bg_ref[:, bcols])
                dgp = dmv * y[b] * gt * (1.0 - gt)
                dg_ref[:, bcols] = dgp.astype(BF16)
                dbg_ref[:, bcols] += jnp.sum(dgp, axis=0, keepdims=True)
                dy.append((dmv * gt).astype(BF16))
            dya = dya + lax.dot_general(dy[0], wa, nt_dims, preferred_element_type=F32)
            do0 = do0 + lax.dot_general(dy[1], wb0, nt_dims, preferred_element_type=F32)
            do1 = do1 + lax.dot_general(dy[1], wb1, nt_dims, preferred_element_type=F32)
            dsg = dsg + lax.dot_general(dy[2], wc, nt_dims, preferred_element_type=F32)
            acc[j, :, 0:256] += lax.dot_general(yav, dy[0], tn_dims, preferred_element_type=F32)
            acc[j, :, 256:512] += lax.dot_general(o0, dy[1], tn_dims, preferred_element_type=F32)
            acc[j, :, 512:768] += lax.dot_general(o1, dy[1], tn_dims, preferred_element_type=F32)
            acc[j, :, 768:1024] += lax.dot_general(sgv, dy[2], tn_dims, preferred_element_type=F32)
        dya_ref[...] = dya
        do_ref[:, :256] = do0
        do_ref[:, 256:] = do1
        dsg_ref[...] = dsg

        @pl.when(i == pl.num_programs(0) - 1)
        def _():
            dc_ref[...] = acc[...].astype(dc_ref.dtype)

    return pl.pallas_call(
        body, out_shape=(jax.ShapeDtypeStruct((S, 3 * D), BF16), jax.ShapeDtypeStruct((S, POOL_W), F32),
                         jax.ShapeDtypeStruct((S, FOX_W), F32), jax.ShapeDtypeStruct((S, SGU_W), F32),
                         jax.ShapeDtypeStruct(grads.shape, grads.dtype), jax.ShapeDtypeStruct((1, 3 * D), F32)),
        grid=(S // tm,),
        in_specs=[row(D), gate(0), gate(1), gate(2), row(POOL_W), row(FOX_W), row(SGU_W), packed, full(1, 3 * D), ANY],
        out_specs=(row(3 * D), row(POOL_W), row(FOX_W), row(SGU_W), packed, full(1, 3 * D)),
        scratch_shapes=[pltpu.VMEM((4, 256, PACK_COLS), F32)], input_output_aliases={9: 4},
        compiler_params=_cp(("arbitrary",)), name=name)(dm, proj, proj, proj, ya, o, sg, packed_w, bg, grads)


def _xattn_probs(qh, kh):
    s = lax.dot_general(qh, kh, (((1,), (1,)), ((), ())), preferred_element_type=F32) * X_SCALE
    p = jnp.exp(s - jnp.max(s, axis=-1, keepdims=True))
    return p / jnp.sum(p, axis=-1, keepdims=True)


def _xattn_fwd(xq, kv, name, tq=512):
    S = xq.shape[0]
    M = kv.shape[0]

    def body(q_ref, k_ref, v_ref, o_ref):
        for h in range(XH):
            sl = slice(h * XHD, (h + 1) * XHD)
            p = _xattn_probs(q_ref[:, sl], k_ref[:, sl])
            o_ref[:, sl] = jnp.dot(p.astype(BF16), v_ref[:, sl], preferred_element_type=F32).astype(BF16)

    return pl.pallas_call(
        body, out_shape=jax.ShapeDtypeStruct((S, D), BF16), grid=(S // tq,),
        in_specs=[pl.BlockSpec((tq, D), lambda i: (i, 0)), pl.BlockSpec((M, D), lambda i: (0, 0)),
                  pl.BlockSpec((M, D), lambda i: (0, 1))],
        out_specs=pl.BlockSpec((tq, D), lambda i: (i, 0)), compiler_params=_cp(("parallel",)), name=name)(xq, kv, kv)


def _xattn_bwd(xq, kv, do, name, tq=512):
    S = xq.shape[0]
    M = kv.shape[0]

    def body(q_ref, k_ref, v_ref, do_ref, dq_ref, dkv_ref, dk_acc, dv_acc):
        i = pl.program_id(0)

        @pl.when(i == 0)
        def _():
            dk_acc[...] = jnp.zeros_like(dk_acc)
            dv_acc[...] = jnp.zeros_like(dv_acc)

        for h in range(XH):
            sl = slice(h * XHD, (h + 1) * XHD)
            qh, kh, vh, doh = q_ref[:, sl], k_ref[:, sl], v_ref[:, sl], do_ref[:, sl]
            p = _xattn_probs(qh, kh)
            dp = lax.dot_general(doh, vh, (((1,), (1,)), ((), ())), preferred_element_type=F32)
            ds = p * (dp - jnp.sum(p * dp, axis=-1, keepdims=True))
            dsb = (ds * X_SCALE).astype(BF16)
            dq_ref[:, sl] = jnp.dot(dsb, kh, preferred_element_type=F32).astype(BF16)
            dk_acc[:, sl] += lax.dot_general(dsb, qh, (((0,), (0,)), ((), ())), preferred_element_type=F32)
            dv_acc[:, sl] += lax.dot_general(p.astype(BF16), doh, (((0,), (0,)), ((), ())), preferred_element_type=F32)

        @pl.when(i == pl.num_programs(0) - 1)
        def _():
            dkv_ref[:, :D] = dk_acc[...].astype(BF16)
            dkv_ref[:, D:] = dv_acc[...].astype(BF16)

    return pl.pallas_call(
        body, out_shape=(jax.ShapeDtypeStruct((S, D), BF16), jax.ShapeDtypeStruct((M, 2 * D), BF16)), grid=(S // tq,),
        in_specs=[pl.BlockSpec((tq, D), lambda i: (i, 0)), pl.BlockSpec((M, D), lambda i: (0, 0)),
                  pl.BlockSpec((M, D), lambda i: (0, 1)), pl.BlockSpec((tq, D), lambda i: (i, 0))],
        out_specs=(pl.BlockSpec((tq, D), lambda i: (i, 0)), pl.BlockSpec((M, 2 * D), lambda i: (0, 0))),
        scratch_shapes=[pltpu.VMEM((M, D), F32), pltpu.VMEM((M, D), F32)],
        compiler_params=_cp(("arbitrary",)), name=name)(xq, kv, kv, do)


def _adam_math(gv, wv, mv, vv):
    c1 = 1.0 - ADAM_B1 ** ADAM_STEP
    c2 = 1.0 - ADAM_B2 ** ADAM_STEP
    nm = ADAM_B1 * mv + (1.0 - ADAM_B1) * gv
    nv = ADAM_B2 * vv + (1.0 - ADAM_B2) * (gv * gv)
    return -ADAM_LR * ((nm / c1) / (jnp.sqrt(nv / c2) + ADAM_EPS) + ADAM_WD * wv), nm, nv


def _adamw(g, w, m, v, name, block=None):
    if block is None:
        block = (1, 256 if g.shape[1] % 256 == 0 else g.shape[1], g.shape[2])
    grid = tuple(s // b for s, b in zip(g.shape, block))

    def body(g_ref, w_ref, m_ref, v_ref, d_ref, nm_ref, nv_ref):
        d_ref[...], nm_ref[...], nv_ref[...] = _adam_math(g_ref[...], w_ref[...], m_ref[...], v_ref[...])

    blk = pl.BlockSpec(block, lambda a, b, c: (a, b, c))
    return pl.pallas_call(
        body, out_shape=(jax.ShapeDtypeStruct(g.shape, F32),) * 3, grid=grid,
        in_specs=[blk] * 4, out_specs=(blk,) * 3, compiler_params=_cp(("parallel",) * 3), name=name)(g, w, m, v)


def _adamw_packed(red, w, m, v, g_index, name, token, tr=256):
    L, r, c = w.shape
    tr = min(tr, r)

    def body(g0_ref, g1_ref, w_ref, m_ref, v_ref, _, g_ref, d_ref, nm_ref, nv_ref):
        gv = jnp.where(pl.program_id(0) == 0, g0_ref[...], g1_ref[...])
        g_ref[0] = gv
        d_ref[0], nm_ref[0], nv_ref[0] = _adam_math(gv, w_ref[0], m_ref[0], v_ref[0])

    gblk = pl.BlockSpec((tr, c), lambda l, i: g_index(i))
    blk = pl.BlockSpec((1, tr, c), lambda l, i: (l, i, 0))
    return pl.pallas_call(
        body, out_shape=(jax.ShapeDtypeStruct(w.shape, F32),) * 4, grid=(L, r // tr),
        in_specs=[gblk, gblk, blk, blk, blk, pl.BlockSpec((8, 128), lambda l, i: (0, 0))], out_specs=(blk,) * 4,
        compiler_params=_cp(("parallel", "parallel")), name=name)(red[0], red[1], w, m, v, token)


def _row_tile(R):
    return next((t for t in (512, 496, 384, 256) if R % t == 0), R)


def _sum_slots(a, out_dtype, name):
    n, R, C = a.shape
    tr = _row_tile(R)

    def body(a_ref, o_ref):
        acc = a_ref[0].astype(F32)
        for k in range(1, n):
            acc = acc + a_ref[k].astype(F32)
        o_ref[...] = acc.astype(out_dtype)

    return pl.pallas_call(
        body, out_shape=jax.ShapeDtypeStruct((R, C), out_dtype), grid=(R // tr,),
        in_specs=[pl.BlockSpec((n, tr, C), lambda i: (0, i, 0))], out_specs=pl.BlockSpec((tr, C), lambda i: (i, 0)),
        compiler_params=_cp(("parallel",)), name=name)(a)


LANDING = pl.BlockSpec(memory_space=pltpu.VMEM)


def _landing_params(shape, dtype):
    return pltpu.CompilerParams(vmem_limit_bytes=math.prod(shape) * jnp.dtype(dtype).itemsize + 4 * 1024 * 1024)


def _place():
    return lax.axis_index("x"), lax.axis_index("y"), lax.axis_index("c")


def _other_chips(x, y):
    return [(1 - x, y), (x, 1 - y), (1 - x, 1 - y)]


def _row_chunks(rows, want, align=16):
    n = want
    while n > 1 and rows % (n * align):
        n -= 1
    return n


def _pair_add(g, name, nch=5):
    n, R, C = g.shape
    half = R // 2
    nch = _row_chunks(half, nch)
    cr = half // nch
    rb = next(t for t in (512, 256, 128, 64, 32, 16) if half % t == 0)

    def body(g_ref, p_ref, got, send_sems, recv_sems, local_sem):
        x, y, c = _place()
        mine0 = pl.multiple_of(c * half, 16)
        theirs0 = (1 - c) * half
        keep = pltpu.make_async_copy(g_ref.at[:, pl.ds(mine0, half), :], p_ref, local_sem)
        keep.start()
        cps = []
        for s in range(n):
            for q in range(nch):
                src = g_ref.at[s, pl.ds(pl.multiple_of(theirs0 + q * cr, 16), cr), :]
                cps.append(pltpu.make_async_remote_copy(
                    src_ref=src, dst_ref=got.at[s, pl.ds(q * cr, cr), :], send_sem=send_sems.at[s * nch + q],
                    recv_sem=recv_sems.at[s * nch + q], device_id=(x, y, 1 - c), device_id_type=MESH))
        for cp in cps:
            cp.start()
        for cp in cps:
            cp.wait()
        keep.wait()

        def add(i, _):
            rows = pl.ds(pl.multiple_of(i * rb, rb), rb)
            for s in range(n):
                p_ref[s, rows, :] = (p_ref[s, rows, :].astype(F32) + got[s, rows, :].astype(F32)).astype(BF16)
            return 0

        lax.fori_loop(0, half // rb, add, 0)

    shape = (n, half, C)
    return pl.pallas_call(
        body, out_shape=jax.ShapeDtypeStruct(shape, g.dtype), in_specs=[ANY], out_specs=LANDING,
        scratch_shapes=[pltpu.VMEM(shape, g.dtype), pltpu.SemaphoreType.DMA((n * nch,)), pltpu.SemaphoreType.DMA((n * nch,)),
                        pltpu.SemaphoreType.DMA],
        compiler_params=_landing_params((2,) + shape, g.dtype), name=name)(g)


def _pair_gather(t, name, nch=10):
    R = t.shape[0]
    nch = _row_chunks(R, nch, 8)
    cr = R // nch

    def body(t_ref, o_ref, send_sems, recv_sems, local_sem):
        x, y, c = _place()
        own = pltpu.make_async_copy(t_ref, o_ref.at[c], local_sem)
        own.start()
        cps = [pltpu.make_async_remote_copy(src_ref=t_ref.at[pl.ds(q * cr, cr), :], dst_ref=o_ref.at[c, pl.ds(q * cr, cr), :],
                                            send_sem=send_sems.at[q], recv_sem=recv_sems.at[q], device_id=(x, y, 1 - c),
                                            device_id_type=MESH) for q in range(nch)]
        for cp in cps:
            cp.start()
        for cp in cps:
            cp.wait()
        own.wait()

    return pl.pallas_call(
        body, out_shape=jax.ShapeDtypeStruct((2,) + t.shape, t.dtype), in_specs=[ANY], out_specs=LANDING,
        scratch_shapes=[pltpu.SemaphoreType.DMA((nch,)), pltpu.SemaphoreType.DMA((nch,)), pltpu.SemaphoreType.DMA],
        compiler_params=_landing_params((2,) + t.shape, t.dtype), name=name)(t)


HBM = pl.BlockSpec(memory_space=pltpu.HBM)
SEM = pl.BlockSpec(memory_space=pltpu.SEMAPHORE)
SPLIT_COPY = pltpu.CompilerParams(has_side_effects=pltpu.SideEffectType.DATAFLOW_SIDE_EFFECTING)


def _split_exchange(src, rows, src_of, tag, nch=5):
    C = src.shape[-1]
    nch = _row_chunks(rows, nch)
    cr = rows // nch
    n = 3 * nch
    land_shape = (4, rows, C)

    def copies(src_ref, land_ref, send_sems, recv_sems):
        x, y, c = _place()
        j = 2 * x + y
        out = []
        for q in range(nch):
            for k, (px, py) in enumerate(_other_chips(x, y)):
                out.append(pltpu.make_async_remote_copy(
                    src_ref=src_of(src_ref, px, py, c, q * cr, cr), dst_ref=land_ref.at[j, pl.ds(q * cr, cr), :],
                    send_sem=send_sems.at[k * nch + q], recv_sem=recv_sems.at[k * nch + q], device_id=(px, py, c),
                    device_id_type=MESH))
        return out

    def start(src_ref, land_ref, send_sems, recv_sems, src_thru, land_thru, token):
        for cp in copies(src_ref, land_ref, send_sems, recv_sems):
            cp.start()
        token[...] = jnp.zeros_like(token)

    send_sems, recv_sems, src_thru, land_thru, token = pl.pallas_call(
        start, name=f"{tag}_start",
        out_shape=(pltpu.SemaphoreType.DMA((n,)), pltpu.SemaphoreType.DMA((n,)), pltpu.HBM(src.shape, src.dtype),
                   pltpu.HBM(land_shape, src.dtype), jax.ShapeDtypeStruct((8, 128), F32)),
        in_specs=(HBM, HBM), out_specs=(SEM, SEM, HBM, HBM, pl.BlockSpec(memory_space=pltpu.VMEM)),
        input_output_aliases={0: 2, 1: 3}, compiler_params=SPLIT_COPY)(
            pltpu.with_memory_space_constraint(src, pltpu.HBM),
            pltpu.with_memory_space_constraint(lax.empty(land_shape, src.dtype), pltpu.HBM))

    def finish(after):
        def wait(src_ref, land_ref, send_sems, recv_sems, after_ref, src_dead, got_ref):
            for cp in copies(src_ref, land_ref, send_sems, recv_sems):
                cp.wait_send()
                cp.wait_recv()

        return pl.pallas_call(
            wait, name=f"{tag}_wait", out_shape=(pltpu.HBM(src.shape, src.dtype), pltpu.HBM(land_shape, src.dtype)),
            in_specs=(HBM, HBM, SEM, SEM, ANY), out_specs=(HBM, HBM), input_output_aliases={0: 0, 1: 1},
            compiler_params=SPLIT_COPY)(src_thru, land_thru, send_sems, recv_sems, after)

    return token, finish


def _gather_finish(shard, land, name, nch=5):
    R, C = shard.shape
    half = R // 2
    nch = _row_chunks(half, nch)
    cr = half // nch

    def body(s_ref, l_ref, o_ref, send_sems, recv_sems, local_sems):
        x, y, c = _place()
        j = 2 * x + y
        mine0 = c * half
        local = [pltpu.make_async_copy(s_ref, o_ref.at[j], local_sems.at[0])]
        remote = []
        for k, (px, py) in enumerate(_other_chips(x, y)):
            jj = 2 * px + py
            local.append(pltpu.make_async_copy(l_ref.at[jj], o_ref.at[jj, pl.ds(pl.multiple_of(mine0, 16), half), :],
                                               local_sems.at[1 + k]))
            for q in range(nch):
                remote.append(pltpu.make_async_remote_copy(
                    src_ref=l_ref.at[jj, pl.ds(q * cr, cr), :],
                    dst_ref=o_ref.at[jj, pl.ds(pl.multiple_of(mine0 + q * cr, 16), cr), :], send_sem=send_sems.at[k * nch + q],
                    recv_sem=recv_sems.at[k * nch + q], device_id=(x, y, 1 - c), device_id_type=MESH))
        for cp in local + remote:
            cp.start()
        for cp in remote + local:
            cp.wait()

    return pl.pallas_call(
        body, out_shape=jax.ShapeDtypeStruct((4, R, C), shard.dtype), in_specs=[ANY, ANY], out_specs=LANDING,
        scratch_shapes=[pltpu.SemaphoreType.DMA((3 * nch,)), pltpu.SemaphoreType.DMA((3 * nch,)), pltpu.SemaphoreType.DMA((4,))],
        compiler_params=_landing_params((4, R, C), shard.dtype), name=name)(shard, land)


def _sum_slots_own(land, own, name):
    n, R, C = land.shape
    tr = _row_tile(R)
    me = (2 * lax.axis_index("x") + lax.axis_index("y")).astype(jnp.int32).reshape(1)
    if own.ndim == 3:
        own_spec = pl.BlockSpec((None, tr, C), lambda i, me: (me[0], i, 0))
    else:
        own_spec = pl.BlockSpec((tr, C), lambda i, me: (i, 0))

    def body(me_ref, land_ref, own_ref, o_ref):
        acc = None
        for k in range(n):
            v = jnp.where(me_ref[0] == k, own_ref[...], land_ref[k]).astype(F32)
            acc = v if acc is None else acc + v
        o_ref[...] = acc

    return pl.pallas_call(
        body, out_shape=jax.ShapeDtypeStruct((R, C), F32),
        grid_spec=pltpu.PrefetchScalarGridSpec(
            num_scalar_prefetch=1, grid=(R // tr,),
            in_specs=[pl.BlockSpec((n, tr, C), lambda i, me: (0, i, 0)), own_spec],
            out_specs=pl.BlockSpec((tr, C), lambda i, me: (i, 0))),
        compiler_params=_cp(("parallel",)), name=name)(me, land, own)


def _reduce_begin(g, tag):
    p = _pair_add(g, f"rs_pair_{tag}")
    token, finish = _split_exchange(p, p.shape[1], lambda ref, px, py, c, r0, cr: ref.at[2 * px + py, pl.ds(r0, cr), :],
                                    f"rs_a2a_{tag}")
    return (finish, g.shape, tag), token


def _reduce_end(state, after):
    finish, shape, tag = state
    p, land = finish(after)
    t = _sum_slots_own(land, p, f"rs_sum_{tag}")
    return _pair_gather(t, f"rs_join_{tag}").reshape(shape[1], shape[2])


def _all_reduce_begin(v, tag):
    p = _sum_slots(_pair_gather(v, f"ar_pair_{tag}"), F32, f"ar_add_{tag}")
    token, finish = _split_exchange(p, p.shape[0], lambda ref, px, py, c, r0, cr: ref.at[pl.ds(r0, cr), :], f"ar_a2a_{tag}")
    return (finish, tag), token


def _all_reduce_end(state, after):
    finish, tag = state
    p, land = finish(after)
    return _sum_slots_own(land, p, f"ar_sum_{tag}")


def _gather_begin(shard, tag):
    half = shard.shape[0] // 2
    token, finish = _split_exchange(
        shard, half, lambda ref, px, py, c, r0, cr: ref.at[pl.ds(pl.multiple_of(c * half + r0, 16), cr), :], f"gather_{tag}")
    return (finish, tag), token


def _gather_end(state, after):
    finish, tag = state
    shard, land = finish(after)
    return _gather_finish(shard, land, f"gather_{tag}_finish")


R_BRANCH, R_OUT, R_WIN, ROWS_A = 0, 256, 512, 1888
R_FF1, R_FF2, R_XKV, R_XQ, R_XO, ROWS_B = 0, 1024, 2048, 2560, 2816, 3072
WIN_ROWS = N_IN // 4


def _w_in_t(a):
    return jnp.transpose(a, (2, 0, 1))


def _pack_shard(w, l):
    xkv, wb = w['w_xkv'][l], w['w_branch_b'][l]
    a = [jnp.concatenate([w['w_branch_a'][l], wb[:256], wb[256:], w['w_branch_c'][l]], axis=1), w['w_out'][l],
         jnp.pad(_w_in_t(w['w_in'])[:, l, :], ((0, ROWS_A - R_WIN - WIN_ROWS), (0, 0)))]
    b = [w['w_ff1'][l], w['w_ff2'][l], jnp.concatenate([xkv[:512], xkv[512:]], axis=1), w['w_xq'][l], w['w_xo'][l]]
    return jnp.concatenate(a, axis=0).astype(BF16), jnp.concatenate(b, axis=0).astype(BF16)


def _w_in_rows(gathered):
    t = gathered[:, R_WIN:R_WIN + WIN_ROWS, :].reshape(N_IN, PACK_COLS)
    return jnp.concatenate([t[2312:5384], t[256:1792], t[1800:2312], t[0:256],
                            jnp.pad(t[1792:1800], ((0, NP - P_F - 8), (0, 0)))], axis=0)


def _w_in_grad_rows(grads, dwt):
    t = jnp.concatenate([dwt[P_A:P_A + 256], dwt[P_Q:P_Q + 1536], dwt[P_F:P_F + 8], dwt[P_C:P_C + 512], dwt[P_G:P_G + 3072]],
                        axis=0)
    for j in range(4):
        rows = t[j * WIN_ROWS:(j + 1) * WIN_ROWS][None].astype(grads.dtype)
        grads = lax.dynamic_update_slice(grads, rows, (j, R_WIN, 0))
    return grads


def _small_prep(sw, l):
    eye = jnp.eye(4, dtype=F32)
    bd = jnp.einsum('gh,gcd->gchd', eye, sw['pool_w'][l]).reshape(POOL_W, POOL_W).astype(BF16)
    tril = jnp.tril(jnp.ones((SGU_CHUNK, SGU_CHUNK), F32))
    wm = (sw['sgu_w'][l] * tril[None]).astype(BF16)
    return dict(
        g_mix=sw['norm_mix_g'][l][None], g_x=sw['norm_xattn_g'][l][None], g_mem=sw['norm_mem_g'][l][None],
        g_ffn=sw['norm_ffn_g'][l][None], bd=bd, pool_scale=sw['pool_scale'][l][None],
        bf=jnp.pad(sw['b_forget'][l], (0, FCOLS - 8))[None], sgu_g=sw['sgu_norm_g'][l][None], wm=wm,
        wmt=jnp.transpose(wm, (0, 2, 1)), sgu_bias=jnp.repeat(sw['sgu_b'][l].T, 64, axis=1), bg=sw['b_gate'][l][None])


def _rows4(r0):
    return dict(n=D, k=D, tn=D, b_block=(4, 256, PACK_COLS), b_index=lambda i, j, k: (0, r0 // 256, 0))


def _rows_t(r0):
    return dict(tb=True, n=D, k=D, tn=D, b_block=(4, 256, PACK_COLS), b_index=lambda i, j, k: (0, r0 // 256, 0))


def _rows_grad(r0):
    return dict(ta=True, tm=D, tn=512, o_block=(4, 256, 512), o_index=lambda i, j, k: (0, r0 // 256, j))


def _add_to(r, e):
    return e + r


def _after(v, token):
    return v if token is None else v + token[0, 0]


def _layer_fwd(x, mem, GA, w_in_t, sp, l, token, second):
    t = f"l{l}"
    S = x.shape[0]
    h = _rms_fwd(x, _after(sp['g_mix'], token), f"rms_mix_{t}")
    proj = _mm(h, w_in_t, name=f"proj_{t}", out_dtype=F32, tb=True)
    d, ya = _pool_fwd(proj, sp['bd'], sp['pool_scale'], f"pool_fwd_{t}")
    fcum = _fgate_fwd(proj, sp['bf'], f"fgate_fwd_{t}")
    f8 = fcum[:, :8]
    fcol = f8.reshape(S, 4, 2).transpose(1, 0, 2)
    frow = f8.T.reshape(4, 2, S)
    qkv = proj[:, P_Q:P_Q + 3 * FOX_W].astype(BF16)
    o, o32, lse = _fox_fwd(qkv, fcol, frow, f"fox_fwd_{t}")
    sg = _sgu_fwd(proj, sp['sgu_g'], sp['wm'], sp['sgu_bias'], f"sgu_fwd_{t}")
    merged = _merge_fwd(proj, ya, o, sg, GA, sp['bg'], f"merge_fwd_{t}")
    x1 = _mm(merged, GA, name=f"out_{t}", out_dtype=F32, extra=x, epi=_add_to, **_rows4(R_OUT))
    GB, token = second(x1)
    hx = _rms_fwd(x1, _after(sp['g_x'], token), f"rms_x_{t}")
    hm = _rms_fwd(mem, sp['g_mem'], f"rms_mem_{t}")
    xq = _mm(hx, GB, name=f"xq_{t}", out_dtype=BF16, **_rows4(R_XQ))
    kv = _mm(hm, GB, name=f"xkv_{t}", out_dtype=BF16, n=2 * D, k=D, tn=512, tk=512, b_block=(None, 512, 512),
             b_index=lambda i, j, k: (j, R_XKV // 512, k))
    o2 = _xattn_fwd(xq, kv, f"xattn_fwd_{t}")
    x2 = _mm(o2, GB, name=f"xo_{t}", out_dtype=F32, extra=x1, epi=_add_to, **_rows4(R_XO))
    hf = _rms_fwd(x2, sp['g_ffn'], f"rms_ffn_{t}")
    z = _mm(hf, GB, name=f"ff1_{t}", out_dtype=F32, n=D_FF, k=D, tn=D, b_block=(None, 1024, PACK_COLS),
            b_index=lambda i, j, k: (j, R_FF1 // 1024, 0))
    x3 = _mm(z, GB, name=f"ff2_{t}", out_dtype=F32, a_fn=_relu2, extra=x2, epi=_add_to, n=D, k=D_FF, tk=1024, tn=D,
             b_block=(None, 1024, PACK_COLS), b_index=lambda i, j, k: (k, R_FF2 // 1024, 0))
    saved = dict(x=x, h=h, proj=proj, d=d, ya=ya, fcol=fcol, frow=frow, qkv=qkv, o=o, o32=o32, lse=lse, sg=sg, merged=merged,
                 x1=x1, hx=hx, hm=hm, xq=xq, kv=kv, o2=o2, x2=x2, hf=hf, z=z, GA=GA, GB=GB, w_in_t=w_in_t)
    return x3, saved


def _layer_bwd(dx3, mem, sp, sv, l, token, early):
    t = f"l{l}"
    S = dx3.shape[0]
    GA, GB = sv['GA'], sv['GB']
    gs = {}
    dx3 = _after(dx3, token)
    gb = lax.empty((4, ROWS_B, PACK_COLS), BF16)
    dz = _mm(dx3, GB, name=f"d_a2_{t}", out_dtype=BF16, tb=True, n=D_FF, k=D, tn=D, b_block=(None, 1024, PACK_COLS),
             b_index=lambda i, j, k: (j, R_FF2 // 1024, 0), extra=sv['z'],
             epi=lambda r, e: r * (2.0 * jnp.maximum(e, 0.0)))
    gb = _mm(sv['z'], dx3, name=f"dw_ff2_{t}", out_dtype=BF16, ta=True, a_fn=_relu2, into=gb, tm=1024, tn=D,
             o_block=(None, 1024, PACK_COLS), o_index=lambda i, j, k: (i, R_FF2 // 1024, 0))
    gb = _mm(sv['hf'], dz, name=f"dw_ff1_{t}", out_dtype=BF16, ta=True, into=gb, tm=1024, tn=D,
             o_block=(None, 1024, PACK_COLS), o_index=lambda i, j, k: (j, R_FF1 // 1024, 0))
    dhf = _mm(dz, GB, name=f"d_hf_{t}", out_dtype=F32, tb=True, n=D, k=D_FF, tn=D, tk=1024, b_block=(None, 1024, PACK_COLS),
              b_index=lambda i, j, k: (k, R_FF1 // 1024, 0))
    dx2, gs['norm_ffn_g'] = _rms_bwd(dhf, sv['x2'], sp['g_ffn'], dx3, f"rms_ffn_bwd_{t}")
    do2 = _mm(dx2, GB, name=f"d_o2_{t}", out_dtype=BF16, **_rows_t(R_XO))
    gb = _mm(sv['o2'], dx2, name=f"dw_xo_{t}", out_dtype=BF16, into=gb, **_rows_grad(R_XO))
    dxq, dkv = _xattn_bwd(sv['xq'], sv['kv'], do2, f"xattn_bwd_{t}")
    gb = _mm(sv['hm'], dkv, name=f"dw_xkv_{t}", out_dtype=BF16, ta=True, into=gb, tm=512, tn=512,
             o_block=(None, 512, 512), o_index=lambda i, j, k: (j, R_XKV // 512, i))
    dhm = _mm(dkv, GB, name=f"d_hm_{t}", out_dtype=F32, tb=True, n=D, k=2 * D, tn=512, tk=512, b_block=(None, 512, 512),
              b_index=lambda i, j, k: (k, R_XKV // 512, j))
    gs['norm_mem_g'] = _rms_bwd(dhm, mem, sp['g_mem'], None, f"rms_mem_bwd_{t}")
    gb = _mm(sv['hx'], dxq, name=f"dw_xq_{t}", out_dtype=BF16, into=gb, **_rows_grad(R_XQ))
    token = early(gb)
    dhx = _mm(dxq, GB, name=f"d_hx_{t}", out_dtype=F32, **_rows_t(R_XQ))
    dx1, gs['norm_xattn_g'] = _rms_bwd(dhx, sv['x1'], _after(sp['g_x'], token), dx2, f"rms_x_bwd_{t}")
    ga = jnp.zeros((4, ROWS_A, PACK_COLS), BF16)
    ga = _mm(sv['merged'], dx1, name=f"dw_out_{t}", out_dtype=BF16, into=ga, **_rows_grad(R_OUT))
    dm = _mm(dx1, GA, name=f"d_merged_{t}", out_dtype=F32, **_rows_t(R_OUT))
    dg, dya, do, dsg, ga, gs['b_gate'] = _merge_bwd(dm, sv['proj'], sv['ya'], sv['o'], sv['sg'], GA, sp['bg'], ga, f"merge_bwd_{t}")
    dc, dws, dbias, gs['sgu_norm_g'] = _sgu_bwd(dsg, sv['proj'], sp['sgu_g'], sp['wm'], sp['wmt'], sp['sgu_bias'], f"sgu_bwd_{t}")
    tril = jnp.tril(jnp.ones((SGU_CHUNK, SGU_CHUNK), F32))
    gs['sgu_w'] = dws * tril[None]
    gs['sgu_b'] = dbias.reshape(SGU_CHUNK, 4, 64).sum(-1).T
    dq, dk, dv, dfrow, dfcol = _fox_bwd(sv['qkv'], sv['o32'], do, sv['lse'], sv['fcol'], sv['frow'], f"fox_bwd_{t}")
    dF = jnp.pad(dfrow.reshape(8, S).T + dfcol.transpose(1, 0, 2).reshape(S, 8), ((0, 0), (0, FCOLS - 8)))
    df, dbf = _fgate_bwd(dF, sv['proj'], sp['bf'], f"fgate_bwd_{t}")
    gs['b_forget'] = dbf[:, :8]
    da, dbd, gs['pool_scale'] = _pool_bwd(dya, sv['d'], sp['bd'], sp['pool_scale'], f"pool_bwd_{t}")
    gs['pool_w'] = jnp.stack([dbd[g * 64:(g + 1) * 64, g * 64:(g + 1) * 64] for g in range(4)])
    dproj = jnp.concatenate([dg, dq, dk, dv, dc, da, df], axis=1)
    dwt = _mm(dproj, sv['h'], name=f"dw_in_{t}", out_dtype=BF16, ta=True, tm=512, tn=1024)
    ga = _w_in_grad_rows(ga, dwt)
    dh = _mm(dproj, sv['w_in_t'], name=f"d_h_{t}", out_dtype=F32, tk=512, tn=D)
    dx, gs['norm_mix_g'] = _rms_bwd(dh, sv['x'], sp['g_mix'], dx1, f"rms_mix_bwd_{t}")
    return dx, ga, gs


SMALL_ROWS = 1424
GRAD_BLOCKS = {
    'w_ff1': ('b', lambda i: (R_FF1 // 256 + i, 0)), 'w_ff2': ('b', lambda i: (R_FF2 // 256 + i, 0)),
    'w_xq': ('b', lambda i: (R_XQ // 256 + i, 0)), 'w_xo': ('b', lambda i: (R_XO // 256 + i, 0)),
    'w_xkv': ('b', lambda i: (R_XKV // 256 + i % 2, i // 2)), 'w_out': ('a', lambda i: (R_OUT // 256 + i, 0)),
    'w_branch_a': ('a', lambda i: (R_BRANCH // 256, 0)), 'w_branch_b': ('a', lambda i: (R_BRANCH // 256, 1 + i)),
    'w_branch_c': ('a', lambda i: (R_BRANCH // 256, 3)),
}


def _pack_small(parts):
    flat = jnp.concatenate([p.reshape(-1) for p in parts])
    return jnp.pad(flat, (0, SMALL_ROWS * 128 - flat.shape[0])).reshape(SMALL_ROWS, 128)


def _unpack_small(buf, shapes):
    flat, out, r = buf.reshape(-1), [], 0
    for s in shapes:
        n = math.prod(s)
        out.append(flat[r:r + n].reshape(s))
        r += n
    return out


def kernel(x, mem, norm_mix_g, w_in, b_forget, pool_w, pool_scale, sgu_norm_g, sgu_w, sgu_b, w_branch_a, w_branch_b, w_branch_c, b_gate, w_out, norm_xattn_g, norm_mem_g, w_xq, w_xkv, w_xo, norm_ffn_g, w_ff1, w_ff2, final_norm_g, loss_target, m_norm_mix_g, m_w_in, m_b_forget, m_pool_w, m_pool_scale, m_sgu_norm_g, m_sgu_w, m_sgu_b, m_w_branch_a, m_w_branch_b, m_w_branch_c, m_b_gate, m_w_out, m_norm_xattn_g, m_norm_mem_g, m_w_xq, m_w_xkv, m_w_xo, m_norm_ffn_g, m_w_ff1, m_w_ff2, m_final_norm_g, v_norm_mix_g, v_w_in, v_b_forget, v_pool_w, v_pool_scale, v_sgu_norm_g, v_sgu_w, v_sgu_b, v_w_branch_a, v_w_branch_b, v_w_branch_c, v_b_gate, v_w_out, v_norm_xattn_g, v_norm_mem_g, v_w_xq, v_w_xkv, v_w_xo, v_norm_ffn_g, v_w_ff1, v_w_ff2, v_final_norm_g):
    args = (norm_mix_g, w_in, b_forget, pool_w, pool_scale, sgu_norm_g, sgu_w, sgu_b, w_branch_a, w_branch_b, w_branch_c, b_gate,
            w_out, norm_xattn_g, norm_mem_g, w_xq, w_xkv, w_xo, norm_ffn_g, w_ff1, w_ff2, final_norm_g)
    margs = (m_norm_mix_g, m_w_in, m_b_forget, m_pool_w, m_pool_scale, m_sgu_norm_g, m_sgu_w, m_sgu_b, m_w_branch_a, m_w_branch_b,
             m_w_branch_c, m_b_gate, m_w_out, m_norm_xattn_g, m_norm_mem_g, m_w_xq, m_w_xkv, m_w_xo, m_norm_ffn_g, m_w_ff1, m_w_ff2,
             m_final_norm_g)
    vargs = (v_norm_mix_g, v_w_in, v_b_forget, v_pool_w, v_pool_scale, v_sgu_norm_g, v_sgu_w, v_sgu_b, v_w_branch_a, v_w_branch_b,
             v_w_branch_c, v_b_gate, v_w_out, v_norm_xattn_g, v_norm_mem_g, v_w_xq, v_w_xkv, v_w_xo, v_norm_ffn_g, v_w_ff1, v_w_ff2,
             v_final_norm_g)
    w = dict(zip(W_NAMES, args))
    mo = dict(zip(W_NAMES, margs))
    vo = dict(zip(W_NAMES, vargs))
    xs, mems, tgt = x[0], mem[0], loss_target[0]
    shards = [_pack_shard(w, l) for l in range(DEPTH)]
    preps = [_small_prep(w, l) for l in range(DEPTH)]

    first_a, _ = _gather_begin(shards[0][0], "a_l0")
    pending_b, token = _gather_begin(shards[0][1], "b_l0")
    GA = None
    act, saved = xs, []
    for l in range(DEPTH):
        nxt = {}
        if l + 1 < DEPTH:
            nxt['a'], ta = _gather_begin(shards[l + 1][0], f"a_l{l + 1}")
            token = ta if token is None else token + ta
        if l == 0:
            GA = _gather_end(first_a, shards[DEPTH - 1][1])

        def second(x1, l=l, pending_b=pending_b, nxt=nxt):
            GB = _gather_end(pending_b, x1)
            if l + 1 == DEPTH:
                return GB, None
            nxt['b'], tb = _gather_begin(shards[l + 1][1], f"b_l{l + 1}")
            return GB, tb

        act, sv = _layer_fwd(act, mems, GA, _w_in_rows(GA), preps[l], l, token, second)
        saved.append(sv)
        if l + 1 < DEPTH:
            GA = _gather_end(nxt['a'], act)
            pending_b, token = nxt['b'], None
    loss_part, dact, d_final_g = _loss_head(act, w['final_norm_g'][None], tgt, "loss_head")

    red_a, red_b, small_g = [None] * DEPTH, [None] * DEPTH, [None] * DEPTH
    token, state_a = None, None
    for l in reversed(range(DEPTH)):
        early = {}

        def start_b(gb, l=l, early=early):
            early['state'], tok = _reduce_begin(gb, f"b_l{l}")
            return tok

        dact, ga, small_g[l] = _layer_bwd(dact, mems, preps[l], saved[l], l, token, start_b)
        if state_a is not None:
            red_a[l + 1] = _reduce_end(state_a, dact)
        red_b[l] = _reduce_end(early['state'], dact)
        state_a, token = _reduce_begin(ga, f"a_l{l}")
    grad_x = dact[None]
    per_layer = [n for n in SMALL_NAMES if n != 'final_norm_g']
    small_shapes = [w[n].shape for n in per_layer] + [(D,), (1,)]
    parts = [jnp.stack([small_g[l][n].reshape(w[n].shape[1:]) for l in range(DEPTH)]) for n in per_layer]
    state_small, token_small = _all_reduce_begin(_pack_small(parts + [d_final_g.reshape(D), loss_part.reshape(1)]), "small")
    token = token + token_small

    grads, delta, new_m, new_v = {}, {}, {}, {}
    for n, (buf, g_index) in GRAD_BLOCKS.items():
        if buf == 'b':
            grads[n], delta[n], new_m[n], new_v[n] = _adamw_packed(red_b, w[n], mo[n], vo[n], g_index, f"adamw_{n}", token)
    red_a[0] = _reduce_end(state_a, new_v['w_xkv'])
    small_red = _unpack_small(_all_reduce_end(state_small, red_a[0]), small_shapes)
    grads.update(zip(per_layer + ['final_norm_g'], small_red[:-1]))
    loss = small_red[-1].reshape(())
    for n, (buf, g_index) in GRAD_BLOCKS.items():
        if buf == 'a':
            grads[n], delta[n], new_m[n], new_v[n] = _adamw_packed(red_a, w[n], mo[n], vo[n], g_index, f"adamw_{n}", token)
    g_t = jnp.stack([r[R_WIN:R_WIN + WIN_ROWS] for r in red_a], axis=1)
    upd = _adamw(g_t, _w_in_t(w['w_in']), _w_in_t(mo['w_in']), _w_in_t(vo['w_in']), "adamw_w_in", block=(WIN_ROWS, DEPTH, 128))
    grads['w_in'], delta['w_in'], new_m['w_in'], new_v['w_in'] = [jnp.transpose(a, (1, 2, 0)) for a in (g_t,) + tuple(upd)]
    small_all = per_layer + ['final_norm_g']
    shapes_all = [w[n].shape for n in small_all]
    packed = [_pack_small([d[n] for n in small_all])[None] for d in (grads, w, mo, vo)]
    ds, ms, vs = _adamw(*packed, "adamw_small")
    for n, a, b, c in zip(small_all, _unpack_small(ds[0], shapes_all), _unpack_small(ms[0], shapes_all), _unpack_small(vs[0], shapes_all)):
        delta[n], new_m[n], new_v[n] = a, b, c

    return (loss, grad_x, *[grads[n] for n in W_NAMES], *[delta[n] for n in W_NAMES], *[new_m[n] for n in W_NAMES],
            *[new_v[n] for n in W_NAMES])
```

```python
import math

import jax
import jax.numpy as jnp
from jax import lax
from jax.experimental import pallas as pl
from jax.experimental.pallas import tpu as pltpu

F32 = jnp.float32
BF16 = jnp.bfloat16

D = 1024
DEPTH = 2
POOL_W = 256
FOX_W = 512
SGU_W = 256
SGU_CHUNK = 128
N_IN = 5384
P_G, P_Q, P_K, P_V, P_C, P_A, P_F = 0, 3072, 3584, 4096, 4608, 5120, 5376
NP = 5632
XH, XHD = 4, 256
D_FF = 4096
EPS = 1e-6
NEG = -1e30
FOX_SCALE = 64 ** -0.5
X_SCALE = 256 ** -0.5
GELU_K = math.sqrt(2.0 / math.pi)
GELU_C = 0.044715

ADAM_LR, ADAM_B1, ADAM_B2, ADAM_EPS, ADAM_WD, ADAM_STEP = 0.001, 0.9, 0.999, 1e-08, 0.01, 10

VMEM_LIMIT = 48 * 1024 * 1024
MESH = pl.DeviceIdType.MESH

IN_NAMES = ['x', 'mem', 'norm_mix_g', 'w_in', 'b_forget', 'pool_w', 'pool_scale', 'sgu_norm_g', 'sgu_w', 'sgu_b',
            'w_branch_a', 'w_branch_b', 'w_branch_c', 'b_gate', 'w_out', 'norm_xattn_g', 'norm_mem_g', 'w_xq',
            'w_xkv', 'w_xo', 'norm_ffn_g', 'w_ff1', 'w_ff2', 'final_norm_g']
W_NAMES = IN_NAMES[2:]
BIG_NAMES = ['w_in', 'w_branch_a', 'w_branch_b', 'w_branch_c', 'w_out', 'w_xq', 'w_xkv', 'w_xo', 'w_ff1', 'w_ff2']
SMALL_NAMES = [n for n in W_NAMES if n not in BIG_NAMES]
PACK_COLS = 1024


ANY = pl.BlockSpec(memory_space=pl.ANY)


def _cp(sem=None):
    return pltpu.CompilerParams(dimension_semantics=sem, vmem_limit_bytes=VMEM_LIMIT)


def _mm(a, b, *, name, out_dtype, ta=False, tb=False, tm=1024, tn=512, tk=1024, a_fn=None, extra=None, epi=None,
        n=None, k=None, b_block=None, b_index=None, into=None, o_block=None, o_index=None):
    M = a.shape[1] if ta else a.shape[0]
    K = k if k is not None else (a.shape[0] if ta else a.shape[1])
    N = n if n is not None else (b.shape[0] if tb else b.shape[1])
    tm, tn, tk = min(tm, M), min(tn, N), min(tk, K)
    assert M % tm == 0 and N % tn == 0 and K % tk == 0, (name, M, N, K)
    nk = K // tk
    a_spec = pl.BlockSpec((tk, tm), lambda i, j, k: (k, i)) if ta else pl.BlockSpec((tm, tk), lambda i, j, k: (i, k))
    if b_block is not None:
        b_spec = pl.BlockSpec(b_block, b_index)
    else:
        b_spec = pl.BlockSpec((tn, tk), lambda i, j, k: (j, k)) if tb else pl.BlockSpec((tk, tn), lambda i, j, k: (k, j))
    dn = (((0 if ta else 1,), (1 if tb else 0,)), ((), ()))
    tile = pl.BlockSpec((tm, tn), lambda i, j, k: (i, j))
    o_spec = pl.BlockSpec(o_block, o_index) if into is not None else tile
    in_specs = [a_spec, b_spec] + ([tile] if extra is not None else []) + ([ANY] if into is not None else [])
    n_in = len(in_specs)

    def body(*refs):
        a_ref, b_ref = refs[0], refs[1]
        e_ref = refs[2] if extra is not None else None
        o_ref, acc_ref = refs[n_in], refs[n_in + 1]
        kk = pl.program_id(2)

        @pl.when(kk == 0)
        def _():
            acc_ref[...] = jnp.zeros_like(acc_ref)

        av = a_ref[...]
        if a_fn is not None:
            av = a_fn(av)
        bv = b_ref[...]
        if bv.ndim == 3:
            bv = bv.reshape(-1, bv.shape[-1])
        acc_ref[...] += lax.dot_general(av.astype(BF16), bv.astype(BF16), dn, preferred_element_type=F32)

        @pl.when(kk == nk - 1)
        def _():
            r = acc_ref[...]
            if epi is not None:
                r = epi(r, e_ref[...])
            o_ref[...] = r.astype(o_ref.dtype).reshape(o_ref.shape)

    args = (a, b) + ((extra,) if extra is not None else ()) + ((into,) if into is not None else ())
    out_shape = jax.ShapeDtypeStruct(into.shape, into.dtype) if into is not None else jax.ShapeDtypeStruct((M, N), out_dtype)
    return pl.pallas_call(
        body, out_shape=out_shape, grid=(M // tm, N // tn, nk), in_specs=in_specs, out_specs=o_spec,
        scratch_shapes=[pltpu.VMEM((tm, tn), F32)], input_output_aliases={n_in - 1: 0} if into is not None else {},
        compiler_params=_cp(("parallel", "parallel", "arbitrary")), name=name)(*args)


def _relu2(z):
    r = jnp.maximum(z.astype(F32), 0.0)
    return r * r


def _rms_fwd(x, g, name, tr=512):
    R, n = x.shape
    tr = min(tr, R)

    def body(x_ref, g_ref, h_ref):
        xv = x_ref[...]
        rstd = lax.rsqrt(jnp.mean(xv * xv, axis=-1, keepdims=True) + EPS)
        h_ref[...] = (xv * rstd * g_ref[...]).astype(BF16)

    return pl.pallas_call(
        body, out_shape=jax.ShapeDtypeStruct((R, n), BF16), grid=(R // tr,),
        in_specs=[pl.BlockSpec((tr, n), lambda i: (i, 0)), pl.BlockSpec((1, n), lambda i: (0, 0))],
        out_specs=pl.BlockSpec((tr, n), lambda i: (i, 0)), compiler_params=_cp(("parallel",)), name=name)(x, g)


def _rms_bwd(dh, x, g, dres, name, tr=512):
    R, n = x.shape
    tr = min(tr, R)
    need_dx = dres is not None

    def body(*refs):
        if need_dx:
            dh_ref, x_ref, g_ref, r_ref, dx_ref, dg_ref = refs
        else:
            dh_ref, x_ref, g_ref, dg_ref = refs
        i = pl.program_id(0)
        xv = x_ref[...]
        dhv = dh_ref[...].astype(F32)
        rstd = lax.rsqrt(jnp.mean(xv * xv, axis=-1, keepdims=True) + EPS)
        xhat = xv * rstd

        @pl.when(i == 0)
        def _():
            dg_ref[...] = jnp.zeros_like(dg_ref)

        dg_ref[...] += jnp.sum(dhv * xhat, axis=0, keepdims=True)
        if need_dx:
            t = dhv * g_ref[...]
            dx_ref[...] = r_ref[...] + rstd * (t - xhat * jnp.mean(t * xhat, axis=-1, keepdims=True))

    row = pl.BlockSpec((tr, n), lambda i: (i, 0))
    vec = pl.BlockSpec((1, n), lambda i: (0, 0))
    if need_dx:
        return pl.pallas_call(
            body, out_shape=(jax.ShapeDtypeStruct((R, n), F32), jax.ShapeDtypeStruct((1, n), F32)), grid=(R // tr,),
            in_specs=[row, row, vec, row], out_specs=(row, vec), compiler_params=_cp(("arbitrary",)), name=name)(dh, x, g, dres)
    return pl.pallas_call(
        body, out_shape=jax.ShapeDtypeStruct((1, n), F32), grid=(R // tr,),
        in_specs=[row, row, vec], out_specs=vec, compiler_params=_cp(("arbitrary",)), name=name)(dh, x, g)


def _loss_head(x, g, tgt, name, tr=512):
    R, n = x.shape

    def body(x_ref, g_ref, t_ref, loss_ref, dx_ref, dg_ref):
        i = pl.program_id(0)
        xv = x_ref[...]
        gv = g_ref[...]
        rstd = lax.rsqrt(jnp.mean(xv * xv, axis=-1, keepdims=True) + EPS)
        xhat = xv * rstd
        e = xhat * gv - t_ref[...]

        @pl.when(i == 0)
        def _():
            loss_ref[...] = jnp.zeros_like(loss_ref)
            dg_ref[...] = jnp.zeros_like(dg_ref)

        loss_ref[...] += 0.5 * jnp.sum(jnp.sum(e * e, axis=-1, keepdims=True) / n, axis=0, keepdims=True)
        dy = e / n
        dg_ref[...] += jnp.sum(dy * xhat, axis=0, keepdims=True)
        t = dy * gv
        dx_ref[...] = rstd * (t - xhat * jnp.mean(t * xhat, axis=-1, keepdims=True))

    row = pl.BlockSpec((tr, n), lambda i: (i, 0))
    vec = pl.BlockSpec((1, n), lambda i: (0, 0))
    one = pl.BlockSpec((1, 1), lambda i: (0, 0))
    return pl.pallas_call(
        body, out_shape=(jax.ShapeDtypeStruct((1, 1), F32), jax.ShapeDtypeStruct((R, n), F32), jax.ShapeDtypeStruct((1, n), F32)),
        grid=(R // tr,), in_specs=[row, vec, row], out_specs=(one, row, vec),
        compiler_params=_cp(("arbitrary",)), name=name)(x, g, tgt)


def _pool_masks(S):
    row = lax.broadcasted_iota(jnp.int32, (S, POOL_W), 0)
    grp = lax.broadcasted_iota(jnp.int32, (S, POOL_W), 1) // 64
    win = jnp.where(grp == 0, 2, jnp.where(grp == 1, 4, jnp.where(grp == 2, 8, 16)))
    cnt = jnp.minimum(row + 1, win).astype(F32)
    return row, grp, cnt


def _by_group(grp, v0, v1, v2, v3):
    return jnp.where(grp == 0, v0, jnp.where(grp == 1, v1, jnp.where(grp == 2, v2, v3)))


def _pool_fwd(proj, bd, scale, name):
    S = proj.shape[0]

    def body(a_ref, bd_ref, sc_ref, d_ref, y_ref):
        a = a_ref[...]
        row, grp, cnt = _pool_masks(S)

        def back(v, k):
            return jnp.where(row >= k, pltpu.roll(v, k, 0), 0.0)

        s1 = a + back(a, 1)
        s2 = s1 + back(s1, 2)
        s3 = s2 + back(s2, 4)
        s4 = s3 + back(s3, 8)
        d = (_by_group(grp, s1, s2, s3, s4) / cnt - a).astype(BF16)
        d_ref[...] = d
        y_ref[...] = (jnp.dot(d, bd_ref[...], preferred_element_type=F32) * sc_ref[...]).astype(BF16)

    full = lambda r, c: pl.BlockSpec((r, c), lambda i: (0, 0))
    return pl.pallas_call(
        body, out_shape=(jax.ShapeDtypeStruct((S, POOL_W), BF16), jax.ShapeDtypeStruct((S, POOL_W), BF16)), grid=(1,),
        in_specs=[pl.BlockSpec((S, POOL_W), lambda i: (0, P_A // POOL_W)), full(POOL_W, POOL_W), full(1, POOL_W)],
        out_specs=(full(S, POOL_W), full(S, POOL_W)), compiler_params=_cp(("arbitrary",)), name=name)(proj, bd, scale)


def _pool_bwd(dya, d, bd, scale, name):
    S = dya.shape[0]

    def body(dy_ref, d_ref, bd_ref, sc_ref, da_ref, dbd_ref, dsc_ref):
        dy = dy_ref[...]
        dv = d_ref[...]
        bdv = bd_ref[...]
        row, grp, cnt = _pool_masks(S)
        yraw = jnp.dot(dv, bdv, preferred_element_type=F32)
        dsc_ref[...] = jnp.sum(dy * yraw, axis=0, keepdims=True)
        tb = (dy * sc_ref[...]).astype(BF16)
        dbd_ref[...] = lax.dot_general(dv, tb, (((0,), (0,)), ((), ())), preferred_element_type=F32)
        dd = lax.dot_general(tb, bdv, (((1,), (1,)), ((), ())), preferred_element_type=F32)
        e = dd / cnt

        def fwd(v, k):
            return jnp.where(row < S - k, pltpu.roll(v, S - k, 0), 0.0)

        r1 = e + fwd(e, 1)
        r2 = r1 + fwd(r1, 2)
        r3 = r2 + fwd(r2, 4)
        r4 = r3 + fwd(r3, 8)
        da_ref[...] = (_by_group(grp, r1, r2, r3, r4) - dd).astype(BF16)

    full = lambda r, c: pl.BlockSpec((r, c), lambda i: (0, 0))
    return pl.pallas_call(
        body, out_shape=(jax.ShapeDtypeStruct((S, POOL_W), BF16), jax.ShapeDtypeStruct((POOL_W, POOL_W), F32),
                         jax.ShapeDtypeStruct((1, POOL_W), F32)), grid=(1,),
        in_specs=[full(S, POOL_W), full(S, POOL_W), full(POOL_W, POOL_W), full(1, POOL_W)],
        out_specs=(full(S, POOL_W), full(POOL_W, POOL_W), full(1, POOL_W)),
        compiler_params=_cp(("arbitrary",)), name=name)(dya, d, bd, scale)


FCOLS = 128


def _log_sigmoid(z):
    return -(jnp.maximum(-z, 0.0) + jnp.log1p(jnp.exp(-jnp.abs(z))))


def _fgate_fwd(proj, bf, name):
    S = proj.shape[0]

    def body(f_ref, b_ref, o_ref):
        v = _log_sigmoid(f_ref[...] + b_ref[...])
        row = lax.broadcasted_iota(jnp.int32, (S, FCOLS), 0)
        k = 1
        while k < S:
            v = v + jnp.where(row >= k, pltpu.roll(v, k, 0), 0.0)
            k *= 2
        o_ref[...] = v

    return pl.pallas_call(
        body, out_shape=jax.ShapeDtypeStruct((S, FCOLS), F32), grid=(1,),
        in_specs=[pl.BlockSpec((S, FCOLS), lambda i: (0, P_F // FCOLS)), pl.BlockSpec((1, FCOLS), lambda i: (0, 0))],
        out_specs=pl.BlockSpec((S, FCOLS), lambda i: (0, 0)), compiler_params=_cp(("arbitrary",)), name=name)(proj, bf)


def _fgate_bwd(dF, proj, bf, name):
    S = proj.shape[0]

    def body(dF_ref, f_ref, b_ref, df_ref, db_ref):
        v = dF_ref[...]
        row = lax.broadcasted_iota(jnp.int32, (S, FCOLS), 0)
        k = 1
        while k < S:
            v = v + jnp.where(row < S - k, pltpu.roll(v, S - k, 0), 0.0)
            k *= 2
        z = f_ref[...] + b_ref[...]
        df = v * (1.0 / (1.0 + jnp.exp(z)))
        db_ref[...] = jnp.sum(df, axis=0, keepdims=True)
        df_ref[...] = jnp.concatenate([df, jnp.zeros_like(df)], axis=1).astype(BF16)

    return pl.pallas_call(
        body, out_shape=(jax.ShapeDtypeStruct((S, 2 * FCOLS), BF16), jax.ShapeDtypeStruct((1, FCOLS), F32)), grid=(1,),
        in_specs=[pl.BlockSpec((S, FCOLS), lambda i: (0, 0)), pl.BlockSpec((S, FCOLS), lambda i: (0, P_F // FCOLS)),
                  pl.BlockSpec((1, FCOLS), lambda i: (0, 0))],
        out_specs=(pl.BlockSpec((S, 2 * FCOLS), lambda i: (0, 0)), pl.BlockSpec((1, FCOLS), lambda i: (0, 0))),
        compiler_params=_cp(("arbitrary",)), name=name)(dF, proj, bf)


def _fox_scores(qe, kj, fq, fk, r0, c0, tq, tk, diagonal):
    s = lax.dot_general(qe, kj, (((1,), (1,)), ((), ())), preferred_element_type=F32) * FOX_SCALE
    s = s + (fq - fk)
    if not diagonal:
        return s
    rows = r0 + lax.broadcasted_iota(jnp.int32, (tq, tk), 0)
    cols = c0 + lax.broadcasted_iota(jnp.int32, (tq, tk), 1)
    return jnp.where(rows >= cols, s, NEG)


FOX_TQ, FOX_TK = 512, 512


def _fox_fwd(qkv, fcol, frow, name):
    S = qkv.shape[0]
    tq, tk = FOX_TQ, min(FOX_TK, S)

    def body(q_ref, k_ref, v_ref, fc_ref, fr_ref, o_ref, o32_ref, lse_ref):
        i = pl.program_id(1)
        r0 = i * tq
        q = q_ref[...]
        half = lax.broadcasted_iota(jnp.int32, (tq, 128), 1) // 64
        qs = [jnp.where(half == e, q, jnp.zeros_like(q)) for e in (0, 1)]
        fqs = [fc_ref[0, :, e:e + 1] for e in (0, 1)]

        def step(j, carry, diagonal=False):
            c0 = pl.multiple_of(j * tk, tk)
            kj = k_ref[pl.ds(c0, tk), :]
            vj = v_ref[pl.ds(c0, tk), :]
            out = []
            for e in (0, 1):
                m, l, acc = carry[e]
                s = _fox_scores(qs[e], kj, fqs[e], fr_ref[0, e:e + 1, pl.ds(c0, tk)], r0, c0, tq, tk, diagonal)
                m_new = jnp.maximum(m, jnp.max(s, axis=-1, keepdims=True))
                alpha = jnp.exp(m - m_new)
                p = jnp.exp(s - m_new)
                out.append((m_new, alpha * l + jnp.sum(p, axis=-1, keepdims=True),
                            alpha * acc + jnp.dot(p.astype(BF16), vj, preferred_element_type=F32)))
            return tuple(out)

        init = (jnp.full((tq, 1), NEG, F32), jnp.zeros((tq, 1), F32), jnp.zeros((tq, 128), F32))
        below = r0 // tk
        carry = lax.fori_loop(0, below, step, (init, init))
        carry = step(below, carry, diagonal=True)
        outs = []
        for e in (0, 1):
            m, l, acc = carry[e]
            outs.append(acc / l)
            lse_ref[0, :, e:e + 1] = m + jnp.log(l)
        o = jnp.where(half == 0, outs[0], outs[1])
        o32_ref[...] = o
        o_ref[...] = o.astype(BF16)

    tile = pl.BlockSpec((tq, 128), lambda h, i: (i, h))
    return pl.pallas_call(
        body, out_shape=(jax.ShapeDtypeStruct((S, FOX_W), BF16), jax.ShapeDtypeStruct((S, FOX_W), F32),
                         jax.ShapeDtypeStruct((4, S, 2), F32)), grid=(4, S // tq),
        in_specs=[tile, pl.BlockSpec((S, 128), lambda h, i: (0, 4 + h)), pl.BlockSpec((S, 128), lambda h, i: (0, 8 + h)),
                  pl.BlockSpec((1, tq, 2), lambda h, i: (h, i, 0)), pl.BlockSpec((1, 2, S), lambda h, i: (h, 0, 0))],
        out_specs=(tile, tile, pl.BlockSpec((1, tq, 2), lambda h, i: (h, i, 0))),
        compiler_params=_cp(("parallel", "parallel")), name=name)(qkv, qkv, qkv, fcol, frow)


def _fox_bwd(qkv, o32, do, lse, fcol, frow, name):
    S = qkv.shape[0]
    tq, tk = FOX_TQ, min(FOX_TK, S)
    nq = S // tq

    def body(q_ref, k_ref, v_ref, o_ref, do_ref, lse_ref, fc_ref, fr_ref, dq_ref, dk_ref, dv_ref, dfr_ref, dfc_ref, dk_acc, dv_acc):
        dk_acc[...] = jnp.zeros_like(dk_acc)
        dv_acc[...] = jnp.zeros_like(dv_acc)
        dfr_ref[...] = jnp.zeros_like(dfr_ref)
        half = lax.broadcasted_iota(jnp.int32, (tq, 128), 1) // 64

        def q_block(i, _):
            r0 = pl.multiple_of(i * tq, tq)
            qi = q_ref[pl.ds(r0, tq), :]
            dob = do_ref[pl.ds(r0, tq), :].astype(BF16)
            row_dot = dob.astype(F32) * o_ref[pl.ds(r0, tq), :]
            qs = [jnp.where(half == e, qi, jnp.zeros_like(qi)) for e in (0, 1)]
            dos = [jnp.where(half == e, dob, jnp.zeros_like(dob)) for e in (0, 1)]
            deltas = [jnp.sum(jnp.where(half == e, row_dot, 0.0), axis=-1, keepdims=True) for e in (0, 1)]
            lses = [lse_ref[0, pl.ds(r0, tq), e:e + 1] for e in (0, 1)]
            fqs = [fc_ref[0, pl.ds(r0, tq), e:e + 1] for e in (0, 1)]

            def step(j, carry, diagonal=False):
                dqs, row_sums = carry
                c0 = pl.multiple_of(j * tk, tk)
                kj = k_ref[pl.ds(c0, tk), :]
                vj = v_ref[pl.ds(c0, tk), :]
                new_dq, new_rows, dkc, dvc = [], [], [], []
                for e in (0, 1):
                    s = _fox_scores(qs[e], kj, fqs[e], fr_ref[0, e:e + 1, pl.ds(c0, tk)], r0, c0, tq, tk, diagonal)
                    p = jnp.exp(s - lses[e])
                    dp = lax.dot_general(dos[e], vj, (((1,), (1,)), ((), ())), preferred_element_type=F32)
                    ds = p * (dp - deltas[e])
                    dfr_ref[0, e:e + 1, pl.ds(c0, tk)] -= jnp.sum(ds, axis=0, keepdims=True)
                    new_rows.append(row_sums[e] + jnp.sum(ds, axis=-1, keepdims=True))
                    dsb = (ds * FOX_SCALE).astype(BF16)
                    dkc.append(lax.dot_general(dsb, qi, (((0,), (0,)), ((), ())), preferred_element_type=F32))
                    dvc.append(lax.dot_general(p.astype(BF16), dob, (((0,), (0,)), ((), ())), preferred_element_type=F32))
                    new_dq.append(dqs[e] + jnp.dot(dsb, kj, preferred_element_type=F32))
                half_k = lax.broadcasted_iota(jnp.int32, (tk, 128), 1) // 64
                dk_acc[pl.ds(c0, tk), :] += jnp.where(half_k == 0, dkc[0], dkc[1])
                dv_acc[pl.ds(c0, tk), :] += jnp.where(half_k == 0, dvc[0], dvc[1])
                return tuple(new_dq), tuple(new_rows)

            zero, zero_col = jnp.zeros((tq, 128), F32), jnp.zeros((tq, 1), F32)
            below = r0 // tk
            carry = lax.fori_loop(0, below, step, ((zero, zero), (zero_col, zero_col)))
            dqs, row_sums = step(below, carry, diagonal=True)
            for e in (0, 1):
                dfc_ref[0, pl.ds(r0, tq), e:e + 1] = row_sums[e]
            dq_ref[pl.ds(r0, tq), :] = jnp.where(half == 0, dqs[0], dqs[1]).astype(BF16)
            return 0

        lax.fori_loop(0, nq, q_block, 0)
        dk_ref[...] = dk_acc[...].astype(BF16)
        dv_ref[...] = dv_acc[...].astype(BF16)

    col = lambda off: pl.BlockSpec((S, 128), lambda h: (0, off + h))
    hs2 = pl.BlockSpec((1, S, 2), lambda h: (h, 0, 0))
    h2s = pl.BlockSpec((1, 2, S), lambda h: (h, 0, 0))
    return pl.pallas_call(
        body, out_shape=(jax.ShapeDtypeStruct((S, FOX_W), BF16),) * 3 + (jax.ShapeDtypeStruct((4, 2, S), F32),
                                                                         jax.ShapeDtypeStruct((4, S, 2), F32)), grid=(4,),
        in_specs=[col(0), col(4), col(8), col(0), col(0), hs2, hs2, h2s],
        out_specs=(col(0), col(0), col(0), h2s, hs2),
        scratch_shapes=[pltpu.VMEM((S, 128), F32), pltpu.VMEM((S, 128), F32)],
        compiler_params=_cp(("parallel",)), name=name)(qkv, qkv, qkv, o32, do, lse, fcol, frow)


def _gelu(x):
    return 0.5 * x * (1.0 + jnp.tanh(GELU_K * (x + GELU_C * x * x * x)))


def _gelu_grad(x):
    th = jnp.tanh(GELU_K * (x + GELU_C * x * x * x))
    return 0.5 * (1.0 + th) + 0.5 * x * (1.0 - th * th) * GELU_K * (1.0 + 3.0 * GELU_C * x * x)


def _sgu_parts(c, gn, w_ref, bias):
    zc = _gelu(c)
    u, vv = zc[:, :SGU_W], zc[:, SGU_W:]
    rstd = lax.rsqrt(jnp.mean(vv * vv, axis=-1, keepdims=True) + EPS)
    vhat = vv * rstd
    vnb = (vhat * gn).astype(BF16)
    grp = lax.broadcasted_iota(jnp.int32, (SGU_CHUNK, SGU_W), 1) // 64
    mixed = bias
    for gi in range(4):
        mixed = mixed + jnp.where(grp == gi, jnp.dot(w_ref[gi], vnb, preferred_element_type=F32), 0.0)
    return u, rstd, vhat, vnb, grp, mixed


def _sgu_fwd(proj, gn, wm, bias, name):
    S = proj.shape[0]

    def body(c_ref, g_ref, w_ref, b_ref, o_ref):
        u, _, _, _, _, mixed = _sgu_parts(c_ref[...], g_ref[...], w_ref, b_ref[...])
        o_ref[...] = (u * mixed).astype(BF16)

    return pl.pallas_call(
        body, out_shape=jax.ShapeDtypeStruct((S, SGU_W), BF16), grid=(S // SGU_CHUNK,),
        in_specs=[pl.BlockSpec((SGU_CHUNK, 2 * SGU_W), lambda i: (i, P_C // (2 * SGU_W))),
                  pl.BlockSpec((1, SGU_W), lambda i: (0, 0)), pl.BlockSpec((4, SGU_CHUNK, SGU_CHUNK), lambda i: (0, 0, 0)),
                  pl.BlockSpec((SGU_CHUNK, SGU_W), lambda i: (0, 0))],
        out_specs=pl.BlockSpec((SGU_CHUNK, SGU_W), lambda i: (i, 0)),
        compiler_params=_cp(("parallel",)), name=name)(proj, gn, wm, bias)


def _sgu_bwd(dsg, proj, gn, wm, wmt, bias, name):
    S = proj.shape[0]

    def body(dsg_ref, c_ref, g_ref, w_ref, wt_ref, b_ref, dc_ref, dw_ref, db_ref, dg_ref):
        i = pl.program_id(0)

        @pl.when(i == 0)
        def _():
            dw_ref[...] = jnp.zeros_like(dw_ref)
            db_ref[...] = jnp.zeros_like(db_ref)
            dg_ref[...] = jnp.zeros_like(dg_ref)

        c = c_ref[...]
        gn_v = g_ref[...]
        u, rstd, vhat, vnb, grp, mixed = _sgu_parts(c, gn_v, w_ref, b_ref[...])
        dsg_v = dsg_ref[...]
        du = dsg_v * mixed
        dmix = dsg_v * u
        db_ref[...] += dmix
        dmb = dmix.astype(BF16)
        dvn = jnp.zeros((SGU_CHUNK, SGU_W), F32)
        for gi in range(4):
            dmg = jnp.where(grp == gi, dmb, jnp.zeros_like(dmb))
            dw_ref[gi] += lax.dot_general(dmg, vnb, (((1,), (1,)), ((), ())), preferred_element_type=F32)
            dvn = dvn + jnp.where(grp == gi, jnp.dot(wt_ref[gi], dmb, preferred_element_type=F32), 0.0)
        dg_ref[...] += jnp.sum(dvn * vhat, axis=0, keepdims=True)
        t = dvn * gn_v
        dvv = rstd * (t - vhat * jnp.mean(t * vhat, axis=-1, keepdims=True))
        dc_ref[...] = (jnp.concatenate([du, dvv], axis=1) * _gelu_grad(c)).astype(BF16)

    w_spec = pl.BlockSpec((4, SGU_CHUNK, SGU_CHUNK), lambda i: (0, 0, 0))
    tile = pl.BlockSpec((SGU_CHUNK, SGU_W), lambda i: (0, 0))
    vec = pl.BlockSpec((1, SGU_W), lambda i: (0, 0))
    return pl.pallas_call(
        body, out_shape=(jax.ShapeDtypeStruct((S, 2 * SGU_W), BF16), jax.ShapeDtypeStruct((4, SGU_CHUNK, SGU_CHUNK), F32),
                         jax.ShapeDtypeStruct((SGU_CHUNK, SGU_W), F32), jax.ShapeDtypeStruct((1, SGU_W), F32)),
        grid=(S // SGU_CHUNK,),
        in_specs=[pl.BlockSpec((SGU_CHUNK, SGU_W), lambda i: (i, 0)),
                  pl.BlockSpec((SGU_CHUNK, 2 * SGU_W), lambda i: (i, P_C // (2 * SGU_W))), vec, w_spec, w_spec, tile],
        out_specs=(pl.BlockSpec((SGU_CHUNK, 2 * SGU_W), lambda i: (i, 0)), w_spec, tile, vec),
        compiler_params=_cp(("arbitrary",)), name=name)(dsg, proj, gn, wm, wmt, bias)


def _sigmoid(z):
    return 1.0 / (1.0 + jnp.exp(-z))


def _merge_specs(tm):
    row = lambda n: pl.BlockSpec((tm, n), lambda i: (i, 0))
    gate = lambda b: pl.BlockSpec((tm, D), lambda i: (i, b))
    full = lambda r, c: pl.BlockSpec((r, c), lambda i: (0, 0))
    packed = pl.BlockSpec((4, 256, PACK_COLS), lambda i: (0, R_BRANCH // 256, 0))
    return row, gate, full, packed


def _branch_shards(c_ref, j):
    return c_ref[j, :, 0:256], c_ref[j, :, 256:512], c_ref[j, :, 512:768], c_ref[j, :, 768:1024]


def _merge_fwd(proj, ya, o, sg, packed_w, bg, name, tm=512):
    S = proj.shape[0]
    row, gate, full, packed = _merge_specs(tm)

    def body(g0, g1, g2, ya_ref, o_ref, sg_ref, c_ref, bg_ref, out_ref):
        yav, ov, sgv = ya_ref[...], o_ref[...], sg_ref[...]
        for j in range(4):
            cols = slice(256 * j, 256 * (j + 1))
            wa, wb0, wb1, wc = _branch_shards(c_ref, j)
            y = (jnp.dot(yav, wa, preferred_element_type=F32),
                 jnp.dot(ov[:, :256], wb0, preferred_element_type=F32) + jnp.dot(ov[:, 256:], wb1, preferred_element_type=F32),
                 jnp.dot(sgv, wc, preferred_element_type=F32))
            acc = jnp.zeros((tm, 256), F32)
            for b, g_ref in enumerate((g0, g1, g2)):
                acc = acc + _sigmoid(g_ref[:, cols] + bg_ref[:, b * D + 256 * j:b * D + 256 * (j + 1)]) * y[b]
            out_ref[:, cols] = acc.astype(BF16)

    return pl.pallas_call(
        body, out_shape=jax.ShapeDtypeStruct((S, D), BF16), grid=(S // tm,),
        in_specs=[gate(0), gate(1), gate(2), row(POOL_W), row(FOX_W), row(SGU_W), packed, full(1, 3 * D)],
        out_specs=row(D), compiler_params=_cp(("parallel",)), name=name)(proj, proj, proj, ya, o, sg, packed_w, bg)


def _merge_bwd(dm, proj, ya, o, sg, packed_w, bg, grads, name, tm=512):
    S = proj.shape[0]
    row, gate, full, packed = _merge_specs(tm)
    tn_dims = (((0,), (0,)), ((), ()))
    nt_dims = (((1,), (1,)), ((), ()))

    def body(dm_ref, g0, g1, g2, ya_ref, o_ref, sg_ref, c_ref, bg_ref, _, dg_ref, dya_ref, do_ref, dsg_ref, dc_ref, dbg_ref, acc):
        i = pl.program_id(0)

        @pl.when(i == 0)
        def _():
            acc[...] = jnp.zeros_like(acc)
            dbg_ref[...] = jnp.zeros_like(dbg_ref)

        yav, ov, sgv = ya_ref[...], o_ref[...], sg_ref[...]
        o0, o1 = ov[:, :256], ov[:, 256:]
        dya = jnp.zeros((tm, POOL_W), F32)
        do0 = jnp.zeros((tm, 256), F32)
        do1 = jnp.zeros((tm, 256), F32)
        dsg = jnp.zeros((tm, SGU_W), F32)
        for j in range(4):
            cols = slice(256 * j, 256 * (j + 1))
            wa, wb0, wb1, wc = _branch_shards(c_ref, j)
            y = (jnp.dot(yav, wa, preferred_element_type=F32),
                 jnp.dot(o0, wb0, preferred_element_type=F32) + jnp.dot(o1, wb1, preferred_element_type=F32),
                 jnp.dot(sgv, wc, preferred_element_type=F32))
            dmv = dm_ref[:, cols]
            dy = []
            for b, g_ref in enumerate((g0, g1, g2)):
                bcols = slice(b * D + 256 * j, b * D + 256 * (j + 1))
                gt = _sigmoid(g_ref[:, cols] + bg_ref[:, bcols])
                dgp = dmv * y[b] * gt * (1.0 - gt)
                dg_ref[:, bcols] = dgp.astype(BF16)
                dbg_ref[:, bcols] += jnp.sum(dgp, axis=0, keepdims=True)
                dy.append((dmv * gt).astype(BF16))
            dya = dya + lax.dot_general(dy[0], wa, nt_dims, preferred_element_type=F32)
            do0 = do0 + lax.dot_general(dy[1], wb0, nt_dims, preferred_element_type=F32)
            do1 = do1 + lax.dot_general(dy[1], wb1, nt_dims, preferred_element_type=F32)
            dsg = dsg + lax.dot_general(dy[2], wc, nt_dims, preferred_element_type=F32)
            acc[j, :, 0:256] += lax.dot_general(yav, dy[0], tn_dims, preferred_element_type=F32)
            acc[j, :, 256:512] += lax.dot_general(o0, dy[1], tn_dims, preferred_element_type=F32)
            acc[j, :, 512:768] += lax.dot_general(o1, dy[1], tn_dims, preferred_element_type=F32)
            acc[j, :, 768:1024] += lax.dot_general(sgv, dy[2], tn_dims, preferred_element_type=F32)
        dya_ref[...] = dya
        do_ref[:, :256] = do0
        do_ref[:, 256:] = do1
        dsg_ref[...] = dsg

        @pl.when(i == pl.num_programs(0) - 1)
        def _():
            dc_ref[...] = acc[...].astype(dc_ref.dtype)

    return pl.pallas_call(
        body, out_shape=(jax.ShapeDtypeStruct((S, 3 * D), BF16), jax.ShapeDtypeStruct((S, POOL_W), F32),
                         jax.ShapeDtypeStruct((S, FOX_W), F32), jax.ShapeDtypeStruct((S, SGU_W), F32),
                         jax.ShapeDtypeStruct(grads.shape, grads.dtype), jax.ShapeDtypeStruct((1, 3 * D), F32)),
        grid=(S // tm,),
        in_specs=[row(D), gate(0), gate(1), gate(2), row(POOL_W), row(FOX_W), row(SGU_W), packed, full(1, 3 * D), ANY],
        out_specs=(row(3 * D), row(POOL_W), row(FOX_W), row(SGU_W), packed, full(1, 3 * D)),
        scratch_shapes=[pltpu.VMEM((4, 256, PACK_COLS), F32)], input_output_aliases={9: 4},
        compiler_params=_cp(("arbitrary",)), name=name)(dm, proj, proj, proj, ya, o, sg, packed_w, bg, grads)


def _xattn_probs(qh, kh):
    s = lax.dot_general(qh, kh, (((1,), (1,)), ((), ())), preferred_element_type=F32) * X_SCALE
    p = jnp.exp(s - jnp.max(s, axis=-1, keepdims=True))
    return p / jnp.sum(p, axis=-1, keepdims=True)


def _xattn_fwd(xq, kv, name, tq=512):
    S = xq.shape[0]
    M = kv.shape[0]

    def body(q_ref, k_ref, v_ref, o_ref):
        for h in range(XH):
            sl = slice(h * XHD, (h + 1) * XHD)
            p = _xattn_probs(q_ref[:, sl], k_ref[:, sl])
            o_ref[:, sl] = jnp.dot(p.astype(BF16), v_ref[:, sl], preferred_element_type=F32).astype(BF16)

    return pl.pallas_call(
        body, out_shape=jax.ShapeDtypeStruct((S, D), BF16), grid=(S // tq,),
        in_specs=[pl.BlockSpec((tq, D), lambda i: (i, 0)), pl.BlockSpec((M, D), lambda i: (0, 0)),
                  pl.BlockSpec((M, D), lambda i: (0, 1))],
        out_specs=pl.BlockSpec((tq, D), lambda i: (i, 0)), compiler_params=_cp(("parallel",)), name=name)(xq, kv, kv)


def _xattn_bwd(xq, kv, do, name, tq=512):
    S = xq.shape[0]
    M = kv.shape[0]

    def body(q_ref, k_ref, v_ref, do_ref, dq_ref, dkv_ref, dk_acc, dv_acc):
        i = pl.program_id(0)

        @pl.when(i == 0)
        def _():
            dk_acc[...] = jnp.zeros_like(dk_acc)
            dv_acc[...] = jnp.zeros_like(dv_acc)

        for h in range(XH):
            sl = slice(h * XHD, (h + 1) * XHD)
            qh, kh, vh, doh = q_ref[:, sl], k_ref[:, sl], v_ref[:, sl], do_ref[:, sl]
            p = _xattn_probs(qh, kh)
            dp = lax.dot_general(doh, vh, (((1,), (1,)), ((), ())), preferred_element_type=F32)
            ds = p * (dp - jnp.sum(p * dp, axis=-1, keepdims=True))
            dsb = (ds * X_SCALE).astype(BF16)
            dq_ref[:, sl] = jnp.dot(dsb, kh, preferred_element_type=F32).astype(BF16)
            dk_acc[:, sl] += lax.dot_general(dsb, qh, (((0,), (0,)), ((), ())), preferred_element_type=F32)
            dv_acc[:, sl] += lax.dot_general(p.astype(BF16), doh, (((0,), (0,)), ((), ())), preferred_element_type=F32)

        @pl.when(i == pl.num_programs(0) - 1)
        def _():
            dkv_ref[:, :D] = dk_acc[...].astype(BF16)
            dkv_ref[:, D:] = dv_acc[...].astype(BF16)

    return pl.pallas_call(
        body, out_shape=(jax.ShapeDtypeStruct((S, D), BF16), jax.ShapeDtypeStruct((M, 2 * D), BF16)), grid=(S // tq,),
        in_specs=[pl.BlockSpec((tq, D), lambda i: (i, 0)), pl.BlockSpec((M, D), lambda i: (0, 0)),
                  pl.BlockSpec((M, D), lambda i: (0, 1)), pl.BlockSpec((tq, D), lambda i: (i, 0))],
        out_specs=(pl.BlockSpec((tq, D), lambda i: (i, 0)), pl.BlockSpec((M, 2 * D), lambda i: (0, 0))),
        scratch_shapes=[pltpu.VMEM((M, D), F32), pltpu.VMEM((M, D), F32)],
        compiler_params=_cp(("arbitrary",)), name=name)(xq, kv, kv, do)


def _adam_math(gv, wv, mv, vv):
    c1 = 1.0 - ADAM_B1 ** ADAM_STEP
    c2 = 1.0 - ADAM_B2 ** ADAM_STEP
    nm = ADAM_B1 * mv + (1.0 - ADAM_B1) * gv
    nv = ADAM_B2 * vv + (1.0 - ADAM_B2) * (gv * gv)
    return -ADAM_LR * ((nm / c1) / (jnp.sqrt(nv / c2) + ADAM_EPS) + ADAM_WD * wv), nm, nv


def _adamw(g, w, m, v, name, block=None):
    if block is None:
        block = (1, 256 if g.shape[1] % 256 == 0 else g.shape[1], g.shape[2])
    grid = tuple(s // b for s, b in zip(g.shape, block))

    def body(g_ref, w_ref, m_ref, v_ref, d_ref, nm_ref, nv_ref):
        d_ref[...], nm_ref[...], nv_ref[...] = _adam_math(g_ref[...], w_ref[...], m_ref[...], v_ref[...])

    blk = pl.BlockSpec(block, lambda a, b, c: (a, b, c))
    return pl.pallas_call(
        body, out_shape=(jax.ShapeDtypeStruct(g.shape, F32),) * 3, grid=grid,
        in_specs=[blk] * 4, out_specs=(blk,) * 3, compiler_params=_cp(("parallel",) * 3), name=name)(g, w, m, v)


def _adamw_packed(red, w, m, v, g_index, name, token, tr=256):
    L, r, c = w.shape
    tr = min(tr, r)

    def body(g0_ref, g1_ref, w_ref, m_ref, v_ref, _, g_ref, d_ref, nm_ref, nv_ref):
        gv = jnp.where(pl.program_id(0) == 0, g0_ref[...], g1_ref[...])
        g_ref[0] = gv
        d_ref[0], nm_ref[0], nv_ref[0] = _adam_math(gv, w_ref[0], m_ref[0], v_ref[0])

    gblk = pl.BlockSpec((tr, c), lambda l, i: g_index(i))
    blk = pl.BlockSpec((1, tr, c), lambda l, i: (l, i, 0))
    return pl.pallas_call(
        body, out_shape=(jax.ShapeDtypeStruct(w.shape, F32),) * 4, grid=(L, r // tr),
        in_specs=[gblk, gblk, blk, blk, blk, pl.BlockSpec((8, 128), lambda l, i: (0, 0))], out_specs=(blk,) * 4,
        compiler_params=_cp(("parallel", "parallel")), name=name)(red[0], red[1], w, m, v, token)


def _row_tile(R):
    return next((t for t in (512, 496, 384, 256) if R % t == 0), R)


def _sum_slots(a, out_dtype, name):
    n, R, C = a.shape
    tr = _row_tile(R)

    def body(a_ref, o_ref):
        acc = a_ref[0].astype(F32)
        for k in range(1, n):
            acc = acc + a_ref[k].astype(F32)
        o_ref[...] = acc.astype(out_dtype)

    return pl.pallas_call(
        body, out_shape=jax.ShapeDtypeStruct((R, C), out_dtype), grid=(R // tr,),
        in_specs=[pl.BlockSpec((n, tr, C), lambda i: (0, i, 0))], out_specs=pl.BlockSpec((tr, C), lambda i: (i, 0)),
        compiler_params=_cp(("parallel",)), name=name)(a)


LANDING = pl.BlockSpec(memory_space=pltpu.VMEM)


def _landing_params(shape, dtype):
    return pltpu.CompilerParams(vmem_limit_bytes=math.prod(shape) * jnp.dtype(dtype).itemsize + 4 * 1024 * 1024)


def _place():
    return lax.axis_index("x"), lax.axis_index("y"), lax.axis_index("c")


def _other_chips(x, y):
    return [(1 - x, y), (x, 1 - y), (1 - x, 1 - y)]


def _row_chunks(rows, want, align=16):
    n = want
    while n > 1 and rows % (n * align):
        n -= 1
    return n


def _pair_add(g, name, nch=5):
    n, R, C = g.shape
    half = R // 2
    nch = _row_chunks(half, nch)
    cr = half // nch
    rb = next(t for t in (512, 256, 128, 64, 32, 16) if half % t == 0)

    def body(g_ref, p_ref, got, send_sems, recv_sems, local_sem):
        x, y, c = _place()
        mine0 = pl.multiple_of(c * half, 16)
        theirs0 = (1 - c) * half
        keep = pltpu.make_async_copy(g_ref.at[:, pl.ds(mine0, half), :], p_ref, local_sem)
        keep.start()
        cps = []
        for s in range(n):
            for q in range(nch):
                src = g_ref.at[s, pl.ds(pl.multiple_of(theirs0 + q * cr, 16), cr), :]
                cps.append(pltpu.make_async_remote_copy(
                    src_ref=src, dst_ref=got.at[s, pl.ds(q * cr, cr), :], send_sem=send_sems.at[s * nch + q],
                    recv_sem=recv_sems.at[s * nch + q], device_id=(x, y, 1 - c), device_id_type=MESH))
        for cp in cps:
            cp.start()
        for cp in cps:
            cp.wait()
        keep.wait()

        def add(i, _):
            rows = pl.ds(pl.multiple_of(i * rb, rb), rb)
            for s in range(n):
                p_ref[s, rows, :] = (p_ref[s, rows, :].astype(F32) + got[s, rows, :].astype(F32)).astype(BF16)
            return 0

        lax.fori_loop(0, half // rb, add, 0)

    shape = (n, half, C)
    return pl.pallas_call(
        body, out_shape=jax.ShapeDtypeStruct(shape, g.dtype), in_specs=[ANY], out_specs=LANDING,
        scratch_shapes=[pltpu.VMEM(shape, g.dtype), pltpu.SemaphoreType.DMA((n * nch,)), pltpu.SemaphoreType.DMA((n * nch,)),
                        pltpu.SemaphoreType.DMA],
        compiler_params=_landing_params((2,) + shape, g.dtype), name=name)(g)


def _pair_gather(t, name, nch=10):
    R = t.shape[0]
    nch = _row_chunks(R, nch, 8)
    cr = R // nch

    def body(t_ref, o_ref, send_sems, recv_sems, local_sem):
        x, y, c = _place()
        own = pltpu.make_async_copy(t_ref, o_ref.at[c], local_sem)
        own.start()
        cps = [pltpu.make_async_remote_copy(src_ref=t_ref.at[pl.ds(q * cr, cr), :], dst_ref=o_ref.at[c, pl.ds(q * cr, cr), :],
                                            send_sem=send_sems.at[q], recv_sem=recv_sems.at[q], device_id=(x, y, 1 - c),
                                            device_id_type=MESH) for q in range(nch)]
        for cp in cps:
            cp.start()
        for cp in cps:
            cp.wait()
        own.wait()

    return pl.pallas_call(
        body, out_shape=jax.ShapeDtypeStruct((2,) + t.shape, t.dtype), in_specs=[ANY], out_specs=LANDING,
        scratch_shapes=[pltpu.SemaphoreType.DMA((nch,)), pltpu.SemaphoreType.DMA((nch,)), pltpu.SemaphoreType.DMA],
        compiler_params=_landing_params((2,) + t.shape, t.dtype), name=name)(t)


HBM = pl.BlockSpec(memory_space=pltpu.HBM)
SEM = pl.BlockSpec(memory_space=pltpu.SEMAPHORE)
SPLIT_COPY = pltpu.CompilerParams(has_side_effects=pltpu.SideEffectType.DATAFLOW_SIDE_EFFECTING)


def _split_exchange(src, rows, src_of, tag, nch=5):
    C = src.shape[-1]
    nch = _row_chunks(rows, nch)
    cr = rows // nch
    n = 3 * nch
    land_shape = (4, rows, C)

    def copies(src_ref, land_ref, send_sems, recv_sems):
        x, y, c = _place()
        j = 2 * x + y
        out = []
        for q in range(nch):
            for k, (px, py) in enumerate(_other_chips(x, y)):
                out.append(pltpu.make_async_remote_copy(
                    src_ref=src_of(src_ref, px, py, c, q * cr, cr), dst_ref=land_ref.at[j, pl.ds(q * cr, cr), :],
                    send_sem=send_sems.at[k * nch + q], recv_sem=recv_sems.at[k * nch + q], device_id=(px, py, c),
                    device_id_type=MESH))
        return out

    def start(src_ref, land_ref, send_sems, recv_sems, src_thru, land_thru, token):
        for cp in copies(src_ref, land_ref, send_sems, recv_sems):
            cp.start()
        token[...] = jnp.zeros_like(token)

    send_sems, recv_sems, src_thru, land_thru, token = pl.pallas_call(
        start, name=f"{tag}_start",
        out_shape=(pltpu.SemaphoreType.DMA((n,)), pltpu.SemaphoreType.DMA((n,)), pltpu.HBM(src.shape, src.dtype),
                   pltpu.HBM(land_shape, src.dtype), jax.ShapeDtypeStruct((8, 128), F32)),
        in_specs=(HBM, HBM), out_specs=(SEM, SEM, HBM, HBM, pl.BlockSpec(memory_space=pltpu.VMEM)),
        input_output_aliases={0: 2, 1: 3}, compiler_params=SPLIT_COPY)(
            pltpu.with_memory_space_constraint(src, pltpu.HBM),
            pltpu.with_memory_space_constraint(lax.empty(land_shape, src.dtype), pltpu.HBM))

    def finish(after):
        def wait(src_ref, land_ref, send_sems, recv_sems, after_ref, src_dead, got_ref):
            for cp in copies(src_ref, land_ref, send_sems, recv_sems):
                cp.wait_send()
                cp.wait_recv()

        return pl.pallas_call(
            wait, name=f"{tag}_wait", out_shape=(pltpu.HBM(src.shape, src.dtype), pltpu.HBM(land_shape, src.dtype)),
            in_specs=(HBM, HBM, SEM, SEM, ANY), out_specs=(HBM, HBM), input_output_aliases={0: 0, 1: 1},
            compiler_params=SPLIT_COPY)(src_thru, land_thru, send_sems, recv_sems, after)

    return token, finish


def _gather_finish(shard, land, name, nch=5):
    R, C = shard.shape
    half = R // 2
    nch = _row_chunks(half, nch)
    cr = half // nch

    def body(s_ref, l_ref, o_ref, send_sems, recv_sems, local_sems):
        x, y, c = _place()
        j = 2 * x + y
        mine0 = c * half
        local = [pltpu.make_async_copy(s_ref, o_ref.at[j], local_sems.at[0])]
        remote = []
        for k, (px, py) in enumerate(_other_chips(x, y)):
            jj = 2 * px + py
            local.append(pltpu.make_async_copy(l_ref.at[jj], o_ref.at[jj, pl.ds(pl.multiple_of(mine0, 16), half), :],
                                               local_sems.at[1 + k]))
            for q in range(nch):
                remote.append(pltpu.make_async_remote_copy(
                    src_ref=l_ref.at[jj, pl.ds(q * cr, cr), :],
                    dst_ref=o_ref.at[jj, pl.ds(pl.multiple_of(mine0 + q * cr, 16), cr), :], send_sem=send_sems.at[k * nch + q],
                    recv_sem=recv_sems.at[k * nch + q], device_id=(x, y, 1 - c), device_id_type=MESH))
        for cp in local + remote:
            cp.start()
        for cp in remote + local:
            cp.wait()

    return pl.pallas_call(
        body, out_shape=jax.ShapeDtypeStruct((4, R, C), shard.dtype), in_specs=[ANY, ANY], out_specs=LANDING,
        scratch_shapes=[pltpu.SemaphoreType.DMA((3 * nch,)), pltpu.SemaphoreType.DMA((3 * nch,)), pltpu.SemaphoreType.DMA((4,))],
        compiler_params=_landing_params((4, R, C), shard.dtype), name=name)(shard, land)


def _sum_slots_own(land, own, name):
    n, R, C = land.shape
    tr = _row_tile(R)
    me = (2 * lax.axis_index("x") + lax.axis_index("y")).astype(jnp.int32).reshape(1)
    if own.ndim == 3:
        own_spec = pl.BlockSpec((None, tr, C), lambda i, me: (me[0], i, 0))
    else:
        own_spec = pl.BlockSpec((tr, C), lambda i, me: (i, 0))

    def body(me_ref, land_ref, own_ref, o_ref):
        acc = None
        for k in range(n):
            v = jnp.where(me_ref[0] == k, own_ref[...], land_ref[k]).astype(F32)
            acc = v if acc is None else acc + v
        o_ref[...] = acc

    return pl.pallas_call(
        body, out_shape=jax.ShapeDtypeStruct((R, C), F32),
        grid_spec=pltpu.PrefetchScalarGridSpec(
            num_scalar_prefetch=1, grid=(R // tr,),
            in_specs=[pl.BlockSpec((n, tr, C), lambda i, me: (0, i, 0)), own_spec],
            out_specs=pl.BlockSpec((tr, C), lambda i, me: (i, 0))),
        compiler_params=_cp(("parallel",)), name=name)(me, land, own)


def _reduce_begin(g, tag):
    p = _pair_add(g, f"rs_pair_{tag}")
    token, finish = _split_exchange(p, p.shape[1], lambda ref, px, py, c, r0, cr: ref.at[2 * px + py, pl.ds(r0, cr), :],
                                    f"rs_a2a_{tag}")
    return (finish, g.shape, tag), token


def _reduce_end(state, after):
    finish, shape, tag = state
    p, land = finish(after)
    t = _sum_slots_own(land, p, f"rs_sum_{tag}")
    return _pair_gather(t, f"rs_join_{tag}").reshape(shape[1], shape[2])


def _all_reduce_begin(v, tag):
    p = _sum_slots(_pair_gather(v, f"ar_pair_{tag}"), F32, f"ar_add_{tag}")
    token, finish = _split_exchange(p, p.shape[0], lambda ref, px, py, c, r0, cr: ref.at[pl.ds(r0, cr), :], f"ar_a2a_{tag}")
    return (finish, tag), token


def _all_reduce_end(state, after):
    finish, tag = state
    p, land = finish(after)
    return _sum_slots_own(land, p, f"ar_sum_{tag}")


def _gather_begin(shard, tag):
    half = shard.shape[0] // 2
    token, finish = _split_exchange(
        shard, half, lambda ref, px, py, c, r0, cr: ref.at[pl.ds(pl.multiple_of(c * half + r0, 16), cr), :], f"gather_{tag}")
    return (finish, tag), token


def _gather_end(state, after):
    finish, tag = state
    shard, land = finish(after)
    return _gather_finish(shard, land, f"gather_{tag}_finish")


R_BRANCH, R_OUT, R_WIN, ROWS_A = 0, 256, 512, 1888
R_FF1, R_FF2, R_XKV, R_XQ, R_XO, ROWS_B = 0, 1024, 2048, 2560, 2816, 3072
WIN_ROWS = N_IN // 4


def _w_in_t(a):
    return jnp.transpose(a, (2, 0, 1))


def _pack_shard(w, l):
    xkv, wb = w['w_xkv'][l], w['w_branch_b'][l]
    a = [jnp.concatenate([w['w_branch_a'][l], wb[:256], wb[256:], w['w_branch_c'][l]], axis=1), w['w_out'][l],
         jnp.pad(_w_in_t(w['w_in'])[:, l, :], ((0, ROWS_A - R_WIN - WIN_ROWS), (0, 0)))]
    b = [w['w_ff1'][l], w['w_ff2'][l], jnp.concatenate([xkv[:512], xkv[512:]], axis=1), w['w_xq'][l], w['w_xo'][l]]
    return jnp.concatenate(a, axis=0).astype(BF16), jnp.concatenate(b, axis=0).astype(BF16)


def _w_in_rows(gathered):
    t = gathered[:, R_WIN:R_WIN + WIN_ROWS, :].reshape(N_IN, PACK_COLS)
    return jnp.concatenate([t[2312:5384], t[256:1792], t[1800:2312], t[0:256],
                            jnp.pad(t[1792:1800], ((0, NP - P_F - 8), (0, 0)))], axis=0)


def _w_in_grad_rows(grads, dwt):
    t = jnp.concatenate([dwt[P_A:P_A + 256], dwt[P_Q:P_Q + 1536], dwt[P_F:P_F + 8], dwt[P_C:P_C + 512], dwt[P_G:P_G + 3072]],
                        axis=0)
    for j in range(4):
        rows = t[j * WIN_ROWS:(j + 1) * WIN_ROWS][None].astype(grads.dtype)
        grads = lax.dynamic_update_slice(grads, rows, (j, R_WIN, 0))
    return grads


def _small_prep(sw, l):
    eye = jnp.eye(4, dtype=F32)
    bd = jnp.einsum('gh,gcd->gchd', eye, sw['pool_w'][l]).reshape(POOL_W, POOL_W).astype(BF16)
    tril = jnp.tril(jnp.ones((SGU_CHUNK, SGU_CHUNK), F32))
    wm = (sw['sgu_w'][l] * tril[None]).astype(BF16)
    return dict(
        g_mix=sw['norm_mix_g'][l][None], g_x=sw['norm_xattn_g'][l][None], g_mem=sw['norm_mem_g'][l][None],
        g_ffn=sw['norm_ffn_g'][l][None], bd=bd, pool_scale=sw['pool_scale'][l][None],
        bf=jnp.pad(sw['b_forget'][l], (0, FCOLS - 8))[None], sgu_g=sw['sgu_norm_g'][l][None], wm=wm,
        wmt=jnp.transpose(wm, (0, 2, 1)), sgu_bias=jnp.repeat(sw['sgu_b'][l].T, 64, axis=1), bg=sw['b_gate'][l][None])


def _rows4(r0):
    return dict(n=D, k=D, tn=D, b_block=(4, 256, PACK_COLS), b_index=lambda i, j, k: (0, r0 // 256, 0))


def _rows_t(r0):
    return dict(tb=True, n=D, k=D, tn=D, b_block=(4, 256, PACK_COLS), b_index=lambda i, j, k: (0, r0 // 256, 0))


def _rows_grad(r0):
    return dict(ta=True, tm=D, tn=512, o_block=(4, 256, 512), o_index=lambda i, j, k: (0, r0 // 256, j))


def _add_to(r, e):
    return e + r


def _after(v, token):
    return v if token is None else v + token[0, 0]


def _layer_fwd(x, mem, GA, w_in_t, sp, l, token, second):
    t = f"l{l}"
    S = x.shape[0]
    h = _rms_fwd(x, _after(sp['g_mix'], token), f"rms_mix_{t}")
    proj = _mm(h, w_in_t, name=f"proj_{t}", out_dtype=F32, tb=True)
    d, ya = _pool_fwd(proj, sp['bd'], sp['pool_scale'], f"pool_fwd_{t}")
    fcum = _fgate_fwd(proj, sp['bf'], f"fgate_fwd_{t}")
    f8 = fcum[:, :8]
    fcol = f8.reshape(S, 4, 2).transpose(1, 0, 2)
    frow = f8.T.reshape(4, 2, S)
    qkv = proj[:, P_Q:P_Q + 3 * FOX_W].astype(BF16)
    o, o32, lse = _fox_fwd(qkv, fcol, frow, f"fox_fwd_{t}")
    sg = _sgu_fwd(proj, sp['sgu_g'], sp['wm'], sp['sgu_bias'], f"sgu_fwd_{t}")
    merged = _merge_fwd(proj, ya, o, sg, GA, sp['bg'], f"merge_fwd_{t}")
    x1 = _mm(merged, GA, name=f"out_{t}", out_dtype=F32, extra=x, epi=_add_to, **_rows4(R_OUT))
    GB, token = second(x1)
    hx = _rms_fwd(x1, _after(sp['g_x'], token), f"rms_x_{t}")
    hm = _rms_fwd(mem, sp['g_mem'], f"rms_mem_{t}")
    xq = _mm(hx, GB, name=f"xq_{t}", out_dtype=BF16, **_rows4(R_XQ))
    kv = _mm(hm, GB, name=f"xkv_{t}", out_dtype=BF16, n=2 * D, k=D, tn=512, tk=512, b_block=(None, 512, 512),
             b_index=lambda i, j, k: (j, R_XKV // 512, k))
    o2 = _xattn_fwd(xq, kv, f"xattn_fwd_{t}")
    x2 = _mm(o2, GB, name=f"xo_{t}", out_dtype=F32, extra=x1, epi=_add_to, **_rows4(R_XO))
    hf = _rms_fwd(x2, sp['g_ffn'], f"rms_ffn_{t}")
    z = _mm(hf, GB, name=f"ff1_{t}", out_dtype=BF16, n=D_FF, k=D, tn=D, b_block=(None, 1024, PACK_COLS),
            b_index=lambda i, j, k: (j, R_FF1 // 1024, 0))
    x3 = _mm(z, GB, name=f"ff2_{t}", out_dtype=F32, a_fn=_relu2, extra=x2, epi=_add_to, n=D, k=D_FF, tk=1024, tn=D,
             b_block=(None, 1024, PACK_COLS), b_index=lambda i, j, k: (k, R_FF2 // 1024, 0))
    saved = dict(x=x, h=h, proj=proj, d=d, ya=ya, fcol=fcol, frow=frow, qkv=qkv, o=o, o32=o32, lse=lse, sg=sg, merged=merged,
                 x1=x1, hx=hx, hm=hm, xq=xq, kv=kv, o2=o2, x2=x2, hf=hf, z=z, GA=GA, GB=GB, w_in_t=w_in_t)
    return x3, saved


def _layer_bwd(dx3, mem, sp, sv, l, token, early):
    t = f"l{l}"
    S = dx3.shape[0]
    GA, GB = sv['GA'], sv['GB']
    gs = {}
    dx3 = _after(dx3, token)
    gb = lax.empty((4, ROWS_B, PACK_COLS), BF16)
    dz = _mm(dx3, GB, name=f"d_a2_{t}", out_dtype=BF16, tb=True, n=D_FF, k=D, tn=D, b_block=(None, 1024, PACK_COLS),
             b_index=lambda i, j, k: (j, R_FF2 // 1024, 0), extra=sv['z'],
             epi=lambda r, e: r * (2.0 * jnp.maximum(e.astype(F32), 0.0)))
    gb = _mm(sv['z'], dx3, name=f"dw_ff2_{t}", out_dtype=BF16, ta=True, a_fn=_relu2, into=gb, tm=1024, tn=D,
             o_block=(None, 1024, PACK_COLS), o_index=lambda i, j, k: (i, R_FF2 // 1024, 0))
    gb = _mm(sv['hf'], dz, name=f"dw_ff1_{t}", out_dtype=BF16, ta=True, into=gb, tm=1024, tn=D,
             o_block=(None, 1024, PACK_COLS), o_index=lambda i, j, k: (j, R_FF1 // 1024, 0))
    dhf = _mm(dz, GB, name=f"d_hf_{t}", out_dtype=F32, tb=True, n=D, k=D_FF, tn=D, tk=1024, b_block=(None, 1024, PACK_COLS),
              b_index=lambda i, j, k: (k, R_FF1 // 1024, 0))
    dx2, gs['norm_ffn_g'] = _rms_bwd(dhf, sv['x2'], sp['g_ffn'], dx3, f"rms_ffn_bwd_{t}")
    do2 = _mm(dx2, GB, name=f"d_o2_{t}", out_dtype=BF16, **_rows_t(R_XO))
    gb = _mm(sv['o2'], dx2, name=f"dw_xo_{t}", out_dtype=BF16, into=gb, **_rows_grad(R_XO))
    dxq, dkv = _xattn_bwd(sv['xq'], sv['kv'], do2, f"xattn_bwd_{t}")
    gb = _mm(sv['hm'], dkv, name=f"dw_xkv_{t}", out_dtype=BF16, ta=True, into=gb, tm=512, tn=512,
             o_block=(None, 512, 512), o_index=lambda i, j, k: (j, R_XKV // 512, i))
    dhm = _mm(dkv, GB, name=f"d_hm_{t}", out_dtype=F32, tb=True, n=D, k=2 * D, tn=512, tk=512, b_block=(None, 512, 512),
              b_index=lambda i, j, k: (k, R_XKV // 512, j))
    gs['norm_mem_g'] = _rms_bwd(dhm, mem, sp['g_mem'], None, f"rms_mem_bwd_{t}")
    gb = _mm(sv['hx'], dxq, name=f"dw_xq_{t}", out_dtype=BF16, into=gb, **_rows_grad(R_XQ))
    token = early(gb)
    dhx = _mm(dxq, GB, name=f"d_hx_{t}", out_dtype=F32, **_rows_t(R_XQ))
    dx1, gs['norm_xattn_g'] = _rms_bwd(dhx, sv['x1'], _after(sp['g_x'], token), dx2, f"rms_x_bwd_{t}")
    ga = jnp.zeros((4, ROWS_A, PACK_COLS), BF16)
    ga = _mm(sv['merged'], dx1, name=f"dw_out_{t}", out_dtype=BF16, into=ga, **_rows_grad(R_OUT))
    dm = _mm(dx1, GA, name=f"d_merged_{t}", out_dtype=F32, **_rows_t(R_OUT))
    dg, dya, do, dsg, ga, gs['b_gate'] = _merge_bwd(dm, sv['proj'], sv['ya'], sv['o'], sv['sg'], GA, sp['bg'], ga, f"merge_bwd_{t}")
    dc, dws, dbias, gs['sgu_norm_g'] = _sgu_bwd(dsg, sv['proj'], sp['sgu_g'], sp['wm'], sp['wmt'], sp['sgu_bias'], f"sgu_bwd_{t}")
    tril = jnp.tril(jnp.ones((SGU_CHUNK, SGU_CHUNK), F32))
    gs['sgu_w'] = dws * tril[None]
    gs['sgu_b'] = dbias.reshape(SGU_CHUNK, 4, 64).sum(-1).T
    dq, dk, dv, dfrow, dfcol = _fox_bwd(sv['qkv'], sv['o32'], do, sv['lse'], sv['fcol'], sv['frow'], f"fox_bwd_{t}")
    dF = jnp.pad(dfrow.reshape(8, S).T + dfcol.transpose(1, 0, 2).reshape(S, 8), ((0, 0), (0, FCOLS - 8)))
    df, dbf = _fgate_bwd(dF, sv['proj'], sp['bf'], f"fgate_bwd_{t}")
    gs['b_forget'] = dbf[:, :8]
    da, dbd, gs['pool_scale'] = _pool_bwd(dya, sv['d'], sp['bd'], sp['pool_scale'], f"pool_bwd_{t}")
    gs['pool_w'] = jnp.stack([dbd[g * 64:(g + 1) * 64, g * 64:(g + 1) * 64] for g in range(4)])
    dproj = jnp.concatenate([dg, dq, dk, dv, dc, da, df], axis=1)
    dwt = _mm(dproj, sv['h'], name=f"dw_in_{t}", out_dtype=BF16, ta=True, tm=512, tn=1024)
    ga = _w_in_grad_rows(ga, dwt)
    dh = _mm(dproj, sv['w_in_t'], name=f"d_h_{t}", out_dtype=F32, tk=512, tn=D)
    dx, gs['norm_mix_g'] = _rms_bwd(dh, sv['x'], sp['g_mix'], dx1, f"rms_mix_bwd_{t}")
    return dx, ga, gs


SMALL_ROWS = 1424
GRAD_BLOCKS = {
    'w_ff1': ('b', lambda i: (R_FF1 // 256 + i, 0)), 'w_ff2': ('b', lambda i: (R_FF2 // 256 + i, 0)),
    'w_xq': ('b', lambda i: (R_XQ // 256 + i, 0)), 'w_xo': ('b', lambda i: (R_XO // 256 + i, 0)),
    'w_xkv': ('b', lambda i: (R_XKV // 256 + i % 2, i // 2)), 'w_out': ('a', lambda i: (R_OUT // 256 + i, 0)),
    'w_branch_a': ('a', lambda i: (R_BRANCH // 256, 0)), 'w_branch_b': ('a', lambda i: (R_BRANCH // 256, 1 + i)),
    'w_branch_c': ('a', lambda i: (R_BRANCH // 256, 3)),
}


def _pack_small(parts):
    flat = jnp.concatenate([p.reshape(-1) for p in parts])
    return jnp.pad(flat, (0, SMALL_ROWS * 128 - flat.shape[0])).reshape(SMALL_ROWS, 128)


def _unpack_small(buf, shapes):
    flat, out, r = buf.reshape(-1), [], 0
    for s in shapes:
        n = math.prod(s)
        out.append(flat[r:r + n].reshape(s))
        r += n
    return out


def kernel(x, mem, norm_mix_g, w_in, b_forget, pool_w, pool_scale, sgu_norm_g, sgu_w, sgu_b, w_branch_a, w_branch_b, w_branch_c, b_gate, w_out, norm_xattn_g, norm_mem_g, w_xq, w_xkv, w_xo, norm_ffn_g, w_ff1, w_ff2, final_norm_g, loss_target, m_norm_mix_g, m_w_in, m_b_forget, m_pool_w, m_pool_scale, m_sgu_norm_g, m_sgu_w, m_sgu_b, m_w_branch_a, m_w_branch_b, m_w_branch_c, m_b_gate, m_w_out, m_norm_xattn_g, m_norm_mem_g, m_w_xq, m_w_xkv, m_w_xo, m_norm_ffn_g, m_w_ff1, m_w_ff2, m_final_norm_g, v_norm_mix_g, v_w_in, v_b_forget, v_pool_w, v_pool_scale, v_sgu_norm_g, v_sgu_w, v_sgu_b, v_w_branch_a, v_w_branch_b, v_w_branch_c, v_b_gate, v_w_out, v_norm_xattn_g, v_norm_mem_g, v_w_xq, v_w_xkv, v_w_xo, v_norm_ffn_g, v_w_ff1, v_w_ff2, v_final_norm_g):
    args = (norm_mix_g, w_in, b_forget, pool_w, pool_scale, sgu_norm_g, sgu_w, sgu_b, w_branch_a, w_branch_b, w_branch_c, b_gate,
            w_out, norm_xattn_g, norm_mem_g, w_xq, w_xkv, w_xo, norm_ffn_g, w_ff1, w_ff2, final_norm_g)
    margs = (m_norm_mix_g, m_w_in, m_b_forget, m_pool_w, m_pool_scale, m_sgu_norm_g, m_sgu_w, m_sgu_b, m_w_branch_a, m_w_branch_b,
             m_w_branch_c, m_b_gate, m_w_out, m_norm_xattn_g, m_norm_mem_g, m_w_xq, m_w_xkv, m_w_xo, m_norm_ffn_g, m_w_ff1, m_w_ff2,
             m_final_norm_g)
    vargs = (v_norm_mix_g, v_w_in, v_b_forget, v_pool_w, v_pool_scale, v_sgu_norm_g, v_sgu_w, v_sgu_b, v_w_branch_a, v_w_branch_b,
             v_w_branch_c, v_b_gate, v_w_out, v_norm_xattn_g, v_norm_mem_g, v_w_xq, v_w_xkv, v_w_xo, v_norm_ffn_g, v_w_ff1, v_w_ff2,
             v_final_norm_g)
    w = dict(zip(W_NAMES, args))
    mo = dict(zip(W_NAMES, margs))
    vo = dict(zip(W_NAMES, vargs))
    xs, mems, tgt = x[0], mem[0], loss_target[0]
    shards = [_pack_shard(w, l) for l in range(DEPTH)]
    preps = [_small_prep(w, l) for l in range(DEPTH)]

    first_a, _ = _gather_begin(shards[0][0], "a_l0")
    pending_b, token = _gather_begin(shards[0][1], "b_l0")
    GA = None
    act, saved = xs, []
    for l in range(DEPTH):
        nxt = {}
        if l + 1 < DEPTH:
            nxt['a'], ta = _gather_begin(shards[l + 1][0], f"a_l{l + 1}")
            token = ta if token is None else token + ta
        if l == 0:
            GA = _gather_end(first_a, shards[DEPTH - 1][1])

        def second(x1, l=l, pending_b=pending_b, nxt=nxt):
            GB = _gather_end(pending_b, x1)
            if l + 1 == DEPTH:
                return GB, None
            nxt['b'], tb = _gather_begin(shards[l + 1][1], f"b_l{l + 1}")
            return GB, tb

        act, sv = _layer_fwd(act, mems, GA, _w_in_rows(GA), preps[l], l, token, second)
        saved.append(sv)
        if l + 1 < DEPTH:
            GA = _gather_end(nxt['a'], act)
            pending_b, token = nxt['b'], None
    loss_part, dact, d_final_g = _loss_head(act, w['final_norm_g'][None], tgt, "loss_head")

    red_a, red_b, small_g = [None] * DEPTH, [None] * DEPTH, [None] * DEPTH
    token, state_a = None, None
    for l in reversed(range(DEPTH)):
        early = {}

        def start_b(gb, l=l, early=early):
            early['state'], tok = _reduce_begin(gb, f"b_l{l}")
            return tok

        dact, ga, small_g[l] = _layer_bwd(dact, mems, preps[l], saved[l], l, token, start_b)
        if state_a is not None:
            red_a[l + 1] = _reduce_end(state_a, dact)
        red_b[l] = _reduce_end(early['state'], dact)
        state_a, token = _reduce_begin(ga, f"a_l{l}")
    grad_x = dact[None]
    per_layer = [n for n in SMALL_NAMES if n != 'final_norm_g']
    small_shapes = [w[n].shape for n in per_layer] + [(D,), (1,)]
    parts = [jnp.stack([small_g[l][n].reshape(w[n].shape[1:]) for l in range(DEPTH)]) for n in per_layer]
    state_small, token_small = _all_reduce_begin(_pack_small(parts + [d_final_g.reshape(D), loss_part.reshape(1)]), "small")
    token = token + token_small

    grads, delta, new_m, new_v = {}, {}, {}, {}
    for n, (buf, g_index) in GRAD_BLOCKS.items():
        if buf == 'b':
            grads[n], delta[n], new_m[n], new_v[n] = _adamw_packed(red_b, w[n], mo[n], vo[n], g_index, f"adamw_{n}", token)
    red_a[0] = _reduce_end(state_a, new_v['w_xkv'])
    small_red = _unpack_small(_all_reduce_end(state_small, red_a[0]), small_shapes)
    grads.update(zip(per_layer + ['final_norm_g'], small_red[:-1]))
    loss = small_red[-1].reshape(())
    for n, (buf, g_index) in GRAD_BLOCKS.items():
        if buf == 'a':
            grads[n], delta[n], new_m[n], new_v[n] = _adamw_packed(red_a, w[n], mo[n], vo[n], g_index, f"adamw_{n}", token)
    g_t = jnp.stack([r[R_WIN:R_WIN + WIN_ROWS] for r in red_a], axis=1)
    upd = _adamw(g_t, _w_in_t(w['w_in']), _w_in_t(mo['w_in']), _w_in_t(vo['w_in']), "adamw_w_in", block=(WIN_ROWS, DEPTH, 128))
    grads['w_in'], delta['w_in'], new_m['w_in'], new_v['w_in'] = [jnp.transpose(a, (1, 2, 0)) for a in (g_t,) + tuple(upd)]
    small_all = per_layer + ['final_norm_g']
    shapes_all = [w[n].shape for n in small_all]
    packed = [_pack_small([d[n] for n in small_all])[None] for d in (grads, w, mo, vo)]
    ds, ms, vs = _adamw(*packed, "adamw_small")
    for n, a, b, c in zip(small_all, _unpack_small(ds[0], shapes_all), _unpack_small(ms[0], shapes_all), _unpack_small(vs[0], shapes_all)):
        delta[n], new_m[n], new_v[n] = a, b, c

    return (loss, grad_x, *[grads[n] for n in W_NAMES], *[delta[n] for n in W_NAMES], *[new_m[n] for n in W_NAMES],
            *[new_v[n] for n in W_NAMES])
```

```python
import math

import jax
import jax.numpy as jnp
from jax import lax
from jax.experimental import pallas as pl
from jax.experimental.pallas import tpu as pltpu

F32 = jnp.float32
BF16 = jnp.bfloat16

D = 1024
DEPTH = 2
POOL_W = 256
FOX_W = 512
SGU_W = 256
SGU_CHUNK = 128
N_IN = 5384
P_G, P_Q, P_K, P_V, P_C, P_A, P_F = 0, 3072, 3584, 4096, 4608, 5120, 5376
NP = 5632
XH, XHD = 4, 256
D_FF = 4096
EPS = 1e-6
NEG = -1e30
FOX_SCALE = 64 ** -0.5
X_SCALE = 256 ** -0.5
GELU_K = math.sqrt(2.0 / math.pi)
GELU_C = 0.044715

ADAM_LR, ADAM_B1, ADAM_B2, ADAM_EPS, ADAM_WD, ADAM_STEP = 0.001, 0.9, 0.999, 1e-08, 0.01, 10

VMEM_LIMIT = 48 * 1024 * 1024
MESH = pl.DeviceIdType.MESH

IN_NAMES = ['x', 'mem', 'norm_mix_g', 'w_in', 'b_forget', 'pool_w', 'pool_scale', 'sgu_norm_g', 'sgu_w', 'sgu_b',
            'w_branch_a', 'w_branch_b', 'w_branch_c', 'b_gate', 'w_out', 'norm_xattn_g', 'norm_mem_g', 'w_xq',
            'w_xkv', 'w_xo', 'norm_ffn_g', 'w_ff1', 'w_ff2', 'final_norm_g']
W_NAMES = IN_NAMES[2:]
BIG_NAMES = ['w_in', 'w_branch_a', 'w_branch_b', 'w_branch_c', 'w_out', 'w_xq', 'w_xkv', 'w_xo', 'w_ff1', 'w_ff2']
SMALL_NAMES = [n for n in W_NAMES if n not in BIG_NAMES]
PACK_COLS = 1024


ANY = pl.BlockSpec(memory_space=pl.ANY)


def _cp(sem=None):
    return pltpu.CompilerParams(dimension_semantics=sem, vmem_limit_bytes=VMEM_LIMIT)


def _mm(a, b, *, name, out_dtype, ta=False, tb=False, tm=1024, tn=512, tk=1024, a_fn=None, extra=None, epi=None,
        n=None, k=None, b_block=None, b_index=None, into=None, o_block=None, o_index=None, norm_bwd=None):
    M = a.shape[1] if ta else a.shape[0]
    K = k if k is not None else (a.shape[0] if ta else a.shape[1])
    N = n if n is not None else (b.shape[0] if tb else b.shape[1])
    tm, tn, tk = min(tm, M), min(tn, N), min(tk, K)
    assert M % tm == 0 and N % tn == 0 and K % tk == 0, (name, M, N, K)
    nk = K // tk
    a_spec = pl.BlockSpec((tk, tm), lambda i, j, k: (k, i)) if ta else pl.BlockSpec((tm, tk), lambda i, j, k: (i, k))
    if b_block is not None:
        b_spec = pl.BlockSpec(b_block, b_index)
    else:
        b_spec = pl.BlockSpec((tn, tk), lambda i, j, k: (j, k)) if tb else pl.BlockSpec((tk, tn), lambda i, j, k: (k, j))
    dn = (((0 if ta else 1,), (1 if tb else 0,)), ((), ()))
    tile = pl.BlockSpec((tm, tn), lambda i, j, k: (i, j))
    o_spec = pl.BlockSpec(o_block, o_index) if into is not None else tile
    in_specs = [a_spec, b_spec] + ([tile] if extra is not None else []) + ([ANY] if into is not None else [])
    vec = pl.BlockSpec((1, N), lambda i, j, k: (0, 0))
    if norm_bwd is not None:
        assert tn == N and extra is None and into is None, name
        in_specs += [tile, tile, vec]
    n_in = len(in_specs)

    def body(*refs):
        a_ref, b_ref = refs[0], refs[1]
        e_ref = refs[2] if extra is not None else None
        o_ref, acc_ref = refs[n_in], refs[-1]
        kk = pl.program_id(2)
        first_rows = pl.program_id(0) == 0

        @pl.when(kk == 0)
        def _():
            acc_ref[...] = jnp.zeros_like(acc_ref)

        av = a_ref[...]
        if a_fn is not None:
            av = a_fn(av)
        bv = b_ref[...]
        if bv.ndim == 3:
            bv = bv.reshape(-1, bv.shape[-1])
        acc_ref[...] += lax.dot_general(av.astype(BF16), bv.astype(BF16), dn, preferred_element_type=F32)

        @pl.when(kk == nk - 1)
        def _():
            r = acc_ref[...]
            if norm_bwd is not None:
                x_ref, r_ref, g_ref, dg_ref = refs[2], refs[3], refs[4], refs[n_in + 1]
                xv = x_ref[...]
                rstd = lax.rsqrt(jnp.mean(xv * xv, axis=-1, keepdims=True) + EPS)
                xhat = xv * rstd

                @pl.when(first_rows)
                def _():
                    dg_ref[...] = jnp.zeros_like(dg_ref)

                dg_ref[...] += jnp.sum(r * xhat, axis=0, keepdims=True)
                t = r * g_ref[...]
                o_ref[...] = r_ref[...] + rstd * (t - xhat * jnp.mean(t * xhat, axis=-1, keepdims=True))
                return
            if epi is not None:
                r = epi(r, e_ref[...])
            o_ref[...] = r.astype(o_ref.dtype).reshape(o_ref.shape)

    args = (a, b) + ((extra,) if extra is not None else ()) + ((into,) if into is not None else ())
    out_shape = jax.ShapeDtypeStruct(into.shape, into.dtype) if into is not None else jax.ShapeDtypeStruct((M, N), out_dtype)
    semantics = ("parallel", "parallel", "arbitrary")
    if norm_bwd is not None:
        xn, gn, dres = norm_bwd
        args += (xn, dres, gn)
        out_shape = (jax.ShapeDtypeStruct((M, N), F32), jax.ShapeDtypeStruct((1, N), F32))
        o_spec = (tile, vec)
        semantics = ("arbitrary", "arbitrary", "arbitrary")
    return pl.pallas_call(
        body, out_shape=out_shape, grid=(M // tm, N // tn, nk), in_specs=in_specs, out_specs=o_spec,
        scratch_shapes=[pltpu.VMEM((tm, tn), F32)], input_output_aliases={n_in - 1: 0} if into is not None else {},
        compiler_params=_cp(semantics), name=name)(*args)


def _relu2(z):
    r = jnp.maximum(z.astype(F32), 0.0)
    return r * r


def _rms_fwd(x, g, name, tr=512):
    R, n = x.shape
    tr = min(tr, R)

    def body(x_ref, g_ref, h_ref):
        xv = x_ref[...]
        rstd = lax.rsqrt(jnp.mean(xv * xv, axis=-1, keepdims=True) + EPS)
        h_ref[...] = (xv * rstd * g_ref[...]).astype(BF16)

    return pl.pallas_call(
        body, out_shape=jax.ShapeDtypeStruct((R, n), BF16), grid=(R // tr,),
        in_specs=[pl.BlockSpec((tr, n), lambda i: (i, 0)), pl.BlockSpec((1, n), lambda i: (0, 0))],
        out_specs=pl.BlockSpec((tr, n), lambda i: (i, 0)), compiler_params=_cp(("parallel",)), name=name)(x, g)


def _rms_bwd(dh, x, g, dres, name, tr=512):
    R, n = x.shape
    tr = min(tr, R)
    need_dx = dres is not None

    def body(*refs):
        if need_dx:
            dh_ref, x_ref, g_ref, r_ref, dx_ref, dg_ref = refs
        else:
            dh_ref, x_ref, g_ref, dg_ref = refs
        i = pl.program_id(0)
        xv = x_ref[...]
        dhv = dh_ref[...].astype(F32)
        rstd = lax.rsqrt(jnp.mean(xv * xv, axis=-1, keepdims=True) + EPS)
        xhat = xv * rstd

        @pl.when(i == 0)
        def _():
            dg_ref[...] = jnp.zeros_like(dg_ref)

        dg_ref[...] += jnp.sum(dhv * xhat, axis=0, keepdims=True)
        if need_dx:
            t = dhv * g_ref[...]
            dx_ref[...] = r_ref[...] + rstd * (t - xhat * jnp.mean(t * xhat, axis=-1, keepdims=True))

    row = pl.BlockSpec((tr, n), lambda i: (i, 0))
    vec = pl.BlockSpec((1, n), lambda i: (0, 0))
    if need_dx:
        return pl.pallas_call(
            body, out_shape=(jax.ShapeDtypeStruct((R, n), F32), jax.ShapeDtypeStruct((1, n), F32)), grid=(R // tr,),
            in_specs=[row, row, vec, row], out_specs=(row, vec), compiler_params=_cp(("arbitrary",)), name=name)(dh, x, g, dres)
    return pl.pallas_call(
        body, out_shape=jax.ShapeDtypeStruct((1, n), F32), grid=(R // tr,),
        in_specs=[row, row, vec], out_specs=vec, compiler_params=_cp(("arbitrary",)), name=name)(dh, x, g)


def _loss_head(x, g, tgt, name, tr=512):
    R, n = x.shape

    def body(x_ref, g_ref, t_ref, loss_ref, dx_ref, dg_ref):
        i = pl.program_id(0)
        xv = x_ref[...]
        gv = g_ref[...]
        rstd = lax.rsqrt(jnp.mean(xv * xv, axis=-1, keepdims=True) + EPS)
        xhat = xv * rstd
        e = xhat * gv - t_ref[...]

        @pl.when(i == 0)
        def _():
            loss_ref[...] = jnp.zeros_like(loss_ref)
            dg_ref[...] = jnp.zeros_like(dg_ref)

        loss_ref[...] += 0.5 * jnp.sum(jnp.sum(e * e, axis=-1, keepdims=True) / n, axis=0, keepdims=True)
        dy = e / n
        dg_ref[...] += jnp.sum(dy * xhat, axis=0, keepdims=True)
        t = dy * gv
        dx_ref[...] = rstd * (t - xhat * jnp.mean(t * xhat, axis=-1, keepdims=True))

    row = pl.BlockSpec((tr, n), lambda i: (i, 0))
    vec = pl.BlockSpec((1, n), lambda i: (0, 0))
    one = pl.BlockSpec((1, 1), lambda i: (0, 0))
    return pl.pallas_call(
        body, out_shape=(jax.ShapeDtypeStruct((1, 1), F32), jax.ShapeDtypeStruct((R, n), F32), jax.ShapeDtypeStruct((1, n), F32)),
        grid=(R // tr,), in_specs=[row, vec, row], out_specs=(one, row, vec),
        compiler_params=_cp(("arbitrary",)), name=name)(x, g, tgt)


def _pool_masks(S):
    row = lax.broadcasted_iota(jnp.int32, (S, POOL_W), 0)
    grp = lax.broadcasted_iota(jnp.int32, (S, POOL_W), 1) // 64
    win = jnp.where(grp == 0, 2, jnp.where(grp == 1, 4, jnp.where(grp == 2, 8, 16)))
    cnt = jnp.minimum(row + 1, win).astype(F32)
    return row, grp, cnt


def _by_group(grp, v0, v1, v2, v3):
    return jnp.where(grp == 0, v0, jnp.where(grp == 1, v1, jnp.where(grp == 2, v2, v3)))


def _pool_fwd(proj, bd, scale, name):
    S = proj.shape[0]

    def body(a_ref, bd_ref, sc_ref, d_ref, y_ref):
        a = a_ref[...]
        row, grp, cnt = _pool_masks(S)

        def back(v, k):
            return jnp.where(row >= k, pltpu.roll(v, k, 0), 0.0)

        s1 = a + back(a, 1)
        s2 = s1 + back(s1, 2)
        s3 = s2 + back(s2, 4)
        s4 = s3 + back(s3, 8)
        d = (_by_group(grp, s1, s2, s3, s4) / cnt - a).astype(BF16)
        d_ref[...] = d
        y_ref[...] = (jnp.dot(d, bd_ref[...], preferred_element_type=F32) * sc_ref[...]).astype(BF16)

    full = lambda r, c: pl.BlockSpec((r, c), lambda i: (0, 0))
    return pl.pallas_call(
        body, out_shape=(jax.ShapeDtypeStruct((S, POOL_W), BF16), jax.ShapeDtypeStruct((S, POOL_W), BF16)), grid=(1,),
        in_specs=[pl.BlockSpec((S, POOL_W), lambda i: (0, P_A // POOL_W)), full(POOL_W, POOL_W), full(1, POOL_W)],
        out_specs=(full(S, POOL_W), full(S, POOL_W)), compiler_params=_cp(("arbitrary",)), name=name)(proj, bd, scale)


def _pool_bwd(dya, d, bd, scale, name):
    S = dya.shape[0]

    def body(dy_ref, d_ref, bd_ref, sc_ref, da_ref, dbd_ref, dsc_ref):
        dy = dy_ref[...]
        dv = d_ref[...]
        bdv = bd_ref[...]
        row, grp, cnt = _pool_masks(S)
        yraw = jnp.dot(dv, bdv, preferred_element_type=F32)
        dsc_ref[...] = jnp.sum(dy * yraw, axis=0, keepdims=True)
        tb = (dy * sc_ref[...]).astype(BF16)
        dbd_ref[...] = lax.dot_general(dv, tb, (((0,), (0,)), ((), ())), preferred_element_type=F32)
        dd = lax.dot_general(tb, bdv, (((1,), (1,)), ((), ())), preferred_element_type=F32)
        e = dd / cnt

        def fwd(v, k):
            return jnp.where(row < S - k, pltpu.roll(v, S - k, 0), 0.0)

        r1 = e + fwd(e, 1)
        r2 = r1 + fwd(r1, 2)
        r3 = r2 + fwd(r2, 4)
        r4 = r3 + fwd(r3, 8)
        da_ref[...] = (_by_group(grp, r1, r2, r3, r4) - dd).astype(BF16)

    full = lambda r, c: pl.BlockSpec((r, c), lambda i: (0, 0))
    return pl.pallas_call(
        body, out_shape=(jax.ShapeDtypeStruct((S, POOL_W), BF16), jax.ShapeDtypeStruct((POOL_W, POOL_W), F32),
                         jax.ShapeDtypeStruct((1, POOL_W), F32)), grid=(1,),
        in_specs=[full(S, POOL_W), full(S, POOL_W), full(POOL_W, POOL_W), full(1, POOL_W)],
        out_specs=(full(S, POOL_W), full(POOL_W, POOL_W), full(1, POOL_W)),
        compiler_params=_cp(("arbitrary",)), name=name)(dya, d, bd, scale)


FCOLS = 128


def _log_sigmoid(z):
    return -(jnp.maximum(-z, 0.0) + jnp.log1p(jnp.exp(-jnp.abs(z))))


def _fgate_fwd(proj, bf, name):
    S = proj.shape[0]

    def body(f_ref, b_ref, o_ref):
        v = _log_sigmoid(f_ref[...] + b_ref[...])
        row = lax.broadcasted_iota(jnp.int32, (S, FCOLS), 0)
        k = 1
        while k < S:
            v = v + jnp.where(row >= k, pltpu.roll(v, k, 0), 0.0)
            k *= 2
        o_ref[...] = v

    return pl.pallas_call(
        body, out_shape=jax.ShapeDtypeStruct((S, FCOLS), F32), grid=(1,),
        in_specs=[pl.BlockSpec((S, FCOLS), lambda i: (0, P_F // FCOLS)), pl.BlockSpec((1, FCOLS), lambda i: (0, 0))],
        out_specs=pl.BlockSpec((S, FCOLS), lambda i: (0, 0)), compiler_params=_cp(("arbitrary",)), name=name)(proj, bf)


def _fgate_bwd(dF, proj, bf, name):
    S = proj.shape[0]

    def body(dF_ref, f_ref, b_ref, df_ref, db_ref):
        v = dF_ref[...]
        row = lax.broadcasted_iota(jnp.int32, (S, FCOLS), 0)
        k = 1
        while k < S:
            v = v + jnp.where(row < S - k, pltpu.roll(v, S - k, 0), 0.0)
            k *= 2
        z = f_ref[...] + b_ref[...]
        df = v * (1.0 / (1.0 + jnp.exp(z)))
        db_ref[...] = jnp.sum(df, axis=0, keepdims=True)
        df_ref[...] = jnp.concatenate([df, jnp.zeros_like(df)], axis=1).astype(BF16)

    return pl.pallas_call(
        body, out_shape=(jax.ShapeDtypeStruct((S, 2 * FCOLS), BF16), jax.ShapeDtypeStruct((1, FCOLS), F32)), grid=(1,),
        in_specs=[pl.BlockSpec((S, FCOLS), lambda i: (0, 0)), pl.BlockSpec((S, FCOLS), lambda i: (0, P_F // FCOLS)),
                  pl.BlockSpec((1, FCOLS), lambda i: (0, 0))],
        out_specs=(pl.BlockSpec((S, 2 * FCOLS), lambda i: (0, 0)), pl.BlockSpec((1, FCOLS), lambda i: (0, 0))),
        compiler_params=_cp(("arbitrary",)), name=name)(dF, proj, bf)


def _fox_scores(qe, kj, fq, fk, r0, c0, tq, tk, diagonal):
    s = lax.dot_general(qe, kj, (((1,), (1,)), ((), ())), preferred_element_type=F32) * FOX_SCALE
    s = s + (fq - fk)
    if not diagonal:
        return s
    rows = r0 + lax.broadcasted_iota(jnp.int32, (tq, tk), 0)
    cols = c0 + lax.broadcasted_iota(jnp.int32, (tq, tk), 1)
    return jnp.where(rows >= cols, s, NEG)


FOX_TQ, FOX_TK = 512, 512


def _fox_fwd(qkv, fcol, frow, name):
    S = qkv.shape[0]
    tq, tk = FOX_TQ, min(FOX_TK, S)

    def body(q_ref, k_ref, v_ref, fc_ref, fr_ref, o_ref, o32_ref, lse_ref):
        i = pl.program_id(1)
        r0 = i * tq
        q = q_ref[...]
        half = lax.broadcasted_iota(jnp.int32, (tq, 128), 1) // 64
        qs = [jnp.where(half == e, q, jnp.zeros_like(q)) for e in (0, 1)]
        fqs = [fc_ref[0, :, e:e + 1] for e in (0, 1)]

        def step(j, carry, diagonal=False):
            c0 = pl.multiple_of(j * tk, tk)
            kj = k_ref[pl.ds(c0, tk), :]
            vj = v_ref[pl.ds(c0, tk), :]
            out = []
            for e in (0, 1):
                m, l, acc = carry[e]
                s = _fox_scores(qs[e], kj, fqs[e], fr_ref[0, e:e + 1, pl.ds(c0, tk)], r0, c0, tq, tk, diagonal)
                m_new = jnp.maximum(m, jnp.max(s, axis=-1, keepdims=True))
                alpha = jnp.exp(m - m_new)
                p = jnp.exp(s - m_new)
                out.append((m_new, alpha * l + jnp.sum(p, axis=-1, keepdims=True),
                            alpha * acc + jnp.dot(p.astype(BF16), vj, preferred_element_type=F32)))
            return tuple(out)

        init = (jnp.full((tq, 1), NEG, F32), jnp.zeros((tq, 1), F32), jnp.zeros((tq, 128), F32))
        below = r0 // tk
        carry = lax.fori_loop(0, below, step, (init, init))
        carry = step(below, carry, diagonal=True)
        outs = []
        for e in (0, 1):
            m, l, acc = carry[e]
            outs.append(acc / l)
            lse_ref[0, :, e:e + 1] = m + jnp.log(l)
        o = jnp.where(half == 0, outs[0], outs[1])
        o32_ref[...] = o
        o_ref[...] = o.astype(BF16)

    tile = pl.BlockSpec((tq, 128), lambda h, i: (i, h))
    return pl.pallas_call(
        body, out_shape=(jax.ShapeDtypeStruct((S, FOX_W), BF16), jax.ShapeDtypeStruct((S, FOX_W), F32),
                         jax.ShapeDtypeStruct((4, S, 2), F32)), grid=(4, S // tq),
        in_specs=[tile, pl.BlockSpec((S, 128), lambda h, i: (0, 4 + h)), pl.BlockSpec((S, 128), lambda h, i: (0, 8 + h)),
                  pl.BlockSpec((1, tq, 2), lambda h, i: (h, i, 0)), pl.BlockSpec((1, 2, S), lambda h, i: (h, 0, 0))],
        out_specs=(tile, tile, pl.BlockSpec((1, tq, 2), lambda h, i: (h, i, 0))),
        compiler_params=_cp(("parallel", "parallel")), name=name)(qkv, qkv, qkv, fcol, frow)


def _fox_bwd(qkv, o32, do, lse, fcol, frow, name):
    S = qkv.shape[0]
    tq, tk = FOX_TQ, min(FOX_TK, S)
    nq = S // tq

    def body(q_ref, k_ref, v_ref, o_ref, do_ref, lse_ref, fc_ref, fr_ref, dq_ref, dk_ref, dv_ref, dfr_ref, dfc_ref, dk_acc, dv_acc):
        dk_acc[...] = jnp.zeros_like(dk_acc)
        dv_acc[...] = jnp.zeros_like(dv_acc)
        dfr_ref[...] = jnp.zeros_like(dfr_ref)
        half = lax.broadcasted_iota(jnp.int32, (tq, 128), 1) // 64

        def q_block(i, _):
            r0 = pl.multiple_of(i * tq, tq)
            qi = q_ref[pl.ds(r0, tq), :]
            dob = do_ref[pl.ds(r0, tq), :].astype(BF16)
            row_dot = dob.astype(F32) * o_ref[pl.ds(r0, tq), :]
            qs = [jnp.where(half == e, qi, jnp.zeros_like(qi)) for e in (0, 1)]
            dos = [jnp.where(half == e, dob, jnp.zeros_like(dob)) for e in (0, 1)]
            deltas = [jnp.sum(jnp.where(half == e, row_dot, 0.0), axis=-1, keepdims=True) for e in (0, 1)]
            lses = [lse_ref[0, pl.ds(r0, tq), e:e + 1] for e in (0, 1)]
            fqs = [fc_ref[0, pl.ds(r0, tq), e:e + 1] for e in (0, 1)]

            def step(j, carry, diagonal=False):
                dqs, row_sums = carry
                c0 = pl.multiple_of(j * tk, tk)
                kj = k_ref[pl.ds(c0, tk), :]
                vj = v_ref[pl.ds(c0, tk), :]
                new_dq, new_rows, dkc, dvc = [], [], [], []
                for e in (0, 1):
                    s = _fox_scores(qs[e], kj, fqs[e], fr_ref[0, e:e + 1, pl.ds(c0, tk)], r0, c0, tq, tk, diagonal)
                    p = jnp.exp(s - lses[e])
                    dp = lax.dot_general(dos[e], vj, (((1,), (1,)), ((), ())), preferred_element_type=F32)
                    ds = p * (dp - deltas[e])
                    dfr_ref[0, e:e + 1, pl.ds(c0, tk)] -= jnp.sum(ds, axis=0, keepdims=True)
                    new_rows.append(row_sums[e] + jnp.sum(ds, axis=-1, keepdims=True))
                    dsb = (ds * FOX_SCALE).astype(BF16)
                    dkc.append(lax.dot_general(dsb, qi, (((0,), (0,)), ((), ())), preferred_element_type=F32))
                    dvc.append(lax.dot_general(p.astype(BF16), dob, (((0,), (0,)), ((), ())), preferred_element_type=F32))
                    new_dq.append(dqs[e] + jnp.dot(dsb, kj, preferred_element_type=F32))
                half_k = lax.broadcasted_iota(jnp.int32, (tk, 128), 1) // 64
                dk_acc[pl.ds(c0, tk), :] += jnp.where(half_k == 0, dkc[0], dkc[1])
                dv_acc[pl.ds(c0, tk), :] += jnp.where(half_k == 0, dvc[0], dvc[1])
                return tuple(new_dq), tuple(new_rows)

            zero, zero_col = jnp.zeros((tq, 128), F32), jnp.zeros((tq, 1), F32)
            below = r0 // tk
            carry = lax.fori_loop(0, below, step, ((zero, zero), (zero_col, zero_col)))
            dqs, row_sums = step(below, carry, diagonal=True)
            for e in (0, 1):
                dfc_ref[0, pl.ds(r0, tq), e:e + 1] = row_sums[e]
            dq_ref[pl.ds(r0, tq), :] = jnp.where(half == 0, dqs[0], dqs[1]).astype(BF16)
            return 0

        lax.fori_loop(0, nq, q_block, 0)
        dk_ref[...] = dk_acc[...].astype(BF16)
        dv_ref[...] = dv_acc[...].astype(BF16)

    col = lambda off: pl.BlockSpec((S, 128), lambda h: (0, off + h))
    hs2 = pl.BlockSpec((1, S, 2), lambda h: (h, 0, 0))
    h2s = pl.BlockSpec((1, 2, S), lambda h: (h, 0, 0))
    return pl.pallas_call(
        body, out_shape=(jax.ShapeDtypeStruct((S, FOX_W), BF16),) * 3 + (jax.ShapeDtypeStruct((4, 2, S), F32),
                                                                         jax.ShapeDtypeStruct((4, S, 2), F32)), grid=(4,),
        in_specs=[col(0), col(4), col(8), col(0), col(0), hs2, hs2, h2s],
        out_specs=(col(0), col(0), col(0), h2s, hs2),
        scratch_shapes=[pltpu.VMEM((S, 128), F32), pltpu.VMEM((S, 128), F32)],
        compiler_params=_cp(("parallel",)), name=name)(qkv, qkv, qkv, o32, do, lse, fcol, frow)


def _gelu(x):
    return 0.5 * x * (1.0 + jnp.tanh(GELU_K * (x + GELU_C * x * x * x)))


def _gelu_grad(x):
    th = jnp.tanh(GELU_K * (x + GELU_C * x * x * x))
    return 0.5 * (1.0 + th) + 0.5 * x * (1.0 - th * th) * GELU_K * (1.0 + 3.0 * GELU_C * x * x)


def _sgu_parts(c, gn, w_ref, bias):
    zc = _gelu(c)
    u, vv = zc[:, :SGU_W], zc[:, SGU_W:]
    rstd = lax.rsqrt(jnp.mean(vv * vv, axis=-1, keepdims=True) + EPS)
    vhat = vv * rstd
    vnb = (vhat * gn).astype(BF16)
    grp = lax.broadcasted_iota(jnp.int32, (SGU_CHUNK, SGU_W), 1) // 64
    mixed = bias
    for gi in range(4):
        mixed = mixed + jnp.where(grp == gi, jnp.dot(w_ref[gi], vnb, preferred_element_type=F32), 0.0)
    return u, rstd, vhat, vnb, grp, mixed


def _sgu_fwd(proj, gn, wm, bias, name):
    S = proj.shape[0]

    def body(c_ref, g_ref, w_ref, b_ref, o_ref):
        u, _, _, _, _, mixed = _sgu_parts(c_ref[...], g_ref[...], w_ref, b_ref[...])
        o_ref[...] = (u * mixed).astype(BF16)

    return pl.pallas_call(
        body, out_shape=jax.ShapeDtypeStruct((S, SGU_W), BF16), grid=(S // SGU_CHUNK,),
        in_specs=[pl.BlockSpec((SGU_CHUNK, 2 * SGU_W), lambda i: (i, P_C // (2 * SGU_W))),
                  pl.BlockSpec((1, SGU_W), lambda i: (0, 0)), pl.BlockSpec((4, SGU_CHUNK, SGU_CHUNK), lambda i: (0, 0, 0)),
                  pl.BlockSpec((SGU_CHUNK, SGU_W), lambda i: (0, 0))],
        out_specs=pl.BlockSpec((SGU_CHUNK, SGU_W), lambda i: (i, 0)),
        compiler_params=_cp(("parallel",)), name=name)(proj, gn, wm, bias)


def _sgu_bwd(dsg, proj, gn, wm, wmt, bias, name):
    S = proj.shape[0]

    def body(dsg_ref, c_ref, g_ref, w_ref, wt_ref, b_ref, dc_ref, dw_ref, db_ref, dg_ref):
        i = pl.program_id(0)

        @pl.when(i == 0)
        def _():
            dw_ref[...] = jnp.zeros_like(dw_ref)
            db_ref[...] = jnp.zeros_like(db_ref)
            dg_ref[...] = jnp.zeros_like(dg_ref)

        c = c_ref[...]
        gn_v = g_ref[...]
        u, rstd, vhat, vnb, grp, mixed = _sgu_parts(c, gn_v, w_ref, b_ref[...])
        dsg_v = dsg_ref[...]
        du = dsg_v * mixed
        dmix = dsg_v * u
        db_ref[...] += dmix
        dmb = dmix.astype(BF16)
        dvn = jnp.zeros((SGU_CHUNK, SGU_W), F32)
        for gi in range(4):
            dmg = jnp.where(grp == gi, dmb, jnp.zeros_like(dmb))
            dw_ref[gi] += lax.dot_general(dmg, vnb, (((1,), (1,)), ((), ())), preferred_element_type=F32)
            dvn = dvn + jnp.where(grp == gi, jnp.dot(wt_ref[gi], dmb, preferred_element_type=F32), 0.0)
        dg_ref[...] += jnp.sum(dvn * vhat, axis=0, keepdims=True)
        t = dvn * gn_v
        dvv = rstd * (t - vhat * jnp.mean(t * vhat, axis=-1, keepdims=True))
        dc_ref[...] = (jnp.concatenate([du, dvv], axis=1) * _gelu_grad(c)).astype(BF16)

    w_spec = pl.BlockSpec((4, SGU_CHUNK, SGU_CHUNK), lambda i: (0, 0, 0))
    tile = pl.BlockSpec((SGU_CHUNK, SGU_W), lambda i: (0, 0))
    vec = pl.BlockSpec((1, SGU_W), lambda i: (0, 0))
    return pl.pallas_call(
        body, out_shape=(jax.ShapeDtypeStruct((S, 2 * SGU_W), BF16), jax.ShapeDtypeStruct((4, SGU_CHUNK, SGU_CHUNK), F32),
                         jax.ShapeDtypeStruct((SGU_CHUNK, SGU_W), F32), jax.ShapeDtypeStruct((1, SGU_W), F32)),
        grid=(S // SGU_CHUNK,),
        in_specs=[pl.BlockSpec((SGU_CHUNK, SGU_W), lambda i: (i, 0)),
                  pl.BlockSpec((SGU_CHUNK, 2 * SGU_W), lambda i: (i, P_C // (2 * SGU_W))), vec, w_spec, w_spec, tile],
        out_specs=(pl.BlockSpec((SGU_CHUNK, 2 * SGU_W), lambda i: (i, 0)), w_spec, tile, vec),
        compiler_params=_cp(("arbitrary",)), name=name)(dsg, proj, gn, wm, wmt, bias)


def _sigmoid(z):
    return 1.0 / (1.0 + jnp.exp(-z))


def _merge_specs(tm):
    row = lambda n: pl.BlockSpec((tm, n), lambda i: (i, 0))
    gate = lambda b: pl.BlockSpec((tm, D), lambda i: (i, b))
    full = lambda r, c: pl.BlockSpec((r, c), lambda i: (0, 0))
    packed = pl.BlockSpec((4, 256, PACK_COLS), lambda i: (0, R_BRANCH // 256, 0))
    return row, gate, full, packed


def _branch_shards(c_ref, j):
    return c_ref[j, :, 0:256], c_ref[j, :, 256:512], c_ref[j, :, 512:768], c_ref[j, :, 768:1024]


def _merge_fwd(proj, ya, o, sg, packed_w, bg, name, tm=512):
    S = proj.shape[0]
    row, gate, full, packed = _merge_specs(tm)

    def body(g0, g1, g2, ya_ref, o_ref, sg_ref, c_ref, bg_ref, out_ref):
        yav, ov, sgv = ya_ref[...], o_ref[...], sg_ref[...]
        for j in range(4):
            cols = slice(256 * j, 256 * (j + 1))
            wa, wb0, wb1, wc = _branch_shards(c_ref, j)
            y = (jnp.dot(yav, wa, preferred_element_type=F32),
                 jnp.dot(ov[:, :256], wb0, preferred_element_type=F32) + jnp.dot(ov[:, 256:], wb1, preferred_element_type=F32),
                 jnp.dot(sgv, wc, preferred_element_type=F32))
            acc = jnp.zeros((tm, 256), F32)
            for b, g_ref in enumerate((g0, g1, g2)):
                acc = acc + _sigmoid(g_ref[:, cols] + bg_ref[:, b * D + 256 * j:b * D + 256 * (j + 1)]) * y[b]
            out_ref[:, cols] = acc.astype(BF16)

    return pl.pallas_call(
        body, out_shape=jax.ShapeDtypeStruct((S, D), BF16), grid=(S // tm,),
        in_specs=[gate(0), gate(1), gate(2), row(POOL_W), row(FOX_W), row(SGU_W), packed, full(1, 3 * D)],
        out_specs=row(D), compiler_params=_cp(("parallel",)), name=name)(proj, proj, proj, ya, o, sg, packed_w, bg)


def _merge_bwd(dm, proj, ya, o, sg, packed_w, bg, grads, name, tm=512):
    S = proj.shape[0]
    row, gate, full, packed = _merge_specs(tm)
    tn_dims = (((0,), (0,)), ((), ()))
    nt_dims = (((1,), (1,)), ((), ()))

    def body(dm_ref, g0, g1, g2, ya_ref, o_ref, sg_ref, c_ref, bg_ref, _, dg_ref, dya_ref, do_ref, dsg_ref, dc_ref, dbg_ref, acc):
        i = pl.program_id(0)

        @pl.when(i == 0)
        def _():
            acc[...] = jnp.zeros_like(acc)
            dbg_ref[...] = jnp.zeros_like(dbg_ref)

        yav, ov, sgv = ya_ref[...], o_ref[...], sg_ref[...]
        o0, o1 = ov[:, :256], ov[:, 256:]
        dya = jnp.zeros((tm, POOL_W), F32)
        do0 = jnp.zeros((tm, 256), F32)
        do1 = jnp.zeros((tm, 256), F32)
        dsg = jnp.zeros((tm, SGU_W), F32)
        for j in range(4):
            cols = slice(256 * j, 256 * (j + 1))
            wa, wb0, wb1, wc = _branch_shards(c_ref, j)
            y = (jnp.dot(yav, wa, preferred_element_type=F32),
                 jnp.dot(o0, wb0, preferred_element_type=F32) + jnp.dot(o1, wb1, preferred_element_type=F32),
                 jnp.dot(sgv, wc, preferred_element_type=F32))
            dmv = dm_ref[:, cols]
            dy = []
            for b, g_ref in enumerate((g0, g1, g2)):
                bcols = slice(b * D + 256 * j, b * D + 256 * (j + 1))
                gt = _sigmoid(g_ref[:, cols] + bg_ref[:, bcols])
                dgp = dmv * y[b] * gt * (1.0 - gt)
                dg_ref[:, bcols] = dgp.astype(BF16)
                dbg_ref[:, bcols] += jnp.sum(dgp, axis=0, keepdims=True)
                dy.append((dmv * gt).astype(BF16))
            dya = dya + lax.dot_general(dy[0], wa, nt_dims, preferred_element_type=F32)
            do0 = do0 + lax.dot_general(dy[1], wb0, nt_dims, preferred_element_type=F32)
            do1 = do1 + lax.dot_general(dy[1], wb1, nt_dims, preferred_element_type=F32)
            dsg = dsg + lax.dot_general(dy[2], wc, nt_dims, preferred_element_type=F32)
            acc[j, :, 0:256] += lax.dot_general(yav, dy[0], tn_dims, preferred_element_type=F32)
            acc[j, :, 256:512] += lax.dot_general(o0, dy[1], tn_dims, preferred_element_type=F32)
            acc[j, :, 512:768] += lax.dot_general(o1, dy[1], tn_dims, preferred_element_type=F32)
            acc[j, :, 768:1024] += lax.dot_general(sgv, dy[2], tn_dims, preferred_element_type=F32)
        dya_ref[...] = dya
        do_ref[:, :256] = do0
        do_ref[:, 256:] = do1
        dsg_ref[...] = dsg

        @pl.when(i == pl.num_programs(0) - 1)
        def _():
            dc_ref[...] = acc[...].astype(dc_ref.dtype)

    return pl.pallas_call(
        body, out_shape=(jax.ShapeDtypeStruct((S, 3 * D), BF16), jax.ShapeDtypeStruct((S, POOL_W), F32),
                         jax.ShapeDtypeStruct((S, FOX_W), F32), jax.ShapeDtypeStruct((S, SGU_W), F32),
                         jax.ShapeDtypeStruct(grads.shape, grads.dtype), jax.ShapeDtypeStruct((1, 3 * D), F32)),
        grid=(S // tm,),
        in_specs=[row(D), gate(0), gate(1), gate(2), row(POOL_W), row(FOX_W), row(SGU_W), packed, full(1, 3 * D), ANY],
        out_specs=(row(3 * D), row(POOL_W), row(FOX_W), row(SGU_W), packed, full(1, 3 * D)),
        scratch_shapes=[pltpu.VMEM((4, 256, PACK_COLS), F32)], input_output_aliases={9: 4},
        compiler_params=_cp(("arbitrary",)), name=name)(dm, proj, proj, proj, ya, o, sg, packed_w, bg, grads)


def _xattn_probs(qh, kh):
    s = lax.dot_general(qh, kh, (((1,), (1,)), ((), ())), preferred_element_type=F32) * X_SCALE
    p = jnp.exp(s - jnp.max(s, axis=-1, keepdims=True))
    return p / jnp.sum(p, axis=-1, keepdims=True)


def _xattn_fwd(xq, kv, name, tq=512):
    S = xq.shape[0]
    M = kv.shape[0]

    def body(q_ref, k_ref, v_ref, o_ref):
        for h in range(XH):
            sl = slice(h * XHD, (h + 1) * XHD)
            p = _xattn_probs(q_ref[:, sl], k_ref[:, sl])
            o_ref[:, sl] = jnp.dot(p.astype(BF16), v_ref[:, sl], preferred_element_type=F32).astype(BF16)

    return pl.pallas_call(
        body, out_shape=jax.ShapeDtypeStruct((S, D), BF16), grid=(S // tq,),
        in_specs=[pl.BlockSpec((tq, D), lambda i: (i, 0)), pl.BlockSpec((M, D), lambda i: (0, 0)),
                  pl.BlockSpec((M, D), lambda i: (0, 1))],
        out_specs=pl.BlockSpec((tq, D), lambda i: (i, 0)), compiler_params=_cp(("parallel",)), name=name)(xq, kv, kv)


def _xattn_bwd(xq, kv, do, name, tq=512):
    S = xq.shape[0]
    M = kv.shape[0]

    def body(q_ref, k_ref, v_ref, do_ref, dq_ref, dkv_ref, dk_acc, dv_acc):
        i = pl.program_id(0)

        @pl.when(i == 0)
        def _():
            dk_acc[...] = jnp.zeros_like(dk_acc)
            dv_acc[...] = jnp.zeros_like(dv_acc)

        for h in range(XH):
            sl = slice(h * XHD, (h + 1) * XHD)
            qh, kh, vh, doh = q_ref[:, sl], k_ref[:, sl], v_ref[:, sl], do_ref[:, sl]
            p = _xattn_probs(qh, kh)
            dp = lax.dot_general(doh, vh, (((1,), (1,)), ((), ())), preferred_element_type=F32)
            ds = p * (dp - jnp.sum(p * dp, axis=-1, keepdims=True))
            dsb = (ds * X_SCALE).astype(BF16)
            dq_ref[:, sl] = jnp.dot(dsb, kh, preferred_element_type=F32).astype(BF16)
            dk_acc[:, sl] += lax.dot_general(dsb, qh, (((0,), (0,)), ((), ())), preferred_element_type=F32)
            dv_acc[:, sl] += lax.dot_general(p.astype(BF16), doh, (((0,), (0,)), ((), ())), preferred_element_type=F32)

        @pl.when(i == pl.num_programs(0) - 1)
        def _():
            dkv_ref[:, :D] = dk_acc[...].astype(BF16)
            dkv_ref[:, D:] = dv_acc[...].astype(BF16)

    return pl.pallas_call(
        body, out_shape=(jax.ShapeDtypeStruct((S, D), BF16), jax.ShapeDtypeStruct((M, 2 * D), BF16)), grid=(S // tq,),
        in_specs=[pl.BlockSpec((tq, D), lambda i: (i, 0)), pl.BlockSpec((M, D), lambda i: (0, 0)),
                  pl.BlockSpec((M, D), lambda i: (0, 1)), pl.BlockSpec((tq, D), lambda i: (i, 0))],
        out_specs=(pl.BlockSpec((tq, D), lambda i: (i, 0)), pl.BlockSpec((M, 2 * D), lambda i: (0, 0))),
        scratch_shapes=[pltpu.VMEM((M, D), F32), pltpu.VMEM((M, D), F32)],
        compiler_params=_cp(("arbitrary",)), name=name)(xq, kv, kv, do)


def _adam_math(gv, wv, mv, vv):
    c1 = 1.0 - ADAM_B1 ** ADAM_STEP
    c2 = 1.0 - ADAM_B2 ** ADAM_STEP
    nm = ADAM_B1 * mv + (1.0 - ADAM_B1) * gv
    nv = ADAM_B2 * vv + (1.0 - ADAM_B2) * (gv * gv)
    return -ADAM_LR * ((nm / c1) / (jnp.sqrt(nv / c2) + ADAM_EPS) + ADAM_WD * wv), nm, nv


def _adamw(g, w, m, v, name, block=None):
    if block is None:
        block = (1, 256 if g.shape[1] % 256 == 0 else g.shape[1], g.shape[2])
    grid = tuple(s // b for s, b in zip(g.shape, block))

    def body(g_ref, w_ref, m_ref, v_ref, d_ref, nm_ref, nv_ref):
        d_ref[...], nm_ref[...], nv_ref[...] = _adam_math(g_ref[...], w_ref[...], m_ref[...], v_ref[...])

    blk = pl.BlockSpec(block, lambda a, b, c: (a, b, c))
    return pl.pallas_call(
        body, out_shape=(jax.ShapeDtypeStruct(g.shape, F32),) * 3, grid=grid,
        in_specs=[blk] * 4, out_specs=(blk,) * 3, compiler_params=_cp(("parallel",) * 3), name=name)(g, w, m, v)


def _adamw_packed(red, w, m, v, g_index, name, token, tr=256):
    L, r, c = w.shape
    tr = min(tr, r)

    def body(g0_ref, g1_ref, w_ref, m_ref, v_ref, _, g_ref, d_ref, nm_ref, nv_ref):
        gv = jnp.where(pl.program_id(0) == 0, g0_ref[...], g1_ref[...])
        g_ref[0] = gv
        d_ref[0], nm_ref[0], nv_ref[0] = _adam_math(gv, w_ref[0], m_ref[0], v_ref[0])

    gblk = pl.BlockSpec((tr, c), lambda l, i: g_index(i))
    blk = pl.BlockSpec((1, tr, c), lambda l, i: (l, i, 0))
    return pl.pallas_call(
        body, out_shape=(jax.ShapeDtypeStruct(w.shape, F32),) * 4, grid=(L, r // tr),
        in_specs=[gblk, gblk, blk, blk, blk, pl.BlockSpec((8, 128), lambda l, i: (0, 0))], out_specs=(blk,) * 4,
        compiler_params=_cp(("parallel", "parallel")), name=name)(red[0], red[1], w, m, v, token)


def _row_tile(R):
    return next((t for t in (512, 496, 384, 256) if R % t == 0), R)


def _sum_slots(a, out_dtype, name):
    n, R, C = a.shape
    tr = _row_tile(R)

    def body(a_ref, o_ref):
        acc = a_ref[0].astype(F32)
        for k in range(1, n):
            acc = acc + a_ref[k].astype(F32)
        o_ref[...] = acc.astype(out_dtype)

    return pl.pallas_call(
        body, out_shape=jax.ShapeDtypeStruct((R, C), out_dtype), grid=(R // tr,),
        in_specs=[pl.BlockSpec((n, tr, C), lambda i: (0, i, 0))], out_specs=pl.BlockSpec((tr, C), lambda i: (i, 0)),
        compiler_params=_cp(("parallel",)), name=name)(a)


LANDING = pl.BlockSpec(memory_space=pltpu.VMEM)


def _landing_params(shape, dtype):
    return pltpu.CompilerParams(vmem_limit_bytes=math.prod(shape) * jnp.dtype(dtype).itemsize + 4 * 1024 * 1024)


def _place():
    return lax.axis_index("x"), lax.axis_index("y"), lax.axis_index("c")


def _other_chips(x, y):
    return [(1 - x, y), (x, 1 - y), (1 - x, 1 - y)]


def _row_chunks(rows, want, align=16):
    n = want
    while n > 1 and rows % (n * align):
        n -= 1
    return n


def _pair_add(g, name, nch=5):
    n, R, C = g.shape
    half = R // 2
    nch = _row_chunks(half, nch)
    cr = half // nch
    rb = next(t for t in (512, 256, 128, 64, 32, 16) if half % t == 0)

    def body(g_ref, p_ref, got, send_sems, recv_sems, local_sem):
        x, y, c = _place()
        mine0 = pl.multiple_of(c * half, 16)
        theirs0 = (1 - c) * half
        keep = pltpu.make_async_copy(g_ref.at[:, pl.ds(mine0, half), :], p_ref, local_sem)
        keep.start()
        cps = []
        for s in range(n):
            for q in range(nch):
                src = g_ref.at[s, pl.ds(pl.multiple_of(theirs0 + q * cr, 16), cr), :]
                cps.append(pltpu.make_async_remote_copy(
                    src_ref=src, dst_ref=got.at[s, pl.ds(q * cr, cr), :], send_sem=send_sems.at[s * nch + q],
                    recv_sem=recv_sems.at[s * nch + q], device_id=(x, y, 1 - c), device_id_type=MESH))
        for cp in cps:
            cp.start()
        for cp in cps:
            cp.wait()
        keep.wait()

        def add(i, _):
            rows = pl.ds(pl.multiple_of(i * rb, rb), rb)
            for s in range(n):
                p_ref[s, rows, :] = (p_ref[s, rows, :].astype(F32) + got[s, rows, :].astype(F32)).astype(BF16)
            return 0

        lax.fori_loop(0, half // rb, add, 0)

    shape = (n, half, C)
    return pl.pallas_call(
        body, out_shape=jax.ShapeDtypeStruct(shape, g.dtype), in_specs=[ANY], out_specs=LANDING,
        scratch_shapes=[pltpu.VMEM(shape, g.dtype), pltpu.SemaphoreType.DMA((n * nch,)), pltpu.SemaphoreType.DMA((n * nch,)),
                        pltpu.SemaphoreType.DMA],
        compiler_params=_landing_params((2,) + shape, g.dtype), name=name)(g)


def _pair_gather(t, name, nch=10):
    R = t.shape[0]
    nch = _row_chunks(R, nch, 8)
    cr = R // nch

    def body(t_ref, o_ref, send_sems, recv_sems, local_sem):
        x, y, c = _place()
        own = pltpu.make_async_copy(t_ref, o_ref.at[c], local_sem)
        own.start()
        cps = [pltpu.make_async_remote_copy(src_ref=t_ref.at[pl.ds(q * cr, cr), :], dst_ref=o_ref.at[c, pl.ds(q * cr, cr), :],
                                            send_sem=send_sems.at[q], recv_sem=recv_sems.at[q], device_id=(x, y, 1 - c),
                                            device_id_type=MESH) for q in range(nch)]
        for cp in cps:
            cp.start()
        for cp in cps:
            cp.wait()
        own.wait()

    return pl.pallas_call(
        body, out_shape=jax.ShapeDtypeStruct((2,) + t.shape, t.dtype), in_specs=[ANY], out_specs=LANDING,
        scratch_shapes=[pltpu.SemaphoreType.DMA((nch,)), pltpu.SemaphoreType.DMA((nch,)), pltpu.SemaphoreType.DMA],
        compiler_params=_landing_params((2,) + t.shape, t.dtype), name=name)(t)


HBM = pl.BlockSpec(memory_space=pltpu.HBM)
SEM = pl.BlockSpec(memory_space=pltpu.SEMAPHORE)
SPLIT_COPY = pltpu.CompilerParams(has_side_effects=pltpu.SideEffectType.DATAFLOW_SIDE_EFFECTING)


def _split_exchange(src, rows, src_of, tag, nch=5):
    C = src.shape[-1]
    nch = _row_chunks(rows, nch)
    cr = rows // nch
    n = 3 * nch
    land_shape = (4, rows, C)

    def copies(src_ref, land_ref, send_sems, recv_sems):
        x, y, c = _place()
        j = 2 * x + y
        out = []
        for q in range(nch):
            for k, (px, py) in enumerate(_other_chips(x, y)):
                out.append(pltpu.make_async_remote_copy(
                    src_ref=src_of(src_ref, px, py, c, q * cr, cr), dst_ref=land_ref.at[j, pl.ds(q * cr, cr), :],
                    send_sem=send_sems.at[k * nch + q], recv_sem=recv_sems.at[k * nch + q], device_id=(px, py, c),
                    device_id_type=MESH))
        return out

    def start(src_ref, land_ref, send_sems, recv_sems, src_thru, land_thru, token):
        for cp in copies(src_ref, land_ref, send_sems, recv_sems):
            cp.start()
        token[...] = jnp.zeros_like(token)

    send_sems, recv_sems, src_thru, land_thru, token = pl.pallas_call(
        start, name=f"{tag}_start",
        out_shape=(pltpu.SemaphoreType.DMA((n,)), pltpu.SemaphoreType.DMA((n,)), pltpu.HBM(src.shape, src.dtype),
                   pltpu.HBM(land_shape, src.dtype), jax.ShapeDtypeStruct((8, 128), F32)),
        in_specs=(HBM, HBM), out_specs=(SEM, SEM, HBM, HBM, pl.BlockSpec(memory_space=pltpu.VMEM)),
        input_output_aliases={0: 2, 1: 3}, compiler_params=SPLIT_COPY)(
            pltpu.with_memory_space_constraint(src, pltpu.HBM),
            pltpu.with_memory_space_constraint(lax.empty(land_shape, src.dtype), pltpu.HBM))

    def finish(after):
        def wait(src_ref, land_ref, send_sems, recv_sems, after_ref, src_dead, got_ref):
            for cp in copies(src_ref, land_ref, send_sems, recv_sems):
                cp.wait_send()
                cp.wait_recv()

        return pl.pallas_call(
            wait, name=f"{tag}_wait", out_shape=(pltpu.HBM(src.shape, src.dtype), pltpu.HBM(land_shape, src.dtype)),
            in_specs=(HBM, HBM, SEM, SEM, ANY), out_specs=(HBM, HBM), input_output_aliases={0: 0, 1: 1},
            compiler_params=SPLIT_COPY)(src_thru, land_thru, send_sems, recv_sems, after)

    return token, finish


def _gather_finish(shard, land, name, nch=5):
    R, C = shard.shape
    half = R // 2
    nch = _row_chunks(half, nch)
    cr = half // nch

    def body(s_ref, l_ref, o_ref, send_sems, recv_sems, local_sems):
        x, y, c = _place()
        j = 2 * x + y
        mine0 = c * half
        local = [pltpu.make_async_copy(s_ref, o_ref.at[j], local_sems.at[0])]
        remote = []
        for k, (px, py) in enumerate(_other_chips(x, y)):
            jj = 2 * px + py
            local.append(pltpu.make_async_copy(l_ref.at[jj], o_ref.at[jj, pl.ds(pl.multiple_of(mine0, 16), half), :],
                                               local_sems.at[1 + k]))
            for q in range(nch):
                remote.append(pltpu.make_async_remote_copy(
                    src_ref=l_ref.at[jj, pl.ds(q * cr, cr), :],
                    dst_ref=o_ref.at[jj, pl.ds(pl.multiple_of(mine0 + q * cr, 16), cr), :], send_sem=send_sems.at[k * nch + q],
                    recv_sem=recv_sems.at[k * nch + q], device_id=(x, y, 1 - c), device_id_type=MESH))
        for cp in local + remote:
            cp.start()
        for cp in remote + local:
            cp.wait()

    return pl.pallas_call(
        body, out_shape=jax.ShapeDtypeStruct((4, R, C), shard.dtype), in_specs=[ANY, ANY], out_specs=LANDING,
        scratch_shapes=[pltpu.SemaphoreType.DMA((3 * nch,)), pltpu.SemaphoreType.DMA((3 * nch,)), pltpu.SemaphoreType.DMA((4,))],
        compiler_params=_landing_params((4, R, C), shard.dtype), name=name)(shard, land)


def _sum_slots_own(land, own, name):
    n, R, C = land.shape
    tr = _row_tile(R)
    me = (2 * lax.axis_index("x") + lax.axis_index("y")).astype(jnp.int32).reshape(1)
    if own.ndim == 3:
        own_spec = pl.BlockSpec((None, tr, C), lambda i, me: (me[0], i, 0))
    else:
        own_spec = pl.BlockSpec((tr, C), lambda i, me: (i, 0))

    def body(me_ref, land_ref, own_ref, o_ref):
        acc = None
        for k in range(n):
            v = jnp.where(me_ref[0] == k, own_ref[...], land_ref[k]).astype(F32)
            acc = v if acc is None else acc + v
        o_ref[...] = acc

    return pl.pallas_call(
        body, out_shape=jax.ShapeDtypeStruct((R, C), F32),
        grid_spec=pltpu.PrefetchScalarGridSpec(
            num_scalar_prefetch=1, grid=(R // tr,),
            in_specs=[pl.BlockSpec((n, tr, C), lambda i, me: (0, i, 0)), own_spec],
            out_specs=pl.BlockSpec((tr, C), lambda i, me: (i, 0))),
        compiler_params=_cp(("parallel",)), name=name)(me, land, own)


def _reduce_begin(g, tag):
    p = _pair_add(g, f"rs_pair_{tag}")
    token, finish = _split_exchange(p, p.shape[1], lambda ref, px, py, c, r0, cr: ref.at[2 * px + py, pl.ds(r0, cr), :],
                                    f"rs_a2a_{tag}")
    return (finish, g.shape, tag), token


def _reduce_end(state, after):
    finish, shape, tag = state
    p, land = finish(after)
    t = _sum_slots_own(land, p, f"rs_sum_{tag}")
    return _pair_gather(t, f"rs_join_{tag}").reshape(shape[1], shape[2])


def _all_reduce_begin(v, tag):
    p = _sum_slots(_pair_gather(v, f"ar_pair_{tag}"), F32, f"ar_add_{tag}")
    token, finish = _split_exchange(p, p.shape[0], lambda ref, px, py, c, r0, cr: ref.at[pl.ds(r0, cr), :], f"ar_a2a_{tag}")
    return (finish, tag), token


def _all_reduce_end(state, after):
    finish, tag = state
    p, land = finish(after)
    return _sum_slots_own(land, p, f"ar_sum_{tag}")


def _gather_begin(shard, tag):
    half = shard.shape[0] // 2
    token, finish = _split_exchange(
        shard, half, lambda ref, px, py, c, r0, cr: ref.at[pl.ds(pl.multiple_of(c * half + r0, 16), cr), :], f"gather_{tag}")
    return (finish, tag), token


def _gather_end(state, after):
    finish, tag = state
    shard, land = finish(after)
    return _gather_finish(shard, land, f"gather_{tag}_finish")


R_BRANCH, R_OUT, R_WIN, ROWS_A = 0, 256, 512, 1888
R_FF1, R_FF2, R_XKV, R_XQ, R_XO, ROWS_B = 0, 1024, 2048, 2560, 2816, 3072
WIN_ROWS = N_IN // 4


def _w_in_t(a):
    return jnp.transpose(a, (2, 0, 1))


def _pack_shard(w, l):
    xkv, wb = w['w_xkv'][l], w['w_branch_b'][l]
    a = [jnp.concatenate([w['w_branch_a'][l], wb[:256], wb[256:], w['w_branch_c'][l]], axis=1), w['w_out'][l],
         jnp.pad(_w_in_t(w['w_in'])[:, l, :], ((0, ROWS_A - R_WIN - WIN_ROWS), (0, 0)))]
    b = [w['w_ff1'][l], w['w_ff2'][l], jnp.concatenate([xkv[:512], xkv[512:]], axis=1), w['w_xq'][l], w['w_xo'][l]]
    return jnp.concatenate(a, axis=0).astype(BF16), jnp.concatenate(b, axis=0).astype(BF16)


def _w_in_rows(gathered):
    t = gathered[:, R_WIN:R_WIN + WIN_ROWS, :].reshape(N_IN, PACK_COLS)
    return jnp.concatenate([t[2312:5384], t[256:1792], t[1800:2312], t[0:256],
                            jnp.pad(t[1792:1800], ((0, NP - P_F - 8), (0, 0)))], axis=0)


def _w_in_grad_rows(grads, dwt):
    t = jnp.concatenate([dwt[P_A:P_A + 256], dwt[P_Q:P_Q + 1536], dwt[P_F:P_F + 8], dwt[P_C:P_C + 512], dwt[P_G:P_G + 3072]],
                        axis=0)
    for j in range(4):
        rows = t[j * WIN_ROWS:(j + 1) * WIN_ROWS][None].astype(grads.dtype)
        grads = lax.dynamic_update_slice(grads, rows, (j, R_WIN, 0))
    return grads


def _small_prep(sw, l):
    eye = jnp.eye(4, dtype=F32)
    bd = jnp.einsum('gh,gcd->gchd', eye, sw['pool_w'][l]).reshape(POOL_W, POOL_W).astype(BF16)
    tril = jnp.tril(jnp.ones((SGU_CHUNK, SGU_CHUNK), F32))
    wm = (sw['sgu_w'][l] * tril[None]).astype(BF16)
    return dict(
        g_mix=sw['norm_mix_g'][l][None], g_x=sw['norm_xattn_g'][l][None], g_mem=sw['norm_mem_g'][l][None],
        g_ffn=sw['norm_ffn_g'][l][None], bd=bd, pool_scale=sw['pool_scale'][l][None],
        bf=jnp.pad(sw['b_forget'][l], (0, FCOLS - 8))[None], sgu_g=sw['sgu_norm_g'][l][None], wm=wm,
        wmt=jnp.transpose(wm, (0, 2, 1)), sgu_bias=jnp.repeat(sw['sgu_b'][l].T, 64, axis=1), bg=sw['b_gate'][l][None])


def _rows4(r0):
    return dict(n=D, k=D, tn=D, b_block=(4, 256, PACK_COLS), b_index=lambda i, j, k: (0, r0 // 256, 0))


def _rows_t(r0):
    return dict(tb=True, n=D, k=D, tn=D, b_block=(4, 256, PACK_COLS), b_index=lambda i, j, k: (0, r0 // 256, 0))


def _rows_grad(r0):
    return dict(ta=True, tm=D, tn=512, o_block=(4, 256, 512), o_index=lambda i, j, k: (0, r0 // 256, j))


def _add_to(r, e):
    return e + r


def _after(v, token):
    return v if token is None else v + token[0, 0]


def _layer_fwd(x, mem, GA, w_in_t, sp, l, token, second):
    t = f"l{l}"
    S = x.shape[0]
    h = _rms_fwd(x, _after(sp['g_mix'], token), f"rms_mix_{t}")
    proj = _mm(h, w_in_t, name=f"proj_{t}", out_dtype=F32, tb=True)
    d, ya = _pool_fwd(proj, sp['bd'], sp['pool_scale'], f"pool_fwd_{t}")
    fcum = _fgate_fwd(proj, sp['bf'], f"fgate_fwd_{t}")
    f8 = fcum[:, :8]
    fcol = f8.reshape(S, 4, 2).transpose(1, 0, 2)
    frow = f8.T.reshape(4, 2, S)
    qkv = proj[:, P_Q:P_Q + 3 * FOX_W].astype(BF16)
    o, o32, lse = _fox_fwd(qkv, fcol, frow, f"fox_fwd_{t}")
    sg = _sgu_fwd(proj, sp['sgu_g'], sp['wm'], sp['sgu_bias'], f"sgu_fwd_{t}")
    merged = _merge_fwd(proj, ya, o, sg, GA, sp['bg'], f"merge_fwd_{t}")
    x1 = _mm(merged, GA, name=f"out_{t}", out_dtype=F32, extra=x, epi=_add_to, **_rows4(R_OUT))
    GB, token = second(x1)
    hx = _rms_fwd(x1, _after(sp['g_x'], token), f"rms_x_{t}")
    hm = _rms_fwd(mem, sp['g_mem'], f"rms_mem_{t}")
    xq = _mm(hx, GB, name=f"xq_{t}", out_dtype=BF16, **_rows4(R_XQ))
    kv = _mm(hm, GB, name=f"xkv_{t}", out_dtype=BF16, n=2 * D, k=D, tn=512, tk=512, b_block=(None, 512, 512),
             b_index=lambda i, j, k: (j, R_XKV // 512, k))
    o2 = _xattn_fwd(xq, kv, f"xattn_fwd_{t}")
    x2 = _mm(o2, GB, name=f"xo_{t}", out_dtype=F32, extra=x1, epi=_add_to, **_rows4(R_XO))
    hf = _rms_fwd(x2, sp['g_ffn'], f"rms_ffn_{t}")
    z = _mm(hf, GB, name=f"ff1_{t}", out_dtype=BF16, n=D_FF, k=D, tn=D, b_block=(None, 1024, PACK_COLS),
            b_index=lambda i, j, k: (j, R_FF1 // 1024, 0))
    x3 = _mm(z, GB, name=f"ff2_{t}", out_dtype=F32, a_fn=_relu2, extra=x2, epi=_add_to, n=D, k=D_FF, tk=1024, tn=D,
             b_block=(None, 1024, PACK_COLS), b_index=lambda i, j, k: (k, R_FF2 // 1024, 0))
    saved = dict(x=x, h=h, proj=proj, d=d, ya=ya, fcol=fcol, frow=frow, qkv=qkv, o=o, o32=o32, lse=lse, sg=sg, merged=merged,
                 x1=x1, hx=hx, hm=hm, xq=xq, kv=kv, o2=o2, x2=x2, hf=hf, z=z, GA=GA, GB=GB, w_in_t=w_in_t)
    return x3, saved


def _layer_bwd(dx3, mem, sp, sv, l, token, early):
    t = f"l{l}"
    S = dx3.shape[0]
    GA, GB = sv['GA'], sv['GB']
    gs = {}
    dx3 = _after(dx3, token)
    gb = lax.empty((4, ROWS_B, PACK_COLS), BF16)
    dz = _mm(dx3, GB, name=f"d_a2_{t}", out_dtype=BF16, tb=True, n=D_FF, k=D, tn=D, b_block=(None, 1024, PACK_COLS),
             b_index=lambda i, j, k: (j, R_FF2 // 1024, 0), extra=sv['z'],
             epi=lambda r, e: r * (2.0 * jnp.maximum(e.astype(F32), 0.0)))
    gb = _mm(sv['z'], dx3, name=f"dw_ff2_{t}", out_dtype=BF16, ta=True, a_fn=_relu2, into=gb, tm=1024, tn=D,
             o_block=(None, 1024, PACK_COLS), o_index=lambda i, j, k: (i, R_FF2 // 1024, 0))
    gb = _mm(sv['hf'], dz, name=f"dw_ff1_{t}", out_dtype=BF16, ta=True, into=gb, tm=1024, tn=D,
             o_block=(None, 1024, PACK_COLS), o_index=lambda i, j, k: (j, R_FF1 // 1024, 0))
    dx2, gs['norm_ffn_g'] = _mm(dz, GB, name=f"d_hf_{t}", out_dtype=F32, tb=True, n=D, k=D_FF, tm=512, tn=D, tk=1024,
                                b_block=(None, 1024, PACK_COLS), b_index=lambda i, j, k: (k, R_FF1 // 1024, 0),
                                norm_bwd=(sv['x2'], sp['g_ffn'], dx3))
    do2 = _mm(dx2, GB, name=f"d_o2_{t}", out_dtype=BF16, **_rows_t(R_XO))
    gb = _mm(sv['o2'], dx2, name=f"dw_xo_{t}", out_dtype=BF16, into=gb, **_rows_grad(R_XO))
    dxq, dkv = _xattn_bwd(sv['xq'], sv['kv'], do2, f"xattn_bwd_{t}")
    gb = _mm(sv['hm'], dkv, name=f"dw_xkv_{t}", out_dtype=BF16, ta=True, into=gb, tm=512, tn=512,
             o_block=(None, 512, 512), o_index=lambda i, j, k: (j, R_XKV // 512, i))
    dhm = _mm(dkv, GB, name=f"d_hm_{t}", out_dtype=F32, tb=True, n=D, k=2 * D, tn=512, tk=512, b_block=(None, 512, 512),
              b_index=lambda i, j, k: (k, R_XKV // 512, j))
    gs['norm_mem_g'] = _rms_bwd(dhm, mem, sp['g_mem'], None, f"rms_mem_bwd_{t}")
    gb = _mm(sv['hx'], dxq, name=f"dw_xq_{t}", out_dtype=BF16, into=gb, **_rows_grad(R_XQ))
    token = early(gb)
    dx1, gs['norm_xattn_g'] = _mm(dxq, GB, name=f"d_hx_{t}", out_dtype=F32, tm=512, **_rows_t(R_XQ),
                                  norm_bwd=(sv['x1'], _after(sp['g_x'], token), dx2))
    ga = jnp.zeros((4, ROWS_A, PACK_COLS), BF16)
    ga = _mm(sv['merged'], dx1, name=f"dw_out_{t}", out_dtype=BF16, into=ga, **_rows_grad(R_OUT))
    dm = _mm(dx1, GA, name=f"d_merged_{t}", out_dtype=F32, **_rows_t(R_OUT))
    dg, dya, do, dsg, ga, gs['b_gate'] = _merge_bwd(dm, sv['proj'], sv['ya'], sv['o'], sv['sg'], GA, sp['bg'], ga, f"merge_bwd_{t}")
    dc, dws, dbias, gs['sgu_norm_g'] = _sgu_bwd(dsg, sv['proj'], sp['sgu_g'], sp['wm'], sp['wmt'], sp['sgu_bias'], f"sgu_bwd_{t}")
    tril = jnp.tril(jnp.ones((SGU_CHUNK, SGU_CHUNK), F32))
    gs['sgu_w'] = dws * tril[None]
    gs['sgu_b'] = dbias.reshape(SGU_CHUNK, 4, 64).sum(-1).T
    dq, dk, dv, dfrow, dfcol = _fox_bwd(sv['qkv'], sv['o32'], do, sv['lse'], sv['fcol'], sv['frow'], f"fox_bwd_{t}")
    dF = jnp.pad(dfrow.reshape(8, S).T + dfcol.transpose(1, 0, 2).reshape(S, 8), ((0, 0), (0, FCOLS - 8)))
    df, dbf = _fgate_bwd(dF, sv['proj'], sp['bf'], f"fgate_bwd_{t}")
    gs['b_forget'] = dbf[:, :8]
    da, dbd, gs['pool_scale'] = _pool_bwd(dya, sv['d'], sp['bd'], sp['pool_scale'], f"pool_bwd_{t}")
    gs['pool_w'] = jnp.stack([dbd[g * 64:(g + 1) * 64, g * 64:(g + 1) * 64] for g in range(4)])
    dproj = jnp.concatenate([dg, dq, dk, dv, dc, da, df], axis=1)
    dwt = _mm(dproj, sv['h'], name=f"dw_in_{t}", out_dtype=BF16, ta=True, tm=512, tn=1024)
    ga = _w_in_grad_rows(ga, dwt)
    dx, gs['norm_mix_g'] = _mm(dproj, sv['w_in_t'], name=f"d_h_{t}", out_dtype=F32, tm=512, tk=512, tn=D,
                               norm_bwd=(sv['x'], sp['g_mix'], dx1))
    return dx, ga, gs


SMALL_ROWS = 1424
GRAD_BLOCKS = {
    'w_ff1': ('b', lambda i: (R_FF1 // 256 + i, 0)), 'w_ff2': ('b', lambda i: (R_FF2 // 256 + i, 0)),
    'w_xq': ('b', lambda i: (R_XQ // 256 + i, 0)), 'w_xo': ('b', lambda i: (R_XO // 256 + i, 0)),
    'w_xkv': ('b', lambda i: (R_XKV // 256 + i % 2, i // 2)), 'w_out': ('a', lambda i: (R_OUT // 256 + i, 0)),
    'w_branch_a': ('a', lambda i: (R_BRANCH // 256, 0)), 'w_branch_b': ('a', lambda i: (R_BRANCH // 256, 1 + i)),
    'w_branch_c': ('a', lambda i: (R_BRANCH // 256, 3)),
}


def _pack_small(parts):
    flat = jnp.concatenate([p.reshape(-1) for p in parts])
    return jnp.pad(flat, (0, SMALL_ROWS * 128 - flat.shape[0])).reshape(SMALL_ROWS, 128)


def _unpack_small(buf, shapes):
    flat, out, r = buf.reshape(-1), [], 0
    for s in shapes:
        n = math.prod(s)
        out.append(flat[r:r + n].reshape(s))
        r += n
    return out


def kernel(x, mem, norm_mix_g, w_in, b_forget, pool_w, pool_scale, sgu_norm_g, sgu_w, sgu_b, w_branch_a, w_branch_b, w_branch_c, b_gate, w_out, norm_xattn_g, norm_mem_g, w_xq, w_xkv, w_xo, norm_ffn_g, w_ff1, w_ff2, final_norm_g, loss_target, m_norm_mix_g, m_w_in, m_b_forget, m_pool_w, m_pool_scale, m_sgu_norm_g, m_sgu_w, m_sgu_b, m_w_branch_a, m_w_branch_b, m_w_branch_c, m_b_gate, m_w_out, m_norm_xattn_g, m_norm_mem_g, m_w_xq, m_w_xkv, m_w_xo, m_norm_ffn_g, m_w_ff1, m_w_ff2, m_final_norm_g, v_norm_mix_g, v_w_in, v_b_forget, v_pool_w, v_pool_scale, v_sgu_norm_g, v_sgu_w, v_sgu_b, v_w_branch_a, v_w_branch_b, v_w_branch_c, v_b_gate, v_w_out, v_norm_xattn_g, v_norm_mem_g, v_w_xq, v_w_xkv, v_w_xo, v_norm_ffn_g, v_w_ff1, v_w_ff2, v_final_norm_g):
    args = (norm_mix_g, w_in, b_forget, pool_w, pool_scale, sgu_norm_g, sgu_w, sgu_b, w_branch_a, w_branch_b, w_branch_c, b_gate,
            w_out, norm_xattn_g, norm_mem_g, w_xq, w_xkv, w_xo, norm_ffn_g, w_ff1, w_ff2, final_norm_g)
    margs = (m_norm_mix_g, m_w_in, m_b_forget, m_pool_w, m_pool_scale, m_sgu_norm_g, m_sgu_w, m_sgu_b, m_w_branch_a, m_w_branch_b,
             m_w_branch_c, m_b_gate, m_w_out, m_norm_xattn_g, m_norm_mem_g, m_w_xq, m_w_xkv, m_w_xo, m_norm_ffn_g, m_w_ff1, m_w_ff2,
             m_final_norm_g)
    vargs = (v_norm_mix_g, v_w_in, v_b_forget, v_pool_w, v_pool_scale, v_sgu_norm_g, v_sgu_w, v_sgu_b, v_w_branch_a, v_w_branch_b,
             v_w_branch_c, v_b_gate, v_w_out, v_norm_xattn_g, v_norm_mem_g, v_w_xq, v_w_xkv, v_w_xo, v_norm_ffn_g, v_w_ff1, v_w_ff2,
             v_final_norm_g)
    w = dict(zip(W_NAMES, args))
    mo = dict(zip(W_NAMES, margs))
    vo = dict(zip(W_NAMES, vargs))
    xs, mems, tgt = x[0], mem[0], loss_target[0]
    shards = [_pack_shard(w, l) for l in range(DEPTH)]
    preps = [_small_prep(w, l) for l in range(DEPTH)]

    first_a, _ = _gather_begin(shards[0][0], "a_l0")
    pending_b, token = _gather_begin(shards[0][1], "b_l0")
    GA = None
    act, saved = xs, []
    for l in range(DEPTH):
        nxt = {}
        if l + 1 < DEPTH:
            nxt['a'], ta = _gather_begin(shards[l + 1][0], f"a_l{l + 1}")
            token = ta if token is None else token + ta
        if l == 0:
            GA = _gather_end(first_a, shards[DEPTH - 1][1])

        def second(x1, l=l, pending_b=pending_b, nxt=nxt):
            GB = _gather_end(pending_b, x1)
            if l + 1 == DEPTH:
                return GB, None
            nxt['b'], tb = _gather_begin(shards[l + 1][1], f"b_l{l + 1}")
            return GB, tb

        act, sv = _layer_fwd(act, mems, GA, _w_in_rows(GA), preps[l], l, token, second)
        saved.append(sv)
        if l + 1 < DEPTH:
            GA = _gather_end(nxt['a'], act)
            pending_b, token = nxt['b'], None
    loss_part, dact, d_final_g = _loss_head(act, w['final_norm_g'][None], tgt, "loss_head")

    red_a, red_b, small_g = [None] * DEPTH, [None] * DEPTH, [None] * DEPTH
    token, state_a = None, None
    for l in reversed(range(DEPTH)):
        early = {}

        def start_b(gb, l=l, early=early):
            early['state'], tok = _reduce_begin(gb, f"b_l{l}")
            return tok

        dact, ga, small_g[l] = _layer_bwd(dact, mems, preps[l], saved[l], l, token, start_b)
        if state_a is not None:
            red_a[l + 1] = _reduce_end(state_a, dact)
        red_b[l] = _reduce_end(early['state'], dact)
        state_a, token = _reduce_begin(ga, f"a_l{l}")
    grad_x = dact[None]
    per_layer = [n for n in SMALL_NAMES if n != 'final_norm_g']
    small_shapes = [w[n].shape for n in per_layer] + [(D,), (1,)]
    parts = [jnp.stack([small_g[l][n].reshape(w[n].shape[1:]) for l in range(DEPTH)]) for n in per_layer]
    state_small, token_small = _all_reduce_begin(_pack_small(parts + [d_final_g.reshape(D), loss_part.reshape(1)]), "small")
    token = token + token_small

    grads, delta, new_m, new_v = {}, {}, {}, {}
    for n, (buf, g_index) in GRAD_BLOCKS.items():
        if buf == 'b':
            grads[n], delta[n], new_m[n], new_v[n] = _adamw_packed(red_b, w[n], mo[n], vo[n], g_index, f"adamw_{n}", token)
    red_a[0] = _reduce_end(state_a, new_v['w_xkv'])
    small_red = _unpack_small(_all_reduce_end(state_small, red_a[0]), small_shapes)
    grads.update(zip(per_layer + ['final_norm_g'], small_red[:-1]))
    loss = small_red[-1].reshape(())
    for n, (buf, g_index) in GRAD_BLOCKS.items():
        if buf == 'a':
            grads[n], delta[n], new_m[n], new_v[n] = _adamw_packed(red_a, w[n], mo[n], vo[n], g_index, f"adamw_{n}", token)
    g_t = jnp.stack([r[R_WIN:R_WIN + WIN_ROWS] for r in red_a], axis=1)
    upd = _adamw(g_t, _w_in_t(w['w_in']), _w_in_t(mo['w_in']), _w_in_t(vo['w_in']), "adamw_w_in", block=(WIN_ROWS, DEPTH, 128))
    grads['w_in'], delta['w_in'], new_m['w_in'], new_v['w_in'] = [jnp.transpose(a, (1, 2, 0)) for a in (g_t,) + tuple(upd)]
    small_all = per_layer + ['final_norm_g']
    shapes_all = [w[n].shape for n in small_all]
    packed = [_pack_small([d[n] for n in small_all])[None] for d in (grads, w, mo, vo)]
    ds, ms, vs = _adamw(*packed, "adamw_small")
    for n, a, b, c in zip(small_all, _unpack_small(ds[0], shapes_all), _unpack_small(ms[0], shapes_all), _unpack_small(vs[0], shapes_all)):
        delta[n], new_m[n], new_v[n] = a, b, c

    return (loss, grad_x, *[grads[n] for n in W_NAMES], *[delta[n] for n in W_NAMES], *[new_m[n] for n in W_NAMES],
            *[new_v[n] for n in W_NAMES])
```

```python
import math

import jax
import jax.numpy as jnp
from jax import lax
from jax.experimental import pallas as pl
from jax.experimental.pallas import tpu as pltpu

F32 = jnp.float32
BF16 = jnp.bfloat16

D = 1024
DEPTH = 2
POOL_W = 256
FOX_W = 512
SGU_W = 256
SGU_CHUNK = 128
N_IN = 5384
P_G, P_Q, P_K, P_V, P_C, P_A, P_F = 0, 3072, 3584, 4096, 4608, 5120, 5376
NP = 5632
XH, XHD = 4, 256
D_FF = 4096
EPS = 1e-6
NEG = -1e30
FOX_SCALE = 64 ** -0.5
X_SCALE = 256 ** -0.5
GELU_K = math.sqrt(2.0 / math.pi)
GELU_C = 0.044715

ADAM_LR, ADAM_B1, ADAM_B2, ADAM_EPS, ADAM_WD, ADAM_STEP = 0.001, 0.9, 0.999, 1e-08, 0.01, 10

VMEM_LIMIT = 48 * 1024 * 1024
WIDE_VMEM_LIMIT = 60 * 1024 * 1024
MESH = pl.DeviceIdType.MESH

IN_NAMES = ['x', 'mem', 'norm_mix_g', 'w_in', 'b_forget', 'pool_w', 'pool_scale', 'sgu_norm_g', 'sgu_w', 'sgu_b',
            'w_branch_a', 'w_branch_b', 'w_branch_c', 'b_gate', 'w_out', 'norm_xattn_g', 'norm_mem_g', 'w_xq',
            'w_xkv', 'w_xo', 'norm_ffn_g', 'w_ff1', 'w_ff2', 'final_norm_g']
W_NAMES = IN_NAMES[2:]
BIG_NAMES = ['w_in', 'w_branch_a', 'w_branch_b', 'w_branch_c', 'w_out', 'w_xq', 'w_xkv', 'w_xo', 'w_ff1', 'w_ff2']
SMALL_NAMES = [n for n in W_NAMES if n not in BIG_NAMES]
PACK_COLS = 1024


ANY = pl.BlockSpec(memory_space=pl.ANY)


def _cp(sem=None, vmem_limit=VMEM_LIMIT):
    return pltpu.CompilerParams(dimension_semantics=sem, vmem_limit_bytes=vmem_limit)


def _mm(a, b, *, name, out_dtype, ta=False, tb=False, tm=1024, tn=512, tk=1024, a_fn=None, extra=None, epi=None,
        n=None, k=None, b_block=None, b_index=None, into=None, o_block=None, o_index=None, norm_bwd=None,
        vmem_limit=VMEM_LIMIT):
    M = a.shape[1] if ta else a.shape[0]
    K = k if k is not None else (a.shape[0] if ta else a.shape[1])
    N = n if n is not None else (b.shape[0] if tb else b.shape[1])
    tm, tn, tk = min(tm, M), min(tn, N), min(tk, K)
    assert M % tm == 0 and N % tn == 0 and K % tk == 0, (name, M, N, K)
    nk = K // tk
    a_spec = pl.BlockSpec((tk, tm), lambda i, j, k: (k, i)) if ta else pl.BlockSpec((tm, tk), lambda i, j, k: (i, k))
    if b_block is not None:
        b_spec = pl.BlockSpec(b_block, b_index)
    else:
        b_spec = pl.BlockSpec((tn, tk), lambda i, j, k: (j, k)) if tb else pl.BlockSpec((tk, tn), lambda i, j, k: (k, j))
    dn = (((0 if ta else 1,), (1 if tb else 0,)), ((), ()))
    tile = pl.BlockSpec((tm, tn), lambda i, j, k: (i, j))
    o_spec = pl.BlockSpec(o_block, o_index) if into is not None else tile
    in_specs = [a_spec, b_spec] + ([tile] if extra is not None else []) + ([ANY] if into is not None else [])
    vec = pl.BlockSpec((1, N), lambda i, j, k: (0, 0))
    if norm_bwd is not None:
        assert tn == N and extra is None and into is None, name
        in_specs += [tile, tile, vec]
    n_in = len(in_specs)

    def body(*refs):
        a_ref, b_ref = refs[0], refs[1]
        e_ref = refs[2] if extra is not None else None
        o_ref, acc_ref = refs[n_in], refs[-1]
        kk = pl.program_id(2)
        first_rows = pl.program_id(0) == 0

        @pl.when(kk == 0)
        def _():
            acc_ref[...] = jnp.zeros_like(acc_ref)

        av = a_ref[...]
        if a_fn is not None:
            av = a_fn(av)
        bv = b_ref[...]
        if bv.ndim == 3:
            bv = bv.reshape(-1, bv.shape[-1])
        acc_ref[...] += lax.dot_general(av.astype(BF16), bv.astype(BF16), dn, preferred_element_type=F32)

        @pl.when(kk == nk - 1)
        def _():
            r = acc_ref[...]
            if norm_bwd is not None:
                x_ref, r_ref, g_ref, dg_ref = refs[2], refs[3], refs[4], refs[n_in + 1]
                xv = x_ref[...]
                rstd = lax.rsqrt(jnp.mean(xv * xv, axis=-1, keepdims=True) + EPS)
                xhat = xv * rstd

                @pl.when(first_rows)
                def _():
                    dg_ref[...] = jnp.zeros_like(dg_ref)

                dg_ref[...] += jnp.sum(r * xhat, axis=0, keepdims=True)
                t = r * g_ref[...]
                o_ref[...] = r_ref[...] + rstd * (t - xhat * jnp.mean(t * xhat, axis=-1, keepdims=True))
                return
            if epi is not None:
                r = epi(r, e_ref[...])
            o_ref[...] = r.astype(o_ref.dtype).reshape(o_ref.shape)

    args = (a, b) + ((extra,) if extra is not None else ()) + ((into,) if into is not None else ())
    out_shape = jax.ShapeDtypeStruct(into.shape, into.dtype) if into is not None else jax.ShapeDtypeStruct((M, N), out_dtype)
    semantics = ("parallel", "parallel", "arbitrary")
    if norm_bwd is not None:
        xn, gn, dres = norm_bwd
        args += (xn, dres, gn)
        out_shape = (jax.ShapeDtypeStruct((M, N), F32), jax.ShapeDtypeStruct((1, N), F32))
        o_spec = (tile, vec)
        semantics = ("arbitrary", "arbitrary", "arbitrary")
    return pl.pallas_call(
        body, out_shape=out_shape, grid=(M // tm, N // tn, nk), in_specs=in_specs, out_specs=o_spec,
        scratch_shapes=[pltpu.VMEM((tm, tn), F32)], input_output_aliases={n_in - 1: 0} if into is not None else {},
        compiler_params=_cp(semantics, vmem_limit), name=name)(*args)


def _relu2(z):
    r = jnp.maximum(z.astype(F32), 0.0)
    return r * r


def _rms_fwd(x, g, name, tr=512):
    R, n = x.shape
    tr = min(tr, R)

    def body(x_ref, g_ref, h_ref):
        xv = x_ref[...]
        rstd = lax.rsqrt(jnp.mean(xv * xv, axis=-1, keepdims=True) + EPS)
        h_ref[...] = (xv * rstd * g_ref[...]).astype(BF16)

    return pl.pallas_call(
        body, out_shape=jax.ShapeDtypeStruct((R, n), BF16), grid=(R // tr,),
        in_specs=[pl.BlockSpec((tr, n), lambda i: (i, 0)), pl.BlockSpec((1, n), lambda i: (0, 0))],
        out_specs=pl.BlockSpec((tr, n), lambda i: (i, 0)), compiler_params=_cp(("parallel",)), name=name)(x, g)


def _rms_bwd(dh, x, g, dres, name, tr=512):
    R, n = x.shape
    tr = min(tr, R)
    need_dx = dres is not None

    def body(*refs):
        if need_dx:
            dh_ref, x_ref, g_ref, r_ref, dx_ref, dg_ref = refs
        else:
            dh_ref, x_ref, g_ref, dg_ref = refs
        i = pl.program_id(0)
        xv = x_ref[...]
        dhv = dh_ref[...].astype(F32)
        rstd = lax.rsqrt(jnp.mean(xv * xv, axis=-1, keepdims=True) + EPS)
        xhat = xv * rstd

        @pl.when(i == 0)
        def _():
            dg_ref[...] = jnp.zeros_like(dg_ref)

        dg_ref[...] += jnp.sum(dhv * xhat, axis=0, keepdims=True)
        if need_dx:
            t = dhv * g_ref[...]
            dx_ref[...] = r_ref[...] + rstd * (t - xhat * jnp.mean(t * xhat, axis=-1, keepdims=True))

    row = pl.BlockSpec((tr, n), lambda i: (i, 0))
    vec = pl.BlockSpec((1, n), lambda i: (0, 0))
    if need_dx:
        return pl.pallas_call(
            body, out_shape=(jax.ShapeDtypeStruct((R, n), F32), jax.ShapeDtypeStruct((1, n), F32)), grid=(R // tr,),
            in_specs=[row, row, vec, row], out_specs=(row, vec), compiler_params=_cp(("arbitrary",)), name=name)(dh, x, g, dres)
    return pl.pallas_call(
        body, out_shape=jax.ShapeDtypeStruct((1, n), F32), grid=(R // tr,),
        in_specs=[row, row, vec], out_specs=vec, compiler_params=_cp(("arbitrary",)), name=name)(dh, x, g)


def _loss_head(x, g, tgt, name, tr=512):
    R, n = x.shape

    def body(x_ref, g_ref, t_ref, loss_ref, dx_ref, dg_ref):
        i = pl.program_id(0)
        xv = x_ref[...]
        gv = g_ref[...]
        rstd = lax.rsqrt(jnp.mean(xv * xv, axis=-1, keepdims=True) + EPS)
        xhat = xv * rstd
        e = xhat * gv - t_ref[...]

        @pl.when(i == 0)
        def _():
            loss_ref[...] = jnp.zeros_like(loss_ref)
            dg_ref[...] = jnp.zeros_like(dg_ref)

        loss_ref[...] += 0.5 * jnp.sum(jnp.sum(e * e, axis=-1, keepdims=True) / n, axis=0, keepdims=True)
        dy = e / n
        dg_ref[...] += jnp.sum(dy * xhat, axis=0, keepdims=True)
        t = dy * gv
        dx_ref[...] = rstd * (t - xhat * jnp.mean(t * xhat, axis=-1, keepdims=True))

    row = pl.BlockSpec((tr, n), lambda i: (i, 0))
    vec = pl.BlockSpec((1, n), lambda i: (0, 0))
    one = pl.BlockSpec((1, 1), lambda i: (0, 0))
    return pl.pallas_call(
        body, out_shape=(jax.ShapeDtypeStruct((1, 1), F32), jax.ShapeDtypeStruct((R, n), F32), jax.ShapeDtypeStruct((1, n), F32)),
        grid=(R // tr,), in_specs=[row, vec, row], out_specs=(one, row, vec),
        compiler_params=_cp(("arbitrary",)), name=name)(x, g, tgt)


def _pool_masks(S):
    row = lax.broadcasted_iota(jnp.int32, (S, POOL_W), 0)
    grp = lax.broadcasted_iota(jnp.int32, (S, POOL_W), 1) // 64
    win = jnp.where(grp == 0, 2, jnp.where(grp == 1, 4, jnp.where(grp == 2, 8, 16)))
    cnt = jnp.minimum(row + 1, win).astype(F32)
    return row, grp, cnt


def _by_group(grp, v0, v1, v2, v3):
    return jnp.where(grp == 0, v0, jnp.where(grp == 1, v1, jnp.where(grp == 2, v2, v3)))


def _pool_fwd(proj, bd, scale, name):
    S = proj.shape[0]

    def body(a_ref, bd_ref, sc_ref, d_ref, y_ref):
        a = a_ref[...]
        row, grp, cnt = _pool_masks(S)

        def back(v, k):
            return jnp.where(row >= k, pltpu.roll(v, k, 0), 0.0)

        s1 = a + back(a, 1)
        s2 = s1 + back(s1, 2)
        s3 = s2 + back(s2, 4)
        s4 = s3 + back(s3, 8)
        d = (_by_group(grp, s1, s2, s3, s4) / cnt - a).astype(BF16)
        d_ref[...] = d
        y_ref[...] = (jnp.dot(d, bd_ref[...], preferred_element_type=F32) * sc_ref[...]).astype(BF16)

    full = lambda r, c: pl.BlockSpec((r, c), lambda i: (0, 0))
    return pl.pallas_call(
        body, out_shape=(jax.ShapeDtypeStruct((S, POOL_W), BF16), jax.ShapeDtypeStruct((S, POOL_W), BF16)), grid=(1,),
        in_specs=[pl.BlockSpec((S, POOL_W), lambda i: (0, P_A // POOL_W)), full(POOL_W, POOL_W), full(1, POOL_W)],
        out_specs=(full(S, POOL_W), full(S, POOL_W)), compiler_params=_cp(("arbitrary",)), name=name)(proj, bd, scale)


def _pool_bwd(dya, d, bd, scale, name):
    S = dya.shape[0]

    def body(dy_ref, d_ref, bd_ref, sc_ref, da_ref, dbd_ref, dsc_ref):
        dy = dy_ref[...]
        dv = d_ref[...]
        bdv = bd_ref[...]
        row, grp, cnt = _pool_masks(S)
        yraw = jnp.dot(dv, bdv, preferred_element_type=F32)
        dsc_ref[...] = jnp.sum(dy * yraw, axis=0, keepdims=True)
        tb = (dy * sc_ref[...]).astype(BF16)
        dbd_ref[...] = lax.dot_general(dv, tb, (((0,), (0,)), ((), ())), preferred_element_type=F32)
        dd = lax.dot_general(tb, bdv, (((1,), (1,)), ((), ())), preferred_element_type=F32)
        e = dd / cnt

        def fwd(v, k):
            return jnp.where(row < S - k, pltpu.roll(v, S - k, 0), 0.0)

        r1 = e + fwd(e, 1)
        r2 = r1 + fwd(r1, 2)
        r3 = r2 + fwd(r2, 4)
        r4 = r3 + fwd(r3, 8)
        da_ref[...] = (_by_group(grp, r1, r2, r3, r4) - dd).astype(BF16)

    full = lambda r, c: pl.BlockSpec((r, c), lambda i: (0, 0))
    return pl.pallas_call(
        body, out_shape=(jax.ShapeDtypeStruct((S, POOL_W), BF16), jax.ShapeDtypeStruct((POOL_W, POOL_W), F32),
                         jax.ShapeDtypeStruct((1, POOL_W), F32)), grid=(1,),
        in_specs=[full(S, POOL_W), full(S, POOL_W), full(POOL_W, POOL_W), full(1, POOL_W)],
        out_specs=(full(S, POOL_W), full(POOL_W, POOL_W), full(1, POOL_W)),
        compiler_params=_cp(("arbitrary",)), name=name)(dya, d, bd, scale)


FCOLS = 128


def _log_sigmoid(z):
    return -(jnp.maximum(-z, 0.0) + jnp.log1p(jnp.exp(-jnp.abs(z))))


def _fgate_fwd(proj, bf, name):
    S = proj.shape[0]

    def body(f_ref, b_ref, o_ref):
        v = _log_sigmoid(f_ref[...] + b_ref[...])
        row = lax.broadcasted_iota(jnp.int32, (S, FCOLS), 0)
        k = 1
        while k < S:
            v = v + jnp.where(row >= k, pltpu.roll(v, k, 0), 0.0)
            k *= 2
        o_ref[...] = v

    return pl.pallas_call(
        body, out_shape=jax.ShapeDtypeStruct((S, FCOLS), F32), grid=(1,),
        in_specs=[pl.BlockSpec((S, FCOLS), lambda i: (0, P_F // FCOLS)), pl.BlockSpec((1, FCOLS), lambda i: (0, 0))],
        out_specs=pl.BlockSpec((S, FCOLS), lambda i: (0, 0)), compiler_params=_cp(("arbitrary",)), name=name)(proj, bf)


def _fgate_bwd(dF, proj, bf, name):
    S = proj.shape[0]

    def body(dF_ref, f_ref, b_ref, df_ref, db_ref):
        v = dF_ref[...]
        row = lax.broadcasted_iota(jnp.int32, (S, FCOLS), 0)
        k = 1
        while k < S:
            v = v + jnp.where(row < S - k, pltpu.roll(v, S - k, 0), 0.0)
            k *= 2
        z = f_ref[...] + b_ref[...]
        df = v * (1.0 / (1.0 + jnp.exp(z)))
        db_ref[...] = jnp.sum(df, axis=0, keepdims=True)
        df_ref[...] = jnp.concatenate([df, jnp.zeros_like(df)], axis=1).astype(BF16)

    return pl.pallas_call(
        body, out_shape=(jax.ShapeDtypeStruct((S, 2 * FCOLS), BF16), jax.ShapeDtypeStruct((1, FCOLS), F32)), grid=(1,),
        in_specs=[pl.BlockSpec((S, FCOLS), lambda i: (0, 0)), pl.BlockSpec((S, FCOLS), lambda i: (0, P_F // FCOLS)),
                  pl.BlockSpec((1, FCOLS), lambda i: (0, 0))],
        out_specs=(pl.BlockSpec((S, 2 * FCOLS), lambda i: (0, 0)), pl.BlockSpec((1, FCOLS), lambda i: (0, 0))),
        compiler_params=_cp(("arbitrary",)), name=name)(dF, proj, bf)


def _fox_scores(qe, kj, fq, fk, r0, c0, tq, tk, diagonal):
    s = lax.dot_general(qe, kj, (((1,), (1,)), ((), ())), preferred_element_type=F32) * FOX_SCALE
    s = s + (fq - fk)
    if not diagonal:
        return s
    rows = r0 + lax.broadcasted_iota(jnp.int32, (tq, tk), 0)
    cols = c0 + lax.broadcasted_iota(jnp.int32, (tq, tk), 1)
    return jnp.where(rows >= cols, s, NEG)


FOX_TQ, FOX_TK = 512, 512


def _fox_fwd(qkv, fcol, frow, name):
    S = qkv.shape[0]
    tq, tk = FOX_TQ, min(FOX_TK, S)

    def body(q_ref, k_ref, v_ref, fc_ref, fr_ref, o_ref, o32_ref, lse_ref):
        i = pl.program_id(1)
        r0 = i * tq
        q = q_ref[...]
        half = lax.broadcasted_iota(jnp.int32, (tq, 128), 1) // 64
        qs = [jnp.where(half == e, q, jnp.zeros_like(q)) for e in (0, 1)]
        fqs = [fc_ref[0, :, e:e + 1] for e in (0, 1)]

        def step(j, carry, diagonal=False):
            c0 = pl.multiple_of(j * tk, tk)
            kj = k_ref[pl.ds(c0, tk), :]
            vj = v_ref[pl.ds(c0, tk), :]
            out = []
            for e in (0, 1):
                m, l, acc = carry[e]
                s = _fox_scores(qs[e], kj, fqs[e], fr_ref[0, e:e + 1, pl.ds(c0, tk)], r0, c0, tq, tk, diagonal)
                m_new = jnp.maximum(m, jnp.max(s, axis=-1, keepdims=True))
                alpha = jnp.exp(m - m_new)
                p = jnp.exp(s - m_new)
                out.append((m_new, alpha * l + jnp.sum(p, axis=-1, keepdims=True),
                            alpha * acc + jnp.dot(p.astype(BF16), vj, preferred_element_type=F32)))
            return tuple(out)

        init = (jnp.full((tq, 1), NEG, F32), jnp.zeros((tq, 1), F32), jnp.zeros((tq, 128), F32))
        below = r0 // tk
        carry = lax.fori_loop(0, below, step, (init, init))
        carry = step(below, carry, diagonal=True)
        outs = []
        for e in (0, 1):
            m, l, acc = carry[e]
            outs.append(acc / l)
            lse_ref[0, :, e:e + 1] = m + jnp.log(l)
        o = jnp.where(half == 0, outs[0], outs[1])
        o32_ref[...] = o
        o_ref[...] = o.astype(BF16)

    tile = pl.BlockSpec((tq, 128), lambda h, i: (i, h))
    return pl.pallas_call(
        body, out_shape=(jax.ShapeDtypeStruct((S, FOX_W), BF16), jax.ShapeDtypeStruct((S, FOX_W), F32),
                         jax.ShapeDtypeStruct((4, S, 2), F32)), grid=(4, S // tq),
        in_specs=[tile, pl.BlockSpec((S, 128), lambda h, i: (0, 4 + h)), pl.BlockSpec((S, 128), lambda h, i: (0, 8 + h)),
                  pl.BlockSpec((1, tq, 2), lambda h, i: (h, i, 0)), pl.BlockSpec((1, 2, S), lambda h, i: (h, 0, 0))],
        out_specs=(tile, tile, pl.BlockSpec((1, tq, 2), lambda h, i: (h, i, 0))),
        compiler_params=_cp(("parallel", "parallel")), name=name)(qkv, qkv, qkv, fcol, frow)


def _fox_bwd(qkv, o32, do, lse, fcol, frow, name):
    S = qkv.shape[0]
    tq, tk = FOX_TQ, min(FOX_TK, S)
    nq = S // tq

    def body(q_ref, k_ref, v_ref, o_ref, do_ref, lse_ref, fc_ref, fr_ref, dq_ref, dk_ref, dv_ref, dfr_ref, dfc_ref, dk_acc, dv_acc):
        dk_acc[...] = jnp.zeros_like(dk_acc)
        dv_acc[...] = jnp.zeros_like(dv_acc)
        dfr_ref[...] = jnp.zeros_like(dfr_ref)
        half = lax.broadcasted_iota(jnp.int32, (tq, 128), 1) // 64

        def q_block(i, _):
            r0 = pl.multiple_of(i * tq, tq)
            qi = q_ref[pl.ds(r0, tq), :]
            dob = do_ref[pl.ds(r0, tq), :].astype(BF16)
            row_dot = dob.astype(F32) * o_ref[pl.ds(r0, tq), :]
            qs = [jnp.where(half == e, qi, jnp.zeros_like(qi)) for e in (0, 1)]
            dos = [jnp.where(half == e, dob, jnp.zeros_like(dob)) for e in (0, 1)]
            deltas = [jnp.sum(jnp.where(half == e, row_dot, 0.0), axis=-1, keepdims=True) for e in (0, 1)]
            lses = [lse_ref[0, pl.ds(r0, tq), e:e + 1] for e in (0, 1)]
            fqs = [fc_ref[0, pl.ds(r0, tq), e:e + 1] for e in (0, 1)]

            def step(j, carry, diagonal=False):
                dqs, row_sums = carry
                c0 = pl.multiple_of(j * tk, tk)
                kj = k_ref[pl.ds(c0, tk), :]
                vj = v_ref[pl.ds(c0, tk), :]
                new_dq, new_rows, dkc, dvc = [], [], [], []
                for e in (0, 1):
                    s = _fox_scores(qs[e], kj, fqs[e], fr_ref[0, e:e + 1, pl.ds(c0, tk)], r0, c0, tq, tk, diagonal)
                    p = jnp.exp(s - lses[e])
                    dp = lax.dot_general(dos[e], vj, (((1,), (1,)), ((), ())), preferred_element_type=F32)
                    ds = p * (dp - deltas[e])
                    dfr_ref[0, e:e + 1, pl.ds(c0, tk)] -= jnp.sum(ds, axis=0, keepdims=True)
                    new_rows.append(row_sums[e] + jnp.sum(ds, axis=-1, keepdims=True))
                    dsb = (ds * FOX_SCALE).astype(BF16)
                    dkc.append(lax.dot_general(dsb, qi, (((0,), (0,)), ((), ())), preferred_element_type=F32))
                    dvc.append(lax.dot_general(p.astype(BF16), dob, (((0,), (0,)), ((), ())), preferred_element_type=F32))
                    new_dq.append(dqs[e] + jnp.dot(dsb, kj, preferred_element_type=F32))
                half_k = lax.broadcasted_iota(jnp.int32, (tk, 128), 1) // 64
                dk_acc[pl.ds(c0, tk), :] += jnp.where(half_k == 0, dkc[0], dkc[1])
                dv_acc[pl.ds(c0, tk), :] += jnp.where(half_k == 0, dvc[0], dvc[1])
                return tuple(new_dq), tuple(new_rows)

            zero, zero_col = jnp.zeros((tq, 128), F32), jnp.zeros((tq, 1), F32)
            below = r0 // tk
            carry = lax.fori_loop(0, below, step, ((zero, zero), (zero_col, zero_col)))
            dqs, row_sums = step(below, carry, diagonal=True)
            for e in (0, 1):
                dfc_ref[0, pl.ds(r0, tq), e:e + 1] = row_sums[e]
            dq_ref[pl.ds(r0, tq), :] = jnp.where(half == 0, dqs[0], dqs[1]).astype(BF16)
            return 0

        lax.fori_loop(0, nq, q_block, 0)
        dk_ref[...] = dk_acc[...].astype(BF16)
        dv_ref[...] = dv_acc[...].astype(BF16)

    col = lambda off: pl.BlockSpec((S, 128), lambda h: (0, off + h))
    hs2 = pl.BlockSpec((1, S, 2), lambda h: (h, 0, 0))
    h2s = pl.BlockSpec((1, 2, S), lambda h: (h, 0, 0))
    return pl.pallas_call(
        body, out_shape=(jax.ShapeDtypeStruct((S, FOX_W), BF16),) * 3 + (jax.ShapeDtypeStruct((4, 2, S), F32),
                                                                         jax.ShapeDtypeStruct((4, S, 2), F32)), grid=(4,),
        in_specs=[col(0), col(4), col(8), col(0), col(0), hs2, hs2, h2s],
        out_specs=(col(0), col(0), col(0), h2s, hs2),
        scratch_shapes=[pltpu.VMEM((S, 128), F32), pltpu.VMEM((S, 128), F32)],
        compiler_params=_cp(("parallel",)), name=name)(qkv, qkv, qkv, o32, do, lse, fcol, frow)


def _gelu(x):
    return 0.5 * x * (1.0 + jnp.tanh(GELU_K * (x + GELU_C * x * x * x)))


def _gelu_grad(x):
    th = jnp.tanh(GELU_K * (x + GELU_C * x * x * x))
    return 0.5 * (1.0 + th) + 0.5 * x * (1.0 - th * th) * GELU_K * (1.0 + 3.0 * GELU_C * x * x)


def _sgu_parts(c, gn, w_ref, bias):
    zc = _gelu(c)
    u, vv = zc[:, :SGU_W], zc[:, SGU_W:]
    rstd = lax.rsqrt(jnp.mean(vv * vv, axis=-1, keepdims=True) + EPS)
    vhat = vv * rstd
    vnb = (vhat * gn).astype(BF16)
    grp = lax.broadcasted_iota(jnp.int32, (SGU_CHUNK, SGU_W), 1) // 64
    mixed = bias
    for gi in range(4):
        mixed = mixed + jnp.where(grp == gi, jnp.dot(w_ref[gi], vnb, preferred_element_type=F32), 0.0)
    return u, rstd, vhat, vnb, grp, mixed


def _sgu_fwd(proj, gn, wm, bias, name):
    S = proj.shape[0]

    def body(c_ref, g_ref, w_ref, b_ref, o_ref):
        u, _, _, _, _, mixed = _sgu_parts(c_ref[...], g_ref[...], w_ref, b_ref[...])
        o_ref[...] = (u * mixed).astype(BF16)

    return pl.pallas_call(
        body, out_shape=jax.ShapeDtypeStruct((S, SGU_W), BF16), grid=(S // SGU_CHUNK,),
        in_specs=[pl.BlockSpec((SGU_CHUNK, 2 * SGU_W), lambda i: (i, P_C // (2 * SGU_W))),
                  pl.BlockSpec((1, SGU_W), lambda i: (0, 0)), pl.BlockSpec((4, SGU_CHUNK, SGU_CHUNK), lambda i: (0, 0, 0)),
                  pl.BlockSpec((SGU_CHUNK, SGU_W), lambda i: (0, 0))],
        out_specs=pl.BlockSpec((SGU_CHUNK, SGU_W), lambda i: (i, 0)),
        compiler_params=_cp(("parallel",)), name=name)(proj, gn, wm, bias)


def _sgu_bwd(dsg, proj, gn, wm, wmt, bias, name):
    S = proj.shape[0]

    def body(dsg_ref, c_ref, g_ref, w_ref, wt_ref, b_ref, dc_ref, dw_ref, db_ref, dg_ref):
        i = pl.program_id(0)

        @pl.when(i == 0)
        def _():
            dw_ref[...] = jnp.zeros_like(dw_ref)
            db_ref[...] = jnp.zeros_like(db_ref)
            dg_ref[...] = jnp.zeros_like(dg_ref)

        c = c_ref[...]
        gn_v = g_ref[...]
        u, rstd, vhat, vnb, grp, mixed = _sgu_parts(c, gn_v, w_ref, b_ref[...])
        dsg_v = dsg_ref[...]
        du = dsg_v * mixed
        dmix = dsg_v * u
        db_ref[...] += dmix
        dmb = dmix.astype(BF16)
        dvn = jnp.zeros((SGU_CHUNK, SGU_W), F32)
        for gi in range(4):
            dmg = jnp.where(grp == gi, dmb, jnp.zeros_like(dmb))
            dw_ref[gi] += lax.dot_general(dmg, vnb, (((1,), (1,)), ((), ())), preferred_element_type=F32)
            dvn = dvn + jnp.where(grp == gi, jnp.dot(wt_ref[gi], dmb, preferred_element_type=F32), 0.0)
        dg_ref[...] += jnp.sum(dvn * vhat, axis=0, keepdims=True)
        t = dvn * gn_v
        dvv = rstd * (t - vhat * jnp.mean(t * vhat, axis=-1, keepdims=True))
        dc_ref[...] = (jnp.concatenate([du, dvv], axis=1) * _gelu_grad(c)).astype(BF16)

    w_spec = pl.BlockSpec((4, SGU_CHUNK, SGU_CHUNK), lambda i: (0, 0, 0))
    tile = pl.BlockSpec((SGU_CHUNK, SGU_W), lambda i: (0, 0))
    vec = pl.BlockSpec((1, SGU_W), lambda i: (0, 0))
    return pl.pallas_call(
        body, out_shape=(jax.ShapeDtypeStruct((S, 2 * SGU_W), BF16), jax.ShapeDtypeStruct((4, SGU_CHUNK, SGU_CHUNK), F32),
                         jax.ShapeDtypeStruct((SGU_CHUNK, SGU_W), F32), jax.ShapeDtypeStruct((1, SGU_W), F32)),
        grid=(S // SGU_CHUNK,),
        in_specs=[pl.BlockSpec((SGU_CHUNK, SGU_W), lambda i: (i, 0)),
                  pl.BlockSpec((SGU_CHUNK, 2 * SGU_W), lambda i: (i, P_C // (2 * SGU_W))), vec, w_spec, w_spec, tile],
        out_specs=(pl.BlockSpec((SGU_CHUNK, 2 * SGU_W), lambda i: (i, 0)), w_spec, tile, vec),
        compiler_params=_cp(("arbitrary",)), name=name)(dsg, proj, gn, wm, wmt, bias)


def _sigmoid(z):
    return 1.0 / (1.0 + jnp.exp(-z))


def _merge_specs(tm):
    row = lambda n: pl.BlockSpec((tm, n), lambda i: (i, 0))
    gate = lambda b: pl.BlockSpec((tm, D), lambda i: (i, b))
    full = lambda r, c: pl.BlockSpec((r, c), lambda i: (0, 0))
    packed = pl.BlockSpec((4, 256, PACK_COLS), lambda i: (0, R_BRANCH // 256, 0))
    return row, gate, full, packed


def _branch_shards(c_ref, j):
    return c_ref[j, :, 0:256], c_ref[j, :, 256:512], c_ref[j, :, 512:768], c_ref[j, :, 768:1024]


def _merge_fwd(proj, ya, o, sg, packed_w, bg, name, tm=512):
    S = proj.shape[0]
    row, gate, full, packed = _merge_specs(tm)

    def body(g0, g1, g2, ya_ref, o_ref, sg_ref, c_ref, bg_ref, out_ref):
        yav, ov, sgv = ya_ref[...], o_ref[...], sg_ref[...]
        for j in range(4):
            cols = slice(256 * j, 256 * (j + 1))
            wa, wb0, wb1, wc = _branch_shards(c_ref, j)
            y = (jnp.dot(yav, wa, preferred_element_type=F32),
                 jnp.dot(ov[:, :256], wb0, preferred_element_type=F32) + jnp.dot(ov[:, 256:], wb1, preferred_element_type=F32),
                 jnp.dot(sgv, wc, preferred_element_type=F32))
            acc = jnp.zeros((tm, 256), F32)
            for b, g_ref in enumerate((g0, g1, g2)):
                acc = acc + _sigmoid(g_ref[:, cols] + bg_ref[:, b * D + 256 * j:b * D + 256 * (j + 1)]) * y[b]
            out_ref[:, cols] = acc.astype(BF16)

    return pl.pallas_call(
        body, out_shape=jax.ShapeDtypeStruct((S, D), BF16), grid=(S // tm,),
        in_specs=[gate(0), gate(1), gate(2), row(POOL_W), row(FOX_W), row(SGU_W), packed, full(1, 3 * D)],
        out_specs=row(D), compiler_params=_cp(("parallel",)), name=name)(proj, proj, proj, ya, o, sg, packed_w, bg)


def _merge_bwd(dm, proj, ya, o, sg, packed_w, bg, grads, name, tm=512):
    S = proj.shape[0]
    row, gate, full, packed = _merge_specs(tm)
    tn_dims = (((0,), (0,)), ((), ()))
    nt_dims = (((1,), (1,)), ((), ()))

    def body(dm_ref, g0, g1, g2, ya_ref, o_ref, sg_ref, c_ref, bg_ref, _, dg_ref, dya_ref, do_ref, dsg_ref, dc_ref, dbg_ref, acc):
        i = pl.program_id(0)

        @pl.when(i == 0)
        def _():
            acc[...] = jnp.zeros_like(acc)
            dbg_ref[...] = jnp.zeros_like(dbg_ref)

        yav, ov, sgv = ya_ref[...], o_ref[...], sg_ref[...]
        o0, o1 = ov[:, :256], ov[:, 256:]
        dya = jnp.zeros((tm, POOL_W), F32)
        do0 = jnp.zeros((tm, 256), F32)
        do1 = jnp.zeros((tm, 256), F32)
        dsg = jnp.zeros((tm, SGU_W), F32)
        for j in range(4):
            cols = slice(256 * j, 256 * (j + 1))
            wa, wb0, wb1, wc = _branch_shards(c_ref, j)
            y = (jnp.dot(yav, wa, preferred_element_type=F32),
                 jnp.dot(o0, wb0, preferred_element_type=F32) + jnp.dot(o1, wb1, preferred_element_type=F32),
                 jnp.dot(sgv, wc, preferred_element_type=F32))
            dmv = dm_ref[:, cols]
            dy = []
            for b, g_ref in enumerate((g0, g1, g2)):
                bcols = slice(b * D + 256 * j, b * D + 256 * (j + 1))
                gt = _sigmoid(g_ref[:, cols] + bg_ref[:, bcols])
                dgp = dmv * y[b] * gt * (1.0 - gt)
                dg_ref[:, bcols] = dgp.astype(BF16)
                dbg_ref[:, bcols] += jnp.sum(dgp, axis=0, keepdims=True)
                dy.append((dmv * gt).astype(BF16))
            dya = dya + lax.dot_general(dy[0], wa, nt_dims, preferred_element_type=F32)
            do0 = do0 + lax.dot_general(dy[1], wb0, nt_dims, preferred_element_type=F32)
            do1 = do1 + lax.dot_general(dy[1], wb1, nt_dims, preferred_element_type=F32)
            dsg = dsg + lax.dot_general(dy[2], wc, nt_dims, preferred_element_type=F32)
            acc[j, :, 0:256] += lax.dot_general(yav, dy[0], tn_dims, preferred_element_type=F32)
            acc[j, :, 256:512] += lax.dot_general(o0, dy[1], tn_dims, preferred_element_type=F32)
            acc[j, :, 512:768] += lax.dot_general(o1, dy[1], tn_dims, preferred_element_type=F32)
            acc[j, :, 768:1024] += lax.dot_general(sgv, dy[2], tn_dims, preferred_element_type=F32)
        dya_ref[...] = dya
        do_ref[:, :256] = do0
        do_ref[:, 256:] = do1
        dsg_ref[...] = dsg

        @pl.when(i == pl.num_programs(0) - 1)
        def _():
            dc_ref[...] = acc[...].astype(dc_ref.dtype)

    return pl.pallas_call(
        body, out_shape=(jax.ShapeDtypeStruct((S, 3 * D), BF16), jax.ShapeDtypeStruct((S, POOL_W), F32),
                         jax.ShapeDtypeStruct((S, FOX_W), F32), jax.ShapeDtypeStruct((S, SGU_W), F32),
                         jax.ShapeDtypeStruct(grads.shape, grads.dtype), jax.ShapeDtypeStruct((1, 3 * D), F32)),
        grid=(S // tm,),
        in_specs=[row(D), gate(0), gate(1), gate(2), row(POOL_W), row(FOX_W), row(SGU_W), packed, full(1, 3 * D), ANY],
        out_specs=(row(3 * D), row(POOL_W), row(FOX_W), row(SGU_W), packed, full(1, 3 * D)),
        scratch_shapes=[pltpu.VMEM((4, 256, PACK_COLS), F32)], input_output_aliases={9: 4},
        compiler_params=_cp(("arbitrary",)), name=name)(dm, proj, proj, proj, ya, o, sg, packed_w, bg, grads)


def _xattn_probs(qh, kh):
    s = lax.dot_general(qh, kh, (((1,), (1,)), ((), ())), preferred_element_type=F32) * X_SCALE
    p = jnp.exp(s - jnp.max(s, axis=-1, keepdims=True))
    return p / jnp.sum(p, axis=-1, keepdims=True)


def _xattn_fwd(xq, kv, name, tq=512):
    S = xq.shape[0]
    M = kv.shape[0]

    def body(q_ref, k_ref, v_ref, o_ref):
        for h in range(XH):
            sl = slice(h * XHD, (h + 1) * XHD)
            p = _xattn_probs(q_ref[:, sl], k_ref[:, sl])
            o_ref[:, sl] = jnp.dot(p.astype(BF16), v_ref[:, sl], preferred_element_type=F32).astype(BF16)

    return pl.pallas_call(
        body, out_shape=jax.ShapeDtypeStruct((S, D), BF16), grid=(S // tq,),
        in_specs=[pl.BlockSpec((tq, D), lambda i: (i, 0)), pl.BlockSpec((M, D), lambda i: (0, 0)),
                  pl.BlockSpec((M, D), lambda i: (0, 1))],
        out_specs=pl.BlockSpec((tq, D), lambda i: (i, 0)), compiler_params=_cp(("parallel",)), name=name)(xq, kv, kv)


def _xattn_bwd(xq, kv, do, name, tq=512):
    S = xq.shape[0]
    M = kv.shape[0]

    def body(q_ref, k_ref, v_ref, do_ref, dq_ref, dkv_ref, dk_acc, dv_acc):
        i = pl.program_id(0)

        @pl.when(i == 0)
        def _():
            dk_acc[...] = jnp.zeros_like(dk_acc)
            dv_acc[...] = jnp.zeros_like(dv_acc)

        for h in range(XH):
            sl = slice(h * XHD, (h + 1) * XHD)
            qh, kh, vh, doh = q_ref[:, sl], k_ref[:, sl], v_ref[:, sl], do_ref[:, sl]
            p = _xattn_probs(qh, kh)
            dp = lax.dot_general(doh, vh, (((1,), (1,)), ((), ())), preferred_element_type=F32)
            ds = p * (dp - jnp.sum(p * dp, axis=-1, keepdims=True))
            dsb = (ds * X_SCALE).astype(BF16)
            dq_ref[:, sl] = jnp.dot(dsb, kh, preferred_element_type=F32).astype(BF16)
            dk_acc[:, sl] += lax.dot_general(dsb, qh, (((0,), (0,)), ((), ())), preferred_element_type=F32)
            dv_acc[:, sl] += lax.dot_general(p.astype(BF16), doh, (((0,), (0,)), ((), ())), preferred_element_type=F32)

        @pl.when(i == pl.num_programs(0) - 1)
        def _():
            dkv_ref[:, :D] = dk_acc[...].astype(BF16)
            dkv_ref[:, D:] = dv_acc[...].astype(BF16)

    return pl.pallas_call(
        body, out_shape=(jax.ShapeDtypeStruct((S, D), BF16), jax.ShapeDtypeStruct((M, 2 * D), BF16)), grid=(S // tq,),
        in_specs=[pl.BlockSpec((tq, D), lambda i: (i, 0)), pl.BlockSpec((M, D), lambda i: (0, 0)),
                  pl.BlockSpec((M, D), lambda i: (0, 1)), pl.BlockSpec((tq, D), lambda i: (i, 0))],
        out_specs=(pl.BlockSpec((tq, D), lambda i: (i, 0)), pl.BlockSpec((M, 2 * D), lambda i: (0, 0))),
        scratch_shapes=[pltpu.VMEM((M, D), F32), pltpu.VMEM((M, D), F32)],
        compiler_params=_cp(("arbitrary",)), name=name)(xq, kv, kv, do)


def _adam_math(gv, wv, mv, vv):
    c1 = 1.0 - ADAM_B1 ** ADAM_STEP
    c2 = 1.0 - ADAM_B2 ** ADAM_STEP
    nm = ADAM_B1 * mv + (1.0 - ADAM_B1) * gv
    nv = ADAM_B2 * vv + (1.0 - ADAM_B2) * (gv * gv)
    return -ADAM_LR * ((nm / c1) / (jnp.sqrt(nv / c2) + ADAM_EPS) + ADAM_WD * wv), nm, nv


def _adamw(g, w, m, v, name, block=None):
    if block is None:
        block = (1, 256 if g.shape[1] % 256 == 0 else g.shape[1], g.shape[2])
    grid = tuple(s // b for s, b in zip(g.shape, block))

    def body(g_ref, w_ref, m_ref, v_ref, d_ref, nm_ref, nv_ref):
        d_ref[...], nm_ref[...], nv_ref[...] = _adam_math(g_ref[...], w_ref[...], m_ref[...], v_ref[...])

    blk = pl.BlockSpec(block, lambda a, b, c: (a, b, c))
    return pl.pallas_call(
        body, out_shape=(jax.ShapeDtypeStruct(g.shape, F32),) * 3, grid=grid,
        in_specs=[blk] * 4, out_specs=(blk,) * 3, compiler_params=_cp(("parallel",) * 3), name=name)(g, w, m, v)


def _adamw_packed(red, w, m, v, g_index, name, token, tr=256):
    L, r, c = w.shape
    tr = min(tr, r)

    def body(g0_ref, g1_ref, w_ref, m_ref, v_ref, _, g_ref, d_ref, nm_ref, nv_ref):
        gv = jnp.where(pl.program_id(0) == 0, g0_ref[...], g1_ref[...])
        g_ref[0] = gv
        d_ref[0], nm_ref[0], nv_ref[0] = _adam_math(gv, w_ref[0], m_ref[0], v_ref[0])

    gblk = pl.BlockSpec((tr, c), lambda l, i: g_index(i))
    blk = pl.BlockSpec((1, tr, c), lambda l, i: (l, i, 0))
    return pl.pallas_call(
        body, out_shape=(jax.ShapeDtypeStruct(w.shape, F32),) * 4, grid=(L, r // tr),
        in_specs=[gblk, gblk, blk, blk, blk, pl.BlockSpec((8, 128), lambda l, i: (0, 0))], out_specs=(blk,) * 4,
        compiler_params=_cp(("parallel", "parallel")), name=name)(red[0], red[1], w, m, v, token)


def _row_tile(R):
    return next((t for t in (512, 496, 384, 256) if R % t == 0), R)


def _sum_slots(a, out_dtype, name):
    n, R, C = a.shape
    tr = _row_tile(R)

    def body(a_ref, o_ref):
        acc = a_ref[0].astype(F32)
        for k in range(1, n):
            acc = acc + a_ref[k].astype(F32)
        o_ref[...] = acc.astype(out_dtype)

    return pl.pallas_call(
        body, out_shape=jax.ShapeDtypeStruct((R, C), out_dtype), grid=(R // tr,),
        in_specs=[pl.BlockSpec((n, tr, C), lambda i: (0, i, 0))], out_specs=pl.BlockSpec((tr, C), lambda i: (i, 0)),
        compiler_params=_cp(("parallel",)), name=name)(a)


LANDING = pl.BlockSpec(memory_space=pltpu.VMEM)


def _landing_params(shape, dtype):
    return pltpu.CompilerParams(vmem_limit_bytes=math.prod(shape) * jnp.dtype(dtype).itemsize + 4 * 1024 * 1024)


def _place():
    return lax.axis_index("x"), lax.axis_index("y"), lax.axis_index("c")


def _other_chips(x, y):
    return [(1 - x, y), (x, 1 - y), (1 - x, 1 - y)]


def _row_chunks(rows, want, align=16):
    n = want
    while n > 1 and rows % (n * align):
        n -= 1
    return n


def _pair_add(g, name, nch=5):
    n, R, C = g.shape
    half = R // 2
    nch = _row_chunks(half, nch)
    cr = half // nch
    rb = next(t for t in (512, 256, 128, 64, 32, 16) if half % t == 0)

    def body(g_ref, p_ref, got, send_sems, recv_sems, local_sem):
        x, y, c = _place()
        mine0 = pl.multiple_of(c * half, 16)
        theirs0 = (1 - c) * half
        keep = pltpu.make_async_copy(g_ref.at[:, pl.ds(mine0, half), :], p_ref, local_sem)
        keep.start()
        cps = []
        for s in range(n):
            for q in range(nch):
                src = g_ref.at[s, pl.ds(pl.multiple_of(theirs0 + q * cr, 16), cr), :]
                cps.append(pltpu.make_async_remote_copy(
                    src_ref=src, dst_ref=got.at[s, pl.ds(q * cr, cr), :], send_sem=send_sems.at[s * nch + q],
                    recv_sem=recv_sems.at[s * nch + q], device_id=(x, y, 1 - c), device_id_type=MESH))
        for cp in cps:
            cp.start()
        for cp in cps:
            cp.wait()
        keep.wait()

        def add(i, _):
            rows = pl.ds(pl.multiple_of(i * rb, rb), rb)
            for s in range(n):
                p_ref[s, rows, :] = (p_ref[s, rows, :].astype(F32) + got[s, rows, :].astype(F32)).astype(BF16)
            return 0

        lax.fori_loop(0, half // rb, add, 0)

    shape = (n, half, C)
    return pl.pallas_call(
        body, out_shape=jax.ShapeDtypeStruct(shape, g.dtype), in_specs=[ANY], out_specs=LANDING,
        scratch_shapes=[pltpu.VMEM(shape, g.dtype), pltpu.SemaphoreType.DMA((n * nch,)), pltpu.SemaphoreType.DMA((n * nch,)),
                        pltpu.SemaphoreType.DMA],
        compiler_params=_landing_params((2,) + shape, g.dtype), name=name)(g)


def _pair_gather(t, name, nch=10):
    R = t.shape[0]
    nch = _row_chunks(R, nch, 8)
    cr = R // nch

    def body(t_ref, o_ref, send_sems, recv_sems, local_sem):
        x, y, c = _place()
        own = pltpu.make_async_copy(t_ref, o_ref.at[c], local_sem)
        own.start()
        cps = [pltpu.make_async_remote_copy(src_ref=t_ref.at[pl.ds(q * cr, cr), :], dst_ref=o_ref.at[c, pl.ds(q * cr, cr), :],
                                            send_sem=send_sems.at[q], recv_sem=recv_sems.at[q], device_id=(x, y, 1 - c),
                                            device_id_type=MESH) for q in range(nch)]
        for cp in cps:
            cp.start()
        for cp in cps:
            cp.wait()
        own.wait()

    return pl.pallas_call(
        body, out_shape=jax.ShapeDtypeStruct((2,) + t.shape, t.dtype), in_specs=[ANY], out_specs=LANDING,
        scratch_shapes=[pltpu.SemaphoreType.DMA((nch,)), pltpu.SemaphoreType.DMA((nch,)), pltpu.SemaphoreType.DMA],
        compiler_params=_landing_params((2,) + t.shape, t.dtype), name=name)(t)


HBM = pl.BlockSpec(memory_space=pltpu.HBM)
SEM = pl.BlockSpec(memory_space=pltpu.SEMAPHORE)
SPLIT_COPY = pltpu.CompilerParams(has_side_effects=pltpu.SideEffectType.DATAFLOW_SIDE_EFFECTING)


def _split_exchange(src, rows, src_of, tag, nch=5):
    C = src.shape[-1]
    nch = _row_chunks(rows, nch)
    cr = rows // nch
    n = 3 * nch
    land_shape = (4, rows, C)

    def copies(src_ref, land_ref, send_sems, recv_sems):
        x, y, c = _place()
        j = 2 * x + y
        out = []
        for q in range(nch):
            for k, (px, py) in enumerate(_other_chips(x, y)):
                out.append(pltpu.make_async_remote_copy(
                    src_ref=src_of(src_ref, px, py, c, q * cr, cr), dst_ref=land_ref.at[j, pl.ds(q * cr, cr), :],
                    send_sem=send_sems.at[k * nch + q], recv_sem=recv_sems.at[k * nch + q], device_id=(px, py, c),
                    device_id_type=MESH))
        return out

    def start(src_ref, land_ref, send_sems, recv_sems, src_thru, land_thru, token):
        for cp in copies(src_ref, land_ref, send_sems, recv_sems):
            cp.start()
        token[...] = jnp.zeros_like(token)

    send_sems, recv_sems, src_thru, land_thru, token = pl.pallas_call(
        start, name=f"{tag}_start",
        out_shape=(pltpu.SemaphoreType.DMA((n,)), pltpu.SemaphoreType.DMA((n,)), pltpu.HBM(src.shape, src.dtype),
                   pltpu.HBM(land_shape, src.dtype), jax.ShapeDtypeStruct((8, 128), F32)),
        in_specs=(HBM, HBM), out_specs=(SEM, SEM, HBM, HBM, pl.BlockSpec(memory_space=pltpu.VMEM)),
        input_output_aliases={0: 2, 1: 3}, compiler_params=SPLIT_COPY)(
            pltpu.with_memory_space_constraint(src, pltpu.HBM),
            pltpu.with_memory_space_constraint(lax.empty(land_shape, src.dtype), pltpu.HBM))

    def finish(after):
        def wait(src_ref, land_ref, send_sems, recv_sems, after_ref, src_dead, got_ref):
            for cp in copies(src_ref, land_ref, send_sems, recv_sems):
                cp.wait_send()
                cp.wait_recv()

        return pl.pallas_call(
            wait, name=f"{tag}_wait", out_shape=(pltpu.HBM(src.shape, src.dtype), pltpu.HBM(land_shape, src.dtype)),
            in_specs=(HBM, HBM, SEM, SEM, ANY), out_specs=(HBM, HBM), input_output_aliases={0: 0, 1: 1},
            compiler_params=SPLIT_COPY)(src_thru, land_thru, send_sems, recv_sems, after)

    return token, finish


def _gather_finish(shard, land, name, nch=5):
    R, C = shard.shape
    half = R // 2
    nch = _row_chunks(half, nch)
    cr = half // nch

    def body(s_ref, l_ref, o_ref, send_sems, recv_sems, local_sems):
        x, y, c = _place()
        j = 2 * x + y
        mine0 = c * half
        local = [pltpu.make_async_copy(s_ref, o_ref.at[j], local_sems.at[0])]
        remote = []
        for k, (px, py) in enumerate(_other_chips(x, y)):
            jj = 2 * px + py
            local.append(pltpu.make_async_copy(l_ref.at[jj], o_ref.at[jj, pl.ds(pl.multiple_of(mine0, 16), half), :],
                                               local_sems.at[1 + k]))
            for q in range(nch):
                remote.append(pltpu.make_async_remote_copy(
                    src_ref=l_ref.at[jj, pl.ds(q * cr, cr), :],
                    dst_ref=o_ref.at[jj, pl.ds(pl.multiple_of(mine0 + q * cr, 16), cr), :], send_sem=send_sems.at[k * nch + q],
                    recv_sem=recv_sems.at[k * nch + q], device_id=(x, y, 1 - c), device_id_type=MESH))
        for cp in local + remote:
            cp.start()
        for cp in remote + local:
            cp.wait()

    return pl.pallas_call(
        body, out_shape=jax.ShapeDtypeStruct((4, R, C), shard.dtype), in_specs=[ANY, ANY], out_specs=LANDING,
        scratch_shapes=[pltpu.SemaphoreType.DMA((3 * nch,)), pltpu.SemaphoreType.DMA((3 * nch,)), pltpu.SemaphoreType.DMA((4,))],
        compiler_params=_landing_params((4, R, C), shard.dtype), name=name)(shard, land)


def _sum_slots_own(land, own, name):
    n, R, C = land.shape
    tr = _row_tile(R)
    me = (2 * lax.axis_index("x") + lax.axis_index("y")).astype(jnp.int32).reshape(1)
    if own.ndim == 3:
        own_spec = pl.BlockSpec((None, tr, C), lambda i, me: (me[0], i, 0))
    else:
        own_spec = pl.BlockSpec((tr, C), lambda i, me: (i, 0))

    def body(me_ref, land_ref, own_ref, o_ref):
        acc = None
        for k in range(n):
            v = jnp.where(me_ref[0] == k, own_ref[...], land_ref[k]).astype(F32)
            acc = v if acc is None else acc + v
        o_ref[...] = acc

    return pl.pallas_call(
        body, out_shape=jax.ShapeDtypeStruct((R, C), F32),
        grid_spec=pltpu.PrefetchScalarGridSpec(
            num_scalar_prefetch=1, grid=(R // tr,),
            in_specs=[pl.BlockSpec((n, tr, C), lambda i, me: (0, i, 0)), own_spec],
            out_specs=pl.BlockSpec((tr, C), lambda i, me: (i, 0))),
        compiler_params=_cp(("parallel",)), name=name)(me, land, own)


def _reduce_begin(g, tag):
    p = _pair_add(g, f"rs_pair_{tag}")
    token, finish = _split_exchange(p, p.shape[1], lambda ref, px, py, c, r0, cr: ref.at[2 * px + py, pl.ds(r0, cr), :],
                                    f"rs_a2a_{tag}")
    return (finish, g.shape, tag), token


def _reduce_end(state, after):
    finish, shape, tag = state
    p, land = finish(after)
    t = _sum_slots_own(land, p, f"rs_sum_{tag}")
    return _pair_gather(t, f"rs_join_{tag}").reshape(shape[1], shape[2])


def _all_reduce_begin(v, tag):
    p = _sum_slots(_pair_gather(v, f"ar_pair_{tag}"), F32, f"ar_add_{tag}")
    token, finish = _split_exchange(p, p.shape[0], lambda ref, px, py, c, r0, cr: ref.at[pl.ds(r0, cr), :], f"ar_a2a_{tag}")
    return (finish, tag), token


def _all_reduce_end(state, after):
    finish, tag = state
    p, land = finish(after)
    return _sum_slots_own(land, p, f"ar_sum_{tag}")


def _gather_begin(shard, tag):
    half = shard.shape[0] // 2
    token, finish = _split_exchange(
        shard, half, lambda ref, px, py, c, r0, cr: ref.at[pl.ds(pl.multiple_of(c * half + r0, 16), cr), :], f"gather_{tag}")
    return (finish, tag), token


def _gather_end(state, after):
    finish, tag = state
    shard, land = finish(after)
    return _gather_finish(shard, land, f"gather_{tag}_finish")


R_BRANCH, R_OUT, R_WIN, ROWS_A = 0, 256, 512, 1888
R_FF1, R_FF2, R_XKV, R_XQ, R_XO, ROWS_B = 0, 1024, 2048, 2560, 2816, 3072
WIN_ROWS = N_IN // 4


def _w_in_t(a):
    return jnp.transpose(a, (2, 0, 1))


def _pack_shard(w, l):
    xkv, wb = w['w_xkv'][l], w['w_branch_b'][l]
    a = [jnp.concatenate([w['w_branch_a'][l], wb[:256], wb[256:], w['w_branch_c'][l]], axis=1), w['w_out'][l],
         jnp.pad(_w_in_t(w['w_in'])[:, l, :], ((0, ROWS_A - R_WIN - WIN_ROWS), (0, 0)))]
    b = [w['w_ff1'][l], w['w_ff2'][l], jnp.concatenate([xkv[:512], xkv[512:]], axis=1), w['w_xq'][l], w['w_xo'][l]]
    return jnp.concatenate(a, axis=0).astype(BF16), jnp.concatenate(b, axis=0).astype(BF16)


def _w_in_rows(gathered):
    t = gathered[:, R_WIN:R_WIN + WIN_ROWS, :].reshape(N_IN, PACK_COLS)
    return jnp.concatenate([t[2312:5384], t[256:1792], t[1800:2312], t[0:256],
                            jnp.pad(t[1792:1800], ((0, NP - P_F - 8), (0, 0)))], axis=0)


def _w_in_grad_rows(grads, dwt):
    t = jnp.concatenate([dwt[P_A:P_A + 256], dwt[P_Q:P_Q + 1536], dwt[P_F:P_F + 8], dwt[P_C:P_C + 512], dwt[P_G:P_G + 3072]],
                        axis=0)
    for j in range(4):
        rows = t[j * WIN_ROWS:(j + 1) * WIN_ROWS][None].astype(grads.dtype)
        grads = lax.dynamic_update_slice(grads, rows, (j, R_WIN, 0))
    return grads


def _small_prep(sw, l):
    eye = jnp.eye(4, dtype=F32)
    bd = jnp.einsum('gh,gcd->gchd', eye, sw['pool_w'][l]).reshape(POOL_W, POOL_W).astype(BF16)
    tril = jnp.tril(jnp.ones((SGU_CHUNK, SGU_CHUNK), F32))
    wm = (sw['sgu_w'][l] * tril[None]).astype(BF16)
    return dict(
        g_mix=sw['norm_mix_g'][l][None], g_x=sw['norm_xattn_g'][l][None], g_mem=sw['norm_mem_g'][l][None],
        g_ffn=sw['norm_ffn_g'][l][None], bd=bd, pool_scale=sw['pool_scale'][l][None],
        bf=jnp.pad(sw['b_forget'][l], (0, FCOLS - 8))[None], sgu_g=sw['sgu_norm_g'][l][None], wm=wm,
        wmt=jnp.transpose(wm, (0, 2, 1)), sgu_bias=jnp.repeat(sw['sgu_b'][l].T, 64, axis=1), bg=sw['b_gate'][l][None])


def _rows4(r0):
    return dict(n=D, k=D, tn=D, b_block=(4, 256, PACK_COLS), b_index=lambda i, j, k: (0, r0 // 256, 0))


def _rows_t(r0):
    return dict(tb=True, n=D, k=D, tn=D, b_block=(4, 256, PACK_COLS), b_index=lambda i, j, k: (0, r0 // 256, 0))


def _rows_grad(r0):
    return dict(ta=True, tm=D, tn=512, o_block=(4, 256, 512), o_index=lambda i, j, k: (0, r0 // 256, j))


def _add_to(r, e):
    return e + r


def _after(v, token):
    return v if token is None else v + token[0, 0]


def _layer_fwd(x, mem, GA, w_in_t, sp, l, token, second):
    t = f"l{l}"
    S = x.shape[0]
    h = _rms_fwd(x, _after(sp['g_mix'], token), f"rms_mix_{t}")
    proj = _mm(h, w_in_t, name=f"proj_{t}", out_dtype=F32, tb=True)
    d, ya = _pool_fwd(proj, sp['bd'], sp['pool_scale'], f"pool_fwd_{t}")
    fcum = _fgate_fwd(proj, sp['bf'], f"fgate_fwd_{t}")
    f8 = fcum[:, :8]
    fcol = f8.reshape(S, 4, 2).transpose(1, 0, 2)
    frow = f8.T.reshape(4, 2, S)
    qkv = proj[:, P_Q:P_Q + 3 * FOX_W].astype(BF16)
    o, o32, lse = _fox_fwd(qkv, fcol, frow, f"fox_fwd_{t}")
    sg = _sgu_fwd(proj, sp['sgu_g'], sp['wm'], sp['sgu_bias'], f"sgu_fwd_{t}")
    merged = _merge_fwd(proj, ya, o, sg, GA, sp['bg'], f"merge_fwd_{t}")
    x1 = _mm(merged, GA, name=f"out_{t}", out_dtype=F32, extra=x, epi=_add_to, **_rows4(R_OUT))
    GB, token = second(x1)
    hx = _rms_fwd(x1, _after(sp['g_x'], token), f"rms_x_{t}")
    hm = _rms_fwd(mem, sp['g_mem'], f"rms_mem_{t}")
    xq = _mm(hx, GB, name=f"xq_{t}", out_dtype=BF16, **_rows4(R_XQ))
    kv = _mm(hm, GB, name=f"xkv_{t}", out_dtype=BF16, n=2 * D, k=D, tn=512, tk=512, b_block=(None, 512, 512),
             b_index=lambda i, j, k: (j, R_XKV // 512, k))
    o2 = _xattn_fwd(xq, kv, f"xattn_fwd_{t}")
    x2 = _mm(o2, GB, name=f"xo_{t}", out_dtype=F32, extra=x1, epi=_add_to, **_rows4(R_XO))
    hf = _rms_fwd(x2, sp['g_ffn'], f"rms_ffn_{t}")
    z = _mm(hf, GB, name=f"ff1_{t}", out_dtype=BF16, n=D_FF, k=D, tn=D, b_block=(None, 1024, PACK_COLS),
            b_index=lambda i, j, k: (j, R_FF1 // 1024, 0))
    x3 = _mm(z, GB, name=f"ff2_{t}", out_dtype=F32, a_fn=_relu2, extra=x2, epi=_add_to, n=D, k=D_FF, tk=1024, tn=D,
             b_block=(None, 1024, PACK_COLS), b_index=lambda i, j, k: (k, R_FF2 // 1024, 0))
    saved = dict(x=x, h=h, proj=proj, d=d, ya=ya, fcol=fcol, frow=frow, qkv=qkv, o=o, o32=o32, lse=lse, sg=sg, merged=merged,
                 x1=x1, hx=hx, hm=hm, xq=xq, kv=kv, o2=o2, x2=x2, hf=hf, z=z, GA=GA, GB=GB, w_in_t=w_in_t)
    return x3, saved


def _layer_bwd(dx3, mem, sp, sv, l, token, early):
    t = f"l{l}"
    S = dx3.shape[0]
    GA, GB = sv['GA'], sv['GB']
    gs = {}
    dx3 = _after(dx3, token)
    gb = lax.empty((4, ROWS_B, PACK_COLS), BF16)
    dz = _mm(dx3, GB, name=f"d_a2_{t}", out_dtype=BF16, tb=True, n=D_FF, k=D, tn=D, b_block=(None, 1024, PACK_COLS),
             b_index=lambda i, j, k: (j, R_FF2 // 1024, 0), extra=sv['z'],
             epi=lambda r, e: r * (2.0 * jnp.maximum(e.astype(F32), 0.0)))
    gb = _mm(sv['z'], dx3, name=f"dw_ff2_{t}", out_dtype=BF16, ta=True, a_fn=_relu2, into=gb, tm=1024, tn=D,
             o_block=(None, 1024, PACK_COLS), o_index=lambda i, j, k: (i, R_FF2 // 1024, 0))
    gb = _mm(sv['hf'], dz, name=f"dw_ff1_{t}", out_dtype=BF16, ta=True, into=gb, tm=1024, tn=D,
             o_block=(None, 1024, PACK_COLS), o_index=lambda i, j, k: (j, R_FF1 // 1024, 0))
    dx2, gs['norm_ffn_g'] = _mm(dz, GB, name=f"d_hf_{t}", out_dtype=F32, tb=True, n=D, k=D_FF, tm=512, tn=D, tk=1024,
                                b_block=(None, 1024, PACK_COLS), b_index=lambda i, j, k: (k, R_FF1 // 1024, 0),
                                norm_bwd=(sv['x2'], sp['g_ffn'], dx3))
    do2 = _mm(dx2, GB, name=f"d_o2_{t}", out_dtype=BF16, **_rows_t(R_XO))
    gb = _mm(sv['o2'], dx2, name=f"dw_xo_{t}", out_dtype=BF16, into=gb, **_rows_grad(R_XO))
    dxq, dkv = _xattn_bwd(sv['xq'], sv['kv'], do2, f"xattn_bwd_{t}")
    gb = _mm(sv['hm'], dkv, name=f"dw_xkv_{t}", out_dtype=BF16, ta=True, into=gb, tm=512, tn=512,
             o_block=(None, 512, 512), o_index=lambda i, j, k: (j, R_XKV // 512, i))
    dhm = _mm(dkv, GB, name=f"d_hm_{t}", out_dtype=F32, tb=True, n=D, k=2 * D, tn=512, tk=512, b_block=(None, 512, 512),
              b_index=lambda i, j, k: (k, R_XKV // 512, j))
    gs['norm_mem_g'] = _rms_bwd(dhm, mem, sp['g_mem'], None, f"rms_mem_bwd_{t}")
    gb = _mm(sv['hx'], dxq, name=f"dw_xq_{t}", out_dtype=BF16, into=gb, **_rows_grad(R_XQ))
    token = early(gb)
    dx1, gs['norm_xattn_g'] = _mm(dxq, GB, name=f"d_hx_{t}", out_dtype=F32, tm=512, **_rows_t(R_XQ),
                                  norm_bwd=(sv['x1'], _after(sp['g_x'], token), dx2))
    ga = jnp.zeros((4, ROWS_A, PACK_COLS), BF16)
    ga = _mm(sv['merged'], dx1, name=f"dw_out_{t}", out_dtype=BF16, into=ga, **_rows_grad(R_OUT))
    dm = _mm(dx1, GA, name=f"d_merged_{t}", out_dtype=F32, **_rows_t(R_OUT))
    dg, dya, do, dsg, ga, gs['b_gate'] = _merge_bwd(dm, sv['proj'], sv['ya'], sv['o'], sv['sg'], GA, sp['bg'], ga, f"merge_bwd_{t}")
    dc, dws, dbias, gs['sgu_norm_g'] = _sgu_bwd(dsg, sv['proj'], sp['sgu_g'], sp['wm'], sp['wmt'], sp['sgu_bias'], f"sgu_bwd_{t}")
    tril = jnp.tril(jnp.ones((SGU_CHUNK, SGU_CHUNK), F32))
    gs['sgu_w'] = dws * tril[None]
    gs['sgu_b'] = dbias.reshape(SGU_CHUNK, 4, 64).sum(-1).T
    dq, dk, dv, dfrow, dfcol = _fox_bwd(sv['qkv'], sv['o32'], do, sv['lse'], sv['fcol'], sv['frow'], f"fox_bwd_{t}")
    dF = jnp.pad(dfrow.reshape(8, S).T + dfcol.transpose(1, 0, 2).reshape(S, 8), ((0, 0), (0, FCOLS - 8)))
    df, dbf = _fgate_bwd(dF, sv['proj'], sp['bf'], f"fgate_bwd_{t}")
    gs['b_forget'] = dbf[:, :8]
    da, dbd, gs['pool_scale'] = _pool_bwd(dya, sv['d'], sp['bd'], sp['pool_scale'], f"pool_bwd_{t}")
    gs['pool_w'] = jnp.stack([dbd[g * 64:(g + 1) * 64, g * 64:(g + 1) * 64] for g in range(4)])
    dproj = jnp.concatenate([dg, dq, dk, dv, dc, da, df], axis=1)
    dwt = _mm(dproj, sv['h'], name=f"dw_in_{t}", out_dtype=BF16, ta=True, tm=512, tn=1024)
    ga = _w_in_grad_rows(ga, dwt)
    dx, gs['norm_mix_g'] = _mm(dproj, sv['w_in_t'], name=f"d_h_{t}", out_dtype=F32, tm=1024, tk=512, tn=D,
                               norm_bwd=(sv['x'], sp['g_mix'], dx1), vmem_limit=WIDE_VMEM_LIMIT)
    return dx, ga, gs


SMALL_ROWS = 1424
GRAD_BLOCKS = {
    'w_ff1': ('b', lambda i: (R_FF1 // 256 + i, 0)), 'w_ff2': ('b', lambda i: (R_FF2 // 256 + i, 0)),
    'w_xq': ('b', lambda i: (R_XQ // 256 + i, 0)), 'w_xo': ('b', lambda i: (R_XO // 256 + i, 0)),
    'w_xkv': ('b', lambda i: (R_XKV // 256 + i % 2, i // 2)), 'w_out': ('a', lambda i: (R_OUT // 256 + i, 0)),
    'w_branch_a': ('a', lambda i: (R_BRANCH // 256, 0)), 'w_branch_b': ('a', lambda i: (R_BRANCH // 256, 1 + i)),
    'w_branch_c': ('a', lambda i: (R_BRANCH // 256, 3)),
}


def _pack_small(parts):
    flat = jnp.concatenate([p.reshape(-1) for p in parts])
    return jnp.pad(flat, (0, SMALL_ROWS * 128 - flat.shape[0])).reshape(SMALL_ROWS, 128)


def _unpack_small(buf, shapes):
    flat, out, r = buf.reshape(-1), [], 0
    for s in shapes:
        n = math.prod(s)
        out.append(flat[r:r + n].reshape(s))
        r += n
    return out


def kernel(x, mem, norm_mix_g, w_in, b_forget, pool_w, pool_scale, sgu_norm_g, sgu_w, sgu_b, w_branch_a, w_branch_b, w_branch_c, b_gate, w_out, norm_xattn_g, norm_mem_g, w_xq, w_xkv, w_xo, norm_ffn_g, w_ff1, w_ff2, final_norm_g, loss_target, m_norm_mix_g, m_w_in, m_b_forget, m_pool_w, m_pool_scale, m_sgu_norm_g, m_sgu_w, m_sgu_b, m_w_branch_a, m_w_branch_b, m_w_branch_c, m_b_gate, m_w_out, m_norm_xattn_g, m_norm_mem_g, m_w_xq, m_w_xkv, m_w_xo, m_norm_ffn_g, m_w_ff1, m_w_ff2, m_final_norm_g, v_norm_mix_g, v_w_in, v_b_forget, v_pool_w, v_pool_scale, v_sgu_norm_g, v_sgu_w, v_sgu_b, v_w_branch_a, v_w_branch_b, v_w_branch_c, v_b_gate, v_w_out, v_norm_xattn_g, v_norm_mem_g, v_w_xq, v_w_xkv, v_w_xo, v_norm_ffn_g, v_w_ff1, v_w_ff2, v_final_norm_g):
    args = (norm_mix_g, w_in, b_forget, pool_w, pool_scale, sgu_norm_g, sgu_w, sgu_b, w_branch_a, w_branch_b, w_branch_c, b_gate,
            w_out, norm_xattn_g, norm_mem_g, w_xq, w_xkv, w_xo, norm_ffn_g, w_ff1, w_ff2, final_norm_g)
    margs = (m_norm_mix_g, m_w_in, m_b_forget, m_pool_w, m_pool_scale, m_sgu_norm_g, m_sgu_w, m_sgu_b, m_w_branch_a, m_w_branch_b,
             m_w_branch_c, m_b_gate, m_w_out, m_norm_xattn_g, m_norm_mem_g, m_w_xq, m_w_xkv, m_w_xo, m_norm_ffn_g, m_w_ff1, m_w_ff2,
             m_final_norm_g)
    vargs = (v_norm_mix_g, v_w_in, v_b_forget, v_pool_w, v_pool_scale, v_sgu_norm_g, v_sgu_w, v_sgu_b, v_w_branch_a, v_w_branch_b,
             v_w_branch_c, v_b_gate, v_w_out, v_norm_xattn_g, v_norm_mem_g, v_w_xq, v_w_xkv, v_w_xo, v_norm_ffn_g, v_w_ff1, v_w_ff2,
             v_final_norm_g)
    w = dict(zip(W_NAMES, args))
    mo = dict(zip(W_NAMES, margs))
    vo = dict(zip(W_NAMES, vargs))
    xs, mems, tgt = x[0], mem[0], loss_target[0]
    shards = [_pack_shard(w, l) for l in range(DEPTH)]
    preps = [_small_prep(w, l) for l in range(DEPTH)]

    first_a, _ = _gather_begin(shards[0][0], "a_l0")
    pending_b, token = _gather_begin(shards[0][1], "b_l0")
    GA = None
    act, saved = xs, []
    for l in range(DEPTH):
        nxt = {}
        if l + 1 < DEPTH:
            nxt['a'], ta = _gather_begin(shards[l + 1][0], f"a_l{l + 1}")
            token = ta if token is None else token + ta
        if l == 0:
            GA = _gather_end(first_a, shards[DEPTH - 1][1])

        def second(x1, l=l, pending_b=pending_b, nxt=nxt):
            GB = _gather_end(pending_b, x1)
            if l + 1 == DEPTH:
                return GB, None
            nxt['b'], tb = _gather_begin(shards[l + 1][1], f"b_l{l + 1}")
            return GB, tb

        act, sv = _layer_fwd(act, mems, GA, _w_in_rows(GA), preps[l], l, token, second)
        saved.append(sv)
        if l + 1 < DEPTH:
            GA = _gather_end(nxt['a'], act)
            pending_b, token = nxt['b'], None
    loss_part, dact, d_final_g = _loss_head(act, w['final_norm_g'][None], tgt, "loss_head")

    red_a, red_b, small_g = [None] * DEPTH, [None] * DEPTH, [None] * DEPTH
    token, state_a = None, None
    for l in reversed(range(DEPTH)):
        early = {}

        def start_b(gb, l=l, early=early):
            early['state'], tok = _reduce_begin(gb, f"b_l{l}")
            return tok

        dact, ga, small_g[l] = _layer_bwd(dact, mems, preps[l], saved[l], l, token, start_b)
        if state_a is not None:
            red_a[l + 1] = _reduce_end(state_a, dact)
        red_b[l] = _reduce_end(early['state'], dact)
        state_a, token = _reduce_begin(ga, f"a_l{l}")
    grad_x = dact[None]
    per_layer = [n for n in SMALL_NAMES if n != 'final_norm_g']
    small_shapes = [w[n].shape for n in per_layer] + [(D,), (1,)]
    parts = [jnp.stack([small_g[l][n].reshape(w[n].shape[1:]) for l in range(DEPTH)]) for n in per_layer]
    state_small, token_small = _all_reduce_begin(_pack_small(parts + [d_final_g.reshape(D), loss_part.reshape(1)]), "small")
    token = token + token_small

    grads, delta, new_m, new_v = {}, {}, {}, {}
    for n, (buf, g_index) in GRAD_BLOCKS.items():
        if buf == 'b':
            grads[n], delta[n], new_m[n], new_v[n] = _adamw_packed(red_b, w[n], mo[n], vo[n], g_index, f"adamw_{n}", token)
    red_a[0] = _reduce_end(state_a, new_v['w_xkv'])
    small_red = _unpack_small(_all_reduce_end(state_small, red_a[0]), small_shapes)
    grads.update(zip(per_layer + ['final_norm_g'], small_red[:-1]))
    loss = small_red[-1].reshape(())
    for n, (buf, g_index) in GRAD_BLOCKS.items():
        if buf == 'a':
            grads[n], delta[n], new_m[n], new_v[n] = _adamw_packed(red_a, w[n], mo[n], vo[n], g_index, f"adamw_{n}", token)
    g_t = jnp.stack([r[R_WIN:R_WIN + WIN_ROWS] for r in red_a], axis=1)
    upd = _adamw(g_t, _w_in_t(w['w_in']), _w_in_t(mo['w_in']), _w_in_t(vo['w_in']), "adamw_w_in", block=(WIN_ROWS, DEPTH, 128))
    grads['w_in'], delta['w_in'], new_m['w_in'], new_v['w_in'] = [jnp.transpose(a, (1, 2, 0)) for a in (g_t,) + tuple(upd)]
    small_all = per_layer + ['final_norm_g']
    shapes_all = [w[n].shape for n in small_all]
    packed = [_pack_small([d[n] for n in small_all])[None] for d in (grads, w, mo, vo)]
    ds, ms, vs = _adamw(*packed, "adamw_small")
    for n, a, b, c in zip(small_all, _unpack_small(ds[0], shapes_all), _unpack_small(ms[0], shapes_all), _unpack_small(vs[0], shapes_all)):
        delta[n], new_m[n], new_v[n] = a, b, c

    return (loss, grad_x, *[grads[n] for n in W_NAMES], *[delta[n] for n in W_NAMES], *[new_m[n] for n in W_NAMES],
            *[new_v[n] for n in W_NAMES])
```

```python
import math

import jax
import jax.numpy as jnp
from jax import lax
from jax.experimental import pallas as pl
from jax.experimental.pallas import tpu as pltpu

F32 = jnp.float32
BF16 = jnp.bfloat16

D = 1024
DEPTH = 2
POOL_W = 256
FOX_W = 512
SGU_W = 256
SGU_CHUNK = 128
N_IN = 5384
P_G, P_Q, P_K, P_V, P_C, P_A, P_F = 0, 3072, 3584, 4096, 4608, 5120, 5376
NP = 5632
XH, XHD = 4, 256
D_FF = 4096
EPS = 1e-6
NEG = -1e30
FOX_SCALE = 64 ** -0.5
X_SCALE = 256 ** -0.5
GELU_K = math.sqrt(2.0 / math.pi)
GELU_C = 0.044715

ADAM_LR, ADAM_B1, ADAM_B2, ADAM_EPS, ADAM_WD, ADAM_STEP = 0.001, 0.9, 0.999, 1e-08, 0.01, 10

VMEM_LIMIT = 48 * 1024 * 1024
WIDE_VMEM_LIMIT = 60 * 1024 * 1024
MESH = pl.DeviceIdType.MESH

IN_NAMES = ['x', 'mem', 'norm_mix_g', 'w_in', 'b_forget', 'pool_w', 'pool_scale', 'sgu_norm_g', 'sgu_w', 'sgu_b',
            'w_branch_a', 'w_branch_b', 'w_branch_c', 'b_gate', 'w_out', 'norm_xattn_g', 'norm_mem_g', 'w_xq',
            'w_xkv', 'w_xo', 'norm_ffn_g', 'w_ff1', 'w_ff2', 'final_norm_g']
W_NAMES = IN_NAMES[2:]
BIG_NAMES = ['w_in', 'w_branch_a', 'w_branch_b', 'w_branch_c', 'w_out', 'w_xq', 'w_xkv', 'w_xo', 'w_ff1', 'w_ff2']
SMALL_NAMES = [n for n in W_NAMES if n not in BIG_NAMES]
PACK_COLS = 1024


ANY = pl.BlockSpec(memory_space=pl.ANY)


def _cp(sem=None, vmem_limit=VMEM_LIMIT):
    return pltpu.CompilerParams(dimension_semantics=sem, vmem_limit_bytes=vmem_limit)


def _mm(a, b, *, name, out_dtype, ta=False, tb=False, tm=1024, tn=512, tk=1024, a_fn=None, extra=None, epi=None,
        n=None, k=None, b_block=None, b_index=None, into=None, o_block=None, o_index=None, norm_bwd=None,
        vmem_limit=VMEM_LIMIT):
    M = a.shape[1] if ta else a.shape[0]
    K = k if k is not None else (a.shape[0] if ta else a.shape[1])
    N = n if n is not None else (b.shape[0] if tb else b.shape[1])
    tm, tn, tk = min(tm, M), min(tn, N), min(tk, K)
    assert M % tm == 0 and N % tn == 0 and K % tk == 0, (name, M, N, K)
    nk = K // tk
    a_spec = pl.BlockSpec((tk, tm), lambda i, j, k: (k, i)) if ta else pl.BlockSpec((tm, tk), lambda i, j, k: (i, k))
    if b_block is not None:
        b_spec = pl.BlockSpec(b_block, b_index)
    else:
        b_spec = pl.BlockSpec((tn, tk), lambda i, j, k: (j, k)) if tb else pl.BlockSpec((tk, tn), lambda i, j, k: (k, j))
    dn = (((0 if ta else 1,), (1 if tb else 0,)), ((), ()))
    tile = pl.BlockSpec((tm, tn), lambda i, j, k: (i, j))
    o_spec = pl.BlockSpec(o_block, o_index) if into is not None else tile
    in_specs = [a_spec, b_spec] + ([tile] if extra is not None else []) + ([ANY] if into is not None else [])
    vec = pl.BlockSpec((1, N), lambda i, j, k: (0, 0))
    if norm_bwd is not None:
        assert tn == N and extra is None and into is None, name
        in_specs += [tile, tile, vec]
    n_in = len(in_specs)

    def body(*refs):
        a_ref, b_ref = refs[0], refs[1]
        e_ref = refs[2] if extra is not None else None
        o_ref, acc_ref = refs[n_in], refs[-1]
        kk = pl.program_id(2)
        first_rows = pl.program_id(0) == 0

        @pl.when(kk == 0)
        def _():
            acc_ref[...] = jnp.zeros_like(acc_ref)

        av = a_ref[...]
        if a_fn is not None:
            av = a_fn(av)
        bv = b_ref[...]
        if bv.ndim == 3:
            bv = bv.reshape(-1, bv.shape[-1])
        acc_ref[...] += lax.dot_general(av.astype(BF16), bv.astype(BF16), dn, preferred_element_type=F32)

        @pl.when(kk == nk - 1)
        def _():
            r = acc_ref[...]
            if norm_bwd is not None:
                x_ref, r_ref, g_ref, dg_ref = refs[2], refs[3], refs[4], refs[n_in + 1]
                xv = x_ref[...]
                rstd = lax.rsqrt(jnp.mean(xv * xv, axis=-1, keepdims=True) + EPS)
                xhat = xv * rstd

                @pl.when(first_rows)
                def _():
                    dg_ref[...] = jnp.zeros_like(dg_ref)

                dg_ref[...] += jnp.sum(r * xhat, axis=0, keepdims=True)
                t = r * g_ref[...]
                o_ref[...] = r_ref[...] + rstd * (t - xhat * jnp.mean(t * xhat, axis=-1, keepdims=True))
                return
            if epi is not None:
                r = epi(r, e_ref[...])
            o_ref[...] = r.astype(o_ref.dtype).reshape(o_ref.shape)

    args = (a, b) + ((extra,) if extra is not None else ()) + ((into,) if into is not None else ())
    out_shape = jax.ShapeDtypeStruct(into.shape, into.dtype) if into is not None else jax.ShapeDtypeStruct((M, N), out_dtype)
    semantics = ("parallel", "parallel", "arbitrary")
    if norm_bwd is not None:
        xn, gn, dres = norm_bwd
        args += (xn, dres, gn)
        out_shape = (jax.ShapeDtypeStruct((M, N), F32), jax.ShapeDtypeStruct((1, N), F32))
        o_spec = (tile, vec)
        semantics = ("arbitrary", "arbitrary", "arbitrary")
    return pl.pallas_call(
        body, out_shape=out_shape, grid=(M // tm, N // tn, nk), in_specs=in_specs, out_specs=o_spec,
        scratch_shapes=[pltpu.VMEM((tm, tn), F32)], input_output_aliases={n_in - 1: 0} if into is not None else {},
        compiler_params=_cp(semantics, vmem_limit), name=name)(*args)


def _relu2(z):
    r = jnp.maximum(z.astype(F32), 0.0)
    return r * r


def _rms_fwd(x, g, name, tr=512):
    R, n = x.shape
    tr = min(tr, R)

    def body(x_ref, g_ref, h_ref):
        xv = x_ref[...]
        rstd = lax.rsqrt(jnp.mean(xv * xv, axis=-1, keepdims=True) + EPS)
        h_ref[...] = (xv * rstd * g_ref[...]).astype(BF16)

    return pl.pallas_call(
        body, out_shape=jax.ShapeDtypeStruct((R, n), BF16), grid=(R // tr,),
        in_specs=[pl.BlockSpec((tr, n), lambda i: (i, 0)), pl.BlockSpec((1, n), lambda i: (0, 0))],
        out_specs=pl.BlockSpec((tr, n), lambda i: (i, 0)), compiler_params=_cp(("parallel",)), name=name)(x, g)


def _rms_bwd(dh, x, g, dres, name, tr=512):
    R, n = x.shape
    tr = min(tr, R)
    need_dx = dres is not None

    def body(*refs):
        if need_dx:
            dh_ref, x_ref, g_ref, r_ref, dx_ref, dg_ref = refs
        else:
            dh_ref, x_ref, g_ref, dg_ref = refs
        i = pl.program_id(0)
        xv = x_ref[...]
        dhv = dh_ref[...].astype(F32)
        rstd = lax.rsqrt(jnp.mean(xv * xv, axis=-1, keepdims=True) + EPS)
        xhat = xv * rstd

        @pl.when(i == 0)
        def _():
            dg_ref[...] = jnp.zeros_like(dg_ref)

        dg_ref[...] += jnp.sum(dhv * xhat, axis=0, keepdims=True)
        if need_dx:
            t = dhv * g_ref[...]
            dx_ref[...] = r_ref[...] + rstd * (t - xhat * jnp.mean(t * xhat, axis=-1, keepdims=True))

    row = pl.BlockSpec((tr, n), lambda i: (i, 0))
    vec = pl.BlockSpec((1, n), lambda i: (0, 0))
    if need_dx:
        return pl.pallas_call(
            body, out_shape=(jax.ShapeDtypeStruct((R, n), F32), jax.ShapeDtypeStruct((1, n), F32)), grid=(R // tr,),
            in_specs=[row, row, vec, row], out_specs=(row, vec), compiler_params=_cp(("arbitrary",)), name=name)(dh, x, g, dres)
    return pl.pallas_call(
        body, out_shape=jax.ShapeDtypeStruct((1, n), F32), grid=(R // tr,),
        in_specs=[row, row, vec], out_specs=vec, compiler_params=_cp(("arbitrary",)), name=name)(dh, x, g)


def _loss_head(x, g, tgt, name, tr=512):
    R, n = x.shape

    def body(x_ref, g_ref, t_ref, loss_ref, dx_ref, dg_ref):
        i = pl.program_id(0)
        xv = x_ref[...]
        gv = g_ref[...]
        rstd = lax.rsqrt(jnp.mean(xv * xv, axis=-1, keepdims=True) + EPS)
        xhat = xv * rstd
        e = xhat * gv - t_ref[...]

        @pl.when(i == 0)
        def _():
            loss_ref[...] = jnp.zeros_like(loss_ref)
            dg_ref[...] = jnp.zeros_like(dg_ref)

        loss_ref[...] += 0.5 * jnp.sum(jnp.sum(e * e, axis=-1, keepdims=True) / n, axis=0, keepdims=True)
        dy = e / n
        dg_ref[...] += jnp.sum(dy * xhat, axis=0, keepdims=True)
        t = dy * gv
        dx_ref[...] = rstd * (t - xhat * jnp.mean(t * xhat, axis=-1, keepdims=True))

    row = pl.BlockSpec((tr, n), lambda i: (i, 0))
    vec = pl.BlockSpec((1, n), lambda i: (0, 0))
    one = pl.BlockSpec((1, 1), lambda i: (0, 0))
    return pl.pallas_call(
        body, out_shape=(jax.ShapeDtypeStruct((1, 1), F32), jax.ShapeDtypeStruct((R, n), F32), jax.ShapeDtypeStruct((1, n), F32)),
        grid=(R // tr,), in_specs=[row, vec, row], out_specs=(one, row, vec),
        compiler_params=_cp(("arbitrary",)), name=name)(x, g, tgt)


def _pool_masks(S):
    row = lax.broadcasted_iota(jnp.int32, (S, POOL_W), 0)
    grp = lax.broadcasted_iota(jnp.int32, (S, POOL_W), 1) // 64
    win = jnp.where(grp == 0, 2, jnp.where(grp == 1, 4, jnp.where(grp == 2, 8, 16)))
    cnt = jnp.minimum(row + 1, win).astype(F32)
    return row, grp, cnt


def _by_group(grp, v0, v1, v2, v3):
    return jnp.where(grp == 0, v0, jnp.where(grp == 1, v1, jnp.where(grp == 2, v2, v3)))


def _pool_fwd(proj, bd, scale, name):
    S = proj.shape[0]

    def body(a_ref, bd_ref, sc_ref, d_ref, y_ref):
        a = a_ref[...]
        row, grp, cnt = _pool_masks(S)

        def back(v, k):
            return jnp.where(row >= k, pltpu.roll(v, k, 0), 0.0)

        s1 = a + back(a, 1)
        s2 = s1 + back(s1, 2)
        s3 = s2 + back(s2, 4)
        s4 = s3 + back(s3, 8)
        d = (_by_group(grp, s1, s2, s3, s4) / cnt - a).astype(BF16)
        d_ref[...] = d
        y_ref[...] = (jnp.dot(d, bd_ref[...], preferred_element_type=F32) * sc_ref[...]).astype(BF16)

    full = lambda r, c: pl.BlockSpec((r, c), lambda i: (0, 0))
    return pl.pallas_call(
        body, out_shape=(jax.ShapeDtypeStruct((S, POOL_W), BF16), jax.ShapeDtypeStruct((S, POOL_W), BF16)), grid=(1,),
        in_specs=[pl.BlockSpec((S, POOL_W), lambda i: (0, P_A // POOL_W)), full(POOL_W, POOL_W), full(1, POOL_W)],
        out_specs=(full(S, POOL_W), full(S, POOL_W)), compiler_params=_cp(("arbitrary",)), name=name)(proj, bd, scale)


def _pool_bwd(dya, d, bd, scale, name):
    S = dya.shape[0]

    def body(dy_ref, d_ref, bd_ref, sc_ref, da_ref, dbd_ref, dsc_ref):
        dy = dy_ref[...]
        dv = d_ref[...]
        bdv = bd_ref[...]
        row, grp, cnt = _pool_masks(S)
        yraw = jnp.dot(dv, bdv, preferred_element_type=F32)
        dsc_ref[...] = jnp.sum(dy * yraw, axis=0, keepdims=True)
        tb = (dy * sc_ref[...]).astype(BF16)
        dbd_ref[...] = lax.dot_general(dv, tb, (((0,), (0,)), ((), ())), preferred_element_type=F32)
        dd = lax.dot_general(tb, bdv, (((1,), (1,)), ((), ())), preferred_element_type=F32)
        e = dd / cnt

        def fwd(v, k):
            return jnp.where(row < S - k, pltpu.roll(v, S - k, 0), 0.0)

        r1 = e + fwd(e, 1)
        r2 = r1 + fwd(r1, 2)
        r3 = r2 + fwd(r2, 4)
        r4 = r3 + fwd(r3, 8)
        da_ref[...] = (_by_group(grp, r1, r2, r3, r4) - dd).astype(BF16)

    full = lambda r, c: pl.BlockSpec((r, c), lambda i: (0, 0))
    return pl.pallas_call(
        body, out_shape=(jax.ShapeDtypeStruct((S, POOL_W), BF16), jax.ShapeDtypeStruct((POOL_W, POOL_W), F32),
                         jax.ShapeDtypeStruct((1, POOL_W), F32)), grid=(1,),
        in_specs=[full(S, POOL_W), full(S, POOL_W), full(POOL_W, POOL_W), full(1, POOL_W)],
        out_specs=(full(S, POOL_W), full(POOL_W, POOL_W), full(1, POOL_W)),
        compiler_params=_cp(("arbitrary",)), name=name)(dya, d, bd, scale)


FCOLS = 128


def _log_sigmoid(z):
    return -(jnp.maximum(-z, 0.0) + jnp.log1p(jnp.exp(-jnp.abs(z))))


def _fgate_fwd(proj, bf, name):
    S = proj.shape[0]

    def body(f_ref, b_ref, o_ref):
        v = _log_sigmoid(f_ref[...] + b_ref[...])
        row = lax.broadcasted_iota(jnp.int32, (S, FCOLS), 0)
        k = 1
        while k < S:
            v = v + jnp.where(row >= k, pltpu.roll(v, k, 0), 0.0)
            k *= 2
        o_ref[...] = v

    return pl.pallas_call(
        body, out_shape=jax.ShapeDtypeStruct((S, FCOLS), F32), grid=(1,),
        in_specs=[pl.BlockSpec((S, FCOLS), lambda i: (0, P_F // FCOLS)), pl.BlockSpec((1, FCOLS), lambda i: (0, 0))],
        out_specs=pl.BlockSpec((S, FCOLS), lambda i: (0, 0)), compiler_params=_cp(("arbitrary",)), name=name)(proj, bf)


def _fgate_bwd(dF, proj, bf, name):
    S = proj.shape[0]

    def body(dF_ref, f_ref, b_ref, df_ref, db_ref):
        v = dF_ref[...]
        row = lax.broadcasted_iota(jnp.int32, (S, FCOLS), 0)
        k = 1
        while k < S:
            v = v + jnp.where(row < S - k, pltpu.roll(v, S - k, 0), 0.0)
            k *= 2
        z = f_ref[...] + b_ref[...]
        df = v * (1.0 / (1.0 + jnp.exp(z)))
        db_ref[...] = jnp.sum(df, axis=0, keepdims=True)
        df_ref[...] = jnp.concatenate([df, jnp.zeros_like(df)], axis=1).astype(BF16)

    return pl.pallas_call(
        body, out_shape=(jax.ShapeDtypeStruct((S, 2 * FCOLS), BF16), jax.ShapeDtypeStruct((1, FCOLS), F32)), grid=(1,),
        in_specs=[pl.BlockSpec((S, FCOLS), lambda i: (0, 0)), pl.BlockSpec((S, FCOLS), lambda i: (0, P_F // FCOLS)),
                  pl.BlockSpec((1, FCOLS), lambda i: (0, 0))],
        out_specs=(pl.BlockSpec((S, 2 * FCOLS), lambda i: (0, 0)), pl.BlockSpec((1, FCOLS), lambda i: (0, 0))),
        compiler_params=_cp(("arbitrary",)), name=name)(dF, proj, bf)


def _fox_scores(qe, kj, fq, fk, r0, c0, tq, tk, diagonal):
    s = lax.dot_general(qe, kj, (((1,), (1,)), ((), ())), preferred_element_type=F32) * FOX_SCALE
    s = s + (fq - fk)
    if not diagonal:
        return s
    rows = r0 + lax.broadcasted_iota(jnp.int32, (tq, tk), 0)
    cols = c0 + lax.broadcasted_iota(jnp.int32, (tq, tk), 1)
    return jnp.where(rows >= cols, s, NEG)


FOX_TQ, FOX_TK = 512, 512


def _fox_fwd(qkv, fcol, frow, name):
    S = qkv.shape[0]
    tq, tk = FOX_TQ, min(FOX_TK, S)

    def body(q_ref, k_ref, v_ref, fc_ref, fr_ref, o_ref, o32_ref, lse_ref):
        i = pl.program_id(1)
        r0 = i * tq
        q = q_ref[...]
        half = lax.broadcasted_iota(jnp.int32, (tq, 128), 1) // 64
        qs = [jnp.where(half == e, q, jnp.zeros_like(q)) for e in (0, 1)]
        fqs = [fc_ref[0, :, e:e + 1] for e in (0, 1)]

        def step(j, carry, diagonal=False):
            c0 = pl.multiple_of(j * tk, tk)
            kj = k_ref[pl.ds(c0, tk), :]
            vj = v_ref[pl.ds(c0, tk), :]
            out = []
            for e in (0, 1):
                m, l, acc = carry[e]
                s = _fox_scores(qs[e], kj, fqs[e], fr_ref[0, e:e + 1, pl.ds(c0, tk)], r0, c0, tq, tk, diagonal)
                m_new = jnp.maximum(m, jnp.max(s, axis=-1, keepdims=True))
                alpha = jnp.exp(m - m_new)
                p = jnp.exp(s - m_new)
                out.append((m_new, alpha * l + jnp.sum(p, axis=-1, keepdims=True),
                            alpha * acc + jnp.dot(p.astype(BF16), vj, preferred_element_type=F32)))
            return tuple(out)

        init = (jnp.full((tq, 1), NEG, F32), jnp.zeros((tq, 1), F32), jnp.zeros((tq, 128), F32))
        below = r0 // tk
        carry = lax.fori_loop(0, below, step, (init, init))
        carry = step(below, carry, diagonal=True)
        outs = []
        for e in (0, 1):
            m, l, acc = carry[e]
            outs.append(acc / l)
            lse_ref[0, :, e:e + 1] = m + jnp.log(l)
        o = jnp.where(half == 0, outs[0], outs[1])
        o32_ref[...] = o
        o_ref[...] = o.astype(BF16)

    tile = pl.BlockSpec((tq, 128), lambda h, i: (i, h))
    return pl.pallas_call(
        body, out_shape=(jax.ShapeDtypeStruct((S, FOX_W), BF16), jax.ShapeDtypeStruct((S, FOX_W), F32),
                         jax.ShapeDtypeStruct((4, S, 2), F32)), grid=(4, S // tq),
        in_specs=[tile, pl.BlockSpec((S, 128), lambda h, i: (0, 4 + h)), pl.BlockSpec((S, 128), lambda h, i: (0, 8 + h)),
                  pl.BlockSpec((1, tq, 2), lambda h, i: (h, i, 0)), pl.BlockSpec((1, 2, S), lambda h, i: (h, 0, 0))],
        out_specs=(tile, tile, pl.BlockSpec((1, tq, 2), lambda h, i: (h, i, 0))),
        compiler_params=_cp(("parallel", "parallel")), name=name)(qkv, qkv, qkv, fcol, frow)


def _fox_bwd(qkv, o32, do, lse, fcol, frow, name):
    S = qkv.shape[0]
    tq, tk = FOX_TQ, min(FOX_TK, S)
    nq = S // tq

    def body(q_ref, k_ref, v_ref, o_ref, do_ref, lse_ref, fc_ref, fr_ref, dq_ref, dk_ref, dv_ref, dfr_ref, dfc_ref, dk_acc, dv_acc):
        dk_acc[...] = jnp.zeros_like(dk_acc)
        dv_acc[...] = jnp.zeros_like(dv_acc)
        dfr_ref[...] = jnp.zeros_like(dfr_ref)
        half = lax.broadcasted_iota(jnp.int32, (tq, 128), 1) // 64

        def q_block(i, _):
            r0 = pl.multiple_of(i * tq, tq)
            qi = q_ref[pl.ds(r0, tq), :]
            dob = do_ref[pl.ds(r0, tq), :].astype(BF16)
            row_dot = dob.astype(F32) * o_ref[pl.ds(r0, tq), :]
            qs = [jnp.where(half == e, qi, jnp.zeros_like(qi)) for e in (0, 1)]
            dos = [jnp.where(half == e, dob, jnp.zeros_like(dob)) for e in (0, 1)]
            deltas = [jnp.sum(jnp.where(half == e, row_dot, 0.0), axis=-1, keepdims=True) for e in (0, 1)]
            lses = [lse_ref[0, pl.ds(r0, tq), e:e + 1] for e in (0, 1)]
            fqs = [fc_ref[0, pl.ds(r0, tq), e:e + 1] for e in (0, 1)]

            def step(j, carry, diagonal=False):
                dqs, row_sums = carry
                c0 = pl.multiple_of(j * tk, tk)
                kj = k_ref[pl.ds(c0, tk), :]
                vj = v_ref[pl.ds(c0, tk), :]
                new_dq, new_rows, dkc, dvc = [], [], [], []
                for e in (0, 1):
                    s = _fox_scores(qs[e], kj, fqs[e], fr_ref[0, e:e + 1, pl.ds(c0, tk)], r0, c0, tq, tk, diagonal)
                    p = jnp.exp(s - lses[e])
                    dp = lax.dot_general(dos[e], vj, (((1,), (1,)), ((), ())), preferred_element_type=F32)
                    ds = p * (dp - deltas[e])
                    dfr_ref[0, e:e + 1, pl.ds(c0, tk)] -= jnp.sum(ds, axis=0, keepdims=True)
                    new_rows.append(row_sums[e] + jnp.sum(ds, axis=-1, keepdims=True))
                    dsb = (ds * FOX_SCALE).astype(BF16)
                    dkc.append(lax.dot_general(dsb, qi, (((0,), (0,)), ((), ())), preferred_element_type=F32))
                    dvc.append(lax.dot_general(p.astype(BF16), dob, (((0,), (0,)), ((), ())), preferred_element_type=F32))
                    new_dq.append(dqs[e] + jnp.dot(dsb, kj, preferred_element_type=F32))
                half_k = lax.broadcasted_iota(jnp.int32, (tk, 128), 1) // 64
                dk_acc[pl.ds(c0, tk), :] += jnp.where(half_k == 0, dkc[0], dkc[1])
                dv_acc[pl.ds(c0, tk), :] += jnp.where(half_k == 0, dvc[0], dvc[1])
                return tuple(new_dq), tuple(new_rows)

            zero, zero_col = jnp.zeros((tq, 128), F32), jnp.zeros((tq, 1), F32)
            below = r0 // tk
            carry = lax.fori_loop(0, below, step, ((zero, zero), (zero_col, zero_col)))
            dqs, row_sums = step(below, carry, diagonal=True)
            for e in (0, 1):
                dfc_ref[0, pl.ds(r0, tq), e:e + 1] = row_sums[e]
            dq_ref[pl.ds(r0, tq), :] = jnp.where(half == 0, dqs[0], dqs[1]).astype(BF16)
            return 0

        lax.fori_loop(0, nq, q_block, 0)
        dk_ref[...] = dk_acc[...].astype(BF16)
        dv_ref[...] = dv_acc[...].astype(BF16)

    col = lambda off: pl.BlockSpec((S, 128), lambda h: (0, off + h))
    hs2 = pl.BlockSpec((1, S, 2), lambda h: (h, 0, 0))
    h2s = pl.BlockSpec((1, 2, S), lambda h: (h, 0, 0))
    return pl.pallas_call(
        body, out_shape=(jax.ShapeDtypeStruct((S, FOX_W), BF16),) * 3 + (jax.ShapeDtypeStruct((4, 2, S), F32),
                                                                         jax.ShapeDtypeStruct((4, S, 2), F32)), grid=(4,),
        in_specs=[col(0), col(4), col(8), col(0), col(0), hs2, hs2, h2s],
        out_specs=(col(0), col(0), col(0), h2s, hs2),
        scratch_shapes=[pltpu.VMEM((S, 128), F32), pltpu.VMEM((S, 128), F32)],
        compiler_params=_cp(("parallel",)), name=name)(qkv, qkv, qkv, o32, do, lse, fcol, frow)


def _gelu(x):
    return 0.5 * x * (1.0 + jnp.tanh(GELU_K * (x + GELU_C * x * x * x)))


def _gelu_grad(x):
    th = jnp.tanh(GELU_K * (x + GELU_C * x * x * x))
    return 0.5 * (1.0 + th) + 0.5 * x * (1.0 - th * th) * GELU_K * (1.0 + 3.0 * GELU_C * x * x)


def _sgu_parts(c, gn, w_ref, bias):
    zc = _gelu(c)
    u, vv = zc[:, :SGU_W], zc[:, SGU_W:]
    rstd = lax.rsqrt(jnp.mean(vv * vv, axis=-1, keepdims=True) + EPS)
    vhat = vv * rstd
    vnb = (vhat * gn).astype(BF16)
    grp = lax.broadcasted_iota(jnp.int32, (SGU_CHUNK, SGU_W), 1) // 64
    mixed = bias
    for gi in range(4):
        mixed = mixed + jnp.where(grp == gi, jnp.dot(w_ref[gi], vnb, preferred_element_type=F32), 0.0)
    return u, rstd, vhat, vnb, grp, mixed


def _sgu_fwd(proj, gn, wm, bias, name):
    S = proj.shape[0]

    def body(c_ref, g_ref, w_ref, b_ref, o_ref):
        u, _, _, _, _, mixed = _sgu_parts(c_ref[...], g_ref[...], w_ref, b_ref[...])
        o_ref[...] = (u * mixed).astype(BF16)

    return pl.pallas_call(
        body, out_shape=jax.ShapeDtypeStruct((S, SGU_W), BF16), grid=(S // SGU_CHUNK,),
        in_specs=[pl.BlockSpec((SGU_CHUNK, 2 * SGU_W), lambda i: (i, P_C // (2 * SGU_W))),
                  pl.BlockSpec((1, SGU_W), lambda i: (0, 0)), pl.BlockSpec((4, SGU_CHUNK, SGU_CHUNK), lambda i: (0, 0, 0)),
                  pl.BlockSpec((SGU_CHUNK, SGU_W), lambda i: (0, 0))],
        out_specs=pl.BlockSpec((SGU_CHUNK, SGU_W), lambda i: (i, 0)),
        compiler_params=_cp(("parallel",)), name=name)(proj, gn, wm, bias)


def _sgu_bwd(dsg, proj, gn, wm, wmt, bias, name):
    S = proj.shape[0]

    def body(dsg_ref, c_ref, g_ref, w_ref, wt_ref, b_ref, dc_ref, dw_ref, db_ref, dg_ref):
        i = pl.program_id(0)

        @pl.when(i == 0)
        def _():
            dw_ref[...] = jnp.zeros_like(dw_ref)
            db_ref[...] = jnp.zeros_like(db_ref)
            dg_ref[...] = jnp.zeros_like(dg_ref)

        c = c_ref[...]
        gn_v = g_ref[...]
        u, rstd, vhat, vnb, grp, mixed = _sgu_parts(c, gn_v, w_ref, b_ref[...])
        dsg_v = dsg_ref[...]
        du = dsg_v * mixed
        dmix = dsg_v * u
        db_ref[...] += dmix
        dmb = dmix.astype(BF16)
        dvn = jnp.zeros((SGU_CHUNK, SGU_W), F32)
        for gi in range(4):
            dmg = jnp.where(grp == gi, dmb, jnp.zeros_like(dmb))
            dw_ref[gi] += lax.dot_general(dmg, vnb, (((1,), (1,)), ((), ())), preferred_element_type=F32)
            dvn = dvn + jnp.where(grp == gi, jnp.dot(wt_ref[gi], dmb, preferred_element_type=F32), 0.0)
        dg_ref[...] += jnp.sum(dvn * vhat, axis=0, keepdims=True)
        t = dvn * gn_v
        dvv = rstd * (t - vhat * jnp.mean(t * vhat, axis=-1, keepdims=True))
        dc_ref[...] = (jnp.concatenate([du, dvv], axis=1) * _gelu_grad(c)).astype(BF16)

    w_spec = pl.BlockSpec((4, SGU_CHUNK, SGU_CHUNK), lambda i: (0, 0, 0))
    tile = pl.BlockSpec((SGU_CHUNK, SGU_W), lambda i: (0, 0))
    vec = pl.BlockSpec((1, SGU_W), lambda i: (0, 0))
    return pl.pallas_call(
        body, out_shape=(jax.ShapeDtypeStruct((S, 2 * SGU_W), BF16), jax.ShapeDtypeStruct((4, SGU_CHUNK, SGU_CHUNK), F32),
                         jax.ShapeDtypeStruct((SGU_CHUNK, SGU_W), F32), jax.ShapeDtypeStruct((1, SGU_W), F32)),
        grid=(S // SGU_CHUNK,),
        in_specs=[pl.BlockSpec((SGU_CHUNK, SGU_W), lambda i: (i, 0)),
                  pl.BlockSpec((SGU_CHUNK, 2 * SGU_W), lambda i: (i, P_C // (2 * SGU_W))), vec, w_spec, w_spec, tile],
        out_specs=(pl.BlockSpec((SGU_CHUNK, 2 * SGU_W), lambda i: (i, 0)), w_spec, tile, vec),
        compiler_params=_cp(("arbitrary",)), name=name)(dsg, proj, gn, wm, wmt, bias)


def _sigmoid(z):
    return 1.0 / (1.0 + jnp.exp(-z))


def _merge_specs(tm):
    row = lambda n: pl.BlockSpec((tm, n), lambda i: (i, 0))
    gate = lambda b: pl.BlockSpec((tm, D), lambda i: (i, b))
    full = lambda r, c: pl.BlockSpec((r, c), lambda i: (0, 0))
    packed = pl.BlockSpec((4, 256, PACK_COLS), lambda i: (0, R_BRANCH // 256, 0))
    return row, gate, full, packed


def _branch_shards(c_ref, j):
    return c_ref[j, :, 0:256], c_ref[j, :, 256:512], c_ref[j, :, 512:768], c_ref[j, :, 768:1024]


def _merge_fwd(proj, ya, o, sg, packed_w, bg, name, tm=512):
    S = proj.shape[0]
    row, gate, full, packed = _merge_specs(tm)

    def body(g0, g1, g2, ya_ref, o_ref, sg_ref, c_ref, bg_ref, out_ref):
        yav, ov, sgv = ya_ref[...], o_ref[...], sg_ref[...]
        for j in range(4):
            cols = slice(256 * j, 256 * (j + 1))
            wa, wb0, wb1, wc = _branch_shards(c_ref, j)
            y = (jnp.dot(yav, wa, preferred_element_type=F32),
                 jnp.dot(ov[:, :256], wb0, preferred_element_type=F32) + jnp.dot(ov[:, 256:], wb1, preferred_element_type=F32),
                 jnp.dot(sgv, wc, preferred_element_type=F32))
            acc = jnp.zeros((tm, 256), F32)
            for b, g_ref in enumerate((g0, g1, g2)):
                acc = acc + _sigmoid(g_ref[:, cols] + bg_ref[:, b * D + 256 * j:b * D + 256 * (j + 1)]) * y[b]
            out_ref[:, cols] = acc.astype(BF16)

    return pl.pallas_call(
        body, out_shape=jax.ShapeDtypeStruct((S, D), BF16), grid=(S // tm,),
        in_specs=[gate(0), gate(1), gate(2), row(POOL_W), row(FOX_W), row(SGU_W), packed, full(1, 3 * D)],
        out_specs=row(D), compiler_params=_cp(("parallel",)), name=name)(proj, proj, proj, ya, o, sg, packed_w, bg)


def _merge_bwd(dm, proj, ya, o, sg, packed_w, bg, grads, name, tm=512):
    S = proj.shape[0]
    row, gate, full, packed = _merge_specs(tm)
    tn_dims = (((0,), (0,)), ((), ()))
    nt_dims = (((1,), (1,)), ((), ()))

    def body(dm_ref, g0, g1, g2, ya_ref, o_ref, sg_ref, c_ref, bg_ref, _, dg_ref, dya_ref, do_ref, dsg_ref, dc_ref, dbg_ref, acc):
        i = pl.program_id(0)

        @pl.when(i == 0)
        def _():
            acc[...] = jnp.zeros_like(acc)
            dbg_ref[...] = jnp.zeros_like(dbg_ref)

        yav, ov, sgv = ya_ref[...], o_ref[...], sg_ref[...]
        o0, o1 = ov[:, :256], ov[:, 256:]
        dya = jnp.zeros((tm, POOL_W), F32)
        do0 = jnp.zeros((tm, 256), F32)
        do1 = jnp.zeros((tm, 256), F32)
        dsg = jnp.zeros((tm, SGU_W), F32)
        for j in range(4):
            cols = slice(256 * j, 256 * (j + 1))
            wa, wb0, wb1, wc = _branch_shards(c_ref, j)
            y = (jnp.dot(yav, wa, preferred_element_type=F32),
                 jnp.dot(o0, wb0, preferred_element_type=F32) + jnp.dot(o1, wb1, preferred_element_type=F32),
                 jnp.dot(sgv, wc, preferred_element_type=F32))
            dmv = dm_ref[:, cols]
            dy = []
            for b, g_ref in enumerate((g0, g1, g2)):
                bcols = slice(b * D + 256 * j, b * D + 256 * (j + 1))
                gt = _sigmoid(g_ref[:, cols] + bg_ref[:, bcols])
                dgp = dmv * y[b] * gt * (1.0 - gt)
                dg_ref[:, bcols] = dgp.astype(BF16)
                dbg_ref[:, bcols] += jnp.sum(dgp, axis=0, keepdims=True)
                dy.append((dmv * gt).astype(BF16))
            dya = dya + lax.dot_general(dy[0], wa, nt_dims, preferred_element_type=F32)
            do0 = do0 + lax.dot_general(dy[1], wb0, nt_dims, preferred_element_type=F32)
            do1 = do1 + lax.dot_general(dy[1], wb1, nt_dims, preferred_element_type=F32)
            dsg = dsg + lax.dot_general(dy[2], wc, nt_dims, preferred_element_type=F32)
            acc[j, :, 0:256] += lax.dot_general(yav, dy[0], tn_dims, preferred_element_type=F32)
            acc[j, :, 256:512] += lax.dot_general(o0, dy[1], tn_dims, preferred_element_type=F32)
            acc[j, :, 512:768] += lax.dot_general(o1, dy[1], tn_dims, preferred_element_type=F32)
            acc[j, :, 768:1024] += lax.dot_general(sgv, dy[2], tn_dims, preferred_element_type=F32)
        dya_ref[...] = dya
        do_ref[:, :256] = do0
        do_ref[:, 256:] = do1
        dsg_ref[...] = dsg

        @pl.when(i == pl.num_programs(0) - 1)
        def _():
            dc_ref[...] = acc[...].astype(dc_ref.dtype)

    return pl.pallas_call(
        body, out_shape=(jax.ShapeDtypeStruct((S, 3 * D), BF16), jax.ShapeDtypeStruct((S, POOL_W), F32),
                         jax.ShapeDtypeStruct((S, FOX_W), F32), jax.ShapeDtypeStruct((S, SGU_W), F32),
                         jax.ShapeDtypeStruct(grads.shape, grads.dtype), jax.ShapeDtypeStruct((1, 3 * D), F32)),
        grid=(S // tm,),
        in_specs=[row(D), gate(0), gate(1), gate(2), row(POOL_W), row(FOX_W), row(SGU_W), packed, full(1, 3 * D), ANY],
        out_specs=(row(3 * D), row(POOL_W), row(FOX_W), row(SGU_W), packed, full(1, 3 * D)),
        scratch_shapes=[pltpu.VMEM((4, 256, PACK_COLS), F32)], input_output_aliases={9: 4},
        compiler_params=_cp(("arbitrary",)), name=name)(dm, proj, proj, proj, ya, o, sg, packed_w, bg, grads)


def _xattn_probs(qh, kh):
    s = lax.dot_general(qh, kh, (((1,), (1,)), ((), ())), preferred_element_type=F32) * X_SCALE
    p = jnp.exp(s - jnp.max(s, axis=-1, keepdims=True))
    return p / jnp.sum(p, axis=-1, keepdims=True)


def _xattn_fwd(xq, kv, name, tq=512):
    S = xq.shape[0]
    M = kv.shape[0]

    def body(q_ref, k_ref, v_ref, o_ref):
        for h in range(XH):
            sl = slice(h * XHD, (h + 1) * XHD)
            p = _xattn_probs(q_ref[:, sl], k_ref[:, sl])
            o_ref[:, sl] = jnp.dot(p.astype(BF16), v_ref[:, sl], preferred_element_type=F32).astype(BF16)

    return pl.pallas_call(
        body, out_shape=jax.ShapeDtypeStruct((S, D), BF16), grid=(S // tq,),
        in_specs=[pl.BlockSpec((tq, D), lambda i: (i, 0)), pl.BlockSpec((M, D), lambda i: (0, 0)),
                  pl.BlockSpec((M, D), lambda i: (0, 1))],
        out_specs=pl.BlockSpec((tq, D), lambda i: (i, 0)), compiler_params=_cp(("parallel",)), name=name)(xq, kv, kv)


def _xattn_bwd(xq, kv, do, name, tq=512):
    S = xq.shape[0]
    M = kv.shape[0]

    def body(q_ref, k_ref, v_ref, do_ref, dq_ref, dkv_ref, dk_acc, dv_acc):
        i = pl.program_id(0)

        @pl.when(i == 0)
        def _():
            dk_acc[...] = jnp.zeros_like(dk_acc)
            dv_acc[...] = jnp.zeros_like(dv_acc)

        for h in range(XH):
            sl = slice(h * XHD, (h + 1) * XHD)
            qh, kh, vh, doh = q_ref[:, sl], k_ref[:, sl], v_ref[:, sl], do_ref[:, sl]
            p = _xattn_probs(qh, kh)
            dp = lax.dot_general(doh, vh, (((1,), (1,)), ((), ())), preferred_element_type=F32)
            ds = p * (dp - jnp.sum(p * dp, axis=-1, keepdims=True))
            dsb = (ds * X_SCALE).astype(BF16)
            dq_ref[:, sl] = jnp.dot(dsb, kh, preferred_element_type=F32).astype(BF16)
            dk_acc[:, sl] += lax.dot_general(dsb, qh, (((0,), (0,)), ((), ())), preferred_element_type=F32)
            dv_acc[:, sl] += lax.dot_general(p.astype(BF16), doh, (((0,), (0,)), ((), ())), preferred_element_type=F32)

        @pl.when(i == pl.num_programs(0) - 1)
        def _():
            dkv_ref[:, :D] = dk_acc[...].astype(BF16)
            dkv_ref[:, D:] = dv_acc[...].astype(BF16)

    return pl.pallas_call(
        body, out_shape=(jax.ShapeDtypeStruct((S, D), BF16), jax.ShapeDtypeStruct((M, 2 * D), BF16)), grid=(S // tq,),
        in_specs=[pl.BlockSpec((tq, D), lambda i: (i, 0)), pl.BlockSpec((M, D), lambda i: (0, 0)),
                  pl.BlockSpec((M, D), lambda i: (0, 1)), pl.BlockSpec((tq, D), lambda i: (i, 0))],
        out_specs=(pl.BlockSpec((tq, D), lambda i: (i, 0)), pl.BlockSpec((M, 2 * D), lambda i: (0, 0))),
        scratch_shapes=[pltpu.VMEM((M, D), F32), pltpu.VMEM((M, D), F32)],
        compiler_params=_cp(("arbitrary",)), name=name)(xq, kv, kv, do)


def _adam_math(gv, wv, mv, vv):
    c1 = 1.0 - ADAM_B1 ** ADAM_STEP
    c2 = 1.0 - ADAM_B2 ** ADAM_STEP
    nm = ADAM_B1 * mv + (1.0 - ADAM_B1) * gv
    nv = ADAM_B2 * vv + (1.0 - ADAM_B2) * (gv * gv)
    return -ADAM_LR * ((nm / c1) / (jnp.sqrt(nv / c2) + ADAM_EPS) + ADAM_WD * wv), nm, nv


def _adamw(g, w, m, v, name, block=None):
    if block is None:
        block = (1, 256 if g.shape[1] % 256 == 0 else g.shape[1], g.shape[2])
    grid = tuple(s // b for s, b in zip(g.shape, block))

    def body(g_ref, w_ref, m_ref, v_ref, d_ref, nm_ref, nv_ref):
        d_ref[...], nm_ref[...], nv_ref[...] = _adam_math(g_ref[...], w_ref[...], m_ref[...], v_ref[...])

    blk = pl.BlockSpec(block, lambda a, b, c: (a, b, c))
    return pl.pallas_call(
        body, out_shape=(jax.ShapeDtypeStruct(g.shape, F32),) * 3, grid=grid,
        in_specs=[blk] * 4, out_specs=(blk,) * 3, compiler_params=_cp(("parallel",) * 3), name=name)(g, w, m, v)


def _adamw_packed(red, w, m, v, g_index, name, token, tr=256):
    L, r, c = w.shape
    tr = min(tr, r)

    def body(g0_ref, g1_ref, w_ref, m_ref, v_ref, _, g_ref, d_ref, nm_ref, nv_ref):
        gv = jnp.where(pl.program_id(0) == 0, g0_ref[...], g1_ref[...])
        g_ref[0] = gv
        d_ref[0], nm_ref[0], nv_ref[0] = _adam_math(gv, w_ref[0], m_ref[0], v_ref[0])

    gblk = pl.BlockSpec((tr, c), lambda l, i: g_index(i))
    blk = pl.BlockSpec((1, tr, c), lambda l, i: (l, i, 0))
    return pl.pallas_call(
        body, out_shape=(jax.ShapeDtypeStruct(w.shape, F32),) * 4, grid=(L, r // tr),
        in_specs=[gblk, gblk, blk, blk, blk, pl.BlockSpec((8, 128), lambda l, i: (0, 0))], out_specs=(blk,) * 4,
        compiler_params=_cp(("parallel", "parallel")), name=name)(red[0], red[1], w, m, v, token)


def _row_tile(R):
    return next((t for t in (512, 496, 384, 256) if R % t == 0), R)


def _sum_slots(a, out_dtype, name):
    n, R, C = a.shape
    tr = _row_tile(R)

    def body(a_ref, o_ref):
        acc = a_ref[0].astype(F32)
        for k in range(1, n):
            acc = acc + a_ref[k].astype(F32)
        o_ref[...] = acc.astype(out_dtype)

    return pl.pallas_call(
        body, out_shape=jax.ShapeDtypeStruct((R, C), out_dtype), grid=(R // tr,),
        in_specs=[pl.BlockSpec((n, tr, C), lambda i: (0, i, 0))], out_specs=pl.BlockSpec((tr, C), lambda i: (i, 0)),
        compiler_params=_cp(("parallel",)), name=name)(a)


LANDING = pl.BlockSpec(memory_space=pltpu.VMEM)


def _landing_params(shape, dtype):
    return pltpu.CompilerParams(vmem_limit_bytes=math.prod(shape) * jnp.dtype(dtype).itemsize + 4 * 1024 * 1024)


def _place():
    return lax.axis_index("x"), lax.axis_index("y"), lax.axis_index("c")


def _other_chips(x, y):
    return [(1 - x, y), (x, 1 - y), (1 - x, 1 - y)]


def _row_chunks(rows, want, align=16):
    n = want
    while n > 1 and rows % (n * align):
        n -= 1
    return n


def _pair_add(g, name, nch=5):
    n, R, C = g.shape
    half = R // 2
    nch = _row_chunks(half, nch)
    cr = half // nch
    rb = next(t for t in (512, 256, 128, 64, 32, 16) if half % t == 0)

    def body(g_ref, p_ref, got, send_sems, recv_sems, local_sem):
        x, y, c = _place()
        mine0 = pl.multiple_of(c * half, 16)
        theirs0 = (1 - c) * half
        keep = pltpu.make_async_copy(g_ref.at[:, pl.ds(mine0, half), :], p_ref, local_sem)
        keep.start()
        cps = []
        for s in range(n):
            for q in range(nch):
                src = g_ref.at[s, pl.ds(pl.multiple_of(theirs0 + q * cr, 16), cr), :]
                cps.append(pltpu.make_async_remote_copy(
                    src_ref=src, dst_ref=got.at[s, pl.ds(q * cr, cr), :], send_sem=send_sems.at[s * nch + q],
                    recv_sem=recv_sems.at[s * nch + q], device_id=(x, y, 1 - c), device_id_type=MESH))
        for cp in cps:
            cp.start()
        for cp in cps:
            cp.wait()
        keep.wait()

        def add(i, _):
            rows = pl.ds(pl.multiple_of(i * rb, rb), rb)
            for s in range(n):
                p_ref[s, rows, :] = (p_ref[s, rows, :].astype(F32) + got[s, rows, :].astype(F32)).astype(BF16)
            return 0

        lax.fori_loop(0, half // rb, add, 0)

    shape = (n, half, C)
    return pl.pallas_call(
        body, out_shape=jax.ShapeDtypeStruct(shape, g.dtype), in_specs=[ANY], out_specs=LANDING,
        scratch_shapes=[pltpu.VMEM(shape, g.dtype), pltpu.SemaphoreType.DMA((n * nch,)), pltpu.SemaphoreType.DMA((n * nch,)),
                        pltpu.SemaphoreType.DMA],
        compiler_params=_landing_params((2,) + shape, g.dtype), name=name)(g)


def _pair_gather(t, name, nch=10):
    R = t.shape[0]
    nch = _row_chunks(R, nch, 8)
    cr = R // nch

    def body(t_ref, o_ref, send_sems, recv_sems, local_sem):
        x, y, c = _place()
        own = pltpu.make_async_copy(t_ref, o_ref.at[c], local_sem)
        own.start()
        cps = [pltpu.make_async_remote_copy(src_ref=t_ref.at[pl.ds(q * cr, cr), :], dst_ref=o_ref.at[c, pl.ds(q * cr, cr), :],
                                            send_sem=send_sems.at[q], recv_sem=recv_sems.at[q], device_id=(x, y, 1 - c),
                                            device_id_type=MESH) for q in range(nch)]
        for cp in cps:
            cp.start()
        for cp in cps:
            cp.wait()
        own.wait()

    return pl.pallas_call(
        body, out_shape=jax.ShapeDtypeStruct((2,) + t.shape, t.dtype), in_specs=[ANY], out_specs=LANDING,
        scratch_shapes=[pltpu.SemaphoreType.DMA((nch,)), pltpu.SemaphoreType.DMA((nch,)), pltpu.SemaphoreType.DMA],
        compiler_params=_landing_params((2,) + t.shape, t.dtype), name=name)(t)


HBM = pl.BlockSpec(memory_space=pltpu.HBM)
SEM = pl.BlockSpec(memory_space=pltpu.SEMAPHORE)
SPLIT_COPY = pltpu.CompilerParams(has_side_effects=pltpu.SideEffectType.DATAFLOW_SIDE_EFFECTING)


def _split_exchange(src, rows, src_of, tag, nch=5):
    C = src.shape[-1]
    nch = _row_chunks(rows, nch)
    cr = rows // nch
    n = 3 * nch
    land_shape = (4, rows, C)

    def copies(src_ref, land_ref, send_sems, recv_sems):
        x, y, c = _place()
        j = 2 * x + y
        out = []
        for q in range(nch):
            for k, (px, py) in enumerate(_other_chips(x, y)):
                out.append(pltpu.make_async_remote_copy(
                    src_ref=src_of(src_ref, px, py, c, q * cr, cr), dst_ref=land_ref.at[j, pl.ds(q * cr, cr), :],
                    send_sem=send_sems.at[k * nch + q], recv_sem=recv_sems.at[k * nch + q], device_id=(px, py, c),
                    device_id_type=MESH))
        return out

    def start(src_ref, land_ref, send_sems, recv_sems, src_thru, land_thru, token):
        for cp in copies(src_ref, land_ref, send_sems, recv_sems):
            cp.start()
        token[...] = jnp.zeros_like(token)

    send_sems, recv_sems, src_thru, land_thru, token = pl.pallas_call(
        start, name=f"{tag}_start",
        out_shape=(pltpu.SemaphoreType.DMA((n,)), pltpu.SemaphoreType.DMA((n,)), pltpu.HBM(src.shape, src.dtype),
                   pltpu.HBM(land_shape, src.dtype), jax.ShapeDtypeStruct((8, 128), F32)),
        in_specs=(HBM, HBM), out_specs=(SEM, SEM, HBM, HBM, pl.BlockSpec(memory_space=pltpu.VMEM)),
        input_output_aliases={0: 2, 1: 3}, compiler_params=SPLIT_COPY)(
            pltpu.with_memory_space_constraint(src, pltpu.HBM),
            pltpu.with_memory_space_constraint(lax.empty(land_shape, src.dtype), pltpu.HBM))

    def finish(after):
        def wait(src_ref, land_ref, send_sems, recv_sems, after_ref, src_dead, got_ref):
            for cp in copies(src_ref, land_ref, send_sems, recv_sems):
                cp.wait_send()
                cp.wait_recv()

        return pl.pallas_call(
            wait, name=f"{tag}_wait", out_shape=(pltpu.HBM(src.shape, src.dtype), pltpu.HBM(land_shape, src.dtype)),
            in_specs=(HBM, HBM, SEM, SEM, ANY), out_specs=(HBM, HBM), input_output_aliases={0: 0, 1: 1},
            compiler_params=SPLIT_COPY)(src_thru, land_thru, send_sems, recv_sems, after)

    return token, finish


def _gather_finish(shard, land, name, nch=5):
    R, C = shard.shape
    half = R // 2
    nch = _row_chunks(half, nch)
    cr = half // nch

    def body(s_ref, l_ref, o_ref, send_sems, recv_sems, local_sems):
        x, y, c = _place()
        j = 2 * x + y
        mine0 = c * half
        local = [pltpu.make_async_copy(s_ref, o_ref.at[j], local_sems.at[0])]
        remote = []
        for k, (px, py) in enumerate(_other_chips(x, y)):
            jj = 2 * px + py
            local.append(pltpu.make_async_copy(l_ref.at[jj], o_ref.at[jj, pl.ds(pl.multiple_of(mine0, 16), half), :],
                                               local_sems.at[1 + k]))
            for q in range(nch):
                remote.append(pltpu.make_async_remote_copy(
                    src_ref=l_ref.at[jj, pl.ds(q * cr, cr), :],
                    dst_ref=o_ref.at[jj, pl.ds(pl.multiple_of(mine0 + q * cr, 16), cr), :], send_sem=send_sems.at[k * nch + q],
                    recv_sem=recv_sems.at[k * nch + q], device_id=(x, y, 1 - c), device_id_type=MESH))
        for cp in local + remote:
            cp.start()
        for cp in remote + local:
            cp.wait()

    return pl.pallas_call(
        body, out_shape=jax.ShapeDtypeStruct((4, R, C), shard.dtype), in_specs=[ANY, ANY], out_specs=LANDING,
        scratch_shapes=[pltpu.SemaphoreType.DMA((3 * nch,)), pltpu.SemaphoreType.DMA((3 * nch,)), pltpu.SemaphoreType.DMA((4,))],
        compiler_params=_landing_params((4, R, C), shard.dtype), name=name)(shard, land)


def _sum_slots_own(land, own, name):
    n, R, C = land.shape
    tr = _row_tile(R)
    me = (2 * lax.axis_index("x") + lax.axis_index("y")).astype(jnp.int32).reshape(1)
    if own.ndim == 3:
        own_spec = pl.BlockSpec((None, tr, C), lambda i, me: (me[0], i, 0))
    else:
        own_spec = pl.BlockSpec((tr, C), lambda i, me: (i, 0))

    def body(me_ref, land_ref, own_ref, o_ref):
        acc = None
        for k in range(n):
            v = jnp.where(me_ref[0] == k, own_ref[...], land_ref[k]).astype(F32)
            acc = v if acc is None else acc + v
        o_ref[...] = acc

    return pl.pallas_call(
        body, out_shape=jax.ShapeDtypeStruct((R, C), F32),
        grid_spec=pltpu.PrefetchScalarGridSpec(
            num_scalar_prefetch=1, grid=(R // tr,),
            in_specs=[pl.BlockSpec((n, tr, C), lambda i, me: (0, i, 0)), own_spec],
            out_specs=pl.BlockSpec((tr, C), lambda i, me: (i, 0))),
        compiler_params=_cp(("parallel",)), name=name)(me, land, own)


def _reduce_begin(g, tag):
    p = _pair_add(g, f"rs_pair_{tag}")
    token, finish = _split_exchange(p, p.shape[1], lambda ref, px, py, c, r0, cr: ref.at[2 * px + py, pl.ds(r0, cr), :],
                                    f"rs_a2a_{tag}")
    return (finish, g.shape, tag), token


def _reduce_end(state, after):
    finish, shape, tag = state
    p, land = finish(after)
    t = _sum_slots_own(land, p, f"rs_sum_{tag}")
    return _pair_gather(t, f"rs_join_{tag}").reshape(shape[1], shape[2])


def _all_reduce_begin(v, tag):
    p = _sum_slots(_pair_gather(v, f"ar_pair_{tag}"), F32, f"ar_add_{tag}")
    token, finish = _split_exchange(p, p.shape[0], lambda ref, px, py, c, r0, cr: ref.at[pl.ds(r0, cr), :], f"ar_a2a_{tag}")
    return (finish, tag), token


def _all_reduce_end(state, after):
    finish, tag = state
    p, land = finish(after)
    return _sum_slots_own(land, p, f"ar_sum_{tag}")


def _gather_begin(shard, tag):
    half = shard.shape[0] // 2
    token, finish = _split_exchange(
        shard, half, lambda ref, px, py, c, r0, cr: ref.at[pl.ds(pl.multiple_of(c * half + r0, 16), cr), :], f"gather_{tag}")
    return (finish, tag), token


def _gather_end(state, after):
    finish, tag = state
    shard, land = finish(after)
    return _gather_finish(shard, land, f"gather_{tag}_finish")


R_BRANCH, R_OUT, R_WIN, ROWS_A = 0, 256, 512, 1888
R_FF1, R_FF2, R_XKV, R_XQ, R_XO, ROWS_B = 0, 1024, 2048, 2560, 2816, 3072
WIN_ROWS = N_IN // 4


def _w_in_t(a):
    return jnp.transpose(a, (2, 0, 1))


def _pack_shard(w, l):
    xkv, wb = w['w_xkv'][l], w['w_branch_b'][l]
    a = [jnp.concatenate([w['w_branch_a'][l], wb[:256], wb[256:], w['w_branch_c'][l]], axis=1), w['w_out'][l],
         jnp.pad(_w_in_t(w['w_in'])[:, l, :], ((0, ROWS_A - R_WIN - WIN_ROWS), (0, 0)))]
    b = [w['w_ff1'][l], w['w_ff2'][l], jnp.concatenate([xkv[:512], xkv[512:]], axis=1), w['w_xq'][l], w['w_xo'][l]]
    return jnp.concatenate(a, axis=0).astype(BF16), jnp.concatenate(b, axis=0).astype(BF16)


def _w_in_rows(gathered):
    t = gathered[:, R_WIN:R_WIN + WIN_ROWS, :].reshape(N_IN, PACK_COLS)
    return jnp.concatenate([t[2312:5384], t[256:1792], t[1800:2312], t[0:256],
                            jnp.pad(t[1792:1800], ((0, NP - P_F - 8), (0, 0)))], axis=0)


def _w_in_grad_rows(grads, dwt):
    t = jnp.concatenate([dwt[P_A:P_A + 256], dwt[P_Q:P_Q + 1536], dwt[P_F:P_F + 8], dwt[P_C:P_C + 512], dwt[P_G:P_G + 3072]],
                        axis=0)
    for j in range(4):
        rows = t[j * WIN_ROWS:(j + 1) * WIN_ROWS][None].astype(grads.dtype)
        grads = lax.dynamic_update_slice(grads, rows, (j, R_WIN, 0))
    return grads


def _small_prep(sw, l):
    eye = jnp.eye(4, dtype=F32)
    bd = jnp.einsum('gh,gcd->gchd', eye, sw['pool_w'][l]).reshape(POOL_W, POOL_W).astype(BF16)
    tril = jnp.tril(jnp.ones((SGU_CHUNK, SGU_CHUNK), F32))
    wm = (sw['sgu_w'][l] * tril[None]).astype(BF16)
    return dict(
        g_mix=sw['norm_mix_g'][l][None], g_x=sw['norm_xattn_g'][l][None], g_mem=sw['norm_mem_g'][l][None],
        g_ffn=sw['norm_ffn_g'][l][None], bd=bd, pool_scale=sw['pool_scale'][l][None],
        bf=jnp.pad(sw['b_forget'][l], (0, FCOLS - 8))[None], sgu_g=sw['sgu_norm_g'][l][None], wm=wm,
        wmt=jnp.transpose(wm, (0, 2, 1)), sgu_bias=jnp.repeat(sw['sgu_b'][l].T, 64, axis=1), bg=sw['b_gate'][l][None])


def _rows4(r0):
    return dict(n=D, k=D, tn=D, b_block=(4, 256, PACK_COLS), b_index=lambda i, j, k: (0, r0 // 256, 0))


def _rows_t(r0):
    return dict(tb=True, n=D, k=D, tn=D, b_block=(4, 256, PACK_COLS), b_index=lambda i, j, k: (0, r0 // 256, 0))


def _rows_grad(r0):
    return dict(ta=True, tm=D, tn=512, o_block=(4, 256, 512), o_index=lambda i, j, k: (0, r0 // 256, j))


def _add_to(r, e):
    return e + r


def _after(v, token):
    return v if token is None else v + token[0, 0]


def _layer_fwd(x, mem, GA, w_in_t, sp, l, token, second):
    t = f"l{l}"
    S = x.shape[0]
    h = _rms_fwd(x, _after(sp['g_mix'], token), f"rms_mix_{t}")
    proj = _mm(h, w_in_t, name=f"proj_{t}", out_dtype=F32, tb=True)
    d, ya = _pool_fwd(proj, sp['bd'], sp['pool_scale'], f"pool_fwd_{t}")
    fcum = _fgate_fwd(proj, sp['bf'], f"fgate_fwd_{t}")
    f8 = fcum[:, :8]
    fcol = f8.reshape(S, 4, 2).transpose(1, 0, 2)
    frow = f8.T.reshape(4, 2, S)
    qkv = proj[:, P_Q:P_Q + 3 * FOX_W].astype(BF16)
    o, o32, lse = _fox_fwd(qkv, fcol, frow, f"fox_fwd_{t}")
    sg = _sgu_fwd(proj, sp['sgu_g'], sp['wm'], sp['sgu_bias'], f"sgu_fwd_{t}")
    merged = _merge_fwd(proj, ya, o, sg, GA, sp['bg'], f"merge_fwd_{t}")
    x1 = _mm(merged, GA, name=f"out_{t}", out_dtype=F32, extra=x, epi=_add_to, **_rows4(R_OUT))
    GB, token = second(x1)
    hx = _rms_fwd(x1, _after(sp['g_x'], token), f"rms_x_{t}")
    hm = _rms_fwd(mem, sp['g_mem'], f"rms_mem_{t}")
    xq = _mm(hx, GB, name=f"xq_{t}", out_dtype=BF16, **_rows4(R_XQ))
    kv = _mm(hm, GB, name=f"xkv_{t}", out_dtype=BF16, n=2 * D, k=D, tn=512, tk=512, b_block=(None, 512, 512),
             b_index=lambda i, j, k: (j, R_XKV // 512, k))
    o2 = _xattn_fwd(xq, kv, f"xattn_fwd_{t}")
    x2 = _mm(o2, GB, name=f"xo_{t}", out_dtype=F32, extra=x1, epi=_add_to, **_rows4(R_XO))
    hf = _rms_fwd(x2, sp['g_ffn'], f"rms_ffn_{t}")
    z = _mm(hf, GB, name=f"ff1_{t}", out_dtype=BF16, n=D_FF, k=D, tn=D, b_block=(None, 1024, PACK_COLS),
            b_index=lambda i, j, k: (j, R_FF1 // 1024, 0))
    x3 = _mm(z, GB, name=f"ff2_{t}", out_dtype=F32, a_fn=_relu2, extra=x2, epi=_add_to, n=D, k=D_FF, tk=1024, tn=D,
             b_block=(None, 1024, PACK_COLS), b_index=lambda i, j, k: (k, R_FF2 // 1024, 0))
    saved = dict(x=x, h=h, proj=proj, d=d, ya=ya, fcol=fcol, frow=frow, qkv=qkv, o=o, o32=o32, lse=lse, sg=sg, merged=merged,
                 x1=x1, hx=hx, hm=hm, xq=xq, kv=kv, o2=o2, x2=x2, hf=hf, z=z, GA=GA, GB=GB, w_in_t=w_in_t)
    return x3, saved


def _layer_bwd(dx3, mem, sp, sv, l, token, early):
    t = f"l{l}"
    S = dx3.shape[0]
    GA, GB = sv['GA'], sv['GB']
    gs = {}
    dx3 = _after(dx3, token)
    gb = lax.empty((4, ROWS_B, PACK_COLS), BF16)
    dz = _mm(dx3, GB, name=f"d_a2_{t}", out_dtype=BF16, tb=True, n=D_FF, k=D, tn=D, b_block=(None, 1024, PACK_COLS),
             b_index=lambda i, j, k: (j, R_FF2 // 1024, 0), extra=sv['z'],
             epi=lambda r, e: r * (2.0 * jnp.maximum(e.astype(F32), 0.0)))
    gb = _mm(sv['z'], dx3, name=f"dw_ff2_{t}", out_dtype=BF16, ta=True, a_fn=_relu2, into=gb, tm=1024, tn=D,
             o_block=(None, 1024, PACK_COLS), o_index=lambda i, j, k: (i, R_FF2 // 1024, 0))
    gb = _mm(sv['hf'], dz, name=f"dw_ff1_{t}", out_dtype=BF16, ta=True, into=gb, tm=1024, tn=D,
             o_block=(None, 1024, PACK_COLS), o_index=lambda i, j, k: (j, R_FF1 // 1024, 0))
    dx2, gs['norm_ffn_g'] = _mm(dz, GB, name=f"d_hf_{t}", out_dtype=F32, tb=True, n=D, k=D_FF, tm=1024, tn=D, tk=1024,
                                b_block=(None, 1024, PACK_COLS), b_index=lambda i, j, k: (k, R_FF1 // 1024, 0),
                                norm_bwd=(sv['x2'], sp['g_ffn'], dx3), vmem_limit=WIDE_VMEM_LIMIT)
    do2 = _mm(dx2, GB, name=f"d_o2_{t}", out_dtype=BF16, **_rows_t(R_XO))
    gb = _mm(sv['o2'], dx2, name=f"dw_xo_{t}", out_dtype=BF16, into=gb, **_rows_grad(R_XO))
    dxq, dkv = _xattn_bwd(sv['xq'], sv['kv'], do2, f"xattn_bwd_{t}")
    gb = _mm(sv['hm'], dkv, name=f"dw_xkv_{t}", out_dtype=BF16, ta=True, into=gb, tm=512, tn=512,
             o_block=(None, 512, 512), o_index=lambda i, j, k: (j, R_XKV // 512, i))
    dhm = _mm(dkv, GB, name=f"d_hm_{t}", out_dtype=F32, tb=True, n=D, k=2 * D, tn=512, tk=512, b_block=(None, 512, 512),
              b_index=lambda i, j, k: (k, R_XKV // 512, j))
    gs['norm_mem_g'] = _rms_bwd(dhm, mem, sp['g_mem'], None, f"rms_mem_bwd_{t}")
    gb = _mm(sv['hx'], dxq, name=f"dw_xq_{t}", out_dtype=BF16, into=gb, **_rows_grad(R_XQ))
    token = early(gb)
    dx1, gs['norm_xattn_g'] = _mm(dxq, GB, name=f"d_hx_{t}", out_dtype=F32, tm=512, **_rows_t(R_XQ),
                                  norm_bwd=(sv['x1'], _after(sp['g_x'], token), dx2))
    ga = jnp.zeros((4, ROWS_A, PACK_COLS), BF16)
    ga = _mm(sv['merged'], dx1, name=f"dw_out_{t}", out_dtype=BF16, into=ga, **_rows_grad(R_OUT))
    dm = _mm(dx1, GA, name=f"d_merged_{t}", out_dtype=F32, **_rows_t(R_OUT))
    dg, dya, do, dsg, ga, gs['b_gate'] = _merge_bwd(dm, sv['proj'], sv['ya'], sv['o'], sv['sg'], GA, sp['bg'], ga, f"merge_bwd_{t}")
    dc, dws, dbias, gs['sgu_norm_g'] = _sgu_bwd(dsg, sv['proj'], sp['sgu_g'], sp['wm'], sp['wmt'], sp['sgu_bias'], f"sgu_bwd_{t}")
    tril = jnp.tril(jnp.ones((SGU_CHUNK, SGU_CHUNK), F32))
    gs['sgu_w'] = dws * tril[None]
    gs['sgu_b'] = dbias.reshape(SGU_CHUNK, 4, 64).sum(-1).T
    dq, dk, dv, dfrow, dfcol = _fox_bwd(sv['qkv'], sv['o32'], do, sv['lse'], sv['fcol'], sv['frow'], f"fox_bwd_{t}")
    dF = jnp.pad(dfrow.reshape(8, S).T + dfcol.transpose(1, 0, 2).reshape(S, 8), ((0, 0), (0, FCOLS - 8)))
    df, dbf = _fgate_bwd(dF, sv['proj'], sp['bf'], f"fgate_bwd_{t}")
    gs['b_forget'] = dbf[:, :8]
    da, dbd, gs['pool_scale'] = _pool_bwd(dya, sv['d'], sp['bd'], sp['pool_scale'], f"pool_bwd_{t}")
    gs['pool_w'] = jnp.stack([dbd[g * 64:(g + 1) * 64, g * 64:(g + 1) * 64] for g in range(4)])
    dproj = jnp.concatenate([dg, dq, dk, dv, dc, da, df], axis=1)
    dwt = _mm(dproj, sv['h'], name=f"dw_in_{t}", out_dtype=BF16, ta=True, tm=512, tn=1024)
    ga = _w_in_grad_rows(ga, dwt)
    dx, gs['norm_mix_g'] = _mm(dproj, sv['w_in_t'], name=f"d_h_{t}", out_dtype=F32, tm=1024, tk=512, tn=D,
                               norm_bwd=(sv['x'], sp['g_mix'], dx1), vmem_limit=WIDE_VMEM_LIMIT)
    return dx, ga, gs


SMALL_ROWS = 1424
GRAD_BLOCKS = {
    'w_ff1': ('b', lambda i: (R_FF1 // 256 + i, 0)), 'w_ff2': ('b', lambda i: (R_FF2 // 256 + i, 0)),
    'w_xq': ('b', lambda i: (R_XQ // 256 + i, 0)), 'w_xo': ('b', lambda i: (R_XO // 256 + i, 0)),
    'w_xkv': ('b', lambda i: (R_XKV // 256 + i % 2, i // 2)), 'w_out': ('a', lambda i: (R_OUT // 256 + i, 0)),
    'w_branch_a': ('a', lambda i: (R_BRANCH // 256, 0)), 'w_branch_b': ('a', lambda i: (R_BRANCH // 256, 1 + i)),
    'w_branch_c': ('a', lambda i: (R_BRANCH // 256, 3)),
}


def _pack_small(parts):
    flat = jnp.concatenate([p.reshape(-1) for p in parts])
    return jnp.pad(flat, (0, SMALL_ROWS * 128 - flat.shape[0])).reshape(SMALL_ROWS, 128)


def _unpack_small(buf, shapes):
    flat, out, r = buf.reshape(-1), [], 0
    for s in shapes:
        n = math.prod(s)
        out.append(flat[r:r + n].reshape(s))
        r += n
    return out


def kernel(x, mem, norm_mix_g, w_in, b_forget, pool_w, pool_scale, sgu_norm_g, sgu_w, sgu_b, w_branch_a, w_branch_b, w_branch_c, b_gate, w_out, norm_xattn_g, norm_mem_g, w_xq, w_xkv, w_xo, norm_ffn_g, w_ff1, w_ff2, final_norm_g, loss_target, m_norm_mix_g, m_w_in, m_b_forget, m_pool_w, m_pool_scale, m_sgu_norm_g, m_sgu_w, m_sgu_b, m_w_branch_a, m_w_branch_b, m_w_branch_c, m_b_gate, m_w_out, m_norm_xattn_g, m_norm_mem_g, m_w_xq, m_w_xkv, m_w_xo, m_norm_ffn_g, m_w_ff1, m_w_ff2, m_final_norm_g, v_norm_mix_g, v_w_in, v_b_forget, v_pool_w, v_pool_scale, v_sgu_norm_g, v_sgu_w, v_sgu_b, v_w_branch_a, v_w_branch_b, v_w_branch_c, v_b_gate, v_w_out, v_norm_xattn_g, v_norm_mem_g, v_w_xq, v_w_xkv, v_w_xo, v_norm_ffn_g, v_w_ff1, v_w_ff2, v_final_norm_g):
    args = (norm_mix_g, w_in, b_forget, pool_w, pool_scale, sgu_norm_g, sgu_w, sgu_b, w_branch_a, w_branch_b, w_branch_c, b_gate,
            w_out, norm_xattn_g, norm_mem_g, w_xq, w_xkv, w_xo, norm_ffn_g, w_ff1, w_ff2, final_norm_g)
    margs = (m_norm_mix_g, m_w_in, m_b_forget, m_pool_w, m_pool_scale, m_sgu_norm_g, m_sgu_w, m_sgu_b, m_w_branch_a, m_w_branch_b,
             m_w_branch_c, m_b_gate, m_w_out, m_norm_xattn_g, m_norm_mem_g, m_w_xq, m_w_xkv, m_w_xo, m_norm_ffn_g, m_w_ff1, m_w_ff2,
             m_final_norm_g)
    vargs = (v_norm_mix_g, v_w_in, v_b_forget, v_pool_w, v_pool_scale, v_sgu_norm_g, v_sgu_w, v_sgu_b, v_w_branch_a, v_w_branch_b,
             v_w_branch_c, v_b_gate, v_w_out, v_norm_xattn_g, v_norm_mem_g, v_w_xq, v_w_xkv, v_w_xo, v_norm_ffn_g, v_w_ff1, v_w_ff2,
             v_final_norm_g)
    w = dict(zip(W_NAMES, args))
    mo = dict(zip(W_NAMES, margs))
    vo = dict(zip(W_NAMES, vargs))
    xs, mems, tgt = x[0], mem[0], loss_target[0]
    shards = [_pack_shard(w, l) for l in range(DEPTH)]
    preps = [_small_prep(w, l) for l in range(DEPTH)]

    first_a, _ = _gather_begin(shards[0][0], "a_l0")
    pending_b, token = _gather_begin(shards[0][1], "b_l0")
    GA = None
    act, saved = xs, []
    for l in range(DEPTH):
        nxt = {}
        if l + 1 < DEPTH:
            nxt['a'], ta = _gather_begin(shards[l + 1][0], f"a_l{l + 1}")
            token = ta if token is None else token + ta
        if l == 0:
            GA = _gather_end(first_a, shards[DEPTH - 1][1])

        def second(x1, l=l, pending_b=pending_b, nxt=nxt):
            GB = _gather_end(pending_b, x1)
            if l + 1 == DEPTH:
                return GB, None
            nxt['b'], tb = _gather_begin(shards[l + 1][1], f"b_l{l + 1}")
            return GB, tb

        act, sv = _layer_fwd(act, mems, GA, _w_in_rows(GA), preps[l], l, token, second)
        saved.append(sv)
        if l + 1 < DEPTH:
            GA = _gather_end(nxt['a'], act)
            pending_b, token = nxt['b'], None
    loss_part, dact, d_final_g = _loss_head(act, w['final_norm_g'][None], tgt, "loss_head")

    red_a, red_b, small_g = [None] * DEPTH, [None] * DEPTH, [None] * DEPTH
    token, state_a = None, None
    for l in reversed(range(DEPTH)):
        early = {}

        def start_b(gb, l=l, early=early):
            early['state'], tok = _reduce_begin(gb, f"b_l{l}")
            return tok

        dact, ga, small_g[l] = _layer_bwd(dact, mems, preps[l], saved[l], l, token, start_b)
        if state_a is not None:
            red_a[l + 1] = _reduce_end(state_a, dact)
        red_b[l] = _reduce_end(early['state'], dact)
        state_a, token = _reduce_begin(ga, f"a_l{l}")
    grad_x = dact[None]
    per_layer = [n for n in SMALL_NAMES if n != 'final_norm_g']
    small_shapes = [w[n].shape for n in per_layer] + [(D,), (1,)]
    parts = [jnp.stack([small_g[l][n].reshape(w[n].shape[1:]) for l in range(DEPTH)]) for n in per_layer]
    state_small, token_small = _all_reduce_begin(_pack_small(parts + [d_final_g.reshape(D), loss_part.reshape(1)]), "small")
    token = token + token_small

    grads, delta, new_m, new_v = {}, {}, {}, {}
    for n, (buf, g_index) in GRAD_BLOCKS.items():
        if buf == 'b':
            grads[n], delta[n], new_m[n], new_v[n] = _adamw_packed(red_b, w[n], mo[n], vo[n], g_index, f"adamw_{n}", token)
    red_a[0] = _reduce_end(state_a, new_v['w_xkv'])
    small_red = _unpack_small(_all_reduce_end(state_small, red_a[0]), small_shapes)
    grads.update(zip(per_layer + ['final_norm_g'], small_red[:-1]))
    loss = small_red[-1].reshape(())
    for n, (buf, g_index) in GRAD_BLOCKS.items():
        if buf == 'a':
            grads[n], delta[n], new_m[n], new_v[n] = _adamw_packed(red_a, w[n], mo[n], vo[n], g_index, f"adamw_{n}", token)
    g_t = jnp.stack([r[R_WIN:R_WIN + WIN_ROWS] for r in red_a], axis=1)
    upd = _adamw(g_t, _w_in_t(w['w_in']), _w_in_t(mo['w_in']), _w_in_t(vo['w_in']), "adamw_w_in", block=(WIN_ROWS, DEPTH, 128))
    grads['w_in'], delta['w_in'], new_m['w_in'], new_v['w_in'] = [jnp.transpose(a, (1, 2, 0)) for a in (g_t,) + tuple(upd)]
    small_all = per_layer + ['final_norm_g']
    shapes_all = [w[n].shape for n in small_all]
    packed = [_pack_small([d[n] for n in small_all])[None] for d in (grads, w, mo, vo)]
    ds, ms, vs = _adamw(*packed, "adamw_small")
    for n, a, b, c in zip(small_all, _unpack_small(ds[0], shapes_all), _unpack_small(ms[0], shapes_all), _unpack_small(vs[0], shapes_all)):
        delta[n], new_m[n], new_v[n] = a, b, c

    return (loss, grad_x, *[grads[n] for n in W_NAMES], *[delta[n] for n in W_NAMES], *[new_m[n] for n in W_NAMES],
            *[new_v[n] for n in W_NAMES])
```

```python
import math

import jax
import jax.numpy as jnp
from jax import lax
from jax.experimental import pallas as pl
from jax.experimental.pallas import tpu as pltpu

F32 = jnp.float32
BF16 = jnp.bfloat16

D = 1024
DEPTH = 2
POOL_W = 256
FOX_W = 512
SGU_W = 256
SGU_CHUNK = 128
N_IN = 5384
P_G, P_Q, P_K, P_V, P_C, P_A, P_F = 0, 3072, 3584, 4096, 4608, 5120, 5376
NP = 5632
XH, XHD = 4, 256
D_FF = 4096
EPS = 1e-6
NEG = -1e30
FOX_SCALE = 64 ** -0.5
X_SCALE = 256 ** -0.5
GELU_K = math.sqrt(2.0 / math.pi)
GELU_C = 0.044715

ADAM_LR, ADAM_B1, ADAM_B2, ADAM_EPS, ADAM_WD, ADAM_STEP = 0.001, 0.9, 0.999, 1e-08, 0.01, 10

VMEM_LIMIT = 48 * 1024 * 1024
WIDE_VMEM_LIMIT = 60 * 1024 * 1024
MESH = pl.DeviceIdType.MESH

IN_NAMES = ['x', 'mem', 'norm_mix_g', 'w_in', 'b_forget', 'pool_w', 'pool_scale', 'sgu_norm_g', 'sgu_w', 'sgu_b',
            'w_branch_a', 'w_branch_b', 'w_branch_c', 'b_gate', 'w_out', 'norm_xattn_g', 'norm_mem_g', 'w_xq',
            'w_xkv', 'w_xo', 'norm_ffn_g', 'w_ff1', 'w_ff2', 'final_norm_g']
W_NAMES = IN_NAMES[2:]
BIG_NAMES = ['w_in', 'w_branch_a', 'w_branch_b', 'w_branch_c', 'w_out', 'w_xq', 'w_xkv', 'w_xo', 'w_ff1', 'w_ff2']
SMALL_NAMES = [n for n in W_NAMES if n not in BIG_NAMES]
PACK_COLS = 1024


ANY = pl.BlockSpec(memory_space=pl.ANY)


def _cp(sem=None, vmem_limit=VMEM_LIMIT):
    return pltpu.CompilerParams(dimension_semantics=sem, vmem_limit_bytes=vmem_limit)


def _mm(a, b, *, name, out_dtype, ta=False, tb=False, tm=1024, tn=512, tk=1024, a_fn=None, extra=None, epi=None,
        n=None, k=None, b_block=None, b_index=None, into=None, o_block=None, o_index=None, norm_bwd=None,
        vmem_limit=VMEM_LIMIT):
    M = a.shape[1] if ta else a.shape[0]
    K = k if k is not None else (a.shape[0] if ta else a.shape[1])
    N = n if n is not None else (b.shape[0] if tb else b.shape[1])
    tm, tn, tk = min(tm, M), min(tn, N), min(tk, K)
    assert M % tm == 0 and N % tn == 0 and K % tk == 0, (name, M, N, K)
    nk = K // tk
    a_spec = pl.BlockSpec((tk, tm), lambda i, j, k: (k, i)) if ta else pl.BlockSpec((tm, tk), lambda i, j, k: (i, k))
    if b_block is not None:
        b_spec = pl.BlockSpec(b_block, b_index)
    else:
        b_spec = pl.BlockSpec((tn, tk), lambda i, j, k: (j, k)) if tb else pl.BlockSpec((tk, tn), lambda i, j, k: (k, j))
    dn = (((0 if ta else 1,), (1 if tb else 0,)), ((), ()))
    tile = pl.BlockSpec((tm, tn), lambda i, j, k: (i, j))
    o_spec = pl.BlockSpec(o_block, o_index) if into is not None else tile
    in_specs = [a_spec, b_spec] + ([tile] if extra is not None else []) + ([ANY] if into is not None else [])
    vec = pl.BlockSpec((1, N), lambda i, j, k: (0, 0))
    if norm_bwd is not None:
        assert tn == N and extra is None and into is None, name
        in_specs += [tile, tile, vec]
    n_in = len(in_specs)

    def body(*refs):
        a_ref, b_ref = refs[0], refs[1]
        e_ref = refs[2] if extra is not None else None
        o_ref, acc_ref = refs[n_in], refs[-1]
        kk = pl.program_id(2)
        first_rows = pl.program_id(0) == 0

        @pl.when(kk == 0)
        def _():
            acc_ref[...] = jnp.zeros_like(acc_ref)

        av = a_ref[...]
        if a_fn is not None:
            av = a_fn(av)
        bv = b_ref[...]
        if bv.ndim == 3:
            bv = bv.reshape(-1, bv.shape[-1])
        acc_ref[...] += lax.dot_general(av.astype(BF16), bv.astype(BF16), dn, preferred_element_type=F32)

        @pl.when(kk == nk - 1)
        def _():
            r = acc_ref[...]
            if norm_bwd is not None:
                x_ref, r_ref, g_ref, dg_ref = refs[2], refs[3], refs[4], refs[n_in + 1]
                xv = x_ref[...]
                rstd = lax.rsqrt(jnp.mean(xv * xv, axis=-1, keepdims=True) + EPS)
                xhat = xv * rstd

                @pl.when(first_rows)
                def _():
                    dg_ref[...] = jnp.zeros_like(dg_ref)

                dg_ref[...] += jnp.sum(r * xhat, axis=0, keepdims=True)
                t = r * g_ref[...]
                o_ref[...] = r_ref[...] + rstd * (t - xhat * jnp.mean(t * xhat, axis=-1, keepdims=True))
                return
            if epi is not None:
                r = epi(r, e_ref[...])
            o_ref[...] = r.astype(o_ref.dtype).reshape(o_ref.shape)

    args = (a, b) + ((extra,) if extra is not None else ()) + ((into,) if into is not None else ())
    out_shape = jax.ShapeDtypeStruct(into.shape, into.dtype) if into is not None else jax.ShapeDtypeStruct((M, N), out_dtype)
    semantics = ("parallel", "parallel", "arbitrary")
    if norm_bwd is not None:
        xn, gn, dres = norm_bwd
        args += (xn, dres, gn)
        out_shape = (jax.ShapeDtypeStruct((M, N), F32), jax.ShapeDtypeStruct((1, N), F32))
        o_spec = (tile, vec)
        semantics = ("arbitrary", "arbitrary", "arbitrary")
    return pl.pallas_call(
        body, out_shape=out_shape, grid=(M // tm, N // tn, nk), in_specs=in_specs, out_specs=o_spec,
        scratch_shapes=[pltpu.VMEM((tm, tn), F32)], input_output_aliases={n_in - 1: 0} if into is not None else {},
        compiler_params=_cp(semantics, vmem_limit), name=name)(*args)


def _relu2(z):
    r = jnp.maximum(z.astype(F32), 0.0)
    return r * r


def _rms_fwd(x, g, name, tr=512):
    R, n = x.shape
    tr = min(tr, R)

    def body(x_ref, g_ref, h_ref):
        xv = x_ref[...]
        rstd = lax.rsqrt(jnp.mean(xv * xv, axis=-1, keepdims=True) + EPS)
        h_ref[...] = (xv * rstd * g_ref[...]).astype(BF16)

    return pl.pallas_call(
        body, out_shape=jax.ShapeDtypeStruct((R, n), BF16), grid=(R // tr,),
        in_specs=[pl.BlockSpec((tr, n), lambda i: (i, 0)), pl.BlockSpec((1, n), lambda i: (0, 0))],
        out_specs=pl.BlockSpec((tr, n), lambda i: (i, 0)), compiler_params=_cp(("parallel",)), name=name)(x, g)


def _rms_bwd(dh, x, g, dres, name, tr=512):
    R, n = x.shape
    tr = min(tr, R)
    need_dx = dres is not None

    def body(*refs):
        if need_dx:
            dh_ref, x_ref, g_ref, r_ref, dx_ref, dg_ref = refs
        else:
            dh_ref, x_ref, g_ref, dg_ref = refs
        i = pl.program_id(0)
        xv = x_ref[...]
        dhv = dh_ref[...].astype(F32)
        rstd = lax.rsqrt(jnp.mean(xv * xv, axis=-1, keepdims=True) + EPS)
        xhat = xv * rstd

        @pl.when(i == 0)
        def _():
            dg_ref[...] = jnp.zeros_like(dg_ref)

        dg_ref[...] += jnp.sum(dhv * xhat, axis=0, keepdims=True)
        if need_dx:
            t = dhv * g_ref[...]
            dx_ref[...] = r_ref[...] + rstd * (t - xhat * jnp.mean(t * xhat, axis=-1, keepdims=True))

    row = pl.BlockSpec((tr, n), lambda i: (i, 0))
    vec = pl.BlockSpec((1, n), lambda i: (0, 0))
    if need_dx:
        return pl.pallas_call(
            body, out_shape=(jax.ShapeDtypeStruct((R, n), F32), jax.ShapeDtypeStruct((1, n), F32)), grid=(R // tr,),
            in_specs=[row, row, vec, row], out_specs=(row, vec), compiler_params=_cp(("arbitrary",)), name=name)(dh, x, g, dres)
    return pl.pallas_call(
        body, out_shape=jax.ShapeDtypeStruct((1, n), F32), grid=(R // tr,),
        in_specs=[row, row, vec], out_specs=vec, compiler_params=_cp(("arbitrary",)), name=name)(dh, x, g)


def _loss_head(x, g, tgt, name, tr=512):
    R, n = x.shape

    def body(x_ref, g_ref, t_ref, loss_ref, dx_ref, dg_ref):
        i = pl.program_id(0)
        xv = x_ref[...]
        gv = g_ref[...]
        rstd = lax.rsqrt(jnp.mean(xv * xv, axis=-1, keepdims=True) + EPS)
        xhat = xv * rstd
        e = xhat * gv - t_ref[...]

        @pl.when(i == 0)
        def _():
            loss_ref[...] = jnp.zeros_like(loss_ref)
            dg_ref[...] = jnp.zeros_like(dg_ref)

        loss_ref[...] += 0.5 * jnp.sum(jnp.sum(e * e, axis=-1, keepdims=True) / n, axis=0, keepdims=True)
        dy = e / n
        dg_ref[...] += jnp.sum(dy * xhat, axis=0, keepdims=True)
        t = dy * gv
        dx_ref[...] = rstd * (t - xhat * jnp.mean(t * xhat, axis=-1, keepdims=True))

    row = pl.BlockSpec((tr, n), lambda i: (i, 0))
    vec = pl.BlockSpec((1, n), lambda i: (0, 0))
    one = pl.BlockSpec((1, 1), lambda i: (0, 0))
    return pl.pallas_call(
        body, out_shape=(jax.ShapeDtypeStruct((1, 1), F32), jax.ShapeDtypeStruct((R, n), F32), jax.ShapeDtypeStruct((1, n), F32)),
        grid=(R // tr,), in_specs=[row, vec, row], out_specs=(one, row, vec),
        compiler_params=_cp(("arbitrary",)), name=name)(x, g, tgt)


def _pool_masks(S):
    row = lax.broadcasted_iota(jnp.int32, (S, POOL_W), 0)
    grp = lax.broadcasted_iota(jnp.int32, (S, POOL_W), 1) // 64
    win = jnp.where(grp == 0, 2, jnp.where(grp == 1, 4, jnp.where(grp == 2, 8, 16)))
    cnt = jnp.minimum(row + 1, win).astype(F32)
    return row, grp, cnt


def _by_group(grp, v0, v1, v2, v3):
    return jnp.where(grp == 0, v0, jnp.where(grp == 1, v1, jnp.where(grp == 2, v2, v3)))


def _pool_fwd(proj, bd, scale, name):
    S = proj.shape[0]

    def body(a_ref, bd_ref, sc_ref, d_ref, y_ref):
        a = a_ref[...]
        row, grp, cnt = _pool_masks(S)

        def back(v, k):
            return jnp.where(row >= k, pltpu.roll(v, k, 0), 0.0)

        s1 = a + back(a, 1)
        s2 = s1 + back(s1, 2)
        s3 = s2 + back(s2, 4)
        s4 = s3 + back(s3, 8)
        d = (_by_group(grp, s1, s2, s3, s4) / cnt - a).astype(BF16)
        d_ref[...] = d
        y_ref[...] = (jnp.dot(d, bd_ref[...], preferred_element_type=F32) * sc_ref[...]).astype(BF16)

    full = lambda r, c: pl.BlockSpec((r, c), lambda i: (0, 0))
    return pl.pallas_call(
        body, out_shape=(jax.ShapeDtypeStruct((S, POOL_W), BF16), jax.ShapeDtypeStruct((S, POOL_W), BF16)), grid=(1,),
        in_specs=[pl.BlockSpec((S, POOL_W), lambda i: (0, P_A // POOL_W)), full(POOL_W, POOL_W), full(1, POOL_W)],
        out_specs=(full(S, POOL_W), full(S, POOL_W)), compiler_params=_cp(("arbitrary",)), name=name)(proj, bd, scale)


def _pool_bwd(dya, d, bd, scale, name):
    S = dya.shape[0]

    def body(dy_ref, d_ref, bd_ref, sc_ref, da_ref, dbd_ref, dsc_ref):
        dy = dy_ref[...]
        dv = d_ref[...]
        bdv = bd_ref[...]
        row, grp, cnt = _pool_masks(S)
        yraw = jnp.dot(dv, bdv, preferred_element_type=F32)
        dsc_ref[...] = jnp.sum(dy * yraw, axis=0, keepdims=True)
        tb = (dy * sc_ref[...]).astype(BF16)
        dbd_ref[...] = lax.dot_general(dv, tb, (((0,), (0,)), ((), ())), preferred_element_type=F32)
        dd = lax.dot_general(tb, bdv, (((1,), (1,)), ((), ())), preferred_element_type=F32)
        e = dd / cnt

        def fwd(v, k):
            return jnp.where(row < S - k, pltpu.roll(v, S - k, 0), 0.0)

        r1 = e + fwd(e, 1)
        r2 = r1 + fwd(r1, 2)
        r3 = r2 + fwd(r2, 4)
        r4 = r3 + fwd(r3, 8)
        da_ref[...] = (_by_group(grp, r1, r2, r3, r4) - dd).astype(BF16)

    full = lambda r, c: pl.BlockSpec((r, c), lambda i: (0, 0))
    return pl.pallas_call(
        body, out_shape=(jax.ShapeDtypeStruct((S, POOL_W), BF16), jax.ShapeDtypeStruct((POOL_W, POOL_W), F32),
                         jax.ShapeDtypeStruct((1, POOL_W), F32)), grid=(1,),
        in_specs=[full(S, POOL_W), full(S, POOL_W), full(POOL_W, POOL_W), full(1, POOL_W)],
        out_specs=(full(S, POOL_W), full(POOL_W, POOL_W), full(1, POOL_W)),
        compiler_params=_cp(("arbitrary",)), name=name)(dya, d, bd, scale)


FCOLS = 128


def _log_sigmoid(z):
    return -(jnp.maximum(-z, 0.0) + jnp.log1p(jnp.exp(-jnp.abs(z))))


def _fgate_fwd(proj, bf, name):
    S = proj.shape[0]

    def body(f_ref, b_ref, o_ref):
        v = _log_sigmoid(f_ref[...] + b_ref[...])
        row = lax.broadcasted_iota(jnp.int32, (S, FCOLS), 0)
        k = 1
        while k < S:
            v = v + jnp.where(row >= k, pltpu.roll(v, k, 0), 0.0)
            k *= 2
        o_ref[...] = v

    return pl.pallas_call(
        body, out_shape=jax.ShapeDtypeStruct((S, FCOLS), F32), grid=(1,),
        in_specs=[pl.BlockSpec((S, FCOLS), lambda i: (0, P_F // FCOLS)), pl.BlockSpec((1, FCOLS), lambda i: (0, 0))],
        out_specs=pl.BlockSpec((S, FCOLS), lambda i: (0, 0)), compiler_params=_cp(("arbitrary",)), name=name)(proj, bf)


def _fgate_bwd(dF, proj, bf, name):
    S = proj.shape[0]

    def body(dF_ref, f_ref, b_ref, df_ref, db_ref):
        v = dF_ref[...]
        row = lax.broadcasted_iota(jnp.int32, (S, FCOLS), 0)
        k = 1
        while k < S:
            v = v + jnp.where(row < S - k, pltpu.roll(v, S - k, 0), 0.0)
            k *= 2
        z = f_ref[...] + b_ref[...]
        df = v * (1.0 / (1.0 + jnp.exp(z)))
        db_ref[...] = jnp.sum(df, axis=0, keepdims=True)
        df_ref[...] = jnp.concatenate([df, jnp.zeros_like(df)], axis=1).astype(BF16)

    return pl.pallas_call(
        body, out_shape=(jax.ShapeDtypeStruct((S, 2 * FCOLS), BF16), jax.ShapeDtypeStruct((1, FCOLS), F32)), grid=(1,),
        in_specs=[pl.BlockSpec((S, FCOLS), lambda i: (0, 0)), pl.BlockSpec((S, FCOLS), lambda i: (0, P_F // FCOLS)),
                  pl.BlockSpec((1, FCOLS), lambda i: (0, 0))],
        out_specs=(pl.BlockSpec((S, 2 * FCOLS), lambda i: (0, 0)), pl.BlockSpec((1, FCOLS), lambda i: (0, 0))),
        compiler_params=_cp(("arbitrary",)), name=name)(dF, proj, bf)


def _fox_scores(qe, kj, fq, fk, r0, c0, tq, tk, diagonal):
    s = lax.dot_general(qe, kj, (((1,), (1,)), ((), ())), preferred_element_type=F32) * FOX_SCALE
    s = s + (fq - fk)
    if not diagonal:
        return s
    rows = r0 + lax.broadcasted_iota(jnp.int32, (tq, tk), 0)
    cols = c0 + lax.broadcasted_iota(jnp.int32, (tq, tk), 1)
    return jnp.where(rows >= cols, s, NEG)


FOX_TQ, FOX_TK = 512, 512


def _fox_fwd(qkv, fcol, frow, name):
    S = qkv.shape[0]
    tq, tk = FOX_TQ, min(FOX_TK, S)

    def body(q_ref, k_ref, v_ref, fc_ref, fr_ref, o_ref, o32_ref, lse_ref):
        i = pl.program_id(1)
        r0 = i * tq
        q = q_ref[...]
        half = lax.broadcasted_iota(jnp.int32, (tq, 128), 1) // 64
        qs = [jnp.where(half == e, q, jnp.zeros_like(q)) for e in (0, 1)]
        fqs = [fc_ref[0, :, e:e + 1] for e in (0, 1)]

        def step(j, carry, diagonal=False):
            c0 = pl.multiple_of(j * tk, tk)
            kj = k_ref[pl.ds(c0, tk), :]
            vj = v_ref[pl.ds(c0, tk), :]
            out = []
            for e in (0, 1):
                m, l, acc = carry[e]
                s = _fox_scores(qs[e], kj, fqs[e], fr_ref[0, e:e + 1, pl.ds(c0, tk)], r0, c0, tq, tk, diagonal)
                m_new = jnp.maximum(m, jnp.max(s, axis=-1, keepdims=True))
                alpha = jnp.exp(m - m_new)
                p = jnp.exp(s - m_new)
                out.append((m_new, alpha * l + jnp.sum(p, axis=-1, keepdims=True),
                            alpha * acc + jnp.dot(p.astype(BF16), vj, preferred_element_type=F32)))
            return tuple(out)

        init = (jnp.full((tq, 1), NEG, F32), jnp.zeros((tq, 1), F32), jnp.zeros((tq, 128), F32))
        below = r0 // tk
        carry = lax.fori_loop(0, below, step, (init, init))
        carry = step(below, carry, diagonal=True)
        outs = []
        for e in (0, 1):
            m, l, acc = carry[e]
            outs.append(acc / l)
            lse_ref[0, :, e:e + 1] = m + jnp.log(l)
        o = jnp.where(half == 0, outs[0], outs[1])
        o32_ref[...] = o
        o_ref[...] = o.astype(BF16)

    tile = pl.BlockSpec((tq, 128), lambda h, i: (i, h))
    return pl.pallas_call(
        body, out_shape=(jax.ShapeDtypeStruct((S, FOX_W), BF16), jax.ShapeDtypeStruct((S, FOX_W), F32),
                         jax.ShapeDtypeStruct((4, S, 2), F32)), grid=(4, S // tq),
        in_specs=[tile, pl.BlockSpec((S, 128), lambda h, i: (0, 4 + h)), pl.BlockSpec((S, 128), lambda h, i: (0, 8 + h)),
                  pl.BlockSpec((1, tq, 2), lambda h, i: (h, i, 0)), pl.BlockSpec((1, 2, S), lambda h, i: (h, 0, 0))],
        out_specs=(tile, tile, pl.BlockSpec((1, tq, 2), lambda h, i: (h, i, 0))),
        compiler_params=_cp(("parallel", "parallel")), name=name)(qkv, qkv, qkv, fcol, frow)


def _fox_bwd(qkv, o32, do, lse, fcol, frow, name):
    S = qkv.shape[0]
    tq, tk = FOX_TQ, min(FOX_TK, S)
    nq = S // tq

    def body(q_ref, k_ref, v_ref, o_ref, do_ref, lse_ref, fc_ref, fr_ref, dq_ref, dk_ref, dv_ref, dfr_ref, dfc_ref, dk_acc, dv_acc):
        dk_acc[...] = jnp.zeros_like(dk_acc)
        dv_acc[...] = jnp.zeros_like(dv_acc)
        dfr_ref[...] = jnp.zeros_like(dfr_ref)
        half = lax.broadcasted_iota(jnp.int32, (tq, 128), 1) // 64

        def q_block(i, _):
            r0 = pl.multiple_of(i * tq, tq)
            qi = q_ref[pl.ds(r0, tq), :]
            dob = do_ref[pl.ds(r0, tq), :].astype(BF16)
            row_dot = dob.astype(F32) * o_ref[pl.ds(r0, tq), :]
            qs = [jnp.where(half == e, qi, jnp.zeros_like(qi)) for e in (0, 1)]
            dos = [jnp.where(half == e, dob, jnp.zeros_like(dob)) for e in (0, 1)]
            deltas = [jnp.sum(jnp.where(half == e, row_dot, 0.0), axis=-1, keepdims=True) for e in (0, 1)]
            lses = [lse_ref[0, pl.ds(r0, tq), e:e + 1] for e in (0, 1)]
            fqs = [fc_ref[0, pl.ds(r0, tq), e:e + 1] for e in (0, 1)]

            def step(j, carry, diagonal=False):
                dqs, row_sums = carry
                c0 = pl.multiple_of(j * tk, tk)
                kj = k_ref[pl.ds(c0, tk), :]
                vj = v_ref[pl.ds(c0, tk), :]
                new_dq, new_rows, dkc, dvc = [], [], [], []
                for e in (0, 1):
                    s = _fox_scores(qs[e], kj, fqs[e], fr_ref[0, e:e + 1, pl.ds(c0, tk)], r0, c0, tq, tk, diagonal)
                    p = jnp.exp(s - lses[e])
                    dp = lax.dot_general(dos[e], vj, (((1,), (1,)), ((), ())), preferred_element_type=F32)
                    ds = p * (dp - deltas[e])
                    dfr_ref[0, e:e + 1, pl.ds(c0, tk)] -= jnp.sum(ds, axis=0, keepdims=True)
                    new_rows.append(row_sums[e] + jnp.sum(ds, axis=-1, keepdims=True))
                    dsb = (ds * FOX_SCALE).astype(BF16)
                    dkc.append(lax.dot_general(dsb, qi, (((0,), (0,)), ((), ())), preferred_element_type=F32))
                    dvc.append(lax.dot_general(p.astype(BF16), dob, (((0,), (0,)), ((), ())), preferred_element_type=F32))
                    new_dq.append(dqs[e] + jnp.dot(dsb, kj, preferred_element_type=F32))
                half_k = lax.broadcasted_iota(jnp.int32, (tk, 128), 1) // 64
                dk_acc[pl.ds(c0, tk), :] += jnp.where(half_k == 0, dkc[0], dkc[1])
                dv_acc[pl.ds(c0, tk), :] += jnp.where(half_k == 0, dvc[0], dvc[1])
                return tuple(new_dq), tuple(new_rows)

            zero, zero_col = jnp.zeros((tq, 128), F32), jnp.zeros((tq, 1), F32)
            below = r0 // tk
            carry = lax.fori_loop(0, below, step, ((zero, zero), (zero_col, zero_col)))
            dqs, row_sums = step(below, carry, diagonal=True)
            for e in (0, 1):
                dfc_ref[0, pl.ds(r0, tq), e:e + 1] = row_sums[e]
            dq_ref[pl.ds(r0, tq), :] = jnp.where(half == 0, dqs[0], dqs[1]).astype(BF16)
            return 0

        lax.fori_loop(0, nq, q_block, 0)
        dk_ref[...] = dk_acc[...].astype(BF16)
        dv_ref[...] = dv_acc[...].astype(BF16)

    col = lambda off: pl.BlockSpec((S, 128), lambda h: (0, off + h))
    hs2 = pl.BlockSpec((1, S, 2), lambda h: (h, 0, 0))
    h2s = pl.BlockSpec((1, 2, S), lambda h: (h, 0, 0))
    return pl.pallas_call(
        body, out_shape=(jax.ShapeDtypeStruct((S, FOX_W), BF16),) * 3 + (jax.ShapeDtypeStruct((4, 2, S), F32),
                                                                         jax.ShapeDtypeStruct((4, S, 2), F32)), grid=(4,),
        in_specs=[col(0), col(4), col(8), col(0), col(0), hs2, hs2, h2s],
        out_specs=(col(0), col(0), col(0), h2s, hs2),
        scratch_shapes=[pltpu.VMEM((S, 128), F32), pltpu.VMEM((S, 128), F32)],
        compiler_params=_cp(("parallel",)), name=name)(qkv, qkv, qkv, o32, do, lse, fcol, frow)


def _gelu(x):
    return 0.5 * x * (1.0 + jnp.tanh(GELU_K * (x + GELU_C * x * x * x)))


def _gelu_grad(x):
    th = jnp.tanh(GELU_K * (x + GELU_C * x * x * x))
    return 0.5 * (1.0 + th) + 0.5 * x * (1.0 - th * th) * GELU_K * (1.0 + 3.0 * GELU_C * x * x)


def _sgu_parts(c, gn, w_ref, bias):
    zc = _gelu(c)
    u, vv = zc[:, :SGU_W], zc[:, SGU_W:]
    rstd = lax.rsqrt(jnp.mean(vv * vv, axis=-1, keepdims=True) + EPS)
    vhat = vv * rstd
    vnb = (vhat * gn).astype(BF16)
    grp = lax.broadcasted_iota(jnp.int32, (SGU_CHUNK, SGU_W), 1) // 64
    mixed = bias
    for gi in range(4):
        mixed = mixed + jnp.where(grp == gi, jnp.dot(w_ref[gi], vnb, preferred_element_type=F32), 0.0)
    return u, rstd, vhat, vnb, grp, mixed


def _sgu_fwd(proj, gn, wm, bias, name):
    S = proj.shape[0]

    def body(c_ref, g_ref, w_ref, b_ref, o_ref):
        u, _, _, _, _, mixed = _sgu_parts(c_ref[...], g_ref[...], w_ref, b_ref[...])
        o_ref[...] = (u * mixed).astype(BF16)

    return pl.pallas_call(
        body, out_shape=jax.ShapeDtypeStruct((S, SGU_W), BF16), grid=(S // SGU_CHUNK,),
        in_specs=[pl.BlockSpec((SGU_CHUNK, 2 * SGU_W), lambda i: (i, P_C // (2 * SGU_W))),
                  pl.BlockSpec((1, SGU_W), lambda i: (0, 0)), pl.BlockSpec((4, SGU_CHUNK, SGU_CHUNK), lambda i: (0, 0, 0)),
                  pl.BlockSpec((SGU_CHUNK, SGU_W), lambda i: (0, 0))],
        out_specs=pl.BlockSpec((SGU_CHUNK, SGU_W), lambda i: (i, 0)),
        compiler_params=_cp(("parallel",)), name=name)(proj, gn, wm, bias)


def _sgu_bwd(dsg, proj, gn, wm, wmt, bias, name):
    S = proj.shape[0]

    def body(dsg_ref, c_ref, g_ref, w_ref, wt_ref, b_ref, dc_ref, dw_ref, db_ref, dg_ref):
        i = pl.program_id(0)

        @pl.when(i == 0)
        def _():
            dw_ref[...] = jnp.zeros_like(dw_ref)
            db_ref[...] = jnp.zeros_like(db_ref)
            dg_ref[...] = jnp.zeros_like(dg_ref)

        c = c_ref[...]
        gn_v = g_ref[...]
        u, rstd, vhat, vnb, grp, mixed = _sgu_parts(c, gn_v, w_ref, b_ref[...])
        dsg_v = dsg_ref[...]
        du = dsg_v * mixed
        dmix = dsg_v * u
        db_ref[...] += dmix
        dmb = dmix.astype(BF16)
        dvn = jnp.zeros((SGU_CHUNK, SGU_W), F32)
        for gi in range(4):
            dmg = jnp.where(grp == gi, dmb, jnp.zeros_like(dmb))
            dw_ref[gi] += lax.dot_general(dmg, vnb, (((1,), (1,)), ((), ())), preferred_element_type=F32)
            dvn = dvn + jnp.where(grp == gi, jnp.dot(wt_ref[gi], dmb, preferred_element_type=F32), 0.0)
        dg_ref[...] += jnp.sum(dvn * vhat, axis=0, keepdims=True)
        t = dvn * gn_v
        dvv = rstd * (t - vhat * jnp.mean(t * vhat, axis=-1, keepdims=True))
        dc_ref[...] = (jnp.concatenate([du, dvv], axis=1) * _gelu_grad(c)).astype(BF16)

    w_spec = pl.BlockSpec((4, SGU_CHUNK, SGU_CHUNK), lambda i: (0, 0, 0))
    tile = pl.BlockSpec((SGU_CHUNK, SGU_W), lambda i: (0, 0))
    vec = pl.BlockSpec((1, SGU_W), lambda i: (0, 0))
    return pl.pallas_call(
        body, out_shape=(jax.ShapeDtypeStruct((S, 2 * SGU_W), BF16), jax.ShapeDtypeStruct((4, SGU_CHUNK, SGU_CHUNK), F32),
                         jax.ShapeDtypeStruct((SGU_CHUNK, SGU_W), F32), jax.ShapeDtypeStruct((1, SGU_W), F32)),
        grid=(S // SGU_CHUNK,),
        in_specs=[pl.BlockSpec((SGU_CHUNK, SGU_W), lambda i: (i, 0)),
                  pl.BlockSpec((SGU_CHUNK, 2 * SGU_W), lambda i: (i, P_C // (2 * SGU_W))), vec, w_spec, w_spec, tile],
        out_specs=(pl.BlockSpec((SGU_CHUNK, 2 * SGU_W), lambda i: (i, 0)), w_spec, tile, vec),
        compiler_params=_cp(("arbitrary",)), name=name)(dsg, proj, gn, wm, wmt, bias)


def _sigmoid(z):
    return 1.0 / (1.0 + jnp.exp(-z))


def _merge_specs(tm):
    row = lambda n: pl.BlockSpec((tm, n), lambda i: (i, 0))
    gate = lambda b: pl.BlockSpec((tm, D), lambda i: (i, b))
    full = lambda r, c: pl.BlockSpec((r, c), lambda i: (0, 0))
    packed = pl.BlockSpec((4, 256, PACK_COLS), lambda i: (0, R_BRANCH // 256, 0))
    return row, gate, full, packed


def _branch_shards(c_ref, j):
    return c_ref[j, :, 0:256], c_ref[j, :, 256:512], c_ref[j, :, 512:768], c_ref[j, :, 768:1024]


def _merge_fwd(proj, ya, o, sg, packed_w, bg, name, tm=512):
    S = proj.shape[0]
    row, gate, full, packed = _merge_specs(tm)

    def body(g0, g1, g2, ya_ref, o_ref, sg_ref, c_ref, bg_ref, out_ref):
        yav, ov, sgv = ya_ref[...], o_ref[...], sg_ref[...]
        for j in range(4):
            cols = slice(256 * j, 256 * (j + 1))
            wa, wb0, wb1, wc = _branch_shards(c_ref, j)
            y = (jnp.dot(yav, wa, preferred_element_type=F32),
                 jnp.dot(ov[:, :256], wb0, preferred_element_type=F32) + jnp.dot(ov[:, 256:], wb1, preferred_element_type=F32),
                 jnp.dot(sgv, wc, preferred_element_type=F32))
            acc = jnp.zeros((tm, 256), F32)
            for b, g_ref in enumerate((g0, g1, g2)):
                acc = acc + _sigmoid(g_ref[:, cols] + bg_ref[:, b * D + 256 * j:b * D + 256 * (j + 1)]) * y[b]
            out_ref[:, cols] = acc.astype(BF16)

    return pl.pallas_call(
        body, out_shape=jax.ShapeDtypeStruct((S, D), BF16), grid=(S // tm,),
        in_specs=[gate(0), gate(1), gate(2), row(POOL_W), row(FOX_W), row(SGU_W), packed, full(1, 3 * D)],
        out_specs=row(D), compiler_params=_cp(("parallel",)), name=name)(proj, proj, proj, ya, o, sg, packed_w, bg)


def _merge_bwd(dm, proj, ya, o, sg, packed_w, bg, grads, name, tm=512):
    S = proj.shape[0]
    row, gate, full, packed = _merge_specs(tm)
    tn_dims = (((0,), (0,)), ((), ()))
    nt_dims = (((1,), (1,)), ((), ()))

    def body(dm_ref, g0, g1, g2, ya_ref, o_ref, sg_ref, c_ref, bg_ref, _, dg_ref, dya_ref, do_ref, dsg_ref, dc_ref, dbg_ref, acc):
        i = pl.program_id(0)

        @pl.when(i == 0)
        def _():
            acc[...] = jnp.zeros_like(acc)
            dbg_ref[...] = jnp.zeros_like(dbg_ref)

        yav, ov, sgv = ya_ref[...], o_ref[...], sg_ref[...]
        o0, o1 = ov[:, :256], ov[:, 256:]
        dya = jnp.zeros((tm, POOL_W), F32)
        do0 = jnp.zeros((tm, 256), F32)
        do1 = jnp.zeros((tm, 256), F32)
        dsg = jnp.zeros((tm, SGU_W), F32)
        for j in range(4):
            cols = slice(256 * j, 256 * (j + 1))
            wa, wb0, wb1, wc = _branch_shards(c_ref, j)
            y = (jnp.dot(yav, wa, preferred_element_type=F32),
                 jnp.dot(o0, wb0, preferred_element_type=F32) + jnp.dot(o1, wb1, preferred_element_type=F32),
                 jnp.dot(sgv, wc, preferred_element_type=F32))
            dmv = dm_ref[:, cols]
            dy = []
            for b, g_ref in enumerate((g0, g1, g2)):
                bcols = slice(b * D + 256 * j, b * D + 256 * (j + 1))
                gt = _sigmoid(g_ref[:, cols] + bg_ref[:, bcols])
                dgp = dmv * y[b] * gt * (1.0 - gt)
                dg_ref[:, bcols] = dgp.astype(BF16)
                dbg_ref[:, bcols] += jnp.sum(dgp, axis=0, keepdims=True)
                dy.append((dmv * gt).astype(BF16))
            dya = dya + lax.dot_general(dy[0], wa, nt_dims, preferred_element_type=F32)
            do0 = do0 + lax.dot_general(dy[1], wb0, nt_dims, preferred_element_type=F32)
            do1 = do1 + lax.dot_general(dy[1], wb1, nt_dims, preferred_element_type=F32)
            dsg = dsg + lax.dot_general(dy[2], wc, nt_dims, preferred_element_type=F32)
            acc[j, :, 0:256] += lax.dot_general(yav, dy[0], tn_dims, preferred_element_type=F32)
            acc[j, :, 256:512] += lax.dot_general(o0, dy[1], tn_dims, preferred_element_type=F32)
            acc[j, :, 512:768] += lax.dot_general(o1, dy[1], tn_dims, preferred_element_type=F32)
            acc[j, :, 768:1024] += lax.dot_general(sgv, dy[2], tn_dims, preferred_element_type=F32)
        dya_ref[...] = dya
        do_ref[:, :256] = do0
        do_ref[:, 256:] = do1
        dsg_ref[...] = dsg

        @pl.when(i == pl.num_programs(0) - 1)
        def _():
            dc_ref[...] = acc[...].astype(dc_ref.dtype)

    return pl.pallas_call(
        body, out_shape=(jax.ShapeDtypeStruct((S, 3 * D), BF16), jax.ShapeDtypeStruct((S, POOL_W), F32),
                         jax.ShapeDtypeStruct((S, FOX_W), F32), jax.ShapeDtypeStruct((S, SGU_W), F32),
                         jax.ShapeDtypeStruct(grads.shape, grads.dtype), jax.ShapeDtypeStruct((1, 3 * D), F32)),
        grid=(S // tm,),
        in_specs=[row(D), gate(0), gate(1), gate(2), row(POOL_W), row(FOX_W), row(SGU_W), packed, full(1, 3 * D), ANY],
        out_specs=(row(3 * D), row(POOL_W), row(FOX_W), row(SGU_W), packed, full(1, 3 * D)),
        scratch_shapes=[pltpu.VMEM((4, 256, PACK_COLS), F32)], input_output_aliases={9: 4},
        compiler_params=_cp(("arbitrary",)), name=name)(dm, proj, proj, proj, ya, o, sg, packed_w, bg, grads)


def _xattn_probs(qh, kh):
    s = lax.dot_general(qh, kh, (((1,), (1,)), ((), ())), preferred_element_type=F32) * X_SCALE
    p = jnp.exp(s - jnp.max(s, axis=-1, keepdims=True))
    return p / jnp.sum(p, axis=-1, keepdims=True)


def _xattn_fwd(xq, kv, name, tq=512):
    S = xq.shape[0]
    M = kv.shape[0]

    def body(q_ref, k_ref, v_ref, o_ref):
        for h in range(XH):
            sl = slice(h * XHD, (h + 1) * XHD)
            p = _xattn_probs(q_ref[:, sl], k_ref[:, sl])
            o_ref[:, sl] = jnp.dot(p.astype(BF16), v_ref[:, sl], preferred_element_type=F32).astype(BF16)

    return pl.pallas_call(
        body, out_shape=jax.ShapeDtypeStruct((S, D), BF16), grid=(S // tq,),
        in_specs=[pl.BlockSpec((tq, D), lambda i: (i, 0)), pl.BlockSpec((M, D), lambda i: (0, 0)),
                  pl.BlockSpec((M, D), lambda i: (0, 1))],
        out_specs=pl.BlockSpec((tq, D), lambda i: (i, 0)), compiler_params=_cp(("parallel",)), name=name)(xq, kv, kv)


def _xattn_bwd(xq, kv, do, name, tq=512):
    S = xq.shape[0]
    M = kv.shape[0]

    def body(q_ref, k_ref, v_ref, do_ref, dq_ref, dkv_ref, dk_acc, dv_acc):
        i = pl.program_id(0)

        @pl.when(i == 0)
        def _():
            dk_acc[...] = jnp.zeros_like(dk_acc)
            dv_acc[...] = jnp.zeros_like(dv_acc)

        for h in range(XH):
            sl = slice(h * XHD, (h + 1) * XHD)
            qh, kh, vh, doh = q_ref[:, sl], k_ref[:, sl], v_ref[:, sl], do_ref[:, sl]
            p = _xattn_probs(qh, kh)
            dp = lax.dot_general(doh, vh, (((1,), (1,)), ((), ())), preferred_element_type=F32)
            ds = p * (dp - jnp.sum(p * dp, axis=-1, keepdims=True))
            dsb = (ds * X_SCALE).astype(BF16)
            dq_ref[:, sl] = jnp.dot(dsb, kh, preferred_element_type=F32).astype(BF16)
            dk_acc[:, sl] += lax.dot_general(dsb, qh, (((0,), (0,)), ((), ())), preferred_element_type=F32)
            dv_acc[:, sl] += lax.dot_general(p.astype(BF16), doh, (((0,), (0,)), ((), ())), preferred_element_type=F32)

        @pl.when(i == pl.num_programs(0) - 1)
        def _():
            dkv_ref[:, :D] = dk_acc[...].astype(BF16)
            dkv_ref[:, D:] = dv_acc[...].astype(BF16)

    return pl.pallas_call(
        body, out_shape=(jax.ShapeDtypeStruct((S, D), BF16), jax.ShapeDtypeStruct((M, 2 * D), BF16)), grid=(S // tq,),
        in_specs=[pl.BlockSpec((tq, D), lambda i: (i, 0)), pl.BlockSpec((M, D), lambda i: (0, 0)),
                  pl.BlockSpec((M, D), lambda i: (0, 1)), pl.BlockSpec((tq, D), lambda i: (i, 0))],
        out_specs=(pl.BlockSpec((tq, D), lambda i: (i, 0)), pl.BlockSpec((M, 2 * D), lambda i: (0, 0))),
        scratch_shapes=[pltpu.VMEM((M, D), F32), pltpu.VMEM((M, D), F32)],
        compiler_params=_cp(("arbitrary",)), name=name)(xq, kv, kv, do)


def _adam_math(gv, wv, mv, vv):
    c1 = 1.0 - ADAM_B1 ** ADAM_STEP
    c2 = 1.0 - ADAM_B2 ** ADAM_STEP
    nm = ADAM_B1 * mv + (1.0 - ADAM_B1) * gv
    nv = ADAM_B2 * vv + (1.0 - ADAM_B2) * (gv * gv)
    return -ADAM_LR * ((nm / c1) / (jnp.sqrt(nv / c2) + ADAM_EPS) + ADAM_WD * wv), nm, nv


def _adamw(g, w, m, v, name, block=None):
    if block is None:
        block = (1, 256 if g.shape[1] % 256 == 0 else g.shape[1], g.shape[2])
    grid = tuple(s // b for s, b in zip(g.shape, block))

    def body(g_ref, w_ref, m_ref, v_ref, d_ref, nm_ref, nv_ref):
        d_ref[...], nm_ref[...], nv_ref[...] = _adam_math(g_ref[...], w_ref[...], m_ref[...], v_ref[...])

    blk = pl.BlockSpec(block, lambda a, b, c: (a, b, c))
    return pl.pallas_call(
        body, out_shape=(jax.ShapeDtypeStruct(g.shape, F32),) * 3, grid=grid,
        in_specs=[blk] * 4, out_specs=(blk,) * 3, compiler_params=_cp(("parallel",) * 3), name=name)(g, w, m, v)


def _adamw_packed(red, w, m, v, g_index, name, token, tr=256):
    L, r, c = w.shape
    tr = min(tr, r)

    def body(g0_ref, g1_ref, w_ref, m_ref, v_ref, _, g_ref, d_ref, nm_ref, nv_ref):
        gv = jnp.where(pl.program_id(0) == 0, g0_ref[...], g1_ref[...])
        g_ref[0] = gv
        d_ref[0], nm_ref[0], nv_ref[0] = _adam_math(gv, w_ref[0], m_ref[0], v_ref[0])

    last = r // tr - 1
    g0blk = pl.BlockSpec((tr, c), lambda l, i: g_index(jnp.where(l == 0, i, last)))
    g1blk = pl.BlockSpec((tr, c), lambda l, i: g_index(jnp.where(l == 0, 0, i)))
    blk = pl.BlockSpec((1, tr, c), lambda l, i: (l, i, 0))
    return pl.pallas_call(
        body, out_shape=(jax.ShapeDtypeStruct(w.shape, F32),) * 4, grid=(L, r // tr),
        in_specs=[g0blk, g1blk, blk, blk, blk, pl.BlockSpec((8, 128), lambda l, i: (0, 0))], out_specs=(blk,) * 4,
        compiler_params=_cp(("parallel", "parallel")), name=name)(red[0], red[1], w, m, v, token)


def _row_tile(R):
    return next((t for t in (512, 496, 384, 256) if R % t == 0), R)


def _sum_slots(a, out_dtype, name):
    n, R, C = a.shape
    tr = _row_tile(R)

    def body(a_ref, o_ref):
        acc = a_ref[0].astype(F32)
        for k in range(1, n):
            acc = acc + a_ref[k].astype(F32)
        o_ref[...] = acc.astype(out_dtype)

    return pl.pallas_call(
        body, out_shape=jax.ShapeDtypeStruct((R, C), out_dtype), grid=(R // tr,),
        in_specs=[pl.BlockSpec((n, tr, C), lambda i: (0, i, 0))], out_specs=pl.BlockSpec((tr, C), lambda i: (i, 0)),
        compiler_params=_cp(("parallel",)), name=name)(a)


LANDING = pl.BlockSpec(memory_space=pltpu.VMEM)


def _landing_params(shape, dtype):
    return pltpu.CompilerParams(vmem_limit_bytes=math.prod(shape) * jnp.dtype(dtype).itemsize + 4 * 1024 * 1024)


def _place():
    return lax.axis_index("x"), lax.axis_index("y"), lax.axis_index("c")


def _other_chips(x, y):
    return [(1 - x, y), (x, 1 - y), (1 - x, 1 - y)]


def _row_chunks(rows, want, align=16):
    n = want
    while n > 1 and rows % (n * align):
        n -= 1
    return n


def _pair_add(g, name, nch=5):
    n, R, C = g.shape
    half = R // 2
    nch = _row_chunks(half, nch)
    cr = half // nch
    rb = next(t for t in (512, 256, 128, 64, 32, 16) if half % t == 0)

    def body(g_ref, p_ref, got, send_sems, recv_sems, local_sem):
        x, y, c = _place()
        mine0 = pl.multiple_of(c * half, 16)
        theirs0 = (1 - c) * half
        keep = pltpu.make_async_copy(g_ref.at[:, pl.ds(mine0, half), :], p_ref, local_sem)
        keep.start()
        cps = []
        for s in range(n):
            for q in range(nch):
                src = g_ref.at[s, pl.ds(pl.multiple_of(theirs0 + q * cr, 16), cr), :]
                cps.append(pltpu.make_async_remote_copy(
                    src_ref=src, dst_ref=got.at[s, pl.ds(q * cr, cr), :], send_sem=send_sems.at[s * nch + q],
                    recv_sem=recv_sems.at[s * nch + q], device_id=(x, y, 1 - c), device_id_type=MESH))
        for cp in cps:
            cp.start()
        for cp in cps:
            cp.wait()
        keep.wait()

        def add(i, _):
            rows = pl.ds(pl.multiple_of(i * rb, rb), rb)
            for s in range(n):
                p_ref[s, rows, :] = (p_ref[s, rows, :].astype(F32) + got[s, rows, :].astype(F32)).astype(BF16)
            return 0

        lax.fori_loop(0, half // rb, add, 0)

    shape = (n, half, C)
    return pl.pallas_call(
        body, out_shape=jax.ShapeDtypeStruct(shape, g.dtype), in_specs=[ANY], out_specs=LANDING,
        scratch_shapes=[pltpu.VMEM(shape, g.dtype), pltpu.SemaphoreType.DMA((n * nch,)), pltpu.SemaphoreType.DMA((n * nch,)),
                        pltpu.SemaphoreType.DMA],
        compiler_params=_landing_params((2,) + shape, g.dtype), name=name)(g)


def _pair_gather(t, name, nch=10):
    R = t.shape[0]
    nch = _row_chunks(R, nch, 8)
    cr = R // nch

    def body(t_ref, o_ref, send_sems, recv_sems, local_sem):
        x, y, c = _place()
        own = pltpu.make_async_copy(t_ref, o_ref.at[c], local_sem)
        own.start()
        cps = [pltpu.make_async_remote_copy(src_ref=t_ref.at[pl.ds(q * cr, cr), :], dst_ref=o_ref.at[c, pl.ds(q * cr, cr), :],
                                            send_sem=send_sems.at[q], recv_sem=recv_sems.at[q], device_id=(x, y, 1 - c),
                                            device_id_type=MESH) for q in range(nch)]
        for cp in cps:
            cp.start()
        for cp in cps:
            cp.wait()
        own.wait()

    return pl.pallas_call(
        body, out_shape=jax.ShapeDtypeStruct((2,) + t.shape, t.dtype), in_specs=[ANY], out_specs=LANDING,
        scratch_shapes=[pltpu.SemaphoreType.DMA((nch,)), pltpu.SemaphoreType.DMA((nch,)), pltpu.SemaphoreType.DMA],
        compiler_params=_landing_params((2,) + t.shape, t.dtype), name=name)(t)


HBM = pl.BlockSpec(memory_space=pltpu.HBM)
SEM = pl.BlockSpec(memory_space=pltpu.SEMAPHORE)
SPLIT_COPY = pltpu.CompilerParams(has_side_effects=pltpu.SideEffectType.DATAFLOW_SIDE_EFFECTING)


def _split_exchange(src, rows, src_of, tag, nch=5):
    C = src.shape[-1]
    nch = _row_chunks(rows, nch)
    cr = rows // nch
    n = 3 * nch
    land_shape = (4, rows, C)

    def copies(src_ref, land_ref, send_sems, recv_sems):
        x, y, c = _place()
        j = 2 * x + y
        out = []
        for q in range(nch):
            for k, (px, py) in enumerate(_other_chips(x, y)):
                out.append(pltpu.make_async_remote_copy(
                    src_ref=src_of(src_ref, px, py, c, q * cr, cr), dst_ref=land_ref.at[j, pl.ds(q * cr, cr), :],
                    send_sem=send_sems.at[k * nch + q], recv_sem=recv_sems.at[k * nch + q], device_id=(px, py, c),
                    device_id_type=MESH))
        return out

    def start(src_ref, land_ref, send_sems, recv_sems, src_thru, land_thru, token):
        for cp in copies(src_ref, land_ref, send_sems, recv_sems):
            cp.start()
        token[...] = jnp.zeros_like(token)

    send_sems, recv_sems, src_thru, land_thru, token = pl.pallas_call(
        start, name=f"{tag}_start",
        out_shape=(pltpu.SemaphoreType.DMA((n,)), pltpu.SemaphoreType.DMA((n,)), pltpu.HBM(src.shape, src.dtype),
                   pltpu.HBM(land_shape, src.dtype), jax.ShapeDtypeStruct((8, 128), F32)),
        in_specs=(HBM, HBM), out_specs=(SEM, SEM, HBM, HBM, pl.BlockSpec(memory_space=pltpu.VMEM)),
        input_output_aliases={0: 2, 1: 3}, compiler_params=SPLIT_COPY)(
            pltpu.with_memory_space_constraint(src, pltpu.HBM),
            pltpu.with_memory_space_constraint(lax.empty(land_shape, src.dtype), pltpu.HBM))

    def finish(after):
        def wait(src_ref, land_ref, send_sems, recv_sems, after_ref, src_dead, got_ref):
            for cp in copies(src_ref, land_ref, send_sems, recv_sems):
                cp.wait_send()
                cp.wait_recv()

        return pl.pallas_call(
            wait, name=f"{tag}_wait", out_shape=(pltpu.HBM(src.shape, src.dtype), pltpu.HBM(land_shape, src.dtype)),
            in_specs=(HBM, HBM, SEM, SEM, ANY), out_specs=(HBM, HBM), input_output_aliases={0: 0, 1: 1},
            compiler_params=SPLIT_COPY)(src_thru, land_thru, send_sems, recv_sems, after)

    return token, finish


def _gather_finish(shard, land, name, nch=5):
    R, C = shard.shape
    half = R // 2
    nch = _row_chunks(half, nch)
    cr = half // nch

    def body(s_ref, l_ref, o_ref, send_sems, recv_sems, local_sems):
        x, y, c = _place()
        j = 2 * x + y
        mine0 = c * half
        local = [pltpu.make_async_copy(s_ref, o_ref.at[j], local_sems.at[0])]
        remote = []
        for k, (px, py) in enumerate(_other_chips(x, y)):
            jj = 2 * px + py
            local.append(pltpu.make_async_copy(l_ref.at[jj], o_ref.at[jj, pl.ds(pl.multiple_of(mine0, 16), half), :],
                                               local_sems.at[1 + k]))
            for q in range(nch):
                remote.append(pltpu.make_async_remote_copy(
                    src_ref=l_ref.at[jj, pl.ds(q * cr, cr), :],
                    dst_ref=o_ref.at[jj, pl.ds(pl.multiple_of(mine0 + q * cr, 16), cr), :], send_sem=send_sems.at[k * nch + q],
                    recv_sem=recv_sems.at[k * nch + q], device_id=(x, y, 1 - c), device_id_type=MESH))
        for cp in local + remote:
            cp.start()
        for cp in remote + local:
            cp.wait()

    return pl.pallas_call(
        body, out_shape=jax.ShapeDtypeStruct((4, R, C), shard.dtype), in_specs=[ANY, ANY], out_specs=LANDING,
        scratch_shapes=[pltpu.SemaphoreType.DMA((3 * nch,)), pltpu.SemaphoreType.DMA((3 * nch,)), pltpu.SemaphoreType.DMA((4,))],
        compiler_params=_landing_params((4, R, C), shard.dtype), name=name)(shard, land)


def _sum_slots_own(land, own, name):
    n, R, C = land.shape
    tr = _row_tile(R)
    me = (2 * lax.axis_index("x") + lax.axis_index("y")).astype(jnp.int32).reshape(1)
    if own.ndim == 3:
        own_spec = pl.BlockSpec((None, tr, C), lambda i, me: (me[0], i, 0))
    else:
        own_spec = pl.BlockSpec((tr, C), lambda i, me: (i, 0))

    def body(me_ref, land_ref, own_ref, o_ref):
        acc = None
        for k in range(n):
            v = jnp.where(me_ref[0] == k, own_ref[...], land_ref[k]).astype(F32)
            acc = v if acc is None else acc + v
        o_ref[...] = acc

    return pl.pallas_call(
        body, out_shape=jax.ShapeDtypeStruct((R, C), F32),
        grid_spec=pltpu.PrefetchScalarGridSpec(
            num_scalar_prefetch=1, grid=(R // tr,),
            in_specs=[pl.BlockSpec((n, tr, C), lambda i, me: (0, i, 0)), own_spec],
            out_specs=pl.BlockSpec((tr, C), lambda i, me: (i, 0))),
        compiler_params=_cp(("parallel",)), name=name)(me, land, own)


def _reduce_begin(g, tag):
    p = _pair_add(g, f"rs_pair_{tag}")
    token, finish = _split_exchange(p, p.shape[1], lambda ref, px, py, c, r0, cr: ref.at[2 * px + py, pl.ds(r0, cr), :],
                                    f"rs_a2a_{tag}")
    return (finish, g.shape, tag), token


def _reduce_end(state, after):
    finish, shape, tag = state
    p, land = finish(after)
    t = _sum_slots_own(land, p, f"rs_sum_{tag}")
    return _pair_gather(t, f"rs_join_{tag}").reshape(shape[1], shape[2])


def _all_reduce_begin(v, tag):
    p = _sum_slots(_pair_gather(v, f"ar_pair_{tag}"), F32, f"ar_add_{tag}")
    token, finish = _split_exchange(p, p.shape[0], lambda ref, px, py, c, r0, cr: ref.at[pl.ds(r0, cr), :], f"ar_a2a_{tag}")
    return (finish, tag), token


def _all_reduce_end(state, after):
    finish, tag = state
    p, land = finish(after)
    return _sum_slots_own(land, p, f"ar_sum_{tag}")


def _gather_begin(shard, tag):
    half = shard.shape[0] // 2
    token, finish = _split_exchange(
        shard, half, lambda ref, px, py, c, r0, cr: ref.at[pl.ds(pl.multiple_of(c * half + r0, 16), cr), :], f"gather_{tag}")
    return (finish, tag), token


def _gather_end(state, after):
    finish, tag = state
    shard, land = finish(after)
    return _gather_finish(shard, land, f"gather_{tag}_finish")


R_BRANCH, R_OUT, R_WIN, ROWS_A = 0, 256, 512, 1888
R_FF1, R_FF2, R_XKV, R_XQ, R_XO, ROWS_B = 0, 1024, 2048, 2560, 2816, 3072
WIN_ROWS = N_IN // 4


def _w_in_t(a):
    return jnp.transpose(a, (2, 0, 1))


def _pack_shard(w, l):
    xkv, wb = w['w_xkv'][l], w['w_branch_b'][l]
    a = [jnp.concatenate([w['w_branch_a'][l], wb[:256], wb[256:], w['w_branch_c'][l]], axis=1), w['w_out'][l],
         jnp.pad(_w_in_t(w['w_in'])[:, l, :], ((0, ROWS_A - R_WIN - WIN_ROWS), (0, 0)))]
    b = [w['w_ff1'][l], w['w_ff2'][l], jnp.concatenate([xkv[:512], xkv[512:]], axis=1), w['w_xq'][l], w['w_xo'][l]]
    return jnp.concatenate(a, axis=0).astype(BF16), jnp.concatenate(b, axis=0).astype(BF16)


def _w_in_rows(gathered):
    t = gathered[:, R_WIN:R_WIN + WIN_ROWS, :].reshape(N_IN, PACK_COLS)
    return jnp.concatenate([t[2312:5384], t[256:1792], t[1800:2312], t[0:256],
                            jnp.pad(t[1792:1800], ((0, NP - P_F - 8), (0, 0)))], axis=0)


def _w_in_grad_rows(grads, dwt):
    t = jnp.concatenate([dwt[P_A:P_A + 256], dwt[P_Q:P_Q + 1536], dwt[P_F:P_F + 8], dwt[P_C:P_C + 512], dwt[P_G:P_G + 3072]],
                        axis=0)
    for j in range(4):
        rows = t[j * WIN_ROWS:(j + 1) * WIN_ROWS][None].astype(grads.dtype)
        grads = lax.dynamic_update_slice(grads, rows, (j, R_WIN, 0))
    return grads


def _small_prep(sw, l):
    eye = jnp.eye(4, dtype=F32)
    bd = jnp.einsum('gh,gcd->gchd', eye, sw['pool_w'][l]).reshape(POOL_W, POOL_W).astype(BF16)
    tril = jnp.tril(jnp.ones((SGU_CHUNK, SGU_CHUNK), F32))
    wm = (sw['sgu_w'][l] * tril[None]).astype(BF16)
    return dict(
        g_mix=sw['norm_mix_g'][l][None], g_x=sw['norm_xattn_g'][l][None], g_mem=sw['norm_mem_g'][l][None],
        g_ffn=sw['norm_ffn_g'][l][None], bd=bd, pool_scale=sw['pool_scale'][l][None],
        bf=jnp.pad(sw['b_forget'][l], (0, FCOLS - 8))[None], sgu_g=sw['sgu_norm_g'][l][None], wm=wm,
        wmt=jnp.transpose(wm, (0, 2, 1)), sgu_bias=jnp.repeat(sw['sgu_b'][l].T, 64, axis=1), bg=sw['b_gate'][l][None])


def _rows4(r0):
    return dict(n=D, k=D, tn=D, b_block=(4, 256, PACK_COLS), b_index=lambda i, j, k: (0, r0 // 256, 0))


def _rows_t(r0):
    return dict(tb=True, n=D, k=D, tn=D, b_block=(4, 256, PACK_COLS), b_index=lambda i, j, k: (0, r0 // 256, 0))


def _rows_grad(r0):
    return dict(ta=True, tm=D, tn=512, o_block=(4, 256, 512), o_index=lambda i, j, k: (0, r0 // 256, j))


def _add_to(r, e):
    return e + r


def _after(v, token):
    return v if token is None else v + token[0, 0]


def _layer_fwd(x, mem, GA, w_in_t, sp, l, token, second):
    t = f"l{l}"
    S = x.shape[0]
    h = _rms_fwd(x, _after(sp['g_mix'], token), f"rms_mix_{t}")
    proj = _mm(h, w_in_t, name=f"proj_{t}", out_dtype=F32, tb=True)
    d, ya = _pool_fwd(proj, sp['bd'], sp['pool_scale'], f"pool_fwd_{t}")
    fcum = _fgate_fwd(proj, sp['bf'], f"fgate_fwd_{t}")
    f8 = fcum[:, :8]
    fcol = f8.reshape(S, 4, 2).transpose(1, 0, 2)
    frow = f8.T.reshape(4, 2, S)
    qkv = proj[:, P_Q:P_Q + 3 * FOX_W].astype(BF16)
    o, o32, lse = _fox_fwd(qkv, fcol, frow, f"fox_fwd_{t}")
    sg = _sgu_fwd(proj, sp['sgu_g'], sp['wm'], sp['sgu_bias'], f"sgu_fwd_{t}")
    merged = _merge_fwd(proj, ya, o, sg, GA, sp['bg'], f"merge_fwd_{t}")
    x1 = _mm(merged, GA, name=f"out_{t}", out_dtype=F32, extra=x, epi=_add_to, **_rows4(R_OUT))
    GB, token = second(x1)
    hx = _rms_fwd(x1, _after(sp['g_x'], token), f"rms_x_{t}")
    hm = _rms_fwd(mem, sp['g_mem'], f"rms_mem_{t}")
    xq = _mm(hx, GB, name=f"xq_{t}", out_dtype=BF16, **_rows4(R_XQ))
    kv = _mm(hm, GB, name=f"xkv_{t}", out_dtype=BF16, n=2 * D, k=D, tn=512, tk=512, b_block=(None, 512, 512),
             b_index=lambda i, j, k: (j, R_XKV // 512, k))
    o2 = _xattn_fwd(xq, kv, f"xattn_fwd_{t}")
    x2 = _mm(o2, GB, name=f"xo_{t}", out_dtype=F32, extra=x1, epi=_add_to, **_rows4(R_XO))
    hf = _rms_fwd(x2, sp['g_ffn'], f"rms_ffn_{t}")
    z = _mm(hf, GB, name=f"ff1_{t}", out_dtype=BF16, n=D_FF, k=D, tn=D, b_block=(None, 1024, PACK_COLS),
            b_index=lambda i, j, k: (j, R_FF1 // 1024, 0))
    x3 = _mm(z, GB, name=f"ff2_{t}", out_dtype=F32, a_fn=_relu2, extra=x2, epi=_add_to, n=D, k=D_FF, tk=1024, tn=D,
             b_block=(None, 1024, PACK_COLS), b_index=lambda i, j, k: (k, R_FF2 // 1024, 0))
    saved = dict(x=x, h=h, proj=proj, d=d, ya=ya, fcol=fcol, frow=frow, qkv=qkv, o=o, o32=o32, lse=lse, sg=sg, merged=merged,
                 x1=x1, hx=hx, hm=hm, xq=xq, kv=kv, o2=o2, x2=x2, hf=hf, z=z, GA=GA, GB=GB, w_in_t=w_in_t)
    return x3, saved


def _layer_bwd(dx3, mem, sp, sv, l, token, early):
    t = f"l{l}"
    S = dx3.shape[0]
    GA, GB = sv['GA'], sv['GB']
    gs = {}
    dx3 = _after(dx3, token)
    gb = lax.empty((4, ROWS_B, PACK_COLS), BF16)
    dz = _mm(dx3, GB, name=f"d_a2_{t}", out_dtype=BF16, tb=True, n=D_FF, k=D, tn=D, b_block=(None, 1024, PACK_COLS),
             b_index=lambda i, j, k: (j, R_FF2 // 1024, 0), extra=sv['z'],
             epi=lambda r, e: r * (2.0 * jnp.maximum(e.astype(F32), 0.0)))
    gb = _mm(sv['z'], dx3, name=f"dw_ff2_{t}", out_dtype=BF16, ta=True, a_fn=_relu2, into=gb, tm=1024, tn=D,
             o_block=(None, 1024, PACK_COLS), o_index=lambda i, j, k: (i, R_FF2 // 1024, 0))
    gb = _mm(sv['hf'], dz, name=f"dw_ff1_{t}", out_dtype=BF16, ta=True, into=gb, tm=1024, tn=D,
             o_block=(None, 1024, PACK_COLS), o_index=lambda i, j, k: (j, R_FF1 // 1024, 0))
    dx2, gs['norm_ffn_g'] = _mm(dz, GB, name=f"d_hf_{t}", out_dtype=F32, tb=True, n=D, k=D_FF, tm=1024, tn=D, tk=1024,
                                b_block=(None, 1024, PACK_COLS), b_index=lambda i, j, k: (k, R_FF1 // 1024, 0),
                                norm_bwd=(sv['x2'], sp['g_ffn'], dx3), vmem_limit=WIDE_VMEM_LIMIT)
    do2 = _mm(dx2, GB, name=f"d_o2_{t}", out_dtype=BF16, **_rows_t(R_XO))
    gb = _mm(sv['o2'], dx2, name=f"dw_xo_{t}", out_dtype=BF16, into=gb, **_rows_grad(R_XO))
    dxq, dkv = _xattn_bwd(sv['xq'], sv['kv'], do2, f"xattn_bwd_{t}")
    gb = _mm(sv['hm'], dkv, name=f"dw_xkv_{t}", out_dtype=BF16, ta=True, into=gb, tm=512, tn=512,
             o_block=(None, 512, 512), o_index=lambda i, j, k: (j, R_XKV // 512, i))
    dhm = _mm(dkv, GB, name=f"d_hm_{t}", out_dtype=F32, tb=True, n=D, k=2 * D, tn=512, tk=512, b_block=(None, 512, 512),
              b_index=lambda i, j, k: (k, R_XKV // 512, j))
    gs['norm_mem_g'] = _rms_bwd(dhm, mem, sp['g_mem'], None, f"rms_mem_bwd_{t}")
    gb = _mm(sv['hx'], dxq, name=f"dw_xq_{t}", out_dtype=BF16, into=gb, **_rows_grad(R_XQ))
    token = early(gb)
    dx1, gs['norm_xattn_g'] = _mm(dxq, GB, name=f"d_hx_{t}", out_dtype=F32, tm=512, **_rows_t(R_XQ),
                                  norm_bwd=(sv['x1'], _after(sp['g_x'], token), dx2))
    ga = jnp.zeros((4, ROWS_A, PACK_COLS), BF16)
    ga = _mm(sv['merged'], dx1, name=f"dw_out_{t}", out_dtype=BF16, into=ga, **_rows_grad(R_OUT))
    dm = _mm(dx1, GA, name=f"d_merged_{t}", out_dtype=F32, **_rows_t(R_OUT))
    dg, dya, do, dsg, ga, gs['b_gate'] = _merge_bwd(dm, sv['proj'], sv['ya'], sv['o'], sv['sg'], GA, sp['bg'], ga, f"merge_bwd_{t}")
    dc, dws, dbias, gs['sgu_norm_g'] = _sgu_bwd(dsg, sv['proj'], sp['sgu_g'], sp['wm'], sp['wmt'], sp['sgu_bias'], f"sgu_bwd_{t}")
    tril = jnp.tril(jnp.ones((SGU_CHUNK, SGU_CHUNK), F32))
    gs['sgu_w'] = dws * tril[None]
    gs['sgu_b'] = dbias.reshape(SGU_CHUNK, 4, 64).sum(-1).T
    dq, dk, dv, dfrow, dfcol = _fox_bwd(sv['qkv'], sv['o32'], do, sv['lse'], sv['fcol'], sv['frow'], f"fox_bwd_{t}")
    dF = jnp.pad(dfrow.reshape(8, S).T + dfcol.transpose(1, 0, 2).reshape(S, 8), ((0, 0), (0, FCOLS - 8)))
    df, dbf = _fgate_bwd(dF, sv['proj'], sp['bf'], f"fgate_bwd_{t}")
    gs['b_forget'] = dbf[:, :8]
    da, dbd, gs['pool_scale'] = _pool_bwd(dya, sv['d'], sp['bd'], sp['pool_scale'], f"pool_bwd_{t}")
    gs['pool_w'] = jnp.stack([dbd[g * 64:(g + 1) * 64, g * 64:(g + 1) * 64] for g in range(4)])
    dproj = jnp.concatenate([dg, dq, dk, dv, dc, da, df], axis=1)
    dwt = _mm(dproj, sv['h'], name=f"dw_in_{t}", out_dtype=BF16, ta=True, tm=512, tn=1024)
    ga = _w_in_grad_rows(ga, dwt)
    dx, gs['norm_mix_g'] = _mm(dproj, sv['w_in_t'], name=f"d_h_{t}", out_dtype=F32, tm=1024, tk=512, tn=D,
                               norm_bwd=(sv['x'], sp['g_mix'], dx1), vmem_limit=WIDE_VMEM_LIMIT)
    return dx, ga, gs


SMALL_ROWS = 1424
GRAD_BLOCKS = {
    'w_ff1': ('b', lambda i: (R_FF1 // 256 + i, 0)), 'w_ff2': ('b', lambda i: (R_FF2 // 256 + i, 0)),
    'w_xq': ('b', lambda i: (R_XQ // 256 + i, 0)), 'w_xo': ('b', lambda i: (R_XO // 256 + i, 0)),
    'w_xkv': ('b', lambda i: (R_XKV // 256 + i % 2, i // 2)), 'w_out': ('a', lambda i: (R_OUT // 256 + i, 0)),
    'w_branch_a': ('a', lambda i: (R_BRANCH // 256, 0)), 'w_branch_b': ('a', lambda i: (R_BRANCH // 256, 1 + i)),
    'w_branch_c': ('a', lambda i: (R_BRANCH // 256, 3)),
}


def _pack_small(parts):
    flat = jnp.concatenate([p.reshape(-1) for p in parts])
    return jnp.pad(flat, (0, SMALL_ROWS * 128 - flat.shape[0])).reshape(SMALL_ROWS, 128)


def _unpack_small(buf, shapes):
    flat, out, r = buf.reshape(-1), [], 0
    for s in shapes:
        n = math.prod(s)
        out.append(flat[r:r + n].reshape(s))
        r += n
    return out


def kernel(x, mem, norm_mix_g, w_in, b_forget, pool_w, pool_scale, sgu_norm_g, sgu_w, sgu_b, w_branch_a, w_branch_b, w_branch_c, b_gate, w_out, norm_xattn_g, norm_mem_g, w_xq, w_xkv, w_xo, norm_ffn_g, w_ff1, w_ff2, final_norm_g, loss_target, m_norm_mix_g, m_w_in, m_b_forget, m_pool_w, m_pool_scale, m_sgu_norm_g, m_sgu_w, m_sgu_b, m_w_branch_a, m_w_branch_b, m_w_branch_c, m_b_gate, m_w_out, m_norm_xattn_g, m_norm_mem_g, m_w_xq, m_w_xkv, m_w_xo, m_norm_ffn_g, m_w_ff1, m_w_ff2, m_final_norm_g, v_norm_mix_g, v_w_in, v_b_forget, v_pool_w, v_pool_scale, v_sgu_norm_g, v_sgu_w, v_sgu_b, v_w_branch_a, v_w_branch_b, v_w_branch_c, v_b_gate, v_w_out, v_norm_xattn_g, v_norm_mem_g, v_w_xq, v_w_xkv, v_w_xo, v_norm_ffn_g, v_w_ff1, v_w_ff2, v_final_norm_g):
    args = (norm_mix_g, w_in, b_forget, pool_w, pool_scale, sgu_norm_g, sgu_w, sgu_b, w_branch_a, w_branch_b, w_branch_c, b_gate,
            w_out, norm_xattn_g, norm_mem_g, w_xq, w_xkv, w_xo, norm_ffn_g, w_ff1, w_ff2, final_norm_g)
    margs = (m_norm_mix_g, m_w_in, m_b_forget, m_pool_w, m_pool_scale, m_sgu_norm_g, m_sgu_w, m_sgu_b, m_w_branch_a, m_w_branch_b,
             m_w_branch_c, m_b_gate, m_w_out, m_norm_xattn_g, m_norm_mem_g, m_w_xq, m_w_xkv, m_w_xo, m_norm_ffn_g, m_w_ff1, m_w_ff2,
             m_final_norm_g)
    vargs = (v_norm_mix_g, v_w_in, v_b_forget, v_pool_w, v_pool_scale, v_sgu_norm_g, v_sgu_w, v_sgu_b, v_w_branch_a, v_w_branch_b,
             v_w_branch_c, v_b_gate, v_w_out, v_norm_xattn_g, v_norm_mem_g, v_w_xq, v_w_xkv, v_w_xo, v_norm_ffn_g, v_w_ff1, v_w_ff2,
             v_final_norm_g)
    w = dict(zip(W_NAMES, args))
    mo = dict(zip(W_NAMES, margs))
    vo = dict(zip(W_NAMES, vargs))
    xs, mems, tgt = x[0], mem[0], loss_target[0]
    shards = [_pack_shard(w, l) for l in range(DEPTH)]
    preps = [_small_prep(w, l) for l in range(DEPTH)]

    first_a, _ = _gather_begin(shards[0][0], "a_l0")
    pending_b, token = _gather_begin(shards[0][1], "b_l0")
    GA = None
    act, saved = xs, []
    for l in range(DEPTH):
        nxt = {}
        if l + 1 < DEPTH:
            nxt['a'], ta = _gather_begin(shards[l + 1][0], f"a_l{l + 1}")
            token = ta if token is None else token + ta
        if l == 0:
            GA = _gather_end(first_a, shards[DEPTH - 1][1])

        def second(x1, l=l, pending_b=pending_b, nxt=nxt):
            GB = _gather_end(pending_b, x1)
            if l + 1 == DEPTH:
                return GB, None
            nxt['b'], tb = _gather_begin(shards[l + 1][1], f"b_l{l + 1}")
            return GB, tb

        act, sv = _layer_fwd(act, mems, GA, _w_in_rows(GA), preps[l], l, token, second)
        saved.append(sv)
        if l + 1 < DEPTH:
            GA = _gather_end(nxt['a'], act)
            pending_b, token = nxt['b'], None
    loss_part, dact, d_final_g = _loss_head(act, w['final_norm_g'][None], tgt, "loss_head")

    red_a, red_b, small_g = [None] * DEPTH, [None] * DEPTH, [None] * DEPTH
    token, state_a = None, None
    for l in reversed(range(DEPTH)):
        early = {}

        def start_b(gb, l=l, early=early):
            early['state'], tok = _reduce_begin(gb, f"b_l{l}")
            return tok

        dact, ga, small_g[l] = _layer_bwd(dact, mems, preps[l], saved[l], l, token, start_b)
        if state_a is not None:
            red_a[l + 1] = _reduce_end(state_a, dact)
        red_b[l] = _reduce_end(early['state'], dact)
        state_a, token = _reduce_begin(ga, f"a_l{l}")
    grad_x = dact[None]
    per_layer = [n for n in SMALL_NAMES if n != 'final_norm_g']
    small_shapes = [w[n].shape for n in per_layer] + [(D,), (1,)]
    parts = [jnp.stack([small_g[l][n].reshape(w[n].shape[1:]) for l in range(DEPTH)]) for n in per_layer]
    state_small, token_small = _all_reduce_begin(_pack_small(parts + [d_final_g.reshape(D), loss_part.reshape(1)]), "small")
    token = token + token_small

    grads, delta, new_m, new_v = {}, {}, {}, {}
    for n, (buf, g_index) in GRAD_BLOCKS.items():
        if buf == 'b':
            grads[n], delta[n], new_m[n], new_v[n] = _adamw_packed(red_b, w[n], mo[n], vo[n], g_index, f"adamw_{n}", token)
    red_a[0] = _reduce_end(state_a, new_v['w_xkv'])
    small_red = _unpack_small(_all_reduce_end(state_small, red_a[0]), small_shapes)
    grads.update(zip(per_layer + ['final_norm_g'], small_red[:-1]))
    loss = small_red[-1].reshape(())
    for n, (buf, g_index) in GRAD_BLOCKS.items():
        if buf == 'a':
            grads[n], delta[n], new_m[n], new_v[n] = _adamw_packed(red_a, w[n], mo[n], vo[n], g_index, f"adamw_{n}", token)
    g_t = jnp.stack([r[R_WIN:R_WIN + WIN_ROWS] for r in red_a], axis=1)
    upd = _adamw(g_t, _w_in_t(w['w_in']), _w_in_t(mo['w_in']), _w_in_t(vo['w_in']), "adamw_w_in", block=(WIN_ROWS, DEPTH, 128))
    grads['w_in'], delta['w_in'], new_m['w_in'], new_v['w_in'] = [jnp.transpose(a, (1, 2, 0)) for a in (g_t,) + tuple(upd)]
    small_all = per_layer + ['final_norm_g']
    shapes_all = [w[n].shape for n in small_all]
    packed = [_pack_small([d[n] for n in small_all])[None] for d in (grads, w, mo, vo)]
    ds, ms, vs = _adamw(*packed, "adamw_small")
    for n, a, b, c in zip(small_all, _unpack_small(ds[0], shapes_all), _unpack_small(ms[0], shapes_all), _unpack_small(vs[0], shapes_all)):
        delta[n], new_m[n], new_v[n] = a, b, c

    return (loss, grad_x, *[grads[n] for n in W_NAMES], *[delta[n] for n in W_NAMES], *[new_m[n] for n in W_NAMES],
            *[new_v[n] for n in W_NAMES])
```
